```python
import math
import jax, jax.numpy as jnp
from jax import lax
import numpy as np

D_MODEL = 2048
BATCH = 2
SEQ = 4096
DEPTH = 2

EPS = 1e-6
GLA_HEADS = 4
GLA_DK = D_MODEL // 16
GLA_DV = D_MODEL // 8
GLA_LOWRANK = 16
GLA_TAU = 16.0
GLA_CHUNK = 64
FOX_HEADS = 8
FOX_DH = D_MODEL // 16
FOX_W = FOX_HEADS * FOX_DH
FOX_BLOCK = 128
S5_WIDTH = D_MODEL
S5_GROUP_CH = 16
S5_GROUPS = S5_WIDTH // S5_GROUP_CH
S5_STATE = 64
N_EVEN = (DEPTH + 1) // 2
N_ODD = DEPTH // 2
AB_SIZES = (GLA_HEADS * GLA_DK, GLA_HEADS * GLA_DK, GLA_HEADS * GLA_DV, GLA_LOWRANK, GLA_HEADS * GLA_DV,
            FOX_W, FOX_W, FOX_W, FOX_HEADS, FOX_W)
AB_IN = sum(AB_SIZES)
MIX_W = GLA_HEADS * GLA_DV + FOX_W

kernel_name = "hybrid_gla_fox_s5_trunk"


def rms_norm(x, g):
    xf = x.astype(jnp.float32)
    y = xf * lax.rsqrt(jnp.mean(xf * xf, axis=-1, keepdims=True) + EPS)
    return (y * g.astype(jnp.float32)).astype(x.dtype)


def gla_chunked(q, k, v, log_a):
    Bsz, H, L, dk = q.shape
    dv = v.shape[-1]
    n = L // GLA_CHUNK

    def to_chunks(t):
        t = t.astype(jnp.float32).reshape(Bsz, H, n, GLA_CHUNK, t.shape[-1])
        return jnp.moveaxis(t, 2, 0)

    qc, kc, vc, gc = (to_chunks(q * (dk ** -0.5)), to_chunks(k), to_chunks(v), to_chunks(log_a))
    mask = jnp.tril(jnp.ones((GLA_CHUNK, GLA_CHUNK), dtype=bool))[:, :, None]

    def step(S, inp):
        qi, ki, vi, gi = inp
        b = jnp.cumsum(gi, axis=-2)
        o_inter = jnp.einsum('bhck,bhkv->bhcv', qi * jnp.exp(b), S)
        rel = b[..., :, None, :] - b[..., None, :, :]
        decay = jnp.exp(jnp.where(mask, rel, -jnp.inf))
        attn = jnp.einsum('bhik,bhjk,bhijk->bhij', qi, ki, decay)
        o_intra = jnp.einsum('bhij,bhjv->bhiv', attn, vi)
        b_last = b[..., -1:, :]
        S = jnp.exp(b_last[..., 0, :])[..., None] * S + jnp.einsum(
            'bhck,bhcv->bhkv', ki * jnp.exp(b_last - b), vi)
        return S, o_inter + o_intra

    S0 = jnp.zeros((Bsz, H, dk, dv), jnp.float32)
    _, out = lax.scan(step, S0, (qc, kc, vc, gc))
    return jnp.moveaxis(out, 0, 2).reshape(Bsz, H, L, dv)


def forgetting_attention(q, k, v, log_f):
    L, d = q.shape[2], q.shape[3]
    scale = d ** -0.5
    c = jnp.cumsum(log_f, axis=-1)
    outs = []
    for blk in range(L // FOX_BLOCK):
        s0, s1 = blk * FOX_BLOCK, (blk + 1) * FOX_BLOCK
        qb, kb, vb = q[:, :, s0:s1], k[:, :, :s1], v[:, :, :s1]
        logits = (jnp.einsum('bhqd,bhkd->bhqk', qb, kb).astype(jnp.float32) * scale
                  + c[:, :, s0:s1, None] - c[:, :, None, :s1])
        causal = jnp.arange(s0, s1)[:, None] >= jnp.arange(s1)[None, :]
        p = jax.nn.softmax(jnp.where(causal, logits, -jnp.inf), axis=-1)
        outs.append(jnp.einsum('bhqk,bhkd->bhqd', p.astype(v.dtype), vb))
    return jnp.concatenate(outs, axis=2)


def _complex_combine(e1, e2):
    a1r, a1i, b1r, b1i = e1
    a2r, a2i, b2r, b2i = e2
    ar = a2r * a1r - a2i * a1i
    ai = a2r * a1i + a2i * a1r
    br = a2r * b1r - a2i * b1i + b2r
    bi = a2r * b1i + a2i * b1r + b2i
    return ar, ai, br, bi


def s5_ssm(u, lam_re, lam_im, log_step, b_re, b_im, c_re, c_im, d):
    Bsz, L, E = u.shape
    f32 = jnp.float32
    uf = u.astype(f32).reshape(Bsz, L, S5_GROUPS, S5_GROUP_CH)
    lr = jnp.minimum(lam_re.astype(f32), -1e-4)
    li = lam_im.astype(f32)
    step = jnp.exp(log_step.astype(f32))[:, None]
    mag = jnp.exp(lr * step)
    lb_re, lb_im = mag * jnp.cos(li * step), mag * jnp.sin(li * step)
    den = lr * lr + li * li
    nr, ni = lb_re - 1.0, lb_im
    coef_re = (nr * lr + ni * li) / den
    coef_im = (ni * lr - nr * li) / den
    br_, bi_ = b_re.astype(f32), b_im.astype(f32)
    bb_re = coef_re[..., None] * br_ - coef_im[..., None] * bi_
    bb_im = coef_re[..., None] * bi_ + coef_im[..., None] * br_
    bu_re = jnp.einsum('gpc,blgc->blgp', bb_re, uf)
    bu_im = jnp.einsum('gpc,blgc->blgp', bb_im, uf)
    a_re = jnp.broadcast_to(lb_re, (1, L, S5_GROUPS, S5_STATE))
    a_im = jnp.broadcast_to(lb_im, (1, L, S5_GROUPS, S5_STATE))
    _, _, x_re, x_im = lax.associative_scan(_complex_combine, (a_re, a_im, bu_re, bu_im), axis=1)
    y = (jnp.einsum('gcp,blgp->blgc', c_re.astype(f32), x_re)
         - jnp.einsum('gcp,blgp->blgc', c_im.astype(f32), x_im))
    return y.reshape(Bsz, L, E) + d.astype(f32) * uf.reshape(Bsz, L, E)


def layer_ab(x, norm_g, w_in, alpha_up, alpha_b, head_g, f_b, w_out):
    Bsz, L, _ = x.shape
    h = rms_norm(x, norm_g)
    split_idx = np.cumsum(AB_SIZES)[:-1].tolist()
    g_q, g_k, g_v, g_lr, g_gate, f_q, f_k, f_v, f_f, f_gate = jnp.split(h @ w_in, split_idx, axis=-1)

    def heads(t, nh):
        return t.reshape(Bsz, L, nh, -1).transpose(0, 2, 1, 3)

    log_a = jax.nn.log_sigmoid((g_lr @ alpha_up + alpha_b).astype(jnp.float32)) / GLA_TAU
    o = gla_chunked(heads(g_q, GLA_HEADS), heads(g_k, GLA_HEADS), heads(g_v, GLA_HEADS),
                    heads(log_a, GLA_HEADS))
    o = o * lax.rsqrt(jnp.mean(o * o, axis=-1, keepdims=True) + EPS)
    o = o * head_g.astype(jnp.float32).reshape(GLA_HEADS, 1, GLA_DV)
    o_gla = o.transpose(0, 2, 1, 3).reshape(Bsz, L, -1).astype(x.dtype) * jax.nn.silu(g_gate)

    log_f = jax.nn.log_sigmoid((f_f + f_b).astype(jnp.float32)).transpose(0, 2, 1)
    o_fox = forgetting_attention(heads(f_q, FOX_HEADS), heads(f_k, FOX_HEADS), heads(f_v, FOX_HEADS), log_f)
    o_fox = o_fox.transpose(0, 2, 1, 3).reshape(Bsz, L, -1) * jax.nn.silu(f_gate)

    return x + jnp.concatenate([o_gla, o_fox], axis=-1) @ w_out


def layer_c(x, norm_g, w_in, lam_re, lam_im, log_step, b_re, b_im, c_re, c_im, d, glu_w, glu_b, w_out):
    h = rms_norm(x, norm_g)
    u, gate = jnp.split(h @ w_in, 2, axis=-1)
    y = s5_ssm(u, lam_re, lam_im, log_step, b_re, b_im, c_re, c_im, d).astype(x.dtype)
    z = jax.nn.gelu(y)
    z = z * jax.nn.sigmoid(z @ glu_w + glu_b)
    return x + (z * jax.nn.silu(gate)) @ w_out


def setup_inputs(seed: int = 0) -> dict:
    key = jax.random.key(seed)
    ks = jax.random.split(key, 24)
    nrm = jax.random.normal
    f32 = jnp.float32
    D, E, G, P, Cg = D_MODEL, S5_WIDTH, S5_GROUPS, S5_STATE, S5_GROUP_CH
    x = nrm(ks[0], (BATCH, SEQ, D), f32)
    ab_norm_g = 1.0 + 0.01 * nrm(ks[1], (N_EVEN, D), f32)
    ab_w_in = nrm(ks[2], (N_EVEN, D, AB_IN), f32) * D ** -0.5
    gla_alpha_up = nrm(ks[3], (N_EVEN, GLA_LOWRANK, GLA_HEADS * GLA_DK), f32) * GLA_LOWRANK ** -0.5
    gla_alpha_b = 0.01 * nrm(ks[4], (N_EVEN, GLA_HEADS * GLA_DK), f32)
    gla_head_g = 1.0 + 0.01 * nrm(ks[5], (N_EVEN, GLA_HEADS * GLA_DV), f32)
    fox_f_b = 1.0 + 0.1 * nrm(ks[6], (N_EVEN, FOX_HEADS), f32)
    ab_w_out = nrm(ks[7], (N_EVEN, MIX_W, D), f32) * MIX_W ** -0.5
    c_norm_g = 1.0 + 0.01 * nrm(ks[8], (N_ODD, D), f32)
    c_w_in = nrm(ks[9], (N_ODD, D, 2 * E), f32) * D ** -0.5
    n_idx = jnp.arange(P, dtype=f32)
    s5_lambda_re = -0.5 + 0.01 * nrm(ks[10], (N_ODD, G, P), f32)
    s5_lambda_im = math.pi * n_idx + 0.01 * nrm(ks[11], (N_ODD, G, P), f32)
    s5_log_step = jax.random.uniform(ks[12], (N_ODD, G), f32, math.log(1e-3), math.log(1e-1))
    s5_b_re = nrm(ks[13], (N_ODD, G, P, Cg), f32) * (2.0 * Cg) ** -0.5
    s5_b_im = nrm(ks[14], (N_ODD, G, P, Cg), f32) * (2.0 * Cg) ** -0.5
    s5_c_re = nrm(ks[15], (N_ODD, G, Cg, P), f32) * (2.0 * P) ** -0.5
    s5_c_im = nrm(ks[16], (N_ODD, G, Cg, P), f32) * (2.0 * P) ** -0.5
    s5_d = nrm(ks[17], (N_ODD, E), f32)
    glu_w = nrm(ks[18], (N_ODD, E, E), f32) * E ** -0.5
    glu_b = 0.01 * nrm(ks[19], (N_ODD, E), f32)
    c_w_out = nrm(ks[20], (N_ODD, E, D), f32) * E ** -0.5
    final_norm_g = 1.0 + 0.01 * nrm(ks[21], (D,), f32)
    return {"x": x, "ab_norm_g": ab_norm_g, "ab_w_in": ab_w_in, "gla_alpha_up": gla_alpha_up,
            "gla_alpha_b": gla_alpha_b, "gla_head_g": gla_head_g, "fox_f_b": fox_f_b,
            "ab_w_out": ab_w_out, "c_norm_g": c_norm_g, "c_w_in": c_w_in,
            "s5_lambda_re": s5_lambda_re, "s5_lambda_im": s5_lambda_im, "s5_log_step": s5_log_step,
            "s5_b_re": s5_b_re, "s5_b_im": s5_b_im, "s5_c_re": s5_c_re, "s5_c_im": s5_c_im,
            "s5_d": s5_d, "glu_w": glu_w, "glu_b": glu_b, "c_w_out": c_w_out,
            "final_norm_g": final_norm_g}


def reference(x, ab_norm_g, ab_w_in, gla_alpha_up, gla_alpha_b, gla_head_g, fox_f_b, ab_w_out,
              c_norm_g, c_w_in, s5_lambda_re, s5_lambda_im, s5_log_step, s5_b_re, s5_b_im,
              s5_c_re, s5_c_im, s5_d, glu_w, glu_b, c_w_out, final_norm_g):
    for layer in range(DEPTH):
        i = layer // 2
        if layer % 2 == 0:
            x = layer_ab(x, ab_norm_g[i], ab_w_in[i], gla_alpha_up[i], gla_alpha_b[i],
                         gla_head_g[i], fox_f_b[i], ab_w_out[i])
        else:
            x = layer_c(x, c_norm_g[i], c_w_in[i], s5_lambda_re[i], s5_lambda_im[i], s5_log_step[i],
                        s5_b_re[i], s5_b_im[i], s5_c_re[i], s5_c_im[i], s5_d[i], glu_w[i], glu_b[i],
                        c_w_out[i])
    return rms_norm(x, final_norm_g)
```

```python
import functools
import math

import jax
import jax.numpy as jnp
from jax import lax
from jax.experimental import pallas as pl
from jax.experimental.pallas import tpu as pltpu

EPS = 1e-6
F32 = jnp.float32
BF16 = jnp.bfloat16

GLA_HEADS = 4
GLA_DK = 128
GLA_DV = 256
GLA_LOWRANK = 16
GLA_TAU = 16.0
GLA_CHUNK = 64
GLA_SUB = 16
FOX_HEADS = 8
FOX_DH = 128
S5_GROUP_CH = 16
S5_STATE = 64
S5_T = 16

VMEM_LIMIT = 56 * 1024 * 1024

NT_DIMS = (((1,), (1,)), ((), ()))


def _cparams(sem):
    return pltpu.CompilerParams(dimension_semantics=sem, vmem_limit_bytes=VMEM_LIMIT)


def _dot(a, b):
    return jnp.dot(a, b, preferred_element_type=F32)


def _dot_nt(a, b):
    return lax.dot_general(a, b, NT_DIMS, preferred_element_type=F32)


def _log_sigmoid(z):
    return -(jnp.maximum(-z, 0.0) + jnp.log1p(jnp.exp(-jnp.abs(z))))


def _sigmoid(z):
    return 1.0 / (1.0 + jnp.exp(-z))


def _silu(z):
    return z * _sigmoid(z)


def _split3(x):
    hi = x.astype(BF16)
    r1 = x - hi.astype(F32)
    mid = r1.astype(BF16)
    lo = (r1 - mid.astype(F32)).astype(BF16)
    return hi, mid, lo


def _rmsnorm_body(x_ref, g_ref, o_ref):
    x = x_ref[...]
    ms = jnp.mean(x * x, axis=-1, keepdims=True)
    o_ref[...] = (x * lax.rsqrt(ms + EPS) * g_ref[...]).astype(o_ref.dtype)


def rmsnorm_bf16(x, g, tm=512):
    m, d = x.shape
    return pl.pallas_call(
        _rmsnorm_body,
        grid=(m // tm,),
        in_specs=[pl.BlockSpec((tm, d), lambda i: (i, 0)),
                  pl.BlockSpec((1, d), lambda i: (0, 0))],
        out_specs=pl.BlockSpec((tm, d), lambda i: (i, 0)),
        out_shape=jax.ShapeDtypeStruct((m, d), BF16),
        compiler_params=_cparams(("parallel",)),
        name="rmsnorm",
    )(x, g.reshape(1, d))


def _mm_body(a_ref, w_ref, o_ref):
    o_ref[...] = _dot(a_ref[...], w_ref[...]).astype(o_ref.dtype)


def matmul(a, w, out_dtype, tm=512, tn=512, name="matmul"):
    m, k = a.shape
    n = w.shape[1]
    tn = min(tn, n)
    return pl.pallas_call(
        _mm_body,
        grid=(m // tm, n // tn),
        in_specs=[pl.BlockSpec((tm, k), lambda i, j: (i, 0)),
                  pl.BlockSpec((k, tn), lambda i, j: (0, j))],
        out_specs=pl.BlockSpec((tm, tn), lambda i, j: (i, j)),
        out_shape=jax.ShapeDtypeStruct((m, n), out_dtype),
        compiler_params=_cparams(("parallel", "parallel")),
        name=name,
    )(a, w)


def _mm_nt_body(wt_ref, a_ref, o_ref):
    o_ref[...] = _dot_nt(wt_ref[...], a_ref[...]).astype(o_ref.dtype)


def matmul_t(wt, a, out_dtype, tm=512, tn=512, name="matmul_t"):
    n, k = wt.shape
    m = a.shape[0]
    tn = min(tn, n)
    return pl.pallas_call(
        _mm_nt_body,
        grid=(m // tm, n // tn),
        in_specs=[pl.BlockSpec((tn, k), lambda i, j: (j, 0)),
                  pl.BlockSpec((tm, k), lambda i, j: (i, 0))],
        out_specs=pl.BlockSpec((tn, tm), lambda i, j: (j, i)),
        out_shape=jax.ShapeDtypeStruct((n, m), out_dtype),
        compiler_params=_cparams(("parallel", "parallel")),
        name=name,
    )(wt, a)


def _gla_body(q_ref, k_ref, vt_ref, glr_ref, aup_ref, ab_ref, gate_ref, hg_ref, o_ref,
              st_ref, b_ref, kf_ref, vts_ref, *, tb):
    ib = pl.program_id(2)
    C, SB = GLA_CHUNK, GLA_SUB

    @pl.when(ib == 0)
    def _():
        st_ref[...] = jnp.zeros_like(st_ref)

    z = _dot(glr_ref[:, :GLA_LOWRANK].astype(BF16), aup_ref[...].astype(BF16)) + ab_ref[...]
    la = _log_sigmoid(z) * (1.0 / GLA_TAU)
    r = lax.broadcasted_iota(jnp.int32, (tb, tb), 0)
    c = lax.broadcasted_iota(jnp.int32, (tb, tb), 1)
    tri = jnp.where((r // C == c // C) & (c <= r), 1.0, 0.0).astype(BF16)
    hi, mid, lo = _split3(la)
    b_ref[...] = _dot(tri, hi) + _dot(tri, mid) + _dot(tri, lo)
    kf_ref[...] = k_ref[...].astype(F32)
    for ci in range(tb // C):
        vts_ref[ci] = vt_ref[:, ci * C:(ci + 1) * C]

    row = lax.broadcasted_iota(jnp.int32, (C, 1), 0)
    blk = row // SB
    rowi = lax.broadcasted_iota(jnp.int32, (C, C), 0) // SB
    coli = lax.broadcasted_iota(jnp.int32, (C, C), 1) // SB
    mask_b = (rowi % 2 == 1) & (coli == rowi - 1)
    lane_c = lax.broadcasted_iota(jnp.int32, (SB, C), 1)
    sub_r = lax.broadcasted_iota(jnp.int32, (SB, 1), 0)
    ones = jnp.ones((GLA_DK, C), BF16)
    neg = -jnp.inf
    scale = GLA_DK ** -0.5

    def chunk(ci, carry):
        r0 = pl.multiple_of(ci * C, C)
        q = q_ref[pl.ds(r0, C), :].astype(F32) * scale
        k = kf_ref[pl.ds(r0, C), :]
        b = b_ref[pl.ds(r0, C), :]
        vt = vts_ref[ci]
        b_last = b[C - 1:C, :]

        st = st_ref[...]
        o = _dot_nt((q * jnp.exp(b)).astype(BF16), st.astype(BF16))
        k_out = (k * jnp.exp(b_last - b)).astype(BF16)
        st_ref[...] = st * jnp.exp(b_last) + _dot(vt, k_out)

        r_a = b[2 * SB - 1:2 * SB, :]
        q_a = q * jnp.exp(jnp.where(blk >= 2, b - r_a, neg))
        k_a = k * jnp.exp(jnp.where(blk < 2, r_a - b, neg))
        r_b = jnp.where(blk < 2, b[SB - 1:SB, :], b[3 * SB - 1:3 * SB, :])
        q_b = q * jnp.exp(jnp.where(blk % 2 == 1, b - r_b, neg))
        k_b = k * jnp.exp(jnp.where(blk % 2 == 0, r_b - b, neg))
        attn = _dot_nt(q_a.astype(BF16), k_a.astype(BF16))
        attn = attn + jnp.where(mask_b, _dot_nt(q_b.astype(BF16), k_b.astype(BF16)), 0.0)

        zs = []
        for bi in range(C // SB):
            s0 = bi * SB
            q_i = q[s0:s0 + SB, :]
            b_i = b[s0:s0 + SB, :]
            for j in range(SB):
                k_j = kf_ref[pl.ds(r0 + (s0 + j), 1), :]
                b_j = b_ref[pl.ds(r0 + (s0 + j), 1), :]
                zs.append(q_i * k_j * jnp.exp(jnp.where(sub_r >= j, b_i - b_j, neg)))
        zsum = _dot(jnp.concatenate(zs, axis=0).astype(BF16), ones)
        diag = []
        for bi in range(C // SB):
            acc = jnp.zeros((SB, C), F32)
            for j in range(SB):
                n0 = (bi * SB + j) * SB
                acc = acc + jnp.where(lane_c == bi * SB + j, zsum[n0:n0 + SB, :], 0.0)
            diag.append(acc)
        attn = attn + jnp.concatenate(diag, axis=0)
        o = o + _dot_nt(attn.astype(BF16), vt)

        o = o * lax.rsqrt(jnp.mean(o * o, axis=-1, keepdims=True) + EPS) * hg_ref[...]
        g = gate_ref[pl.ds(r0, C), :].astype(F32)
        o_ref[pl.ds(r0, C), :] = (o * _silu(g)).astype(o_ref.dtype)
        return carry

    lax.fori_loop(0, tb // C, chunk, 0)


def gla(proj, vt, small, alpha_up, alpha_b, head_g, *, batch, seq, q_col, k_col, gate_col, tb=512):
    t = batch * seq
    nb = seq // tb
    dk, dv, h = GLA_DK, GLA_DV, GLA_HEADS
    qb, kb, gb = q_col // dk, k_col // dk, gate_col // dv
    tok = lambda b, hh, i: b * nb + i
    return pl.pallas_call(
        functools.partial(_gla_body, tb=tb),
        grid=(batch, h, nb),
        in_specs=[
            pl.BlockSpec((tb, dk), lambda b, hh, i: (tok(b, hh, i), qb + hh)),
            pl.BlockSpec((tb, dk), lambda b, hh, i: (tok(b, hh, i), kb + hh)),
            pl.BlockSpec((dv, tb), lambda b, hh, i: (hh, tok(b, hh, i))),
            pl.BlockSpec((tb, 128), lambda b, hh, i: (tok(b, hh, i), 0)),
            pl.BlockSpec((GLA_LOWRANK, dk), lambda b, hh, i: (0, hh)),
            pl.BlockSpec((1, dk), lambda b, hh, i: (0, hh)),
            pl.BlockSpec((tb, dv), lambda b, hh, i: (tok(b, hh, i), gb + hh)),
            pl.BlockSpec((1, dv), lambda b, hh, i: (0, hh)),
        ],
        out_specs=pl.BlockSpec((tb, dv), lambda b, hh, i: (tok(b, hh, i), hh)),
        out_shape=jax.ShapeDtypeStruct((t, h * dv), BF16),
        scratch_shapes=[pltpu.VMEM((dv, dk), F32), pltpu.VMEM((tb, dk), F32), pltpu.VMEM((tb, dk), F32),
                        pltpu.VMEM((tb // GLA_CHUNK, dv, GLA_CHUNK), BF16)],
        compiler_params=_cparams(("parallel", "parallel", "arbitrary")),
        name="gla",
    )(proj, proj, vt, small, alpha_up, alpha_b.reshape(1, -1), proj, head_g.reshape(1, -1))


def _fox_gate_body(s_ref, fb_ref, o_ref, *, blk):
    n = s_ref.shape[0] // blk
    r = lax.broadcasted_iota(jnp.int32, (blk, blk), 0)
    c = lax.broadcasted_iota(jnp.int32, (blk, blk), 1)
    tri = jnp.where(c <= r, 1.0, 0.0).astype(BF16)

    def step(i, carry):
        r0 = pl.multiple_of(i * blk, blk)
        lf = _log_sigmoid(s_ref[pl.ds(r0, blk), :] + fb_ref[...])
        hi, mid, lo = _split3(lf)
        cs = _dot(tri, hi) + _dot(tri, mid) + _dot(tri, lo) + carry
        o_ref[pl.ds(r0, blk), :] = cs
        return cs[blk - 1:blk, :]

    lax.fori_loop(0, n, step, jnp.zeros((1, s_ref.shape[1]), F32))


def fox_gate_cumsum(small, fb_row, *, batch, seq, blk=128):
    return pl.pallas_call(
        functools.partial(_fox_gate_body, blk=blk),
        grid=(batch,),
        in_specs=[pl.BlockSpec((seq, 128), lambda b: (b, 0)),
                  pl.BlockSpec((1, 128), lambda b: (0, 0))],
        out_specs=pl.BlockSpec((seq, 128), lambda b: (b, 0)),
        out_shape=jax.ShapeDtypeStruct((batch * seq, 128), F32),
        compiler_params=_cparams(("parallel",)),
        name="fox_gate",
    )(small, fb_row)


def _fox_body(q_ref, k_ref, v_ref, cq_ref, ck_ref, gate_ref, o_ref, m_ref, l_ref, acc_ref, *, tq, tk):
    i = pl.program_id(2)
    j = pl.program_id(3)
    scale = FOX_DH ** -0.5

    @pl.when(j == 0)
    def _():
        m_ref[...] = jnp.full_like(m_ref, -jnp.inf)
        l_ref[...] = jnp.zeros_like(l_ref)
        acc_ref[...] = jnp.zeros_like(acc_ref)

    def update(masked):
        s = _dot_nt(q_ref[...], k_ref[...]) * scale + (cq_ref[0] - ck_ref[0])
        if masked:
            r = lax.broadcasted_iota(jnp.int32, (tq, tk), 0)
            c = lax.broadcasted_iota(jnp.int32, (tq, tk), 1)
            s = jnp.where(r >= c, s, -jnp.inf)
        m_old = m_ref[...]
        m_new = jnp.maximum(m_old, jnp.max(s, axis=-1, keepdims=True))
        alpha = jnp.exp(m_old - m_new)
        p = jnp.exp(s - m_new)
        l_ref[...] = alpha * l_ref[...] + jnp.sum(p, axis=-1, keepdims=True)
        acc_ref[...] = alpha * acc_ref[...] + _dot(p.astype(BF16), v_ref[...])
        m_ref[...] = m_new

    @pl.when(j < i)
    def _():
        update(False)

    @pl.when(j == i)
    def _():
        update(True)
        o = acc_ref[...] / l_ref[...]
        o_ref[...] = (o * _silu(gate_ref[...].astype(F32))).astype(o_ref.dtype)


def fox(proj, c_col, c_row, *, batch, seq, q_col, k_col, v_col, gate_col, t=512):
    tt = batch * seq
    nb = seq // t
    d, h = FOX_DH, FOX_HEADS
    qb, kb, vb, gb = q_col // d, k_col // d, v_col // d, gate_col // d
    return pl.pallas_call(
        functools.partial(_fox_body, tq=t, tk=t),
        grid=(batch, h, nb, nb),
        in_specs=[
            pl.BlockSpec((t, d), lambda b, hh, i, j: (b * nb + i, qb + hh)),
            pl.BlockSpec((t, d), lambda b, hh, i, j: (b * nb + jnp.minimum(i, j), kb + hh)),
            pl.BlockSpec((t, d), lambda b, hh, i, j: (b * nb + jnp.minimum(i, j), vb + hh)),
            pl.BlockSpec((1, t, 1), lambda b, hh, i, j: (b * h + hh, i, 0)),
            pl.BlockSpec((1, 1, t), lambda b, hh, i, j: (b * h + hh, 0, jnp.minimum(i, j))),
            pl.BlockSpec((t, d), lambda b, hh, i, j: (b * nb + i, gb + hh)),
        ],
        out_specs=pl.BlockSpec((t, d), lambda b, hh, i, j: (b * nb + i, hh)),
        out_shape=jax.ShapeDtypeStruct((tt, h * d), BF16),
        scratch_shapes=[pltpu.VMEM((t, 1), F32), pltpu.VMEM((t, 1), F32), pltpu.VMEM((t, d), F32)],
        compiler_params=_cparams(("parallel", "parallel", "parallel", "arbitrary")),
        name="fox",
    )(proj, proj, proj, c_col, c_row, proj)


def _outproj_body(*refs, n_lhs, emit_resid, k_splits):
    lhs = refs[:n_lhs]
    w_ref, x_ref, g_ref = refs[n_lhs:n_lhs + 3]
    outs = refs[n_lhs + 3:]
    acc = x_ref[...]
    k0 = 0
    for a_ref, kk in zip(lhs, k_splits):
        acc = acc + _dot(a_ref[...], w_ref[k0:k0 + kk, :])
        k0 += kk
    normed = acc * lax.rsqrt(jnp.mean(acc * acc, axis=-1, keepdims=True) + EPS) * g_ref[...]
    if emit_resid:
        outs[0][...] = acc
        outs[1][...] = normed.astype(outs[1].dtype)
    else:
        outs[0][...] = normed.astype(outs[0].dtype)


def outproj_norm(lhs_list, w, x, g, *, emit_resid, tm=256, name="outproj"):
    m, d = x.shape
    k_splits = tuple(a.shape[1] for a in lhs_list)
    in_specs = [pl.BlockSpec((tm, kk), lambda i: (i, 0)) for kk in k_splits]
    in_specs += [pl.BlockSpec(w.shape, lambda i: (0, 0)),
                 pl.BlockSpec((tm, d), lambda i: (i, 0)),
                 pl.BlockSpec((1, d), lambda i: (0, 0))]
    row = pl.BlockSpec((tm, d), lambda i: (i, 0))
    if emit_resid:
        out_specs = [row, row]
        out_shape = [jax.ShapeDtypeStruct((m, d), F32), jax.ShapeDtypeStruct((m, d), BF16)]
    else:
        out_specs = row
        out_shape = jax.ShapeDtypeStruct((m, d), F32)
    return pl.pallas_call(
        functools.partial(_outproj_body, n_lhs=len(lhs_list), emit_resid=emit_resid, k_splits=k_splits),
        grid=(m // tm,),
        in_specs=in_specs,
        out_specs=out_specs,
        out_shape=out_shape,
        compiler_params=_cparams(("parallel",)),
        name=name,
    )(*lhs_list, w, x, g.reshape(1, d))


def _s5_body(u_ref, w_ref, n_ref, m_ref, aa_ref, ab_ref, d_ref, o_ref, *, rows_per_seq):
    u = u_ref[0]
    rows = u.shape[0]
    x = _dot(u, n_ref[0])
    pos = lax.broadcasted_iota(jnp.int32, (rows, 1), 0) % rows_per_seq
    half = S5_STATE
    sh, lvl = 1, 0
    while sh < rows_per_seq:
        xs = jnp.where(pos >= sh, pltpu.roll(x, sh, axis=0), 0.0)
        x = x + aa_ref[0, lvl:lvl + 1, :] * xs + ab_ref[0, lvl:lvl + 1, :] * pltpu.roll(xs, half, axis=1)
        sh, lvl = sh * 2, lvl + 1
    x_prev = jnp.where(pos >= 1, pltpu.roll(x, 1, axis=0), 0.0)
    y = _dot(u, w_ref[0]) + _dot(x_prev.astype(BF16), m_ref[0]) + d_ref[0] * u.astype(F32)
    o_ref[0] = y.astype(o_ref.dtype)


def s5_apply(u_g, w_g, n_g, m_g, a_a, a_b, d_g, *, rows_per_seq):
    g, rows, width = u_g.shape
    spec3 = lambda shp: pl.BlockSpec((1,) + shp, lambda i: (i, 0, 0))
    return pl.pallas_call(
        functools.partial(_s5_body, rows_per_seq=rows_per_seq),
        grid=(g,),
        in_specs=[spec3((rows, width)), spec3(w_g.shape[1:]), spec3(n_g.shape[1:]), spec3(m_g.shape[1:]),
                  spec3(a_a.shape[1:]), spec3(a_b.shape[1:]), spec3(d_g.shape[1:])],
        out_specs=spec3((rows, width)),
        out_shape=jax.ShapeDtypeStruct((g, rows, width), BF16),
        compiler_params=_cparams(("parallel",)),
        name="s5",
    )(u_g, w_g, n_g, m_g, a_a, a_b, d_g)


def s5_operators(lam_re, lam_im, log_step, b_re, b_im, c_re, c_im, d, *, n_levels):
    hp = lax.Precision.HIGHEST
    g, p = lam_re.shape
    cg, t = S5_GROUP_CH, S5_T
    lr = jnp.minimum(lam_re.astype(F32), -1e-4)
    li = lam_im.astype(F32)
    step = jnp.exp(log_step.astype(F32))[:, None]
    mag = jnp.exp(lr * step)
    lb_re, lb_im = mag * jnp.cos(li * step), mag * jnp.sin(li * step)
    den = lr * lr + li * li
    nr, ni = lb_re - 1.0, lb_im
    coef_re = (nr * lr + ni * li) / den
    coef_im = (ni * lr - nr * li) / den
    bb_re = coef_re[..., None] * b_re - coef_im[..., None] * b_im
    bb_im = coef_re[..., None] * b_im + coef_im[..., None] * b_re

    def power(m):
        mm = m.astype(F32)[None, :, None]
        mg = jnp.exp(lr[:, None, :] * step[:, None, :] * mm)
        ang = li[:, None, :] * step[:, None, :] * mm
        return mg * jnp.cos(ang), mg * jnp.sin(ang)

    pw_re, pw_im = power(jnp.arange(t + 1))
    cp_re = c_re[:, None] * pw_re[:, :, None, :] - c_im[:, None] * pw_im[:, :, None, :]
    cp_im = c_re[:, None] * pw_im[:, :, None, :] + c_im[:, None] * pw_re[:, :, None, :]
    taps = (jnp.einsum('gxcp,gpd->gxcd', cp_re[:, :t], bb_re, precision=hp)
            - jnp.einsum('gxcp,gpd->gxcd', cp_im[:, :t], bb_im, precision=hp))
    shift = (jnp.arange(t)[None, None, :] - jnp.arange(t)[None, :, None]
             == jnp.arange(t)[:, None, None]).astype(F32)
    w_g = jnp.einsum('xst,gxcd->gsdtc', shift, taps, precision=hp).reshape(g, t * cg, t * cg)
    pr, pi = pw_re[:, t - 1::-1][:, :t], pw_im[:, t - 1::-1][:, :t]
    n_re = pr[:, :, None, :] * bb_re.transpose(0, 2, 1)[:, None] - pi[:, :, None, :] * bb_im.transpose(0, 2, 1)[:, None]
    n_im = pr[:, :, None, :] * bb_im.transpose(0, 2, 1)[:, None] + pi[:, :, None, :] * bb_re.transpose(0, 2, 1)[:, None]
    n_g = jnp.concatenate([n_re, n_im], axis=-1).reshape(g, t * cg, 2 * p)
    m_re = cp_re[:, 1:].transpose(0, 3, 1, 2).reshape(g, p, t * cg)
    m_im = -cp_im[:, 1:].transpose(0, 3, 1, 2).reshape(g, p, t * cg)
    m_g = jnp.concatenate([m_re, m_im], axis=1)
    ar, ai = power(t * (2 ** jnp.arange(n_levels)))
    a_a = jnp.concatenate([ar, ar], axis=-1)
    a_b = jnp.concatenate([-ai, ai], axis=-1)
    d_g = jnp.tile(d.astype(F32).reshape(g, 1, cg), (1, t, 1)).reshape(g, 1, t * cg)
    return w_g.astype(BF16), n_g.astype(BF16), m_g.astype(BF16), a_a, a_b, d_g


def _gelu_tanh(y):
    return 0.5 * y * (1.0 + jnp.tanh(math.sqrt(2.0 / math.pi) * (y + 0.044715 * (y * y * y))))


def _glu_body(y_ref, w_ref, b_ref, gate_ref, o_ref, z_ref, *, tn):
    j = pl.program_id(1)

    @pl.when(j == 0)
    def _():
        z_ref[...] = _gelu_tanh(y_ref[...].astype(F32))

    c0 = pl.multiple_of(j * tn, tn)
    z = z_ref[:, pl.ds(c0, tn)]
    lin = _dot(z_ref[...].astype(BF16), w_ref[...]) + b_ref[...]
    o_ref[...] = (z * _sigmoid(lin) * _silu(gate_ref[...].astype(F32))).astype(o_ref.dtype)


def glu_gate(y, w, b, proj, gate_col, tm=512, tn=512):
    m, e = y.shape
    gb = gate_col // tn
    return pl.pallas_call(
        functools.partial(_glu_body, tn=tn),
        grid=(m // tm, e // tn),
        in_specs=[pl.BlockSpec((tm, e), lambda i, j: (i, 0)),
                  pl.BlockSpec((e, tn), lambda i, j: (0, j)),
                  pl.BlockSpec((1, tn), lambda i, j: (0, j)),
                  pl.BlockSpec((tm, tn), lambda i, j: (i, gb + j))],
        out_specs=pl.BlockSpec((tm, tn), lambda i, j: (i, j)),
        out_shape=jax.ShapeDtypeStruct((m, e), BF16),
        scratch_shapes=[pltpu.VMEM((tm, e), F32)],
        compiler_params=_cparams(("parallel", "arbitrary")),
        name="glu",
    )(y, w, b.reshape(1, e), proj)


def kernel(x, ab_norm_g, ab_w_in, gla_alpha_up, gla_alpha_b, gla_head_g, fox_f_b, ab_w_out, c_norm_g, c_w_in, s5_lambda_re, s5_lambda_im, s5_log_step, s5_b_re, s5_b_im, s5_c_re, s5_c_im, s5_d, glu_w, glu_b, c_w_out, final_norm_g):
    batch, seq, d = x.shape
    t = batch * seq
    x2 = x.reshape(t, d)

    w_in = ab_w_in[0]
    hk, hv, fw = GLA_HEADS * GLA_DK, GLA_HEADS * GLA_DV, FOX_HEADS * FOX_DH
    sizes = (hk, hk, hv, GLA_LOWRANK, hv, fw, fw, fw, FOX_HEADS, fw)
    offs = [0]
    for s in sizes:
        offs.append(offs[-1] + s)
    seg = lambda n: w_in[:, offs[n]:offs[n + 1]]
    w_main = jnp.concatenate([seg(0), seg(1), seg(4), seg(5), seg(6), seg(7), seg(9)], axis=1).astype(BF16)
    c_gq, c_gk, c_gg = 0, hk, 2 * hk
    c_fq = c_gg + hv
    c_fk, c_fv, c_fg = c_fq + fw, c_fq + 2 * fw, c_fq + 3 * fw
    w_vt = seg(2).T.astype(BF16)
    pad = 128 - GLA_LOWRANK - FOX_HEADS
    w_small = jnp.concatenate([seg(3), seg(8), jnp.zeros((d, pad), F32)], axis=1).astype(BF16)

    h0 = rmsnorm_bf16(x2, ab_norm_g[0])
    proj = matmul(h0, w_main, BF16, name="ab_in")
    vt = matmul_t(w_vt, h0, BF16, name="ab_in_vt")
    small = matmul(h0, w_small, F32, name="ab_in_small")

    o_gla = gla(proj, vt, small, gla_alpha_up[0], gla_alpha_b[0], gla_head_g[0],
                batch=batch, seq=seq, q_col=c_gq, k_col=c_gk, gate_col=c_gg)

    fb_row = jnp.zeros((1, 128), F32).at[0, GLA_LOWRANK:GLA_LOWRANK + FOX_HEADS].set(fox_f_b[0])
    csum = fox_gate_cumsum(small, fb_row, batch=batch, seq=seq)
    c_bhl = csum[:, GLA_LOWRANK:GLA_LOWRANK + FOX_HEADS].reshape(batch, seq, FOX_HEADS).transpose(0, 2, 1)
    c_bhl = c_bhl.reshape(batch * FOX_HEADS, seq)
    o_fox = fox(proj, c_bhl[:, :, None], c_bhl[:, None, :], batch=batch, seq=seq,
                q_col=c_fq, k_col=c_fk, v_col=c_fv, gate_col=c_fg)

    x1, h1 = outproj_norm([o_gla, o_fox], ab_w_out[0].astype(BF16), x2, c_norm_g[0],
                          emit_resid=True, name="ab_out")

    e = d
    proj_c = matmul(h1, c_w_in[0].astype(BF16), BF16, name="c_in")
    g = e // S5_GROUP_CH
    n_chunks = t // S5_T
    rows_per_seq = seq // S5_T
    u_g = proj_c[:, :e].reshape(n_chunks, S5_T, g, S5_GROUP_CH).transpose(2, 0, 1, 3)
    u_g = u_g.reshape(g, n_chunks, S5_T * S5_GROUP_CH)
    n_levels = max(1, (rows_per_seq - 1).bit_length())
    ops = s5_operators(s5_lambda_re[0], s5_lambda_im[0], s5_log_step[0], s5_b_re[0], s5_b_im[0],
                       s5_c_re[0], s5_c_im[0], s5_d[0], n_levels=n_levels)
    y_g = s5_apply(u_g, *ops, rows_per_seq=rows_per_seq)
    y = y_g.reshape(g, n_chunks, S5_T, S5_GROUP_CH).transpose(1, 2, 0, 3).reshape(t, e)

    zz = glu_gate(y, glu_w[0].astype(BF16), glu_b[0], proj_c, gate_col=e)
    out = outproj_norm([zz], c_w_out[0].astype(BF16), x1, final_norm_g, emit_resid=False, name="c_out")
    return out.reshape(batch, seq, d)
```

```python
import functools
import math

import jax
import jax.numpy as jnp
from jax import lax
from jax.experimental import pallas as pl
from jax.experimental.pallas import tpu as pltpu

EPS = 1e-6
F32 = jnp.float32
BF16 = jnp.bfloat16

GLA_HEADS = 4
GLA_DK = 128
GLA_DV = 256
GLA_LOWRANK = 16
GLA_TAU = 16.0
GLA_CHUNK = 64
GLA_SUB = 16
FOX_HEADS = 8
FOX_DH = 128
FOX_AUG = 128
S5_GROUP_CH = 16
S5_STATE = 64
S5_T = 16
S5_TAB = 32

VMEM_LIMIT = 56 * 1024 * 1024

NT_DIMS = (((1,), (1,)), ((), ()))


def _cparams(sem):
    return pltpu.CompilerParams(dimension_semantics=sem, vmem_limit_bytes=VMEM_LIMIT)


def _dot(a, b):
    return jnp.dot(a, b, preferred_element_type=F32)


def _dot_nt(a, b):
    return lax.dot_general(a, b, NT_DIMS, preferred_element_type=F32)


def _log_sigmoid(z):
    return -(jnp.maximum(-z, 0.0) + jnp.log1p(jnp.exp(-jnp.abs(z))))


def _sigmoid(z):
    return 1.0 / (1.0 + jnp.exp(-z))


def _silu(z):
    return z * _sigmoid(z)


def _split3(x):
    hi = x.astype(BF16)
    r1 = x - hi.astype(F32)
    mid = r1.astype(BF16)
    lo = (r1 - mid.astype(F32)).astype(BF16)
    return hi, mid, lo


def _rmsnorm_body(x_ref, g_ref, o_ref):
    x = x_ref[...]
    ms = jnp.mean(x * x, axis=-1, keepdims=True)
    o_ref[...] = (x * lax.rsqrt(ms + EPS) * g_ref[...]).astype(o_ref.dtype)


def rmsnorm_bf16(x, g, tm=512):
    m, d = x.shape
    return pl.pallas_call(
        _rmsnorm_body,
        grid=(m // tm,),
        in_specs=[pl.BlockSpec((tm, d), lambda i: (i, 0)),
                  pl.BlockSpec((1, d), lambda i: (0, 0))],
        out_specs=pl.BlockSpec((tm, d), lambda i: (i, 0)),
        out_shape=jax.ShapeDtypeStruct((m, d), BF16),
        compiler_params=_cparams(("parallel",)),
        name="rmsnorm",
    )(x, g.reshape(1, d))


def _mm_body(a_ref, w_ref, o_ref):
    o_ref[...] = _dot(a_ref[...], w_ref[...]).astype(o_ref.dtype)


def matmul(a, w, out_dtype, tm=512, tn=512, name="matmul"):
    m, k = a.shape
    n = w.shape[1]
    tn = min(tn, n)
    return pl.pallas_call(
        _mm_body,
        grid=(m // tm, n // tn),
        in_specs=[pl.BlockSpec((tm, k), lambda i, j: (i, 0)),
                  pl.BlockSpec((k, tn), lambda i, j: (0, j))],
        out_specs=pl.BlockSpec((tm, tn), lambda i, j: (i, j)),
        out_shape=jax.ShapeDtypeStruct((m, n), out_dtype),
        compiler_params=_cparams(("parallel", "parallel")),
        name=name,
    )(a, w)


def _mm_nt_body(wt_ref, a_ref, o_ref):
    o_ref[...] = _dot_nt(wt_ref[...], a_ref[...]).astype(o_ref.dtype)


def matmul_t(wt, a, out_dtype, tm=512, tn=512, name="matmul_t"):
    n, k = wt.shape
    m = a.shape[0]
    tn = min(tn, n)
    return pl.pallas_call(
        _mm_nt_body,
        grid=(m // tm, n // tn),
        in_specs=[pl.BlockSpec((tn, k), lambda i, j: (j, 0)),
                  pl.BlockSpec((tm, k), lambda i, j: (i, 0))],
        out_specs=pl.BlockSpec((tn, tm), lambda i, j: (j, i)),
        out_shape=jax.ShapeDtypeStruct((n, m), out_dtype),
        compiler_params=_cparams(("parallel", "parallel")),
        name=name,
    )(wt, a)


def matmul_t_grouped(wt, a2, steps, out_dtype, tm=512, tn=512, name="matmul_tg"):
    nf, k = wt.shape
    n = a2.shape[0]
    return pl.pallas_call(
        _mm_nt_body,
        grid=(nf // tn, steps, n // tm),
        in_specs=[pl.BlockSpec((tn, k), lambda j, t, i: (j, 0)),
                  pl.BlockSpec((tm, k), lambda j, t, i: (i, t))],
        out_specs=pl.BlockSpec((None, tn, tm), lambda j, t, i: (t, j, i)),
        out_shape=jax.ShapeDtypeStruct((steps, nf, n), out_dtype),
        compiler_params=_cparams(("parallel", "parallel", "parallel")),
        name=name,
    )(wt, a2)


def _gla_body(q_ref, k_ref, vt_ref, glr_ref, aup_ref, ab_ref, gate_ref, hg_ref, o_ref,
              st_ref, b_ref, kf_ref, vts_ref, *, tb):
    ib = pl.program_id(2)
    C, SB = GLA_CHUNK, GLA_SUB

    @pl.when(ib == 0)
    def _():
        st_ref[...] = jnp.zeros_like(st_ref)

    z = _dot(glr_ref[:, :GLA_LOWRANK].astype(BF16), aup_ref[...].astype(BF16)) + ab_ref[...]
    la = _log_sigmoid(z) * (1.0 / GLA_TAU)
    r = lax.broadcasted_iota(jnp.int32, (tb, tb), 0)
    c = lax.broadcasted_iota(jnp.int32, (tb, tb), 1)
    tri = jnp.where((r // C == c // C) & (c <= r), 1.0, 0.0).astype(BF16)
    hi, mid, lo = _split3(la)
    b_ref[...] = _dot(tri, hi) + _dot(tri, mid) + _dot(tri, lo)
    kf_ref[...] = k_ref[...].astype(F32)
    for ci in range(tb // C):
        vts_ref[ci] = vt_ref[:, ci * C:(ci + 1) * C]

    row = lax.broadcasted_iota(jnp.int32, (C, 1), 0)
    blk = row // SB
    rowi = lax.broadcasted_iota(jnp.int32, (C, C), 0) // SB
    coli = lax.broadcasted_iota(jnp.int32, (C, C), 1) // SB
    mask_b = (rowi % 2 == 1) & (coli == rowi - 1)
    lane_c = lax.broadcasted_iota(jnp.int32, (SB, C), 1)
    sub_r = lax.broadcasted_iota(jnp.int32, (SB, 1), 0)
    ones = jnp.ones((GLA_DK, C), BF16)
    neg = -jnp.inf
    scale = GLA_DK ** -0.5

    def chunk(ci, carry):
        r0 = pl.multiple_of(ci * C, C)
        q = q_ref[pl.ds(r0, C), :].astype(F32) * scale
        k = kf_ref[pl.ds(r0, C), :]
        b = b_ref[pl.ds(r0, C), :]
        vt = vts_ref[ci]
        b_last = b[C - 1:C, :]

        st = st_ref[...]
        o = _dot_nt((q * jnp.exp(b)).astype(BF16), st.astype(BF16))
        k_out = (k * jnp.exp(b_last - b)).astype(BF16)
        st_ref[...] = st * jnp.exp(b_last) + _dot(vt, k_out)

        r_a = b[2 * SB - 1:2 * SB, :]
        q_a = q * jnp.exp(jnp.where(blk >= 2, b - r_a, neg))
        k_a = k * jnp.exp(jnp.where(blk < 2, r_a - b, neg))
        r_b = jnp.where(blk < 2, b[SB - 1:SB, :], b[3 * SB - 1:3 * SB, :])
        q_b = q * jnp.exp(jnp.where(blk % 2 == 1, b - r_b, neg))
        k_b = k * jnp.exp(jnp.where(blk % 2 == 0, r_b - b, neg))
        attn = _dot_nt(q_a.astype(BF16), k_a.astype(BF16))
        attn = attn + jnp.where(mask_b, _dot_nt(q_b.astype(BF16), k_b.astype(BF16)), 0.0)

        zs = []
        for bi in range(C // SB):
            s0 = bi * SB
            q_i = q[s0:s0 + SB, :]
            b_i = b[s0:s0 + SB, :]
            for j in range(SB):
                k_j = kf_ref[pl.ds(r0 + (s0 + j), 1), :]
                b_j = b_ref[pl.ds(r0 + (s0 + j), 1), :]
                zs.append(q_i * k_j * jnp.exp(jnp.where(sub_r >= j, b_i - b_j, neg)))
        zsum = _dot(jnp.concatenate(zs, axis=0).astype(BF16), ones)
        diag = []
        for bi in range(C // SB):
            acc = jnp.zeros((SB, C), F32)
            for j in range(SB):
                n0 = (bi * SB + j) * SB
                acc = acc + jnp.where(lane_c == bi * SB + j, zsum[n0:n0 + SB, :], 0.0)
            diag.append(acc)
        attn = attn + jnp.concatenate(diag, axis=0)
        o = o + _dot_nt(attn.astype(BF16), vt)

        o = o * lax.rsqrt(jnp.mean(o * o, axis=-1, keepdims=True) + EPS) * hg_ref[...]
        g = gate_ref[pl.ds(r0, C), :].astype(F32)
        o_ref[pl.ds(r0, C), :] = (o * _silu(g)).astype(o_ref.dtype)
        return carry

    lax.fori_loop(0, tb // C, chunk, 0)


def gla(proj, vt, small, alpha_up, alpha_b, head_g, *, batch, seq, q_col, k_col, gate_col, tb=512):
    t = batch * seq
    nb = seq // tb
    dk, dv, h = GLA_DK, GLA_DV, GLA_HEADS
    qb, kb, gb = q_col // dk, k_col // dk, gate_col // dv
    tok = lambda b, hh, i: b * nb + i
    return pl.pallas_call(
        functools.partial(_gla_body, tb=tb),
        grid=(batch, h, nb),
        in_specs=[
            pl.BlockSpec((tb, dk), lambda b, hh, i: (tok(b, hh, i), qb + hh)),
            pl.BlockSpec((tb, dk), lambda b, hh, i: (tok(b, hh, i), kb + hh)),
            pl.BlockSpec((dv, tb), lambda b, hh, i: (hh, tok(b, hh, i))),
            pl.BlockSpec((tb, 128), lambda b, hh, i: (tok(b, hh, i), 0)),
            pl.BlockSpec((GLA_LOWRANK, dk), lambda b, hh, i: (0, hh)),
            pl.BlockSpec((1, dk), lambda b, hh, i: (0, hh)),
            pl.BlockSpec((tb, dv), lambda b, hh, i: (tok(b, hh, i), gb + hh)),
            pl.BlockSpec((1, dv), lambda b, hh, i: (0, hh)),
        ],
        out_specs=pl.BlockSpec((tb, dv), lambda b, hh, i: (tok(b, hh, i), hh)),
        out_shape=jax.ShapeDtypeStruct((t, h * dv), BF16),
        scratch_shapes=[pltpu.VMEM((dv, dk), F32), pltpu.VMEM((tb, dk), F32), pltpu.VMEM((tb, dk), F32),
                        pltpu.VMEM((tb // GLA_CHUNK, dv, GLA_CHUNK), BF16)],
        compiler_params=_cparams(("parallel", "parallel", "arbitrary")),
        name="gla",
    )(proj, proj, vt, small, alpha_up, alpha_b.reshape(1, -1), proj, head_g.reshape(1, -1))


def _fox_gate_body(s_ref, fb_ref, qa_ref, ka_ref, *, blk, col0):
    n = s_ref.shape[0] // blk
    r = lax.broadcasted_iota(jnp.int32, (blk, blk), 0)
    c = lax.broadcasted_iota(jnp.int32, (blk, blk), 1)
    tri = jnp.where(c <= r, 1.0, 0.0).astype(BF16)
    lane = lax.broadcasted_iota(jnp.int32, (blk, FOX_AUG), 1)

    def step(i, carry):
        r0 = pl.multiple_of(i * blk, blk)
        lf = _log_sigmoid(s_ref[pl.ds(r0, blk), :] + fb_ref[...])
        hi, mid, lo = _split3(lf)
        cs = _dot(tri, hi) + _dot(tri, mid) + _dot(tri, lo) + carry
        for h in range(FOX_HEADS):
            col = cs[:, col0 + h:col0 + h + 1]
            c0, c1, c2 = (p.astype(F32) for p in _split3(col))
            qa = jnp.where(lane == 0, c0, jnp.where(lane == 1, c1, jnp.where(lane == 2, c2,
                           jnp.where(lane < 6, 1.0, 0.0))))
            ka = jnp.where(lane < 3, 1.0, jnp.where(lane == 3, -c0, jnp.where(lane == 4, -c1,
                           jnp.where(lane == 5, -c2, 0.0))))
            qa_ref[h, pl.ds(r0, blk), :] = qa.astype(BF16)
            ka_ref[h, pl.ds(r0, blk), :] = ka.astype(BF16)
        return cs[blk - 1:blk, :]

    lax.fori_loop(0, n, step, jnp.zeros((1, s_ref.shape[1]), F32))


def fox_gate(small, fb_row, *, batch, seq, col0, blk=128):
    shp = jax.ShapeDtypeStruct((FOX_HEADS, batch * seq, FOX_AUG), BF16)
    spec = pl.BlockSpec((FOX_HEADS, seq, FOX_AUG), lambda b: (0, b, 0))
    return pl.pallas_call(
        functools.partial(_fox_gate_body, blk=blk, col0=col0),
        grid=(batch,),
        in_specs=[pl.BlockSpec((seq, 128), lambda b: (b, 0)),
                  pl.BlockSpec((1, 128), lambda b: (0, 0))],
        out_specs=[spec, spec],
        out_shape=[shp, shp],
        compiler_params=_cparams(("parallel",)),
        name="fox_gate",
    )(small, fb_row)


def _fox_body(it_ref, jt_ref, q_ref, qa_ref, k_ref, ka_ref, vt_ref, gate_ref, o_ref,
              m_ref, l_ref, acc_ref, *, tq, tk):
    p = pl.program_id(2)
    i = it_ref[p]
    j = jt_ref[p]

    @pl.when(j == 0)
    def _():
        m_ref[...] = jnp.full_like(m_ref, -jnp.inf)
        l_ref[...] = jnp.zeros_like(l_ref)
        acc_ref[...] = jnp.zeros_like(acc_ref)

    def update(masked):
        q_aug = jnp.concatenate([q_ref[...], qa_ref[...]], axis=1)
        k_aug = jnp.concatenate([k_ref[...], ka_ref[...]], axis=1)
        st = _dot_nt(k_aug, q_aug)
        if masked:
            kr = lax.broadcasted_iota(jnp.int32, (tk, tq), 0)
            qc = lax.broadcasted_iota(jnp.int32, (tk, tq), 1)
            st = jnp.where(qc >= kr, st, -jnp.inf)
        m_old = m_ref[...]
        m_new = jnp.maximum(m_old, jnp.max(st, axis=0, keepdims=True))
        alpha = jnp.exp(m_old - m_new)
        pt = jnp.exp(st - m_new)
        l_ref[...] = alpha * l_ref[...] + jnp.sum(pt, axis=0, keepdims=True)
        acc_ref[...] = alpha * acc_ref[...] + _dot(vt_ref[...], pt.astype(BF16))
        m_ref[...] = m_new

    @pl.when(j < i)
    def _():
        update(False)

    @pl.when(j == i)
    def _():
        update(True)
        o = (acc_ref[...] / l_ref[...]).T
        o_ref[...] = (o * _silu(gate_ref[...].astype(F32))).astype(o_ref.dtype)


def fox(proj, vt, qa, ka, *, batch, seq, q_col, k_col, gate_col, vt_row, t=512):
    tt = batch * seq
    nb = seq // t
    d, h = FOX_DH, FOX_HEADS
    qb, kb, gb, vb = q_col // d, k_col // d, gate_col // d, vt_row // d
    pairs = [(i, j) for i in range(nb) for j in range(i + 1)]
    it = jnp.array([p[0] for p in pairs], jnp.int32)
    jt = jnp.array([p[1] for p in pairs], jnp.int32)
    grid_spec = pltpu.PrefetchScalarGridSpec(
        num_scalar_prefetch=2,
        grid=(batch, h, len(pairs)),
        in_specs=[
            pl.BlockSpec((t, d), lambda b, hh, p, it, jt: (b * nb + it[p], qb + hh)),
            pl.BlockSpec((None, t, FOX_AUG), lambda b, hh, p, it, jt: (hh, b * nb + it[p], 0)),
            pl.BlockSpec((t, d), lambda b, hh, p, it, jt: (b * nb + jt[p], kb + hh)),
            pl.BlockSpec((None, t, FOX_AUG), lambda b, hh, p, it, jt: (hh, b * nb + jt[p], 0)),
            pl.BlockSpec((d, t), lambda b, hh, p, it, jt: (vb + hh, b * nb + jt[p])),
            pl.BlockSpec((t, d), lambda b, hh, p, it, jt: (b * nb + it[p], gb + hh)),
        ],
        out_specs=pl.BlockSpec((t, d), lambda b, hh, p, it, jt: (b * nb + it[p], hh)),
        scratch_shapes=[pltpu.VMEM((1, t), F32), pltpu.VMEM((1, t), F32), pltpu.VMEM((d, t), F32)],
    )
    return pl.pallas_call(
        functools.partial(_fox_body, tq=t, tk=t),
        grid_spec=grid_spec,
        out_shape=jax.ShapeDtypeStruct((tt, h * d), BF16),
        compiler_params=_cparams(("parallel", "parallel", "arbitrary")),
        name="fox",
    )(it, jt, proj, qa, proj, ka, vt, proj)


def _outproj_body(*refs, n_lhs, emit_resid, k_splits):
    lhs = refs[:n_lhs]
    w_ref, x_ref, g_ref = refs[n_lhs:n_lhs + 3]
    outs = refs[n_lhs + 3:]
    acc = x_ref[...]
    k0 = 0
    for a_ref, kk in zip(lhs, k_splits):
        acc = acc + _dot(a_ref[...], w_ref[k0:k0 + kk, :])
        k0 += kk
    normed = acc * lax.rsqrt(jnp.mean(acc * acc, axis=-1, keepdims=True) + EPS) * g_ref[...]
    if emit_resid:
        outs[0][...] = acc
        outs[1][...] = normed.astype(outs[1].dtype)
    else:
        outs[0][...] = normed.astype(outs[0].dtype)


def outproj_norm(lhs_list, w, x, g, *, emit_resid, tm=256, name="outproj"):
    m, d = x.shape
    k_splits = tuple(a.shape[1] for a in lhs_list)
    in_specs = [pl.BlockSpec((tm, kk), lambda i: (i, 0)) for kk in k_splits]
    in_specs += [pl.BlockSpec(w.shape, lambda i: (0, 0)),
                 pl.BlockSpec((tm, d), lambda i: (i, 0)),
                 pl.BlockSpec((1, d), lambda i: (0, 0))]
    row = pl.BlockSpec((tm, d), lambda i: (i, 0))
    if emit_resid:
        out_specs = [row, row]
        out_shape = [jax.ShapeDtypeStruct((m, d), F32), jax.ShapeDtypeStruct((m, d), BF16)]
    else:
        out_specs = row
        out_shape = jax.ShapeDtypeStruct((m, d), F32)
    return pl.pallas_call(
        functools.partial(_outproj_body, n_lhs=len(lhs_list), emit_resid=emit_resid, k_splits=k_splits),
        grid=(m // tm,),
        in_specs=in_specs,
        out_specs=out_specs,
        out_shape=out_shape,
        compiler_params=_cparams(("parallel",)),
        name=name,
    )(*lhs_list, w, x, g.reshape(1, d))


def _s5_body(u_ref, wt_ref, nt_ref, mt_ref, tab_ref, o_ref, *, cols_per_seq, n_levels):
    t, cg, n = u_ref.shape
    u = u_ref[...].reshape(t * cg, n)
    tab = tab_ref[0]
    half = S5_STATE
    x = _dot(nt_ref[0], u)
    pos = lax.broadcasted_iota(jnp.int32, (1, n), 1) % cols_per_seq
    for lvl in range(n_levels):
        sh = 1 << lvl
        xs = jnp.where(pos >= sh, pltpu.roll(x, sh, axis=1), 0.0)
        a_a = tab[:2 * half, 1 + lvl:2 + lvl]
        a_b = tab[:2 * half, 1 + n_levels + lvl:2 + n_levels + lvl]
        x = x + a_a * xs + a_b * pltpu.roll(xs, half, axis=0)
    x_prev = jnp.where(pos >= 1, pltpu.roll(x, 1, axis=1), 0.0)
    y = _dot(wt_ref[0], u) + _dot(mt_ref[0], x_prev.astype(BF16)) + tab[:, 0:1] * u.astype(F32)
    o_ref[...] = y.astype(o_ref.dtype).reshape(t, cg, n)


def s5_apply(proj_t, wt_g, nt_g, mt_g, tab, *, groups, cols_per_seq, n_levels):
    t, _, n = proj_t.shape
    cg = S5_GROUP_CH
    spec3 = lambda shp: pl.BlockSpec((1,) + shp, lambda i: (i, 0, 0))
    return pl.pallas_call(
        functools.partial(_s5_body, cols_per_seq=cols_per_seq, n_levels=n_levels),
        grid=(groups,),
        in_specs=[pl.BlockSpec((t, cg, n), lambda i: (0, i, 0)),
                  spec3(wt_g.shape[1:]), spec3(nt_g.shape[1:]), spec3(mt_g.shape[1:]), spec3(tab.shape[1:])],
        out_specs=pl.BlockSpec((t, cg, n), lambda i: (0, i, 0)),
        out_shape=jax.ShapeDtypeStruct((t, groups * cg, n), BF16),
        compiler_params=_cparams(("parallel",)),
        name="s5",
    )(proj_t, wt_g, nt_g, mt_g, tab)


def s5_operators(lam_re, lam_im, log_step, b_re, b_im, c_re, c_im, d, *, n_levels):
    hp = lax.Precision.HIGHEST
    g, p = lam_re.shape
    cg, t = S5_GROUP_CH, S5_T
    lr = jnp.minimum(lam_re.astype(F32), -1e-4)
    li = lam_im.astype(F32)
    step = jnp.exp(log_step.astype(F32))[:, None]
    mag = jnp.exp(lr * step)
    lb_re, lb_im = mag * jnp.cos(li * step), mag * jnp.sin(li * step)
    den = lr * lr + li * li
    nr, ni = lb_re - 1.0, lb_im
    coef_re = (nr * lr + ni * li) / den
    coef_im = (ni * lr - nr * li) / den
    bb_re = coef_re[..., None] * b_re - coef_im[..., None] * b_im
    bb_im = coef_re[..., None] * b_im + coef_im[..., None] * b_re

    def power(m):
        mm = m.astype(F32)[None, :, None]
        mg = jnp.exp(lr[:, None, :] * step[:, None, :] * mm)
        ang = li[:, None, :] * step[:, None, :] * mm
        return mg * jnp.cos(ang), mg * jnp.sin(ang)

    pw_re, pw_im = power(jnp.arange(t + 1))
    cp_re = c_re[:, None] * pw_re[:, :, None, :] - c_im[:, None] * pw_im[:, :, None, :]
    cp_im = c_re[:, None] * pw_im[:, :, None, :] + c_im[:, None] * pw_re[:, :, None, :]
    taps = (jnp.einsum('gxcp,gpd->gxcd', cp_re[:, :t], bb_re, precision=hp)
            - jnp.einsum('gxcp,gpd->gxcd', cp_im[:, :t], bb_im, precision=hp))
    shift = (jnp.arange(t)[None, None, :] - jnp.arange(t)[None, :, None]
             == jnp.arange(t)[:, None, None]).astype(F32)
    wt_g = jnp.einsum('xst,gxcd->gtcsd', shift, taps, precision=hp).reshape(g, t * cg, t * cg)
    pr, pi = pw_re[:, t - 1::-1], pw_im[:, t - 1::-1]
    bt_re, bt_im = bb_re[:, :, None, :], bb_im[:, :, None, :]
    prt, pit = pr.transpose(0, 2, 1)[:, :, :, None], pi.transpose(0, 2, 1)[:, :, :, None]
    n_re = (prt * bt_re - pit * bt_im).reshape(g, p, t * cg)
    n_im = (prt * bt_im + pit * bt_re).reshape(g, p, t * cg)
    nt_g = jnp.concatenate([n_re, n_im], axis=1)
    m_re = cp_re[:, 1:].reshape(g, t * cg, p)
    m_im = cp_im[:, 1:].reshape(g, t * cg, p)
    mt_g = jnp.concatenate([m_re, -m_im], axis=2)
    ar, ai = power(t * (2 ** jnp.arange(n_levels)))
    a_a = jnp.concatenate([ar, ar], axis=-1).transpose(0, 2, 1)
    a_b = jnp.concatenate([-ai, ai], axis=-1).transpose(0, 2, 1)
    d_col = jnp.tile(d.astype(F32).reshape(g, cg), (1, t))[:, :, None]
    rows = cg * t
    padr = lambda a: jnp.pad(a, ((0, 0), (0, rows - a.shape[1]), (0, 0)))
    tab = jnp.concatenate([d_col, padr(a_a), padr(a_b)], axis=2)
    tab = jnp.pad(tab, ((0, 0), (0, 0), (0, S5_TAB - tab.shape[2])))
    return wt_g.astype(BF16), nt_g.astype(BF16), mt_g.astype(BF16), tab


def _gelu_tanh(y):
    return 0.5 * y * (1.0 + jnp.tanh(math.sqrt(2.0 / math.pi) * (y + 0.044715 * (y * y * y))))


def _glu_body(y_ref, w_ref, b_ref, gate_ref, o_ref, z_ref, zb_ref, *, tf):
    j = pl.program_id(2)

    @pl.when(j == 0)
    def _():
        z = _gelu_tanh(y_ref[...].astype(F32))
        z_ref[...] = z
        zb_ref[...] = z.astype(BF16)

    f0 = pl.multiple_of(j * tf, tf)
    z = z_ref[pl.ds(f0, tf), :]
    lin = _dot(w_ref[...], zb_ref[...]) + b_ref[...]
    out = z * _sigmoid(lin) * _silu(gate_ref[...].astype(F32))
    o_ref[...] = out.T.astype(o_ref.dtype)


def glu_gate(y_t, w_t, b, proj_t, *, gate_row, tm=512, tf=512):
    s, e, n = y_t.shape
    gb = gate_row // tf
    nf = e // tf
    out2 = pl.pallas_call(
        functools.partial(_glu_body, tf=tf),
        grid=(s, n // tm, nf),
        in_specs=[pl.BlockSpec((None, e, tm), lambda t, i, j: (t, 0, i)),
                  pl.BlockSpec((tf, e), lambda t, i, j: (j, 0)),
                  pl.BlockSpec((tf, 1), lambda t, i, j: (j, 0)),
                  pl.BlockSpec((None, tf, tm), lambda t, i, j: (t, gb + j, i))],
        out_specs=pl.BlockSpec((tm, tf), lambda t, i, j: (i, t * nf + j)),
        out_shape=jax.ShapeDtypeStruct((n, s * e), BF16),
        scratch_shapes=[pltpu.VMEM((e, tm), F32), pltpu.VMEM((e, tm), BF16)],
        compiler_params=_cparams(("parallel", "parallel", "arbitrary")),
        name="glu",
    )(y_t, w_t, b.reshape(e, 1), proj_t)
    return out2.reshape(n * s, e)


def kernel(x, ab_norm_g, ab_w_in, gla_alpha_up, gla_alpha_b, gla_head_g, fox_f_b, ab_w_out, c_norm_g, c_w_in, s5_lambda_re, s5_lambda_im, s5_log_step, s5_b_re, s5_b_im, s5_c_re, s5_c_im, s5_d, glu_w, glu_b, c_w_out, final_norm_g):
    batch, seq, d = x.shape
    t = batch * seq
    x2 = x.reshape(t, d)

    w_in = ab_w_in[0]
    hk, hv, fw = GLA_HEADS * GLA_DK, GLA_HEADS * GLA_DV, FOX_HEADS * FOX_DH
    sizes = (hk, hk, hv, GLA_LOWRANK, hv, fw, fw, fw, FOX_HEADS, fw)
    offs = [0]
    for s in sizes:
        offs.append(offs[-1] + s)
    seg = lambda n: w_in[:, offs[n]:offs[n + 1]]
    w_main = jnp.concatenate([seg(0), seg(1), seg(4), seg(5) * (FOX_DH ** -0.5), seg(6), seg(9)],
                             axis=1).astype(BF16)
    c_gq, c_gk, c_gg = 0, hk, 2 * hk
    c_fq = c_gg + hv
    c_fk, c_fg = c_fq + fw, c_fq + 2 * fw
    w_vt = jnp.concatenate([seg(2), seg(7)], axis=1).T.astype(BF16)
    pad = 128 - GLA_LOWRANK - FOX_HEADS
    w_small = jnp.concatenate([seg(3), seg(8), jnp.zeros((d, pad), F32)], axis=1).astype(BF16)

    h0 = rmsnorm_bf16(x2, ab_norm_g[0])
    proj = matmul(h0, w_main, BF16, name="ab_in")
    vt = matmul_t(w_vt, h0, BF16, name="ab_in_vt")
    small = matmul(h0, w_small, F32, name="ab_in_small")

    o_gla = gla(proj, vt, small, gla_alpha_up[0], gla_alpha_b[0], gla_head_g[0],
                batch=batch, seq=seq, q_col=c_gq, k_col=c_gk, gate_col=c_gg)

    fb_row = jnp.zeros((1, 128), F32).at[0, GLA_LOWRANK:GLA_LOWRANK + FOX_HEADS].set(fox_f_b[0])
    qa, ka = fox_gate(small, fb_row, batch=batch, seq=seq, col0=GLA_LOWRANK)
    o_fox = fox(proj, vt, qa, ka, batch=batch, seq=seq, q_col=c_fq, k_col=c_fk, gate_col=c_fg, vt_row=hv)

    x1, h1 = outproj_norm([o_gla, o_fox], ab_w_out[0].astype(BF16), x2, c_norm_g[0],
                          emit_resid=True, name="ab_out")

    e = d
    groups = e // S5_GROUP_CH
    n_chunks = t // S5_T
    cols_per_seq = seq // S5_T
    n_levels = max(1, (cols_per_seq - 1).bit_length())
    proj_t = matmul_t_grouped(c_w_in[0].T.astype(BF16), h1.reshape(n_chunks, S5_T * d), S5_T, BF16,
                              name="c_in")
    ops = s5_operators(s5_lambda_re[0], s5_lambda_im[0], s5_log_step[0], s5_b_re[0], s5_b_im[0],
                       s5_c_re[0], s5_c_im[0], s5_d[0], n_levels=n_levels)
    y_t = s5_apply(proj_t, *ops, groups=groups, cols_per_seq=cols_per_seq, n_levels=n_levels)
    zz = glu_gate(y_t, glu_w[0].T.astype(BF16), glu_b[0], proj_t, gate_row=e)
    out = outproj_norm([zz], c_w_out[0].astype(BF16), x1, final_norm_g, emit_resid=False, name="c_out")
    return out.reshape(batch, seq, d)
```

```python
import functools
import math

import jax
import jax.numpy as jnp
from jax import lax
from jax.experimental import pallas as pl
from jax.experimental.pallas import tpu as pltpu

EPS = 1e-6
F32 = jnp.float32
BF16 = jnp.bfloat16

GLA_HEADS = 4
GLA_DK = 128
GLA_DV = 256
GLA_LOWRANK = 16
GLA_TAU = 16.0
GLA_CHUNK = 64
GLA_SUB = 16
FOX_HEADS = 8
FOX_DH = 128
FOX_AUG = 128
S5_GROUP_CH = 16
S5_STATE = 64
S5_T = 16
S5_TAB = 32

VMEM_LIMIT = 56 * 1024 * 1024

NT_DIMS = (((1,), (1,)), ((), ()))
LOG2E = math.log2(math.e)


def _cparams(sem):
    return pltpu.CompilerParams(dimension_semantics=sem, vmem_limit_bytes=VMEM_LIMIT)


def _dot(a, b):
    return jnp.dot(a, b, preferred_element_type=F32)


def _dot_nt(a, b):
    return lax.dot_general(a, b, NT_DIMS, preferred_element_type=F32)


def _log_sigmoid(z):
    return -(jnp.maximum(-z, 0.0) + jnp.log1p(jnp.exp(-jnp.abs(z))))


def _sigmoid(z):
    return 1.0 / (1.0 + jnp.exp(-z))


def _silu(z):
    return z * _sigmoid(z)


def _split3(x):
    hi = x.astype(BF16)
    r1 = x - hi.astype(F32)
    mid = r1.astype(BF16)
    lo = (r1 - mid.astype(F32)).astype(BF16)
    return hi, mid, lo


def _rmsnorm_body(x_ref, g_ref, o_ref):
    x = x_ref[...]
    ms = jnp.mean(x * x, axis=-1, keepdims=True)
    o_ref[...] = (x * lax.rsqrt(ms + EPS) * g_ref[...]).astype(o_ref.dtype)


def rmsnorm_bf16(x, g, tm=512):
    m, d = x.shape
    return pl.pallas_call(
        _rmsnorm_body,
        grid=(m // tm,),
        in_specs=[pl.BlockSpec((tm, d), lambda i: (i, 0)),
                  pl.BlockSpec((1, d), lambda i: (0, 0))],
        out_specs=pl.BlockSpec((tm, d), lambda i: (i, 0)),
        out_shape=jax.ShapeDtypeStruct((m, d), BF16),
        compiler_params=_cparams(("parallel",)),
        name="rmsnorm",
    )(x, g.reshape(1, d))


def _mm_body(a_ref, w_ref, o_ref):
    o_ref[...] = _dot(a_ref[...], w_ref[...]).astype(o_ref.dtype)


def matmul(a, w, out_dtype, tm=512, tn=512, name="matmul"):
    m, k = a.shape
    n = w.shape[1]
    tn = min(tn, n)
    return pl.pallas_call(
        _mm_body,
        grid=(m // tm, n // tn),
        in_specs=[pl.BlockSpec((tm, k), lambda i, j: (i, 0)),
                  pl.BlockSpec((k, tn), lambda i, j: (0, j))],
        out_specs=pl.BlockSpec((tm, tn), lambda i, j: (i, j)),
        out_shape=jax.ShapeDtypeStruct((m, n), out_dtype),
        compiler_params=_cparams(("parallel", "parallel")),
        name=name,
    )(a, w)


def _mm_nt_body(wt_ref, a_ref, o_ref):
    o_ref[...] = _dot_nt(wt_ref[...], a_ref[...]).astype(o_ref.dtype)


def matmul_t(wt, a, out_dtype, tm=512, tn=512, name="matmul_t"):
    n, k = wt.shape
    m = a.shape[0]
    tn = min(tn, n)
    return pl.pallas_call(
        _mm_nt_body,
        grid=(m // tm, n // tn),
        in_specs=[pl.BlockSpec((tn, k), lambda i, j: (j, 0)),
                  pl.BlockSpec((tm, k), lambda i, j: (i, 0))],
        out_specs=pl.BlockSpec((tn, tm), lambda i, j: (j, i)),
        out_shape=jax.ShapeDtypeStruct((n, m), out_dtype),
        compiler_params=_cparams(("parallel", "parallel")),
        name=name,
    )(wt, a)


def matmul_t_grouped(wt, a2, steps, out_dtype, tm=512, tn=512, name="matmul_tg"):
    nf, k = wt.shape
    n = a2.shape[0]
    return pl.pallas_call(
        _mm_nt_body,
        grid=(nf // tn, steps, n // tm),
        in_specs=[pl.BlockSpec((tn, k), lambda j, t, i: (j, 0)),
                  pl.BlockSpec((tm, k), lambda j, t, i: (i, t))],
        out_specs=pl.BlockSpec((None, tn, tm), lambda j, t, i: (t, j, i)),
        out_shape=jax.ShapeDtypeStruct((steps, nf, n), out_dtype),
        compiler_params=_cparams(("parallel", "parallel", "parallel")),
        name=name,
    )(wt, a2)


def _gla_body(q_ref, k_ref, vt_ref, glr_ref, aup_ref, ab_ref, gate_ref, hg_ref, o_ref,
              st_ref, b_ref, kf_ref, vts_ref, *, tb):
    ib = pl.program_id(2)
    C, SB = GLA_CHUNK, GLA_SUB

    @pl.when(ib == 0)
    def _():
        st_ref[...] = jnp.zeros_like(st_ref)

    z = _dot(glr_ref[:, :GLA_LOWRANK].astype(BF16), aup_ref[...].astype(BF16)) + ab_ref[...]
    la = _log_sigmoid(z) * (1.0 / GLA_TAU)
    r = lax.broadcasted_iota(jnp.int32, (tb, tb), 0)
    c = lax.broadcasted_iota(jnp.int32, (tb, tb), 1)
    tri = jnp.where((r // C == c // C) & (c <= r), 1.0, 0.0).astype(BF16)
    hi, mid, lo = _split3(la)
    b_ref[...] = _dot(tri, hi) + _dot(tri, mid) + _dot(tri, lo)
    kf_ref[...] = k_ref[...].astype(F32)
    for ci in range(tb // C):
        vts_ref[ci] = vt_ref[:, ci * C:(ci + 1) * C]

    row = lax.broadcasted_iota(jnp.int32, (C, 1), 0)
    blk = row // SB
    rowi = lax.broadcasted_iota(jnp.int32, (C, C), 0) // SB
    coli = lax.broadcasted_iota(jnp.int32, (C, C), 1) // SB
    mask_b = (rowi % 2 == 1) & (coli == rowi - 1)
    lane_c = lax.broadcasted_iota(jnp.int32, (SB, C), 1)
    sub_r = lax.broadcasted_iota(jnp.int32, (SB, 1), 0)
    ones = jnp.ones((GLA_DK, C), BF16)
    neg = -jnp.inf
    scale = GLA_DK ** -0.5

    def chunk(ci, carry):
        r0 = pl.multiple_of(ci * C, C)
        q = q_ref[pl.ds(r0, C), :].astype(F32) * scale
        k = kf_ref[pl.ds(r0, C), :]
        b = b_ref[pl.ds(r0, C), :]
        vt = vts_ref[ci]
        b_last = b[C - 1:C, :]

        st = st_ref[...]
        o = _dot_nt((q * jnp.exp(b)).astype(BF16), st.astype(BF16))
        k_out = (k * jnp.exp(b_last - b)).astype(BF16)
        st_ref[...] = st * jnp.exp(b_last) + _dot(vt, k_out)

        r_a = b[2 * SB - 1:2 * SB, :]
        q_a = q * jnp.exp(jnp.where(blk >= 2, b - r_a, neg))
        k_a = k * jnp.exp(jnp.where(blk < 2, r_a - b, neg))
        r_b = jnp.where(blk < 2, b[SB - 1:SB, :], b[3 * SB - 1:3 * SB, :])
        q_b = q * jnp.exp(jnp.where(blk % 2 == 1, b - r_b, neg))
        k_b = k * jnp.exp(jnp.where(blk % 2 == 0, r_b - b, neg))
        attn = _dot_nt(q_a.astype(BF16), k_a.astype(BF16))
        attn = attn + jnp.where(mask_b, _dot_nt(q_b.astype(BF16), k_b.astype(BF16)), 0.0)

        zs = []
        for bi in range(C // SB):
            s0 = bi * SB
            q_i = q[s0:s0 + SB, :]
            b_i = b[s0:s0 + SB, :]
            for j in range(SB):
                k_j = kf_ref[pl.ds(r0 + (s0 + j), 1), :]
                b_j = b_ref[pl.ds(r0 + (s0 + j), 1), :]
                zs.append(q_i * k_j * jnp.exp(jnp.where(sub_r >= j, b_i - b_j, neg)))
        zsum = _dot(jnp.concatenate(zs, axis=0).astype(BF16), ones)
        diag = []
        for bi in range(C // SB):
            acc = jnp.zeros((SB, C), F32)
            for j in range(SB):
                n0 = (bi * SB + j) * SB
                acc = acc + jnp.where(lane_c == bi * SB + j, zsum[n0:n0 + SB, :], 0.0)
            diag.append(acc)
        attn = attn + jnp.concatenate(diag, axis=0)
        o = o + _dot_nt(attn.astype(BF16), vt)

        o = o * lax.rsqrt(jnp.mean(o * o, axis=-1, keepdims=True) + EPS) * hg_ref[...]
        g = gate_ref[pl.ds(r0, C), :].astype(F32)
        o_ref[pl.ds(r0, C), :] = (o * _silu(g)).astype(o_ref.dtype)
        return carry

    lax.fori_loop(0, tb // C, chunk, 0, unroll=2)


def gla(proj, vt, small, alpha_up, alpha_b, head_g, *, batch, seq, q_col, k_col, gate_col, tb=512):
    t = batch * seq
    nb = seq // tb
    dk, dv, h = GLA_DK, GLA_DV, GLA_HEADS
    qb, kb, gb = q_col // dk, k_col // dk, gate_col // dv
    tok = lambda b, hh, i: b * nb + i
    return pl.pallas_call(
        functools.partial(_gla_body, tb=tb),
        grid=(batch, h, nb),
        in_specs=[
            pl.BlockSpec((tb, dk), lambda b, hh, i: (tok(b, hh, i), qb + hh)),
            pl.BlockSpec((tb, dk), lambda b, hh, i: (tok(b, hh, i), kb + hh)),
            pl.BlockSpec((dv, tb), lambda b, hh, i: (hh, tok(b, hh, i))),
            pl.BlockSpec((tb, 128), lambda b, hh, i: (tok(b, hh, i), 0)),
            pl.BlockSpec((GLA_LOWRANK, dk), lambda b, hh, i: (0, hh)),
            pl.BlockSpec((1, dk), lambda b, hh, i: (0, hh)),
            pl.BlockSpec((tb, dv), lambda b, hh, i: (tok(b, hh, i), gb + hh)),
            pl.BlockSpec((1, dv), lambda b, hh, i: (0, hh)),
        ],
        out_specs=pl.BlockSpec((tb, dv), lambda b, hh, i: (tok(b, hh, i), hh)),
        out_shape=jax.ShapeDtypeStruct((t, h * dv), BF16),
        scratch_shapes=[pltpu.VMEM((dv, dk), F32), pltpu.VMEM((tb, dk), F32), pltpu.VMEM((tb, dk), F32),
                        pltpu.VMEM((tb // GLA_CHUNK, dv, GLA_CHUNK), BF16)],
        compiler_params=_cparams(("parallel", "parallel", "arbitrary")),
        name="gla",
    )(proj, proj, vt, small, alpha_up, alpha_b.reshape(1, -1), proj, head_g.reshape(1, -1))


def _fox_gate_body(s_ref, fb_ref, qa_ref, ka_ref, *, blk, col0):
    n = s_ref.shape[0] // blk
    r = lax.broadcasted_iota(jnp.int32, (blk, blk), 0)
    c = lax.broadcasted_iota(jnp.int32, (blk, blk), 1)
    tri = jnp.where(c <= r, 1.0, 0.0).astype(BF16)
    lane = lax.broadcasted_iota(jnp.int32, (blk, FOX_AUG), 1)

    def step(i, carry):
        r0 = pl.multiple_of(i * blk, blk)
        lf = _log_sigmoid(s_ref[pl.ds(r0, blk), :] + fb_ref[...])
        hi, mid, lo = _split3(lf)
        cs = _dot(tri, hi) + _dot(tri, mid) + _dot(tri, lo) + carry
        for h in range(FOX_HEADS):
            col = cs[:, col0 + h:col0 + h + 1] * LOG2E
            c0, c1, c2 = (p.astype(F32) for p in _split3(col))
            qa = jnp.where(lane == 0, c0, jnp.where(lane == 1, c1, jnp.where(lane == 2, c2,
                           jnp.where(lane < 6, 1.0, 0.0))))
            ka = jnp.where(lane < 3, 1.0, jnp.where(lane == 3, -c0, jnp.where(lane == 4, -c1,
                           jnp.where(lane == 5, -c2, 0.0))))
            qa_ref[h, pl.ds(r0, blk), :] = qa.astype(BF16)
            ka_ref[h, pl.ds(r0, blk), :] = ka.astype(BF16)
        return cs[blk - 1:blk, :]

    lax.fori_loop(0, n, step, jnp.zeros((1, s_ref.shape[1]), F32))


def fox_gate(small, fb_row, *, batch, seq, col0, blk=128):
    shp = jax.ShapeDtypeStruct((FOX_HEADS, batch * seq, FOX_AUG), BF16)
    spec = pl.BlockSpec((FOX_HEADS, seq, FOX_AUG), lambda b: (0, b, 0))
    return pl.pallas_call(
        functools.partial(_fox_gate_body, blk=blk, col0=col0),
        grid=(batch,),
        in_specs=[pl.BlockSpec((seq, 128), lambda b: (b, 0)),
                  pl.BlockSpec((1, 128), lambda b: (0, 0))],
        out_specs=[spec, spec],
        out_shape=[shp, shp],
        compiler_params=_cparams(("parallel",)),
        name="fox_gate",
    )(small, fb_row)


def _fox_body(it_ref, jt_ref, q_ref, qa_ref, k_ref, ka_ref, vt_ref, gate_ref, o_ref,
              m_ref, l_ref, acc_ref, *, tq, tk, hpb):
    p = pl.program_id(2)
    i = it_ref[p]
    j = jt_ref[p]
    d = FOX_DH

    @pl.when(j == 0)
    def _():
        m_ref[...] = jnp.full_like(m_ref, -jnp.inf)
        l_ref[...] = jnp.zeros_like(l_ref)
        acc_ref[...] = jnp.zeros_like(acc_ref)

    def update(hh, masked):
        q_aug = jnp.concatenate([q_ref[:, hh * d:(hh + 1) * d], qa_ref[hh]], axis=1)
        k_aug = jnp.concatenate([k_ref[:, hh * d:(hh + 1) * d], ka_ref[hh]], axis=1)
        st = _dot_nt(k_aug, q_aug)
        if masked:
            kr = lax.broadcasted_iota(jnp.int32, (tk, tq), 0)
            qc = lax.broadcasted_iota(jnp.int32, (tk, tq), 1)
            st = jnp.where(qc >= kr, st, -jnp.inf)
        m_old = m_ref[hh]
        m_new = jnp.maximum(m_old, jnp.max(st, axis=0, keepdims=True))
        alpha = jnp.exp2(m_old - m_new)
        pt = jnp.exp2(st - m_new)
        l_ref[hh] = alpha * l_ref[hh] + jnp.sum(pt, axis=0, keepdims=True)
        acc_ref[hh] = alpha * acc_ref[hh] + _dot(vt_ref[hh * d:(hh + 1) * d, :], pt.astype(BF16))
        m_ref[hh] = m_new

    @pl.when(j < i)
    def _():
        for hh in range(hpb):
            update(hh, False)

    @pl.when(j == i)
    def _():
        for hh in range(hpb):
            update(hh, True)
            o = (acc_ref[hh] / l_ref[hh]).T
            g = gate_ref[:, hh * d:(hh + 1) * d].astype(F32)
            o_ref[:, hh * d:(hh + 1) * d] = (o * _silu(g)).astype(o_ref.dtype)


def fox(proj, vt, qa, ka, *, batch, seq, q_col, k_col, gate_col, vt_row, t=512, hpb=2):
    tt = batch * seq
    nb = seq // t
    d, h = FOX_DH, FOX_HEADS
    w = hpb * d
    qb, kb, gb, vb = q_col // w, k_col // w, gate_col // w, vt_row // w
    pairs = [(i, j) for i in range(nb) for j in range(i + 1)]
    it = jnp.array([p[0] for p in pairs], jnp.int32)
    jt = jnp.array([p[1] for p in pairs], jnp.int32)
    grid_spec = pltpu.PrefetchScalarGridSpec(
        num_scalar_prefetch=2,
        grid=(batch, h // hpb, len(pairs)),
        in_specs=[
            pl.BlockSpec((t, w), lambda b, hh, p, it, jt: (b * nb + it[p], qb + hh)),
            pl.BlockSpec((hpb, t, FOX_AUG), lambda b, hh, p, it, jt: (hh, b * nb + it[p], 0)),
            pl.BlockSpec((t, w), lambda b, hh, p, it, jt: (b * nb + jt[p], kb + hh)),
            pl.BlockSpec((hpb, t, FOX_AUG), lambda b, hh, p, it, jt: (hh, b * nb + jt[p], 0)),
            pl.BlockSpec((w, t), lambda b, hh, p, it, jt: (vb + hh, b * nb + jt[p])),
            pl.BlockSpec((t, w), lambda b, hh, p, it, jt: (b * nb + it[p], gb + hh)),
        ],
        out_specs=pl.BlockSpec((t, w), lambda b, hh, p, it, jt: (b * nb + it[p], hh)),
        scratch_shapes=[pltpu.VMEM((hpb, 1, t), F32), pltpu.VMEM((hpb, 1, t), F32), pltpu.VMEM((hpb, d, t), F32)],
    )
    return pl.pallas_call(
        functools.partial(_fox_body, tq=t, tk=t, hpb=hpb),
        grid_spec=grid_spec,
        out_shape=jax.ShapeDtypeStruct((tt, h * d), BF16),
        compiler_params=_cparams(("parallel", "parallel", "arbitrary")),
        name="fox",
    )(it, jt, proj, qa, proj, ka, vt, proj)


def _outproj_mid_body(a1_ref, a2_ref, w_ref, x_ref, g_ref, r_ref, h_ref, n_ref, *, steps):
    k1 = a1_ref.shape[1]
    acc = x_ref[...] + _dot(a1_ref[...], w_ref[:k1, :]) + _dot(a2_ref[...], w_ref[k1:, :])
    r_ref[...] = acc
    normed = acc * lax.rsqrt(jnp.mean(acc * acc, axis=-1, keepdims=True) + EPS) * g_ref[...]
    nl, tm, lanes = n_ref.shape
    d = nl * lanes
    for j in range(nl):
        n_ref[j] = normed[:, j * lanes:(j + 1) * lanes]
    for s in range(steps):
        for j in range(nl):
            c0 = s * d + j * lanes
            h_ref[:, c0:c0 + lanes] = n_ref[j, pl.ds(s, tm // steps, stride=steps), :].astype(h_ref.dtype)


def outproj_mid(a1, a2, w, x, g, *, steps, tm=256, name="outproj_mid"):
    m, d = x.shape
    row = pl.BlockSpec((tm, d), lambda i: (i, 0))
    return pl.pallas_call(
        functools.partial(_outproj_mid_body, steps=steps),
        grid=(m // tm,),
        in_specs=[pl.BlockSpec((tm, a1.shape[1]), lambda i: (i, 0)),
                  pl.BlockSpec((tm, a2.shape[1]), lambda i: (i, 0)),
                  pl.BlockSpec(w.shape, lambda i: (0, 0)),
                  row,
                  pl.BlockSpec((1, d), lambda i: (0, 0))],
        out_specs=[row, pl.BlockSpec((tm // steps, steps * d), lambda i: (i, 0))],
        out_shape=[jax.ShapeDtypeStruct((m, d), F32), jax.ShapeDtypeStruct((m // steps, steps * d), BF16)],
        scratch_shapes=[pltpu.VMEM((d // 128, tm, 128), F32)],
        compiler_params=_cparams(("parallel",)),
        name=name,
    )(a1, a2, w, x, g.reshape(1, d))


def _outproj_final_body(a_ref, w_ref, x_ref, g_ref, o_ref, a_tok_ref, *, steps):
    nl, tm, lanes = a_tok_ref.shape
    e = nl * lanes
    for s in range(steps):
        for j in range(nl):
            c0 = s * e + j * lanes
            a_tok_ref[j, pl.ds(s, tm // steps, stride=steps), :] = a_ref[:, c0:c0 + lanes].astype(F32)
    a_tok = jnp.concatenate([a_tok_ref[j] for j in range(nl)], axis=1).astype(BF16)
    acc = x_ref[...] + _dot(a_tok, w_ref[...])
    o_ref[...] = acc * lax.rsqrt(jnp.mean(acc * acc, axis=-1, keepdims=True) + EPS) * g_ref[...]


def outproj_final(a_grouped, w, x, g, *, steps, tm=256, name="outproj_final"):
    m, d = x.shape
    e = w.shape[0]
    row = pl.BlockSpec((tm, d), lambda i: (i, 0))
    return pl.pallas_call(
        functools.partial(_outproj_final_body, steps=steps),
        grid=(m // tm,),
        in_specs=[pl.BlockSpec((tm // steps, steps * e), lambda i: (i, 0)),
                  pl.BlockSpec(w.shape, lambda i: (0, 0)),
                  row,
                  pl.BlockSpec((1, d), lambda i: (0, 0))],
        out_specs=row,
        out_shape=jax.ShapeDtypeStruct((m, d), F32),
        scratch_shapes=[pltpu.VMEM((e // 128, tm, 128), F32)],
        compiler_params=_cparams(("parallel",)),
        name=name,
    )(a_grouped, w, x, g.reshape(1, d))


def _s5_body(u_ref, wt_ref, nt_ref, mt_ref, tab_ref, o_ref, *, cols_per_seq, n_levels):
    t, cg, n = u_ref.shape
    u = u_ref[...].reshape(t * cg, n)
    tab = tab_ref[0]
    half = S5_STATE
    x = _dot(nt_ref[0], u)
    pos = lax.broadcasted_iota(jnp.int32, (1, n), 1) % cols_per_seq
    for lvl in range(n_levels):
        sh = 1 << lvl
        xs = jnp.where(pos >= sh, pltpu.roll(x, sh, axis=1), 0.0)
        a_a = tab[:2 * half, 1 + lvl:2 + lvl]
        a_b = tab[:2 * half, 1 + n_levels + lvl:2 + n_levels + lvl]
        x = x + a_a * xs + a_b * pltpu.roll(xs, half, axis=0)
    x_prev = jnp.where(pos >= 1, pltpu.roll(x, 1, axis=1), 0.0)
    y = _dot(wt_ref[0], u) + _dot(mt_ref[0], x_prev.astype(BF16)) + tab[:, 0:1] * u.astype(F32)
    o_ref[...] = y.astype(o_ref.dtype).reshape(t, cg, n)


def s5_apply(proj_t, wt_g, nt_g, mt_g, tab, *, groups, cols_per_seq, n_levels):
    t, _, n = proj_t.shape
    cg = S5_GROUP_CH
    spec3 = lambda shp: pl.BlockSpec((1,) + shp, lambda i: (i, 0, 0))
    return pl.pallas_call(
        functools.partial(_s5_body, cols_per_seq=cols_per_seq, n_levels=n_levels),
        grid=(groups,),
        in_specs=[pl.BlockSpec((t, cg, n), lambda i: (0, i, 0)),
                  spec3(wt_g.shape[1:]), spec3(nt_g.shape[1:]), spec3(mt_g.shape[1:]), spec3(tab.shape[1:])],
        out_specs=pl.BlockSpec((t, cg, n), lambda i: (0, i, 0)),
        out_shape=jax.ShapeDtypeStruct((t, groups * cg, n), BF16),
        compiler_params=_cparams(("parallel",)),
        name="s5",
    )(proj_t, wt_g, nt_g, mt_g, tab)


def s5_operators(lam_re, lam_im, log_step, b_re, b_im, c_re, c_im, d, *, n_levels):
    hp = lax.Precision.HIGHEST
    g, p = lam_re.shape
    cg, t = S5_GROUP_CH, S5_T
    lr = jnp.minimum(lam_re.astype(F32), -1e-4)
    li = lam_im.astype(F32)
    step = jnp.exp(log_step.astype(F32))[:, None]
    mag = jnp.exp(lr * step)
    lb_re, lb_im = mag * jnp.cos(li * step), mag * jnp.sin(li * step)
    den = lr * lr + li * li
    nr, ni = lb_re - 1.0, lb_im
    coef_re = (nr * lr + ni * li) / den
    coef_im = (ni * lr - nr * li) / den
    bb_re = coef_re[..., None] * b_re - coef_im[..., None] * b_im
    bb_im = coef_re[..., None] * b_im + coef_im[..., None] * b_re

    def power(m):
        mm = m.astype(F32)[None, :, None]
        mg = jnp.exp(lr[:, None, :] * step[:, None, :] * mm)
        ang = li[:, None, :] * step[:, None, :] * mm
        return mg * jnp.cos(ang), mg * jnp.sin(ang)

    pw_re, pw_im = power(jnp.arange(t + 1))
    cp_re = c_re[:, None] * pw_re[:, :, None, :] - c_im[:, None] * pw_im[:, :, None, :]
    cp_im = c_re[:, None] * pw_im[:, :, None, :] + c_im[:, None] * pw_re[:, :, None, :]
    taps = (jnp.einsum('gxcp,gpd->gxcd', cp_re[:, :t], bb_re, precision=hp)
            - jnp.einsum('gxcp,gpd->gxcd', cp_im[:, :t], bb_im, precision=hp))
    shift = (jnp.arange(t)[None, None, :] - jnp.arange(t)[None, :, None]
             == jnp.arange(t)[:, None, None]).astype(F32)
    wt_g = jnp.einsum('xst,gxcd->gtcsd', shift, taps, precision=hp).reshape(g, t * cg, t * cg)
    pr, pi = pw_re[:, t - 1::-1], pw_im[:, t - 1::-1]
    bt_re, bt_im = bb_re[:, :, None, :], bb_im[:, :, None, :]
    prt, pit = pr.transpose(0, 2, 1)[:, :, :, None], pi.transpose(0, 2, 1)[:, :, :, None]
    n_re = (prt * bt_re - pit * bt_im).reshape(g, p, t * cg)
    n_im = (prt * bt_im + pit * bt_re).reshape(g, p, t * cg)
    nt_g = jnp.concatenate([n_re, n_im], axis=1)
    m_re = cp_re[:, 1:].reshape(g, t * cg, p)
    m_im = cp_im[:, 1:].reshape(g, t * cg, p)
    mt_g = jnp.concatenate([m_re, -m_im], axis=2)
    ar, ai = power(t * (2 ** jnp.arange(n_levels)))
    a_a = jnp.concatenate([ar, ar], axis=-1).transpose(0, 2, 1)
    a_b = jnp.concatenate([-ai, ai], axis=-1).transpose(0, 2, 1)
    d_col = jnp.tile(d.astype(F32).reshape(g, cg), (1, t))[:, :, None]
    rows = cg * t
    padr = lambda a: jnp.pad(a, ((0, 0), (0, rows - a.shape[1]), (0, 0)))
    tab = jnp.concatenate([d_col, padr(a_a), padr(a_b)], axis=2)
    tab = jnp.pad(tab, ((0, 0), (0, 0), (0, S5_TAB - tab.shape[2])))
    return wt_g.astype(BF16), nt_g.astype(BF16), mt_g.astype(BF16), tab


def _gelu_tanh(y):
    return 0.5 * y * (1.0 + jnp.tanh(math.sqrt(2.0 / math.pi) * (y + 0.044715 * (y * y * y))))


def _glu_body(y_ref, w_ref, b_ref, gate_ref, o_ref, z_ref, zb_ref, *, tf):
    j = pl.program_id(2)

    @pl.when(j == 0)
    def _():
        z = _gelu_tanh(y_ref[...].astype(F32))
        z_ref[...] = z
        zb_ref[...] = z.astype(BF16)

    f0 = pl.multiple_of(j * tf, tf)
    z = z_ref[pl.ds(f0, tf), :]
    lin = _dot(w_ref[...], zb_ref[...]) + b_ref[...]
    out = z * _sigmoid(lin) * _silu(gate_ref[...].astype(F32))
    o_ref[...] = out.T.astype(o_ref.dtype)


def glu_gate(y_t, w_t, b, proj_t, *, gate_row, tm=512, tf=512):
    s, e, n = y_t.shape
    gb = gate_row // tf
    nf = e // tf
    return pl.pallas_call(
        functools.partial(_glu_body, tf=tf),
        grid=(s, n // tm, nf),
        in_specs=[pl.BlockSpec((None, e, tm), lambda t, i, j: (t, 0, i)),
                  pl.BlockSpec((tf, e), lambda t, i, j: (j, 0)),
                  pl.BlockSpec((tf, 1), lambda t, i, j: (j, 0)),
                  pl.BlockSpec((None, tf, tm), lambda t, i, j: (t, gb + j, i))],
        out_specs=pl.BlockSpec((tm, tf), lambda t, i, j: (i, t * nf + j)),
        out_shape=jax.ShapeDtypeStruct((n, s * e), BF16),
        scratch_shapes=[pltpu.VMEM((e, tm), F32), pltpu.VMEM((e, tm), BF16)],
        compiler_params=_cparams(("parallel", "parallel", "arbitrary")),
        name="glu",
    )(y_t, w_t, b.reshape(e, 1), proj_t)


def kernel(x, ab_norm_g, ab_w_in, gla_alpha_up, gla_alpha_b, gla_head_g, fox_f_b, ab_w_out, c_norm_g, c_w_in, s5_lambda_re, s5_lambda_im, s5_log_step, s5_b_re, s5_b_im, s5_c_re, s5_c_im, s5_d, glu_w, glu_b, c_w_out, final_norm_g):
    batch, seq, d = x.shape
    t = batch * seq
    x2 = x.reshape(t, d)

    w_in = ab_w_in[0]
    hk, hv, fw = GLA_HEADS * GLA_DK, GLA_HEADS * GLA_DV, FOX_HEADS * FOX_DH
    sizes = (hk, hk, hv, GLA_LOWRANK, hv, fw, fw, fw, FOX_HEADS, fw)
    offs = [0]
    for s in sizes:
        offs.append(offs[-1] + s)
    seg = lambda n: w_in[:, offs[n]:offs[n + 1]]
    w_main = jnp.concatenate([seg(0), seg(1), seg(4), seg(5) * (FOX_DH ** -0.5 * LOG2E), seg(6), seg(9)],
                             axis=1).astype(BF16)
    c_gq, c_gk, c_gg = 0, hk, 2 * hk
    c_fq = c_gg + hv
    c_fk, c_fg = c_fq + fw, c_fq + 2 * fw
    w_vt = jnp.concatenate([seg(2), seg(7)], axis=1).T.astype(BF16)
    pad = 128 - GLA_LOWRANK - FOX_HEADS
    w_small = jnp.concatenate([seg(3), seg(8), jnp.zeros((d, pad), F32)], axis=1).astype(BF16)

    h0 = rmsnorm_bf16(x2, ab_norm_g[0])
    proj = matmul(h0, w_main, BF16, name="ab_in")
    vt = matmul_t(w_vt, h0, BF16, name="ab_in_vt")
    small = matmul(h0, w_small, F32, name="ab_in_small")

    o_gla = gla(proj, vt, small, gla_alpha_up[0], gla_alpha_b[0], gla_head_g[0],
                batch=batch, seq=seq, q_col=c_gq, k_col=c_gk, gate_col=c_gg)

    fb_row = jnp.zeros((1, 128), F32).at[0, GLA_LOWRANK:GLA_LOWRANK + FOX_HEADS].set(fox_f_b[0])
    qa, ka = fox_gate(small, fb_row, batch=batch, seq=seq, col0=GLA_LOWRANK)
    o_fox = fox(proj, vt, qa, ka, batch=batch, seq=seq, q_col=c_fq, k_col=c_fk, gate_col=c_fg, vt_row=hv)

    x1, h1g = outproj_mid(o_gla, o_fox, ab_w_out[0].astype(BF16), x2, c_norm_g[0], steps=S5_T, name="ab_out")

    e = d
    groups = e // S5_GROUP_CH
    n_chunks = t // S5_T
    cols_per_seq = seq // S5_T
    n_levels = max(1, (cols_per_seq - 1).bit_length())
    proj_t = matmul_t_grouped(c_w_in[0].T.astype(BF16), h1g, S5_T, BF16, name="c_in")
    ops = s5_operators(s5_lambda_re[0], s5_lambda_im[0], s5_log_step[0], s5_b_re[0], s5_b_im[0],
                       s5_c_re[0], s5_c_im[0], s5_d[0], n_levels=n_levels)
    y_t = s5_apply(proj_t, *ops, groups=groups, cols_per_seq=cols_per_seq, n_levels=n_levels)
    zz = glu_gate(y_t, glu_w[0].T.astype(BF16), glu_b[0], proj_t, gate_row=e)
    out = outproj_final(zz, c_w_out[0].astype(BF16), x1, final_norm_g, steps=S5_T, name="c_out")
    return out.reshape(batch, seq, d)
```

```python
import functools
import math

import jax
import jax.numpy as jnp
from jax import lax
from jax.experimental import pallas as pl
from jax.experimental.pallas import tpu as pltpu

EPS = 1e-6
F32 = jnp.float32
BF16 = jnp.bfloat16

GLA_HEADS = 4
GLA_DK = 128
GLA_DV = 256
GLA_LOWRANK = 16
GLA_TAU = 16.0
GLA_CHUNK = 64
GLA_SUB = 16
FOX_HEADS = 8
FOX_DH = 128
FOX_AUG = 128
S5_GROUP_CH = 16
S5_STATE = 64
S5_T = 16
S5_TAB = 32

VMEM_LIMIT = 56 * 1024 * 1024

NT_DIMS = (((1,), (1,)), ((), ()))
LOG2E = math.log2(math.e)


def _cparams(sem):
    return pltpu.CompilerParams(dimension_semantics=sem, vmem_limit_bytes=VMEM_LIMIT)


def _dot(a, b):
    return jnp.dot(a, b, preferred_element_type=F32)


def _dot_nt(a, b):
    return lax.dot_general(a, b, NT_DIMS, preferred_element_type=F32)


def _log_sigmoid(z):
    return -(jnp.maximum(-z, 0.0) + jnp.log1p(jnp.exp(-jnp.abs(z))))


def _sigmoid(z):
    return 1.0 / (1.0 + jnp.exp(-z))


def _silu(z):
    return z * _sigmoid(z)


def _split3(x):
    hi = x.astype(BF16)
    r1 = x - hi.astype(F32)
    mid = r1.astype(BF16)
    lo = (r1 - mid.astype(F32)).astype(BF16)
    return hi, mid, lo


def _rmsnorm_body(x_ref, g_ref, o_ref):
    x = x_ref[...]
    ms = jnp.mean(x * x, axis=-1, keepdims=True)
    o_ref[...] = (x * lax.rsqrt(ms + EPS) * g_ref[...]).astype(o_ref.dtype)


def rmsnorm_bf16(x, g, tm=512):
    m, d = x.shape
    return pl.pallas_call(
        _rmsnorm_body,
        grid=(m // tm,),
        in_specs=[pl.BlockSpec((tm, d), lambda i: (i, 0)),
                  pl.BlockSpec((1, d), lambda i: (0, 0))],
        out_specs=pl.BlockSpec((tm, d), lambda i: (i, 0)),
        out_shape=jax.ShapeDtypeStruct((m, d), BF16),
        compiler_params=_cparams(("parallel",)),
        name="rmsnorm",
    )(x, g.reshape(1, d))


MM_SUB = 512


def _mm_body(a_ref, w_ref, o_ref):
    a = a_ref[...]
    for c0 in range(0, o_ref.shape[1], MM_SUB):
        o_ref[:, c0:c0 + MM_SUB] = _dot(a, w_ref[:, c0:c0 + MM_SUB]).astype(o_ref.dtype)


def matmul(a, w, out_dtype, tm=512, tn=512, name="matmul"):
    m, k = a.shape
    n = w.shape[1]
    return pl.pallas_call(
        _mm_body,
        grid=(n // tn, m // tm),
        in_specs=[pl.BlockSpec((tm, k), lambda j, i: (i, 0)),
                  pl.BlockSpec((k, tn), lambda j, i: (0, j))],
        out_specs=pl.BlockSpec((tm, tn), lambda j, i: (i, j)),
        out_shape=jax.ShapeDtypeStruct((m, n), out_dtype),
        compiler_params=_cparams(("parallel", "parallel")),
        name=name,
    )(a, w)


def _mm_nt_body(wt_ref, a_ref, o_ref):
    a = a_ref[...]
    for r0 in range(0, o_ref.shape[0], MM_SUB):
        o_ref[r0:r0 + MM_SUB, :] = _dot_nt(wt_ref[r0:r0 + MM_SUB, :], a).astype(o_ref.dtype)


def _mm_nt_small_body(wt_ref, ws_ref, a_ref, o_ref, s_ref):
    _mm_nt_body(wt_ref, a_ref, o_ref)
    s_ref[...] = _dot(a_ref[...], ws_ref[...])


def matmul_t_plus_small(wt, w_small, a, tm=512, name="matmul_t"):
    n, k = wt.shape
    m = a.shape[0]
    ns = w_small.shape[1]
    return pl.pallas_call(
        _mm_nt_small_body,
        grid=(m // tm,),
        in_specs=[pl.BlockSpec((n, k), lambda i: (0, 0)),
                  pl.BlockSpec((k, ns), lambda i: (0, 0)),
                  pl.BlockSpec((tm, k), lambda i: (i, 0))],
        out_specs=[pl.BlockSpec((n, tm), lambda i: (0, i)),
                   pl.BlockSpec((tm, ns), lambda i: (i, 0))],
        out_shape=[jax.ShapeDtypeStruct((n, m), BF16), jax.ShapeDtypeStruct((m, ns), F32)],
        compiler_params=_cparams(("parallel",)),
        name=name,
    )(wt, w_small, a)


def matmul_t_grouped(wt, a2, steps, out_dtype, tm=512, tn=2048, name="matmul_tg"):
    nf, k = wt.shape
    n = a2.shape[0]
    return pl.pallas_call(
        _mm_nt_body,
        grid=(nf // tn, steps, n // tm),
        in_specs=[pl.BlockSpec((tn, k), lambda j, t, i: (j, 0)),
                  pl.BlockSpec((tm, k), lambda j, t, i: (i, t))],
        out_specs=pl.BlockSpec((None, tn, tm), lambda j, t, i: (t, j, i)),
        out_shape=jax.ShapeDtypeStruct((steps, nf, n), out_dtype),
        compiler_params=_cparams(("parallel", "parallel", "parallel")),
        name=name,
    )(wt, a2)


def _gla_body(q_ref, k_ref, vt_ref, glr_ref, aup_ref, ab_ref, gate_ref, hg_ref, o_ref,
              st_ref, b_ref, kf_ref, vts_ref, *, tb):
    ib = pl.program_id(2)
    C, SB = GLA_CHUNK, GLA_SUB

    @pl.when(ib == 0)
    def _():
        st_ref[...] = jnp.zeros_like(st_ref)

    z = _dot(glr_ref[:, :GLA_LOWRANK].astype(BF16), aup_ref[...].astype(BF16)) + ab_ref[...]
    la = _log_sigmoid(z) * (1.0 / GLA_TAU)
    r = lax.broadcasted_iota(jnp.int32, (tb, tb), 0)
    c = lax.broadcasted_iota(jnp.int32, (tb, tb), 1)
    tri = jnp.where((r // C == c // C) & (c <= r), 1.0, 0.0).astype(BF16)
    hi, mid, lo = _split3(la)
    b_ref[...] = _dot(tri, hi) + _dot(tri, mid) + _dot(tri, lo)
    kf_ref[...] = k_ref[...].astype(F32)
    for ci in range(tb // C):
        vts_ref[ci] = vt_ref[:, ci * C:(ci + 1) * C]

    row = lax.broadcasted_iota(jnp.int32, (C, 1), 0)
    blk = row // SB
    rowi = lax.broadcasted_iota(jnp.int32, (C, C), 0) // SB
    coli = lax.broadcasted_iota(jnp.int32, (C, C), 1) // SB
    mask_b = (rowi % 2 == 1) & (coli == rowi - 1)
    lane_c = lax.broadcasted_iota(jnp.int32, (SB, C), 1)
    sub_r = lax.broadcasted_iota(jnp.int32, (SB, 1), 0)
    ones = jnp.ones((GLA_DK, C), BF16)
    neg = -jnp.inf
    scale = GLA_DK ** -0.5

    def chunk(ci, carry):
        r0 = pl.multiple_of(ci * C, C)
        q = q_ref[pl.ds(r0, C), :].astype(F32) * scale
        k = kf_ref[pl.ds(r0, C), :]
        b = b_ref[pl.ds(r0, C), :]
        vt = vts_ref[ci]
        b_last = b[C - 1:C, :]

        st = st_ref[...]
        o = _dot_nt((q * jnp.exp(b)).astype(BF16), st.astype(BF16))
        k_out = (k * jnp.exp(b_last - b)).astype(BF16)
        st_ref[...] = st * jnp.exp(b_last) + _dot(vt, k_out)

        r_a = b[2 * SB - 1:2 * SB, :]
        q_a = q * jnp.exp(jnp.where(blk >= 2, b - r_a, neg))
        k_a = k * jnp.exp(jnp.where(blk < 2, r_a - b, neg))
        r_b = jnp.where(blk < 2, b[SB - 1:SB, :], b[3 * SB - 1:3 * SB, :])
        q_b = q * jnp.exp(jnp.where(blk % 2 == 1, b - r_b, neg))
        k_b = k * jnp.exp(jnp.where(blk % 2 == 0, r_b - b, neg))
        attn = _dot_nt(q_a.astype(BF16), k_a.astype(BF16))
        attn = attn + jnp.where(mask_b, _dot_nt(q_b.astype(BF16), k_b.astype(BF16)), 0.0)

        zs = []
        for bi in range(C // SB):
            s0 = bi * SB
            q_i = q[s0:s0 + SB, :]
            b_i = b[s0:s0 + SB, :]
            for j in range(SB):
                k_j = kf_ref[pl.ds(r0 + (s0 + j), 1), :]
                b_j = b_ref[pl.ds(r0 + (s0 + j), 1), :]
                zs.append(q_i * k_j * jnp.exp(jnp.where(sub_r >= j, b_i - b_j, neg)))
        zsum = _dot(jnp.concatenate(zs, axis=0).astype(BF16), ones)
        diag = []
        for bi in range(C // SB):
            acc = jnp.zeros((SB, C), F32)
            for j in range(SB):
                n0 = (bi * SB + j) * SB
                acc = acc + jnp.where(lane_c == bi * SB + j, zsum[n0:n0 + SB, :], 0.0)
            diag.append(acc)
        attn = attn + jnp.concatenate(diag, axis=0)
        o = o + _dot_nt(attn.astype(BF16), vt)

        o = o * lax.rsqrt(jnp.mean(o * o, axis=-1, keepdims=True) + EPS) * hg_ref[...]
        g = gate_ref[pl.ds(r0, C), :].astype(F32)
        o_ref[pl.ds(r0, C), :] = (o * _silu(g)).astype(o_ref.dtype)
        return carry

    lax.fori_loop(0, tb // C, chunk, 0, unroll=4)


def gla(proj, vt, small, alpha_up, alpha_b, head_g, *, batch, seq, q_col, k_col, gate_col, tb=512):
    t = batch * seq
    nb = seq // tb
    dk, dv, h = GLA_DK, GLA_DV, GLA_HEADS
    qb, kb, gb = q_col // dk, k_col // dk, gate_col // dv
    tok = lambda b, hh, i: b * nb + i
    return pl.pallas_call(
        functools.partial(_gla_body, tb=tb),
        grid=(batch, h, nb),
        in_specs=[
            pl.BlockSpec((tb, dk), lambda b, hh, i: (tok(b, hh, i), qb + hh)),
            pl.BlockSpec((tb, dk), lambda b, hh, i: (tok(b, hh, i), kb + hh)),
            pl.BlockSpec((dv, tb), lambda b, hh, i: (hh, tok(b, hh, i))),
            pl.BlockSpec((tb, 128), lambda b, hh, i: (tok(b, hh, i), 0)),
            pl.BlockSpec((GLA_LOWRANK, dk), lambda b, hh, i: (0, hh)),
            pl.BlockSpec((1, dk), lambda b, hh, i: (0, hh)),
            pl.BlockSpec((tb, dv), lambda b, hh, i: (tok(b, hh, i), gb + hh)),
            pl.BlockSpec((1, dv), lambda b, hh, i: (0, hh)),
        ],
        out_specs=pl.BlockSpec((tb, dv), lambda b, hh, i: (tok(b, hh, i), hh)),
        out_shape=jax.ShapeDtypeStruct((t, h * dv), BF16),
        scratch_shapes=[pltpu.VMEM((dv, dk), F32), pltpu.VMEM((tb, dk), F32), pltpu.VMEM((tb, dk), F32),
                        pltpu.VMEM((tb // GLA_CHUNK, dv, GLA_CHUNK), BF16)],
        compiler_params=_cparams(("parallel", "parallel", "arbitrary")),
        name="gla",
    )(proj, proj, vt, small, alpha_up, alpha_b.reshape(1, -1), proj, head_g.reshape(1, -1))


def _fox_gate_body(s_ref, fb_ref, qa_ref, ka_ref, *, blk, col0):
    n = s_ref.shape[0] // blk
    r = lax.broadcasted_iota(jnp.int32, (blk, blk), 0)
    c = lax.broadcasted_iota(jnp.int32, (blk, blk), 1)
    tri = jnp.where(c <= r, 1.0, 0.0).astype(BF16)
    lane = lax.broadcasted_iota(jnp.int32, (blk, FOX_AUG), 1)

    def step(i, carry):
        r0 = pl.multiple_of(i * blk, blk)
        lf = _log_sigmoid(s_ref[pl.ds(r0, blk), :] + fb_ref[...])
        hi, mid, lo = _split3(lf)
        cs = _dot(tri, hi) + _dot(tri, mid) + _dot(tri, lo) + carry
        for h in range(FOX_HEADS):
            col = cs[:, col0 + h:col0 + h + 1] * LOG2E
            c0, c1, c2 = (p.astype(F32) for p in _split3(col))
            qa = jnp.where(lane == 0, c0, jnp.where(lane == 1, c1, jnp.where(lane == 2, c2,
                           jnp.where(lane < 6, 1.0, 0.0))))
            ka = jnp.where(lane < 3, 1.0, jnp.where(lane == 3, -c0, jnp.where(lane == 4, -c1,
                           jnp.where(lane == 5, -c2, 0.0))))
            qa_ref[h, pl.ds(r0, blk), :] = qa.astype(BF16)
            ka_ref[h, pl.ds(r0, blk), :] = ka.astype(BF16)
        return cs[blk - 1:blk, :]

    lax.fori_loop(0, n, step, jnp.zeros((1, s_ref.shape[1]), F32))


def fox_gate(small, fb_row, *, batch, seq, col0, blk=256):
    shp = jax.ShapeDtypeStruct((FOX_HEADS, batch * seq, FOX_AUG), BF16)
    spec = pl.BlockSpec((FOX_HEADS, seq, FOX_AUG), lambda b: (0, b, 0))
    return pl.pallas_call(
        functools.partial(_fox_gate_body, blk=blk, col0=col0),
        grid=(batch,),
        in_specs=[pl.BlockSpec((seq, 128), lambda b: (b, 0)),
                  pl.BlockSpec((1, 128), lambda b: (0, 0))],
        out_specs=[spec, spec],
        out_shape=[shp, shp],
        compiler_params=_cparams(("parallel",)),
        name="fox_gate",
    )(small, fb_row)


def _fox_body(it_ref, jt_ref, q_ref, qa_ref, k_ref, ka_ref, vt_ref, gate_ref, o_ref,
              m_ref, l_ref, acc_ref, *, tq, tk, hpb):
    p = pl.program_id(2)
    i = it_ref[p]
    j = jt_ref[p]
    d = FOX_DH

    @pl.when(j == 0)
    def _():
        m_ref[...] = jnp.full_like(m_ref, -jnp.inf)
        l_ref[...] = jnp.zeros_like(l_ref)
        acc_ref[...] = jnp.zeros_like(acc_ref)

    def update(hh, masked):
        q_aug = jnp.concatenate([q_ref[:, hh * d:(hh + 1) * d], qa_ref[hh]], axis=1)
        k_aug = jnp.concatenate([k_ref[:, hh * d:(hh + 1) * d], ka_ref[hh]], axis=1)
        st = _dot_nt(k_aug, q_aug)
        if masked:
            kr = lax.broadcasted_iota(jnp.int32, (tk, tq), 0)
            qc = lax.broadcasted_iota(jnp.int32, (tk, tq), 1)
            st = jnp.where(qc >= kr, st, -jnp.inf)
        m_old = m_ref[hh]
        m_new = jnp.maximum(m_old, jnp.max(st, axis=0, keepdims=True))
        alpha = jnp.exp2(m_old - m_new)
        pt = jnp.exp2(st - m_new)
        l_ref[hh] = alpha * l_ref[hh] + jnp.sum(pt, axis=0, keepdims=True)
        acc_ref[hh] = alpha * acc_ref[hh] + _dot(vt_ref[hh * d:(hh + 1) * d, :], pt.astype(BF16))
        m_ref[hh] = m_new

    @pl.when(j < i)
    def _():
        for hh in range(hpb):
            update(hh, False)

    @pl.when(j == i)
    def _():
        for hh in range(hpb):
            update(hh, True)
            o = (acc_ref[hh] / l_ref[hh]).T
            g = gate_ref[:, hh * d:(hh + 1) * d].astype(F32)
            o_ref[:, hh * d:(hh + 1) * d] = (o * _silu(g)).astype(o_ref.dtype)


def fox(proj, vt, qa, ka, *, batch, seq, q_col, k_col, gate_col, vt_row, t=512, hpb=2):
    tt = batch * seq
    nb = seq // t
    d, h = FOX_DH, FOX_HEADS
    w = hpb * d
    qb, kb, gb, vb = q_col // w, k_col // w, gate_col // w, vt_row // w
    pairs = [(i, j) for i in range(nb) for j in range(i + 1)]
    it = jnp.array([p[0] for p in pairs], jnp.int32)
    jt = jnp.array([p[1] for p in pairs], jnp.int32)
    grid_spec = pltpu.PrefetchScalarGridSpec(
        num_scalar_prefetch=2,
        grid=(batch, h // hpb, len(pairs)),
        in_specs=[
            pl.BlockSpec((t, w), lambda b, hh, p, it, jt: (b * nb + it[p], qb + hh)),
            pl.BlockSpec((hpb, t, FOX_AUG), lambda b, hh, p, it, jt: (hh, b * nb + it[p], 0)),
            pl.BlockSpec((t, w), lambda b, hh, p, it, jt: (b * nb + jt[p], kb + hh)),
            pl.BlockSpec((hpb, t, FOX_AUG), lambda b, hh, p, it, jt: (hh, b * nb + jt[p], 0)),
            pl.BlockSpec((w, t), lambda b, hh, p, it, jt: (vb + hh, b * nb + jt[p])),
            pl.BlockSpec((t, w), lambda b, hh, p, it, jt: (b * nb + it[p], gb + hh)),
        ],
        out_specs=pl.BlockSpec((t, w), lambda b, hh, p, it, jt: (b * nb + it[p], hh)),
        scratch_shapes=[pltpu.VMEM((hpb, 1, t), F32), pltpu.VMEM((hpb, 1, t), F32), pltpu.VMEM((hpb, d, t), F32)],
    )
    return pl.pallas_call(
        functools.partial(_fox_body, tq=t, tk=t, hpb=hpb),
        grid_spec=grid_spec,
        out_shape=jax.ShapeDtypeStruct((tt, h * d), BF16),
        compiler_params=_cparams(("parallel", "parallel", "arbitrary")),
        name="fox",
    )(it, jt, proj, qa, proj, ka, vt, proj)


def _outproj_mid_body(a1_ref, a2_ref, w_ref, x_ref, g_ref, r_ref, h_ref, n_ref, *, steps):
    k1 = a1_ref.shape[1]
    acc = x_ref[...] + _dot(a1_ref[...], w_ref[:k1, :]) + _dot(a2_ref[...], w_ref[k1:, :])
    r_ref[...] = acc
    normed = acc * lax.rsqrt(jnp.mean(acc * acc, axis=-1, keepdims=True) + EPS) * g_ref[...]
    nl, tm, lanes = n_ref.shape
    d = nl * lanes
    for j in range(nl):
        n_ref[j] = normed[:, j * lanes:(j + 1) * lanes]
    for s in range(steps):
        for j in range(nl):
            c0 = s * d + j * lanes
            h_ref[:, c0:c0 + lanes] = n_ref[j, pl.ds(s, tm // steps, stride=steps), :].astype(h_ref.dtype)


def outproj_mid(a1, a2, w, x, g, *, steps, tm=256, name="outproj_mid"):
    m, d = x.shape
    row = pl.BlockSpec((tm, d), lambda i: (i, 0))
    return pl.pallas_call(
        functools.partial(_outproj_mid_body, steps=steps),
        grid=(m // tm,),
        in_specs=[pl.BlockSpec((tm, a1.shape[1]), lambda i: (i, 0)),
                  pl.BlockSpec((tm, a2.shape[1]), lambda i: (i, 0)),
                  pl.BlockSpec(w.shape, lambda i: (0, 0)),
                  row,
                  pl.BlockSpec((1, d), lambda i: (0, 0))],
        out_specs=[row, pl.BlockSpec((tm // steps, steps * d), lambda i: (i, 0))],
        out_shape=[jax.ShapeDtypeStruct((m, d), F32), jax.ShapeDtypeStruct((m // steps, steps * d), BF16)],
        scratch_shapes=[pltpu.VMEM((d // 128, tm, 128), F32)],
        compiler_params=_cparams(("parallel",)),
        name=name,
    )(a1, a2, w, x, g.reshape(1, d))


def _outproj_final_body(a_ref, w_ref, x_ref, g_ref, o_ref, a_tok_ref, *, steps):
    nl, tm, lanes = a_tok_ref.shape
    e = nl * lanes
    for s in range(steps):
        for j in range(nl):
            c0 = s * e + j * lanes
            a_tok_ref[j, pl.ds(s, tm // steps, stride=steps), :] = a_ref[:, c0:c0 + lanes].astype(F32)
    a_tok = jnp.concatenate([a_tok_ref[j] for j in range(nl)], axis=1).astype(BF16)
    acc = x_ref[...] + _dot(a_tok, w_ref[...])
    o_ref[...] = acc * lax.rsqrt(jnp.mean(acc * acc, axis=-1, keepdims=True) + EPS) * g_ref[...]


def outproj_final(a_grouped, w, x, g, *, steps, tm=256, name="outproj_final"):
    m, d = x.shape
    e = w.shape[0]
    row = pl.BlockSpec((tm, d), lambda i: (i, 0))
    return pl.pallas_call(
        functools.partial(_outproj_final_body, steps=steps),
        grid=(m // tm,),
        in_specs=[pl.BlockSpec((tm // steps, steps * e), lambda i: (i, 0)),
                  pl.BlockSpec(w.shape, lambda i: (0, 0)),
                  row,
                  pl.BlockSpec((1, d), lambda i: (0, 0))],
        out_specs=row,
        out_shape=jax.ShapeDtypeStruct((m, d), F32),
        scratch_shapes=[pltpu.VMEM((e // 128, tm, 128), F32)],
        compiler_params=_cparams(("parallel",)),
        name=name,
    )(a_grouped, w, x, g.reshape(1, d))


def _s5_body(u_ref, wt_ref, nt_ref, mt_ref, tab_ref, o_ref, *, cols_per_seq, n_levels):
    t, cg, n = u_ref.shape
    u = u_ref[...].reshape(t * cg, n)
    tab = tab_ref[0]
    half = S5_STATE
    x = _dot(nt_ref[0], u)
    pos = lax.broadcasted_iota(jnp.int32, (1, n), 1) % cols_per_seq
    for lvl in range(n_levels):
        sh = 1 << lvl
        xs = jnp.where(pos >= sh, pltpu.roll(x, sh, axis=1), 0.0)
        a_a = tab[:2 * half, 1 + lvl:2 + lvl]
        a_b = tab[:2 * half, 1 + n_levels + lvl:2 + n_levels + lvl]
        x = x + a_a * xs + a_b * pltpu.roll(xs, half, axis=0)
    x_prev = jnp.where(pos >= 1, pltpu.roll(x, 1, axis=1), 0.0)
    y = _dot(wt_ref[0], u) + _dot(mt_ref[0], x_prev.astype(BF16)) + tab[:, 0:1] * u.astype(F32)
    o_ref[...] = y.astype(o_ref.dtype).reshape(t, cg, n)


def s5_apply(proj_t, wt_g, nt_g, mt_g, tab, *, groups, cols_per_seq, n_levels):
    t, _, n = proj_t.shape
    cg = S5_GROUP_CH
    spec3 = lambda shp: pl.BlockSpec((1,) + shp, lambda i: (i, 0, 0))
    return pl.pallas_call(
        functools.partial(_s5_body, cols_per_seq=cols_per_seq, n_levels=n_levels),
        grid=(groups,),
        in_specs=[pl.BlockSpec((t, cg, n), lambda i: (0, i, 0)),
                  spec3(wt_g.shape[1:]), spec3(nt_g.shape[1:]), spec3(mt_g.shape[1:]), spec3(tab.shape[1:])],
        out_specs=pl.BlockSpec((t, cg, n), lambda i: (0, i, 0)),
        out_shape=jax.ShapeDtypeStruct((t, groups * cg, n), BF16),
        compiler_params=_cparams(("parallel",)),
        name="s5",
    )(proj_t, wt_g, nt_g, mt_g, tab)


def s5_operators(lam_re, lam_im, log_step, b_re, b_im, c_re, c_im, d, *, n_levels):
    hp = lax.Precision.HIGHEST
    g, p = lam_re.shape
    cg, t = S5_GROUP_CH, S5_T
    lr = jnp.minimum(lam_re.astype(F32), -1e-4)
    li = lam_im.astype(F32)
    step = jnp.exp(log_step.astype(F32))[:, None]
    mag = jnp.exp(lr * step)
    lb_re, lb_im = mag * jnp.cos(li * step), mag * jnp.sin(li * step)
    den = lr * lr + li * li
    nr, ni = lb_re - 1.0, lb_im
    coef_re = (nr * lr + ni * li) / den
    coef_im = (ni * lr - nr * li) / den
    bb_re = coef_re[..., None] * b_re - coef_im[..., None] * b_im
    bb_im = coef_re[..., None] * b_im + coef_im[..., None] * b_re

    def power(m):
        mm = m.astype(F32)[None, :, None]
        mg = jnp.exp(lr[:, None, :] * step[:, None, :] * mm)
        ang = li[:, None, :] * step[:, None, :] * mm
        return mg * jnp.cos(ang), mg * jnp.sin(ang)

    pw_re, pw_im = power(jnp.arange(t + 1))
    cp_re = c_re[:, None] * pw_re[:, :, None, :] - c_im[:, None] * pw_im[:, :, None, :]
    cp_im = c_re[:, None] * pw_im[:, :, None, :] + c_im[:, None] * pw_re[:, :, None, :]
    taps = (jnp.einsum('gxcp,gpd->gxcd', cp_re[:, :t], bb_re, precision=hp)
            - jnp.einsum('gxcp,gpd->gxcd', cp_im[:, :t], bb_im, precision=hp))
    shift = (jnp.arange(t)[None, None, :] - jnp.arange(t)[None, :, None]
             == jnp.arange(t)[:, None, None]).astype(F32)
    wt_g = jnp.einsum('xst,gxcd->gtcsd', shift, taps, precision=hp).reshape(g, t * cg, t * cg)
    pr, pi = pw_re[:, t - 1::-1], pw_im[:, t - 1::-1]
    bt_re, bt_im = bb_re[:, :, None, :], bb_im[:, :, None, :]
    prt, pit = pr.transpose(0, 2, 1)[:, :, :, None], pi.transpose(0, 2, 1)[:, :, :, None]
    n_re = (prt * bt_re - pit * bt_im).reshape(g, p, t * cg)
    n_im = (prt * bt_im + pit * bt_re).reshape(g, p, t * cg)
    nt_g = jnp.concatenate([n_re, n_im], axis=1)
    m_re = cp_re[:, 1:].reshape(g, t * cg, p)
    m_im = cp_im[:, 1:].reshape(g, t * cg, p)
    mt_g = jnp.concatenate([m_re, -m_im], axis=2)
    ar, ai = power(t * (2 ** jnp.arange(n_levels)))
    a_a = jnp.concatenate([ar, ar], axis=-1).transpose(0, 2, 1)
    a_b = jnp.concatenate([-ai, ai], axis=-1).transpose(0, 2, 1)
    d_col = jnp.tile(d.astype(F32).reshape(g, cg), (1, t))[:, :, None]
    rows = cg * t
    padr = lambda a: jnp.pad(a, ((0, 0), (0, rows - a.shape[1]), (0, 0)))
    tab = jnp.concatenate([d_col, padr(a_a), padr(a_b)], axis=2)
    tab = jnp.pad(tab, ((0, 0), (0, 0), (0, S5_TAB - tab.shape[2])))
    return wt_g.astype(BF16), nt_g.astype(BF16), mt_g.astype(BF16), tab


def _gelu_tanh(y):
    return 0.5 * y * (1.0 + jnp.tanh(math.sqrt(2.0 / math.pi) * (y + 0.044715 * (y * y * y))))


def _glu_body(y_ref, w_ref, b_ref, gate_ref, o_ref, z_ref, zb_ref):
    z = _gelu_tanh(y_ref[...].astype(F32))
    z_ref[...] = z
    zb_ref[...] = z.astype(BF16)
    for f0 in range(0, o_ref.shape[1], MM_SUB):
        rows = slice(f0, f0 + MM_SUB)
        lin = _dot(w_ref[rows, :], zb_ref[...]) + b_ref[rows, :]
        out = z_ref[rows, :] * _sigmoid(lin) * _silu(gate_ref[rows, :].astype(F32))
        o_ref[:, rows] = out.T.astype(o_ref.dtype)


def glu_gate(y_t, w_t, b, proj_t, *, gate_row, tm=512):
    s, e, n = y_t.shape
    gb = gate_row // e
    return pl.pallas_call(
        _glu_body,
        grid=(s, n // tm),
        in_specs=[pl.BlockSpec((None, e, tm), lambda t, i: (t, 0, i)),
                  pl.BlockSpec((e, e), lambda t, i: (0, 0)),
                  pl.BlockSpec((e, 1), lambda t, i: (0, 0)),
                  pl.BlockSpec((None, e, tm), lambda t, i: (t, gb, i))],
        out_specs=pl.BlockSpec((tm, e), lambda t, i: (i, t)),
        out_shape=jax.ShapeDtypeStruct((n, s * e), BF16),
        scratch_shapes=[pltpu.VMEM((e, tm), F32), pltpu.VMEM((e, tm), BF16)],
        compiler_params=_cparams(("parallel", "parallel")),
        name="glu",
    )(y_t, w_t, b.reshape(e, 1), proj_t)


def kernel(x, ab_norm_g, ab_w_in, gla_alpha_up, gla_alpha_b, gla_head_g, fox_f_b, ab_w_out, c_norm_g, c_w_in, s5_lambda_re, s5_lambda_im, s5_log_step, s5_b_re, s5_b_im, s5_c_re, s5_c_im, s5_d, glu_w, glu_b, c_w_out, final_norm_g):
    batch, seq, d = x.shape
    t = batch * seq
    x2 = x.reshape(t, d)

    w_in = ab_w_in[0]
    hk, hv, fw = GLA_HEADS * GLA_DK, GLA_HEADS * GLA_DV, FOX_HEADS * FOX_DH
    sizes = (hk, hk, hv, GLA_LOWRANK, hv, fw, fw, fw, FOX_HEADS, fw)
    offs = [0]
    for s in sizes:
        offs.append(offs[-1] + s)
    seg = lambda n: w_in[:, offs[n]:offs[n + 1]]
    w_main = jnp.concatenate([seg(0), seg(1), seg(4), seg(5) * (FOX_DH ** -0.5 * LOG2E), seg(6), seg(9)],
                             axis=1).astype(BF16)
    c_gq, c_gk, c_gg = 0, hk, 2 * hk
    c_fq = c_gg + hv
    c_fk, c_fg = c_fq + fw, c_fq + 2 * fw
    w_vt = jnp.concatenate([seg(2), seg(7)], axis=1).T.astype(BF16)
    pad = 128 - GLA_LOWRANK - FOX_HEADS
    w_small = jnp.concatenate([seg(3), seg(8), jnp.zeros((d, pad), F32)], axis=1).astype(BF16)

    h0 = rmsnorm_bf16(x2, ab_norm_g[0])
    proj = matmul(h0, w_main, BF16, tn=w_main.shape[1] // 2, name="ab_in")
    vt, small = matmul_t_plus_small(w_vt, w_small, h0, name="ab_in_vt")

    o_gla = gla(proj, vt, small, gla_alpha_up[0], gla_alpha_b[0], gla_head_g[0],
                batch=batch, seq=seq, q_col=c_gq, k_col=c_gk, gate_col=c_gg)

    fb_row = jnp.zeros((1, 128), F32).at[0, GLA_LOWRANK:GLA_LOWRANK + FOX_HEADS].set(fox_f_b[0])
    qa, ka = fox_gate(small, fb_row, batch=batch, seq=seq, col0=GLA_LOWRANK)
    o_fox = fox(proj, vt, qa, ka, batch=batch, seq=seq, q_col=c_fq, k_col=c_fk, gate_col=c_fg, vt_row=hv)

    x1, h1g = outproj_mid(o_gla, o_fox, ab_w_out[0].astype(BF16), x2, c_norm_g[0], steps=S5_T, name="ab_out")

    e = d
    groups = e // S5_GROUP_CH
    n_chunks = t // S5_T
    cols_per_seq = seq // S5_T
    n_levels = max(1, (cols_per_seq - 1).bit_length())
    proj_t = matmul_t_grouped(c_w_in[0].T.astype(BF16), h1g, S5_T, BF16, name="c_in")
    ops = s5_operators(s5_lambda_re[0], s5_lambda_im[0], s5_log_step[0], s5_b_re[0], s5_b_im[0],
                       s5_c_re[0], s5_c_im[0], s5_d[0], n_levels=n_levels)
    y_t = s5_apply(proj_t, *ops, groups=groups, cols_per_seq=cols_per_seq, n_levels=n_levels)
    zz = glu_gate(y_t, glu_w[0].T.astype(BF16), glu_b[0], proj_t, gate_row=e)
    out = outproj_final(zz, c_w_out[0].astype(BF16), x1, final_norm_g, steps=S5_T, name="c_out")
    return out.reshape(batch, seq, d)
```

```python
import functools
import math

import jax
import jax.numpy as jnp
from jax import lax
from jax.experimental import pallas as pl
from jax.experimental.pallas import tpu as pltpu

EPS = 1e-6
F32 = jnp.float32
BF16 = jnp.bfloat16

GLA_HEADS = 4
GLA_DK = 128
GLA_DV = 256
GLA_LOWRANK = 16
GLA_TAU = 16.0
GLA_CHUNK = 64
GLA_SUB = 8
FOX_HEADS = 8
FOX_DH = 128
FOX_AUG = 128
S5_GROUP_CH = 16
S5_STATE = 64
S5_T = 16
S5_TAB = 32

VMEM_LIMIT = 56 * 1024 * 1024

NT_DIMS = (((1,), (1,)), ((), ()))
TN_DIMS = (((0,), (0,)), ((), ()))
LOG2E = math.log2(math.e)


def _cparams(sem):
    return pltpu.CompilerParams(dimension_semantics=sem, vmem_limit_bytes=VMEM_LIMIT)


def _dot(a, b):
    return jnp.dot(a, b, preferred_element_type=F32)


def _dot_nt(a, b):
    return lax.dot_general(a, b, NT_DIMS, preferred_element_type=F32)


def _dot_tn(a, b):
    return lax.dot_general(a, b, TN_DIMS, preferred_element_type=F32)


def _log_sigmoid(z):
    return -(jnp.maximum(-z, 0.0) + jnp.log1p(jnp.exp(-jnp.abs(z))))


def _sigmoid(z):
    return 1.0 / (1.0 + jnp.exp(-z))


def _silu(z):
    return z * _sigmoid(z)


def _split3(x):
    hi = x.astype(BF16)
    r1 = x - hi.astype(F32)
    mid = r1.astype(BF16)
    lo = (r1 - mid.astype(F32)).astype(BF16)
    return hi, mid, lo


def _rmsnorm_body(x_ref, g_ref, ws_ref, o_ref, s_ref):
    x = x_ref[...]
    ms = jnp.mean(x * x, axis=-1, keepdims=True)
    h = (x * lax.rsqrt(ms + EPS) * g_ref[...]).astype(o_ref.dtype)
    o_ref[...] = h
    s_ref[...] = _dot(h, ws_ref[...])


def rmsnorm_bf16(x, g, w_small, tm=512):
    m, d = x.shape
    ns = w_small.shape[1]
    return pl.pallas_call(
        _rmsnorm_body,
        grid=(m // tm,),
        in_specs=[pl.BlockSpec((tm, d), lambda i: (i, 0)),
                  pl.BlockSpec((1, d), lambda i: (0, 0)),
                  pl.BlockSpec((d, ns), lambda i: (0, 0))],
        out_specs=[pl.BlockSpec((tm, d), lambda i: (i, 0)),
                   pl.BlockSpec((tm, ns), lambda i: (i, 0))],
        out_shape=[jax.ShapeDtypeStruct((m, d), BF16), jax.ShapeDtypeStruct((m, ns), F32)],
        compiler_params=_cparams(("parallel",)),
        name="rmsnorm",
    )(x, g.reshape(1, d), w_small)


MM_SUB = 512


def _mm_body(a_ref, w_ref, o_ref):
    a = a_ref[...]
    for c0 in range(0, o_ref.shape[1], MM_SUB):
        o_ref[:, c0:c0 + MM_SUB] = _dot(a, w_ref[:, c0:c0 + MM_SUB]).astype(o_ref.dtype)


def matmul(a, w, out_dtype, tm=512, tn=512, name="matmul"):
    m, k = a.shape
    n = w.shape[1]
    return pl.pallas_call(
        _mm_body,
        grid=(n // tn, m // tm),
        in_specs=[pl.BlockSpec((tm, k), lambda j, i: (i, 0)),
                  pl.BlockSpec((k, tn), lambda j, i: (0, j))],
        out_specs=pl.BlockSpec((tm, tn), lambda j, i: (i, j)),
        out_shape=jax.ShapeDtypeStruct((m, n), out_dtype),
        compiler_params=_cparams(("parallel", "parallel")),
        name=name,
    )(a, w)


def _mm_nt_body(wt_ref, a_ref, o_ref):
    a = a_ref[...]
    for r0 in range(0, o_ref.shape[0], MM_SUB):
        o_ref[r0:r0 + MM_SUB, :] = _dot_nt(wt_ref[r0:r0 + MM_SUB, :], a).astype(o_ref.dtype)


def matmul_t_grouped(wt, a2, steps, out_dtype, tm=512, tn=2048, name="matmul_tg"):
    nf, k = wt.shape
    n = a2.shape[0]
    return pl.pallas_call(
        _mm_nt_body,
        grid=(nf // tn, steps, n // tm),
        in_specs=[pl.BlockSpec((tn, k), lambda j, t, i: (j, 0)),
                  pl.BlockSpec((tm, k), lambda j, t, i: (i, t))],
        out_specs=pl.BlockSpec((None, tn, tm), lambda j, t, i: (t, j, i)),
        out_shape=jax.ShapeDtypeStruct((steps, nf, n), out_dtype),
        compiler_params=_cparams(("parallel", "parallel", "parallel")),
        name=name,
    )(wt, a2)


def _gla_body(q_ref, k_ref, v_ref, glr_ref, aup_ref, ab_ref, gate_ref, hg_ref, o_ref,
              s_ref, b_ref, kf_ref, kts_ref, dec_ref, *, tb):
    ib = pl.program_id(2)
    C, SB = GLA_CHUNK, GLA_SUB

    @pl.when(ib == 0)
    def _():
        s_ref[...] = jnp.zeros_like(s_ref)

    z = _dot(glr_ref[:, :GLA_LOWRANK].astype(BF16), aup_ref[...].astype(BF16)) + ab_ref[...]
    la = _log_sigmoid(z) * (1.0 / GLA_TAU)
    tw = 2 * C
    r = lax.broadcasted_iota(jnp.int32, (tw, tw), 0)
    c = lax.broadcasted_iota(jnp.int32, (tw, tw), 1)
    tri = jnp.where((r // C == c // C) & (c <= r), 1.0, 0.0).astype(BF16)
    hi = la.astype(BF16)
    lo = (la - hi.astype(F32)).astype(BF16)
    b_blk = jnp.concatenate(
        [_dot(tri, hi[r0:r0 + tw]) + _dot(tri, lo[r0:r0 + tw]) for r0 in range(0, tb, tw)], axis=0)
    k_blk = k_ref[...].astype(F32)
    b_ref[...] = b_blk
    kf_ref[...] = k_blk
    nc = tb // C
    b3 = b_blk.reshape(nc, C, GLA_DK)
    b_end = b3[:, C - 1:C, :]
    k_end = (k_blk.reshape(nc, C, GLA_DK) * jnp.exp(b_end - b3)).reshape(tb, GLA_DK)
    kt_end = k_end.T.astype(BF16)
    dec_t = jnp.exp(b_end.reshape(nc, GLA_DK)).T
    for ci in range(nc):
        kts_ref[ci] = kt_end[:, ci * C:(ci + 1) * C]
        dec_ref[ci] = jnp.broadcast_to(dec_t[:, ci:ci + 1], (GLA_DK, GLA_DV))

    row = lax.broadcasted_iota(jnp.int32, (C, 1), 0)
    rowi = lax.broadcasted_iota(jnp.int32, (C, C), 0)
    coli = lax.broadcasted_iota(jnp.int32, (C, C), 1)
    lane_c = lax.broadcasted_iota(jnp.int32, (SB, C), 1)
    sub_r = lax.broadcasted_iota(jnp.int32, (SB, 1), 0)
    ones = jnp.ones((GLA_DK, C), BF16)
    neg = -jnp.inf
    scale = GLA_DK ** -0.5

    def chunk(ci, carry):
        r0 = pl.multiple_of(ci * C, C)
        q = q_ref[pl.ds(r0, C), :].astype(F32) * scale
        k = kf_ref[pl.ds(r0, C), :]
        b = b_ref[pl.ds(r0, C), :]
        v = v_ref[pl.ds(r0, C), :]

        s = s_ref[...]
        o = _dot((q * jnp.exp(b)).astype(BF16), s.astype(BF16))
        s_ref[...] = s * dec_ref[ci] + _dot(kts_ref[ci], v)

        attn = jnp.zeros((C, C), F32)
        h = C // 2
        while h >= SB:
            ref = jnp.broadcast_to(b.reshape(C // (2 * h), 2 * h, GLA_DK)[:, h - 1:h, :],
                                   (C // (2 * h), 2 * h, GLA_DK)).reshape(C, GLA_DK)
            upper = (row // h) % 2 == 1
            q_h = q * jnp.exp(jnp.where(upper, b - ref, neg))
            k_h = k * jnp.exp(jnp.where(upper, neg, ref - b))
            a_h = _dot_nt(q_h.astype(BF16), k_h.astype(BF16))
            attn = attn + (a_h if 2 * h == C else jnp.where(rowi // (2 * h) == coli // (2 * h), a_h, 0.0))
            h //= 2

        zs = []
        for bi in range(C // SB):
            s0 = bi * SB
            q_i = q[s0:s0 + SB, :]
            b_i = b[s0:s0 + SB, :]
            for j in range(SB):
                k_j = kf_ref[pl.ds(r0 + (s0 + j), 1), :]
                b_j = b_ref[pl.ds(r0 + (s0 + j), 1), :]
                zs.append(q_i * k_j * jnp.exp(jnp.where(sub_r >= j, b_i - b_j, neg)))
        zsum = _dot(jnp.concatenate(zs, axis=0).astype(BF16), ones)
        diag = []
        for bi in range(C // SB):
            acc = jnp.zeros((SB, C), F32)
            for j in range(SB):
                n0 = (bi * SB + j) * SB
                acc = acc + jnp.where(lane_c == bi * SB + j, zsum[n0:n0 + SB, :], 0.0)
            diag.append(acc)
        attn = attn + jnp.concatenate(diag, axis=0)
        o = o + _dot(attn.astype(BF16), v)

        o = o * lax.rsqrt(jnp.mean(o * o, axis=-1, keepdims=True) + EPS) * hg_ref[...]
        g = gate_ref[pl.ds(r0, C), :].astype(F32)
        o_ref[pl.ds(r0, C), :] = (o * _silu(g)).astype(o_ref.dtype)
        return carry

    lax.fori_loop(0, tb // C, chunk, 0, unroll=4)


def gla(proj, small, alpha_up, alpha_b, head_g, *, batch, seq, q_col, k_col, v_col, gate_col, tb=512):
    t = batch * seq
    nb = seq // tb
    dk, dv, h = GLA_DK, GLA_DV, GLA_HEADS
    qb, kb, vb, gb = q_col // dk, k_col // dk, v_col // dv, gate_col // dv
    tok = lambda b, hh, i: b * nb + i
    return pl.pallas_call(
        functools.partial(_gla_body, tb=tb),
        grid=(batch, h, nb),
        in_specs=[
            pl.BlockSpec((tb, dk), lambda b, hh, i: (tok(b, hh, i), qb + hh)),
            pl.BlockSpec((tb, dk), lambda b, hh, i: (tok(b, hh, i), kb + hh)),
            pl.BlockSpec((tb, dv), lambda b, hh, i: (tok(b, hh, i), vb + hh)),
            pl.BlockSpec((tb, 128), lambda b, hh, i: (tok(b, hh, i), 0)),
            pl.BlockSpec((GLA_LOWRANK, dk), lambda b, hh, i: (0, hh)),
            pl.BlockSpec((1, dk), lambda b, hh, i: (0, hh)),
            pl.BlockSpec((tb, dv), lambda b, hh, i: (tok(b, hh, i), gb + hh)),
            pl.BlockSpec((1, dv), lambda b, hh, i: (0, hh)),
        ],
        out_specs=pl.BlockSpec((tb, dv), lambda b, hh, i: (tok(b, hh, i), hh)),
        out_shape=jax.ShapeDtypeStruct((t, h * dv), BF16),
        scratch_shapes=[pltpu.VMEM((dk, dv), F32), pltpu.VMEM((tb, dk), F32), pltpu.VMEM((tb, dk), F32),
                        pltpu.VMEM((tb // GLA_CHUNK, dk, GLA_CHUNK), BF16),
                        pltpu.VMEM((tb // GLA_CHUNK, dk, dv), F32)],
        compiler_params=_cparams(("parallel", "parallel", "arbitrary")),
        name="gla",
    )(proj, proj, proj, small, alpha_up, alpha_b.reshape(1, -1), proj, head_g.reshape(1, -1))


def _fox_gate_body(s_ref, fb_ref, qa_ref, ka_ref, *, blk, col0):
    n = s_ref.shape[0] // blk
    r = lax.broadcasted_iota(jnp.int32, (blk, blk), 0)
    c = lax.broadcasted_iota(jnp.int32, (blk, blk), 1)
    tri = jnp.where(c <= r, 1.0, 0.0).astype(BF16)
    lane = lax.broadcasted_iota(jnp.int32, (blk, FOX_AUG), 1)

    def step(i, carry):
        r0 = pl.multiple_of(i * blk, blk)
        lf = _log_sigmoid(s_ref[pl.ds(r0, blk), :] + fb_ref[...])
        hi, mid, lo = _split3(lf)
        cs = _dot(tri, hi) + _dot(tri, mid) + _dot(tri, lo) + carry
        for h in range(FOX_HEADS):
            col = cs[:, col0 + h:col0 + h + 1] * LOG2E
            c0, c1, c2 = (p.astype(F32) for p in _split3(col))
            qa = jnp.where(lane == 0, c0, jnp.where(lane == 1, c1, jnp.where(lane == 2, c2,
                           jnp.where(lane < 6, 1.0, 0.0))))
            ka = jnp.where(lane < 3, 1.0, jnp.where(lane == 3, -c0, jnp.where(lane == 4, -c1,
                           jnp.where(lane == 5, -c2, 0.0))))
            qa_ref[h, pl.ds(r0, blk), :] = qa.astype(BF16)
            ka_ref[h, pl.ds(r0, blk), :] = ka.astype(BF16)
        return cs[blk - 1:blk, :]

    lax.fori_loop(0, n, step, jnp.zeros((1, s_ref.shape[1]), F32))


def fox_gate(small, fb_row, *, batch, seq, col0, blk=256):
    shp = jax.ShapeDtypeStruct((FOX_HEADS, batch * seq, FOX_AUG), BF16)
    spec = pl.BlockSpec((FOX_HEADS, seq, FOX_AUG), lambda b: (0, b, 0))
    return pl.pallas_call(
        functools.partial(_fox_gate_body, blk=blk, col0=col0),
        grid=(batch,),
        in_specs=[pl.BlockSpec((seq, 128), lambda b: (b, 0)),
                  pl.BlockSpec((1, 128), lambda b: (0, 0))],
        out_specs=[spec, spec],
        out_shape=[shp, shp],
        compiler_params=_cparams(("parallel",)),
        name="fox_gate",
    )(small, fb_row)


def _fox_body(it_ref, jt_ref, q_ref, qa_ref, k_ref, ka_ref, v_ref, gate_ref, o_ref,
              m_ref, l_ref, acc_ref, *, tq, tk, hpb):
    p = pl.program_id(2)
    i = it_ref[p]
    j = jt_ref[p]
    d = FOX_DH

    @pl.when(j == 0)
    def _():
        m_ref[...] = jnp.full_like(m_ref, -jnp.inf)
        l_ref[...] = jnp.zeros_like(l_ref)
        acc_ref[...] = jnp.zeros_like(acc_ref)

    def update(hh, masked):
        q_aug = jnp.concatenate([q_ref[:, hh * d:(hh + 1) * d], qa_ref[hh]], axis=1)
        k_aug = jnp.concatenate([k_ref[:, hh * d:(hh + 1) * d], ka_ref[hh]], axis=1)
        st = _dot_nt(k_aug, q_aug)
        if masked:
            kr = lax.broadcasted_iota(jnp.int32, (tk, tq), 0)
            qc = lax.broadcasted_iota(jnp.int32, (tk, tq), 1)
            st = jnp.where(qc >= kr, st, -jnp.inf)
        m_old = m_ref[hh]
        m_new = jnp.maximum(m_old, jnp.max(st, axis=0, keepdims=True))
        alpha = jnp.exp2(m_old - m_new)
        pt = jnp.exp2(st - m_new)
        l_ref[hh] = alpha * l_ref[hh] + jnp.sum(pt, axis=0, keepdims=True)
        acc_ref[hh] = alpha * acc_ref[hh] + _dot_tn(v_ref[:, hh * d:(hh + 1) * d], pt.astype(BF16))
        m_ref[hh] = m_new

    @pl.when(j < i)
    def _():
        for hh in range(hpb):
            update(hh, False)

    @pl.when(j == i)
    def _():
        for hh in range(hpb):
            update(hh, True)
            o = (acc_ref[hh] / l_ref[hh]).T
            g = gate_ref[:, hh * d:(hh + 1) * d].astype(F32)
            o_ref[:, hh * d:(hh + 1) * d] = (o * _silu(g)).astype(o_ref.dtype)


def fox(proj, qa, ka, *, batch, seq, q_col, k_col, v_col, gate_col, t=512, hpb=2):
    tt = batch * seq
    nb = seq // t
    d, h = FOX_DH, FOX_HEADS
    w = hpb * d
    qb, kb, gb, vb = q_col // w, k_col // w, gate_col // w, v_col // w
    pairs = [(i, j) for i in range(nb) for j in range(i + 1)]
    it = jnp.array([p[0] for p in pairs], jnp.int32)
    jt = jnp.array([p[1] for p in pairs], jnp.int32)
    grid_spec = pltpu.PrefetchScalarGridSpec(
        num_scalar_prefetch=2,
        grid=(batch, h // hpb, len(pairs)),
        in_specs=[
            pl.BlockSpec((t, w), lambda b, hh, p, it, jt: (b * nb + it[p], qb + hh)),
            pl.BlockSpec((hpb, t, FOX_AUG), lambda b, hh, p, it, jt: (hh, b * nb + it[p], 0)),
            pl.BlockSpec((t, w), lambda b, hh, p, it, jt: (b * nb + jt[p], kb + hh)),
            pl.BlockSpec((hpb, t, FOX_AUG), lambda b, hh, p, it, jt: (hh, b * nb + jt[p], 0)),
            pl.BlockSpec((t, w), lambda b, hh, p, it, jt: (b * nb + jt[p], vb + hh)),
            pl.BlockSpec((t, w), lambda b, hh, p, it, jt: (b * nb + it[p], gb + hh)),
        ],
        out_specs=pl.BlockSpec((t, w), lambda b, hh, p, it, jt: (b * nb + it[p], hh)),
        scratch_shapes=[pltpu.VMEM((hpb, 1, t), F32), pltpu.VMEM((hpb, 1, t), F32), pltpu.VMEM((hpb, d, t), F32)],
    )
    return pl.pallas_call(
        functools.partial(_fox_body, tq=t, tk=t, hpb=hpb),
        grid_spec=grid_spec,
        out_shape=jax.ShapeDtypeStruct((tt, h * d), BF16),
        compiler_params=_cparams(("parallel", "parallel", "arbitrary")),
        name="fox",
    )(it, jt, proj, qa, proj, ka, proj, proj)


def _outproj_mid_body(a1_ref, a2_ref, w_ref, x_ref, g_ref, r_ref, h_ref, n_ref, *, steps):
    k1 = a1_ref.shape[1]
    acc = x_ref[...] + _dot(a1_ref[...], w_ref[:k1, :]) + _dot(a2_ref[...], w_ref[k1:, :])
    r_ref[...] = acc
    normed = acc * lax.rsqrt(jnp.mean(acc * acc, axis=-1, keepdims=True) + EPS) * g_ref[...]
    nl, tm, lanes = n_ref.shape
    d = nl * lanes
    for j in range(nl):
        n_ref[j] = normed[:, j * lanes:(j + 1) * lanes]
    for s in range(steps):
        for j in range(nl):
            c0 = s * d + j * lanes
            h_ref[:, c0:c0 + lanes] = n_ref[j, pl.ds(s, tm // steps, stride=steps), :].astype(h_ref.dtype)


def outproj_mid(a1, a2, w, x, g, *, steps, tm=256, name="outproj_mid"):
    m, d = x.shape
    row = pl.BlockSpec((tm, d), lambda i: (i, 0))
    return pl.pallas_call(
        functools.partial(_outproj_mid_body, steps=steps),
        grid=(m // tm,),
        in_specs=[pl.BlockSpec((tm, a1.shape[1]), lambda i: (i, 0)),
                  pl.BlockSpec((tm, a2.shape[1]), lambda i: (i, 0)),
                  pl.BlockSpec(w.shape, lambda i: (0, 0)),
                  row,
                  pl.BlockSpec((1, d), lambda i: (0, 0))],
        out_specs=[row, pl.BlockSpec((tm // steps, steps * d), lambda i: (i, 0))],
        out_shape=[jax.ShapeDtypeStruct((m, d), F32), jax.ShapeDtypeStruct((m // steps, steps * d), BF16)],
        scratch_shapes=[pltpu.VMEM((d // 128, tm, 128), F32)],
        compiler_params=_cparams(("parallel",)),
        name=name,
    )(a1, a2, w, x, g.reshape(1, d))


def _outproj_final_body(a_ref, w_ref, x_ref, g_ref, o_ref, a_tok_ref, *, steps):
    nl, tm, lanes = a_tok_ref.shape
    e = nl * lanes
    for s in range(steps):
        for j in range(nl):
            c0 = s * e + j * lanes
            a_tok_ref[j, pl.ds(s, tm // steps, stride=steps), :] = a_ref[:, c0:c0 + lanes].astype(F32)
    a_tok = jnp.concatenate([a_tok_ref[j] for j in range(nl)], axis=1).astype(BF16)
    acc = x_ref[...] + _dot(a_tok, w_ref[...])
    o_ref[...] = acc * lax.rsqrt(jnp.mean(acc * acc, axis=-1, keepdims=True) + EPS) * g_ref[...]


def outproj_final(a_grouped, w, x, g, *, steps, tm=256, name="outproj_final"):
    m, d = x.shape
    e = w.shape[0]
    row = pl.BlockSpec((tm, d), lambda i: (i, 0))
    return pl.pallas_call(
        functools.partial(_outproj_final_body, steps=steps),
        grid=(m // tm,),
        in_specs=[pl.BlockSpec((tm // steps, steps * e), lambda i: (i, 0)),
                  pl.BlockSpec(w.shape, lambda i: (0, 0)),
                  row,
                  pl.BlockSpec((1, d), lambda i: (0, 0))],
        out_specs=row,
        out_shape=jax.ShapeDtypeStruct((m, d), F32),
        scratch_shapes=[pltpu.VMEM((e // 128, tm, 128), F32)],
        compiler_params=_cparams(("parallel",)),
        name=name,
    )(a_grouped, w, x, g.reshape(1, d))


def _s5_body(u_ref, wt_ref, nt_ref, mt_ref, tab_ref, o_ref, *, cols_per_seq, n_levels):
    t, cg, n = u_ref.shape
    u = u_ref[...].reshape(t * cg, n)
    tab = tab_ref[0]
    half = S5_STATE
    x = _dot(nt_ref[0], u)
    pos = lax.broadcasted_iota(jnp.int32, (1, n), 1) % cols_per_seq
    for lvl in range(n_levels):
        sh = 1 << lvl
        xs = jnp.where(pos >= sh, pltpu.roll(x, sh, axis=1), 0.0)
        a_a = tab[:2 * half, 1 + lvl:2 + lvl]
        a_b = tab[:2 * half, 1 + n_levels + lvl:2 + n_levels + lvl]
        x = x + a_a * xs + a_b * pltpu.roll(xs, half, axis=0)
    x_prev = jnp.where(pos >= 1, pltpu.roll(x, 1, axis=1), 0.0)
    y = _dot(wt_ref[0], u) + _dot(mt_ref[0], x_prev.astype(BF16)) + tab[:, 0:1] * u.astype(F32)
    o_ref[...] = y.astype(o_ref.dtype).reshape(t, cg, n)


def s5_apply(proj_t, wt_g, nt_g, mt_g, tab, *, groups, cols_per_seq, n_levels):
    t, _, n = proj_t.shape
    cg = S5_GROUP_CH
    spec3 = lambda shp: pl.BlockSpec((1,) + shp, lambda i: (i, 0, 0))
    return pl.pallas_call(
        functools.partial(_s5_body, cols_per_seq=cols_per_seq, n_levels=n_levels),
        grid=(groups,),
        in_specs=[pl.BlockSpec((t, cg, n), lambda i: (0, i, 0)),
                  spec3(wt_g.shape[1:]), spec3(nt_g.shape[1:]), spec3(mt_g.shape[1:]), spec3(tab.shape[1:])],
        out_specs=pl.BlockSpec((t, cg, n), lambda i: (0, i, 0)),
        out_shape=jax.ShapeDtypeStruct((t, groups * cg, n), BF16),
        compiler_params=_cparams(("parallel",)),
        name="s5",
    )(proj_t, wt_g, nt_g, mt_g, tab)


def s5_operators(lam_re, lam_im, log_step, b_re, b_im, c_re, c_im, d, *, n_levels):
    hp = lax.Precision.HIGHEST
    g, p = lam_re.shape
    cg, t = S5_GROUP_CH, S5_T
    lr = jnp.minimum(lam_re.astype(F32), -1e-4)
    li = lam_im.astype(F32)
    step = jnp.exp(log_step.astype(F32))[:, None]
    mag = jnp.exp(lr * step)
    lb_re, lb_im = mag * jnp.cos(li * step), mag * jnp.sin(li * step)
    den = lr * lr + li * li
    nr, ni = lb_re - 1.0, lb_im
    coef_re = (nr * lr + ni * li) / den
    coef_im = (ni * lr - nr * li) / den
    bb_re = coef_re[..., None] * b_re - coef_im[..., None] * b_im
    bb_im = coef_re[..., None] * b_im + coef_im[..., None] * b_re

    def power(m):
        mm = m.astype(F32)[None, :, None]
        mg = jnp.exp(lr[:, None, :] * step[:, None, :] * mm)
        ang = li[:, None, :] * step[:, None, :] * mm
        return mg * jnp.cos(ang), mg * jnp.sin(ang)

    pw_re, pw_im = power(jnp.arange(t + 1))
    cp_re = c_re[:, None] * pw_re[:, :, None, :] - c_im[:, None] * pw_im[:, :, None, :]
    cp_im = c_re[:, None] * pw_im[:, :, None, :] + c_im[:, None] * pw_re[:, :, None, :]
    taps = (jnp.einsum('gxcp,gpd->gxcd', cp_re[:, :t], bb_re, precision=hp)
            - jnp.einsum('gxcp,gpd->gxcd', cp_im[:, :t], bb_im, precision=hp))
    shift = (jnp.arange(t)[None, None, :] - jnp.arange(t)[None, :, None]
             == jnp.arange(t)[:, None, None]).astype(F32)
    wt_g = jnp.einsum('xst,gxcd->gtcsd', shift, taps, precision=hp).reshape(g, t * cg, t * cg)
    pr, pi = pw_re[:, t - 1::-1], pw_im[:, t - 1::-1]
    bt_re, bt_im = bb_re[:, :, None, :], bb_im[:, :, None, :]
    prt, pit = pr.transpose(0, 2, 1)[:, :, :, None], pi.transpose(0, 2, 1)[:, :, :, None]
    n_re = (prt * bt_re - pit * bt_im).reshape(g, p, t * cg)
    n_im = (prt * bt_im + pit * bt_re).reshape(g, p, t * cg)
    nt_g = jnp.concatenate([n_re, n_im], axis=1)
    m_re = cp_re[:, 1:].reshape(g, t * cg, p)
    m_im = cp_im[:, 1:].reshape(g, t * cg, p)
    mt_g = jnp.concatenate([m_re, -m_im], axis=2)
    ar, ai = power(t * (2 ** jnp.arange(n_levels)))
    a_a = jnp.concatenate([ar, ar], axis=-1).transpose(0, 2, 1)
    a_b = jnp.concatenate([-ai, ai], axis=-1).transpose(0, 2, 1)
    d_col = jnp.tile(d.astype(F32).reshape(g, cg), (1, t))[:, :, None]
    rows = cg * t
    padr = lambda a: jnp.pad(a, ((0, 0), (0, rows - a.shape[1]), (0, 0)))
    tab = jnp.concatenate([d_col, padr(a_a), padr(a_b)], axis=2)
    tab = jnp.pad(tab, ((0, 0), (0, 0), (0, S5_TAB - tab.shape[2])))
    return wt_g.astype(BF16), nt_g.astype(BF16), mt_g.astype(BF16), tab


def _gelu_tanh(y):
    return 0.5 * y * (1.0 + jnp.tanh(math.sqrt(2.0 / math.pi) * (y + 0.044715 * (y * y * y))))


def _glu_body(y_ref, w_ref, b_ref, gate_ref, o_ref, z_ref, zb_ref):
    z = _gelu_tanh(y_ref[...].astype(F32))
    z_ref[...] = z
    zb_ref[...] = z.astype(BF16)
    for f0 in range(0, o_ref.shape[1], MM_SUB):
        rows = slice(f0, f0 + MM_SUB)
        lin = _dot(w_ref[rows, :], zb_ref[...]) + b_ref[rows, :]
        out = z_ref[rows, :] * _sigmoid(lin) * _silu(gate_ref[rows, :].astype(F32))
        o_ref[:, rows] = out.T.astype(o_ref.dtype)


def glu_gate(y_t, w_t, b, proj_t, *, gate_row, tm=512):
    s, e, n = y_t.shape
    gb = gate_row // e
    return pl.pallas_call(
        _glu_body,
        grid=(s, n // tm),
        in_specs=[pl.BlockSpec((None, e, tm), lambda t, i: (t, 0, i)),
                  pl.BlockSpec((e, e), lambda t, i: (0, 0)),
                  pl.BlockSpec((e, 1), lambda t, i: (0, 0)),
                  pl.BlockSpec((None, e, tm), lambda t, i: (t, gb, i))],
        out_specs=pl.BlockSpec((tm, e), lambda t, i: (i, t)),
        out_shape=jax.ShapeDtypeStruct((n, s * e), BF16),
        scratch_shapes=[pltpu.VMEM((e, tm), F32), pltpu.VMEM((e, tm), BF16)],
        compiler_params=_cparams(("parallel", "parallel")),
        name="glu",
    )(y_t, w_t, b.reshape(e, 1), proj_t)


def kernel(x, ab_norm_g, ab_w_in, gla_alpha_up, gla_alpha_b, gla_head_g, fox_f_b, ab_w_out, c_norm_g, c_w_in, s5_lambda_re, s5_lambda_im, s5_log_step, s5_b_re, s5_b_im, s5_c_re, s5_c_im, s5_d, glu_w, glu_b, c_w_out, final_norm_g):
    batch, seq, d = x.shape
    t = batch * seq
    x2 = x.reshape(t, d)

    w_in = ab_w_in[0]
    hk, hv, fw = GLA_HEADS * GLA_DK, GLA_HEADS * GLA_DV, FOX_HEADS * FOX_DH
    sizes = (hk, hk, hv, GLA_LOWRANK, hv, fw, fw, fw, FOX_HEADS, fw)
    offs = [0]
    for s in sizes:
        offs.append(offs[-1] + s)
    seg = lambda n: w_in[:, offs[n]:offs[n + 1]]
    w_main = jnp.concatenate([w_in[:, :offs[3]], seg(4), seg(5) * (FOX_DH ** -0.5 * LOG2E),
                              w_in[:, offs[6]:offs[8]], seg(9)], axis=1).astype(BF16)
    c_gq, c_gk, c_gv, c_gg = 0, hk, 2 * hk, 2 * hk + hv
    c_fq = c_gg + hv
    c_fk, c_fv, c_fg = c_fq + fw, c_fq + 2 * fw, c_fq + 3 * fw
    pad = 128 - GLA_LOWRANK - FOX_HEADS
    w_small = jnp.concatenate([seg(3), seg(8), jnp.zeros((d, pad), F32)], axis=1).astype(BF16)

    h0, small = rmsnorm_bf16(x2, ab_norm_g[0], w_small)
    proj = matmul(h0, w_main, BF16, tn=w_main.shape[1] // 2, name="ab_in")

    o_gla = gla(proj, small, gla_alpha_up[0], gla_alpha_b[0], gla_head_g[0],
                batch=batch, seq=seq, q_col=c_gq, k_col=c_gk, v_col=c_gv, gate_col=c_gg)

    fb_row = jnp.zeros((1, 128), F32).at[0, GLA_LOWRANK:GLA_LOWRANK + FOX_HEADS].set(fox_f_b[0])
    qa, ka = fox_gate(small, fb_row, batch=batch, seq=seq, col0=GLA_LOWRANK)
    o_fox = fox(proj, qa, ka, batch=batch, seq=seq, q_col=c_fq, k_col=c_fk, v_col=c_fv, gate_col=c_fg)

    x1, h1g = outproj_mid(o_gla, o_fox, ab_w_out[0].astype(BF16), x2, c_norm_g[0], steps=S5_T, name="ab_out")

    e = d
    groups = e // S5_GROUP_CH
    n_chunks = t // S5_T
    cols_per_seq = seq // S5_T
    n_levels = max(1, (cols_per_seq - 1).bit_length())
    proj_t = matmul_t_grouped(c_w_in[0].T.astype(BF16), h1g, S5_T, BF16, name="c_in")
    ops = s5_operators(s5_lambda_re[0], s5_lambda_im[0], s5_log_step[0], s5_b_re[0], s5_b_im[0],
                       s5_c_re[0], s5_c_im[0], s5_d[0], n_levels=n_levels)
    y_t = s5_apply(proj_t, *ops, groups=groups, cols_per_seq=cols_per_seq, n_levels=n_levels)
    zz = glu_gate(y_t, glu_w[0].T.astype(BF16), glu_b[0], proj_t, gate_row=e)
    out = outproj_final(zz, c_w_out[0].astype(BF16), x1, final_norm_g, steps=S5_T, name="c_out")
    return out.reshape(batch, seq, d)
```

```python
import functools
import math

import jax
import jax.numpy as jnp
from jax import lax
from jax.experimental import pallas as pl
from jax.experimental.pallas import tpu as pltpu

EPS = 1e-6
F32 = jnp.float32
BF16 = jnp.bfloat16

GLA_HEADS = 4
GLA_DK = 128
GLA_DV = 256
GLA_LOWRANK = 16
GLA_TAU = 16.0
GLA_CHUNK = 64
GLA_SUB = 8
FOX_HEADS = 8
FOX_DH = 128
FOX_AUG = 128
S5_GROUP_CH = 16
S5_STATE = 64
S5_T = 16
S5_TAB = 32

VMEM_LIMIT = 56 * 1024 * 1024

NT_DIMS = (((1,), (1,)), ((), ()))
TN_DIMS = (((0,), (0,)), ((), ()))
LOG2E = math.log2(math.e)


def _cparams(sem):
    return pltpu.CompilerParams(dimension_semantics=sem, vmem_limit_bytes=VMEM_LIMIT)


def _dot(a, b):
    return jnp.dot(a, b, preferred_element_type=F32)


def _dot_nt(a, b):
    return lax.dot_general(a, b, NT_DIMS, preferred_element_type=F32)


def _dot_tn(a, b):
    return lax.dot_general(a, b, TN_DIMS, preferred_element_type=F32)


def _log_sigmoid(z):
    return -(jnp.maximum(-z, 0.0) + jnp.log1p(jnp.exp(-jnp.abs(z))))


def _sigmoid(z):
    return 1.0 / (1.0 + jnp.exp(-z))


def _silu(z):
    return z * _sigmoid(z)


def _split3(x):
    hi = x.astype(BF16)
    r1 = x - hi.astype(F32)
    mid = r1.astype(BF16)
    lo = (r1 - mid.astype(F32)).astype(BF16)
    return hi, mid, lo


def _rmsnorm_body(x_ref, g_ref, ws_ref, o_ref, s_ref):
    x = x_ref[...]
    ms = jnp.mean(x * x, axis=-1, keepdims=True)
    h = (x * lax.rsqrt(ms + EPS) * g_ref[...]).astype(o_ref.dtype)
    o_ref[...] = h
    s_ref[...] = _dot(h, ws_ref[...])


def rmsnorm_bf16(x, g, w_small, tm=512):
    m, d = x.shape
    ns = w_small.shape[1]
    return pl.pallas_call(
        _rmsnorm_body,
        grid=(m // tm,),
        in_specs=[pl.BlockSpec((tm, d), lambda i: (i, 0)),
                  pl.BlockSpec((1, d), lambda i: (0, 0)),
                  pl.BlockSpec((d, ns), lambda i: (0, 0))],
        out_specs=[pl.BlockSpec((tm, d), lambda i: (i, 0)),
                   pl.BlockSpec((tm, ns), lambda i: (i, 0))],
        out_shape=[jax.ShapeDtypeStruct((m, d), BF16), jax.ShapeDtypeStruct((m, ns), F32)],
        compiler_params=_cparams(("parallel",)),
        name="rmsnorm",
    )(x, g.reshape(1, d), w_small)


def _ab_wprep_body(w_ref, o_ref, s_ref, *, offs):
    w = w_ref[0]
    seg = lambda n: w[:, offs[n]:offs[n + 1]]
    o_ref[...] = jnp.concatenate(
        [w[:, :offs[3]], seg(4), seg(5) * (FOX_DH ** -0.5 * LOG2E), w[:, offs[6]:offs[8]], seg(9)],
        axis=1).astype(o_ref.dtype)
    pad = s_ref.shape[1] - (offs[4] - offs[3]) - (offs[9] - offs[8])
    s_ref[...] = jnp.concatenate([seg(3), seg(8), jnp.zeros((w.shape[0], pad), F32)], axis=1).astype(s_ref.dtype)


def ab_weight_prep(w_in3, offs, tr=256):
    _, d, n_in = w_in3.shape
    n_main = n_in - (offs[4] - offs[3]) - (offs[9] - offs[8])
    return pl.pallas_call(
        functools.partial(_ab_wprep_body, offs=tuple(offs)),
        grid=(d // tr,),
        in_specs=[pl.BlockSpec((1, tr, n_in), lambda i: (0, i, 0))],
        out_specs=[pl.BlockSpec((tr, n_main), lambda i: (i, 0)), pl.BlockSpec((tr, 128), lambda i: (i, 0))],
        out_shape=[jax.ShapeDtypeStruct((d, n_main), BF16), jax.ShapeDtypeStruct((d, 128), BF16)],
        compiler_params=_cparams(("parallel",)),
        name="ab_wprep",
    )(w_in3)


MM_SUB = 512


def _mm_body(a_ref, w_ref, o_ref):
    a = a_ref[...]
    for c0 in range(0, o_ref.shape[1], MM_SUB):
        o_ref[:, c0:c0 + MM_SUB] = _dot(a, w_ref[:, c0:c0 + MM_SUB]).astype(o_ref.dtype)


def matmul(a, w, out_dtype, tm=512, tn=512, name="matmul"):
    m, k = a.shape
    n = w.shape[1]
    return pl.pallas_call(
        _mm_body,
        grid=(n // tn, m // tm),
        in_specs=[pl.BlockSpec((tm, k), lambda j, i: (i, 0)),
                  pl.BlockSpec((k, tn), lambda j, i: (0, j))],
        out_specs=pl.BlockSpec((tm, tn), lambda j, i: (i, j)),
        out_shape=jax.ShapeDtypeStruct((m, n), out_dtype),
        compiler_params=_cparams(("parallel", "parallel")),
        name=name,
    )(a, w)


def _mm_nt_body(wt_ref, a_ref, o_ref):
    a = a_ref[...]
    for r0 in range(0, o_ref.shape[0], MM_SUB):
        o_ref[r0:r0 + MM_SUB, :] = _dot_nt(wt_ref[r0:r0 + MM_SUB, :], a).astype(o_ref.dtype)


def matmul_t_grouped(wt, a2, steps, out_dtype, tm=512, tn=2048, name="matmul_tg"):
    nf, k = wt.shape
    n = a2.shape[0]
    return pl.pallas_call(
        _mm_nt_body,
        grid=(nf // tn, steps, n // tm),
        in_specs=[pl.BlockSpec((tn, k), lambda j, t, i: (j, 0)),
                  pl.BlockSpec((tm, k), lambda j, t, i: (i, t))],
        out_specs=pl.BlockSpec((None, tn, tm), lambda j, t, i: (t, j, i)),
        out_shape=jax.ShapeDtypeStruct((steps, nf, n), out_dtype),
        compiler_params=_cparams(("parallel", "parallel", "parallel")),
        name=name,
    )(wt, a2)


def _gla_body(q_ref, k_ref, v_ref, glr_ref, aup_ref, ab_ref, gate_ref, hg_ref, o_ref,
              s_ref, b_ref, kf_ref, kts_ref, dec_ref, *, tb):
    ib = pl.program_id(2)
    C, SB = GLA_CHUNK, GLA_SUB

    @pl.when(ib == 0)
    def _():
        s_ref[...] = jnp.zeros_like(s_ref)

    z = _dot(glr_ref[:, :GLA_LOWRANK].astype(BF16), aup_ref[...].astype(BF16)) + ab_ref[...]
    la = _log_sigmoid(z) * (1.0 / GLA_TAU)
    tw = 2 * C
    r = lax.broadcasted_iota(jnp.int32, (tw, tw), 0)
    c = lax.broadcasted_iota(jnp.int32, (tw, tw), 1)
    tri = jnp.where((r // C == c // C) & (c <= r), 1.0, 0.0).astype(BF16)
    hi = la.astype(BF16)
    lo = (la - hi.astype(F32)).astype(BF16)
    b_blk = jnp.concatenate(
        [_dot(tri, hi[r0:r0 + tw]) + _dot(tri, lo[r0:r0 + tw]) for r0 in range(0, tb, tw)], axis=0)
    k_blk = k_ref[...].astype(F32)
    b_ref[...] = b_blk
    kf_ref[...] = k_blk
    nc = tb // C
    b3 = b_blk.reshape(nc, C, GLA_DK)
    b_end = b3[:, C - 1:C, :]
    k_end = (k_blk.reshape(nc, C, GLA_DK) * jnp.exp(b_end - b3)).reshape(tb, GLA_DK)
    kt_end = k_end.T.astype(BF16)
    dec_t = jnp.exp(b_end.reshape(nc, GLA_DK)).T
    for ci in range(nc):
        kts_ref[ci] = kt_end[:, ci * C:(ci + 1) * C]
        dec_ref[ci] = jnp.broadcast_to(dec_t[:, ci:ci + 1], (GLA_DK, GLA_DV))

    row = lax.broadcasted_iota(jnp.int32, (C, 1), 0)
    rowi = lax.broadcasted_iota(jnp.int32, (C, C), 0)
    coli = lax.broadcasted_iota(jnp.int32, (C, C), 1)
    lane_c = lax.broadcasted_iota(jnp.int32, (SB, C), 1)
    sub_r = lax.broadcasted_iota(jnp.int32, (SB, 1), 0)
    ones = jnp.ones((GLA_DK, C), BF16)
    neg = -jnp.inf
    scale = GLA_DK ** -0.5

    def chunk(ci, carry):
        r0 = pl.multiple_of(ci * C, C)
        q = q_ref[pl.ds(r0, C), :].astype(F32) * scale
        k = kf_ref[pl.ds(r0, C), :]
        b = b_ref[pl.ds(r0, C), :]
        v = v_ref[pl.ds(r0, C), :]

        s = s_ref[...]
        o = _dot((q * jnp.exp(b)).astype(BF16), s.astype(BF16))
        s_ref[...] = s * dec_ref[ci] + _dot(kts_ref[ci], v)

        attn = jnp.zeros((C, C), F32)
        h = C // 2
        while h >= SB:
            ref = jnp.broadcast_to(b.reshape(C // (2 * h), 2 * h, GLA_DK)[:, h - 1:h, :],
                                   (C // (2 * h), 2 * h, GLA_DK)).reshape(C, GLA_DK)
            upper = (row // h) % 2 == 1
            q_h = q * jnp.exp(jnp.where(upper, b - ref, neg))
            k_h = k * jnp.exp(jnp.where(upper, neg, ref - b))
            a_h = _dot_nt(q_h.astype(BF16), k_h.astype(BF16))
            attn = attn + (a_h if 2 * h == C else jnp.where(rowi // (2 * h) == coli // (2 * h), a_h, 0.0))
            h //= 2

        zs = []
        for bi in range(C // SB):
            s0 = bi * SB
            q_i = q[s0:s0 + SB, :]
            b_i = b[s0:s0 + SB, :]
            for j in range(SB):
                k_j = kf_ref[pl.ds(r0 + (s0 + j), 1), :]
                b_j = b_ref[pl.ds(r0 + (s0 + j), 1), :]
                zs.append(q_i * k_j * jnp.exp(jnp.where(sub_r >= j, b_i - b_j, neg)))
        zsum = _dot(jnp.concatenate(zs, axis=0).astype(BF16), ones)
        diag = []
        for bi in range(C // SB):
            acc = jnp.zeros((SB, C), F32)
            for j in range(SB):
                n0 = (bi * SB + j) * SB
                acc = acc + jnp.where(lane_c == bi * SB + j, zsum[n0:n0 + SB, :], 0.0)
            diag.append(acc)
        attn = attn + jnp.concatenate(diag, axis=0)
        o = o + _dot(attn.astype(BF16), v)

        o = o * lax.rsqrt(jnp.mean(o * o, axis=-1, keepdims=True) + EPS) * hg_ref[...]
        g = gate_ref[pl.ds(r0, C), :].astype(F32)
        o_ref[pl.ds(r0, C), :] = (o * _silu(g)).astype(o_ref.dtype)
        return carry

    lax.fori_loop(0, tb // C, chunk, 0, unroll=4)


def gla(proj, small, alpha_up, alpha_b, head_g, *, batch, seq, q_col, k_col, v_col, gate_col, tb=512):
    t = batch * seq
    nb = seq // tb
    dk, dv, h = GLA_DK, GLA_DV, GLA_HEADS
    qb, kb, vb, gb = q_col // dk, k_col // dk, v_col // dv, gate_col // dv
    tok = lambda b, hh, i: b * nb + i
    return pl.pallas_call(
        functools.partial(_gla_body, tb=tb),
        grid=(batch, h, nb),
        in_specs=[
            pl.BlockSpec((tb, dk), lambda b, hh, i: (tok(b, hh, i), qb + hh)),
            pl.BlockSpec((tb, dk), lambda b, hh, i: (tok(b, hh, i), kb + hh)),
            pl.BlockSpec((tb, dv), lambda b, hh, i: (tok(b, hh, i), vb + hh)),
            pl.BlockSpec((tb, 128), lambda b, hh, i: (tok(b, hh, i), 0)),
            pl.BlockSpec((GLA_LOWRANK, dk), lambda b, hh, i: (0, hh)),
            pl.BlockSpec((1, dk), lambda b, hh, i: (0, hh)),
            pl.BlockSpec((tb, dv), lambda b, hh, i: (tok(b, hh, i), gb + hh)),
            pl.BlockSpec((1, dv), lambda b, hh, i: (0, hh)),
        ],
        out_specs=pl.BlockSpec((tb, dv), lambda b, hh, i: (tok(b, hh, i), hh)),
        out_shape=jax.ShapeDtypeStruct((t, h * dv), BF16),
        scratch_shapes=[pltpu.VMEM((dk, dv), F32), pltpu.VMEM((tb, dk), F32), pltpu.VMEM((tb, dk), F32),
                        pltpu.VMEM((tb // GLA_CHUNK, dk, GLA_CHUNK), BF16),
                        pltpu.VMEM((tb // GLA_CHUNK, dk, dv), F32)],
        compiler_params=_cparams(("parallel", "parallel", "arbitrary")),
        name="gla",
    )(proj, proj, proj, small, alpha_up, alpha_b.reshape(1, -1), proj, head_g.reshape(1, -1))


def _fox_gate_body(s_ref, fb_ref, qa_ref, ka_ref, *, blk, col0):
    n = s_ref.shape[0] // blk
    r = lax.broadcasted_iota(jnp.int32, (blk, blk), 0)
    c = lax.broadcasted_iota(jnp.int32, (blk, blk), 1)
    tri = jnp.where(c <= r, 1.0, 0.0).astype(BF16)
    lane = lax.broadcasted_iota(jnp.int32, (blk, FOX_AUG), 1)

    def step(i, carry):
        r0 = pl.multiple_of(i * blk, blk)
        lf = _log_sigmoid(s_ref[pl.ds(r0, blk), :] + fb_ref[...])
        hi, mid, lo = _split3(lf)
        cs = _dot(tri, hi) + _dot(tri, mid) + _dot(tri, lo) + carry
        for h in range(FOX_HEADS):
            col = cs[:, col0 + h:col0 + h + 1] * LOG2E
            c0, c1, c2 = (p.astype(F32) for p in _split3(col))
            qa = jnp.where(lane == 0, c0, jnp.where(lane == 1, c1, jnp.where(lane == 2, c2,
                           jnp.where(lane < 6, 1.0, 0.0))))
            ka = jnp.where(lane < 3, 1.0, jnp.where(lane == 3, -c0, jnp.where(lane == 4, -c1,
                           jnp.where(lane == 5, -c2, 0.0))))
            qa_ref[h, pl.ds(r0, blk), :] = qa.astype(BF16)
            ka_ref[h, pl.ds(r0, blk), :] = ka.astype(BF16)
        return cs[blk - 1:blk, :]

    lax.fori_loop(0, n, step, jnp.zeros((1, s_ref.shape[1]), F32))


def fox_gate(small, fb_row, *, batch, seq, col0, blk=256):
    shp = jax.ShapeDtypeStruct((FOX_HEADS, batch * seq, FOX_AUG), BF16)
    spec = pl.BlockSpec((FOX_HEADS, seq, FOX_AUG), lambda b: (0, b, 0))
    return pl.pallas_call(
        functools.partial(_fox_gate_body, blk=blk, col0=col0),
        grid=(batch,),
        in_specs=[pl.BlockSpec((seq, 128), lambda b: (b, 0)),
                  pl.BlockSpec((1, 128), lambda b: (0, 0))],
        out_specs=[spec, spec],
        out_shape=[shp, shp],
        compiler_params=_cparams(("parallel",)),
        name="fox_gate",
    )(small, fb_row)


def _fox_body(it_ref, jt_ref, q_ref, qa_ref, k_ref, ka_ref, v_ref, gate_ref, o_ref,
              m_ref, l_ref, acc_ref, *, tq, tk, hpb):
    p = pl.program_id(2)
    i = it_ref[p]
    j = jt_ref[p]
    d = FOX_DH

    @pl.when(j == 0)
    def _():
        m_ref[...] = jnp.full_like(m_ref, -jnp.inf)
        l_ref[...] = jnp.zeros_like(l_ref)
        acc_ref[...] = jnp.zeros_like(acc_ref)

    def update(hh, masked):
        q_aug = jnp.concatenate([q_ref[:, hh * d:(hh + 1) * d], qa_ref[hh]], axis=1)
        k_aug = jnp.concatenate([k_ref[:, hh * d:(hh + 1) * d], ka_ref[hh]], axis=1)
        st = _dot_nt(k_aug, q_aug)
        if masked:
            kr = lax.broadcasted_iota(jnp.int32, (tk, tq), 0)
            qc = lax.broadcasted_iota(jnp.int32, (tk, tq), 1)
            st = jnp.where(qc >= kr, st, -jnp.inf)
        m_old = m_ref[hh]
        m_new = jnp.maximum(m_old, jnp.max(st, axis=0, keepdims=True))
        alpha = jnp.exp2(m_old - m_new)
        pt = jnp.exp2(st - m_new)
        l_ref[hh] = alpha * l_ref[hh] + jnp.sum(pt, axis=0, keepdims=True)
        acc_ref[hh] = alpha * acc_ref[hh] + _dot_tn(v_ref[:, hh * d:(hh + 1) * d], pt.astype(BF16))
        m_ref[hh] = m_new

    @pl.when(j < i)
    def _():
        for hh in range(hpb):
            update(hh, False)

    @pl.when(j == i)
    def _():
        for hh in range(hpb):
            update(hh, True)
            o = (acc_ref[hh] / l_ref[hh]).T
            g = gate_ref[:, hh * d:(hh + 1) * d].astype(F32)
            o_ref[:, hh * d:(hh + 1) * d] = (o * _silu(g)).astype(o_ref.dtype)


def fox(proj, qa, ka, *, batch, seq, q_col, k_col, v_col, gate_col, t=512, hpb=2):
    tt = batch * seq
    nb = seq // t
    d, h = FOX_DH, FOX_HEADS
    w = hpb * d
    qb, kb, gb, vb = q_col // w, k_col // w, gate_col // w, v_col // w
    pairs = [(i, j) for i in range(nb) for j in range(i + 1)]
    it = jnp.array([p[0] for p in pairs], jnp.int32)
    jt = jnp.array([p[1] for p in pairs], jnp.int32)
    grid_spec = pltpu.PrefetchScalarGridSpec(
        num_scalar_prefetch=2,
        grid=(batch, h // hpb, len(pairs)),
        in_specs=[
            pl.BlockSpec((t, w), lambda b, hh, p, it, jt: (b * nb + it[p], qb + hh)),
            pl.BlockSpec((hpb, t, FOX_AUG), lambda b, hh, p, it, jt: (hh, b * nb + it[p], 0)),
            pl.BlockSpec((t, w), lambda b, hh, p, it, jt: (b * nb + jt[p], kb + hh)),
            pl.BlockSpec((hpb, t, FOX_AUG), lambda b, hh, p, it, jt: (hh, b * nb + jt[p], 0)),
            pl.BlockSpec((t, w), lambda b, hh, p, it, jt: (b * nb + jt[p], vb + hh)),
            pl.BlockSpec((t, w), lambda b, hh, p, it, jt: (b * nb + it[p], gb + hh)),
        ],
        out_specs=pl.BlockSpec((t, w), lambda b, hh, p, it, jt: (b * nb + it[p], hh)),
        scratch_shapes=[pltpu.VMEM((hpb, 1, t), F32), pltpu.VMEM((hpb, 1, t), F32), pltpu.VMEM((hpb, d, t), F32)],
    )
    return pl.pallas_call(
        functools.partial(_fox_body, tq=t, tk=t, hpb=hpb),
        grid_spec=grid_spec,
        out_shape=jax.ShapeDtypeStruct((tt, h * d), BF16),
        compiler_params=_cparams(("parallel", "parallel", "arbitrary")),
        name="fox",
    )(it, jt, proj, qa, proj, ka, proj, proj)


def _regroup_perm(tm, steps, to_grouped):
    r = lax.broadcasted_iota(jnp.int32, (tm, tm), 0)
    c = lax.broadcasted_iota(jnp.int32, (tm, tm), 1)
    nc = tm // steps
    src = (r % nc) * steps + r // nc if to_grouped else (r % steps) * nc + r // steps
    return jnp.where(c == src, 1.0, 0.0).astype(BF16)


def _outproj_mid_body(a1_ref, a2_ref, w_ref, x_ref, g_ref, r_ref, h_ref, *, steps):
    k1 = a1_ref.shape[1]
    tm, d = x_ref.shape
    nc = tm // steps
    acc = x_ref[...] + _dot(a1_ref[...], w_ref[:k1, :]) + _dot(a2_ref[...], w_ref[k1:, :])
    r_ref[...] = acc
    normed = (acc * lax.rsqrt(jnp.mean(acc * acc, axis=-1, keepdims=True) + EPS) * g_ref[...]).astype(BF16)
    by_step = _dot(_regroup_perm(tm, steps, True), normed).astype(h_ref.dtype)
    for s in range(steps):
        h_ref[:, s * d:(s + 1) * d] = by_step[s * nc:(s + 1) * nc, :]


def outproj_mid(a1, a2, w, x, g, *, steps, tm=256, name="outproj_mid"):
    m, d = x.shape
    row = pl.BlockSpec((tm, d), lambda i: (i, 0))
    return pl.pallas_call(
        functools.partial(_outproj_mid_body, steps=steps),
        grid=(m // tm,),
        in_specs=[pl.BlockSpec((tm, a1.shape[1]), lambda i: (i, 0)),
                  pl.BlockSpec((tm, a2.shape[1]), lambda i: (i, 0)),
                  pl.BlockSpec(w.shape, lambda i: (0, 0)),
                  row,
                  pl.BlockSpec((1, d), lambda i: (0, 0))],
        out_specs=[row, pl.BlockSpec((tm // steps, steps * d), lambda i: (i, 0))],
        out_shape=[jax.ShapeDtypeStruct((m, d), F32), jax.ShapeDtypeStruct((m // steps, steps * d), BF16)],
        compiler_params=_cparams(("parallel",)),
        name=name,
    )(a1, a2, w, x, g.reshape(1, d))


def _outproj_final_body(a_ref, w_ref, x_ref, g_ref, o_ref, *, steps):
    tm = x_ref.shape[0]
    e = w_ref.shape[0]
    by_step = jnp.concatenate([a_ref[:, s * e:(s + 1) * e] for s in range(steps)], axis=0)
    a_tok = _dot(_regroup_perm(tm, steps, False), by_step).astype(BF16)
    acc = x_ref[...] + _dot(a_tok, w_ref[...])
    o_ref[...] = acc * lax.rsqrt(jnp.mean(acc * acc, axis=-1, keepdims=True) + EPS) * g_ref[...]


def outproj_final(a_grouped, w, x, g, *, steps, tm=256, name="outproj_final"):
    m, d = x.shape
    e = w.shape[0]
    row = pl.BlockSpec((tm, d), lambda i: (i, 0))
    return pl.pallas_call(
        functools.partial(_outproj_final_body, steps=steps),
        grid=(m // tm,),
        in_specs=[pl.BlockSpec((tm // steps, steps * e), lambda i: (i, 0)),
                  pl.BlockSpec(w.shape, lambda i: (0, 0)),
                  row,
                  pl.BlockSpec((1, d), lambda i: (0, 0))],
        out_specs=row,
        out_shape=jax.ShapeDtypeStruct((m, d), F32),
        compiler_params=_cparams(("parallel",)),
        name=name,
    )(a_grouped, w, x, g.reshape(1, d))


def _s5_body(u_ref, wt_ref, nt_ref, mt_ref, tab_ref, o_ref, *, cols_per_seq, n_levels):
    t, cg, n = u_ref.shape
    u = u_ref[...].reshape(t * cg, n)
    tab = tab_ref[0]
    half = S5_STATE
    x = _dot(nt_ref[0], u)
    pos = lax.broadcasted_iota(jnp.int32, (1, n), 1) % cols_per_seq
    for lvl in range(n_levels):
        sh = 1 << lvl
        xs = jnp.where(pos >= sh, pltpu.roll(x, sh, axis=1), 0.0)
        a_a = tab[:2 * half, 1 + lvl:2 + lvl]
        a_b = tab[:2 * half, 1 + n_levels + lvl:2 + n_levels + lvl]
        x = x + a_a * xs + a_b * pltpu.roll(xs, half, axis=0)
    x_prev = jnp.where(pos >= 1, pltpu.roll(x, 1, axis=1), 0.0)
    y = _dot(wt_ref[0], u) + _dot(mt_ref[0], x_prev.astype(BF16)) + tab[:, 0:1] * u.astype(F32)
    o_ref[...] = y.astype(o_ref.dtype).reshape(t, cg, n)


def s5_apply(proj_t, wt_g, nt_g, mt_g, tab, *, groups, cols_per_seq, n_levels):
    t, _, n = proj_t.shape
    cg = S5_GROUP_CH
    spec3 = lambda shp: pl.BlockSpec((1,) + shp, lambda i: (i, 0, 0))
    return pl.pallas_call(
        functools.partial(_s5_body, cols_per_seq=cols_per_seq, n_levels=n_levels),
        grid=(groups,),
        in_specs=[pl.BlockSpec((t, cg, n), lambda i: (0, i, 0)),
                  spec3(wt_g.shape[1:]), spec3(nt_g.shape[1:]), spec3(mt_g.shape[1:]), spec3(tab.shape[1:])],
        out_specs=pl.BlockSpec((t, cg, n), lambda i: (0, i, 0)),
        out_shape=jax.ShapeDtypeStruct((t, groups * cg, n), BF16),
        compiler_params=_cparams(("parallel",)),
        name="s5",
    )(proj_t, wt_g, nt_g, mt_g, tab)


def s5_operators(lam_re, lam_im, log_step, b_re, b_im, c_re, c_im, d, *, n_levels):
    hp = lax.Precision.HIGHEST
    g, p = lam_re.shape
    cg, t = S5_GROUP_CH, S5_T
    lr = jnp.minimum(lam_re.astype(F32), -1e-4)
    li = lam_im.astype(F32)
    step = jnp.exp(log_step.astype(F32))[:, None]
    mag = jnp.exp(lr * step)
    lb_re, lb_im = mag * jnp.cos(li * step), mag * jnp.sin(li * step)
    den = lr * lr + li * li
    nr, ni = lb_re - 1.0, lb_im
    coef_re = (nr * lr + ni * li) / den
    coef_im = (ni * lr - nr * li) / den
    bb_re = coef_re[..., None] * b_re - coef_im[..., None] * b_im
    bb_im = coef_re[..., None] * b_im + coef_im[..., None] * b_re

    def power(m):
        mm = m.astype(F32)[None, :, None]
        mg = jnp.exp(lr[:, None, :] * step[:, None, :] * mm)
        ang = li[:, None, :] * step[:, None, :] * mm
        return mg * jnp.cos(ang), mg * jnp.sin(ang)

    pw_re, pw_im = power(jnp.arange(t + 1))
    cp_re = c_re[:, None] * pw_re[:, :, None, :] - c_im[:, None] * pw_im[:, :, None, :]
    cp_im = c_re[:, None] * pw_im[:, :, None, :] + c_im[:, None] * pw_re[:, :, None, :]
    taps = (jnp.einsum('gxcp,gpd->gxcd', cp_re[:, :t], bb_re, precision=hp)
            - jnp.einsum('gxcp,gpd->gxcd', cp_im[:, :t], bb_im, precision=hp))
    shift = (jnp.arange(t)[None, None, :] - jnp.arange(t)[None, :, None]
             == jnp.arange(t)[:, None, None]).astype(F32)
    wt_g = jnp.einsum('xst,gxcd->gtcsd', shift, taps, precision=hp).reshape(g, t * cg, t * cg)
    pr, pi = pw_re[:, t - 1::-1], pw_im[:, t - 1::-1]
    bt_re, bt_im = bb_re[:, :, None, :], bb_im[:, :, None, :]
    prt, pit = pr.transpose(0, 2, 1)[:, :, :, None], pi.transpose(0, 2, 1)[:, :, :, None]
    n_re = (prt * bt_re - pit * bt_im).reshape(g, p, t * cg)
    n_im = (prt * bt_im + pit * bt_re).reshape(g, p, t * cg)
    nt_g = jnp.concatenate([n_re, n_im], axis=1)
    m_re = cp_re[:, 1:].reshape(g, t * cg, p)
    m_im = cp_im[:, 1:].reshape(g, t * cg, p)
    mt_g = jnp.concatenate([m_re, -m_im], axis=2)
    ar, ai = power(t * (2 ** jnp.arange(n_levels)))
    a_a = jnp.concatenate([ar, ar], axis=-1).transpose(0, 2, 1)
    a_b = jnp.concatenate([-ai, ai], axis=-1).transpose(0, 2, 1)
    d_col = jnp.tile(d.astype(F32).reshape(g, cg), (1, t))[:, :, None]
    rows = cg * t
    padr = lambda a: jnp.pad(a, ((0, 0), (0, rows - a.shape[1]), (0, 0)))
    tab = jnp.concatenate([d_col, padr(a_a), padr(a_b)], axis=2)
    tab = jnp.pad(tab, ((0, 0), (0, 0), (0, S5_TAB - tab.shape[2])))
    return wt_g.astype(BF16), nt_g.astype(BF16), mt_g.astype(BF16), tab


def _gelu_tanh(y):
    return 0.5 * y * (1.0 + jnp.tanh(math.sqrt(2.0 / math.pi) * (y + 0.044715 * (y * y * y))))


def _glu_body(y_ref, w_ref, b_ref, gate_ref, o_ref, z_ref, zb_ref):
    z = _gelu_tanh(y_ref[...].astype(F32))
    z_ref[...] = z
    zb_ref[...] = z.astype(BF16)
    for f0 in range(0, o_ref.shape[1], MM_SUB):
        rows = slice(f0, f0 + MM_SUB)
        lin = _dot(w_ref[rows, :], zb_ref[...]) + b_ref[rows, :]
        out = z_ref[rows, :] * _sigmoid(lin) * _silu(gate_ref[rows, :].astype(F32))
        o_ref[:, rows] = out.T.astype(o_ref.dtype)


def glu_gate(y_t, w_t, b, proj_t, *, gate_row, tm=512):
    s, e, n = y_t.shape
    gb = gate_row // e
    return pl.pallas_call(
        _glu_body,
        grid=(s, n // tm),
        in_specs=[pl.BlockSpec((None, e, tm), lambda t, i: (t, 0, i)),
                  pl.BlockSpec((e, e), lambda t, i: (0, 0)),
                  pl.BlockSpec((e, 1), lambda t, i: (0, 0)),
                  pl.BlockSpec((None, e, tm), lambda t, i: (t, gb, i))],
        out_specs=pl.BlockSpec((tm, e), lambda t, i: (i, t)),
        out_shape=jax.ShapeDtypeStruct((n, s * e), BF16),
        scratch_shapes=[pltpu.VMEM((e, tm), F32), pltpu.VMEM((e, tm), BF16)],
        compiler_params=_cparams(("parallel", "parallel")),
        name="glu",
    )(y_t, w_t, b.reshape(e, 1), proj_t)


def kernel(x, ab_norm_g, ab_w_in, gla_alpha_up, gla_alpha_b, gla_head_g, fox_f_b, ab_w_out, c_norm_g, c_w_in, s5_lambda_re, s5_lambda_im, s5_log_step, s5_b_re, s5_b_im, s5_c_re, s5_c_im, s5_d, glu_w, glu_b, c_w_out, final_norm_g):
    batch, seq, d = x.shape
    t = batch * seq
    x2 = x.reshape(t, d)

    hk, hv, fw = GLA_HEADS * GLA_DK, GLA_HEADS * GLA_DV, FOX_HEADS * FOX_DH
    sizes = (hk, hk, hv, GLA_LOWRANK, hv, fw, fw, fw, FOX_HEADS, fw)
    offs = [0]
    for s in sizes:
        offs.append(offs[-1] + s)
    w_main, w_small = ab_weight_prep(ab_w_in, offs)
    c_gq, c_gk, c_gv, c_gg = 0, hk, 2 * hk, 2 * hk + hv
    c_fq = c_gg + hv
    c_fk, c_fv, c_fg = c_fq + fw, c_fq + 2 * fw, c_fq + 3 * fw

    h0, small = rmsnorm_bf16(x2, ab_norm_g[0], w_small)
    proj = matmul(h0, w_main, BF16, tn=w_main.shape[1] // 2, name="ab_in")

    o_gla = gla(proj, small, gla_alpha_up[0], gla_alpha_b[0], gla_head_g[0],
                batch=batch, seq=seq, q_col=c_gq, k_col=c_gk, v_col=c_gv, gate_col=c_gg)

    fb_row = jnp.zeros((1, 128), F32).at[0, GLA_LOWRANK:GLA_LOWRANK + FOX_HEADS].set(fox_f_b[0])
    qa, ka = fox_gate(small, fb_row, batch=batch, seq=seq, col0=GLA_LOWRANK)
    o_fox = fox(proj, qa, ka, batch=batch, seq=seq, q_col=c_fq, k_col=c_fk, v_col=c_fv, gate_col=c_fg)

    x1, h1g = outproj_mid(o_gla, o_fox, ab_w_out[0].astype(BF16), x2, c_norm_g[0], steps=S5_T, name="ab_out")

    e = d
    groups = e // S5_GROUP_CH
    n_chunks = t // S5_T
    cols_per_seq = seq // S5_T
    n_levels = max(1, (cols_per_seq - 1).bit_length())
    proj_t = matmul_t_grouped(c_w_in[0].T.astype(BF16), h1g, S5_T, BF16, name="c_in")
    ops = s5_operators(s5_lambda_re[0], s5_lambda_im[0], s5_log_step[0], s5_b_re[0], s5_b_im[0],
                       s5_c_re[0], s5_c_im[0], s5_d[0], n_levels=n_levels)
    y_t = s5_apply(proj_t, *ops, groups=groups, cols_per_seq=cols_per_seq, n_levels=n_levels)
    zz = glu_gate(y_t, glu_w[0].T.astype(BF16), glu_b[0], proj_t, gate_row=e)
    out = outproj_final(zz, c_w_out[0].astype(BF16), x1, final_norm_g, steps=S5_T, name="c_out")
    return out.reshape(batch, seq, d)
```

```python
import functools
import math

import jax
import jax.numpy as jnp
from jax import lax
from jax.experimental import pallas as pl
from jax.experimental.pallas import tpu as pltpu

EPS = 1e-6
F32 = jnp.float32
BF16 = jnp.bfloat16

GLA_HEADS = 4
GLA_DK = 128
GLA_DV = 256
GLA_LOWRANK = 16
GLA_TAU = 16.0
GLA_CHUNK = 64
GLA_SUB = 8
FOX_HEADS = 8
FOX_DH = 128
FOX_AUG = 128
S5_GROUP_CH = 16
S5_STATE = 64
S5_T = 16
S5_TAB = 32

VMEM_LIMIT = 56 * 1024 * 1024

NT_DIMS = (((1,), (1,)), ((), ()))
TN_DIMS = (((0,), (0,)), ((), ()))
LOG2E = math.log2(math.e)


def _cparams(sem):
    return pltpu.CompilerParams(dimension_semantics=sem, vmem_limit_bytes=VMEM_LIMIT)


def _dot(a, b):
    return jnp.dot(a, b, preferred_element_type=F32)


def _dot_nt(a, b):
    return lax.dot_general(a, b, NT_DIMS, preferred_element_type=F32)


def _dot_tn(a, b):
    return lax.dot_general(a, b, TN_DIMS, preferred_element_type=F32)


def _log_sigmoid(z):
    return -(jnp.maximum(-z, 0.0) + jnp.log1p(jnp.exp(-jnp.abs(z))))


def _sigmoid(z):
    return 1.0 / (1.0 + jnp.exp(-z))


def _silu(z):
    return z * _sigmoid(z)


def _split3(x):
    hi = x.astype(BF16)
    r1 = x - hi.astype(F32)
    mid = r1.astype(BF16)
    lo = (r1 - mid.astype(F32)).astype(BF16)
    return hi, mid, lo


def _rmsnorm_body(x_ref, g_ref, ws_ref, o_ref, s_ref):
    x = x_ref[...]
    ms = jnp.mean(x * x, axis=-1, keepdims=True)
    h = (x * lax.rsqrt(ms + EPS) * g_ref[...]).astype(o_ref.dtype)
    o_ref[...] = h
    s_ref[...] = _dot_nt(h, ws_ref[...])


def rmsnorm_bf16(x, g, wt_small, tm=512):
    m, d = x.shape
    ns = wt_small.shape[0]
    return pl.pallas_call(
        _rmsnorm_body,
        grid=(m // tm,),
        in_specs=[pl.BlockSpec((tm, d), lambda i: (i, 0)),
                  pl.BlockSpec((1, d), lambda i: (0, 0)),
                  pl.BlockSpec((ns, d), lambda i: (0, 0))],
        out_specs=[pl.BlockSpec((tm, d), lambda i: (i, 0)),
                   pl.BlockSpec((tm, ns), lambda i: (i, 0))],
        out_shape=[jax.ShapeDtypeStruct((m, d), BF16), jax.ShapeDtypeStruct((m, ns), F32)],
        compiler_params=_cparams(("parallel",)),
        name="rmsnorm",
    )(x, g.reshape(1, d), wt_small)


def _ab_wprep_body(w_ref, o_ref, s_ref, *, offs):
    w = w_ref[0]
    seg = lambda n: w[offs[n]:offs[n + 1], :]
    o_ref[...] = jnp.concatenate(
        [w[:offs[3], :], seg(4), seg(5) * (FOX_DH ** -0.5 * LOG2E), w[offs[6]:offs[8], :], seg(9)],
        axis=0).astype(o_ref.dtype)
    pad = s_ref.shape[0] - (offs[4] - offs[3]) - (offs[9] - offs[8])
    s_ref[...] = jnp.concatenate([seg(3), seg(8), jnp.zeros((pad, w.shape[1]), F32)], axis=0).astype(s_ref.dtype)


def ab_weight_prep(w_t3, offs, tk=256):
    _, n_in, d = w_t3.shape
    n_main = n_in - (offs[4] - offs[3]) - (offs[9] - offs[8])
    return pl.pallas_call(
        functools.partial(_ab_wprep_body, offs=tuple(offs)),
        grid=(d // tk,),
        in_specs=[pl.BlockSpec((1, n_in, tk), lambda i: (0, 0, i))],
        out_specs=[pl.BlockSpec((n_main, tk), lambda i: (0, i)), pl.BlockSpec((128, tk), lambda i: (0, i))],
        out_shape=[jax.ShapeDtypeStruct((n_main, d), BF16), jax.ShapeDtypeStruct((128, d), BF16)],
        compiler_params=_cparams(("parallel",)),
        name="ab_wprep",
    )(w_t3)


MM_SUB = 512


def _mm_body(a_ref, wt_ref, o_ref):
    a = a_ref[...]
    for c0 in range(0, o_ref.shape[1], MM_SUB):
        o_ref[:, c0:c0 + MM_SUB] = _dot_nt(a, wt_ref[c0:c0 + MM_SUB, :]).astype(o_ref.dtype)


def matmul(a, wt, out_dtype, tm=512, tn=512, name="matmul"):
    m, k = a.shape
    n = wt.shape[0]
    return pl.pallas_call(
        _mm_body,
        grid=(n // tn, m // tm),
        in_specs=[pl.BlockSpec((tm, k), lambda j, i: (i, 0)),
                  pl.BlockSpec((tn, k), lambda j, i: (j, 0))],
        out_specs=pl.BlockSpec((tm, tn), lambda j, i: (i, j)),
        out_shape=jax.ShapeDtypeStruct((m, n), out_dtype),
        compiler_params=_cparams(("parallel", "parallel")),
        name=name,
    )(a, wt)


def _transpose_into(wt_ref, w_ref):
    for c0 in range(0, w_ref.shape[1], MM_SUB):
        wt_ref[c0:c0 + MM_SUB, :] = w_ref[:, c0:c0 + MM_SUB].T


def _mm_t_body(w_ref, a_ref, o_ref, wt_ref):
    @pl.when((pl.program_id(1) == 0) & (pl.program_id(2) == 0))
    def _():
        _transpose_into(wt_ref, w_ref)

    a = a_ref[...]
    for r0 in range(0, o_ref.shape[0], MM_SUB):
        o_ref[r0:r0 + MM_SUB, :] = _dot_nt(wt_ref[r0:r0 + MM_SUB, :], a).astype(o_ref.dtype)


def matmul_t_grouped(w, a2, steps, out_dtype, tm=512, tn=2048, name="matmul_tg"):
    k, nf = w.shape
    n = a2.shape[0]
    return pl.pallas_call(
        _mm_t_body,
        grid=(nf // tn, steps, n // tm),
        in_specs=[pl.BlockSpec((k, tn), lambda j, t, i: (0, j)),
                  pl.BlockSpec((tm, k), lambda j, t, i: (i, t))],
        out_specs=pl.BlockSpec((None, tn, tm), lambda j, t, i: (t, j, i)),
        out_shape=jax.ShapeDtypeStruct((steps, nf, n), out_dtype),
        scratch_shapes=[pltpu.VMEM((tn, k), BF16)],
        compiler_params=_cparams(("arbitrary", "arbitrary", "arbitrary")),
        name=name,
    )(w, a2)


def _gla_body(q_ref, k_ref, v_ref, glr_ref, aup_ref, ab_ref, gate_ref, hg_ref, o_ref,
              s_ref, b_ref, kf_ref, kts_ref, dec_ref, *, tb):
    ib = pl.program_id(2)
    C, SB = GLA_CHUNK, GLA_SUB

    @pl.when(ib == 0)
    def _():
        s_ref[...] = jnp.zeros_like(s_ref)

    z = _dot(glr_ref[:, :GLA_LOWRANK].astype(BF16), aup_ref[...].astype(BF16)) + ab_ref[...]
    la = _log_sigmoid(z) * (1.0 / GLA_TAU)
    tw = 2 * C
    r = lax.broadcasted_iota(jnp.int32, (tw, tw), 0)
    c = lax.broadcasted_iota(jnp.int32, (tw, tw), 1)
    tri = jnp.where((r // C == c // C) & (c <= r), 1.0, 0.0).astype(BF16)
    hi = la.astype(BF16)
    lo = (la - hi.astype(F32)).astype(BF16)
    b_blk = jnp.concatenate(
        [_dot(tri, hi[r0:r0 + tw]) + _dot(tri, lo[r0:r0 + tw]) for r0 in range(0, tb, tw)], axis=0)
    k_blk = k_ref[...].astype(F32)
    b_ref[...] = b_blk
    kf_ref[...] = k_blk
    nc = tb // C
    b3 = b_blk.reshape(nc, C, GLA_DK)
    b_end = b3[:, C - 1:C, :]
    k_end = (k_blk.reshape(nc, C, GLA_DK) * jnp.exp(b_end - b3)).reshape(tb, GLA_DK)
    kt_end = k_end.T.astype(BF16)
    dec_t = jnp.exp(b_end.reshape(nc, GLA_DK)).T
    for ci in range(nc):
        kts_ref[ci] = kt_end[:, ci * C:(ci + 1) * C]
        dec_ref[ci] = jnp.broadcast_to(dec_t[:, ci:ci + 1], (GLA_DK, GLA_DV))

    row = lax.broadcasted_iota(jnp.int32, (C, 1), 0)
    rowi = lax.broadcasted_iota(jnp.int32, (C, C), 0)
    coli = lax.broadcasted_iota(jnp.int32, (C, C), 1)
    lane_c = lax.broadcasted_iota(jnp.int32, (SB, C), 1)
    sub_r = lax.broadcasted_iota(jnp.int32, (SB, 1), 0)
    ones = jnp.ones((GLA_DK, C), BF16)
    neg = -jnp.inf
    scale = GLA_DK ** -0.5

    def chunk(ci, carry):
        r0 = pl.multiple_of(ci * C, C)
        q = q_ref[pl.ds(r0, C), :].astype(F32) * scale
        k = kf_ref[pl.ds(r0, C), :]
        b = b_ref[pl.ds(r0, C), :]
        v = v_ref[pl.ds(r0, C), :]

        s = s_ref[...]
        o = _dot((q * jnp.exp(b)).astype(BF16), s.astype(BF16))
        s_ref[...] = s * dec_ref[ci] + _dot(kts_ref[ci], v)

        attn = jnp.zeros((C, C), F32)
        h = C // 2
        while h >= SB:
            ref = jnp.broadcast_to(b.reshape(C // (2 * h), 2 * h, GLA_DK)[:, h - 1:h, :],
                                   (C // (2 * h), 2 * h, GLA_DK)).reshape(C, GLA_DK)
            upper = (row // h) % 2 == 1
            q_h = q * jnp.exp(jnp.where(upper, b - ref, neg))
            k_h = k * jnp.exp(jnp.where(upper, neg, ref - b))
            a_h = _dot_nt(q_h.astype(BF16), k_h.astype(BF16))
            attn = attn + (a_h if 2 * h == C else jnp.where(rowi // (2 * h) == coli // (2 * h), a_h, 0.0))
            h //= 2

        zs = []
        for bi in range(C // SB):
            s0 = bi * SB
            q_i = q[s0:s0 + SB, :]
            b_i = b[s0:s0 + SB, :]
            for j in range(SB):
                k_j = kf_ref[pl.ds(r0 + (s0 + j), 1), :]
                b_j = b_ref[pl.ds(r0 + (s0 + j), 1), :]
                zs.append(q_i * k_j * jnp.exp(jnp.where(sub_r >= j, b_i - b_j, neg)))
        zsum = _dot(jnp.concatenate(zs, axis=0).astype(BF16), ones)
        diag = []
        for bi in range(C // SB):
            acc = jnp.zeros((SB, C), F32)
            for j in range(SB):
                n0 = (bi * SB + j) * SB
                acc = acc + jnp.where(lane_c == bi * SB + j, zsum[n0:n0 + SB, :], 0.0)
            diag.append(acc)
        attn = attn + jnp.concatenate(diag, axis=0)
        o = o + _dot(attn.astype(BF16), v)

        o = o * lax.rsqrt(jnp.mean(o * o, axis=-1, keepdims=True) + EPS) * hg_ref[...]
        g = gate_ref[pl.ds(r0, C), :].astype(F32)
        o_ref[pl.ds(r0, C), :] = (o * _silu(g)).astype(o_ref.dtype)
        return carry

    lax.fori_loop(0, tb // C, chunk, 0, unroll=4)


def gla(proj, small, alpha_up, alpha_b, head_g, *, batch, seq, q_col, k_col, v_col, gate_col, tb=512):
    t = batch * seq
    nb = seq // tb
    dk, dv, h = GLA_DK, GLA_DV, GLA_HEADS
    qb, kb, vb, gb = q_col // dk, k_col // dk, v_col // dv, gate_col // dv
    tok = lambda b, hh, i: b * nb + i
    return pl.pallas_call(
        functools.partial(_gla_body, tb=tb),
        grid=(batch, h, nb),
        in_specs=[
            pl.BlockSpec((tb, dk), lambda b, hh, i: (tok(b, hh, i), qb + hh)),
            pl.BlockSpec((tb, dk), lambda b, hh, i: (tok(b, hh, i), kb + hh)),
            pl.BlockSpec((tb, dv), lambda b, hh, i: (tok(b, hh, i), vb + hh)),
            pl.BlockSpec((tb, 128), lambda b, hh, i: (tok(b, hh, i), 0)),
            pl.BlockSpec((GLA_LOWRANK, dk), lambda b, hh, i: (0, hh)),
            pl.BlockSpec((1, dk), lambda b, hh, i: (0, hh)),
            pl.BlockSpec((tb, dv), lambda b, hh, i: (tok(b, hh, i), gb + hh)),
            pl.BlockSpec((1, dv), lambda b, hh, i: (0, hh)),
        ],
        out_specs=pl.BlockSpec((tb, dv), lambda b, hh, i: (tok(b, hh, i), hh)),
        out_shape=jax.ShapeDtypeStruct((t, h * dv), BF16),
        scratch_shapes=[pltpu.VMEM((dk, dv), F32), pltpu.VMEM((tb, dk), F32), pltpu.VMEM((tb, dk), F32),
                        pltpu.VMEM((tb // GLA_CHUNK, dk, GLA_CHUNK), BF16),
                        pltpu.VMEM((tb // GLA_CHUNK, dk, dv), F32)],
        compiler_params=_cparams(("parallel", "parallel", "arbitrary")),
        name="gla",
    )(proj, proj, proj, small, alpha_up, alpha_b.reshape(1, -1), proj, head_g.reshape(1, -1))


def _fox_gate_body(s_ref, fb_ref, qa_ref, ka_ref, *, blk, col0):
    n = s_ref.shape[0] // blk
    r = lax.broadcasted_iota(jnp.int32, (blk, blk), 0)
    c = lax.broadcasted_iota(jnp.int32, (blk, blk), 1)
    tri = jnp.where(c <= r, 1.0, 0.0).astype(BF16)
    lane = lax.broadcasted_iota(jnp.int32, (blk, FOX_AUG), 1)

    def step(i, carry):
        r0 = pl.multiple_of(i * blk, blk)
        lf = _log_sigmoid(s_ref[pl.ds(r0, blk), :] + fb_ref[...])
        hi, mid, lo = _split3(lf)
        cs = _dot(tri, hi) + _dot(tri, mid) + _dot(tri, lo) + carry
        for h in range(FOX_HEADS):
            col = cs[:, col0 + h:col0 + h + 1] * LOG2E
            c0, c1, c2 = (p.astype(F32) for p in _split3(col))
            qa = jnp.where(lane == 0, c0, jnp.where(lane == 1, c1, jnp.where(lane == 2, c2,
                           jnp.where(lane < 6, 1.0, 0.0))))
            ka = jnp.where(lane < 3, 1.0, jnp.where(lane == 3, -c0, jnp.where(lane == 4, -c1,
                           jnp.where(lane == 5, -c2, 0.0))))
            qa_ref[h, pl.ds(r0, blk), :] = qa.astype(BF16)
            ka_ref[h, pl.ds(r0, blk), :] = ka.astype(BF16)
        return cs[blk - 1:blk, :]

    lax.fori_loop(0, n, step, jnp.zeros((1, s_ref.shape[1]), F32))


def fox_gate(small, fb_row, *, batch, seq, col0, blk=256):
    shp = jax.ShapeDtypeStruct((FOX_HEADS, batch * seq, FOX_AUG), BF16)
    spec = pl.BlockSpec((FOX_HEADS, seq, FOX_AUG), lambda b: (0, b, 0))
    return pl.pallas_call(
        functools.partial(_fox_gate_body, blk=blk, col0=col0),
        grid=(batch,),
        in_specs=[pl.BlockSpec((seq, 128), lambda b: (b, 0)),
                  pl.BlockSpec((1, 128), lambda b: (0, 0))],
        out_specs=[spec, spec],
        out_shape=[shp, shp],
        compiler_params=_cparams(("parallel",)),
        name="fox_gate",
    )(small, fb_row)


def _fox_body(it_ref, jt_ref, q_ref, qa_ref, k_ref, ka_ref, v_ref, gate_ref, o_ref,
              m_ref, l_ref, acc_ref, *, tq, tk, hpb):
    p = pl.program_id(2)
    i = it_ref[p]
    j = jt_ref[p]
    d = FOX_DH

    @pl.when(j == 0)
    def _():
        m_ref[...] = jnp.full_like(m_ref, -jnp.inf)
        l_ref[...] = jnp.zeros_like(l_ref)
        acc_ref[...] = jnp.zeros_like(acc_ref)

    def update(hh, masked):
        q_aug = jnp.concatenate([q_ref[:, hh * d:(hh + 1) * d], qa_ref[hh]], axis=1)
        k_aug = jnp.concatenate([k_ref[:, hh * d:(hh + 1) * d], ka_ref[hh]], axis=1)
        st = _dot_nt(k_aug, q_aug)
        if masked:
            kr = lax.broadcasted_iota(jnp.int32, (tk, tq), 0)
            qc = lax.broadcasted_iota(jnp.int32, (tk, tq), 1)
            st = jnp.where(qc >= kr, st, -jnp.inf)
        m_old = m_ref[hh]
        m_new = jnp.maximum(m_old, jnp.max(st, axis=0, keepdims=True))
        alpha = jnp.exp2(m_old - m_new)
        pt = jnp.exp2(st - m_new)
        l_ref[hh] = alpha * l_ref[hh] + jnp.sum(pt, axis=0, keepdims=True)
        acc_ref[hh] = alpha * acc_ref[hh] + _dot_tn(v_ref[:, hh * d:(hh + 1) * d], pt.astype(BF16))
        m_ref[hh] = m_new

    @pl.when(j < i)
    def _():
        for hh in range(hpb):
            update(hh, False)

    @pl.when(j == i)
    def _():
        for hh in range(hpb):
            update(hh, True)
            o = (acc_ref[hh] / l_ref[hh]).T
            g = gate_ref[:, hh * d:(hh + 1) * d].astype(F32)
            o_ref[:, hh * d:(hh + 1) * d] = (o * _silu(g)).astype(o_ref.dtype)


def fox(proj, qa, ka, *, batch, seq, q_col, k_col, v_col, gate_col, t=512, hpb=2):
    tt = batch * seq
    nb = seq // t
    d, h = FOX_DH, FOX_HEADS
    w = hpb * d
    qb, kb, gb, vb = q_col // w, k_col // w, gate_col // w, v_col // w
    pairs = [(i, j) for i in range(nb) for j in range(i + 1)]
    it = jnp.array([p[0] for p in pairs], jnp.int32)
    jt = jnp.array([p[1] for p in pairs], jnp.int32)
    grid_spec = pltpu.PrefetchScalarGridSpec(
        num_scalar_prefetch=2,
        grid=(batch, h // hpb, len(pairs)),
        in_specs=[
            pl.BlockSpec((t, w), lambda b, hh, p, it, jt: (b * nb + it[p], qb + hh)),
            pl.BlockSpec((hpb, t, FOX_AUG), lambda b, hh, p, it, jt: (hh, b * nb + it[p], 0)),
            pl.BlockSpec((t, w), lambda b, hh, p, it, jt: (b * nb + jt[p], kb + hh)),
            pl.BlockSpec((hpb, t, FOX_AUG), lambda b, hh, p, it, jt: (hh, b * nb + jt[p], 0)),
            pl.BlockSpec((t, w), lambda b, hh, p, it, jt: (b * nb + jt[p], vb + hh)),
            pl.BlockSpec((t, w), lambda b, hh, p, it, jt: (b * nb + it[p], gb + hh)),
        ],
        out_specs=pl.BlockSpec((t, w), lambda b, hh, p, it, jt: (b * nb + it[p], hh)),
        scratch_shapes=[pltpu.VMEM((hpb, 1, t), F32), pltpu.VMEM((hpb, 1, t), F32), pltpu.VMEM((hpb, d, t), F32)],
    )
    return pl.pallas_call(
        functools.partial(_fox_body, tq=t, tk=t, hpb=hpb),
        grid_spec=grid_spec,
        out_shape=jax.ShapeDtypeStruct((tt, h * d), BF16),
        compiler_params=_cparams(("parallel", "parallel", "arbitrary")),
        name="fox",
    )(it, jt, proj, qa, proj, ka, proj, proj)


def _regroup_perm(tm, steps, to_grouped):
    r = lax.broadcasted_iota(jnp.int32, (tm, tm), 0)
    c = lax.broadcasted_iota(jnp.int32, (tm, tm), 1)
    nc = tm // steps
    src = (r % nc) * steps + r // nc if to_grouped else (r % steps) * nc + r // steps
    return jnp.where(c == src, 1.0, 0.0).astype(BF16)


def _outproj_mid_body(a1_ref, a2_ref, w_ref, x_ref, g_ref, r_ref, h_ref, *, steps):
    k1 = a1_ref.shape[1]
    tm, d = x_ref.shape
    nc = tm // steps
    acc = x_ref[...] + _dot(a1_ref[...], w_ref[:k1, :]) + _dot(a2_ref[...], w_ref[k1:, :])
    r_ref[...] = acc
    normed = (acc * lax.rsqrt(jnp.mean(acc * acc, axis=-1, keepdims=True) + EPS) * g_ref[...]).astype(BF16)
    by_step = _dot(_regroup_perm(tm, steps, True), normed).astype(h_ref.dtype)
    for s in range(steps):
        h_ref[:, s * d:(s + 1) * d] = by_step[s * nc:(s + 1) * nc, :]


def outproj_mid(a1, a2, w, x, g, *, steps, tm=256, name="outproj_mid"):
    m, d = x.shape
    row = pl.BlockSpec((tm, d), lambda i: (i, 0))
    return pl.pallas_call(
        functools.partial(_outproj_mid_body, steps=steps),
        grid=(m // tm,),
        in_specs=[pl.BlockSpec((tm, a1.shape[1]), lambda i: (i, 0)),
                  pl.BlockSpec((tm, a2.shape[1]), lambda i: (i, 0)),
                  pl.BlockSpec(w.shape, lambda i: (0, 0)),
                  row,
                  pl.BlockSpec((1, d), lambda i: (0, 0))],
        out_specs=[row, pl.BlockSpec((tm // steps, steps * d), lambda i: (i, 0))],
        out_shape=[jax.ShapeDtypeStruct((m, d), F32), jax.ShapeDtypeStruct((m // steps, steps * d), BF16)],
        compiler_params=_cparams(("parallel",)),
        name=name,
    )(a1, a2, w, x, g.reshape(1, d))


def _outproj_final_body(a_ref, w_ref, x_ref, g_ref, o_ref, *, steps):
    tm = x_ref.shape[0]
    e = w_ref.shape[0]
    by_step = jnp.concatenate([a_ref[:, s * e:(s + 1) * e] for s in range(steps)], axis=0)
    a_tok = _dot(_regroup_perm(tm, steps, False), by_step).astype(BF16)
    acc = x_ref[...] + _dot(a_tok, w_ref[...])
    o_ref[...] = acc * lax.rsqrt(jnp.mean(acc * acc, axis=-1, keepdims=True) + EPS) * g_ref[...]


def outproj_final(a_grouped, w, x, g, *, steps, tm=256, name="outproj_final"):
    m, d = x.shape
    e = w.shape[0]
    row = pl.BlockSpec((tm, d), lambda i: (i, 0))
    return pl.pallas_call(
        functools.partial(_outproj_final_body, steps=steps),
        grid=(m // tm,),
        in_specs=[pl.BlockSpec((tm // steps, steps * e), lambda i: (i, 0)),
                  pl.BlockSpec(w.shape, lambda i: (0, 0)),
                  row,
                  pl.BlockSpec((1, d), lambda i: (0, 0))],
        out_specs=row,
        out_shape=jax.ShapeDtypeStruct((m, d), F32),
        compiler_params=_cparams(("parallel",)),
        name=name,
    )(a_grouped, w, x, g.reshape(1, d))


def _s5_body(u_ref, wt_ref, nt_ref, mt_ref, tab_ref, o_ref, *, cols_per_seq, n_levels):
    t, cg, n = u_ref.shape
    u = u_ref[...].reshape(t * cg, n)
    tab = tab_ref[0]
    half = S5_STATE
    x = _dot(nt_ref[0], u)
    pos = lax.broadcasted_iota(jnp.int32, (1, n), 1) % cols_per_seq
    for lvl in range(n_levels):
        sh = 1 << lvl
        xs = jnp.where(pos >= sh, pltpu.roll(x, sh, axis=1), 0.0)
        a_a = tab[:2 * half, 1 + lvl:2 + lvl]
        a_b = tab[:2 * half, 1 + n_levels + lvl:2 + n_levels + lvl]
        x = x + a_a * xs + a_b * pltpu.roll(xs, half, axis=0)
    x_prev = jnp.where(pos >= 1, pltpu.roll(x, 1, axis=1), 0.0)
    y = _dot(wt_ref[0], u) + _dot(mt_ref[0], x_prev.astype(BF16)) + tab[:, 0:1] * u.astype(F32)
    o_ref[...] = y.astype(o_ref.dtype).reshape(t, cg, n)


def s5_apply(proj_t, wt_g, nt_g, mt_g, tab, *, groups, cols_per_seq, n_levels):
    t, _, n = proj_t.shape
    cg = S5_GROUP_CH
    spec3 = lambda shp: pl.BlockSpec((1,) + shp, lambda i: (i, 0, 0))
    return pl.pallas_call(
        functools.partial(_s5_body, cols_per_seq=cols_per_seq, n_levels=n_levels),
        grid=(groups,),
        in_specs=[pl.BlockSpec((t, cg, n), lambda i: (0, i, 0)),
                  spec3(wt_g.shape[1:]), spec3(nt_g.shape[1:]), spec3(mt_g.shape[1:]), spec3(tab.shape[1:])],
        out_specs=pl.BlockSpec((t, cg, n), lambda i: (0, i, 0)),
        out_shape=jax.ShapeDtypeStruct((t, groups * cg, n), BF16),
        compiler_params=_cparams(("parallel",)),
        name="s5",
    )(proj_t, wt_g, nt_g, mt_g, tab)


def s5_operators(lam_re, lam_im, log_step, b_re, b_im, c_re, c_im, d, *, n_levels):
    hp = lax.Precision.HIGHEST
    g, p = lam_re.shape
    cg, t = S5_GROUP_CH, S5_T
    lr = jnp.minimum(lam_re.astype(F32), -1e-4)
    li = lam_im.astype(F32)
    step = jnp.exp(log_step.astype(F32))[:, None]
    mag = jnp.exp(lr * step)
    lb_re, lb_im = mag * jnp.cos(li * step), mag * jnp.sin(li * step)
    den = lr * lr + li * li
    nr, ni = lb_re - 1.0, lb_im
    coef_re = (nr * lr + ni * li) / den
    coef_im = (ni * lr - nr * li) / den
    bb_re = coef_re[..., None] * b_re - coef_im[..., None] * b_im
    bb_im = coef_re[..., None] * b_im + coef_im[..., None] * b_re

    def power(m):
        mm = m.astype(F32)[None, :, None]
        mg = jnp.exp(lr[:, None, :] * step[:, None, :] * mm)
        ang = li[:, None, :] * step[:, None, :] * mm
        return mg * jnp.cos(ang), mg * jnp.sin(ang)

    pw_re, pw_im = power(jnp.arange(t + 1))
    cp_re = c_re[:, None] * pw_re[:, :, None, :] - c_im[:, None] * pw_im[:, :, None, :]
    cp_im = c_re[:, None] * pw_im[:, :, None, :] + c_im[:, None] * pw_re[:, :, None, :]
    taps = (jnp.einsum('gxcp,gpd->gxcd', cp_re[:, :t], bb_re, precision=hp)
            - jnp.einsum('gxcp,gpd->gxcd', cp_im[:, :t], bb_im, precision=hp))
    shift = (jnp.arange(t)[None, None, :] - jnp.arange(t)[None, :, None]
             == jnp.arange(t)[:, None, None]).astype(F32)
    wt_g = jnp.einsum('xst,gxcd->gtcsd', shift, taps, precision=hp).reshape(g, t * cg, t * cg)
    pr, pi = pw_re[:, t - 1::-1], pw_im[:, t - 1::-1]
    bt_re, bt_im = bb_re[:, :, None, :], bb_im[:, :, None, :]
    prt, pit = pr.transpose(0, 2, 1)[:, :, :, None], pi.transpose(0, 2, 1)[:, :, :, None]
    n_re = (prt * bt_re - pit * bt_im).reshape(g, p, t * cg)
    n_im = (prt * bt_im + pit * bt_re).reshape(g, p, t * cg)
    nt_g = jnp.concatenate([n_re, n_im], axis=1)
    m_re = cp_re[:, 1:].reshape(g, t * cg, p)
    m_im = cp_im[:, 1:].reshape(g, t * cg, p)
    mt_g = jnp.concatenate([m_re, -m_im], axis=2)
    ar, ai = power(t * (2 ** jnp.arange(n_levels)))
    a_a = jnp.concatenate([ar, ar], axis=-1).transpose(0, 2, 1)
    a_b = jnp.concatenate([-ai, ai], axis=-1).transpose(0, 2, 1)
    d_col = jnp.tile(d.astype(F32).reshape(g, cg), (1, t))[:, :, None]
    rows = cg * t
    padr = lambda a: jnp.pad(a, ((0, 0), (0, rows - a.shape[1]), (0, 0)))
    tab = jnp.concatenate([d_col, padr(a_a), padr(a_b)], axis=2)
    tab = jnp.pad(tab, ((0, 0), (0, 0), (0, S5_TAB - tab.shape[2])))
    return wt_g.astype(BF16), nt_g.astype(BF16), mt_g.astype(BF16), tab


def _gelu_tanh(y):
    return 0.5 * y * (1.0 + jnp.tanh(math.sqrt(2.0 / math.pi) * (y + 0.044715 * (y * y * y))))


def _glu_body(y_ref, w_ref, b_ref, gate_ref, o_ref, z_ref, zb_ref, wt_ref):
    @pl.when((pl.program_id(0) == 0) & (pl.program_id(1) == 0))
    def _():
        _transpose_into(wt_ref, w_ref)

    z = _gelu_tanh(y_ref[...].astype(F32))
    z_ref[...] = z
    zb_ref[...] = z.astype(BF16)
    for f0 in range(0, o_ref.shape[1], MM_SUB):
        rows = slice(f0, f0 + MM_SUB)
        lin = _dot(wt_ref[rows, :], zb_ref[...]) + b_ref[rows, :]
        out = z_ref[rows, :] * _sigmoid(lin) * _silu(gate_ref[rows, :].astype(F32))
        o_ref[:, rows] = out.T.astype(o_ref.dtype)


def glu_gate(y_t, w, b, proj_t, *, gate_row, tm=512):
    s, e, n = y_t.shape
    gb = gate_row // e
    return pl.pallas_call(
        _glu_body,
        grid=(s, n // tm),
        in_specs=[pl.BlockSpec((None, e, tm), lambda t, i: (t, 0, i)),
                  pl.BlockSpec((e, e), lambda t, i: (0, 0)),
                  pl.BlockSpec((e, 1), lambda t, i: (0, 0)),
                  pl.BlockSpec((None, e, tm), lambda t, i: (t, gb, i))],
        out_specs=pl.BlockSpec((tm, e), lambda t, i: (i, t)),
        out_shape=jax.ShapeDtypeStruct((n, s * e), BF16),
        scratch_shapes=[pltpu.VMEM((e, tm), F32), pltpu.VMEM((e, tm), BF16), pltpu.VMEM((e, e), BF16)],
        compiler_params=_cparams(("arbitrary", "arbitrary")),
        name="glu",
    )(y_t, w, b.reshape(e, 1), proj_t)


def kernel(x, ab_norm_g, ab_w_in, gla_alpha_up, gla_alpha_b, gla_head_g, fox_f_b, ab_w_out, c_norm_g, c_w_in, s5_lambda_re, s5_lambda_im, s5_log_step, s5_b_re, s5_b_im, s5_c_re, s5_c_im, s5_d, glu_w, glu_b, c_w_out, final_norm_g):
    batch, seq, d = x.shape
    t = batch * seq
    x2 = x.reshape(t, d)

    hk, hv, fw = GLA_HEADS * GLA_DK, GLA_HEADS * GLA_DV, FOX_HEADS * FOX_DH
    sizes = (hk, hk, hv, GLA_LOWRANK, hv, fw, fw, fw, FOX_HEADS, fw)
    offs = [0]
    for s in sizes:
        offs.append(offs[-1] + s)
    w_main, w_small = ab_weight_prep(jnp.swapaxes(ab_w_in, 1, 2), offs)
    c_gq, c_gk, c_gv, c_gg = 0, hk, 2 * hk, 2 * hk + hv
    c_fq = c_gg + hv
    c_fk, c_fv, c_fg = c_fq + fw, c_fq + 2 * fw, c_fq + 3 * fw

    h0, small = rmsnorm_bf16(x2, ab_norm_g[0], w_small)
    proj = matmul(h0, w_main, BF16, tn=w_main.shape[0] // 2, name="ab_in")

    o_gla = gla(proj, small, gla_alpha_up[0], gla_alpha_b[0], gla_head_g[0],
                batch=batch, seq=seq, q_col=c_gq, k_col=c_gk, v_col=c_gv, gate_col=c_gg)

    fb_row = jnp.zeros((1, 128), F32).at[0, GLA_LOWRANK:GLA_LOWRANK + FOX_HEADS].set(fox_f_b[0])
    qa, ka = fox_gate(small, fb_row, batch=batch, seq=seq, col0=GLA_LOWRANK)
    o_fox = fox(proj, qa, ka, batch=batch, seq=seq, q_col=c_fq, k_col=c_fk, v_col=c_fv, gate_col=c_fg)

    x1, h1g = outproj_mid(o_gla, o_fox, ab_w_out[0].astype(BF16), x2, c_norm_g[0], steps=S5_T, name="ab_out")

    e = d
    groups = e // S5_GROUP_CH
    n_chunks = t // S5_T
    cols_per_seq = seq // S5_T
    n_levels = max(1, (cols_per_seq - 1).bit_length())
    proj_t = matmul_t_grouped(c_w_in[0].astype(BF16), h1g, S5_T, BF16, name="c_in")
    ops = s5_operators(s5_lambda_re[0], s5_lambda_im[0], s5_log_step[0], s5_b_re[0], s5_b_im[0],
                       s5_c_re[0], s5_c_im[0], s5_d[0], n_levels=n_levels)
    y_t = s5_apply(proj_t, *ops, groups=groups, cols_per_seq=cols_per_seq, n_levels=n_levels)
    zz = glu_gate(y_t, glu_w[0].astype(BF16), glu_b[0], proj_t, gate_row=e)
    out = outproj_final(zz, c_w_out[0].astype(BF16), x1, final_norm_g, steps=S5_T, name="c_out")
    return out.reshape(batch, seq, d)
```

```python
import functools
import math

import jax
import jax.numpy as jnp
from jax import lax
from jax.experimental import pallas as pl
from jax.experimental.pallas import tpu as pltpu

EPS = 1e-6
F32 = jnp.float32
BF16 = jnp.bfloat16

GLA_HEADS = 4
GLA_DK = 128
GLA_DV = 256
GLA_LOWRANK = 16
GLA_TAU = 16.0
GLA_CHUNK = 64
GLA_SUB = 8
FOX_HEADS = 8
FOX_DH = 128
FOX_AUG = 128
S5_GROUP_CH = 16
S5_STATE = 64
S5_T = 16
S5_TAB = 32

VMEM_LIMIT = 56 * 1024 * 1024

NT_DIMS = (((1,), (1,)), ((), ()))
TN_DIMS = (((0,), (0,)), ((), ()))
LOG2E = math.log2(math.e)


def _cparams(sem):
    return pltpu.CompilerParams(dimension_semantics=sem, vmem_limit_bytes=VMEM_LIMIT)


def _dot(a, b):
    return jnp.dot(a, b, preferred_element_type=F32)


def _dot_nt(a, b):
    return lax.dot_general(a, b, NT_DIMS, preferred_element_type=F32)


def _dot_tn(a, b):
    return lax.dot_general(a, b, TN_DIMS, preferred_element_type=F32)


def _log_sigmoid(z):
    return -(jnp.maximum(-z, 0.0) + jnp.log1p(jnp.exp(-jnp.abs(z))))


def _sigmoid(z):
    return 1.0 / (1.0 + jnp.exp(-z))


def _silu(z):
    return z * _sigmoid(z)


def _split3(x):
    hi = x.astype(BF16)
    r1 = x - hi.astype(F32)
    mid = r1.astype(BF16)
    lo = (r1 - mid.astype(F32)).astype(BF16)
    return hi, mid, lo


def _rmsnorm_body(x_ref, g_ref, ws_ref, o_ref, s_ref):
    x = x_ref[...]
    ms = jnp.mean(x * x, axis=-1, keepdims=True)
    h = (x * lax.rsqrt(ms + EPS) * g_ref[...]).astype(o_ref.dtype)
    o_ref[...] = h
    s_ref[...] = _dot_nt(h, ws_ref[...])


def rmsnorm_bf16(x, g, wt_small, tm=512):
    m, d = x.shape
    ns = wt_small.shape[0]
    return pl.pallas_call(
        _rmsnorm_body,
        grid=(m // tm,),
        in_specs=[pl.BlockSpec((tm, d), lambda i: (i, 0)),
                  pl.BlockSpec((1, d), lambda i: (0, 0)),
                  pl.BlockSpec((ns, d), lambda i: (0, 0))],
        out_specs=[pl.BlockSpec((tm, d), lambda i: (i, 0)),
                   pl.BlockSpec((tm, ns), lambda i: (i, 0))],
        out_shape=[jax.ShapeDtypeStruct((m, d), BF16), jax.ShapeDtypeStruct((m, ns), F32)],
        compiler_params=_cparams(("parallel",)),
        name="rmsnorm",
    )(x, g.reshape(1, d), wt_small)


def _ab_wprep_body(w_ref, o_ref, s_ref, *, offs):
    w = w_ref[0]
    seg = lambda n: w[offs[n]:offs[n + 1], :]
    o_ref[...] = jnp.concatenate(
        [w[:offs[3], :], seg(4), seg(5) * (FOX_DH ** -0.5 * LOG2E), w[offs[6]:offs[8], :], seg(9)],
        axis=0).astype(o_ref.dtype)
    pad = s_ref.shape[0] - (offs[4] - offs[3]) - (offs[9] - offs[8])
    s_ref[...] = jnp.concatenate([seg(3), seg(8), jnp.zeros((pad, w.shape[1]), F32)], axis=0).astype(s_ref.dtype)


def ab_weight_prep(w_t3, offs, tk=256):
    _, n_in, d = w_t3.shape
    n_main = n_in - (offs[4] - offs[3]) - (offs[9] - offs[8])
    return pl.pallas_call(
        functools.partial(_ab_wprep_body, offs=tuple(offs)),
        grid=(d // tk,),
        in_specs=[pl.BlockSpec((1, n_in, tk), lambda i: (0, 0, i))],
        out_specs=[pl.BlockSpec((n_main, tk), lambda i: (0, i)), pl.BlockSpec((128, tk), lambda i: (0, i))],
        out_shape=[jax.ShapeDtypeStruct((n_main, d), BF16), jax.ShapeDtypeStruct((128, d), BF16)],
        compiler_params=_cparams(("parallel",)),
        name="ab_wprep",
    )(w_t3)


MM_SUB = 512


def _mm_body(a_ref, wt_ref, o_ref):
    a = a_ref[...]
    for c0 in range(0, o_ref.shape[1], MM_SUB):
        o_ref[:, c0:c0 + MM_SUB] = _dot_nt(a, wt_ref[c0:c0 + MM_SUB, :]).astype(o_ref.dtype)


def matmul(a, wt, out_dtype, tm=512, tn=512, name="matmul"):
    m, k = a.shape
    n = wt.shape[0]
    return pl.pallas_call(
        _mm_body,
        grid=(n // tn, m // tm),
        in_specs=[pl.BlockSpec((tm, k), lambda j, i: (i, 0)),
                  pl.BlockSpec((tn, k), lambda j, i: (j, 0))],
        out_specs=pl.BlockSpec((tm, tn), lambda j, i: (i, j)),
        out_shape=jax.ShapeDtypeStruct((m, n), out_dtype),
        compiler_params=_cparams(("parallel", "parallel")),
        name=name,
    )(a, wt)


def _transpose_into(wt_ref, w_ref):
    for c0 in range(0, w_ref.shape[1], MM_SUB):
        wt_ref[c0:c0 + MM_SUB, :] = w_ref[:, c0:c0 + MM_SUB].T


def _mm_t_body(w_ref, a_ref, o_ref, wt_ref):
    @pl.when((pl.program_id(1) == 0) & (pl.program_id(2) == 0))
    def _():
        _transpose_into(wt_ref, w_ref)

    a = a_ref[...]
    for r0 in range(0, o_ref.shape[0], MM_SUB):
        o_ref[r0:r0 + MM_SUB, :] = _dot_nt(wt_ref[r0:r0 + MM_SUB, :], a).astype(o_ref.dtype)


def matmul_t_grouped(w, a2, steps, out_dtype, tm=512, tn=2048, name="matmul_tg"):
    k, nf = w.shape
    n = a2.shape[0]
    return pl.pallas_call(
        _mm_t_body,
        grid=(nf // tn, steps, n // tm),
        in_specs=[pl.BlockSpec((k, tn), lambda j, t, i: (0, j)),
                  pl.BlockSpec((tm, k), lambda j, t, i: (i, t))],
        out_specs=pl.BlockSpec((None, tn, tm), lambda j, t, i: (t, j, i)),
        out_shape=jax.ShapeDtypeStruct((steps, nf, n), out_dtype),
        scratch_shapes=[pltpu.VMEM((tn, k), BF16)],
        compiler_params=_cparams(("arbitrary", "arbitrary", "arbitrary")),
        name=name,
    )(w, a2)


def _gla_body(q_ref, k_ref, v_ref, glr_ref, aup_ref, ab_ref, gate_ref, hg_ref, o_ref,
              s_ref, b_ref, kf_ref, kts_ref, dec_ref, *, tb):
    ib = pl.program_id(2)
    C, SB = GLA_CHUNK, GLA_SUB

    @pl.when(ib == 0)
    def _():
        s_ref[...] = jnp.zeros_like(s_ref)

    z = _dot(glr_ref[:, :GLA_LOWRANK].astype(BF16), aup_ref[...].astype(BF16)) + ab_ref[...]
    la = _log_sigmoid(z) * (1.0 / GLA_TAU)
    tw = 2 * C
    r = lax.broadcasted_iota(jnp.int32, (tw, tw), 0)
    c = lax.broadcasted_iota(jnp.int32, (tw, tw), 1)
    tri = jnp.where((r // C == c // C) & (c <= r), 1.0, 0.0).astype(BF16)
    hi = la.astype(BF16)
    lo = (la - hi.astype(F32)).astype(BF16)
    b_blk = jnp.concatenate(
        [_dot(tri, hi[r0:r0 + tw]) + _dot(tri, lo[r0:r0 + tw]) for r0 in range(0, tb, tw)], axis=0)
    k_blk = k_ref[...].astype(F32)
    b_ref[...] = b_blk
    kf_ref[...] = k_blk
    nc = tb // C
    b3 = b_blk.reshape(nc, C, GLA_DK)
    b_end = b3[:, C - 1:C, :]
    k_end = (k_blk.reshape(nc, C, GLA_DK) * jnp.exp(b_end - b3)).reshape(tb, GLA_DK)
    kt_end = k_end.T.astype(BF16)
    dec_t = jnp.exp(b_end.reshape(nc, GLA_DK)).T
    for ci in range(nc):
        kts_ref[ci] = kt_end[:, ci * C:(ci + 1) * C]
        dec_ref[ci] = jnp.broadcast_to(dec_t[:, ci:ci + 1], (GLA_DK, GLA_DV))

    row = lax.broadcasted_iota(jnp.int32, (C, 1), 0)
    rowi = lax.broadcasted_iota(jnp.int32, (C, C), 0)
    coli = lax.broadcasted_iota(jnp.int32, (C, C), 1)
    lane_c = lax.broadcasted_iota(jnp.int32, (SB, C), 1)
    sub_r = lax.broadcasted_iota(jnp.int32, (SB, 1), 0)
    ones = jnp.ones((GLA_DK, C), BF16)
    neg = -jnp.inf
    scale = GLA_DK ** -0.5

    def chunk(ci, carry):
        r0 = pl.multiple_of(ci * C, C)
        q = q_ref[pl.ds(r0, C), :].astype(F32) * scale
        k = kf_ref[pl.ds(r0, C), :]
        b = b_ref[pl.ds(r0, C), :]
        v = v_ref[pl.ds(r0, C), :]

        s = s_ref[...]
        o = _dot((q * jnp.exp(b)).astype(BF16), s.astype(BF16))
        s_ref[...] = s * dec_ref[ci] + _dot(kts_ref[ci], v)

        attn = jnp.zeros((C, C), F32)
        h = C // 2
        while h >= SB:
            ref = jnp.broadcast_to(b.reshape(C // (2 * h), 2 * h, GLA_DK)[:, h - 1:h, :],
                                   (C // (2 * h), 2 * h, GLA_DK)).reshape(C, GLA_DK)
            upper = (row // h) % 2 == 1
            q_h = q * jnp.exp(jnp.where(upper, b - ref, neg))
            k_h = k * jnp.exp(jnp.where(upper, neg, ref - b))
            a_h = _dot_nt(q_h.astype(BF16), k_h.astype(BF16))
            attn = attn + (a_h if 2 * h == C else jnp.where(rowi // (2 * h) == coli // (2 * h), a_h, 0.0))
            h //= 2

        zs = []
        for bi in range(C // SB):
            s0 = bi * SB
            q_i = q[s0:s0 + SB, :]
            b_i = b[s0:s0 + SB, :]
            for j in range(SB):
                k_j = kf_ref[pl.ds(r0 + (s0 + j), 1), :]
                b_j = b_ref[pl.ds(r0 + (s0 + j), 1), :]
                zs.append(q_i * k_j * jnp.exp(jnp.where(sub_r >= j, b_i - b_j, neg)))
        zsum = _dot(jnp.concatenate(zs, axis=0).astype(BF16), ones)
        diag = []
        for bi in range(C // SB):
            acc = jnp.zeros((SB, C), F32)
            for j in range(SB):
                n0 = (bi * SB + j) * SB
                acc = acc + jnp.where(lane_c == bi * SB + j, zsum[n0:n0 + SB, :], 0.0)
            diag.append(acc)
        attn = attn + jnp.concatenate(diag, axis=0)
        o = o + _dot(attn.astype(BF16), v)

        o = o * lax.rsqrt(jnp.mean(o * o, axis=-1, keepdims=True) + EPS) * hg_ref[...]
        g = gate_ref[pl.ds(r0, C), :].astype(F32)
        o_ref[pl.ds(r0, C), :] = (o * _silu(g)).astype(o_ref.dtype)
        return carry

    lax.fori_loop(0, tb // C, chunk, 0, unroll=4)


def gla(proj, small, alpha_up, alpha_b, head_g, *, batch, seq, q_col, k_col, v_col, gate_col, tb=512):
    t = batch * seq
    nb = seq // tb
    dk, dv, h = GLA_DK, GLA_DV, GLA_HEADS
    qb, kb, vb, gb = q_col // dk, k_col // dk, v_col // dv, gate_col // dv
    tok = lambda b, hh, i: b * nb + i
    return pl.pallas_call(
        functools.partial(_gla_body, tb=tb),
        grid=(batch, h, nb),
        in_specs=[
            pl.BlockSpec((tb, dk), lambda b, hh, i: (tok(b, hh, i), qb + hh)),
            pl.BlockSpec((tb, dk), lambda b, hh, i: (tok(b, hh, i), kb + hh)),
            pl.BlockSpec((tb, dv), lambda b, hh, i: (tok(b, hh, i), vb + hh)),
            pl.BlockSpec((tb, 128), lambda b, hh, i: (tok(b, hh, i), 0)),
            pl.BlockSpec((GLA_LOWRANK, dk), lambda b, hh, i: (0, hh)),
            pl.BlockSpec((1, dk), lambda b, hh, i: (0, hh)),
            pl.BlockSpec((tb, dv), lambda b, hh, i: (tok(b, hh, i), gb + hh)),
            pl.BlockSpec((1, dv), lambda b, hh, i: (0, hh)),
        ],
        out_specs=pl.BlockSpec((tb, dv), lambda b, hh, i: (tok(b, hh, i), hh)),
        out_shape=jax.ShapeDtypeStruct((t, h * dv), BF16),
        scratch_shapes=[pltpu.VMEM((dk, dv), F32), pltpu.VMEM((tb, dk), F32), pltpu.VMEM((tb, dk), F32),
                        pltpu.VMEM((tb // GLA_CHUNK, dk, GLA_CHUNK), BF16),
                        pltpu.VMEM((tb // GLA_CHUNK, dk, dv), F32)],
        compiler_params=_cparams(("parallel", "parallel", "arbitrary")),
        name="gla",
    )(proj, proj, proj, small, alpha_up, alpha_b.reshape(1, -1), proj, head_g.reshape(1, -1))


def _fox_gate_body(s_ref, fb_ref, qa_ref, ka_ref, *, blk, col0):
    n = s_ref.shape[0] // blk
    r = lax.broadcasted_iota(jnp.int32, (blk, blk), 0)
    c = lax.broadcasted_iota(jnp.int32, (blk, blk), 1)
    tri = jnp.where(c <= r, 1.0, 0.0).astype(BF16)
    lane = lax.broadcasted_iota(jnp.int32, (blk, FOX_AUG), 1)

    def step(i, carry):
        r0 = pl.multiple_of(i * blk, blk)
        lf = _log_sigmoid(s_ref[pl.ds(r0, blk), :] + fb_ref[...])
        hi, mid, lo = _split3(lf)
        cs = _dot(tri, hi) + _dot(tri, mid) + _dot(tri, lo) + carry
        for h in range(FOX_HEADS):
            col = cs[:, col0 + h:col0 + h + 1] * LOG2E
            c0, c1, c2 = (p.astype(F32) for p in _split3(col))
            qa = jnp.where(lane == 0, c0, jnp.where(lane == 1, c1, jnp.where(lane == 2, c2,
                           jnp.where(lane < 6, 1.0, 0.0))))
            ka = jnp.where(lane < 3, 1.0, jnp.where(lane == 3, -c0, jnp.where(lane == 4, -c1,
                           jnp.where(lane == 5, -c2, 0.0))))
            qa_ref[h, pl.ds(r0, blk), :] = qa.astype(BF16)
            ka_ref[h, pl.ds(r0, blk), :] = ka.astype(BF16)
        return cs[blk - 1:blk, :]

    lax.fori_loop(0, n, step, jnp.zeros((1, s_ref.shape[1]), F32))


def fox_gate(small, fb_row, *, batch, seq, col0, blk=256):
    shp = jax.ShapeDtypeStruct((FOX_HEADS, batch * seq, FOX_AUG), BF16)
    spec = pl.BlockSpec((FOX_HEADS, seq, FOX_AUG), lambda b: (0, b, 0))
    return pl.pallas_call(
        functools.partial(_fox_gate_body, blk=blk, col0=col0),
        grid=(batch,),
        in_specs=[pl.BlockSpec((seq, 128), lambda b: (b, 0)),
                  pl.BlockSpec((1, 128), lambda b: (0, 0))],
        out_specs=[spec, spec],
        out_shape=[shp, shp],
        compiler_params=_cparams(("parallel",)),
        name="fox_gate",
    )(small, fb_row)


def _fox_body(it_ref, jt_ref, q_ref, qa_ref, k_ref, ka_ref, v_ref, gate_ref, o_ref,
              m_ref, l_ref, acc_ref, *, tq, tk, hpb):
    p = pl.program_id(2)
    i = it_ref[p]
    j = jt_ref[p]
    d = FOX_DH

    @pl.when(j == 0)
    def _():
        m_ref[...] = jnp.full_like(m_ref, -jnp.inf)
        l_ref[...] = jnp.zeros_like(l_ref)
        acc_ref[...] = jnp.zeros_like(acc_ref)

    def update(hh, masked):
        q_aug = jnp.concatenate([q_ref[:, hh * d:(hh + 1) * d], qa_ref[hh]], axis=1)
        k_aug = jnp.concatenate([k_ref[:, hh * d:(hh + 1) * d], ka_ref[hh]], axis=1)
        st = _dot_nt(k_aug, q_aug)
        if masked:
            kr = lax.broadcasted_iota(jnp.int32, (tk, tq), 0)
            qc = lax.broadcasted_iota(jnp.int32, (tk, tq), 1)
            st = jnp.where(qc >= kr, st, -jnp.inf)
        m_old = m_ref[hh]
        m_new = jnp.maximum(m_old, jnp.max(st, axis=0, keepdims=True))
        alpha = jnp.exp2(m_old - m_new)
        pt = jnp.exp2(st - m_new)
        l_ref[hh] = alpha * l_ref[hh] + jnp.sum(pt, axis=0, keepdims=True)
        acc_ref[hh] = alpha * acc_ref[hh] + _dot_tn(v_ref[:, hh * d:(hh + 1) * d], pt.astype(BF16))
        m_ref[hh] = m_new

    @pl.when(j < i)
    def _():
        for hh in range(hpb):
            update(hh, False)

    @pl.when(j == i)
    def _():
        for hh in range(hpb):
            update(hh, True)
            o = (acc_ref[hh] / l_ref[hh]).T
            g = gate_ref[:, hh * d:(hh + 1) * d].astype(F32)
            o_ref[:, hh * d:(hh + 1) * d] = (o * _silu(g)).astype(o_ref.dtype)


def fox(proj, qa, ka, *, batch, seq, q_col, k_col, v_col, gate_col, t=512, hpb=2):
    tt = batch * seq
    nb = seq // t
    d, h = FOX_DH, FOX_HEADS
    w = hpb * d
    qb, kb, gb, vb = q_col // w, k_col // w, gate_col // w, v_col // w
    pairs = [(i, j) for i in range(nb) for j in range(i + 1)]
    it = jnp.array([p[0] for p in pairs], jnp.int32)
    jt = jnp.array([p[1] for p in pairs], jnp.int32)
    grid_spec = pltpu.PrefetchScalarGridSpec(
        num_scalar_prefetch=2,
        grid=(batch, h // hpb, len(pairs)),
        in_specs=[
            pl.BlockSpec((t, w), lambda b, hh, p, it, jt: (b * nb + it[p], qb + hh)),
            pl.BlockSpec((hpb, t, FOX_AUG), lambda b, hh, p, it, jt: (hh, b * nb + it[p], 0)),
            pl.BlockSpec((t, w), lambda b, hh, p, it, jt: (b * nb + jt[p], kb + hh)),
            pl.BlockSpec((hpb, t, FOX_AUG), lambda b, hh, p, it, jt: (hh, b * nb + jt[p], 0)),
            pl.BlockSpec((t, w), lambda b, hh, p, it, jt: (b * nb + jt[p], vb + hh)),
            pl.BlockSpec((t, w), lambda b, hh, p, it, jt: (b * nb + it[p], gb + hh)),
        ],
        out_specs=pl.BlockSpec((t, w), lambda b, hh, p, it, jt: (b * nb + it[p], hh)),
        scratch_shapes=[pltpu.VMEM((hpb, 1, t), F32), pltpu.VMEM((hpb, 1, t), F32), pltpu.VMEM((hpb, d, t), F32)],
    )
    return pl.pallas_call(
        functools.partial(_fox_body, tq=t, tk=t, hpb=hpb),
        grid_spec=grid_spec,
        out_shape=jax.ShapeDtypeStruct((tt, h * d), BF16),
        compiler_params=_cparams(("parallel", "parallel", "arbitrary")),
        name="fox",
    )(it, jt, proj, qa, proj, ka, proj, proj)


def _regroup_perm(tm, steps, to_grouped):
    r = lax.broadcasted_iota(jnp.int32, (tm, tm), 0)
    c = lax.broadcasted_iota(jnp.int32, (tm, tm), 1)
    nc = tm // steps
    src = (r % nc) * steps + r // nc if to_grouped else (r % steps) * nc + r // steps
    return jnp.where(c == src, 1.0, 0.0).astype(BF16)


def _outproj_mid_body(a1_ref, a2_ref, w_ref, x_ref, g_ref, r_ref, h_ref, *, steps):
    k1 = a1_ref.shape[1]
    tm, d = x_ref.shape
    nc = tm // steps
    acc = x_ref[...] + _dot(a1_ref[...], w_ref[:k1, :]) + _dot(a2_ref[...], w_ref[k1:, :])
    r_ref[...] = acc
    normed = (acc * lax.rsqrt(jnp.mean(acc * acc, axis=-1, keepdims=True) + EPS) * g_ref[...]).astype(BF16)
    by_step = _dot(_regroup_perm(tm, steps, True), normed).astype(h_ref.dtype)
    for s in range(steps):
        h_ref[:, s * d:(s + 1) * d] = by_step[s * nc:(s + 1) * nc, :]


def outproj_mid(a1, a2, w, x, g, *, steps, tm=256, name="outproj_mid"):
    m, d = x.shape
    row = pl.BlockSpec((tm, d), lambda i: (i, 0))
    return pl.pallas_call(
        functools.partial(_outproj_mid_body, steps=steps),
        grid=(m // tm,),
        in_specs=[pl.BlockSpec((tm, a1.shape[1]), lambda i: (i, 0)),
                  pl.BlockSpec((tm, a2.shape[1]), lambda i: (i, 0)),
                  pl.BlockSpec(w.shape, lambda i: (0, 0)),
                  row,
                  pl.BlockSpec((1, d), lambda i: (0, 0))],
        out_specs=[row, pl.BlockSpec((tm // steps, steps * d), lambda i: (i, 0))],
        out_shape=[jax.ShapeDtypeStruct((m, d), F32), jax.ShapeDtypeStruct((m // steps, steps * d), BF16)],
        compiler_params=_cparams(("parallel",)),
        name=name,
    )(a1, a2, w, x, g.reshape(1, d))


def _outproj_final_body(a_ref, w_ref, x_ref, g_ref, o_ref, *, steps):
    tm = x_ref.shape[0]
    e = w_ref.shape[0]
    by_step = jnp.concatenate([a_ref[:, s * e:(s + 1) * e] for s in range(steps)], axis=0)
    a_tok = _dot(_regroup_perm(tm, steps, False), by_step).astype(BF16)
    acc = x_ref[...] + _dot(a_tok, w_ref[...])
    o_ref[...] = acc * lax.rsqrt(jnp.mean(acc * acc, axis=-1, keepdims=True) + EPS) * g_ref[...]


def outproj_final(a_grouped, w, x, g, *, steps, tm=256, name="outproj_final"):
    m, d = x.shape
    e = w.shape[0]
    row = pl.BlockSpec((tm, d), lambda i: (i, 0))
    return pl.pallas_call(
        functools.partial(_outproj_final_body, steps=steps),
        grid=(m // tm,),
        in_specs=[pl.BlockSpec((tm // steps, steps * e), lambda i: (i, 0)),
                  pl.BlockSpec(w.shape, lambda i: (0, 0)),
                  row,
                  pl.BlockSpec((1, d), lambda i: (0, 0))],
        out_specs=row,
        out_shape=jax.ShapeDtypeStruct((m, d), F32),
        compiler_params=_cparams(("parallel",)),
        name=name,
    )(a_grouped, w, x, g.reshape(1, d))


S5_GPB = 2


def _s5_group(u, wt, nt, mt, d_col, arow, *, cols_per_seq, n_levels):
    n = u.shape[1]
    p = S5_STATE
    nseq = n // cols_per_seq
    x = _dot(nt, u)

    def to_rows(part):
        return jnp.concatenate([part[:, s * cols_per_seq:(s + 1) * cols_per_seq] for s in range(nseq)], axis=0).T

    def to_cols(rows):
        rt = rows.T
        return jnp.concatenate([rt[s * p:(s + 1) * p, :] for s in range(nseq)], axis=1)

    xr, xi = to_rows(x[:p, :]), to_rows(x[p:, :])
    pos = lax.broadcasted_iota(jnp.int32, (cols_per_seq, 1), 0)
    for lvl in range(n_levels):
        sh = 1 << lvl
        keep = pos >= sh
        sr = jnp.where(keep, pltpu.roll(xr, sh, axis=0), 0.0)
        si = jnp.where(keep, pltpu.roll(xi, sh, axis=0), 0.0)
        ar, ai = arow[lvl:lvl + 1, :], arow[n_levels + lvl:n_levels + lvl + 1, :]
        xr, xi = xr + ar * sr - ai * si, xi + ar * si + ai * sr
    keep = pos >= 1
    pr = jnp.where(keep, pltpu.roll(xr, 1, axis=0), 0.0)
    pi = jnp.where(keep, pltpu.roll(xi, 1, axis=0), 0.0)
    x_prev = jnp.concatenate([to_cols(pr), to_cols(pi)], axis=0).astype(BF16)
    return _dot(wt, u) + _dot(mt, x_prev) + d_col * u.astype(F32)


def _s5_body(u_ref, wt_ref, nt_ref, mt_ref, tab_ref, arow_ref, o_ref, *, cols_per_seq, n_levels):
    t, _, n = u_ref.shape
    cg = S5_GROUP_CH
    for gi in range(S5_GPB):
        ch = slice(gi * cg, (gi + 1) * cg)
        u = u_ref[:, ch, :].reshape(t * cg, n)
        y = _s5_group(u, wt_ref[gi], nt_ref[gi], mt_ref[gi], tab_ref[gi][:, 0:1], arow_ref[gi],
                      cols_per_seq=cols_per_seq, n_levels=n_levels)
        o_ref[:, ch, :] = y.astype(o_ref.dtype).reshape(t, cg, n)


def s5_apply(proj_t, wt_g, nt_g, mt_g, tab, arow, *, groups, cols_per_seq, n_levels):
    t, _, n = proj_t.shape
    cg = S5_GROUP_CH * S5_GPB
    spec3 = lambda shp: pl.BlockSpec((S5_GPB,) + shp, lambda i: (i, 0, 0))
    return pl.pallas_call(
        functools.partial(_s5_body, cols_per_seq=cols_per_seq, n_levels=n_levels),
        grid=(groups // S5_GPB,),
        in_specs=[pl.BlockSpec((t, cg, n), lambda i: (0, i, 0)),
                  spec3(wt_g.shape[1:]), spec3(nt_g.shape[1:]), spec3(mt_g.shape[1:]), spec3(tab.shape[1:]),
                  spec3(arow.shape[1:])],
        out_specs=pl.BlockSpec((t, cg, n), lambda i: (0, i, 0)),
        out_shape=jax.ShapeDtypeStruct((t, groups * S5_GROUP_CH, n), BF16),
        compiler_params=_cparams(("parallel",)),
        name="s5",
    )(proj_t, wt_g, nt_g, mt_g, tab, arow)


def s5_operators(lam_re, lam_im, log_step, b_re, b_im, c_re, c_im, d, *, n_levels, n_seq):
    hp = lax.Precision.HIGHEST
    g, p = lam_re.shape
    cg, t = S5_GROUP_CH, S5_T
    lr = jnp.minimum(lam_re.astype(F32), -1e-4)
    li = lam_im.astype(F32)
    step = jnp.exp(log_step.astype(F32))[:, None]
    mag = jnp.exp(lr * step)
    lb_re, lb_im = mag * jnp.cos(li * step), mag * jnp.sin(li * step)
    den = lr * lr + li * li
    nr, ni = lb_re - 1.0, lb_im
    coef_re = (nr * lr + ni * li) / den
    coef_im = (ni * lr - nr * li) / den
    bb_re = coef_re[..., None] * b_re - coef_im[..., None] * b_im
    bb_im = coef_re[..., None] * b_im + coef_im[..., None] * b_re

    def power(m):
        mm = m.astype(F32)[None, :, None]
        mg = jnp.exp(lr[:, None, :] * step[:, None, :] * mm)
        ang = li[:, None, :] * step[:, None, :] * mm
        return mg * jnp.cos(ang), mg * jnp.sin(ang)

    pw_re, pw_im = power(jnp.arange(t + 1))
    cp_re = c_re[:, None] * pw_re[:, :, None, :] - c_im[:, None] * pw_im[:, :, None, :]
    cp_im = c_re[:, None] * pw_im[:, :, None, :] + c_im[:, None] * pw_re[:, :, None, :]
    taps = (jnp.einsum('gxcp,gpd->gxcd', cp_re[:, :t], bb_re, precision=hp)
            - jnp.einsum('gxcp,gpd->gxcd', cp_im[:, :t], bb_im, precision=hp))
    shift = (jnp.arange(t)[None, None, :] - jnp.arange(t)[None, :, None]
             == jnp.arange(t)[:, None, None]).astype(F32)
    wt_g = jnp.einsum('xst,gxcd->gtcsd', shift, taps, precision=hp).reshape(g, t * cg, t * cg)
    pr, pi = pw_re[:, t - 1::-1], pw_im[:, t - 1::-1]
    bt_re, bt_im = bb_re[:, :, None, :], bb_im[:, :, None, :]
    prt, pit = pr.transpose(0, 2, 1)[:, :, :, None], pi.transpose(0, 2, 1)[:, :, :, None]
    n_re = (prt * bt_re - pit * bt_im).reshape(g, p, t * cg)
    n_im = (prt * bt_im + pit * bt_re).reshape(g, p, t * cg)
    nt_g = jnp.concatenate([n_re, n_im], axis=1)
    m_re = cp_re[:, 1:].reshape(g, t * cg, p)
    m_im = cp_im[:, 1:].reshape(g, t * cg, p)
    mt_g = jnp.concatenate([m_re, -m_im], axis=2)
    ar, ai = power(t * (2 ** jnp.arange(n_levels)))
    arow = jnp.concatenate([jnp.tile(ar, (1, 1, n_seq)), jnp.tile(ai, (1, 1, n_seq))], axis=1)
    d_col = jnp.tile(d.astype(F32).reshape(g, cg), (1, t))[:, :, None]
    tab = d_col
    tab = jnp.pad(tab, ((0, 0), (0, 0), (0, S5_TAB - tab.shape[2])))
    return wt_g.astype(BF16), nt_g.astype(BF16), mt_g.astype(BF16), tab, arow


def _gelu_tanh(y):
    return 0.5 * y * (1.0 + jnp.tanh(math.sqrt(2.0 / math.pi) * (y + 0.044715 * (y * y * y))))


def _glu_body(y_ref, w_ref, b_ref, gate_ref, o_ref, z_ref, zb_ref, wt_ref):
    @pl.when((pl.program_id(0) == 0) & (pl.program_id(1) == 0))
    def _():
        _transpose_into(wt_ref, w_ref)

    z = _gelu_tanh(y_ref[...].astype(F32))
    z_ref[...] = z
    zb_ref[...] = z.astype(BF16)
    for f0 in range(0, o_ref.shape[1], MM_SUB):
        rows = slice(f0, f0 + MM_SUB)
        lin = _dot(wt_ref[rows, :], zb_ref[...]) + b_ref[rows, :]
        out = z_ref[rows, :] * _sigmoid(lin) * _silu(gate_ref[rows, :].astype(F32))
        o_ref[:, rows] = out.T.astype(o_ref.dtype)


def glu_gate(y_t, w, b, proj_t, *, gate_row, tm=512):
    s, e, n = y_t.shape
    gb = gate_row // e
    return pl.pallas_call(
        _glu_body,
        grid=(s, n // tm),
        in_specs=[pl.BlockSpec((None, e, tm), lambda t, i: (t, 0, i)),
                  pl.BlockSpec((e, e), lambda t, i: (0, 0)),
                  pl.BlockSpec((e, 1), lambda t, i: (0, 0)),
                  pl.BlockSpec((None, e, tm), lambda t, i: (t, gb, i))],
        out_specs=pl.BlockSpec((tm, e), lambda t, i: (i, t)),
        out_shape=jax.ShapeDtypeStruct((n, s * e), BF16),
        scratch_shapes=[pltpu.VMEM((e, tm), F32), pltpu.VMEM((e, tm), BF16), pltpu.VMEM((e, e), BF16)],
        compiler_params=_cparams(("arbitrary", "arbitrary")),
        name="glu",
    )(y_t, w, b.reshape(e, 1), proj_t)


def kernel(x, ab_norm_g, ab_w_in, gla_alpha_up, gla_alpha_b, gla_head_g, fox_f_b, ab_w_out, c_norm_g, c_w_in, s5_lambda_re, s5_lambda_im, s5_log_step, s5_b_re, s5_b_im, s5_c_re, s5_c_im, s5_d, glu_w, glu_b, c_w_out, final_norm_g):
    batch, seq, d = x.shape
    t = batch * seq
    x2 = x.reshape(t, d)

    hk, hv, fw = GLA_HEADS * GLA_DK, GLA_HEADS * GLA_DV, FOX_HEADS * FOX_DH
    sizes = (hk, hk, hv, GLA_LOWRANK, hv, fw, fw, fw, FOX_HEADS, fw)
    offs = [0]
    for s in sizes:
        offs.append(offs[-1] + s)
    w_main, w_small = ab_weight_prep(jnp.swapaxes(ab_w_in, 1, 2), offs)
    c_gq, c_gk, c_gv, c_gg = 0, hk, 2 * hk, 2 * hk + hv
    c_fq = c_gg + hv
    c_fk, c_fv, c_fg = c_fq + fw, c_fq + 2 * fw, c_fq + 3 * fw

    h0, small = rmsnorm_bf16(x2, ab_norm_g[0], w_small)
    proj = matmul(h0, w_main, BF16, tn=w_main.shape[0] // 2, name="ab_in")

    o_gla = gla(proj, small, gla_alpha_up[0], gla_alpha_b[0], gla_head_g[0],
                batch=batch, seq=seq, q_col=c_gq, k_col=c_gk, v_col=c_gv, gate_col=c_gg)

    fb_row = jnp.zeros((1, 128), F32).at[0, GLA_LOWRANK:GLA_LOWRANK + FOX_HEADS].set(fox_f_b[0])
    qa, ka = fox_gate(small, fb_row, batch=batch, seq=seq, col0=GLA_LOWRANK)
    o_fox = fox(proj, qa, ka, batch=batch, seq=seq, q_col=c_fq, k_col=c_fk, v_col=c_fv, gate_col=c_fg)

    x1, h1g = outproj_mid(o_gla, o_fox, ab_w_out[0].astype(BF16), x2, c_norm_g[0], steps=S5_T, name="ab_out")

    e = d
    groups = e // S5_GROUP_CH
    n_chunks = t // S5_T
    cols_per_seq = seq // S5_T
    n_levels = max(1, (cols_per_seq - 1).bit_length())
    proj_t = matmul_t_grouped(c_w_in[0].astype(BF16), h1g, S5_T, BF16, name="c_in")
    ops = s5_operators(s5_lambda_re[0], s5_lambda_im[0], s5_log_step[0], s5_b_re[0], s5_b_im[0],
                       s5_c_re[0], s5_c_im[0], s5_d[0], n_levels=n_levels, n_seq=batch)
    y_t = s5_apply(proj_t, *ops, groups=groups, cols_per_seq=cols_per_seq, n_levels=n_levels)
    zz = glu_gate(y_t, glu_w[0].astype(BF16), glu_b[0], proj_t, gate_row=e)
    out = outproj_final(zz, c_w_out[0].astype(BF16), x1, final_norm_g, steps=S5_T, name="c_out")
    return out.reshape(batch, seq, d)
```

```python
import functools
import math

import jax
import jax.numpy as jnp
from jax import lax
from jax.experimental import pallas as pl
from jax.experimental.pallas import tpu as pltpu

EPS = 1e-6
F32 = jnp.float32
BF16 = jnp.bfloat16

GLA_HEADS = 4
GLA_DK = 128
GLA_DV = 256
GLA_LOWRANK = 16
GLA_TAU = 16.0
GLA_CHUNK = 64
GLA_SUB = 8
FOX_HEADS = 8
FOX_DH = 128
FOX_AUG = 128
S5_GROUP_CH = 16
S5_STATE = 64
S5_T = 16
S5_TAB = 32

VMEM_LIMIT = 56 * 1024 * 1024

NT_DIMS = (((1,), (1,)), ((), ()))
TN_DIMS = (((0,), (0,)), ((), ()))
LOG2E = math.log2(math.e)


def _cparams(sem):
    return pltpu.CompilerParams(dimension_semantics=sem, vmem_limit_bytes=VMEM_LIMIT)


def _dot(a, b):
    return jnp.dot(a, b, preferred_element_type=F32)


def _dot_nt(a, b):
    return lax.dot_general(a, b, NT_DIMS, preferred_element_type=F32)


def _dot_tn(a, b):
    return lax.dot_general(a, b, TN_DIMS, preferred_element_type=F32)


def _log_sigmoid(z):
    return -(jnp.maximum(-z, 0.0) + jnp.log1p(jnp.exp(-jnp.abs(z))))


def _sigmoid(z):
    return 1.0 / (1.0 + jnp.exp(-z))


def _silu(z):
    return z * _sigmoid(z)


def _split3(x):
    hi = x.astype(BF16)
    r1 = x - hi.astype(F32)
    mid = r1.astype(BF16)
    lo = (r1 - mid.astype(F32)).astype(BF16)
    return hi, mid, lo


def _rmsnorm_body(x_ref, g_ref, ws_ref, o_ref, s_ref):
    x = x_ref[...]
    ms = jnp.mean(x * x, axis=-1, keepdims=True)
    h = (x * lax.rsqrt(ms + EPS) * g_ref[...]).astype(o_ref.dtype)
    o_ref[...] = h
    s_ref[...] = _dot_nt(h, ws_ref[...])


def rmsnorm_bf16(x, g, wt_small, tm=512):
    m, d = x.shape
    ns = wt_small.shape[0]
    return pl.pallas_call(
        _rmsnorm_body,
        grid=(m // tm,),
        in_specs=[pl.BlockSpec((tm, d), lambda i: (i, 0)),
                  pl.BlockSpec((1, d), lambda i: (0, 0)),
                  pl.BlockSpec((ns, d), lambda i: (0, 0))],
        out_specs=[pl.BlockSpec((tm, d), lambda i: (i, 0)),
                   pl.BlockSpec((tm, ns), lambda i: (i, 0))],
        out_shape=[jax.ShapeDtypeStruct((m, d), BF16), jax.ShapeDtypeStruct((m, ns), F32)],
        compiler_params=_cparams(("parallel",)),
        name="rmsnorm",
    )(x, g.reshape(1, d), wt_small)


def _ab_wprep_body(w_ref, o_ref, s_ref, *, offs):
    w = w_ref[0]
    seg = lambda n: w[offs[n]:offs[n + 1], :]
    o_ref[...] = jnp.concatenate(
        [w[:offs[3], :], seg(4), seg(5) * (FOX_DH ** -0.5 * LOG2E), w[offs[6]:offs[8], :], seg(9)],
        axis=0).astype(o_ref.dtype)
    pad = s_ref.shape[0] - (offs[4] - offs[3]) - (offs[9] - offs[8])
    s_ref[...] = jnp.concatenate([seg(3), seg(8), jnp.zeros((pad, w.shape[1]), F32)], axis=0).astype(s_ref.dtype)


def ab_weight_prep(w_t3, offs, tk=256):
    _, n_in, d = w_t3.shape
    n_main = n_in - (offs[4] - offs[3]) - (offs[9] - offs[8])
    return pl.pallas_call(
        functools.partial(_ab_wprep_body, offs=tuple(offs)),
        grid=(d // tk,),
        in_specs=[pl.BlockSpec((1, n_in, tk), lambda i: (0, 0, i))],
        out_specs=[pl.BlockSpec((n_main, tk), lambda i: (0, i)), pl.BlockSpec((128, tk), lambda i: (0, i))],
        out_shape=[jax.ShapeDtypeStruct((n_main, d), BF16), jax.ShapeDtypeStruct((128, d), BF16)],
        compiler_params=_cparams(("parallel",)),
        name="ab_wprep",
    )(w_t3)


MM_SUB = 512


def _mm_body(a_ref, wt_ref, o_ref):
    a = a_ref[...]
    for c0 in range(0, o_ref.shape[1], MM_SUB):
        o_ref[:, c0:c0 + MM_SUB] = _dot_nt(a, wt_ref[c0:c0 + MM_SUB, :]).astype(o_ref.dtype)


def matmul(a, wt, out_dtype, tm=512, tn=512, name="matmul"):
    m, k = a.shape
    n = wt.shape[0]
    return pl.pallas_call(
        _mm_body,
        grid=(n // tn, m // tm),
        in_specs=[pl.BlockSpec((tm, k), lambda j, i: (i, 0)),
                  pl.BlockSpec((tn, k), lambda j, i: (j, 0))],
        out_specs=pl.BlockSpec((tm, tn), lambda j, i: (i, j)),
        out_shape=jax.ShapeDtypeStruct((m, n), out_dtype),
        compiler_params=_cparams(("parallel", "parallel")),
        name=name,
    )(a, wt)


def _transpose_into(wt_ref, w_ref):
    for c0 in range(0, w_ref.shape[1], MM_SUB):
        wt_ref[c0:c0 + MM_SUB, :] = w_ref[:, c0:c0 + MM_SUB].T


def _mm_t_body(w_ref, a_ref, o_ref, wt_ref):
    @pl.when((pl.program_id(1) == 0) & (pl.program_id(2) == 0))
    def _():
        _transpose_into(wt_ref, w_ref)

    a = a_ref[...]
    for r0 in range(0, o_ref.shape[0], MM_SUB):
        o_ref[r0:r0 + MM_SUB, :] = _dot_nt(wt_ref[r0:r0 + MM_SUB, :], a).astype(o_ref.dtype)


def matmul_t_grouped(w, a2, steps, out_dtype, tm=512, tn=2048, name="matmul_tg"):
    k, nf = w.shape
    n = a2.shape[0]
    return pl.pallas_call(
        _mm_t_body,
        grid=(nf // tn, steps, n // tm),
        in_specs=[pl.BlockSpec((k, tn), lambda j, t, i: (0, j)),
                  pl.BlockSpec((tm, k), lambda j, t, i: (i, t))],
        out_specs=pl.BlockSpec((None, tn, tm), lambda j, t, i: (t, j, i)),
        out_shape=jax.ShapeDtypeStruct((steps, nf, n), out_dtype),
        scratch_shapes=[pltpu.VMEM((tn, k), BF16)],
        compiler_params=_cparams(("arbitrary", "arbitrary", "arbitrary")),
        name=name,
    )(w, a2)


def _gla_body(q_ref, k_ref, v_ref, glr_ref, aup_ref, ab_ref, gate_ref, hg_ref, o_ref,
              s_ref, b_ref, kf_ref, kts_ref, dec_ref, *, tb):
    ib = pl.program_id(2)
    C, SB = GLA_CHUNK, GLA_SUB

    @pl.when(ib == 0)
    def _():
        s_ref[...] = jnp.zeros_like(s_ref)

    z = _dot(glr_ref[:, :GLA_LOWRANK].astype(BF16), aup_ref[...].astype(BF16)) + ab_ref[...]
    la = _log_sigmoid(z) * (1.0 / GLA_TAU)
    tw = 2 * C
    r = lax.broadcasted_iota(jnp.int32, (tw, tw), 0)
    c = lax.broadcasted_iota(jnp.int32, (tw, tw), 1)
    tri = jnp.where((r // C == c // C) & (c <= r), 1.0, 0.0).astype(BF16)
    hi = la.astype(BF16)
    lo = (la - hi.astype(F32)).astype(BF16)
    b_blk = jnp.concatenate(
        [_dot(tri, hi[r0:r0 + tw]) + _dot(tri, lo[r0:r0 + tw]) for r0 in range(0, tb, tw)], axis=0)
    k_blk = k_ref[...].astype(F32)
    b_ref[...] = b_blk
    kf_ref[...] = k_blk
    nc = tb // C
    b3 = b_blk.reshape(nc, C, GLA_DK)
    b_end = b3[:, C - 1:C, :]
    k_end = (k_blk.reshape(nc, C, GLA_DK) * jnp.exp(b_end - b3)).reshape(tb, GLA_DK)
    kt_end = k_end.T.astype(BF16)
    dec_t = jnp.exp(b_end.reshape(nc, GLA_DK)).T
    for ci in range(nc):
        kts_ref[ci] = kt_end[:, ci * C:(ci + 1) * C]
        dec_ref[ci] = jnp.broadcast_to(dec_t[:, ci:ci + 1], (GLA_DK, GLA_DV))

    row = lax.broadcasted_iota(jnp.int32, (C, 1), 0)
    rowi = lax.broadcasted_iota(jnp.int32, (C, C), 0)
    coli = lax.broadcasted_iota(jnp.int32, (C, C), 1)
    lane_c = lax.broadcasted_iota(jnp.int32, (SB, C), 1)
    sub_r = lax.broadcasted_iota(jnp.int32, (SB, 1), 0)
    ones = jnp.ones((GLA_DK, C), BF16)
    neg = -jnp.inf
    scale = GLA_DK ** -0.5

    def chunk(ci, carry):
        r0 = pl.multiple_of(ci * C, C)
        q = q_ref[pl.ds(r0, C), :].astype(F32) * scale
        k = kf_ref[pl.ds(r0, C), :]
        b = b_ref[pl.ds(r0, C), :]
        v = v_ref[pl.ds(r0, C), :]

        s = s_ref[...]
        o = _dot((q * jnp.exp(b)).astype(BF16), s.astype(BF16))
        s_ref[...] = s * dec_ref[ci] + _dot(kts_ref[ci], v)

        attn = jnp.zeros((C, C), F32)
        h = C // 2
        while h >= SB:
            ref = jnp.broadcast_to(b.reshape(C // (2 * h), 2 * h, GLA_DK)[:, h - 1:h, :],
                                   (C // (2 * h), 2 * h, GLA_DK)).reshape(C, GLA_DK)
            upper = (row // h) % 2 == 1
            q_h = q * jnp.exp(jnp.where(upper, b - ref, neg))
            k_h = k * jnp.exp(jnp.where(upper, neg, ref - b))
            a_h = _dot_nt(q_h.astype(BF16), k_h.astype(BF16))
            attn = attn + (a_h if 2 * h == C else jnp.where(rowi // (2 * h) == coli // (2 * h), a_h, 0.0))
            h //= 2

        zs = []
        for bi in range(C // SB):
            s0 = bi * SB
            q_i = q[s0:s0 + SB, :]
            b_i = b[s0:s0 + SB, :]
            for j in range(SB):
                k_j = kf_ref[pl.ds(r0 + (s0 + j), 1), :]
                b_j = b_ref[pl.ds(r0 + (s0 + j), 1), :]
                zs.append(q_i * k_j * jnp.exp(jnp.where(sub_r >= j, b_i - b_j, neg)))
        zsum = _dot(jnp.concatenate(zs, axis=0).astype(BF16), ones)
        diag = []
        for bi in range(C // SB):
            acc = jnp.zeros((SB, C), F32)
            for j in range(SB):
                n0 = (bi * SB + j) * SB
                acc = acc + jnp.where(lane_c == bi * SB + j, zsum[n0:n0 + SB, :], 0.0)
            diag.append(acc)
        attn = attn + jnp.concatenate(diag, axis=0)
        o = o + _dot(attn.astype(BF16), v)

        o = o * lax.rsqrt(jnp.mean(o * o, axis=-1, keepdims=True) + EPS) * hg_ref[...]
        g = gate_ref[pl.ds(r0, C), :].astype(F32)
        o_ref[pl.ds(r0, C), :] = (o * _silu(g)).astype(o_ref.dtype)
        return carry

    lax.fori_loop(0, tb // C, chunk, 0, unroll=4)


def gla(proj, small, alpha_up, alpha_b, head_g, *, batch, seq, q_col, k_col, v_col, gate_col, tb=512):
    t = batch * seq
    nb = seq // tb
    dk, dv, h = GLA_DK, GLA_DV, GLA_HEADS
    qb, kb, vb, gb = q_col // dk, k_col // dk, v_col // dv, gate_col // dv
    tok = lambda b, hh, i: b * nb + i
    return pl.pallas_call(
        functools.partial(_gla_body, tb=tb),
        grid=(batch, h, nb),
        in_specs=[
            pl.BlockSpec((tb, dk), lambda b, hh, i: (tok(b, hh, i), qb + hh)),
            pl.BlockSpec((tb, dk), lambda b, hh, i: (tok(b, hh, i), kb + hh)),
            pl.BlockSpec((tb, dv), lambda b, hh, i: (tok(b, hh, i), vb + hh)),
            pl.BlockSpec((tb, 128), lambda b, hh, i: (tok(b, hh, i), 0)),
            pl.BlockSpec((GLA_LOWRANK, dk), lambda b, hh, i: (0, hh)),
            pl.BlockSpec((1, dk), lambda b, hh, i: (0, hh)),
            pl.BlockSpec((tb, dv), lambda b, hh, i: (tok(b, hh, i), gb + hh)),
            pl.BlockSpec((1, dv), lambda b, hh, i: (0, hh)),
        ],
        out_specs=pl.BlockSpec((tb, dv), lambda b, hh, i: (tok(b, hh, i), hh)),
        out_shape=jax.ShapeDtypeStruct((t, h * dv), BF16),
        scratch_shapes=[pltpu.VMEM((dk, dv), F32), pltpu.VMEM((tb, dk), F32), pltpu.VMEM((tb, dk), F32),
                        pltpu.VMEM((tb // GLA_CHUNK, dk, GLA_CHUNK), BF16),
                        pltpu.VMEM((tb // GLA_CHUNK, dk, dv), F32)],
        compiler_params=_cparams(("parallel", "parallel", "arbitrary")),
        name="gla",
    )(proj, proj, proj, small, alpha_up, alpha_b.reshape(1, -1), proj, head_g.reshape(1, -1))


def _fox_gate_body(s_ref, fb_ref, qa_ref, ka_ref, *, blk, col0):
    n = s_ref.shape[0] // blk
    r = lax.broadcasted_iota(jnp.int32, (blk, blk), 0)
    c = lax.broadcasted_iota(jnp.int32, (blk, blk), 1)
    tri = jnp.where(c <= r, 1.0, 0.0).astype(BF16)
    lane = lax.broadcasted_iota(jnp.int32, (blk, FOX_AUG), 1)

    def step(i, carry):
        r0 = pl.multiple_of(i * blk, blk)
        lf = _log_sigmoid(s_ref[pl.ds(r0, blk), :] + fb_ref[...])
        hi, mid, lo = _split3(lf)
        cs = _dot(tri, hi) + _dot(tri, mid) + _dot(tri, lo) + carry
        for h in range(FOX_HEADS):
            col = cs[:, col0 + h:col0 + h + 1] * LOG2E
            c0, c1, c2 = (p.astype(F32) for p in _split3(col))
            qa = jnp.where(lane == 0, c0, jnp.where(lane == 1, c1, jnp.where(lane == 2, c2,
                           jnp.where(lane < 6, 1.0, 0.0))))
            ka = jnp.where(lane < 3, 1.0, jnp.where(lane == 3, -c0, jnp.where(lane == 4, -c1,
                           jnp.where(lane == 5, -c2, 0.0))))
            qa_ref[h, pl.ds(r0, blk), :] = qa.astype(BF16)
            ka_ref[h, pl.ds(r0, blk), :] = ka.astype(BF16)
        return cs[blk - 1:blk, :]

    lax.fori_loop(0, n, step, jnp.zeros((1, s_ref.shape[1]), F32))


def fox_gate(small, fb_row, *, batch, seq, col0, blk=256):
    shp = jax.ShapeDtypeStruct((FOX_HEADS, batch * seq, FOX_AUG), BF16)
    spec = pl.BlockSpec((FOX_HEADS, seq, FOX_AUG), lambda b: (0, b, 0))
    return pl.pallas_call(
        functools.partial(_fox_gate_body, blk=blk, col0=col0),
        grid=(batch,),
        in_specs=[pl.BlockSpec((seq, 128), lambda b: (b, 0)),
                  pl.BlockSpec((1, 128), lambda b: (0, 0))],
        out_specs=[spec, spec],
        out_shape=[shp, shp],
        compiler_params=_cparams(("parallel",)),
        name="fox_gate",
    )(small, fb_row)


def _fox_body(it_ref, jt_ref, q_ref, qa_ref, k_ref, ka_ref, v_ref, gate_ref, o_ref,
              m_ref, l_ref, acc_ref, *, tq, tk, hpb):
    p = pl.program_id(2)
    i = it_ref[p]
    j = jt_ref[p]
    d = FOX_DH

    @pl.when(j == 0)
    def _():
        m_ref[...] = jnp.full_like(m_ref, -jnp.inf)
        l_ref[...] = jnp.zeros_like(l_ref)
        acc_ref[...] = jnp.zeros_like(acc_ref)

    def update(hh, masked):
        q_aug = jnp.concatenate([q_ref[:, hh * d:(hh + 1) * d], qa_ref[hh]], axis=1)
        k_aug = jnp.concatenate([k_ref[:, hh * d:(hh + 1) * d], ka_ref[hh]], axis=1)
        st = _dot_nt(k_aug, q_aug)
        if masked:
            kr = lax.broadcasted_iota(jnp.int32, (tk, tq), 0)
            qc = lax.broadcasted_iota(jnp.int32, (tk, tq), 1)
            st = jnp.where(qc >= kr, st, -jnp.inf)
        m_old = m_ref[hh]
        m_new = jnp.maximum(m_old, jnp.max(st, axis=0, keepdims=True))
        alpha = jnp.exp2(m_old - m_new)
        pt = jnp.exp2(st - m_new)
        l_ref[hh] = alpha * l_ref[hh] + jnp.sum(pt, axis=0, keepdims=True)
        acc_ref[hh] = alpha * acc_ref[hh] + _dot_tn(v_ref[:, hh * d:(hh + 1) * d], pt.astype(BF16))
        m_ref[hh] = m_new

    @pl.when(j < i)
    def _():
        for hh in range(hpb):
            update(hh, False)

    @pl.when(j == i)
    def _():
        for hh in range(hpb):
            update(hh, True)
            o = (acc_ref[hh] / l_ref[hh]).T
            g = gate_ref[:, hh * d:(hh + 1) * d].astype(F32)
            o_ref[:, hh * d:(hh + 1) * d] = (o * _silu(g)).astype(o_ref.dtype)


def fox(proj, qa, ka, *, batch, seq, q_col, k_col, v_col, gate_col, t=1024, hpb=4):
    tt = batch * seq
    nb = seq // t
    d, h = FOX_DH, FOX_HEADS
    w = hpb * d
    qb, kb, gb, vb = q_col // w, k_col // w, gate_col // w, v_col // w
    pairs = [(i, j) for i in range(nb) for j in range(i + 1)]
    it = jnp.array([p[0] for p in pairs], jnp.int32)
    jt = jnp.array([p[1] for p in pairs], jnp.int32)
    grid_spec = pltpu.PrefetchScalarGridSpec(
        num_scalar_prefetch=2,
        grid=(batch, h // hpb, len(pairs)),
        in_specs=[
            pl.BlockSpec((t, w), lambda b, hh, p, it, jt: (b * nb + it[p], qb + hh)),
            pl.BlockSpec((hpb, t, FOX_AUG), lambda b, hh, p, it, jt: (hh, b * nb + it[p], 0)),
            pl.BlockSpec((t, w), lambda b, hh, p, it, jt: (b * nb + jt[p], kb + hh)),
            pl.BlockSpec((hpb, t, FOX_AUG), lambda b, hh, p, it, jt: (hh, b * nb + jt[p], 0)),
            pl.BlockSpec((t, w), lambda b, hh, p, it, jt: (b * nb + jt[p], vb + hh)),
            pl.BlockSpec((t, w), lambda b, hh, p, it, jt: (b * nb + it[p], gb + hh)),
        ],
        out_specs=pl.BlockSpec((t, w), lambda b, hh, p, it, jt: (b * nb + it[p], hh)),
        scratch_shapes=[pltpu.VMEM((hpb, 1, t), F32), pltpu.VMEM((hpb, 1, t), F32), pltpu.VMEM((hpb, d, t), F32)],
    )
    return pl.pallas_call(
        functools.partial(_fox_body, tq=t, tk=t, hpb=hpb),
        grid_spec=grid_spec,
        out_shape=jax.ShapeDtypeStruct((tt, h * d), BF16),
        compiler_params=_cparams(("parallel", "parallel", "arbitrary")),
        name="fox",
    )(it, jt, proj, qa, proj, ka, proj, proj)


def _regroup_perm(tm, steps, to_grouped):
    r = lax.broadcasted_iota(jnp.int32, (tm, tm), 0)
    c = lax.broadcasted_iota(jnp.int32, (tm, tm), 1)
    nc = tm // steps
    src = (r % nc) * steps + r // nc if to_grouped else (r % steps) * nc + r // steps
    return jnp.where(c == src, 1.0, 0.0).astype(BF16)


def _outproj_mid_body(a1_ref, a2_ref, w_ref, x_ref, g_ref, r_ref, h_ref, *, steps):
    k1 = a1_ref.shape[1]
    tm, d = x_ref.shape
    nc = tm // steps
    acc = x_ref[...] + _dot(a1_ref[...], w_ref[:k1, :]) + _dot(a2_ref[...], w_ref[k1:, :])
    r_ref[...] = acc
    normed = (acc * lax.rsqrt(jnp.mean(acc * acc, axis=-1, keepdims=True) + EPS) * g_ref[...]).astype(BF16)
    by_step = _dot(_regroup_perm(tm, steps, True), normed).astype(h_ref.dtype)
    for s in range(steps):
        h_ref[:, s * d:(s + 1) * d] = by_step[s * nc:(s + 1) * nc, :]


def outproj_mid(a1, a2, w, x, g, *, steps, tm=256, name="outproj_mid"):
    m, d = x.shape
    row = pl.BlockSpec((tm, d), lambda i: (i, 0))
    return pl.pallas_call(
        functools.partial(_outproj_mid_body, steps=steps),
        grid=(m // tm,),
        in_specs=[pl.BlockSpec((tm, a1.shape[1]), lambda i: (i, 0)),
                  pl.BlockSpec((tm, a2.shape[1]), lambda i: (i, 0)),
                  pl.BlockSpec(w.shape, lambda i: (0, 0)),
                  row,
                  pl.BlockSpec((1, d), lambda i: (0, 0))],
        out_specs=[row, pl.BlockSpec((tm // steps, steps * d), lambda i: (i, 0))],
        out_shape=[jax.ShapeDtypeStruct((m, d), F32), jax.ShapeDtypeStruct((m // steps, steps * d), BF16)],
        compiler_params=_cparams(("parallel",)),
        name=name,
    )(a1, a2, w, x, g.reshape(1, d))


def _outproj_final_body(a_ref, w_ref, x_ref, g_ref, o_ref, *, steps):
    tm = x_ref.shape[0]
    e = w_ref.shape[0]
    by_step = jnp.concatenate([a_ref[:, s * e:(s + 1) * e] for s in range(steps)], axis=0)
    a_tok = _dot(_regroup_perm(tm, steps, False), by_step).astype(BF16)
    acc = x_ref[...] + _dot(a_tok, w_ref[...])
    o_ref[...] = acc * lax.rsqrt(jnp.mean(acc * acc, axis=-1, keepdims=True) + EPS) * g_ref[...]


def outproj_final(a_grouped, w, x, g, *, steps, tm=256, name="outproj_final"):
    m, d = x.shape
    e = w.shape[0]
    row = pl.BlockSpec((tm, d), lambda i: (i, 0))
    return pl.pallas_call(
        functools.partial(_outproj_final_body, steps=steps),
        grid=(m // tm,),
        in_specs=[pl.BlockSpec((tm // steps, steps * e), lambda i: (i, 0)),
                  pl.BlockSpec(w.shape, lambda i: (0, 0)),
                  row,
                  pl.BlockSpec((1, d), lambda i: (0, 0))],
        out_specs=row,
        out_shape=jax.ShapeDtypeStruct((m, d), F32),
        compiler_params=_cparams(("parallel",)),
        name=name,
    )(a_grouped, w, x, g.reshape(1, d))


S5_GPB = 2


def _s5_group(u, wt, nt, mt, d_col, arow, *, cols_per_seq, n_levels):
    n = u.shape[1]
    p = S5_STATE
    nseq = n // cols_per_seq
    x = _dot(nt, u)

    def to_rows(part):
        return jnp.concatenate([part[:, s * cols_per_seq:(s + 1) * cols_per_seq] for s in range(nseq)], axis=0).T

    def to_cols(rows):
        rt = rows.T
        return jnp.concatenate([rt[s * p:(s + 1) * p, :] for s in range(nseq)], axis=1)

    xr, xi = to_rows(x[:p, :]), to_rows(x[p:, :])
    pos = lax.broadcasted_iota(jnp.int32, (cols_per_seq, 1), 0)
    for lvl in range(n_levels):
        sh = 1 << lvl
        keep = pos >= sh
        sr = jnp.where(keep, pltpu.roll(xr, sh, axis=0), 0.0)
        si = jnp.where(keep, pltpu.roll(xi, sh, axis=0), 0.0)
        ar, ai = arow[lvl:lvl + 1, :], arow[n_levels + lvl:n_levels + lvl + 1, :]
        xr, xi = xr + ar * sr - ai * si, xi + ar * si + ai * sr
    keep = pos >= 1
    pr = jnp.where(keep, pltpu.roll(xr, 1, axis=0), 0.0)
    pi = jnp.where(keep, pltpu.roll(xi, 1, axis=0), 0.0)
    x_prev = jnp.concatenate([to_cols(pr), to_cols(pi)], axis=0).astype(BF16)
    return _dot(wt, u) + _dot(mt, x_prev) + d_col * u.astype(F32)


def _s5_body(u_ref, wt_ref, nt_ref, mt_ref, tab_ref, arow_ref, o_ref, *, cols_per_seq, n_levels):
    t, _, n = u_ref.shape
    cg = S5_GROUP_CH
    for gi in range(S5_GPB):
        ch = slice(gi * cg, (gi + 1) * cg)
        u = u_ref[:, ch, :].reshape(t * cg, n)
        y = _s5_group(u, wt_ref[gi], nt_ref[gi], mt_ref[gi], tab_ref[gi][:, 0:1], arow_ref[gi],
                      cols_per_seq=cols_per_seq, n_levels=n_levels)
        o_ref[:, ch, :] = y.astype(o_ref.dtype).reshape(t, cg, n)


def s5_apply(proj_t, wt_g, nt_g, mt_g, tab, arow, *, groups, cols_per_seq, n_levels):
    t, _, n = proj_t.shape
    cg = S5_GROUP_CH * S5_GPB
    spec3 = lambda shp: pl.BlockSpec((S5_GPB,) + shp, lambda i: (i, 0, 0))
    return pl.pallas_call(
        functools.partial(_s5_body, cols_per_seq=cols_per_seq, n_levels=n_levels),
        grid=(groups // S5_GPB,),
        in_specs=[pl.BlockSpec((t, cg, n), lambda i: (0, i, 0)),
                  spec3(wt_g.shape[1:]), spec3(nt_g.shape[1:]), spec3(mt_g.shape[1:]), spec3(tab.shape[1:]),
                  spec3(arow.shape[1:])],
        out_specs=pl.BlockSpec((t, cg, n), lambda i: (0, i, 0)),
        out_shape=jax.ShapeDtypeStruct((t, groups * S5_GROUP_CH, n), BF16),
        compiler_params=_cparams(("parallel",)),
        name="s5",
    )(proj_t, wt_g, nt_g, mt_g, tab, arow)


def s5_operators(lam_re, lam_im, log_step, b_re, b_im, c_re, c_im, d, *, n_levels, n_seq):
    hp = lax.Precision.HIGHEST
    g, p = lam_re.shape
    cg, t = S5_GROUP_CH, S5_T
    lr = jnp.minimum(lam_re.astype(F32), -1e-4)
    li = lam_im.astype(F32)
    step = jnp.exp(log_step.astype(F32))[:, None]
    mag = jnp.exp(lr * step)
    lb_re, lb_im = mag * jnp.cos(li * step), mag * jnp.sin(li * step)
    den = lr * lr + li * li
    nr, ni = lb_re - 1.0, lb_im
    coef_re = (nr * lr + ni * li) / den
    coef_im = (ni * lr - nr * li) / den
    bb_re = coef_re[..., None] * b_re - coef_im[..., None] * b_im
    bb_im = coef_re[..., None] * b_im + coef_im[..., None] * b_re

    def power(m):
        mm = m.astype(F32)[None, :, None]
        mg = jnp.exp(lr[:, None, :] * step[:, None, :] * mm)
        ang = li[:, None, :] * step[:, None, :] * mm
        return mg * jnp.cos(ang), mg * jnp.sin(ang)

    pw_re, pw_im = power(jnp.arange(t + 1))
    cp_re = c_re[:, None] * pw_re[:, :, None, :] - c_im[:, None] * pw_im[:, :, None, :]
    cp_im = c_re[:, None] * pw_im[:, :, None, :] + c_im[:, None] * pw_re[:, :, None, :]
    taps = (jnp.einsum('gxcp,gpd->gxcd', cp_re[:, :t], bb_re, precision=hp)
            - jnp.einsum('gxcp,gpd->gxcd', cp_im[:, :t], bb_im, precision=hp))
    shift = (jnp.arange(t)[None, None, :] - jnp.arange(t)[None, :, None]
             == jnp.arange(t)[:, None, None]).astype(F32)
    wt_g = jnp.einsum('xst,gxcd->gtcsd', shift, taps, precision=hp).reshape(g, t * cg, t * cg)
    pr, pi = pw_re[:, t - 1::-1], pw_im[:, t - 1::-1]
    bt_re, bt_im = bb_re[:, :, None, :], bb_im[:, :, None, :]
    prt, pit = pr.transpose(0, 2, 1)[:, :, :, None], pi.transpose(0, 2, 1)[:, :, :, None]
    n_re = (prt * bt_re - pit * bt_im).reshape(g, p, t * cg)
    n_im = (prt * bt_im + pit * bt_re).reshape(g, p, t * cg)
    nt_g = jnp.concatenate([n_re, n_im], axis=1)
    m_re = cp_re[:, 1:].reshape(g, t * cg, p)
    m_im = cp_im[:, 1:].reshape(g, t * cg, p)
    mt_g = jnp.concatenate([m_re, -m_im], axis=2)
    ar, ai = power(t * (2 ** jnp.arange(n_levels)))
    arow = jnp.concatenate([jnp.tile(ar, (1, 1, n_seq)), jnp.tile(ai, (1, 1, n_seq))], axis=1)
    d_col = jnp.tile(d.astype(F32).reshape(g, cg), (1, t))[:, :, None]
    tab = d_col
    tab = jnp.pad(tab, ((0, 0), (0, 0), (0, S5_TAB - tab.shape[2])))
    return wt_g.astype(BF16), nt_g.astype(BF16), mt_g.astype(BF16), tab, arow


def _gelu_tanh(y):
    return 0.5 * y * (1.0 + jnp.tanh(math.sqrt(2.0 / math.pi) * (y + 0.044715 * (y * y * y))))


def _glu_body(y_ref, w_ref, b_ref, gate_ref, o_ref, z_ref, zb_ref, wt_ref):
    @pl.when((pl.program_id(0) == 0) & (pl.program_id(1) == 0))
    def _():
        _transpose_into(wt_ref, w_ref)

    z = _gelu_tanh(y_ref[...].astype(F32))
    z_ref[...] = z
    zb_ref[...] = z.astype(BF16)
    for f0 in range(0, o_ref.shape[1], MM_SUB):
        rows = slice(f0, f0 + MM_SUB)
        lin = _dot(wt_ref[rows, :], zb_ref[...]) + b_ref[rows, :]
        out = z_ref[rows, :] * _sigmoid(lin) * _silu(gate_ref[rows, :].astype(F32))
        o_ref[:, rows] = out.T.astype(o_ref.dtype)


def glu_gate(y_t, w, b, proj_t, *, gate_row, tm=512):
    s, e, n = y_t.shape
    gb = gate_row // e
    return pl.pallas_call(
        _glu_body,
        grid=(s, n // tm),
        in_specs=[pl.BlockSpec((None, e, tm), lambda t, i: (t, 0, i)),
                  pl.BlockSpec((e, e), lambda t, i: (0, 0)),
                  pl.BlockSpec((e, 1), lambda t, i: (0, 0)),
                  pl.BlockSpec((None, e, tm), lambda t, i: (t, gb, i))],
        out_specs=pl.BlockSpec((tm, e), lambda t, i: (i, t)),
        out_shape=jax.ShapeDtypeStruct((n, s * e), BF16),
        scratch_shapes=[pltpu.VMEM((e, tm), F32), pltpu.VMEM((e, tm), BF16), pltpu.VMEM((e, e), BF16)],
        compiler_params=_cparams(("arbitrary", "arbitrary")),
        name="glu",
    )(y_t, w, b.reshape(e, 1), proj_t)


def kernel(x, ab_norm_g, ab_w_in, gla_alpha_up, gla_alpha_b, gla_head_g, fox_f_b, ab_w_out, c_norm_g, c_w_in, s5_lambda_re, s5_lambda_im, s5_log_step, s5_b_re, s5_b_im, s5_c_re, s5_c_im, s5_d, glu_w, glu_b, c_w_out, final_norm_g):
    batch, seq, d = x.shape
    t = batch * seq
    x2 = x.reshape(t, d)

    hk, hv, fw = GLA_HEADS * GLA_DK, GLA_HEADS * GLA_DV, FOX_HEADS * FOX_DH
    sizes = (hk, hk, hv, GLA_LOWRANK, hv, fw, fw, fw, FOX_HEADS, fw)
    offs = [0]
    for s in sizes:
        offs.append(offs[-1] + s)
    w_main, w_small = ab_weight_prep(jnp.swapaxes(ab_w_in, 1, 2), offs)
    c_gq, c_gk, c_gv, c_gg = 0, hk, 2 * hk, 2 * hk + hv
    c_fq = c_gg + hv
    c_fk, c_fv, c_fg = c_fq + fw, c_fq + 2 * fw, c_fq + 3 * fw

    h0, small = rmsnorm_bf16(x2, ab_norm_g[0], w_small)
    proj = matmul(h0, w_main, BF16, tn=w_main.shape[0] // 2, name="ab_in")

    o_gla = gla(proj, small, gla_alpha_up[0], gla_alpha_b[0], gla_head_g[0],
                batch=batch, seq=seq, q_col=c_gq, k_col=c_gk, v_col=c_gv, gate_col=c_gg)

    fb_row = jnp.zeros((1, 128), F32).at[0, GLA_LOWRANK:GLA_LOWRANK + FOX_HEADS].set(fox_f_b[0])
    qa, ka = fox_gate(small, fb_row, batch=batch, seq=seq, col0=GLA_LOWRANK)
    o_fox = fox(proj, qa, ka, batch=batch, seq=seq, q_col=c_fq, k_col=c_fk, v_col=c_fv, gate_col=c_fg)

    x1, h1g = outproj_mid(o_gla, o_fox, ab_w_out[0].astype(BF16), x2, c_norm_g[0], steps=S5_T, name="ab_out")

    e = d
    groups = e // S5_GROUP_CH
    n_chunks = t // S5_T
    cols_per_seq = seq // S5_T
    n_levels = max(1, (cols_per_seq - 1).bit_length())
    proj_t = matmul_t_grouped(c_w_in[0].astype(BF16), h1g, S5_T, BF16, name="c_in")
    ops = s5_operators(s5_lambda_re[0], s5_lambda_im[0], s5_log_step[0], s5_b_re[0], s5_b_im[0],
                       s5_c_re[0], s5_c_im[0], s5_d[0], n_levels=n_levels, n_seq=batch)
    y_t = s5_apply(proj_t, *ops, groups=groups, cols_per_seq=cols_per_seq, n_levels=n_levels)
    zz = glu_gate(y_t, glu_w[0].astype(BF16), glu_b[0], proj_t, gate_row=e)
    out = outproj_final(zz, c_w_out[0].astype(BF16), x1, final_norm_g, steps=S5_T, name="c_out")
    return out.reshape(batch, seq, d)
```

```python
import functools
import math

import jax
import jax.numpy as jnp
from jax import lax
from jax.experimental import pallas as pl
from jax.experimental.pallas import tpu as pltpu

EPS = 1e-6
F32 = jnp.float32
BF16 = jnp.bfloat16

GLA_HEADS = 4
GLA_DK = 128
GLA_DV = 256
GLA_LOWRANK = 16
GLA_TAU = 16.0
GLA_CHUNK = 64
GLA_SUB = 8
GLA_UNROLL = 2
FOX_HEADS = 8
FOX_DH = 128
FOX_AUG = 128
S5_GROUP_CH = 16
S5_STATE = 64
S5_T = 16
S5_TAB = 32

VMEM_LIMIT = 56 * 1024 * 1024

NT_DIMS = (((1,), (1,)), ((), ()))
TN_DIMS = (((0,), (0,)), ((), ()))
LOG2E = math.log2(math.e)


def _cparams(sem):
    return pltpu.CompilerParams(dimension_semantics=sem, vmem_limit_bytes=VMEM_LIMIT)


def _dot(a, b):
    return jnp.dot(a, b, preferred_element_type=F32)


def _dot_nt(a, b):
    return lax.dot_general(a, b, NT_DIMS, preferred_element_type=F32)


def _dot_tn(a, b):
    return lax.dot_general(a, b, TN_DIMS, preferred_element_type=F32)


def _log_sigmoid(z):
    return -(jnp.maximum(-z, 0.0) + jnp.log1p(jnp.exp(-jnp.abs(z))))


def _sigmoid(z):
    return 1.0 / (1.0 + jnp.exp(-z))


def _silu(z):
    return z * _sigmoid(z)


def _split3(x):
    hi = x.astype(BF16)
    r1 = x - hi.astype(F32)
    mid = r1.astype(BF16)
    lo = (r1 - mid.astype(F32)).astype(BF16)
    return hi, mid, lo


def _rmsnorm_body(x_ref, g_ref, ws_ref, o_ref, s_ref):
    x = x_ref[...]
    ms = jnp.mean(x * x, axis=-1, keepdims=True)
    h = (x * lax.rsqrt(ms + EPS) * g_ref[...]).astype(o_ref.dtype)
    o_ref[...] = h
    s_ref[...] = _dot_nt(h, ws_ref[...])


def rmsnorm_bf16(x, g, wt_small, tm=512):
    m, d = x.shape
    ns = wt_small.shape[0]
    return pl.pallas_call(
        _rmsnorm_body,
        grid=(m // tm,),
        in_specs=[pl.BlockSpec((tm, d), lambda i: (i, 0)),
                  pl.BlockSpec((1, d), lambda i: (0, 0)),
                  pl.BlockSpec((ns, d), lambda i: (0, 0))],
        out_specs=[pl.BlockSpec((tm, d), lambda i: (i, 0)),
                   pl.BlockSpec((tm, ns), lambda i: (i, 0))],
        out_shape=[jax.ShapeDtypeStruct((m, d), BF16), jax.ShapeDtypeStruct((m, ns), F32)],
        compiler_params=_cparams(("parallel",)),
        name="rmsnorm",
    )(x, g.reshape(1, d), wt_small)


def _ab_wprep_body(w_ref, o_ref, s_ref, *, offs):
    w = w_ref[0]
    seg = lambda n: w[offs[n]:offs[n + 1], :]
    o_ref[...] = jnp.concatenate(
        [w[:offs[3], :], seg(4), seg(5) * (FOX_DH ** -0.5 * LOG2E), w[offs[6]:offs[8], :], seg(9)],
        axis=0).astype(o_ref.dtype)
    pad = s_ref.shape[0] - (offs[4] - offs[3]) - (offs[9] - offs[8])
    s_ref[...] = jnp.concatenate([seg(3), seg(8), jnp.zeros((pad, w.shape[1]), F32)], axis=0).astype(s_ref.dtype)


def ab_weight_prep(w_t3, offs, tk=256):
    _, n_in, d = w_t3.shape
    n_main = n_in - (offs[4] - offs[3]) - (offs[9] - offs[8])
    return pl.pallas_call(
        functools.partial(_ab_wprep_body, offs=tuple(offs)),
        grid=(d // tk,),
        in_specs=[pl.BlockSpec((1, n_in, tk), lambda i: (0, 0, i))],
        out_specs=[pl.BlockSpec((n_main, tk), lambda i: (0, i)), pl.BlockSpec((128, tk), lambda i: (0, i))],
        out_shape=[jax.ShapeDtypeStruct((n_main, d), BF16), jax.ShapeDtypeStruct((128, d), BF16)],
        compiler_params=_cparams(("parallel",)),
        name="ab_wprep",
    )(w_t3)


MM_SUB = 512


def _mm_body(a_ref, wt_ref, o_ref):
    a = a_ref[...]
    for c0 in range(0, o_ref.shape[1], MM_SUB):
        o_ref[:, c0:c0 + MM_SUB] = _dot_nt(a, wt_ref[c0:c0 + MM_SUB, :]).astype(o_ref.dtype)


def matmul(a, wt, out_dtype, tm=512, tn=512, name="matmul"):
    m, k = a.shape
    n = wt.shape[0]
    return pl.pallas_call(
        _mm_body,
        grid=(n // tn, m // tm),
        in_specs=[pl.BlockSpec((tm, k), lambda j, i: (i, 0)),
                  pl.BlockSpec((tn, k), lambda j, i: (j, 0))],
        out_specs=pl.BlockSpec((tm, tn), lambda j, i: (i, j)),
        out_shape=jax.ShapeDtypeStruct((m, n), out_dtype),
        compiler_params=_cparams(("parallel", "parallel")),
        name=name,
    )(a, wt)


def _transpose_into(wt_ref, w_ref):
    for c0 in range(0, w_ref.shape[1], MM_SUB):
        wt_ref[c0:c0 + MM_SUB, :] = w_ref[:, c0:c0 + MM_SUB].T


def _mm_t_body(w_ref, a_ref, o_ref, wt_ref):
    @pl.when((pl.program_id(1) == 0) & (pl.program_id(2) == 0))
    def _():
        _transpose_into(wt_ref, w_ref)

    a = a_ref[...]
    for r0 in range(0, o_ref.shape[0], MM_SUB):
        o_ref[r0:r0 + MM_SUB, :] = _dot_nt(wt_ref[r0:r0 + MM_SUB, :], a).astype(o_ref.dtype)


def matmul_t_grouped(w, a2, steps, out_dtype, tm=512, tn=2048, name="matmul_tg"):
    k, nf = w.shape
    n = a2.shape[0]
    return pl.pallas_call(
        _mm_t_body,
        grid=(nf // tn, steps, n // tm),
        in_specs=[pl.BlockSpec((k, tn), lambda j, t, i: (0, j)),
                  pl.BlockSpec((tm, k), lambda j, t, i: (i, t))],
        out_specs=pl.BlockSpec((None, tn, tm), lambda j, t, i: (t, j, i)),
        out_shape=jax.ShapeDtypeStruct((steps, nf, n), out_dtype),
        scratch_shapes=[pltpu.VMEM((tn, k), BF16)],
        compiler_params=_cparams(("arbitrary", "arbitrary", "arbitrary")),
        name=name,
    )(w, a2)


def _gla_body(q_ref, k_ref, v_ref, glr_ref, aup_ref, ab_ref, gate_ref, hg_ref, o_ref,
              s_ref, b_ref, kf_ref, kts_ref, dec_ref, *, tb, hpb):
    ib = pl.program_id(2)
    C, SB, dk, dv = GLA_CHUNK, GLA_SUB, GLA_DK, GLA_DV

    @pl.when(ib == 0)
    def _():
        s_ref[...] = jnp.zeros_like(s_ref)

    glr = glr_ref[:, :GLA_LOWRANK].astype(BF16)
    tw = 2 * C
    r = lax.broadcasted_iota(jnp.int32, (tw, tw), 0)
    c = lax.broadcasted_iota(jnp.int32, (tw, tw), 1)
    tri = jnp.where((r // C == c // C) & (c <= r), 1.0, 0.0).astype(BF16)
    nc = tb // C
    for hh in range(hpb):
        kc = slice(hh * dk, (hh + 1) * dk)
        z = _dot(glr, aup_ref[:, kc].astype(BF16)) + ab_ref[:, kc]
        la = _log_sigmoid(z) * (1.0 / GLA_TAU)
        hi = la.astype(BF16)
        lo = (la - hi.astype(F32)).astype(BF16)
        b_blk = jnp.concatenate(
            [_dot(tri, hi[r0:r0 + tw]) + _dot(tri, lo[r0:r0 + tw]) for r0 in range(0, tb, tw)], axis=0)
        k_blk = k_ref[:, kc].astype(F32)
        b_ref[hh] = b_blk
        kf_ref[hh] = k_blk
        b3 = b_blk.reshape(nc, C, dk)
        b_end = b3[:, C - 1:C, :]
        k_end = (k_blk.reshape(nc, C, dk) * jnp.exp(b_end - b3)).reshape(tb, dk)
        kt_end = k_end.T.astype(BF16)
        dec_t = jnp.exp(b_end.reshape(nc, dk)).T
        for ci in range(nc):
            kts_ref[hh, ci] = kt_end[:, ci * C:(ci + 1) * C]
            dec_ref[hh, ci] = jnp.broadcast_to(dec_t[:, ci:ci + 1], (dk, dv))

    row = lax.broadcasted_iota(jnp.int32, (C, 1), 0)
    rowi = lax.broadcasted_iota(jnp.int32, (C, C), 0)
    coli = lax.broadcasted_iota(jnp.int32, (C, C), 1)
    lane_c = lax.broadcasted_iota(jnp.int32, (SB, C), 1)
    sub_r = lax.broadcasted_iota(jnp.int32, (SB, 1), 0)
    ones = jnp.ones((GLA_DK, C), BF16)
    neg = -jnp.inf
    scale = GLA_DK ** -0.5

    def chunk_head(ci, r0, hh):
        kc = slice(hh * dk, (hh + 1) * dk)
        vc = slice(hh * dv, (hh + 1) * dv)
        q = q_ref[pl.ds(r0, C), kc].astype(F32) * scale
        k = kf_ref[hh, pl.ds(r0, C), :]
        b = b_ref[hh, pl.ds(r0, C), :]
        v = v_ref[pl.ds(r0, C), vc]

        s = s_ref[hh]
        o = _dot((q * jnp.exp(b)).astype(BF16), s.astype(BF16))
        s_ref[hh] = s * dec_ref[hh, ci] + _dot(kts_ref[hh, ci], v)

        attn = jnp.zeros((C, C), F32)
        h = C // 2
        while h >= SB:
            ref = jnp.broadcast_to(b.reshape(C // (2 * h), 2 * h, GLA_DK)[:, h - 1:h, :],
                                   (C // (2 * h), 2 * h, GLA_DK)).reshape(C, GLA_DK)
            upper = (row // h) % 2 == 1
            q_h = q * jnp.exp(jnp.where(upper, b - ref, neg))
            k_h = k * jnp.exp(jnp.where(upper, neg, ref - b))
            a_h = _dot_nt(q_h.astype(BF16), k_h.astype(BF16))
            attn = attn + (a_h if 2 * h == C else jnp.where(rowi // (2 * h) == coli // (2 * h), a_h, 0.0))
            h //= 2

        zs = []
        for bi in range(C // SB):
            s0 = bi * SB
            q_i = q[s0:s0 + SB, :]
            b_i = b[s0:s0 + SB, :]
            for j in range(SB):
                k_j = kf_ref[hh, pl.ds(r0 + (s0 + j), 1), :]
                b_j = b_ref[hh, pl.ds(r0 + (s0 + j), 1), :]
                zs.append(q_i * k_j * jnp.exp(jnp.where(sub_r >= j, b_i - b_j, neg)))
        zsum = _dot(jnp.concatenate(zs, axis=0).astype(BF16), ones)
        diag = []
        for bi in range(C // SB):
            acc = jnp.zeros((SB, C), F32)
            for j in range(SB):
                n0 = (bi * SB + j) * SB
                acc = acc + jnp.where(lane_c == bi * SB + j, zsum[n0:n0 + SB, :], 0.0)
            diag.append(acc)
        attn = attn + jnp.concatenate(diag, axis=0)
        o = o + _dot(attn.astype(BF16), v)

        o = o * lax.rsqrt(jnp.mean(o * o, axis=-1, keepdims=True) + EPS) * hg_ref[:, vc]
        g = gate_ref[pl.ds(r0, C), vc].astype(F32)
        o_ref[pl.ds(r0, C), vc] = (o * _silu(g)).astype(o_ref.dtype)

    def chunk(ci, carry):
        r0 = pl.multiple_of(ci * C, C)
        for hh in range(hpb):
            chunk_head(ci, r0, hh)
        return carry

    lax.fori_loop(0, tb // C, chunk, 0, unroll=GLA_UNROLL)


def gla(proj, small, alpha_up, alpha_b, head_g, *, batch, seq, q_col, k_col, v_col, gate_col, tb=512, hpb=4):
    t = batch * seq
    nb = seq // tb
    dk, dv, h = GLA_DK * hpb, GLA_DV * hpb, GLA_HEADS // hpb
    qb, kb, vb, gb = q_col // dk, k_col // dk, v_col // dv, gate_col // dv
    tok = lambda b, hh, i: b * nb + i
    nc = tb // GLA_CHUNK
    return pl.pallas_call(
        functools.partial(_gla_body, tb=tb, hpb=hpb),
        grid=(batch, h, nb),
        in_specs=[
            pl.BlockSpec((tb, dk), lambda b, hh, i: (tok(b, hh, i), qb + hh)),
            pl.BlockSpec((tb, dk), lambda b, hh, i: (tok(b, hh, i), kb + hh)),
            pl.BlockSpec((tb, dv), lambda b, hh, i: (tok(b, hh, i), vb + hh)),
            pl.BlockSpec((tb, 128), lambda b, hh, i: (tok(b, hh, i), 0)),
            pl.BlockSpec((GLA_LOWRANK, dk), lambda b, hh, i: (0, hh)),
            pl.BlockSpec((1, dk), lambda b, hh, i: (0, hh)),
            pl.BlockSpec((tb, dv), lambda b, hh, i: (tok(b, hh, i), gb + hh)),
            pl.BlockSpec((1, dv), lambda b, hh, i: (0, hh)),
        ],
        out_specs=pl.BlockSpec((tb, dv), lambda b, hh, i: (tok(b, hh, i), hh)),
        out_shape=jax.ShapeDtypeStruct((t, GLA_HEADS * GLA_DV), BF16),
        scratch_shapes=[pltpu.VMEM((hpb, GLA_DK, GLA_DV), F32), pltpu.VMEM((hpb, tb, GLA_DK), F32),
                        pltpu.VMEM((hpb, tb, GLA_DK), F32),
                        pltpu.VMEM((hpb, nc, GLA_DK, GLA_CHUNK), BF16),
                        pltpu.VMEM((hpb, nc, GLA_DK, GLA_DV), F32)],
        compiler_params=_cparams(("parallel", "parallel", "arbitrary")),
        name="gla",
    )(proj, proj, proj, small, alpha_up, alpha_b.reshape(1, -1), proj, head_g.reshape(1, -1))


def _fox_gate_body(s_ref, fb_ref, qa_ref, ka_ref, *, blk, col0):
    n = s_ref.shape[0] // blk
    r = lax.broadcasted_iota(jnp.int32, (blk, blk), 0)
    c = lax.broadcasted_iota(jnp.int32, (blk, blk), 1)
    tri = jnp.where(c <= r, 1.0, 0.0).astype(BF16)
    lane = lax.broadcasted_iota(jnp.int32, (blk, FOX_AUG), 1)

    def step(i, carry):
        r0 = pl.multiple_of(i * blk, blk)
        lf = _log_sigmoid(s_ref[pl.ds(r0, blk), :] + fb_ref[...])
        hi, mid, lo = _split3(lf)
        cs = _dot(tri, hi) + _dot(tri, mid) + _dot(tri, lo) + carry
        for h in range(FOX_HEADS):
            col = cs[:, col0 + h:col0 + h + 1] * LOG2E
            c0, c1, c2 = (p.astype(F32) for p in _split3(col))
            qa = jnp.where(lane == 0, c0, jnp.where(lane == 1, c1, jnp.where(lane == 2, c2,
                           jnp.where(lane < 6, 1.0, 0.0))))
            ka = jnp.where(lane < 3, 1.0, jnp.where(lane == 3, -c0, jnp.where(lane == 4, -c1,
                           jnp.where(lane == 5, -c2, 0.0))))
            qa_ref[h, pl.ds(r0, blk), :] = qa.astype(BF16)
            ka_ref[h, pl.ds(r0, blk), :] = ka.astype(BF16)
        return cs[blk - 1:blk, :]

    lax.fori_loop(0, n, step, jnp.zeros((1, s_ref.shape[1]), F32))


def fox_gate(small, fb_row, *, batch, seq, col0, blk=256):
    shp = jax.ShapeDtypeStruct((FOX_HEADS, batch * seq, FOX_AUG), BF16)
    spec = pl.BlockSpec((FOX_HEADS, seq, FOX_AUG), lambda b: (0, b, 0))
    return pl.pallas_call(
        functools.partial(_fox_gate_body, blk=blk, col0=col0),
        grid=(batch,),
        in_specs=[pl.BlockSpec((seq, 128), lambda b: (b, 0)),
                  pl.BlockSpec((1, 128), lambda b: (0, 0))],
        out_specs=[spec, spec],
        out_shape=[shp, shp],
        compiler_params=_cparams(("parallel",)),
        name="fox_gate",
    )(small, fb_row)


def _fox_body(it_ref, jt_ref, q_ref, qa_ref, k_ref, ka_ref, v_ref, gate_ref, o_ref,
              m_ref, l_ref, acc_ref, *, tq, tk, hpb):
    p = pl.program_id(2)
    i = it_ref[p]
    j = jt_ref[p]
    d = FOX_DH

    @pl.when(j == 0)
    def _():
        m_ref[...] = jnp.full_like(m_ref, -jnp.inf)
        l_ref[...] = jnp.zeros_like(l_ref)
        acc_ref[...] = jnp.zeros_like(acc_ref)

    def update(hh, masked):
        q_aug = jnp.concatenate([q_ref[:, hh * d:(hh + 1) * d], qa_ref[hh]], axis=1)
        k_aug = jnp.concatenate([k_ref[:, hh * d:(hh + 1) * d], ka_ref[hh]], axis=1)
        st = _dot_nt(k_aug, q_aug)
        if masked:
            kr = lax.broadcasted_iota(jnp.int32, (tk, tq), 0)
            qc = lax.broadcasted_iota(jnp.int32, (tk, tq), 1)
            st = jnp.where(qc >= kr, st, -jnp.inf)
        m_old = m_ref[hh]
        m_new = jnp.maximum(m_old, jnp.max(st, axis=0, keepdims=True))
        alpha = jnp.exp2(m_old - m_new)
        pt = jnp.exp2(st - m_new)
        l_ref[hh] = alpha * l_ref[hh] + jnp.sum(pt, axis=0, keepdims=True)
        acc_ref[hh] = alpha * acc_ref[hh] + _dot_tn(v_ref[:, hh * d:(hh + 1) * d], pt.astype(BF16))
        m_ref[hh] = m_new

    @pl.when(j < i)
    def _():
        for hh in range(hpb):
            update(hh, False)

    @pl.when(j == i)
    def _():
        for hh in range(hpb):
            update(hh, True)
            o = (acc_ref[hh] / l_ref[hh]).T
            g = gate_ref[:, hh * d:(hh + 1) * d].astype(F32)
            o_ref[:, hh * d:(hh + 1) * d] = (o * _silu(g)).astype(o_ref.dtype)


def fox(proj, qa, ka, *, batch, seq, q_col, k_col, v_col, gate_col, t=1024, hpb=4):
    tt = batch * seq
    nb = seq // t
    d, h = FOX_DH, FOX_HEADS
    w = hpb * d
    qb, kb, gb, vb = q_col // w, k_col // w, gate_col // w, v_col // w
    pairs = [(i, j) for i in range(nb) for j in range(i + 1)]
    it = jnp.array([p[0] for p in pairs], jnp.int32)
    jt = jnp.array([p[1] for p in pairs], jnp.int32)
    grid_spec = pltpu.PrefetchScalarGridSpec(
        num_scalar_prefetch=2,
        grid=(batch, h // hpb, len(pairs)),
        in_specs=[
            pl.BlockSpec((t, w), lambda b, hh, p, it, jt: (b * nb + it[p], qb + hh)),
            pl.BlockSpec((hpb, t, FOX_AUG), lambda b, hh, p, it, jt: (hh, b * nb + it[p], 0)),
            pl.BlockSpec((t, w), lambda b, hh, p, it, jt: (b * nb + jt[p], kb + hh)),
            pl.BlockSpec((hpb, t, FOX_AUG), lambda b, hh, p, it, jt: (hh, b * nb + jt[p], 0)),
            pl.BlockSpec((t, w), lambda b, hh, p, it, jt: (b * nb + jt[p], vb + hh)),
            pl.BlockSpec((t, w), lambda b, hh, p, it, jt: (b * nb + it[p], gb + hh)),
        ],
        out_specs=pl.BlockSpec((t, w), lambda b, hh, p, it, jt: (b * nb + it[p], hh)),
        scratch_shapes=[pltpu.VMEM((hpb, 1, t), F32), pltpu.VMEM((hpb, 1, t), F32), pltpu.VMEM((hpb, d, t), F32)],
    )
    return pl.pallas_call(
        functools.partial(_fox_body, tq=t, tk=t, hpb=hpb),
        grid_spec=grid_spec,
        out_shape=jax.ShapeDtypeStruct((tt, h * d), BF16),
        compiler_params=_cparams(("parallel", "parallel", "arbitrary")),
        name="fox",
    )(it, jt, proj, qa, proj, ka, proj, proj)


def _regroup_perm(tm, steps, to_grouped):
    r = lax.broadcasted_iota(jnp.int32, (tm, tm), 0)
    c = lax.broadcasted_iota(jnp.int32, (tm, tm), 1)
    nc = tm // steps
    src = (r % nc) * steps + r // nc if to_grouped else (r % steps) * nc + r // steps
    return jnp.where(c == src, 1.0, 0.0).astype(BF16)


def _outproj_mid_body(a1_ref, a2_ref, w_ref, x_ref, g_ref, r_ref, h_ref, *, steps):
    k1 = a1_ref.shape[1]
    tm, d = x_ref.shape
    nc = tm // steps
    acc = x_ref[...] + _dot(a1_ref[...], w_ref[:k1, :]) + _dot(a2_ref[...], w_ref[k1:, :])
    r_ref[...] = acc
    normed = (acc * lax.rsqrt(jnp.mean(acc * acc, axis=-1, keepdims=True) + EPS) * g_ref[...]).astype(BF16)
    by_step = _dot(_regroup_perm(tm, steps, True), normed).astype(h_ref.dtype)
    for s in range(steps):
        h_ref[:, s * d:(s + 1) * d] = by_step[s * nc:(s + 1) * nc, :]


def outproj_mid(a1, a2, w, x, g, *, steps, tm=256, name="outproj_mid"):
    m, d = x.shape
    row = pl.BlockSpec((tm, d), lambda i: (i, 0))
    return pl.pallas_call(
        functools.partial(_outproj_mid_body, steps=steps),
        grid=(m // tm,),
        in_specs=[pl.BlockSpec((tm, a1.shape[1]), lambda i: (i, 0)),
                  pl.BlockSpec((tm, a2.shape[1]), lambda i: (i, 0)),
                  pl.BlockSpec(w.shape, lambda i: (0, 0)),
                  row,
                  pl.BlockSpec((1, d), lambda i: (0, 0))],
        out_specs=[row, pl.BlockSpec((tm // steps, steps * d), lambda i: (i, 0))],
        out_shape=[jax.ShapeDtypeStruct((m, d), F32), jax.ShapeDtypeStruct((m // steps, steps * d), BF16)],
        compiler_params=_cparams(("parallel",)),
        name=name,
    )(a1, a2, w, x, g.reshape(1, d))


def _outproj_final_body(a_ref, w_ref, x_ref, g_ref, o_ref, *, steps):
    tm = x_ref.shape[0]
    e = w_ref.shape[0]
    by_step = jnp.concatenate([a_ref[:, s * e:(s + 1) * e] for s in range(steps)], axis=0)
    a_tok = _dot(_regroup_perm(tm, steps, False), by_step).astype(BF16)
    acc = x_ref[...] + _dot(a_tok, w_ref[...])
    o_ref[...] = acc * lax.rsqrt(jnp.mean(acc * acc, axis=-1, keepdims=True) + EPS) * g_ref[...]


def outproj_final(a_grouped, w, x, g, *, steps, tm=256, name="outproj_final"):
    m, d = x.shape
    e = w.shape[0]
    row = pl.BlockSpec((tm, d), lambda i: (i, 0))
    return pl.pallas_call(
        functools.partial(_outproj_final_body, steps=steps),
        grid=(m // tm,),
        in_specs=[pl.BlockSpec((tm // steps, steps * e), lambda i: (i, 0)),
                  pl.BlockSpec(w.shape, lambda i: (0, 0)),
                  row,
                  pl.BlockSpec((1, d), lambda i: (0, 0))],
        out_specs=row,
        out_shape=jax.ShapeDtypeStruct((m, d), F32),
        compiler_params=_cparams(("parallel",)),
        name=name,
    )(a_grouped, w, x, g.reshape(1, d))


S5_GPB = 2


def _s5_group(u, wt, nt, mt, d_col, arow, *, cols_per_seq, n_levels):
    n = u.shape[1]
    p = S5_STATE
    nseq = n // cols_per_seq
    x = _dot(nt, u)

    def to_rows(part):
        return jnp.concatenate([part[:, s * cols_per_seq:(s + 1) * cols_per_seq] for s in range(nseq)], axis=0).T

    def to_cols(rows):
        rt = rows.T
        return jnp.concatenate([rt[s * p:(s + 1) * p, :] for s in range(nseq)], axis=1)

    xr, xi = to_rows(x[:p, :]), to_rows(x[p:, :])
    pos = lax.broadcasted_iota(jnp.int32, (cols_per_seq, 1), 0)
    for lvl in range(n_levels):
        sh = 1 << lvl
        keep = pos >= sh
        sr = jnp.where(keep, pltpu.roll(xr, sh, axis=0), 0.0)
        si = jnp.where(keep, pltpu.roll(xi, sh, axis=0), 0.0)
        ar, ai = arow[lvl:lvl + 1, :], arow[n_levels + lvl:n_levels + lvl + 1, :]
        xr, xi = xr + ar * sr - ai * si, xi + ar * si + ai * sr
    keep = pos >= 1
    pr = jnp.where(keep, pltpu.roll(xr, 1, axis=0), 0.0)
    pi = jnp.where(keep, pltpu.roll(xi, 1, axis=0), 0.0)
    x_prev = jnp.concatenate([to_cols(pr), to_cols(pi)], axis=0).astype(BF16)
    return _dot(wt, u) + _dot(mt, x_prev) + d_col * u.astype(F32)


def _s5_body(u_ref, wt_ref, nt_ref, mt_ref, tab_ref, arow_ref, o_ref, *, cols_per_seq, n_levels):
    t, _, n = u_ref.shape
    cg = S5_GROUP_CH
    for gi in range(S5_GPB):
        ch = slice(gi * cg, (gi + 1) * cg)
        u = u_ref[:, ch, :].reshape(t * cg, n)
        y = _s5_group(u, wt_ref[gi], nt_ref[gi], mt_ref[gi], tab_ref[gi][:, 0:1], arow_ref[gi],
                      cols_per_seq=cols_per_seq, n_levels=n_levels)
        o_ref[:, ch, :] = y.astype(o_ref.dtype).reshape(t, cg, n)


def s5_apply(proj_t, wt_g, nt_g, mt_g, tab, arow, *, groups, cols_per_seq, n_levels):
    t, _, n = proj_t.shape
    cg = S5_GROUP_CH * S5_GPB
    spec3 = lambda shp: pl.BlockSpec((S5_GPB,) + shp, lambda i: (i, 0, 0))
    return pl.pallas_call(
        functools.partial(_s5_body, cols_per_seq=cols_per_seq, n_levels=n_levels),
        grid=(groups // S5_GPB,),
        in_specs=[pl.BlockSpec((t, cg, n), lambda i: (0, i, 0)),
                  spec3(wt_g.shape[1:]), spec3(nt_g.shape[1:]), spec3(mt_g.shape[1:]), spec3(tab.shape[1:]),
                  spec3(arow.shape[1:])],
        out_specs=pl.BlockSpec((t, cg, n), lambda i: (0, i, 0)),
        out_shape=jax.ShapeDtypeStruct((t, groups * S5_GROUP_CH, n), BF16),
        compiler_params=_cparams(("parallel",)),
        name="s5",
    )(proj_t, wt_g, nt_g, mt_g, tab, arow)


def s5_operators(lam_re, lam_im, log_step, b_re, b_im, c_re, c_im, d, *, n_levels, n_seq):
    hp = lax.Precision.HIGHEST
    g, p = lam_re.shape
    cg, t = S5_GROUP_CH, S5_T
    lr = jnp.minimum(lam_re.astype(F32), -1e-4)
    li = lam_im.astype(F32)
    step = jnp.exp(log_step.astype(F32))[:, None]
    mag = jnp.exp(lr * step)
    lb_re, lb_im = mag * jnp.cos(li * step), mag * jnp.sin(li * step)
    den = lr * lr + li * li
    nr, ni = lb_re - 1.0, lb_im
    coef_re = (nr * lr + ni * li) / den
    coef_im = (ni * lr - nr * li) / den
    bb_re = coef_re[..., None] * b_re - coef_im[..., None] * b_im
    bb_im = coef_re[..., None] * b_im + coef_im[..., None] * b_re

    def power(m):
        mm = m.astype(F32)[None, :, None]
        mg = jnp.exp(lr[:, None, :] * step[:, None, :] * mm)
        ang = li[:, None, :] * step[:, None, :] * mm
        return mg * jnp.cos(ang), mg * jnp.sin(ang)

    pw_re, pw_im = power(jnp.arange(t + 1))
    cp_re = c_re[:, None] * pw_re[:, :, None, :] - c_im[:, None] * pw_im[:, :, None, :]
    cp_im = c_re[:, None] * pw_im[:, :, None, :] + c_im[:, None] * pw_re[:, :, None, :]
    taps = (jnp.einsum('gxcp,gpd->gxcd', cp_re[:, :t], bb_re, precision=hp)
            - jnp.einsum('gxcp,gpd->gxcd', cp_im[:, :t], bb_im, precision=hp))
    shift = (jnp.arange(t)[None, None, :] - jnp.arange(t)[None, :, None]
             == jnp.arange(t)[:, None, None]).astype(F32)
    wt_g = jnp.einsum('xst,gxcd->gtcsd', shift, taps, precision=hp).reshape(g, t * cg, t * cg)
    pr, pi = pw_re[:, t - 1::-1], pw_im[:, t - 1::-1]
    bt_re, bt_im = bb_re[:, :, None, :], bb_im[:, :, None, :]
    prt, pit = pr.transpose(0, 2, 1)[:, :, :, None], pi.transpose(0, 2, 1)[:, :, :, None]
    n_re = (prt * bt_re - pit * bt_im).reshape(g, p, t * cg)
    n_im = (prt * bt_im + pit * bt_re).reshape(g, p, t * cg)
    nt_g = jnp.concatenate([n_re, n_im], axis=1)
    m_re = cp_re[:, 1:].reshape(g, t * cg, p)
    m_im = cp_im[:, 1:].reshape(g, t * cg, p)
    mt_g = jnp.concatenate([m_re, -m_im], axis=2)
    ar, ai = power(t * (2 ** jnp.arange(n_levels)))
    arow = jnp.concatenate([jnp.tile(ar, (1, 1, n_seq)), jnp.tile(ai, (1, 1, n_seq))], axis=1)
    d_col = jnp.tile(d.astype(F32).reshape(g, cg), (1, t))[:, :, None]
    tab = d_col
    tab = jnp.pad(tab, ((0, 0), (0, 0), (0, S5_TAB - tab.shape[2])))
    return wt_g.astype(BF16), nt_g.astype(BF16), mt_g.astype(BF16), tab, arow


def _gelu_tanh(y):
    return 0.5 * y * (1.0 + jnp.tanh(math.sqrt(2.0 / math.pi) * (y + 0.044715 * (y * y * y))))


def _glu_body(y_ref, w_ref, b_ref, gate_ref, o_ref, z_ref, zb_ref, wt_ref):
    @pl.when((pl.program_id(0) == 0) & (pl.program_id(1) == 0))
    def _():
        _transpose_into(wt_ref, w_ref)

    z = _gelu_tanh(y_ref[...].astype(F32))
    z_ref[...] = z
    zb_ref[...] = z.astype(BF16)
    for f0 in range(0, o_ref.shape[1], MM_SUB):
        rows = slice(f0, f0 + MM_SUB)
        lin = _dot(wt_ref[rows, :], zb_ref[...]) + b_ref[rows, :]
        out = z_ref[rows, :] * _sigmoid(lin) * _silu(gate_ref[rows, :].astype(F32))
        o_ref[:, rows] = out.T.astype(o_ref.dtype)


def glu_gate(y_t, w, b, proj_t, *, gate_row, tm=512):
    s, e, n = y_t.shape
    gb = gate_row // e
    return pl.pallas_call(
        _glu_body,
        grid=(s, n // tm),
        in_specs=[pl.BlockSpec((None, e, tm), lambda t, i: (t, 0, i)),
                  pl.BlockSpec((e, e), lambda t, i: (0, 0)),
                  pl.BlockSpec((e, 1), lambda t, i: (0, 0)),
                  pl.BlockSpec((None, e, tm), lambda t, i: (t, gb, i))],
        out_specs=pl.BlockSpec((tm, e), lambda t, i: (i, t)),
        out_shape=jax.ShapeDtypeStruct((n, s * e), BF16),
        scratch_shapes=[pltpu.VMEM((e, tm), F32), pltpu.VMEM((e, tm), BF16), pltpu.VMEM((e, e), BF16)],
        compiler_params=_cparams(("arbitrary", "arbitrary")),
        name="glu",
    )(y_t, w, b.reshape(e, 1), proj_t)


def kernel(x, ab_norm_g, ab_w_in, gla_alpha_up, gla_alpha_b, gla_head_g, fox_f_b, ab_w_out, c_norm_g, c_w_in, s5_lambda_re, s5_lambda_im, s5_log_step, s5_b_re, s5_b_im, s5_c_re, s5_c_im, s5_d, glu_w, glu_b, c_w_out, final_norm_g):
    batch, seq, d = x.shape
    t = batch * seq
    x2 = x.reshape(t, d)

    hk, hv, fw = GLA_HEADS * GLA_DK, GLA_HEADS * GLA_DV, FOX_HEADS * FOX_DH
    sizes = (hk, hk, hv, GLA_LOWRANK, hv, fw, fw, fw, FOX_HEADS, fw)
    offs = [0]
    for s in sizes:
        offs.append(offs[-1] + s)
    w_main, w_small = ab_weight_prep(jnp.swapaxes(ab_w_in, 1, 2), offs)
    c_gq, c_gk, c_gv, c_gg = 0, hk, 2 * hk, 2 * hk + hv
    c_fq = c_gg + hv
    c_fk, c_fv, c_fg = c_fq + fw, c_fq + 2 * fw, c_fq + 3 * fw

    h0, small = rmsnorm_bf16(x2, ab_norm_g[0], w_small)
    proj = matmul(h0, w_main, BF16, tn=w_main.shape[0] // 2, name="ab_in")

    o_gla = gla(proj, small, gla_alpha_up[0], gla_alpha_b[0], gla_head_g[0],
                batch=batch, seq=seq, q_col=c_gq, k_col=c_gk, v_col=c_gv, gate_col=c_gg)

    fb_row = jnp.zeros((1, 128), F32).at[0, GLA_LOWRANK:GLA_LOWRANK + FOX_HEADS].set(fox_f_b[0])
    qa, ka = fox_gate(small, fb_row, batch=batch, seq=seq, col0=GLA_LOWRANK)
    o_fox = fox(proj, qa, ka, batch=batch, seq=seq, q_col=c_fq, k_col=c_fk, v_col=c_fv, gate_col=c_fg)

    x1, h1g = outproj_mid(o_gla, o_fox, ab_w_out[0].astype(BF16), x2, c_norm_g[0], steps=S5_T, name="ab_out")

    e = d
    groups = e // S5_GROUP_CH
    n_chunks = t // S5_T
    cols_per_seq = seq // S5_T
    n_levels = max(1, (cols_per_seq - 1).bit_length())
    proj_t = matmul_t_grouped(c_w_in[0].astype(BF16), h1g, S5_T, BF16, name="c_in")
    ops = s5_operators(s5_lambda_re[0], s5_lambda_im[0], s5_log_step[0], s5_b_re[0], s5_b_im[0],
                       s5_c_re[0], s5_c_im[0], s5_d[0], n_levels=n_levels, n_seq=batch)
    y_t = s5_apply(proj_t, *ops, groups=groups, cols_per_seq=cols_per_seq, n_levels=n_levels)
    zz = glu_gate(y_t, glu_w[0].astype(BF16), glu_b[0], proj_t, gate_row=e)
    out = outproj_final(zz, c_w_out[0].astype(BF16), x1, final_norm_g, steps=S5_T, name="c_out")
    return out.reshape(batch, seq, d)
```

```python
import functools
import math

import jax
import jax.numpy as jnp
from jax import lax
from jax.experimental import pallas as pl
from jax.experimental.pallas import tpu as pltpu

EPS = 1e-6
F32 = jnp.float32
BF16 = jnp.bfloat16

GLA_HEADS = 4
GLA_DK = 128
GLA_DV = 256
GLA_LOWRANK = 16
GLA_TAU = 16.0
GLA_CHUNK = 64
GLA_SUB = 8
GLA_UNROLL = 2
FOX_HEADS = 8
FOX_DH = 128
FOX_AUG = 128
S5_GROUP_CH = 16
S5_STATE = 64
S5_T = 16

VMEM_LIMIT = 56 * 1024 * 1024

NT_DIMS = (((1,), (1,)), ((), ()))
TN_DIMS = (((0,), (0,)), ((), ()))
LOG2E = math.log2(math.e)


def _cparams(sem):
    return pltpu.CompilerParams(dimension_semantics=sem, vmem_limit_bytes=VMEM_LIMIT)


def _dot(a, b):
    return jnp.dot(a, b, preferred_element_type=F32)


def _dot_nt(a, b):
    return lax.dot_general(a, b, NT_DIMS, preferred_element_type=F32)


def _dot_tn(a, b):
    return lax.dot_general(a, b, TN_DIMS, preferred_element_type=F32)


def _log_sigmoid(z):
    return -(jnp.maximum(-z, 0.0) + jnp.log1p(jnp.exp(-jnp.abs(z))))


def _sigmoid(z):
    return 1.0 / (1.0 + jnp.exp(-z))


def _silu(z):
    return z * _sigmoid(z)


def _split3(x):
    hi = x.astype(BF16)
    r1 = x - hi.astype(F32)
    mid = r1.astype(BF16)
    lo = (r1 - mid.astype(F32)).astype(BF16)
    return hi, mid, lo


def _rmsnorm_body(x_ref, g_ref, ws_ref, o_ref, s_ref):
    x = x_ref[...]
    ms = jnp.mean(x * x, axis=-1, keepdims=True)
    h = (x * lax.rsqrt(ms + EPS) * g_ref[...]).astype(o_ref.dtype)
    o_ref[...] = h
    s_ref[...] = _dot_nt(h, ws_ref[...])


def rmsnorm_bf16(x, g, wt_small, tm=512):
    m, d = x.shape
    ns = wt_small.shape[0]
    return pl.pallas_call(
        _rmsnorm_body,
        grid=(m // tm,),
        in_specs=[pl.BlockSpec((tm, d), lambda i: (i, 0)),
                  pl.BlockSpec((1, d), lambda i: (0, 0)),
                  pl.BlockSpec((ns, d), lambda i: (0, 0))],
        out_specs=[pl.BlockSpec((tm, d), lambda i: (i, 0)),
                   pl.BlockSpec((tm, ns), lambda i: (i, 0))],
        out_shape=[jax.ShapeDtypeStruct((m, d), BF16), jax.ShapeDtypeStruct((m, ns), F32)],
        compiler_params=_cparams(("parallel",)),
        name="rmsnorm",
    )(x, g.reshape(1, d), wt_small)


def _ab_wprep_body(w_ref, o_ref, s_ref, *, offs):
    w = w_ref[0]
    seg = lambda n: w[offs[n]:offs[n + 1], :]
    o_ref[...] = jnp.concatenate(
        [w[:offs[3], :], seg(4), seg(5) * (FOX_DH ** -0.5 * LOG2E), w[offs[6]:offs[8], :], seg(9)],
        axis=0).astype(o_ref.dtype)
    pad = s_ref.shape[0] - (offs[4] - offs[3]) - (offs[9] - offs[8])
    s_ref[...] = jnp.concatenate([seg(3), seg(8), jnp.zeros((pad, w.shape[1]), F32)], axis=0).astype(s_ref.dtype)


def ab_weight_prep(w_t3, offs, tk=256):
    _, n_in, d = w_t3.shape
    n_main = n_in - (offs[4] - offs[3]) - (offs[9] - offs[8])
    return pl.pallas_call(
        functools.partial(_ab_wprep_body, offs=tuple(offs)),
        grid=(d // tk,),
        in_specs=[pl.BlockSpec((1, n_in, tk), lambda i: (0, 0, i))],
        out_specs=[pl.BlockSpec((n_main, tk), lambda i: (0, i)), pl.BlockSpec((128, tk), lambda i: (0, i))],
        out_shape=[jax.ShapeDtypeStruct((n_main, d), BF16), jax.ShapeDtypeStruct((128, d), BF16)],
        compiler_params=_cparams(("parallel",)),
        name="ab_wprep",
    )(w_t3)


MM_SUB = 512


def _mm_body(a_ref, wt_ref, o_ref):
    a = a_ref[...]
    for c0 in range(0, o_ref.shape[1], MM_SUB):
        o_ref[:, c0:c0 + MM_SUB] = _dot_nt(a, wt_ref[c0:c0 + MM_SUB, :]).astype(o_ref.dtype)


def matmul(a, wt, out_dtype, tm=512, tn=512, name="matmul"):
    m, k = a.shape
    n = wt.shape[0]
    return pl.pallas_call(
        _mm_body,
        grid=(n // tn, m // tm),
        in_specs=[pl.BlockSpec((tm, k), lambda j, i: (i, 0)),
                  pl.BlockSpec((tn, k), lambda j, i: (j, 0))],
        out_specs=pl.BlockSpec((tm, tn), lambda j, i: (i, j)),
        out_shape=jax.ShapeDtypeStruct((m, n), out_dtype),
        compiler_params=_cparams(("parallel", "parallel")),
        name=name,
    )(a, wt)


def _transpose_into(wt_ref, w_ref):
    for c0 in range(0, w_ref.shape[1], MM_SUB):
        wt_ref[c0:c0 + MM_SUB, :] = w_ref[:, c0:c0 + MM_SUB].T


def _mm_t_body(w_ref, a_ref, o_ref, wt_ref):
    @pl.when((pl.program_id(1) == 0) & (pl.program_id(2) == 0))
    def _():
        _transpose_into(wt_ref, w_ref)

    a = a_ref[...]
    for r0 in range(0, o_ref.shape[0], MM_SUB):
        o_ref[r0:r0 + MM_SUB, :] = _dot_nt(wt_ref[r0:r0 + MM_SUB, :], a).astype(o_ref.dtype)


def matmul_t_grouped(w, a2, steps, out_dtype, tm=512, tn=2048, name="matmul_tg"):
    k, nf = w.shape
    n = a2.shape[0]
    return pl.pallas_call(
        _mm_t_body,
        grid=(nf // tn, steps, n // tm),
        in_specs=[pl.BlockSpec((k, tn), lambda j, t, i: (0, j)),
                  pl.BlockSpec((tm, k), lambda j, t, i: (i, t))],
        out_specs=pl.BlockSpec((None, tn, tm), lambda j, t, i: (t, j, i)),
        out_shape=jax.ShapeDtypeStruct((steps, nf, n), out_dtype),
        scratch_shapes=[pltpu.VMEM((tn, k), BF16)],
        compiler_params=_cparams(("arbitrary", "arbitrary", "arbitrary")),
        name=name,
    )(w, a2)


def _gla_body(q_ref, k_ref, v_ref, glr_ref, aup_ref, ab_ref, gate_ref, hg_ref, o_ref,
              s_ref, b_ref, kf_ref, kts_ref, dec_ref, *, tb, hpb):
    ib = pl.program_id(2)
    C, SB, dk, dv = GLA_CHUNK, GLA_SUB, GLA_DK, GLA_DV

    @pl.when(ib == 0)
    def _():
        s_ref[...] = jnp.zeros_like(s_ref)

    glr = glr_ref[:, :GLA_LOWRANK].astype(BF16)
    tw = 2 * C
    r = lax.broadcasted_iota(jnp.int32, (tw, tw), 0)
    c = lax.broadcasted_iota(jnp.int32, (tw, tw), 1)
    tri = jnp.where((r // C == c // C) & (c <= r), 1.0, 0.0).astype(BF16)
    nc = tb // C
    for hh in range(hpb):
        kc = slice(hh * dk, (hh + 1) * dk)
        z = _dot(glr, aup_ref[:, kc].astype(BF16)) + ab_ref[:, kc]
        la = _log_sigmoid(z) * (1.0 / GLA_TAU)
        hi = la.astype(BF16)
        lo = (la - hi.astype(F32)).astype(BF16)
        b_blk = jnp.concatenate(
            [_dot(tri, hi[r0:r0 + tw]) + _dot(tri, lo[r0:r0 + tw]) for r0 in range(0, tb, tw)], axis=0)
        k_blk = k_ref[:, kc].astype(F32)
        b_ref[hh] = b_blk
        kf_ref[hh] = k_blk
        b3 = b_blk.reshape(nc, C, dk)
        b_end = b3[:, C - 1:C, :]
        k_end = (k_blk.reshape(nc, C, dk) * jnp.exp(b_end - b3)).reshape(tb, dk)
        kt_end = k_end.T.astype(BF16)
        dec_t = jnp.exp(b_end.reshape(nc, dk)).T
        for ci in range(nc):
            kts_ref[hh, ci] = kt_end[:, ci * C:(ci + 1) * C]
            dec_ref[hh, ci] = jnp.broadcast_to(dec_t[:, ci:ci + 1], (dk, dv))

    row = lax.broadcasted_iota(jnp.int32, (C, 1), 0)
    rowi = lax.broadcasted_iota(jnp.int32, (C, C), 0)
    coli = lax.broadcasted_iota(jnp.int32, (C, C), 1)
    lane_c = lax.broadcasted_iota(jnp.int32, (SB, C), 1)
    sub_r = lax.broadcasted_iota(jnp.int32, (SB, 1), 0)
    ones = jnp.ones((GLA_DK, C), BF16)
    neg = -jnp.inf
    scale = GLA_DK ** -0.5

    def chunk_head(ci, r0, hh):
        kc = slice(hh * dk, (hh + 1) * dk)
        vc = slice(hh * dv, (hh + 1) * dv)
        q = q_ref[pl.ds(r0, C), kc].astype(F32) * scale
        k = kf_ref[hh, pl.ds(r0, C), :]
        b = b_ref[hh, pl.ds(r0, C), :]
        v = v_ref[pl.ds(r0, C), vc]

        s = s_ref[hh]
        o = _dot((q * jnp.exp(b)).astype(BF16), s.astype(BF16))
        s_ref[hh] = s * dec_ref[hh, ci] + _dot(kts_ref[hh, ci], v)

        attn = jnp.zeros((C, C), F32)
        h = C // 2
        while h >= SB:
            ref = jnp.broadcast_to(b.reshape(C // (2 * h), 2 * h, GLA_DK)[:, h - 1:h, :],
                                   (C // (2 * h), 2 * h, GLA_DK)).reshape(C, GLA_DK)
            upper = (row // h) % 2 == 1
            q_h = q * jnp.exp(jnp.where(upper, b - ref, neg))
            k_h = k * jnp.exp(jnp.where(upper, neg, ref - b))
            a_h = _dot_nt(q_h.astype(BF16), k_h.astype(BF16))
            attn = attn + (a_h if 2 * h == C else jnp.where(rowi // (2 * h) == coli // (2 * h), a_h, 0.0))
            h //= 2

        zs = []
        for bi in range(C // SB):
            s0 = bi * SB
            q_i = q[s0:s0 + SB, :]
            b_i = b[s0:s0 + SB, :]
            for j in range(SB):
                k_j = kf_ref[hh, pl.ds(r0 + (s0 + j), 1), :]
                b_j = b_ref[hh, pl.ds(r0 + (s0 + j), 1), :]
                zs.append(q_i * k_j * jnp.exp(jnp.where(sub_r >= j, b_i - b_j, neg)))
        zsum = _dot(jnp.concatenate(zs, axis=0).astype(BF16), ones)
        diag = []
        for bi in range(C // SB):
            acc = jnp.zeros((SB, C), F32)
            for j in range(SB):
                n0 = (bi * SB + j) * SB
                acc = acc + jnp.where(lane_c == bi * SB + j, zsum[n0:n0 + SB, :], 0.0)
            diag.append(acc)
        attn = attn + jnp.concatenate(diag, axis=0)
        o = o + _dot(attn.astype(BF16), v)

        o = o * lax.rsqrt(jnp.mean(o * o, axis=-1, keepdims=True) + EPS) * hg_ref[:, vc]
        g = gate_ref[pl.ds(r0, C), vc].astype(F32)
        o_ref[pl.ds(r0, C), vc] = (o * _silu(g)).astype(o_ref.dtype)

    def chunk(ci, carry):
        r0 = pl.multiple_of(ci * C, C)
        for hh in range(hpb):
            chunk_head(ci, r0, hh)
        return carry

    lax.fori_loop(0, tb // C, chunk, 0, unroll=GLA_UNROLL)


def gla(proj, small, alpha_up, alpha_b, head_g, *, batch, seq, q_col, k_col, v_col, gate_col, tb=512, hpb=4):
    t = batch * seq
    nb = seq // tb
    dk, dv, h = GLA_DK * hpb, GLA_DV * hpb, GLA_HEADS // hpb
    qb, kb, vb, gb = q_col // dk, k_col // dk, v_col // dv, gate_col // dv
    tok = lambda b, hh, i: b * nb + i
    nc = tb // GLA_CHUNK
    return pl.pallas_call(
        functools.partial(_gla_body, tb=tb, hpb=hpb),
        grid=(batch, h, nb),
        in_specs=[
            pl.BlockSpec((tb, dk), lambda b, hh, i: (tok(b, hh, i), qb + hh)),
            pl.BlockSpec((tb, dk), lambda b, hh, i: (tok(b, hh, i), kb + hh)),
            pl.BlockSpec((tb, dv), lambda b, hh, i: (tok(b, hh, i), vb + hh)),
            pl.BlockSpec((tb, 128), lambda b, hh, i: (tok(b, hh, i), 0)),
            pl.BlockSpec((GLA_LOWRANK, dk), lambda b, hh, i: (0, hh)),
            pl.BlockSpec((1, dk), lambda b, hh, i: (0, hh)),
            pl.BlockSpec((tb, dv), lambda b, hh, i: (tok(b, hh, i), gb + hh)),
            pl.BlockSpec((1, dv), lambda b, hh, i: (0, hh)),
        ],
        out_specs=pl.BlockSpec((tb, dv), lambda b, hh, i: (tok(b, hh, i), hh)),
        out_shape=jax.ShapeDtypeStruct((t, GLA_HEADS * GLA_DV), BF16),
        scratch_shapes=[pltpu.VMEM((hpb, GLA_DK, GLA_DV), F32), pltpu.VMEM((hpb, tb, GLA_DK), F32),
                        pltpu.VMEM((hpb, tb, GLA_DK), F32),
                        pltpu.VMEM((hpb, nc, GLA_DK, GLA_CHUNK), BF16),
                        pltpu.VMEM((hpb, nc, GLA_DK, GLA_DV), F32)],
        compiler_params=_cparams(("parallel", "parallel", "arbitrary")),
        name="gla",
    )(proj, proj, proj, small, alpha_up, alpha_b.reshape(1, -1), proj, head_g.reshape(1, -1))


def _fox_gate_body(s_ref, fb_ref, qa_ref, ka_ref, *, blk, col0):
    n = s_ref.shape[0] // blk
    r = lax.broadcasted_iota(jnp.int32, (blk, blk), 0)
    c = lax.broadcasted_iota(jnp.int32, (blk, blk), 1)
    tri = jnp.where(c <= r, 1.0, 0.0).astype(BF16)
    lane = lax.broadcasted_iota(jnp.int32, (blk, FOX_AUG), 1)

    def step(i, carry):
        r0 = pl.multiple_of(i * blk, blk)
        lf = _log_sigmoid(s_ref[pl.ds(r0, blk), :] + fb_ref[...])
        hi, mid, lo = _split3(lf)
        cs = _dot(tri, hi) + _dot(tri, mid) + _dot(tri, lo) + carry
        for h in range(FOX_HEADS):
            col = cs[:, col0 + h:col0 + h + 1] * LOG2E
            c0, c1, c2 = (p.astype(F32) for p in _split3(col))
            qa = jnp.where(lane == 0, c0, jnp.where(lane == 1, c1, jnp.where(lane == 2, c2,
                           jnp.where(lane < 6, 1.0, 0.0))))
            ka = jnp.where(lane < 3, 1.0, jnp.where(lane == 3, -c0, jnp.where(lane == 4, -c1,
                           jnp.where(lane == 5, -c2, 0.0))))
            qa_ref[h, pl.ds(r0, blk), :] = qa.astype(BF16)
            ka_ref[h, pl.ds(r0, blk), :] = ka.astype(BF16)
        return cs[blk - 1:blk, :]

    lax.fori_loop(0, n, step, jnp.zeros((1, s_ref.shape[1]), F32))


def fox_gate(small, fb_row, *, batch, seq, col0, blk=256):
    shp = jax.ShapeDtypeStruct((FOX_HEADS, batch * seq, FOX_AUG), BF16)
    spec = pl.BlockSpec((FOX_HEADS, seq, FOX_AUG), lambda b: (0, b, 0))
    return pl.pallas_call(
        functools.partial(_fox_gate_body, blk=blk, col0=col0),
        grid=(batch,),
        in_specs=[pl.BlockSpec((seq, 128), lambda b: (b, 0)),
                  pl.BlockSpec((1, 128), lambda b: (0, 0))],
        out_specs=[spec, spec],
        out_shape=[shp, shp],
        compiler_params=_cparams(("parallel",)),
        name="fox_gate",
    )(small, fb_row)


def _fox_body(it_ref, jt_ref, q_ref, qa_ref, k_ref, ka_ref, v_ref, gate_ref, o_ref,
              m_ref, l_ref, acc_ref, *, tq, tk, hpb):
    p = pl.program_id(2)
    i = it_ref[p]
    j = jt_ref[p]
    d = FOX_DH

    @pl.when(j == 0)
    def _():
        m_ref[...] = jnp.full_like(m_ref, -jnp.inf)
        l_ref[...] = jnp.zeros_like(l_ref)
        acc_ref[...] = jnp.zeros_like(acc_ref)

    def update(hh, masked):
        q_aug = jnp.concatenate([q_ref[:, hh * d:(hh + 1) * d], qa_ref[hh]], axis=1)
        k_aug = jnp.concatenate([k_ref[:, hh * d:(hh + 1) * d], ka_ref[hh]], axis=1)
        st = _dot_nt(k_aug, q_aug)
        if masked:
            kr = lax.broadcasted_iota(jnp.int32, (tk, tq), 0)
            qc = lax.broadcasted_iota(jnp.int32, (tk, tq), 1)
            st = jnp.where(qc >= kr, st, -jnp.inf)
        m_old = m_ref[hh]
        m_new = jnp.maximum(m_old, jnp.max(st, axis=0, keepdims=True))
        alpha = jnp.exp2(m_old - m_new)
        pt = jnp.exp2(st - m_new)
        l_ref[hh] = alpha * l_ref[hh] + jnp.sum(pt, axis=0, keepdims=True)
        acc_ref[hh] = alpha * acc_ref[hh] + _dot_tn(v_ref[:, hh * d:(hh + 1) * d], pt.astype(BF16))
        m_ref[hh] = m_new

    @pl.when(j < i)
    def _():
        for hh in range(hpb):
            update(hh, False)

    @pl.when(j == i)
    def _():
        for hh in range(hpb):
            update(hh, True)
            o = (acc_ref[hh] / l_ref[hh]).T
            g = gate_ref[:, hh * d:(hh + 1) * d].astype(F32)
            o_ref[:, hh * d:(hh + 1) * d] = (o * _silu(g)).astype(o_ref.dtype)


def fox(proj, qa, ka, *, batch, seq, q_col, k_col, v_col, gate_col, t=1024, hpb=4):
    tt = batch * seq
    nb = seq // t
    d, h = FOX_DH, FOX_HEADS
    w = hpb * d
    qb, kb, gb, vb = q_col // w, k_col // w, gate_col // w, v_col // w
    pairs = [(i, j) for i in range(nb) for j in range(i + 1)]
    it = jnp.array([p[0] for p in pairs], jnp.int32)
    jt = jnp.array([p[1] for p in pairs], jnp.int32)
    grid_spec = pltpu.PrefetchScalarGridSpec(
        num_scalar_prefetch=2,
        grid=(batch, h // hpb, len(pairs)),
        in_specs=[
            pl.BlockSpec((t, w), lambda b, hh, p, it, jt: (b * nb + it[p], qb + hh)),
            pl.BlockSpec((hpb, t, FOX_AUG), lambda b, hh, p, it, jt: (hh, b * nb + it[p], 0)),
            pl.BlockSpec((t, w), lambda b, hh, p, it, jt: (b * nb + jt[p], kb + hh)),
            pl.BlockSpec((hpb, t, FOX_AUG), lambda b, hh, p, it, jt: (hh, b * nb + jt[p], 0)),
            pl.BlockSpec((t, w), lambda b, hh, p, it, jt: (b * nb + jt[p], vb + hh)),
            pl.BlockSpec((t, w), lambda b, hh, p, it, jt: (b * nb + it[p], gb + hh)),
        ],
        out_specs=pl.BlockSpec((t, w), lambda b, hh, p, it, jt: (b * nb + it[p], hh)),
        scratch_shapes=[pltpu.VMEM((hpb, 1, t), F32), pltpu.VMEM((hpb, 1, t), F32), pltpu.VMEM((hpb, d, t), F32)],
    )
    return pl.pallas_call(
        functools.partial(_fox_body, tq=t, tk=t, hpb=hpb),
        grid_spec=grid_spec,
        out_shape=jax.ShapeDtypeStruct((tt, h * d), BF16),
        compiler_params=_cparams(("parallel", "parallel", "arbitrary")),
        name="fox",
    )(it, jt, proj, qa, proj, ka, proj, proj)


def _regroup_perm(tm, steps, to_grouped):
    r = lax.broadcasted_iota(jnp.int32, (tm, tm), 0)
    c = lax.broadcasted_iota(jnp.int32, (tm, tm), 1)
    nc = tm // steps
    src = (r % nc) * steps + r // nc if to_grouped else (r % steps) * nc + r // steps
    return jnp.where(c == src, 1.0, 0.0).astype(BF16)


def _outproj_mid_body(a1_ref, a2_ref, w_ref, x_ref, g_ref, r_ref, h_ref, *, steps):
    k1 = a1_ref.shape[1]
    tm, d = x_ref.shape
    nc = tm // steps
    acc = x_ref[...] + _dot(a1_ref[...], w_ref[:k1, :]) + _dot(a2_ref[...], w_ref[k1:, :])
    r_ref[...] = acc
    normed = (acc * lax.rsqrt(jnp.mean(acc * acc, axis=-1, keepdims=True) + EPS) * g_ref[...]).astype(BF16)
    by_step = _dot(_regroup_perm(tm, steps, True), normed).astype(h_ref.dtype)
    for s in range(steps):
        h_ref[:, s * d:(s + 1) * d] = by_step[s * nc:(s + 1) * nc, :]


def outproj_mid(a1, a2, w, x, g, *, steps, tm=256, name="outproj_mid"):
    m, d = x.shape
    row = pl.BlockSpec((tm, d), lambda i: (i, 0))
    return pl.pallas_call(
        functools.partial(_outproj_mid_body, steps=steps),
        grid=(m // tm,),
        in_specs=[pl.BlockSpec((tm, a1.shape[1]), lambda i: (i, 0)),
                  pl.BlockSpec((tm, a2.shape[1]), lambda i: (i, 0)),
                  pl.BlockSpec(w.shape, lambda i: (0, 0)),
                  row,
                  pl.BlockSpec((1, d), lambda i: (0, 0))],
        out_specs=[row, pl.BlockSpec((tm // steps, steps * d), lambda i: (i, 0))],
        out_shape=[jax.ShapeDtypeStruct((m, d), F32), jax.ShapeDtypeStruct((m // steps, steps * d), BF16)],
        compiler_params=_cparams(("parallel",)),
        name=name,
    )(a1, a2, w, x, g.reshape(1, d))


def _outproj_final_body(a_ref, w_ref, x_ref, g_ref, o_ref, *, steps):
    tm = x_ref.shape[0]
    e = w_ref.shape[0]
    by_step = jnp.concatenate([a_ref[:, s * e:(s + 1) * e] for s in range(steps)], axis=0)
    a_tok = _dot(_regroup_perm(tm, steps, False), by_step).astype(BF16)
    acc = x_ref[...] + _dot(a_tok, w_ref[...])
    o_ref[...] = acc * lax.rsqrt(jnp.mean(acc * acc, axis=-1, keepdims=True) + EPS) * g_ref[...]


def outproj_final(a_grouped, w, x, g, *, steps, tm=256, name="outproj_final"):
    m, d = x.shape
    e = w.shape[0]
    row = pl.BlockSpec((tm, d), lambda i: (i, 0))
    return pl.pallas_call(
        functools.partial(_outproj_final_body, steps=steps),
        grid=(m // tm,),
        in_specs=[pl.BlockSpec((tm // steps, steps * e), lambda i: (i, 0)),
                  pl.BlockSpec(w.shape, lambda i: (0, 0)),
                  row,
                  pl.BlockSpec((1, d), lambda i: (0, 0))],
        out_specs=row,
        out_shape=jax.ShapeDtypeStruct((m, d), F32),
        compiler_params=_cparams(("parallel",)),
        name=name,
    )(a_grouped, w, x, g.reshape(1, d))


S5_GPB = 2


def _s5_group(u, wt, nt, mt, arow, *, cols_per_seq, n_levels):
    n = u.shape[1]
    p = S5_STATE
    nseq = n // cols_per_seq
    x = _dot(nt, u)

    def to_rows(part):
        return jnp.concatenate([part[:, s * cols_per_seq:(s + 1) * cols_per_seq] for s in range(nseq)], axis=0).T

    def to_cols(rows):
        rt = rows.T
        return jnp.concatenate([rt[s * p:(s + 1) * p, :] for s in range(nseq)], axis=1)

    xr, xi = to_rows(x[:p, :]), to_rows(x[p:, :])
    pos = lax.broadcasted_iota(jnp.int32, (cols_per_seq, 1), 0)
    for lvl in range(n_levels):
        sh = 1 << lvl
        keep = pos >= sh
        sr = jnp.where(keep, pltpu.roll(xr, sh, axis=0), 0.0)
        si = jnp.where(keep, pltpu.roll(xi, sh, axis=0), 0.0)
        ar, ai = arow[lvl:lvl + 1, :], arow[n_levels + lvl:n_levels + lvl + 1, :]
        xr, xi = xr + ar * sr - ai * si, xi + ar * si + ai * sr
    keep = pos >= 1
    pr = jnp.where(keep, pltpu.roll(xr, 1, axis=0), 0.0)
    pi = jnp.where(keep, pltpu.roll(xi, 1, axis=0), 0.0)
    x_prev = jnp.concatenate([to_cols(pr), to_cols(pi)], axis=0).astype(BF16)
    return _dot(wt, u) + _dot(mt, x_prev)


def _s5_body(u_ref, wt_ref, nt_ref, mt_ref, arow_ref, o_ref, *, cols_per_seq, n_levels):
    t, _, n = u_ref.shape
    cg = S5_GROUP_CH
    for gi in range(S5_GPB):
        ch = slice(gi * cg, (gi + 1) * cg)
        u = u_ref[:, ch, :].reshape(t * cg, n)
        y = _s5_group(u, wt_ref[gi], nt_ref[gi], mt_ref[gi], arow_ref[gi],
                      cols_per_seq=cols_per_seq, n_levels=n_levels)
        o_ref[:, ch, :] = y.astype(o_ref.dtype).reshape(t, cg, n)


def s5_apply(proj_t, wt_g, nt_g, mt_g, arow, *, groups, cols_per_seq, n_levels):
    t, _, n = proj_t.shape
    cg = S5_GROUP_CH * S5_GPB
    spec3 = lambda shp: pl.BlockSpec((S5_GPB,) + shp, lambda i: (i, 0, 0))
    return pl.pallas_call(
        functools.partial(_s5_body, cols_per_seq=cols_per_seq, n_levels=n_levels),
        grid=(groups // S5_GPB,),
        in_specs=[pl.BlockSpec((t, cg, n), lambda i: (0, i, 0)),
                  spec3(wt_g.shape[1:]), spec3(nt_g.shape[1:]), spec3(mt_g.shape[1:]), spec3(arow.shape[1:])],
        out_specs=pl.BlockSpec((t, cg, n), lambda i: (0, i, 0)),
        out_shape=jax.ShapeDtypeStruct((t, groups * S5_GROUP_CH, n), BF16),
        compiler_params=_cparams(("parallel",)),
        name="s5",
    )(proj_t, wt_g, nt_g, mt_g, arow)


def s5_operators(lam_re, lam_im, log_step, b_re, b_im, c_re, c_im, d, *, n_levels, n_seq):
    hp = lax.Precision.HIGHEST
    g, p = lam_re.shape
    cg, t = S5_GROUP_CH, S5_T
    lr = jnp.minimum(lam_re.astype(F32), -1e-4)
    li = lam_im.astype(F32)
    step = jnp.exp(log_step.astype(F32))[:, None]
    mag = jnp.exp(lr * step)
    lb_re, lb_im = mag * jnp.cos(li * step), mag * jnp.sin(li * step)
    den = lr * lr + li * li
    nr, ni = lb_re - 1.0, lb_im
    coef_re = (nr * lr + ni * li) / den
    coef_im = (ni * lr - nr * li) / den
    bb_re = coef_re[..., None] * b_re - coef_im[..., None] * b_im
    bb_im = coef_re[..., None] * b_im + coef_im[..., None] * b_re

    def power(m):
        mm = m.astype(F32)[None, :, None]
        mg = jnp.exp(lr[:, None, :] * step[:, None, :] * mm)
        ang = li[:, None, :] * step[:, None, :] * mm
        return mg * jnp.cos(ang), mg * jnp.sin(ang)

    pw_re, pw_im = power(jnp.arange(t + 1))
    pwt_re, pwt_im = pw_re.transpose(0, 2, 1), pw_im.transpose(0, 2, 1)

    def pow_times_b(pr, pi):
        re = pr[:, :, :, None] * bb_re[:, :, None, :] - pi[:, :, :, None] * bb_im[:, :, None, :]
        im = pr[:, :, :, None] * bb_im[:, :, None, :] + pi[:, :, :, None] * bb_re[:, :, None, :]
        return re.reshape(g, p, t * cg), im.reshape(g, p, t * cg)

    pb_re, pb_im = pow_times_b(pwt_re[:, :, :t], pwt_im[:, :, :t])
    taps = (jnp.einsum('gcp,gpy->gcy', c_re.astype(F32), pb_re, precision=hp)
            - jnp.einsum('gcp,gpy->gcy', c_im.astype(F32), pb_im, precision=hp))
    y_idx = jnp.arange(t * cg)
    taps = taps + d.astype(F32).reshape(g, cg, 1) * (y_idx[None, :] == jnp.arange(cg)[:, None]).astype(F32)[None]
    place = ((y_idx[None, :, None] // cg == jnp.arange(t)[:, None, None] - y_idx[None, None, :] // cg)
             & (y_idx[None, :, None] % cg == y_idx[None, None, :] % cg)).astype(BF16)
    wt_g = jnp.einsum('gcy,tyz->gtcz', taps.astype(BF16), place,
                      preferred_element_type=F32).astype(BF16).reshape(g, t * cg, t * cg)
    n_re, n_im = pow_times_b(pwt_re[:, :, t - 1::-1], pwt_im[:, :, t - 1::-1])
    nt_g = jnp.concatenate([n_re, n_im], axis=1).astype(BF16)
    cp_re = c_re[:, None] * pw_re[:, 1:, None, :] - c_im[:, None] * pw_im[:, 1:, None, :]
    cp_im = c_re[:, None] * pw_im[:, 1:, None, :] + c_im[:, None] * pw_re[:, 1:, None, :]
    mt_g = jnp.concatenate([cp_re, -cp_im], axis=-1).reshape(g, t * cg, 2 * p).astype(BF16)
    ar, ai = power(t * (2 ** jnp.arange(n_levels)))
    arow = jnp.concatenate([jnp.tile(ar, (1, 1, n_seq)), jnp.tile(ai, (1, 1, n_seq))], axis=1)
    return wt_g, nt_g, mt_g, arow


def _gelu_tanh(y):
    return 0.5 * y * (1.0 + jnp.tanh(math.sqrt(2.0 / math.pi) * (y + 0.044715 * (y * y * y))))


def _glu_body(y_ref, w_ref, b_ref, gate_ref, o_ref, z_ref, zb_ref, wt_ref):
    @pl.when((pl.program_id(0) == 0) & (pl.program_id(1) == 0))
    def _():
        _transpose_into(wt_ref, w_ref)

    z = _gelu_tanh(y_ref[...].astype(F32))
    z_ref[...] = z
    zb_ref[...] = z.astype(BF16)
    for f0 in range(0, o_ref.shape[1], MM_SUB):
        rows = slice(f0, f0 + MM_SUB)
        lin = _dot(wt_ref[rows, :], zb_ref[...]) + b_ref[rows, :]
        out = z_ref[rows, :] * _sigmoid(lin) * _silu(gate_ref[rows, :].astype(F32))
        o_ref[:, rows] = out.T.astype(o_ref.dtype)


def glu_gate(y_t, w, b, proj_t, *, gate_row, tm=512):
    s, e, n = y_t.shape
    gb = gate_row // e
    return pl.pallas_call(
        _glu_body,
        grid=(s, n // tm),
        in_specs=[pl.BlockSpec((None, e, tm), lambda t, i: (t, 0, i)),
                  pl.BlockSpec((e, e), lambda t, i: (0, 0)),
                  pl.BlockSpec((e, 1), lambda t, i: (0, 0)),
                  pl.BlockSpec((None, e, tm), lambda t, i: (t, gb, i))],
        out_specs=pl.BlockSpec((tm, e), lambda t, i: (i, t)),
        out_shape=jax.ShapeDtypeStruct((n, s * e), BF16),
        scratch_shapes=[pltpu.VMEM((e, tm), F32), pltpu.VMEM((e, tm), BF16), pltpu.VMEM((e, e), BF16)],
        compiler_params=_cparams(("arbitrary", "arbitrary")),
        name="glu",
    )(y_t, w, b.reshape(e, 1), proj_t)


def kernel(x, ab_norm_g, ab_w_in, gla_alpha_up, gla_alpha_b, gla_head_g, fox_f_b, ab_w_out, c_norm_g, c_w_in, s5_lambda_re, s5_lambda_im, s5_log_step, s5_b_re, s5_b_im, s5_c_re, s5_c_im, s5_d, glu_w, glu_b, c_w_out, final_norm_g):
    batch, seq, d = x.shape
    t = batch * seq
    x2 = x.reshape(t, d)

    hk, hv, fw = GLA_HEADS * GLA_DK, GLA_HEADS * GLA_DV, FOX_HEADS * FOX_DH
    sizes = (hk, hk, hv, GLA_LOWRANK, hv, fw, fw, fw, FOX_HEADS, fw)
    offs = [0]
    for s in sizes:
        offs.append(offs[-1] + s)
    w_main, w_small = ab_weight_prep(jnp.swapaxes(ab_w_in, 1, 2), offs)
    c_gq, c_gk, c_gv, c_gg = 0, hk, 2 * hk, 2 * hk + hv
    c_fq = c_gg + hv
    c_fk, c_fv, c_fg = c_fq + fw, c_fq + 2 * fw, c_fq + 3 * fw

    h0, small = rmsnorm_bf16(x2, ab_norm_g[0], w_small)
    proj = matmul(h0, w_main, BF16, tn=w_main.shape[0] // 2, name="ab_in")

    o_gla = gla(proj, small, gla_alpha_up[0], gla_alpha_b[0], gla_head_g[0],
                batch=batch, seq=seq, q_col=c_gq, k_col=c_gk, v_col=c_gv, gate_col=c_gg)

    fb_row = jnp.zeros((1, 128), F32).at[0, GLA_LOWRANK:GLA_LOWRANK + FOX_HEADS].set(fox_f_b[0])
    qa, ka = fox_gate(small, fb_row, batch=batch, seq=seq, col0=GLA_LOWRANK)
    o_fox = fox(proj, qa, ka, batch=batch, seq=seq, q_col=c_fq, k_col=c_fk, v_col=c_fv, gate_col=c_fg)

    x1, h1g = outproj_mid(o_gla, o_fox, ab_w_out[0].astype(BF16), x2, c_norm_g[0], steps=S5_T, name="ab_out")

    e = d
    groups = e // S5_GROUP_CH
    n_chunks = t // S5_T
    cols_per_seq = seq // S5_T
    n_levels = max(1, (cols_per_seq - 1).bit_length())
    proj_t = matmul_t_grouped(c_w_in[0].astype(BF16), h1g, S5_T, BF16, name="c_in")
    ops = s5_operators(s5_lambda_re[0], s5_lambda_im[0], s5_log_step[0], s5_b_re[0], s5_b_im[0],
                       s5_c_re[0], s5_c_im[0], s5_d[0], n_levels=n_levels, n_seq=batch)
    y_t = s5_apply(proj_t, *ops, groups=groups, cols_per_seq=cols_per_seq, n_levels=n_levels)
    zz = glu_gate(y_t, glu_w[0].astype(BF16), glu_b[0], proj_t, gate_row=e)
    out = outproj_final(zz, c_w_out[0].astype(BF16), x1, final_norm_g, steps=S5_T, name="c_out")
    return out.reshape(batch, seq, d)
```

```python
import functools
import math

import jax
import jax.numpy as jnp
from jax import lax
from jax.experimental import pallas as pl
from jax.experimental.pallas import tpu as pltpu

EPS = 1e-6
F32 = jnp.float32
BF16 = jnp.bfloat16

GLA_HEADS = 4
GLA_DK = 128
GLA_DV = 256
GLA_LOWRANK = 16
GLA_TAU = 16.0
GLA_CHUNK = 64
GLA_SUB = 8
GLA_UNROLL = 2
FOX_HEADS = 8
FOX_DH = 128
FOX_AUG = 128
FOX_AUG_GROUP = 8
S5_GROUP_CH = 16
S5_STATE = 64
S5_T = 16

VMEM_LIMIT = 56 * 1024 * 1024

NT_DIMS = (((1,), (1,)), ((), ()))
TN_DIMS = (((0,), (0,)), ((), ()))
LOG2E = math.log2(math.e)


def _cparams(sem):
    return pltpu.CompilerParams(dimension_semantics=sem, vmem_limit_bytes=VMEM_LIMIT)


def _dot(a, b):
    return jnp.dot(a, b, preferred_element_type=F32)


def _dot_nt(a, b):
    return lax.dot_general(a, b, NT_DIMS, preferred_element_type=F32)


def _dot_tn(a, b):
    return lax.dot_general(a, b, TN_DIMS, preferred_element_type=F32)


def _log_sigmoid(z):
    return -(jnp.maximum(-z, 0.0) + jnp.log1p(jnp.exp(-jnp.abs(z))))


def _sigmoid(z):
    return 1.0 / (1.0 + jnp.exp(-z))


def _silu(z):
    return z * _sigmoid(z)


def _split3(x):
    hi = x.astype(BF16)
    r1 = x - hi.astype(F32)
    mid = r1.astype(BF16)
    lo = (r1 - mid.astype(F32)).astype(BF16)
    return hi, mid, lo


def _rmsnorm_body(x_ref, g_ref, ws_ref, o_ref, s_ref):
    x = x_ref[...]
    ms = jnp.mean(x * x, axis=-1, keepdims=True)
    h = (x * lax.rsqrt(ms + EPS) * g_ref[...]).astype(o_ref.dtype)
    o_ref[...] = h
    s_ref[...] = _dot_nt(h, ws_ref[...])


def rmsnorm_bf16(x, g, wt_small, tm=512):
    m, d = x.shape
    ns = wt_small.shape[0]
    return pl.pallas_call(
        _rmsnorm_body,
        grid=(m // tm,),
        in_specs=[pl.BlockSpec((tm, d), lambda i: (i, 0)),
                  pl.BlockSpec((1, d), lambda i: (0, 0)),
                  pl.BlockSpec((ns, d), lambda i: (0, 0))],
        out_specs=[pl.BlockSpec((tm, d), lambda i: (i, 0)),
                   pl.BlockSpec((tm, ns), lambda i: (i, 0))],
        out_shape=[jax.ShapeDtypeStruct((m, d), BF16), jax.ShapeDtypeStruct((m, ns), F32)],
        compiler_params=_cparams(("parallel",)),
        name="rmsnorm",
    )(x, g.reshape(1, d), wt_small)


def _ab_wprep_body(w_ref, o_ref, s_ref, *, offs):
    w = w_ref[0]
    seg = lambda n: w[offs[n]:offs[n + 1], :]
    o_ref[...] = jnp.concatenate(
        [w[:offs[3], :], seg(4), seg(5) * (FOX_DH ** -0.5 * LOG2E), w[offs[6]:offs[8], :], seg(9)],
        axis=0).astype(o_ref.dtype)
    pad = s_ref.shape[0] - (offs[4] - offs[3]) - (offs[9] - offs[8])
    s_ref[...] = jnp.concatenate([seg(3), seg(8), jnp.zeros((pad, w.shape[1]), F32)], axis=0).astype(s_ref.dtype)


def ab_weight_prep(w_t3, offs, tk=256):
    _, n_in, d = w_t3.shape
    n_main = n_in - (offs[4] - offs[3]) - (offs[9] - offs[8])
    return pl.pallas_call(
        functools.partial(_ab_wprep_body, offs=tuple(offs)),
        grid=(d // tk,),
        in_specs=[pl.BlockSpec((1, n_in, tk), lambda i: (0, 0, i))],
        out_specs=[pl.BlockSpec((n_main, tk), lambda i: (0, i)), pl.BlockSpec((128, tk), lambda i: (0, i))],
        out_shape=[jax.ShapeDtypeStruct((n_main, d), BF16), jax.ShapeDtypeStruct((128, d), BF16)],
        compiler_params=_cparams(("parallel",)),
        name="ab_wprep",
    )(w_t3)


MM_SUB = 512


def _mm_body(a_ref, wt_ref, o_ref):
    a = a_ref[...]
    for c0 in range(0, o_ref.shape[1], MM_SUB):
        o_ref[:, c0:c0 + MM_SUB] = _dot_nt(a, wt_ref[c0:c0 + MM_SUB, :]).astype(o_ref.dtype)


def matmul(a, wt, out_dtype, tm=512, tn=512, name="matmul"):
    m, k = a.shape
    n = wt.shape[0]
    return pl.pallas_call(
        _mm_body,
        grid=(n // tn, m // tm),
        in_specs=[pl.BlockSpec((tm, k), lambda j, i: (i, 0)),
                  pl.BlockSpec((tn, k), lambda j, i: (j, 0))],
        out_specs=pl.BlockSpec((tm, tn), lambda j, i: (i, j)),
        out_shape=jax.ShapeDtypeStruct((m, n), out_dtype),
        compiler_params=_cparams(("parallel", "parallel")),
        name=name,
    )(a, wt)


def _transpose_into(wt_ref, w_ref):
    for c0 in range(0, w_ref.shape[1], MM_SUB):
        wt_ref[c0:c0 + MM_SUB, :] = w_ref[:, c0:c0 + MM_SUB].T


def _mm_t_body(w_ref, a_ref, o_ref, wt_ref):
    @pl.when((pl.program_id(1) == 0) & (pl.program_id(2) == 0))
    def _():
        _transpose_into(wt_ref, w_ref)

    a = a_ref[...]
    for r0 in range(0, o_ref.shape[0], MM_SUB):
        o_ref[r0:r0 + MM_SUB, :] = _dot_nt(wt_ref[r0:r0 + MM_SUB, :], a).astype(o_ref.dtype)


def matmul_t_grouped(w, a2, steps, out_dtype, tm=512, tn=2048, name="matmul_tg"):
    k, nf = w.shape
    n = a2.shape[0]
    return pl.pallas_call(
        _mm_t_body,
        grid=(nf // tn, steps, n // tm),
        in_specs=[pl.BlockSpec((k, tn), lambda j, t, i: (0, j)),
                  pl.BlockSpec((tm, k), lambda j, t, i: (i, t))],
        out_specs=pl.BlockSpec((None, tn, tm), lambda j, t, i: (t, j, i)),
        out_shape=jax.ShapeDtypeStruct((steps, nf, n), out_dtype),
        scratch_shapes=[pltpu.VMEM((tn, k), BF16)],
        compiler_params=_cparams(("arbitrary", "arbitrary", "arbitrary")),
        name=name,
    )(w, a2)


def _gla_body(q_ref, k_ref, v_ref, glr_ref, aup_ref, ab_ref, gate_ref, hg_ref, o_ref,
              s_ref, b_ref, kf_ref, kts_ref, dec_ref, *, tb, hpb):
    ib = pl.program_id(2)
    C, SB, dk, dv = GLA_CHUNK, GLA_SUB, GLA_DK, GLA_DV

    @pl.when(ib == 0)
    def _():
        s_ref[...] = jnp.zeros_like(s_ref)

    glr = glr_ref[:, :GLA_LOWRANK].astype(BF16)
    tw = 2 * C
    r = lax.broadcasted_iota(jnp.int32, (tw, tw), 0)
    c = lax.broadcasted_iota(jnp.int32, (tw, tw), 1)
    tri = jnp.where((r // C == c // C) & (c <= r), 1.0, 0.0).astype(BF16)
    nc = tb // C
    for hh in range(hpb):
        kc = slice(hh * dk, (hh + 1) * dk)
        z = _dot(glr, aup_ref[:, kc].astype(BF16)) + ab_ref[:, kc]
        la = _log_sigmoid(z) * (1.0 / GLA_TAU)
        hi = la.astype(BF16)
        lo = (la - hi.astype(F32)).astype(BF16)
        b_blk = jnp.concatenate(
            [_dot(tri, hi[r0:r0 + tw]) + _dot(tri, lo[r0:r0 + tw]) for r0 in range(0, tb, tw)], axis=0)
        k_blk = k_ref[:, kc].astype(F32)
        b_ref[hh] = b_blk
        kf_ref[hh] = k_blk
        b3 = b_blk.reshape(nc, C, dk)
        b_end = b3[:, C - 1:C, :]
        k_end = (k_blk.reshape(nc, C, dk) * jnp.exp(b_end - b3)).reshape(tb, dk)
        kt_end = k_end.T.astype(BF16)
        dec_t = jnp.exp(b_end.reshape(nc, dk)).T
        for ci in range(nc):
            kts_ref[hh, ci] = kt_end[:, ci * C:(ci + 1) * C]
            dec_ref[hh, ci] = jnp.broadcast_to(dec_t[:, ci:ci + 1], (dk, dv))

    row = lax.broadcasted_iota(jnp.int32, (C, 1), 0)
    rowi = lax.broadcasted_iota(jnp.int32, (C, C), 0)
    coli = lax.broadcasted_iota(jnp.int32, (C, C), 1)
    lane_c = lax.broadcasted_iota(jnp.int32, (SB, C), 1)
    sub_r = lax.broadcasted_iota(jnp.int32, (SB, 1), 0)
    ones = jnp.ones((GLA_DK, C), BF16)
    neg = -jnp.inf
    scale = GLA_DK ** -0.5

    def chunk_head(ci, r0, hh):
        kc = slice(hh * dk, (hh + 1) * dk)
        vc = slice(hh * dv, (hh + 1) * dv)
        q = q_ref[pl.ds(r0, C), kc].astype(F32) * scale
        k = kf_ref[hh, pl.ds(r0, C), :]
        b = b_ref[hh, pl.ds(r0, C), :]
        v = v_ref[pl.ds(r0, C), vc]

        s = s_ref[hh]
        o = _dot((q * jnp.exp(b)).astype(BF16), s.astype(BF16))
        s_ref[hh] = s * dec_ref[hh, ci] + _dot(kts_ref[hh, ci], v)

        attn = jnp.zeros((C, C), F32)
        h = C // 2
        while h >= SB:
            ref = jnp.broadcast_to(b.reshape(C // (2 * h), 2 * h, GLA_DK)[:, h - 1:h, :],
                                   (C // (2 * h), 2 * h, GLA_DK)).reshape(C, GLA_DK)
            upper = (row // h) % 2 == 1
            q_h = q * jnp.exp(jnp.where(upper, b - ref, neg))
            k_h = k * jnp.exp(jnp.where(upper, neg, ref - b))
            a_h = _dot_nt(q_h.astype(BF16), k_h.astype(BF16))
            attn = attn + (a_h if 2 * h == C else jnp.where(rowi // (2 * h) == coli // (2 * h), a_h, 0.0))
            h //= 2

        zs = []
        for bi in range(C // SB):
            s0 = bi * SB
            q_i = q[s0:s0 + SB, :]
            b_i = b[s0:s0 + SB, :]
            for j in range(SB):
                k_j = kf_ref[hh, pl.ds(r0 + (s0 + j), 1), :]
                b_j = b_ref[hh, pl.ds(r0 + (s0 + j), 1), :]
                zs.append(q_i * k_j * jnp.exp(jnp.where(sub_r >= j, b_i - b_j, neg)))
        zsum = _dot(jnp.concatenate(zs, axis=0).astype(BF16), ones)
        diag = []
        for bi in range(C // SB):
            acc = jnp.zeros((SB, C), F32)
            for j in range(SB):
                n0 = (bi * SB + j) * SB
                acc = acc + jnp.where(lane_c == bi * SB + j, zsum[n0:n0 + SB, :], 0.0)
            diag.append(acc)
        attn = attn + jnp.concatenate(diag, axis=0)
        o = o + _dot(attn.astype(BF16), v)

        o = o * lax.rsqrt(jnp.mean(o * o, axis=-1, keepdims=True) + EPS) * hg_ref[:, vc]
        g = gate_ref[pl.ds(r0, C), vc].astype(F32)
        o_ref[pl.ds(r0, C), vc] = (o * _silu(g)).astype(o_ref.dtype)

    def chunk(ci, carry):
        r0 = pl.multiple_of(ci * C, C)
        for hh in range(hpb):
            chunk_head(ci, r0, hh)
        return carry

    lax.fori_loop(0, tb // C, chunk, 0, unroll=GLA_UNROLL)


def gla(proj, small, alpha_up, alpha_b, head_g, *, batch, seq, q_col, k_col, v_col, gate_col, tb=512, hpb=4):
    t = batch * seq
    nb = seq // tb
    dk, dv, h = GLA_DK * hpb, GLA_DV * hpb, GLA_HEADS // hpb
    qb, kb, vb, gb = q_col // dk, k_col // dk, v_col // dv, gate_col // dv
    tok = lambda b, hh, i: b * nb + i
    nc = tb // GLA_CHUNK
    return pl.pallas_call(
        functools.partial(_gla_body, tb=tb, hpb=hpb),
        grid=(batch, h, nb),
        in_specs=[
            pl.BlockSpec((tb, dk), lambda b, hh, i: (tok(b, hh, i), qb + hh)),
            pl.BlockSpec((tb, dk), lambda b, hh, i: (tok(b, hh, i), kb + hh)),
            pl.BlockSpec((tb, dv), lambda b, hh, i: (tok(b, hh, i), vb + hh)),
            pl.BlockSpec((tb, 128), lambda b, hh, i: (tok(b, hh, i), 0)),
            pl.BlockSpec((GLA_LOWRANK, dk), lambda b, hh, i: (0, hh)),
            pl.BlockSpec((1, dk), lambda b, hh, i: (0, hh)),
            pl.BlockSpec((tb, dv), lambda b, hh, i: (tok(b, hh, i), gb + hh)),
            pl.BlockSpec((1, dv), lambda b, hh, i: (0, hh)),
        ],
        out_specs=pl.BlockSpec((tb, dv), lambda b, hh, i: (tok(b, hh, i), hh)),
        out_shape=jax.ShapeDtypeStruct((t, GLA_HEADS * GLA_DV), BF16),
        scratch_shapes=[pltpu.VMEM((hpb, GLA_DK, GLA_DV), F32), pltpu.VMEM((hpb, tb, GLA_DK), F32),
                        pltpu.VMEM((hpb, tb, GLA_DK), F32),
                        pltpu.VMEM((hpb, nc, GLA_DK, GLA_CHUNK), BF16),
                        pltpu.VMEM((hpb, nc, GLA_DK, GLA_DV), F32)],
        compiler_params=_cparams(("parallel", "parallel", "arbitrary")),
        name="gla",
    )(proj, proj, proj, small, alpha_up, alpha_b.reshape(1, -1), proj, head_g.reshape(1, -1))


def _fox_gate_body(s_ref, fb_ref, qa_ref, ka_ref, *, blk, col0):
    n = s_ref.shape[0] // blk
    r = lax.broadcasted_iota(jnp.int32, (blk, blk), 0)
    c = lax.broadcasted_iota(jnp.int32, (blk, blk), 1)
    tri = jnp.where(c <= r, 1.0, 0.0).astype(BF16)
    lanes = s_ref.shape[1]
    pr = lax.broadcasted_iota(jnp.int32, (3 * lanes, FOX_AUG), 0)
    pc = lax.broadcasted_iota(jnp.int32, (3 * lanes, FOX_AUG), 1)
    head, piece = pr % lanes - col0, pr // lanes
    is_head = (head >= 0) & (head < FOX_HEADS)
    sel_q = jnp.where(is_head & (pc == head * FOX_AUG_GROUP + piece), 1.0, 0.0).astype(BF16)
    sel_k = jnp.where(is_head & (pc == head * FOX_AUG_GROUP + 3 + piece), -1.0, 0.0).astype(BF16)
    lane = lax.broadcasted_iota(jnp.int32, (1, FOX_AUG), 1)
    used = lane < FOX_HEADS * FOX_AUG_GROUP
    one_q = jnp.where(used & (lane % FOX_AUG_GROUP >= 3) & (lane % FOX_AUG_GROUP < 6), 1.0, 0.0)
    one_k = jnp.where(used & (lane % FOX_AUG_GROUP < 3), 1.0, 0.0)

    def step(i, carry):
        r0 = pl.multiple_of(i * blk, blk)
        lf = _log_sigmoid(s_ref[pl.ds(r0, blk), :] + fb_ref[...])
        hi, mid, lo = _split3(lf)
        cs = _dot(tri, hi) + _dot(tri, mid) + _dot(tri, lo) + carry
        pieces = jnp.concatenate(_split3(cs * LOG2E), axis=1)
        qa_ref[pl.ds(r0, blk), :] = (_dot(pieces, sel_q) + one_q).astype(BF16)
        ka_ref[pl.ds(r0, blk), :] = (_dot(pieces, sel_k) + one_k).astype(BF16)
        return cs[blk - 1:blk, :]

    lax.fori_loop(0, n, step, jnp.zeros((1, s_ref.shape[1]), F32))


def fox_gate(small, fb_row, *, batch, seq, col0, blk=256):
    shp = jax.ShapeDtypeStruct((batch * seq, FOX_AUG), BF16)
    spec = pl.BlockSpec((seq, FOX_AUG), lambda b: (b, 0))
    return pl.pallas_call(
        functools.partial(_fox_gate_body, blk=blk, col0=col0),
        grid=(batch,),
        in_specs=[pl.BlockSpec((seq, 128), lambda b: (b, 0)),
                  pl.BlockSpec((1, 128), lambda b: (0, 0))],
        out_specs=[spec, spec],
        out_shape=[shp, shp],
        compiler_params=_cparams(("parallel",)),
        name="fox_gate",
    )(small, fb_row)


def _fox_body(it_ref, jt_ref, q_ref, qa_ref, k_ref, ka_ref, v_ref, gate_ref, o_ref,
              m_ref, l_ref, acc_ref, *, tq, tk, hpb):
    p = pl.program_id(2)
    i = it_ref[p]
    j = jt_ref[p]
    d = FOX_DH

    @pl.when(j == 0)
    def _():
        m_ref[...] = jnp.full_like(m_ref, -jnp.inf)
        l_ref[...] = jnp.zeros_like(l_ref)
        acc_ref[...] = jnp.zeros_like(acc_ref)

    aug_lane = lax.broadcasted_iota(jnp.int32, (1, FOX_AUG), 1)

    def update(hh, masked):
        mine = aug_lane // FOX_AUG_GROUP == pl.program_id(1) * hpb + hh
        zero = jnp.zeros((), BF16)
        q_aug = jnp.concatenate([q_ref[:, hh * d:(hh + 1) * d], jnp.where(mine, qa_ref[...], zero)], axis=1)
        k_aug = jnp.concatenate([k_ref[:, hh * d:(hh + 1) * d], jnp.where(mine, ka_ref[...], zero)], axis=1)
        st = _dot_nt(k_aug, q_aug)
        if masked:
            kr = lax.broadcasted_iota(jnp.int32, (tk, tq), 0)
            qc = lax.broadcasted_iota(jnp.int32, (tk, tq), 1)
            st = jnp.where(qc >= kr, st, -jnp.inf)
        m_old = m_ref[hh]
        m_new = jnp.maximum(m_old, jnp.max(st, axis=0, keepdims=True))
        alpha = jnp.exp2(m_old - m_new)
        pt = jnp.exp2(st - m_new)
        l_ref[hh] = alpha * l_ref[hh] + jnp.sum(pt, axis=0, keepdims=True)
        acc_ref[hh] = alpha * acc_ref[hh] + _dot_tn(v_ref[:, hh * d:(hh + 1) * d], pt.astype(BF16))
        m_ref[hh] = m_new

    @pl.when(j < i)
    def _():
        for hh in range(hpb):
            update(hh, False)

    @pl.when(j == i)
    def _():
        for hh in range(hpb):
            update(hh, True)
            o = (acc_ref[hh] / l_ref[hh]).T
            g = gate_ref[:, hh * d:(hh + 1) * d].astype(F32)
            o_ref[:, hh * d:(hh + 1) * d] = (o * _silu(g)).astype(o_ref.dtype)


def fox(proj, qa, ka, *, batch, seq, q_col, k_col, v_col, gate_col, t=1024, hpb=4):
    tt = batch * seq
    nb = seq // t
    d, h = FOX_DH, FOX_HEADS
    w = hpb * d
    qb, kb, gb, vb = q_col // w, k_col // w, gate_col // w, v_col // w
    pairs = [(i, j) for i in range(nb) for j in range(i + 1)]
    it = jnp.array([p[0] for p in pairs], jnp.int32)
    jt = jnp.array([p[1] for p in pairs], jnp.int32)
    grid_spec = pltpu.PrefetchScalarGridSpec(
        num_scalar_prefetch=2,
        grid=(batch, h // hpb, len(pairs)),
        in_specs=[
            pl.BlockSpec((t, w), lambda b, hh, p, it, jt: (b * nb + it[p], qb + hh)),
            pl.BlockSpec((t, FOX_AUG), lambda b, hh, p, it, jt: (b * nb + it[p], 0)),
            pl.BlockSpec((t, w), lambda b, hh, p, it, jt: (b * nb + jt[p], kb + hh)),
            pl.BlockSpec((t, FOX_AUG), lambda b, hh, p, it, jt: (b * nb + jt[p], 0)),
            pl.BlockSpec((t, w), lambda b, hh, p, it, jt: (b * nb + jt[p], vb + hh)),
            pl.BlockSpec((t, w), lambda b, hh, p, it, jt: (b * nb + it[p], gb + hh)),
        ],
        out_specs=pl.BlockSpec((t, w), lambda b, hh, p, it, jt: (b * nb + it[p], hh)),
        scratch_shapes=[pltpu.VMEM((hpb, 1, t), F32), pltpu.VMEM((hpb, 1, t), F32), pltpu.VMEM((hpb, d, t), F32)],
    )
    return pl.pallas_call(
        functools.partial(_fox_body, tq=t, tk=t, hpb=hpb),
        grid_spec=grid_spec,
        out_shape=jax.ShapeDtypeStruct((tt, h * d), BF16),
        compiler_params=_cparams(("parallel", "parallel", "arbitrary")),
        name="fox",
    )(it, jt, proj, qa, proj, ka, proj, proj)


def _regroup_perm(tm, steps, to_grouped):
    r = lax.broadcasted_iota(jnp.int32, (tm, tm), 0)
    c = lax.broadcasted_iota(jnp.int32, (tm, tm), 1)
    nc = tm // steps
    src = (r % nc) * steps + r // nc if to_grouped else (r % steps) * nc + r // steps
    return jnp.where(c == src, 1.0, 0.0).astype(BF16)


def _outproj_mid_body(a1_ref, a2_ref, w_ref, x_ref, g_ref, r_ref, h_ref, *, steps):
    k1 = a1_ref.shape[1]
    tm, d = x_ref.shape
    nc = tm // steps
    acc = x_ref[...] + _dot(a1_ref[...], w_ref[:k1, :]) + _dot(a2_ref[...], w_ref[k1:, :])
    r_ref[...] = acc
    normed = (acc * lax.rsqrt(jnp.mean(acc * acc, axis=-1, keepdims=True) + EPS) * g_ref[...]).astype(BF16)
    by_step = _dot(_regroup_perm(tm, steps, True), normed).astype(h_ref.dtype)
    for s in range(steps):
        h_ref[:, s * d:(s + 1) * d] = by_step[s * nc:(s + 1) * nc, :]


def outproj_mid(a1, a2, w, x, g, *, steps, tm=256, name="outproj_mid"):
    m, d = x.shape
    row = pl.BlockSpec((tm, d), lambda i: (i, 0))
    return pl.pallas_call(
        functools.partial(_outproj_mid_body, steps=steps),
        grid=(m // tm,),
        in_specs=[pl.BlockSpec((tm, a1.shape[1]), lambda i: (i, 0)),
                  pl.BlockSpec((tm, a2.shape[1]), lambda i: (i, 0)),
                  pl.BlockSpec(w.shape, lambda i: (0, 0)),
                  row,
                  pl.BlockSpec((1, d), lambda i: (0, 0))],
        out_specs=[row, pl.BlockSpec((tm // steps, steps * d), lambda i: (i, 0))],
        out_shape=[jax.ShapeDtypeStruct((m, d), F32), jax.ShapeDtypeStruct((m // steps, steps * d), BF16)],
        compiler_params=_cparams(("parallel",)),
        name=name,
    )(a1, a2, w, x, g.reshape(1, d))


def _outproj_final_body(a_ref, w_ref, x_ref, g_ref, o_ref, *, steps):
    tm = x_ref.shape[0]
    e = w_ref.shape[0]
    by_step = jnp.concatenate([a_ref[:, s * e:(s + 1) * e] for s in range(steps)], axis=0)
    a_tok = _dot(_regroup_perm(tm, steps, False), by_step).astype(BF16)
    acc = x_ref[...] + _dot(a_tok, w_ref[...])
    o_ref[...] = acc * lax.rsqrt(jnp.mean(acc * acc, axis=-1, keepdims=True) + EPS) * g_ref[...]


def outproj_final(a_grouped, w, x, g, *, steps, tm=256, name="outproj_final"):
    m, d = x.shape
    e = w.shape[0]
    row = pl.BlockSpec((tm, d), lambda i: (i, 0))
    return pl.pallas_call(
        functools.partial(_outproj_final_body, steps=steps),
        grid=(m // tm,),
        in_specs=[pl.BlockSpec((tm // steps, steps * e), lambda i: (i, 0)),
                  pl.BlockSpec(w.shape, lambda i: (0, 0)),
                  row,
                  pl.BlockSpec((1, d), lambda i: (0, 0))],
        out_specs=row,
        out_shape=jax.ShapeDtypeStruct((m, d), F32),
        compiler_params=_cparams(("parallel",)),
        name=name,
    )(a_grouped, w, x, g.reshape(1, d))


S5_GPB = 2


def _s5_group(u, wt, nt, mt, arow, *, cols_per_seq, n_levels):
    n = u.shape[1]
    p = S5_STATE
    nseq = n // cols_per_seq
    x = _dot(nt, u)

    def to_rows(part):
        return jnp.concatenate([part[:, s * cols_per_seq:(s + 1) * cols_per_seq] for s in range(nseq)], axis=0).T

    def to_cols(rows):
        rt = rows.T
        return jnp.concatenate([rt[s * p:(s + 1) * p, :] for s in range(nseq)], axis=1)

    xr, xi = to_rows(x[:p, :]), to_rows(x[p:, :])
    pos = lax.broadcasted_iota(jnp.int32, (cols_per_seq, 1), 0)
    for lvl in range(n_levels):
        sh = 1 << lvl
        keep = pos >= sh
        sr = jnp.where(keep, pltpu.roll(xr, sh, axis=0), 0.0)
        si = jnp.where(keep, pltpu.roll(xi, sh, axis=0), 0.0)
        ar, ai = arow[lvl:lvl + 1, :], arow[n_levels + lvl:n_levels + lvl + 1, :]
        xr, xi = xr + ar * sr - ai * si, xi + ar * si + ai * sr
    keep = pos >= 1
    pr = jnp.where(keep, pltpu.roll(xr, 1, axis=0), 0.0)
    pi = jnp.where(keep, pltpu.roll(xi, 1, axis=0), 0.0)
    x_prev = jnp.concatenate([to_cols(pr), to_cols(pi)], axis=0).astype(BF16)
    return _dot(wt, u) + _dot(mt, x_prev)


def _s5_body(u_ref, wt_ref, nt_ref, mt_ref, arow_ref, o_ref, *, cols_per_seq, n_levels):
    t, _, n = u_ref.shape
    cg = S5_GROUP_CH
    for gi in range(S5_GPB):
        ch = slice(gi * cg, (gi + 1) * cg)
        u = u_ref[:, ch, :].reshape(t * cg, n)
        y = _s5_group(u, wt_ref[gi], nt_ref[gi], mt_ref[gi], arow_ref[gi],
                      cols_per_seq=cols_per_seq, n_levels=n_levels)
        o_ref[:, ch, :] = y.astype(o_ref.dtype).reshape(t, cg, n)


def s5_apply(proj_t, wt_g, nt_g, mt_g, arow, *, groups, cols_per_seq, n_levels):
    t, _, n = proj_t.shape
    cg = S5_GROUP_CH * S5_GPB
    spec3 = lambda shp: pl.BlockSpec((S5_GPB,) + shp, lambda i: (i, 0, 0))
    return pl.pallas_call(
        functools.partial(_s5_body, cols_per_seq=cols_per_seq, n_levels=n_levels),
        grid=(groups // S5_GPB,),
        in_specs=[pl.BlockSpec((t, cg, n), lambda i: (0, i, 0)),
                  spec3(wt_g.shape[1:]), spec3(nt_g.shape[1:]), spec3(mt_g.shape[1:]), spec3(arow.shape[1:])],
        out_specs=pl.BlockSpec((t, cg, n), lambda i: (0, i, 0)),
        out_shape=jax.ShapeDtypeStruct((t, groups * S5_GROUP_CH, n), BF16),
        compiler_params=_cparams(("parallel",)),
        name="s5",
    )(proj_t, wt_g, nt_g, mt_g, arow)


def s5_operators(lam_re, lam_im, log_step, b_re, b_im, c_re, c_im, d, *, n_levels, n_seq):
    hp = lax.Precision.HIGHEST
    g, p = lam_re.shape
    cg, t = S5_GROUP_CH, S5_T
    lr = jnp.minimum(lam_re.astype(F32), -1e-4)
    li = lam_im.astype(F32)
    step = jnp.exp(log_step.astype(F32))[:, None]
    mag = jnp.exp(lr * step)
    lb_re, lb_im = mag * jnp.cos(li * step), mag * jnp.sin(li * step)
    den = lr * lr + li * li
    nr, ni = lb_re - 1.0, lb_im
    coef_re = (nr * lr + ni * li) / den
    coef_im = (ni * lr - nr * li) / den
    bb_re = coef_re[..., None] * b_re - coef_im[..., None] * b_im
    bb_im = coef_re[..., None] * b_im + coef_im[..., None] * b_re

    def power(m):
        mm = m.astype(F32)[None, :, None]
        mg = jnp.exp(lr[:, None, :] * step[:, None, :] * mm)
        ang = li[:, None, :] * step[:, None, :] * mm
        return mg * jnp.cos(ang), mg * jnp.sin(ang)

    pw_re, pw_im = power(jnp.arange(t + 1))
    pwt_re, pwt_im = pw_re.transpose(0, 2, 1), pw_im.transpose(0, 2, 1)

    def pow_times_b(pr, pi):
        re = pr[:, :, :, None] * bb_re[:, :, None, :] - pi[:, :, :, None] * bb_im[:, :, None, :]
        im = pr[:, :, :, None] * bb_im[:, :, None, :] + pi[:, :, :, None] * bb_re[:, :, None, :]
        return re.reshape(g, p, t * cg), im.reshape(g, p, t * cg)

    pb_re, pb_im = pow_times_b(pwt_re[:, :, :t], pwt_im[:, :, :t])
    taps = (jnp.einsum('gcp,gpy->gcy', c_re.astype(F32), pb_re, precision=hp)
            - jnp.einsum('gcp,gpy->gcy', c_im.astype(F32), pb_im, precision=hp))
    y_idx = jnp.arange(t * cg)
    taps = taps + d.astype(F32).reshape(g, cg, 1) * (y_idx[None, :] == jnp.arange(cg)[:, None]).astype(F32)[None]
    place = ((y_idx[None, :, None] // cg == jnp.arange(t)[:, None, None] - y_idx[None, None, :] // cg)
             & (y_idx[None, :, None] % cg == y_idx[None, None, :] % cg)).astype(BF16)
    wt_g = jnp.einsum('gcy,tyz->gtcz', taps.astype(BF16), place,
                      preferred_element_type=F32).astype(BF16).reshape(g, t * cg, t * cg)
    n_re, n_im = pow_times_b(pwt_re[:, :, t - 1::-1], pwt_im[:, :, t - 1::-1])
    nt_g = jnp.concatenate([n_re, n_im], axis=1).astype(BF16)
    cp_re = c_re[:, None] * pw_re[:, 1:, None, :] - c_im[:, None] * pw_im[:, 1:, None, :]
    cp_im = c_re[:, None] * pw_im[:, 1:, None, :] + c_im[:, None] * pw_re[:, 1:, None, :]
    mt_g = jnp.concatenate([cp_re, -cp_im], axis=-1).reshape(g, t * cg, 2 * p).astype(BF16)
    ar, ai = power(t * (2 ** jnp.arange(n_levels)))
    arow = jnp.concatenate([jnp.tile(ar, (1, 1, n_seq)), jnp.tile(ai, (1, 1, n_seq))], axis=1)
    return wt_g, nt_g, mt_g, arow


def _gelu_tanh(y):
    return 0.5 * y * (1.0 + jnp.tanh(math.sqrt(2.0 / math.pi) * (y + 0.044715 * (y * y * y))))


def _glu_body(y_ref, w_ref, b_ref, gate_ref, o_ref, z_ref, zb_ref, wt_ref):
    @pl.when((pl.program_id(0) == 0) & (pl.program_id(1) == 0))
    def _():
        _transpose_into(wt_ref, w_ref)

    z = _gelu_tanh(y_ref[...].astype(F32))
    z_ref[...] = z
    zb_ref[...] = z.astype(BF16)
    for f0 in range(0, o_ref.shape[1], MM_SUB):
        rows = slice(f0, f0 + MM_SUB)
        lin = _dot(wt_ref[rows, :], zb_ref[...]) + b_ref[rows, :]
        out = z_ref[rows, :] * _sigmoid(lin) * _silu(gate_ref[rows, :].astype(F32))
        o_ref[:, rows] = out.T.astype(o_ref.dtype)


def glu_gate(y_t, w, b, proj_t, *, gate_row, tm=512):
    s, e, n = y_t.shape
    gb = gate_row // e
    return pl.pallas_call(
        _glu_body,
        grid=(s, n // tm),
        in_specs=[pl.BlockSpec((None, e, tm), lambda t, i: (t, 0, i)),
                  pl.BlockSpec((e, e), lambda t, i: (0, 0)),
                  pl.BlockSpec((e, 1), lambda t, i: (0, 0)),
                  pl.BlockSpec((None, e, tm), lambda t, i: (t, gb, i))],
        out_specs=pl.BlockSpec((tm, e), lambda t, i: (i, t)),
        out_shape=jax.ShapeDtypeStruct((n, s * e), BF16),
        scratch_shapes=[pltpu.VMEM((e, tm), F32), pltpu.VMEM((e, tm), BF16), pltpu.VMEM((e, e), BF16)],
        compiler_params=_cparams(("arbitrary", "arbitrary")),
        name="glu",
    )(y_t, w, b.reshape(e, 1), proj_t)


def kernel(x, ab_norm_g, ab_w_in, gla_alpha_up, gla_alpha_b, gla_head_g, fox_f_b, ab_w_out, c_norm_g, c_w_in, s5_lambda_re, s5_lambda_im, s5_log_step, s5_b_re, s5_b_im, s5_c_re, s5_c_im, s5_d, glu_w, glu_b, c_w_out, final_norm_g):
    batch, seq, d = x.shape
    t = batch * seq
    x2 = x.reshape(t, d)

    hk, hv, fw = GLA_HEADS * GLA_DK, GLA_HEADS * GLA_DV, FOX_HEADS * FOX_DH
    sizes = (hk, hk, hv, GLA_LOWRANK, hv, fw, fw, fw, FOX_HEADS, fw)
    offs = [0]
    for s in sizes:
        offs.append(offs[-1] + s)
    w_main, w_small = ab_weight_prep(jnp.swapaxes(ab_w_in, 1, 2), offs)
    c_gq, c_gk, c_gv, c_gg = 0, hk, 2 * hk, 2 * hk + hv
    c_fq = c_gg + hv
    c_fk, c_fv, c_fg = c_fq + fw, c_fq + 2 * fw, c_fq + 3 * fw

    h0, small = rmsnorm_bf16(x2, ab_norm_g[0], w_small)
    proj = matmul(h0, w_main, BF16, tn=w_main.shape[0] // 2, name="ab_in")

    o_gla = gla(proj, small, gla_alpha_up[0], gla_alpha_b[0], gla_head_g[0],
                batch=batch, seq=seq, q_col=c_gq, k_col=c_gk, v_col=c_gv, gate_col=c_gg)

    fb_row = jnp.zeros((1, 128), F32).at[0, GLA_LOWRANK:GLA_LOWRANK + FOX_HEADS].set(fox_f_b[0])
    qa, ka = fox_gate(small, fb_row, batch=batch, seq=seq, col0=GLA_LOWRANK)
    o_fox = fox(proj, qa, ka, batch=batch, seq=seq, q_col=c_fq, k_col=c_fk, v_col=c_fv, gate_col=c_fg)

    x1, h1g = outproj_mid(o_gla, o_fox, ab_w_out[0].astype(BF16), x2, c_norm_g[0], steps=S5_T, name="ab_out")

    e = d
    groups = e // S5_GROUP_CH
    n_chunks = t // S5_T
    cols_per_seq = seq // S5_T
    n_levels = max(1, (cols_per_seq - 1).bit_length())
    proj_t = matmul_t_grouped(c_w_in[0].astype(BF16), h1g, S5_T, BF16, name="c_in")
    ops = s5_operators(s5_lambda_re[0], s5_lambda_im[0], s5_log_step[0], s5_b_re[0], s5_b_im[0],
                       s5_c_re[0], s5_c_im[0], s5_d[0], n_levels=n_levels, n_seq=batch)
    y_t = s5_apply(proj_t, *ops, groups=groups, cols_per_seq=cols_per_seq, n_levels=n_levels)
    zz = glu_gate(y_t, glu_w[0].astype(BF16), glu_b[0], proj_t, gate_row=e)
    out = outproj_final(zz, c_w_out[0].astype(BF16), x1, final_norm_g, steps=S5_T, name="c_out")
    return out.reshape(batch, seq, d)
```

```python
import functools
import math

import jax
import jax.numpy as jnp
from jax import lax
from jax.experimental import pallas as pl
from jax.experimental.pallas import tpu as pltpu

EPS = 1e-6
F32 = jnp.float32
BF16 = jnp.bfloat16

GLA_HEADS = 4
GLA_DK = 128
GLA_DV = 256
GLA_LOWRANK = 16
GLA_TAU = 16.0
GLA_CHUNK = 64
GLA_SUB = 8
GLA_UNROLL = 2
FOX_HEADS = 8
FOX_DH = 128
FOX_AUG = 128
FOX_AUG_GROUP = 8
S5_GROUP_CH = 16
S5_STATE = 64
S5_T = 16

VMEM_LIMIT = 56 * 1024 * 1024

NT_DIMS = (((1,), (1,)), ((), ()))
TN_DIMS = (((0,), (0,)), ((), ()))
LOG2E = math.log2(math.e)


def _cparams(sem):
    return pltpu.CompilerParams(dimension_semantics=sem, vmem_limit_bytes=VMEM_LIMIT)


def _dot(a, b):
    return jnp.dot(a, b, preferred_element_type=F32)


def _dot_nt(a, b):
    return lax.dot_general(a, b, NT_DIMS, preferred_element_type=F32)


def _dot_tn(a, b):
    return lax.dot_general(a, b, TN_DIMS, preferred_element_type=F32)


def _log_sigmoid(z):
    return -(jnp.maximum(-z, 0.0) + jnp.log1p(jnp.exp(-jnp.abs(z))))


def _sigmoid(z):
    return 1.0 / (1.0 + jnp.exp(-z))


def _silu(z):
    return z * _sigmoid(z)


def _split3(x):
    hi = x.astype(BF16)
    r1 = x - hi.astype(F32)
    mid = r1.astype(BF16)
    lo = (r1 - mid.astype(F32)).astype(BF16)
    return hi, mid, lo


def _norm_proj_body(x_ref, g_ref, wt_ref, ws_ref, o_ref, s_ref):
    x = x_ref[...]
    ms = jnp.mean(x * x, axis=-1, keepdims=True)
    h = (x * lax.rsqrt(ms + EPS) * g_ref[...]).astype(BF16)
    for c0 in range(0, o_ref.shape[1], MM_SUB):
        o_ref[:, c0:c0 + MM_SUB] = _dot_nt(h, wt_ref[c0:c0 + MM_SUB, :]).astype(o_ref.dtype)
    s_ref[...] = _dot_nt(h, ws_ref[...])


def norm_proj(x, g, wt, wt_small, tm=512, tn=512, name="norm_proj"):
    m, d = x.shape
    n = wt.shape[0]
    ns = wt_small.shape[0]
    return pl.pallas_call(
        _norm_proj_body,
        grid=(n // tn, m // tm),
        in_specs=[pl.BlockSpec((tm, d), lambda j, i: (i, 0)),
                  pl.BlockSpec((1, d), lambda j, i: (0, 0)),
                  pl.BlockSpec((tn, d), lambda j, i: (j, 0)),
                  pl.BlockSpec((ns, d), lambda j, i: (0, 0))],
        out_specs=[pl.BlockSpec((tm, tn), lambda j, i: (i, j)),
                   pl.BlockSpec((tm, ns), lambda j, i: (i, 0))],
        out_shape=[jax.ShapeDtypeStruct((m, n), BF16), jax.ShapeDtypeStruct((m, ns), F32)],
        compiler_params=_cparams(("arbitrary", "arbitrary")),
        name=name,
    )(x, g.reshape(1, d), wt, wt_small)


def _ab_wprep_body(w_ref, o_ref, s_ref, *, offs):
    w = w_ref[0]
    seg = lambda n: w[offs[n]:offs[n + 1], :]
    o_ref[...] = jnp.concatenate(
        [w[:offs[3], :], seg(4), seg(5) * (FOX_DH ** -0.5 * LOG2E), w[offs[6]:offs[8], :], seg(9)],
        axis=0).astype(o_ref.dtype)
    pad = s_ref.shape[0] - (offs[4] - offs[3]) - (offs[9] - offs[8])
    s_ref[...] = jnp.concatenate([seg(3), seg(8), jnp.zeros((pad, w.shape[1]), F32)], axis=0).astype(s_ref.dtype)


def ab_weight_prep(w_t3, offs, tk=256):
    _, n_in, d = w_t3.shape
    n_main = n_in - (offs[4] - offs[3]) - (offs[9] - offs[8])
    return pl.pallas_call(
        functools.partial(_ab_wprep_body, offs=tuple(offs)),
        grid=(d // tk,),
        in_specs=[pl.BlockSpec((1, n_in, tk), lambda i: (0, 0, i))],
        out_specs=[pl.BlockSpec((n_main, tk), lambda i: (0, i)), pl.BlockSpec((128, tk), lambda i: (0, i))],
        out_shape=[jax.ShapeDtypeStruct((n_main, d), BF16), jax.ShapeDtypeStruct((128, d), BF16)],
        compiler_params=_cparams(("parallel",)),
        name="ab_wprep",
    )(w_t3)


MM_SUB = 512


def _transpose_into(wt_ref, w_ref):
    for c0 in range(0, w_ref.shape[1], MM_SUB):
        wt_ref[c0:c0 + MM_SUB, :] = w_ref[:, c0:c0 + MM_SUB].astype(wt_ref.dtype).T


def _mm_t_body(w_ref, a_ref, o_ref, wt_ref):
    @pl.when((pl.program_id(1) == 0) & (pl.program_id(2) == 0))
    def _():
        _transpose_into(wt_ref, w_ref)

    a = a_ref[...]
    for r0 in range(0, o_ref.shape[0], MM_SUB):
        o_ref[r0:r0 + MM_SUB, :] = _dot_nt(wt_ref[r0:r0 + MM_SUB, :], a).astype(o_ref.dtype)


def matmul_t_grouped(w, a2, steps, out_dtype, tm=512, tn=2048, name="matmul_tg"):
    k, nf = w.shape
    n = a2.shape[0]
    return pl.pallas_call(
        _mm_t_body,
        grid=(nf // tn, steps, n // tm),
        in_specs=[pl.BlockSpec((k, tn), lambda j, t, i: (0, j), pipeline_mode=pl.Buffered(1)),
                  pl.BlockSpec((tm, k), lambda j, t, i: (i, t))],
        out_specs=pl.BlockSpec((None, tn, tm), lambda j, t, i: (t, j, i)),
        out_shape=jax.ShapeDtypeStruct((steps, nf, n), out_dtype),
        scratch_shapes=[pltpu.VMEM((tn, k), BF16)],
        compiler_params=_cparams(("arbitrary", "arbitrary", "arbitrary")),
        name=name,
    )(w, a2)


def _gla_body(q_ref, k_ref, v_ref, glr_ref, aup_ref, ab_ref, gate_ref, hg_ref, o_ref,
              s_ref, b_ref, kf_ref, kts_ref, dec_ref, *, tb, hpb):
    ib = pl.program_id(2)
    C, SB, dk, dv = GLA_CHUNK, GLA_SUB, GLA_DK, GLA_DV

    @pl.when(ib == 0)
    def _():
        s_ref[...] = jnp.zeros_like(s_ref)

    glr = glr_ref[:, :GLA_LOWRANK].astype(BF16)
    tw = 2 * C
    r = lax.broadcasted_iota(jnp.int32, (tw, tw), 0)
    c = lax.broadcasted_iota(jnp.int32, (tw, tw), 1)
    tri = jnp.where((r // C == c // C) & (c <= r), 1.0, 0.0).astype(BF16)
    nc = tb // C
    for hh in range(hpb):
        kc = slice(hh * dk, (hh + 1) * dk)
        z = _dot(glr, aup_ref[:, kc].astype(BF16)) + ab_ref[:, kc]
        la = _log_sigmoid(z) * (1.0 / GLA_TAU)
        hi = la.astype(BF16)
        lo = (la - hi.astype(F32)).astype(BF16)
        b_blk = jnp.concatenate(
            [_dot(tri, hi[r0:r0 + tw]) + _dot(tri, lo[r0:r0 + tw]) for r0 in range(0, tb, tw)], axis=0)
        k_blk = k_ref[:, kc].astype(F32)
        b_ref[hh] = b_blk
        kf_ref[hh] = k_blk
        b3 = b_blk.reshape(nc, C, dk)
        b_end = b3[:, C - 1:C, :]
        k_end = (k_blk.reshape(nc, C, dk) * jnp.exp(b_end - b3)).reshape(tb, dk)
        kt_end = k_end.T.astype(BF16)
        dec_t = jnp.exp(b_end.reshape(nc, dk)).T
        for ci in range(nc):
            kts_ref[hh, ci] = kt_end[:, ci * C:(ci + 1) * C]
            dec_ref[hh, ci] = jnp.broadcast_to(dec_t[:, ci:ci + 1], (dk, dv))

    row = lax.broadcasted_iota(jnp.int32, (C, 1), 0)
    rowi = lax.broadcasted_iota(jnp.int32, (C, C), 0)
    coli = lax.broadcasted_iota(jnp.int32, (C, C), 1)
    lane_c = lax.broadcasted_iota(jnp.int32, (SB, C), 1)
    sub_r = lax.broadcasted_iota(jnp.int32, (SB, 1), 0)
    ones = jnp.ones((GLA_DK, C), BF16)
    neg = -jnp.inf
    scale = GLA_DK ** -0.5

    def chunk_head(ci, r0, hh):
        kc = slice(hh * dk, (hh + 1) * dk)
        vc = slice(hh * dv, (hh + 1) * dv)
        q = q_ref[pl.ds(r0, C), kc].astype(F32) * scale
        k = kf_ref[hh, pl.ds(r0, C), :]
        b = b_ref[hh, pl.ds(r0, C), :]
        v = v_ref[pl.ds(r0, C), vc]

        s = s_ref[hh]
        o = _dot((q * jnp.exp(b)).astype(BF16), s.astype(BF16))
        s_ref[hh] = s * dec_ref[hh, ci] + _dot(kts_ref[hh, ci], v)

        attn = jnp.zeros((C, C), F32)
        h = C // 2
        while h >= SB:
            ref = jnp.broadcast_to(b.reshape(C // (2 * h), 2 * h, GLA_DK)[:, h - 1:h, :],
                                   (C // (2 * h), 2 * h, GLA_DK)).reshape(C, GLA_DK)
            upper = (row // h) % 2 == 1
            q_h = q * jnp.exp(jnp.where(upper, b - ref, neg))
            k_h = k * jnp.exp(jnp.where(upper, neg, ref - b))
            a_h = _dot_nt(q_h.astype(BF16), k_h.astype(BF16))
            attn = attn + (a_h if 2 * h == C else jnp.where(rowi // (2 * h) == coli // (2 * h), a_h, 0.0))
            h //= 2

        zs = []
        for bi in range(C // SB):
            s0 = bi * SB
            q_i = q[s0:s0 + SB, :]
            b_i = b[s0:s0 + SB, :]
            for j in range(SB):
                k_j = kf_ref[hh, pl.ds(r0 + (s0 + j), 1), :]
                b_j = b_ref[hh, pl.ds(r0 + (s0 + j), 1), :]
                zs.append(q_i * k_j * jnp.exp(jnp.where(sub_r >= j, b_i - b_j, neg)))
        zsum = _dot(jnp.concatenate(zs, axis=0).astype(BF16), ones)
        diag = []
        for bi in range(C // SB):
            acc = jnp.zeros((SB, C), F32)
            for j in range(SB):
                n0 = (bi * SB + j) * SB
                acc = acc + jnp.where(lane_c == bi * SB + j, zsum[n0:n0 + SB, :], 0.0)
            diag.append(acc)
        attn = attn + jnp.concatenate(diag, axis=0)
        o = o + _dot(attn.astype(BF16), v)

        o = o * lax.rsqrt(jnp.mean(o * o, axis=-1, keepdims=True) + EPS) * hg_ref[:, vc]
        g = gate_ref[pl.ds(r0, C), vc].astype(F32)
        o_ref[pl.ds(r0, C), vc] = (o * _silu(g)).astype(o_ref.dtype)

    def chunk(ci, carry):
        r0 = pl.multiple_of(ci * C, C)
        for hh in range(hpb):
            chunk_head(ci, r0, hh)
        return carry

    lax.fori_loop(0, tb // C, chunk, 0, unroll=GLA_UNROLL)


def gla(proj, small, alpha_up, alpha_b, head_g, *, batch, seq, q_col, k_col, v_col, gate_col, tb=512, hpb=4):
    t = batch * seq
    nb = seq // tb
    dk, dv, h = GLA_DK * hpb, GLA_DV * hpb, GLA_HEADS // hpb
    qb, kb, vb, gb = q_col // dk, k_col // dk, v_col // dv, gate_col // dv
    tok = lambda b, hh, i: b * nb + i
    nc = tb // GLA_CHUNK
    return pl.pallas_call(
        functools.partial(_gla_body, tb=tb, hpb=hpb),
        grid=(batch, h, nb),
        in_specs=[
            pl.BlockSpec((tb, dk), lambda b, hh, i: (tok(b, hh, i), qb + hh)),
            pl.BlockSpec((tb, dk), lambda b, hh, i: (tok(b, hh, i), kb + hh)),
            pl.BlockSpec((tb, dv), lambda b, hh, i: (tok(b, hh, i), vb + hh)),
            pl.BlockSpec((tb, 128), lambda b, hh, i: (tok(b, hh, i), 0)),
            pl.BlockSpec((GLA_LOWRANK, dk), lambda b, hh, i: (0, hh)),
            pl.BlockSpec((1, dk), lambda b, hh, i: (0, hh)),
            pl.BlockSpec((tb, dv), lambda b, hh, i: (tok(b, hh, i), gb + hh)),
            pl.BlockSpec((1, dv), lambda b, hh, i: (0, hh)),
        ],
        out_specs=pl.BlockSpec((tb, dv), lambda b, hh, i: (tok(b, hh, i), hh)),
        out_shape=jax.ShapeDtypeStruct((t, GLA_HEADS * GLA_DV), BF16),
        scratch_shapes=[pltpu.VMEM((hpb, GLA_DK, GLA_DV), F32), pltpu.VMEM((hpb, tb, GLA_DK), F32),
                        pltpu.VMEM((hpb, tb, GLA_DK), F32),
                        pltpu.VMEM((hpb, nc, GLA_DK, GLA_CHUNK), BF16),
                        pltpu.VMEM((hpb, nc, GLA_DK, GLA_DV), F32)],
        compiler_params=_cparams(("parallel", "parallel", "arbitrary")),
        name="gla",
    )(proj, proj, proj, small, alpha_up, alpha_b.reshape(1, -1), proj, head_g.reshape(1, -1))


def _fox_gate_body(s_ref, fb_ref, qa_ref, ka_ref, *, blk, col0):
    n = s_ref.shape[0] // blk
    r = lax.broadcasted_iota(jnp.int32, (blk, blk), 0)
    c = lax.broadcasted_iota(jnp.int32, (blk, blk), 1)
    tri = jnp.where(c <= r, 1.0, 0.0).astype(BF16)
    lanes = s_ref.shape[1]
    pr = lax.broadcasted_iota(jnp.int32, (3 * lanes, FOX_AUG), 0)
    pc = lax.broadcasted_iota(jnp.int32, (3 * lanes, FOX_AUG), 1)
    head, piece = pr % lanes - col0, pr // lanes
    is_head = (head >= 0) & (head < FOX_HEADS)
    sel_q = jnp.where(is_head & (pc == head * FOX_AUG_GROUP + piece), 1.0, 0.0).astype(BF16)
    sel_k = jnp.where(is_head & (pc == head * FOX_AUG_GROUP + 3 + piece), -1.0, 0.0).astype(BF16)
    lane = lax.broadcasted_iota(jnp.int32, (1, FOX_AUG), 1)
    used = lane < FOX_HEADS * FOX_AUG_GROUP
    one_q = jnp.where(used & (lane % FOX_AUG_GROUP >= 3) & (lane % FOX_AUG_GROUP < 6), 1.0, 0.0)
    one_k = jnp.where(used & (lane % FOX_AUG_GROUP < 3), 1.0, 0.0)

    def step(i, carry):
        r0 = pl.multiple_of(i * blk, blk)
        lf = _log_sigmoid(s_ref[pl.ds(r0, blk), :] + fb_ref[...])
        hi, mid, lo = _split3(lf)
        cs = _dot(tri, hi) + _dot(tri, mid) + _dot(tri, lo) + carry
        pieces = jnp.concatenate(_split3(cs * LOG2E), axis=1)
        qa_ref[pl.ds(r0, blk), :] = (_dot(pieces, sel_q) + one_q).astype(BF16)
        ka_ref[pl.ds(r0, blk), :] = (_dot(pieces, sel_k) + one_k).astype(BF16)
        return cs[blk - 1:blk, :]

    lax.fori_loop(0, n, step, jnp.zeros((1, s_ref.shape[1]), F32))


def fox_gate(small, fb_row, *, batch, seq, col0, blk=256):
    shp = jax.ShapeDtypeStruct((batch * seq, FOX_AUG), BF16)
    spec = pl.BlockSpec((seq, FOX_AUG), lambda b: (b, 0))
    return pl.pallas_call(
        functools.partial(_fox_gate_body, blk=blk, col0=col0),
        grid=(batch,),
        in_specs=[pl.BlockSpec((seq, 128), lambda b: (b, 0)),
                  pl.BlockSpec((1, 128), lambda b: (0, 0))],
        out_specs=[spec, spec],
        out_shape=[shp, shp],
        compiler_params=_cparams(("parallel",)),
        name="fox_gate",
    )(small, fb_row)


def _fox_body(it_ref, jt_ref, q_ref, qa_ref, k_ref, ka_ref, v_ref, gate_ref, o_ref,
              m_ref, l_ref, acc_ref, *, tq, tk, hpb):
    p = pl.program_id(2)
    i = it_ref[p]
    j = jt_ref[p]
    d = FOX_DH

    @pl.when(j == 0)
    def _():
        m_ref[...] = jnp.full_like(m_ref, -jnp.inf)
        l_ref[...] = jnp.zeros_like(l_ref)
        acc_ref[...] = jnp.zeros_like(acc_ref)

    aug_lane = lax.broadcasted_iota(jnp.int32, (1, FOX_AUG), 1)

    def update(hh, masked):
        mine = aug_lane // FOX_AUG_GROUP == pl.program_id(1) * hpb + hh
        zero = jnp.zeros((), BF16)
        q_aug = jnp.concatenate([q_ref[:, hh * d:(hh + 1) * d], jnp.where(mine, qa_ref[...], zero)], axis=1)
        k_aug = jnp.concatenate([k_ref[:, hh * d:(hh + 1) * d], jnp.where(mine, ka_ref[...], zero)], axis=1)
        st = _dot_nt(k_aug, q_aug)
        if masked:
            kr = lax.broadcasted_iota(jnp.int32, (tk, tq), 0)
            qc = lax.broadcasted_iota(jnp.int32, (tk, tq), 1)
            st = jnp.where(qc >= kr, st, -jnp.inf)
        m_old = m_ref[hh]
        m_new = jnp.maximum(m_old, jnp.max(st, axis=0, keepdims=True))
        alpha = jnp.exp2(m_old - m_new)
        pt = jnp.exp2(st - m_new)
        l_ref[hh] = alpha * l_ref[hh] + jnp.sum(pt, axis=0, keepdims=True)
        acc_ref[hh] = alpha * acc_ref[hh] + _dot_tn(v_ref[:, hh * d:(hh + 1) * d], pt.astype(BF16))
        m_ref[hh] = m_new

    @pl.when(j < i)
    def _():
        for hh in range(hpb):
            update(hh, False)

    @pl.when(j == i)
    def _():
        for hh in range(hpb):
            update(hh, True)
            o = (acc_ref[hh] / l_ref[hh]).T
            g = gate_ref[:, hh * d:(hh + 1) * d].astype(F32)
            o_ref[:, hh * d:(hh + 1) * d] = (o * _silu(g)).astype(o_ref.dtype)


def fox(proj, qa, ka, *, batch, seq, q_col, k_col, v_col, gate_col, t=1024, hpb=4):
    tt = batch * seq
    nb = seq // t
    d, h = FOX_DH, FOX_HEADS
    w = hpb * d
    qb, kb, gb, vb = q_col // w, k_col // w, gate_col // w, v_col // w
    pairs = [(i, j) for i in range(nb) for j in range(i + 1)]
    it = jnp.array([p[0] for p in pairs], jnp.int32)
    jt = jnp.array([p[1] for p in pairs], jnp.int32)
    grid_spec = pltpu.PrefetchScalarGridSpec(
        num_scalar_prefetch=2,
        grid=(batch, h // hpb, len(pairs)),
        in_specs=[
            pl.BlockSpec((t, w), lambda b, hh, p, it, jt: (b * nb + it[p], qb + hh)),
            pl.BlockSpec((t, FOX_AUG), lambda b, hh, p, it, jt: (b * nb + it[p], 0)),
            pl.BlockSpec((t, w), lambda b, hh, p, it, jt: (b * nb + jt[p], kb + hh)),
            pl.BlockSpec((t, FOX_AUG), lambda b, hh, p, it, jt: (b * nb + jt[p], 0)),
            pl.BlockSpec((t, w), lambda b, hh, p, it, jt: (b * nb + jt[p], vb + hh)),
            pl.BlockSpec((t, w), lambda b, hh, p, it, jt: (b * nb + it[p], gb + hh)),
        ],
        out_specs=pl.BlockSpec((t, w), lambda b, hh, p, it, jt: (b * nb + it[p], hh)),
        scratch_shapes=[pltpu.VMEM((hpb, 1, t), F32), pltpu.VMEM((hpb, 1, t), F32), pltpu.VMEM((hpb, d, t), F32)],
    )
    return pl.pallas_call(
        functools.partial(_fox_body, tq=t, tk=t, hpb=hpb),
        grid_spec=grid_spec,
        out_shape=jax.ShapeDtypeStruct((tt, h * d), BF16),
        compiler_params=_cparams(("parallel", "parallel", "arbitrary")),
        name="fox",
    )(it, jt, proj, qa, proj, ka, proj, proj)


def _regroup_perm(tm, steps, to_grouped):
    r = lax.broadcasted_iota(jnp.int32, (tm, tm), 0)
    c = lax.broadcasted_iota(jnp.int32, (tm, tm), 1)
    nc = tm // steps
    src = (r % nc) * steps + r // nc if to_grouped else (r % steps) * nc + r // steps
    return jnp.where(c == src, 1.0, 0.0).astype(BF16)


def _cast_weight_once(wb_ref, w_ref):
    @pl.when(pl.program_id(0) == 0)
    def _():
        for r0 in range(0, w_ref.shape[0], MM_SUB):
            wb_ref[r0:r0 + MM_SUB, :] = w_ref[r0:r0 + MM_SUB, :].astype(wb_ref.dtype)


def _outproj_mid_body(a1_ref, a2_ref, wf_ref, x_ref, g_ref, r_ref, h_ref, w_ref, *, steps):
    k1 = a1_ref.shape[1]
    tm, d = x_ref.shape
    nc = tm // steps
    _cast_weight_once(w_ref, wf_ref)
    acc = x_ref[...] + _dot(a1_ref[...], w_ref[:k1, :]) + _dot(a2_ref[...], w_ref[k1:, :])
    r_ref[...] = acc
    normed = (acc * lax.rsqrt(jnp.mean(acc * acc, axis=-1, keepdims=True) + EPS) * g_ref[...]).astype(BF16)
    by_step = _dot(_regroup_perm(tm, steps, True), normed).astype(h_ref.dtype)
    for s in range(steps):
        h_ref[:, s * d:(s + 1) * d] = by_step[s * nc:(s + 1) * nc, :]


def outproj_mid(a1, a2, w, x, g, *, steps, tm=256, name="outproj_mid"):
    m, d = x.shape
    row = pl.BlockSpec((tm, d), lambda i: (i, 0))
    return pl.pallas_call(
        functools.partial(_outproj_mid_body, steps=steps),
        grid=(m // tm,),
        in_specs=[pl.BlockSpec((tm, a1.shape[1]), lambda i: (i, 0)),
                  pl.BlockSpec((tm, a2.shape[1]), lambda i: (i, 0)),
                  pl.BlockSpec(w.shape, lambda i: (0, 0), pipeline_mode=pl.Buffered(1)),
                  row,
                  pl.BlockSpec((1, d), lambda i: (0, 0))],
        out_specs=[row, pl.BlockSpec((tm // steps, steps * d), lambda i: (i, 0))],
        out_shape=[jax.ShapeDtypeStruct((m, d), F32), jax.ShapeDtypeStruct((m // steps, steps * d), BF16)],
        scratch_shapes=[pltpu.VMEM(w.shape, BF16)],
        compiler_params=_cparams(("arbitrary",)),
        name=name,
    )(a1, a2, w, x, g.reshape(1, d))


def _outproj_final_body(a_ref, wf_ref, x_ref, g_ref, o_ref, w_ref, *, steps):
    tm = x_ref.shape[0]
    e = w_ref.shape[0]
    by_step = jnp.concatenate([a_ref[:, s * e:(s + 1) * e] for s in range(steps)], axis=0)
    a_tok = _dot(_regroup_perm(tm, steps, False), by_step).astype(BF16)
    _cast_weight_once(w_ref, wf_ref)
    acc = x_ref[...] + _dot(a_tok, w_ref[...])
    o_ref[...] = acc * lax.rsqrt(jnp.mean(acc * acc, axis=-1, keepdims=True) + EPS) * g_ref[...]


def outproj_final(a_grouped, w, x, g, *, steps, tm=256, name="outproj_final"):
    m, d = x.shape
    e = w.shape[0]
    row = pl.BlockSpec((tm, d), lambda i: (i, 0))
    return pl.pallas_call(
        functools.partial(_outproj_final_body, steps=steps),
        grid=(m // tm,),
        in_specs=[pl.BlockSpec((tm // steps, steps * e), lambda i: (i, 0)),
                  pl.BlockSpec(w.shape, lambda i: (0, 0), pipeline_mode=pl.Buffered(1)),
                  row,
                  pl.BlockSpec((1, d), lambda i: (0, 0))],
        out_specs=row,
        out_shape=jax.ShapeDtypeStruct((m, d), F32),
        scratch_shapes=[pltpu.VMEM(w.shape, BF16)],
        compiler_params=_cparams(("arbitrary",)),
        name=name,
    )(a_grouped, w, x, g.reshape(1, d))


S5_GPB = 2


def _s5_group(u, wt, nt, mt, arow, *, cols_per_seq, n_levels):
    n = u.shape[1]
    p = S5_STATE
    nseq = n // cols_per_seq
    x = _dot(nt, u)

    def to_rows(part):
        return jnp.concatenate([part[:, s * cols_per_seq:(s + 1) * cols_per_seq] for s in range(nseq)], axis=0).T

    def to_cols(rows):
        rt = rows.T
        return jnp.concatenate([rt[s * p:(s + 1) * p, :] for s in range(nseq)], axis=1)

    xr, xi = to_rows(x[:p, :]), to_rows(x[p:, :])
    pos = lax.broadcasted_iota(jnp.int32, (cols_per_seq, 1), 0)
    for lvl in range(n_levels):
        sh = 1 << lvl
        keep = pos >= sh
        sr = jnp.where(keep, pltpu.roll(xr, sh, axis=0), 0.0)
        si = jnp.where(keep, pltpu.roll(xi, sh, axis=0), 0.0)
        ar, ai = arow[lvl:lvl + 1, :], arow[n_levels + lvl:n_levels + lvl + 1, :]
        xr, xi = xr + ar * sr - ai * si, xi + ar * si + ai * sr
    keep = pos >= 1
    pr = jnp.where(keep, pltpu.roll(xr, 1, axis=0), 0.0)
    pi = jnp.where(keep, pltpu.roll(xi, 1, axis=0), 0.0)
    x_prev = jnp.concatenate([to_cols(pr), to_cols(pi)], axis=0).astype(BF16)
    return _dot(wt, u) + _dot(mt, x_prev)


def _s5_body(u_ref, wt_ref, nt_ref, mt_ref, arow_ref, o_ref, *, cols_per_seq, n_levels):
    t, _, n = u_ref.shape
    cg = S5_GROUP_CH
    for gi in range(S5_GPB):
        ch = slice(gi * cg, (gi + 1) * cg)
        u = u_ref[:, ch, :].reshape(t * cg, n)
        y = _s5_group(u, wt_ref[gi], nt_ref[gi], mt_ref[gi], arow_ref[gi],
                      cols_per_seq=cols_per_seq, n_levels=n_levels)
        o_ref[:, ch, :] = y.astype(o_ref.dtype).reshape(t, cg, n)


def s5_apply(proj_t, wt_g, nt_g, mt_g, arow, *, groups, cols_per_seq, n_levels):
    t, _, n = proj_t.shape
    cg = S5_GROUP_CH * S5_GPB
    spec3 = lambda shp: pl.BlockSpec((S5_GPB,) + shp, lambda i: (i, 0, 0))
    return pl.pallas_call(
        functools.partial(_s5_body, cols_per_seq=cols_per_seq, n_levels=n_levels),
        grid=(groups // S5_GPB,),
        in_specs=[pl.BlockSpec((t, cg, n), lambda i: (0, i, 0)),
                  spec3(wt_g.shape[1:]), spec3(nt_g.shape[1:]), spec3(mt_g.shape[1:]), spec3(arow.shape[1:])],
        out_specs=pl.BlockSpec((t, cg, n), lambda i: (0, i, 0)),
        out_shape=jax.ShapeDtypeStruct((t, groups * S5_GROUP_CH, n), BF16),
        compiler_params=_cparams(("parallel",)),
        name="s5",
    )(proj_t, wt_g, nt_g, mt_g, arow)


def s5_operators(lam_re, lam_im, log_step, b_re, b_im, c_re, c_im, d, *, n_levels, n_seq):
    hp = lax.Precision.HIGHEST
    g, p = lam_re.shape
    cg, t = S5_GROUP_CH, S5_T
    lr = jnp.minimum(lam_re.astype(F32), -1e-4)
    li = lam_im.astype(F32)
    step = jnp.exp(log_step.astype(F32))[:, None]
    mag = jnp.exp(lr * step)
    lb_re, lb_im = mag * jnp.cos(li * step), mag * jnp.sin(li * step)
    den = lr * lr + li * li
    nr, ni = lb_re - 1.0, lb_im
    coef_re = (nr * lr + ni * li) / den
    coef_im = (ni * lr - nr * li) / den
    bb_re = coef_re[..., None] * b_re - coef_im[..., None] * b_im
    bb_im = coef_re[..., None] * b_im + coef_im[..., None] * b_re

    def power(m):
        mm = m.astype(F32)[None, :, None]
        mg = jnp.exp(lr[:, None, :] * step[:, None, :] * mm)
        ang = li[:, None, :] * step[:, None, :] * mm
        return mg * jnp.cos(ang), mg * jnp.sin(ang)

    pw_re, pw_im = power(jnp.arange(t + 1))
    pwt_re, pwt_im = pw_re.transpose(0, 2, 1), pw_im.transpose(0, 2, 1)

    def pow_times_b(pr, pi):
        re = pr[:, :, :, None] * bb_re[:, :, None, :] - pi[:, :, :, None] * bb_im[:, :, None, :]
        im = pr[:, :, :, None] * bb_im[:, :, None, :] + pi[:, :, :, None] * bb_re[:, :, None, :]
        return re.reshape(g, p, t * cg), im.reshape(g, p, t * cg)

    pb_re, pb_im = pow_times_b(pwt_re[:, :, :t], pwt_im[:, :, :t])
    taps = (jnp.einsum('gcp,gpy->gcy', c_re.astype(F32), pb_re, precision=hp)
            - jnp.einsum('gcp,gpy->gcy', c_im.astype(F32), pb_im, precision=hp))
    y_idx = jnp.arange(t * cg)
    taps = taps + d.astype(F32).reshape(g, cg, 1) * (y_idx[None, :] == jnp.arange(cg)[:, None]).astype(F32)[None]
    place = ((y_idx[None, :, None] // cg == jnp.arange(t)[:, None, None] - y_idx[None, None, :] // cg)
             & (y_idx[None, :, None] % cg == y_idx[None, None, :] % cg)).astype(BF16)
    wt_g = jnp.einsum('gcy,tyz->gtcz', taps.astype(BF16), place,
                      preferred_element_type=F32).astype(BF16).reshape(g, t * cg, t * cg)
    n_re, n_im = pow_times_b(pwt_re[:, :, t - 1::-1], pwt_im[:, :, t - 1::-1])
    nt_g = jnp.concatenate([n_re, n_im], axis=1).astype(BF16)
    cp_re = c_re[:, None] * pw_re[:, 1:, None, :] - c_im[:, None] * pw_im[:, 1:, None, :]
    cp_im = c_re[:, None] * pw_im[:, 1:, None, :] + c_im[:, None] * pw_re[:, 1:, None, :]
    mt_g = jnp.concatenate([cp_re, -cp_im], axis=-1).reshape(g, t * cg, 2 * p).astype(BF16)
    ar, ai = power(t * (2 ** jnp.arange(n_levels)))
    arow = jnp.concatenate([jnp.tile(ar, (1, 1, n_seq)), jnp.tile(ai, (1, 1, n_seq))], axis=1)
    return wt_g, nt_g, mt_g, arow


def _gelu_tanh(y):
    return 0.5 * y * (1.0 + jnp.tanh(math.sqrt(2.0 / math.pi) * (y + 0.044715 * (y * y * y))))


def _glu_body(y_ref, w_ref, b_ref, gate_ref, o_ref, z_ref, zb_ref, wt_ref):
    @pl.when((pl.program_id(0) == 0) & (pl.program_id(1) == 0))
    def _():
        _transpose_into(wt_ref, w_ref)

    z = _gelu_tanh(y_ref[...].astype(F32))
    z_ref[...] = z
    zb_ref[...] = z.astype(BF16)
    for f0 in range(0, o_ref.shape[1], MM_SUB):
        rows = slice(f0, f0 + MM_SUB)
        lin = _dot(wt_ref[rows, :], zb_ref[...]) + b_ref[rows, :]
        out = z_ref[rows, :] * _sigmoid(lin) * _silu(gate_ref[rows, :].astype(F32))
        o_ref[:, rows] = out.T.astype(o_ref.dtype)


def glu_gate(y_t, w, b, proj_t, *, gate_row, tm=512):
    s, e, n = y_t.shape
    gb = gate_row // e
    return pl.pallas_call(
        _glu_body,
        grid=(s, n // tm),
        in_specs=[pl.BlockSpec((None, e, tm), lambda t, i: (t, 0, i)),
                  pl.BlockSpec((e, e), lambda t, i: (0, 0), pipeline_mode=pl.Buffered(1)),
                  pl.BlockSpec((e, 1), lambda t, i: (0, 0), pipeline_mode=pl.Buffered(1)),
                  pl.BlockSpec((None, e, tm), lambda t, i: (t, gb, i))],
        out_specs=pl.BlockSpec((tm, e), lambda t, i: (i, t)),
        out_shape=jax.ShapeDtypeStruct((n, s * e), BF16),
        scratch_shapes=[pltpu.VMEM((e, tm), F32), pltpu.VMEM((e, tm), BF16), pltpu.VMEM((e, e), BF16)],
        compiler_params=_cparams(("arbitrary", "arbitrary")),
        name="glu",
    )(y_t, w, b.reshape(e, 1), proj_t)


def kernel(x, ab_norm_g, ab_w_in, gla_alpha_up, gla_alpha_b, gla_head_g, fox_f_b, ab_w_out, c_norm_g, c_w_in, s5_lambda_re, s5_lambda_im, s5_log_step, s5_b_re, s5_b_im, s5_c_re, s5_c_im, s5_d, glu_w, glu_b, c_w_out, final_norm_g):
    batch, seq, d = x.shape
    t = batch * seq
    x2 = x.reshape(t, d)

    hk, hv, fw = GLA_HEADS * GLA_DK, GLA_HEADS * GLA_DV, FOX_HEADS * FOX_DH
    sizes = (hk, hk, hv, GLA_LOWRANK, hv, fw, fw, fw, FOX_HEADS, fw)
    offs = [0]
    for s in sizes:
        offs.append(offs[-1] + s)
    w_main, w_small = ab_weight_prep(jnp.swapaxes(ab_w_in, 1, 2), offs)
    c_gq, c_gk, c_gv, c_gg = 0, hk, 2 * hk, 2 * hk + hv
    c_fq = c_gg + hv
    c_fk, c_fv, c_fg = c_fq + fw, c_fq + 2 * fw, c_fq + 3 * fw

    proj, small = norm_proj(x2, ab_norm_g[0], w_main, w_small, tn=w_main.shape[0] // 2, name="ab_in")

    o_gla = gla(proj, small, gla_alpha_up[0], gla_alpha_b[0], gla_head_g[0],
                batch=batch, seq=seq, q_col=c_gq, k_col=c_gk, v_col=c_gv, gate_col=c_gg)

    fb_row = jnp.zeros((1, 128), F32).at[0, GLA_LOWRANK:GLA_LOWRANK + FOX_HEADS].set(fox_f_b[0])
    qa, ka = fox_gate(small, fb_row, batch=batch, seq=seq, col0=GLA_LOWRANK)
    o_fox = fox(proj, qa, ka, batch=batch, seq=seq, q_col=c_fq, k_col=c_fk, v_col=c_fv, gate_col=c_fg)

    x1, h1g = outproj_mid(o_gla, o_fox, ab_w_out[0], x2, c_norm_g[0], steps=S5_T, name="ab_out")

    e = d
    groups = e // S5_GROUP_CH
    n_chunks = t // S5_T
    cols_per_seq = seq // S5_T
    n_levels = max(1, (cols_per_seq - 1).bit_length())
    proj_t = matmul_t_grouped(c_w_in[0], h1g, S5_T, BF16, name="c_in")
    ops = s5_operators(s5_lambda_re[0], s5_lambda_im[0], s5_log_step[0], s5_b_re[0], s5_b_im[0],
                       s5_c_re[0], s5_c_im[0], s5_d[0], n_levels=n_levels, n_seq=batch)
    y_t = s5_apply(proj_t, *ops, groups=groups, cols_per_seq=cols_per_seq, n_levels=n_levels)
    zz = glu_gate(y_t, glu_w[0], glu_b[0], proj_t, gate_row=e)
    out = outproj_final(zz, c_w_out[0], x1, final_norm_g, steps=S5_T, name="c_out")
    return out.reshape(batch, seq, d)
```

```python
import functools
import math

import jax
import jax.numpy as jnp
from jax import lax
from jax.experimental import pallas as pl
from jax.experimental.pallas import tpu as pltpu

EPS = 1e-6
F32 = jnp.float32
BF16 = jnp.bfloat16

GLA_HEADS = 4
GLA_DK = 128
GLA_DV = 256
GLA_LOWRANK = 16
GLA_TAU = 16.0
GLA_CHUNK = 64
GLA_SUB = 8
GLA_UNROLL = 2
FOX_HEADS = 8
FOX_DH = 128
FOX_AUG = 128
FOX_AUG_GROUP = 8
S5_GROUP_CH = 16
S5_STATE = 64
S5_T = 16

VMEM_LIMIT = 56 * 1024 * 1024

NT_DIMS = (((1,), (1,)), ((), ()))
TN_DIMS = (((0,), (0,)), ((), ()))
LOG2E = math.log2(math.e)


def _cparams(sem):
    return pltpu.CompilerParams(dimension_semantics=sem, vmem_limit_bytes=VMEM_LIMIT)


def _dot(a, b):
    return jnp.dot(a, b, preferred_element_type=F32)


def _dot_nt(a, b):
    return lax.dot_general(a, b, NT_DIMS, preferred_element_type=F32)


def _dot_tn(a, b):
    return lax.dot_general(a, b, TN_DIMS, preferred_element_type=F32)


def _log_sigmoid(z):
    return -(jnp.maximum(-z, 0.0) + jnp.log1p(jnp.exp(-jnp.abs(z))))


def _sigmoid(z):
    return 1.0 / (1.0 + jnp.exp(-z))


def _silu(z):
    return z * _sigmoid(z)


def _split3(x):
    hi = x.astype(BF16)
    r1 = x - hi.astype(F32)
    mid = r1.astype(BF16)
    lo = (r1 - mid.astype(F32)).astype(BF16)
    return hi, mid, lo


def _norm_proj_body(x_ref, g_ref, wt_ref, ws_ref, o_ref, s_ref):
    x = x_ref[...]
    ms = jnp.mean(x * x, axis=-1, keepdims=True)
    h = (x * lax.rsqrt(ms + EPS) * g_ref[...]).astype(BF16)
    for c0 in range(0, o_ref.shape[1], MM_SUB):
        o_ref[:, c0:c0 + MM_SUB] = _dot_nt(h, wt_ref[c0:c0 + MM_SUB, :]).astype(o_ref.dtype)
    s_ref[...] = _dot_nt(h, ws_ref[...])


def norm_proj(x, g, wt, wt_small, tm=512, tn=512, name="norm_proj"):
    m, d = x.shape
    n = wt.shape[0]
    ns = wt_small.shape[0]
    proj, small = pl.pallas_call(
        _norm_proj_body,
        grid=(n // tn, m // tm),
        in_specs=[pl.BlockSpec((tm, d), lambda j, i: (i, 0)),
                  pl.BlockSpec((1, d), lambda j, i: (0, 0)),
                  pl.BlockSpec((tn, d), lambda j, i: (j, 0)),
                  pl.BlockSpec((ns, d), lambda j, i: (0, 0))],
        out_specs=[pl.BlockSpec((tm, tn), lambda j, i: (i, j)),
                   pl.BlockSpec((None, tm, ns), lambda j, i: (j, i, 0))],
        out_shape=[jax.ShapeDtypeStruct((m, n), BF16), jax.ShapeDtypeStruct((n // tn, m, ns), F32)],
        compiler_params=_cparams(("arbitrary", "arbitrary")),
        name=name,
    )(x, g.reshape(1, d), wt, wt_small)
    return proj, small[0]


def _ab_wprep_body(w_ref, o_ref, s_ref, *, offs):
    w = w_ref[0]
    seg = lambda n: w[offs[n]:offs[n + 1], :]
    o_ref[...] = jnp.concatenate(
        [w[:offs[3], :], seg(4), seg(5) * (FOX_DH ** -0.5 * LOG2E), w[offs[6]:offs[8], :], seg(9)],
        axis=0).astype(o_ref.dtype)
    pad = s_ref.shape[0] - (offs[4] - offs[3]) - (offs[9] - offs[8])
    s_ref[...] = jnp.concatenate([seg(3), seg(8), jnp.zeros((pad, w.shape[1]), F32)], axis=0).astype(s_ref.dtype)


def ab_weight_prep(w_t3, offs, tk=256):
    _, n_in, d = w_t3.shape
    n_main = n_in - (offs[4] - offs[3]) - (offs[9] - offs[8])
    return pl.pallas_call(
        functools.partial(_ab_wprep_body, offs=tuple(offs)),
        grid=(d // tk,),
        in_specs=[pl.BlockSpec((1, n_in, tk), lambda i: (0, 0, i))],
        out_specs=[pl.BlockSpec((n_main, tk), lambda i: (0, i)), pl.BlockSpec((128, tk), lambda i: (0, i))],
        out_shape=[jax.ShapeDtypeStruct((n_main, d), BF16), jax.ShapeDtypeStruct((128, d), BF16)],
        compiler_params=_cparams(("parallel",)),
        name="ab_wprep",
    )(w_t3)


MM_SUB = 512


def _transpose_into(wt_ref, w_ref):
    for c0 in range(0, w_ref.shape[1], MM_SUB):
        wt_ref[c0:c0 + MM_SUB, :] = w_ref[:, c0:c0 + MM_SUB].astype(wt_ref.dtype).T


def _mm_t_body(w_ref, a_ref, o_ref, wt_ref):
    @pl.when((pl.program_id(1) == 0) & (pl.program_id(2) == 0))
    def _():
        _transpose_into(wt_ref, w_ref)

    a = a_ref[...]
    for r0 in range(0, o_ref.shape[0], MM_SUB):
        o_ref[r0:r0 + MM_SUB, :] = _dot_nt(wt_ref[r0:r0 + MM_SUB, :], a).astype(o_ref.dtype)


def matmul_t_grouped(w, a2, steps, out_dtype, tm=512, tn=2048, name="matmul_tg"):
    k, nf = w.shape
    n = a2.shape[0]
    return pl.pallas_call(
        _mm_t_body,
        grid=(nf // tn, steps, n // tm),
        in_specs=[pl.BlockSpec((k, tn), lambda j, t, i: (0, j), pipeline_mode=pl.Buffered(1)),
                  pl.BlockSpec((tm, k), lambda j, t, i: (i, t))],
        out_specs=pl.BlockSpec((None, tn, tm), lambda j, t, i: (t, j, i)),
        out_shape=jax.ShapeDtypeStruct((steps, nf, n), out_dtype),
        scratch_shapes=[pltpu.VMEM((tn, k), BF16)],
        compiler_params=_cparams(("arbitrary", "arbitrary", "arbitrary")),
        name=name,
    )(w, a2)


def _gla_body(q_ref, k_ref, v_ref, glr_ref, aup_ref, ab_ref, gate_ref, hg_ref, o_ref,
              s_ref, b_ref, kf_ref, kts_ref, dec_ref, *, tb, hpb):
    ib = pl.program_id(2)
    C, SB, dk, dv = GLA_CHUNK, GLA_SUB, GLA_DK, GLA_DV

    @pl.when(ib == 0)
    def _():
        s_ref[...] = jnp.zeros_like(s_ref)

    glr = glr_ref[:, :GLA_LOWRANK].astype(BF16)
    tw = 2 * C
    r = lax.broadcasted_iota(jnp.int32, (tw, tw), 0)
    c = lax.broadcasted_iota(jnp.int32, (tw, tw), 1)
    tri = jnp.where((r // C == c // C) & (c <= r), 1.0, 0.0).astype(BF16)
    nc = tb // C
    for hh in range(hpb):
        kc = slice(hh * dk, (hh + 1) * dk)
        z = _dot(glr, aup_ref[:, kc].astype(BF16)) + ab_ref[:, kc]
        la = _log_sigmoid(z) * (1.0 / GLA_TAU)
        hi = la.astype(BF16)
        lo = (la - hi.astype(F32)).astype(BF16)
        b_blk = jnp.concatenate(
            [_dot(tri, hi[r0:r0 + tw]) + _dot(tri, lo[r0:r0 + tw]) for r0 in range(0, tb, tw)], axis=0)
        k_blk = k_ref[:, kc].astype(F32)
        b_ref[hh] = b_blk
        kf_ref[hh] = k_blk
        b3 = b_blk.reshape(nc, C, dk)
        b_end = b3[:, C - 1:C, :]
        k_end = (k_blk.reshape(nc, C, dk) * jnp.exp(b_end - b3)).reshape(tb, dk)
        kt_end = k_end.T.astype(BF16)
        dec_t = jnp.exp(b_end.reshape(nc, dk)).T
        for ci in range(nc):
            kts_ref[hh, ci] = kt_end[:, ci * C:(ci + 1) * C]
            dec_ref[hh, ci] = jnp.broadcast_to(dec_t[:, ci:ci + 1], (dk, dv))

    row = lax.broadcasted_iota(jnp.int32, (C, 1), 0)
    rowi = lax.broadcasted_iota(jnp.int32, (C, C), 0)
    coli = lax.broadcasted_iota(jnp.int32, (C, C), 1)
    lane_c = lax.broadcasted_iota(jnp.int32, (SB, C), 1)
    sub_r = lax.broadcasted_iota(jnp.int32, (SB, 1), 0)
    ones = jnp.ones((GLA_DK, C), BF16)
    neg = -jnp.inf
    scale = GLA_DK ** -0.5

    def chunk_head(ci, r0, hh):
        kc = slice(hh * dk, (hh + 1) * dk)
        vc = slice(hh * dv, (hh + 1) * dv)
        q = q_ref[pl.ds(r0, C), kc].astype(F32) * scale
        k = kf_ref[hh, pl.ds(r0, C), :]
        b = b_ref[hh, pl.ds(r0, C), :]
        v = v_ref[pl.ds(r0, C), vc]

        s = s_ref[hh]
        o = _dot((q * jnp.exp(b)).astype(BF16), s.astype(BF16))
        s_ref[hh] = s * dec_ref[hh, ci] + _dot(kts_ref[hh, ci], v)

        attn = jnp.zeros((C, C), F32)
        h = C // 2
        while h >= SB:
            ref = jnp.broadcast_to(b.reshape(C // (2 * h), 2 * h, GLA_DK)[:, h - 1:h, :],
                                   (C // (2 * h), 2 * h, GLA_DK)).reshape(C, GLA_DK)
            upper = (row // h) % 2 == 1
            q_h = q * jnp.exp(jnp.where(upper, b - ref, neg))
            k_h = k * jnp.exp(jnp.where(upper, neg, ref - b))
            a_h = _dot_nt(q_h.astype(BF16), k_h.astype(BF16))
            attn = attn + (a_h if 2 * h == C else jnp.where(rowi // (2 * h) == coli // (2 * h), a_h, 0.0))
            h //= 2

        zs = []
        for bi in range(C // SB):
            s0 = bi * SB
            q_i = q[s0:s0 + SB, :]
            b_i = b[s0:s0 + SB, :]
            for j in range(SB):
                k_j = kf_ref[hh, pl.ds(r0 + (s0 + j), 1), :]
                b_j = b_ref[hh, pl.ds(r0 + (s0 + j), 1), :]
                zs.append(q_i * k_j * jnp.exp(jnp.where(sub_r >= j, b_i - b_j, neg)))
        zsum = _dot(jnp.concatenate(zs, axis=0).astype(BF16), ones)
        diag = []
        for bi in range(C // SB):
            acc = jnp.zeros((SB, C), F32)
            for j in range(SB):
                n0 = (bi * SB + j) * SB
                acc = acc + jnp.where(lane_c == bi * SB + j, zsum[n0:n0 + SB, :], 0.0)
            diag.append(acc)
        attn = attn + jnp.concatenate(diag, axis=0)
        o = o + _dot(attn.astype(BF16), v)

        o = o * lax.rsqrt(jnp.mean(o * o, axis=-1, keepdims=True) + EPS) * hg_ref[:, vc]
        g = gate_ref[pl.ds(r0, C), vc].astype(F32)
        o_ref[pl.ds(r0, C), vc] = (o * _silu(g)).astype(o_ref.dtype)

    def chunk(ci, carry):
        r0 = pl.multiple_of(ci * C, C)
        for hh in range(hpb):
            chunk_head(ci, r0, hh)
        return carry

    lax.fori_loop(0, tb // C, chunk, 0, unroll=GLA_UNROLL)


def gla(proj, small, alpha_up, alpha_b, head_g, *, batch, seq, q_col, k_col, v_col, gate_col, tb=512, hpb=4):
    t = batch * seq
    nb = seq // tb
    dk, dv, h = GLA_DK * hpb, GLA_DV * hpb, GLA_HEADS // hpb
    qb, kb, vb, gb = q_col // dk, k_col // dk, v_col // dv, gate_col // dv
    tok = lambda b, hh, i: b * nb + i
    nc = tb // GLA_CHUNK
    return pl.pallas_call(
        functools.partial(_gla_body, tb=tb, hpb=hpb),
        grid=(batch, h, nb),
        in_specs=[
            pl.BlockSpec((tb, dk), lambda b, hh, i: (tok(b, hh, i), qb + hh)),
            pl.BlockSpec((tb, dk), lambda b, hh, i: (tok(b, hh, i), kb + hh)),
            pl.BlockSpec((tb, dv), lambda b, hh, i: (tok(b, hh, i), vb + hh)),
            pl.BlockSpec((tb, 128), lambda b, hh, i: (tok(b, hh, i), 0)),
            pl.BlockSpec((GLA_LOWRANK, dk), lambda b, hh, i: (0, hh)),
            pl.BlockSpec((1, dk), lambda b, hh, i: (0, hh)),
            pl.BlockSpec((tb, dv), lambda b, hh, i: (tok(b, hh, i), gb + hh)),
            pl.BlockSpec((1, dv), lambda b, hh, i: (0, hh)),
        ],
        out_specs=pl.BlockSpec((tb, dv), lambda b, hh, i: (tok(b, hh, i), hh)),
        out_shape=jax.ShapeDtypeStruct((t, GLA_HEADS * GLA_DV), BF16),
        scratch_shapes=[pltpu.VMEM((hpb, GLA_DK, GLA_DV), F32), pltpu.VMEM((hpb, tb, GLA_DK), F32),
                        pltpu.VMEM((hpb, tb, GLA_DK), F32),
                        pltpu.VMEM((hpb, nc, GLA_DK, GLA_CHUNK), BF16),
                        pltpu.VMEM((hpb, nc, GLA_DK, GLA_DV), F32)],
        compiler_params=_cparams(("parallel", "parallel", "arbitrary")),
        name="gla",
    )(proj, proj, proj, small, alpha_up, alpha_b.reshape(1, -1), proj, head_g.reshape(1, -1))


def _fox_gate_body(s_ref, fb_ref, qa_ref, ka_ref, *, blk, col0):
    n = s_ref.shape[0] // blk
    r = lax.broadcasted_iota(jnp.int32, (blk, blk), 0)
    c = lax.broadcasted_iota(jnp.int32, (blk, blk), 1)
    tri = jnp.where(c <= r, 1.0, 0.0).astype(BF16)
    lanes = s_ref.shape[1]
    pr = lax.broadcasted_iota(jnp.int32, (3 * lanes, FOX_AUG), 0)
    pc = lax.broadcasted_iota(jnp.int32, (3 * lanes, FOX_AUG), 1)
    head, piece = pr % lanes - col0, pr // lanes
    is_head = (head >= 0) & (head < FOX_HEADS)
    sel_q = jnp.where(is_head & (pc == head * FOX_AUG_GROUP + piece), 1.0, 0.0).astype(BF16)
    sel_k = jnp.where(is_head & (pc == head * FOX_AUG_GROUP + 3 + piece), -1.0, 0.0).astype(BF16)
    lane = lax.broadcasted_iota(jnp.int32, (1, FOX_AUG), 1)
    used = lane < FOX_HEADS * FOX_AUG_GROUP
    one_q = jnp.where(used & (lane % FOX_AUG_GROUP >= 3) & (lane % FOX_AUG_GROUP < 6), 1.0, 0.0)
    one_k = jnp.where(used & (lane % FOX_AUG_GROUP < 3), 1.0, 0.0)

    def step(i, carry):
        r0 = pl.multiple_of(i * blk, blk)
        lf = _log_sigmoid(s_ref[pl.ds(r0, blk), :] + fb_ref[...])
        hi, mid, lo = _split3(lf)
        cs = _dot(tri, hi) + _dot(tri, mid) + _dot(tri, lo) + carry
        pieces = jnp.concatenate(_split3(cs * LOG2E), axis=1)
        qa_ref[pl.ds(r0, blk), :] = (_dot(pieces, sel_q) + one_q).astype(BF16)
        ka_ref[pl.ds(r0, blk), :] = (_dot(pieces, sel_k) + one_k).astype(BF16)
        return cs[blk - 1:blk, :]

    lax.fori_loop(0, n, step, jnp.zeros((1, s_ref.shape[1]), F32))


def fox_gate(small, fb_row, *, batch, seq, col0, blk=256):
    shp = jax.ShapeDtypeStruct((batch * seq, FOX_AUG), BF16)
    spec = pl.BlockSpec((seq, FOX_AUG), lambda b: (b, 0))
    return pl.pallas_call(
        functools.partial(_fox_gate_body, blk=blk, col0=col0),
        grid=(batch,),
        in_specs=[pl.BlockSpec((seq, 128), lambda b: (b, 0)),
                  pl.BlockSpec((1, 128), lambda b: (0, 0))],
        out_specs=[spec, spec],
        out_shape=[shp, shp],
        compiler_params=_cparams(("parallel",)),
        name="fox_gate",
    )(small, fb_row)


def _fox_body(it_ref, jt_ref, q_ref, qa_ref, k_ref, ka_ref, v_ref, gate_ref, o_ref,
              m_ref, l_ref, acc_ref, *, tq, tk, hpb):
    p = pl.program_id(2)
    i = it_ref[p]
    j = jt_ref[p]
    d = FOX_DH

    @pl.when(j == 0)
    def _():
        m_ref[...] = jnp.full_like(m_ref, -jnp.inf)
        l_ref[...] = jnp.zeros_like(l_ref)
        acc_ref[...] = jnp.zeros_like(acc_ref)

    aug_lane = lax.broadcasted_iota(jnp.int32, (1, FOX_AUG), 1)

    def update(hh, masked):
        mine = aug_lane // FOX_AUG_GROUP == pl.program_id(1) * hpb + hh
        zero = jnp.zeros((), BF16)
        q_aug = jnp.concatenate([q_ref[:, hh * d:(hh + 1) * d], jnp.where(mine, qa_ref[...], zero)], axis=1)
        k_aug = jnp.concatenate([k_ref[:, hh * d:(hh + 1) * d], jnp.where(mine, ka_ref[...], zero)], axis=1)
        st = _dot_nt(k_aug, q_aug)
        if masked:
            kr = lax.broadcasted_iota(jnp.int32, (tk, tq), 0)
            qc = lax.broadcasted_iota(jnp.int32, (tk, tq), 1)
            st = jnp.where(qc >= kr, st, -jnp.inf)
        m_old = m_ref[hh]
        m_new = jnp.maximum(m_old, jnp.max(st, axis=0, keepdims=True))
        alpha = jnp.exp2(m_old - m_new)
        pt = jnp.exp2(st - m_new)
        l_ref[hh] = alpha * l_ref[hh] + jnp.sum(pt, axis=0, keepdims=True)
        acc_ref[hh] = alpha * acc_ref[hh] + _dot_tn(v_ref[:, hh * d:(hh + 1) * d], pt.astype(BF16))
        m_ref[hh] = m_new

    @pl.when(j < i)
    def _():
        for hh in range(hpb):
            update(hh, False)

    @pl.when(j == i)
    def _():
        for hh in range(hpb):
            update(hh, True)
            o = (acc_ref[hh] / l_ref[hh]).T
            g = gate_ref[:, hh * d:(hh + 1) * d].astype(F32)
            o_ref[:, hh * d:(hh + 1) * d] = (o * _silu(g)).astype(o_ref.dtype)


def fox(proj, qa, ka, *, batch, seq, q_col, k_col, v_col, gate_col, t=1024, hpb=4):
    tt = batch * seq
    nb = seq // t
    d, h = FOX_DH, FOX_HEADS
    w = hpb * d
    qb, kb, gb, vb = q_col // w, k_col // w, gate_col // w, v_col // w
    pairs = [(i, j) for i in range(nb) for j in range(i + 1)]
    it = jnp.array([p[0] for p in pairs], jnp.int32)
    jt = jnp.array([p[1] for p in pairs], jnp.int32)
    grid_spec = pltpu.PrefetchScalarGridSpec(
        num_scalar_prefetch=2,
        grid=(batch, h // hpb, len(pairs)),
        in_specs=[
            pl.BlockSpec((t, w), lambda b, hh, p, it, jt: (b * nb + it[p], qb + hh)),
            pl.BlockSpec((t, FOX_AUG), lambda b, hh, p, it, jt: (b * nb + it[p], 0)),
            pl.BlockSpec((t, w), lambda b, hh, p, it, jt: (b * nb + jt[p], kb + hh)),
            pl.BlockSpec((t, FOX_AUG), lambda b, hh, p, it, jt: (b * nb + jt[p], 0)),
            pl.BlockSpec((t, w), lambda b, hh, p, it, jt: (b * nb + jt[p], vb + hh)),
            pl.BlockSpec((t, w), lambda b, hh, p, it, jt: (b * nb + it[p], gb + hh)),
        ],
        out_specs=pl.BlockSpec((t, w), lambda b, hh, p, it, jt: (b * nb + it[p], hh)),
        scratch_shapes=[pltpu.VMEM((hpb, 1, t), F32), pltpu.VMEM((hpb, 1, t), F32), pltpu.VMEM((hpb, d, t), F32)],
    )
    return pl.pallas_call(
        functools.partial(_fox_body, tq=t, tk=t, hpb=hpb),
        grid_spec=grid_spec,
        out_shape=jax.ShapeDtypeStruct((tt, h * d), BF16),
        compiler_params=_cparams(("parallel", "parallel", "arbitrary")),
        name="fox",
    )(it, jt, proj, qa, proj, ka, proj, proj)


def _regroup_perm(tm, steps, to_grouped):
    r = lax.broadcasted_iota(jnp.int32, (tm, tm), 0)
    c = lax.broadcasted_iota(jnp.int32, (tm, tm), 1)
    nc = tm // steps
    src = (r % nc) * steps + r // nc if to_grouped else (r % steps) * nc + r // steps
    return jnp.where(c == src, 1.0, 0.0).astype(BF16)


def _cast_weight_once(wb_ref, w_ref):
    @pl.when(pl.program_id(0) == 0)
    def _():
        for r0 in range(0, w_ref.shape[0], MM_SUB):
            wb_ref[r0:r0 + MM_SUB, :] = w_ref[r0:r0 + MM_SUB, :].astype(wb_ref.dtype)


def _outproj_mid_body(a1_ref, a2_ref, wf_ref, x_ref, g_ref, r_ref, h_ref, w_ref, *, steps):
    k1 = a1_ref.shape[1]
    tm, d = x_ref.shape
    nc = tm // steps
    _cast_weight_once(w_ref, wf_ref)
    acc = x_ref[...] + _dot(a1_ref[...], w_ref[:k1, :]) + _dot(a2_ref[...], w_ref[k1:, :])
    r_ref[...] = acc
    normed = (acc * lax.rsqrt(jnp.mean(acc * acc, axis=-1, keepdims=True) + EPS) * g_ref[...]).astype(BF16)
    by_step = _dot(_regroup_perm(tm, steps, True), normed).astype(h_ref.dtype)
    for s in range(steps):
        h_ref[:, s * d:(s + 1) * d] = by_step[s * nc:(s + 1) * nc, :]


def outproj_mid(a1, a2, w, x, g, *, steps, tm=256, name="outproj_mid"):
    m, d = x.shape
    row = pl.BlockSpec((tm, d), lambda i: (i, 0))
    return pl.pallas_call(
        functools.partial(_outproj_mid_body, steps=steps),
        grid=(m // tm,),
        in_specs=[pl.BlockSpec((tm, a1.shape[1]), lambda i: (i, 0)),
                  pl.BlockSpec((tm, a2.shape[1]), lambda i: (i, 0)),
                  pl.BlockSpec(w.shape, lambda i: (0, 0), pipeline_mode=pl.Buffered(1)),
                  row,
                  pl.BlockSpec((1, d), lambda i: (0, 0))],
        out_specs=[row, pl.BlockSpec((tm // steps, steps * d), lambda i: (i, 0))],
        out_shape=[jax.ShapeDtypeStruct((m, d), F32), jax.ShapeDtypeStruct((m // steps, steps * d), BF16)],
        scratch_shapes=[pltpu.VMEM(w.shape, BF16)],
        compiler_params=_cparams(("arbitrary",)),
        name=name,
    )(a1, a2, w, x, g.reshape(1, d))


def _outproj_final_body(a_ref, wf_ref, x_ref, g_ref, o_ref, w_ref, *, steps):
    tm = x_ref.shape[0]
    e = w_ref.shape[0]
    by_step = jnp.concatenate([a_ref[:, s * e:(s + 1) * e] for s in range(steps)], axis=0)
    a_tok = _dot(_regroup_perm(tm, steps, False), by_step).astype(BF16)
    _cast_weight_once(w_ref, wf_ref)
    acc = x_ref[...] + _dot(a_tok, w_ref[...])
    o_ref[...] = acc * lax.rsqrt(jnp.mean(acc * acc, axis=-1, keepdims=True) + EPS) * g_ref[...]


def outproj_final(a_grouped, w, x, g, *, steps, tm=256, name="outproj_final"):
    m, d = x.shape
    e = w.shape[0]
    row = pl.BlockSpec((tm, d), lambda i: (i, 0))
    return pl.pallas_call(
        functools.partial(_outproj_final_body, steps=steps),
        grid=(m // tm,),
        in_specs=[pl.BlockSpec((tm // steps, steps * e), lambda i: (i, 0)),
                  pl.BlockSpec(w.shape, lambda i: (0, 0), pipeline_mode=pl.Buffered(1)),
                  row,
                  pl.BlockSpec((1, d), lambda i: (0, 0))],
        out_specs=row,
        out_shape=jax.ShapeDtypeStruct((m, d), F32),
        scratch_shapes=[pltpu.VMEM(w.shape, BF16)],
        compiler_params=_cparams(("arbitrary",)),
        name=name,
    )(a_grouped, w, x, g.reshape(1, d))


S5_GPB = 2


def _s5_group(u, wt, nt, mt, arow, *, cols_per_seq, n_levels):
    n = u.shape[1]
    p = S5_STATE
    nseq = n // cols_per_seq
    x = _dot(nt, u)

    def to_rows(part):
        return jnp.concatenate([part[:, s * cols_per_seq:(s + 1) * cols_per_seq] for s in range(nseq)], axis=0).T

    def to_cols(rows):
        rt = rows.T
        return jnp.concatenate([rt[s * p:(s + 1) * p, :] for s in range(nseq)], axis=1)

    xr, xi = to_rows(x[:p, :]), to_rows(x[p:, :])
    pos = lax.broadcasted_iota(jnp.int32, (cols_per_seq, 1), 0)
    for lvl in range(n_levels):
        sh = 1 << lvl
        keep = pos >= sh
        sr = jnp.where(keep, pltpu.roll(xr, sh, axis=0), 0.0)
        si = jnp.where(keep, pltpu.roll(xi, sh, axis=0), 0.0)
        ar, ai = arow[lvl:lvl + 1, :], arow[n_levels + lvl:n_levels + lvl + 1, :]
        xr, xi = xr + ar * sr - ai * si, xi + ar * si + ai * sr
    keep = pos >= 1
    pr = jnp.where(keep, pltpu.roll(xr, 1, axis=0), 0.0)
    pi = jnp.where(keep, pltpu.roll(xi, 1, axis=0), 0.0)
    x_prev = jnp.concatenate([to_cols(pr), to_cols(pi)], axis=0).astype(BF16)
    return _dot(wt, u) + _dot(mt, x_prev)


def _s5_body(u_ref, wt_ref, nt_ref, mt_ref, arow_ref, o_ref, *, cols_per_seq, n_levels):
    t, _, n = u_ref.shape
    cg = S5_GROUP_CH
    for gi in range(S5_GPB):
        ch = slice(gi * cg, (gi + 1) * cg)
        u = u_ref[:, ch, :].reshape(t * cg, n)
        y = _s5_group(u, wt_ref[gi], nt_ref[gi], mt_ref[gi], arow_ref[gi],
                      cols_per_seq=cols_per_seq, n_levels=n_levels)
        o_ref[:, ch, :] = y.astype(o_ref.dtype).reshape(t, cg, n)


def s5_apply(proj_t, wt_g, nt_g, mt_g, arow, *, groups, cols_per_seq, n_levels):
    t, _, n = proj_t.shape
    cg = S5_GROUP_CH * S5_GPB
    spec3 = lambda shp: pl.BlockSpec((S5_GPB,) + shp, lambda i: (i, 0, 0))
    return pl.pallas_call(
        functools.partial(_s5_body, cols_per_seq=cols_per_seq, n_levels=n_levels),
        grid=(groups // S5_GPB,),
        in_specs=[pl.BlockSpec((t, cg, n), lambda i: (0, i, 0)),
                  spec3(wt_g.shape[1:]), spec3(nt_g.shape[1:]), spec3(mt_g.shape[1:]), spec3(arow.shape[1:])],
        out_specs=pl.BlockSpec((t, cg, n), lambda i: (0, i, 0)),
        out_shape=jax.ShapeDtypeStruct((t, groups * S5_GROUP_CH, n), BF16),
        compiler_params=_cparams(("parallel",)),
        name="s5",
    )(proj_t, wt_g, nt_g, mt_g, arow)


def s5_operators(lam_re, lam_im, log_step, b_re, b_im, c_re, c_im, d, *, n_levels, n_seq):
    hp = lax.Precision.HIGHEST
    g, p = lam_re.shape
    cg, t = S5_GROUP_CH, S5_T
    lr = jnp.minimum(lam_re.astype(F32), -1e-4)
    li = lam_im.astype(F32)
    step = jnp.exp(log_step.astype(F32))[:, None]
    mag = jnp.exp(lr * step)
    lb_re, lb_im = mag * jnp.cos(li * step), mag * jnp.sin(li * step)
    den = lr * lr + li * li
    nr, ni = lb_re - 1.0, lb_im
    coef_re = (nr * lr + ni * li) / den
    coef_im = (ni * lr - nr * li) / den
    bb_re = coef_re[..., None] * b_re - coef_im[..., None] * b_im
    bb_im = coef_re[..., None] * b_im + coef_im[..., None] * b_re

    def power(m):
        mm = m.astype(F32)[None, :, None]
        mg = jnp.exp(lr[:, None, :] * step[:, None, :] * mm)
        ang = li[:, None, :] * step[:, None, :] * mm
        return mg * jnp.cos(ang), mg * jnp.sin(ang)

    pw_re, pw_im = power(jnp.arange(t + 1))
    pwt_re, pwt_im = pw_re.transpose(0, 2, 1), pw_im.transpose(0, 2, 1)

    def pow_times_b(pr, pi):
        re = pr[:, :, :, None] * bb_re[:, :, None, :] - pi[:, :, :, None] * bb_im[:, :, None, :]
        im = pr[:, :, :, None] * bb_im[:, :, None, :] + pi[:, :, :, None] * bb_re[:, :, None, :]
        return re.reshape(g, p, t * cg), im.reshape(g, p, t * cg)

    pb_re, pb_im = pow_times_b(pwt_re[:, :, :t], pwt_im[:, :, :t])
    taps = (jnp.einsum('gcp,gpy->gcy', c_re.astype(F32), pb_re, precision=hp)
            - jnp.einsum('gcp,gpy->gcy', c_im.astype(F32), pb_im, precision=hp))
    y_idx = jnp.arange(t * cg)
    taps = taps + d.astype(F32).reshape(g, cg, 1) * (y_idx[None, :] == jnp.arange(cg)[:, None]).astype(F32)[None]
    place = ((y_idx[None, :, None] // cg == jnp.arange(t)[:, None, None] - y_idx[None, None, :] // cg)
             & (y_idx[None, :, None] % cg == y_idx[None, None, :] % cg)).astype(BF16)
    wt_g = jnp.einsum('gcy,tyz->gtcz', taps.astype(BF16), place,
                      preferred_element_type=F32).astype(BF16).reshape(g, t * cg, t * cg)
    n_re, n_im = pow_times_b(pwt_re[:, :, t - 1::-1], pwt_im[:, :, t - 1::-1])
    nt_g = jnp.concatenate([n_re, n_im], axis=1).astype(BF16)
    cp_re = c_re[:, None] * pw_re[:, 1:, None, :] - c_im[:, None] * pw_im[:, 1:, None, :]
    cp_im = c_re[:, None] * pw_im[:, 1:, None, :] + c_im[:, None] * pw_re[:, 1:, None, :]
    mt_g = jnp.concatenate([cp_re, -cp_im], axis=-1).reshape(g, t * cg, 2 * p).astype(BF16)
    ar, ai = power(t * (2 ** jnp.arange(n_levels)))
    arow = jnp.concatenate([jnp.tile(ar, (1, 1, n_seq)), jnp.tile(ai, (1, 1, n_seq))], axis=1)
    return wt_g, nt_g, mt_g, arow


def _gelu_tanh(y):
    return 0.5 * y * (1.0 + jnp.tanh(math.sqrt(2.0 / math.pi) * (y + 0.044715 * (y * y * y))))


def _glu_body(y_ref, w_ref, b_ref, gate_ref, o_ref, z_ref, zb_ref, wt_ref):
    @pl.when((pl.program_id(0) == 0) & (pl.program_id(1) == 0))
    def _():
        _transpose_into(wt_ref, w_ref)

    z = _gelu_tanh(y_ref[...].astype(F32))
    z_ref[...] = z
    zb_ref[...] = z.astype(BF16)
    for f0 in range(0, o_ref.shape[1], MM_SUB):
        rows = slice(f0, f0 + MM_SUB)
        lin = _dot(wt_ref[rows, :], zb_ref[...]) + b_ref[rows, :]
        out = z_ref[rows, :] * _sigmoid(lin) * _silu(gate_ref[rows, :].astype(F32))
        o_ref[:, rows] = out.T.astype(o_ref.dtype)


def glu_gate(y_t, w, b, proj_t, *, gate_row, tm=512):
    s, e, n = y_t.shape
    gb = gate_row // e
    return pl.pallas_call(
        _glu_body,
        grid=(s, n // tm),
        in_specs=[pl.BlockSpec((None, e, tm), lambda t, i: (t, 0, i)),
                  pl.BlockSpec((e, e), lambda t, i: (0, 0), pipeline_mode=pl.Buffered(1)),
                  pl.BlockSpec((e, 1), lambda t, i: (0, 0), pipeline_mode=pl.Buffered(1)),
                  pl.BlockSpec((None, e, tm), lambda t, i: (t, gb, i))],
        out_specs=pl.BlockSpec((tm, e), lambda t, i: (i, t)),
        out_shape=jax.ShapeDtypeStruct((n, s * e), BF16),
        scratch_shapes=[pltpu.VMEM((e, tm), F32), pltpu.VMEM((e, tm), BF16), pltpu.VMEM((e, e), BF16)],
        compiler_params=_cparams(("arbitrary", "arbitrary")),
        name="glu",
    )(y_t, w, b.reshape(e, 1), proj_t)


def kernel(x, ab_norm_g, ab_w_in, gla_alpha_up, gla_alpha_b, gla_head_g, fox_f_b, ab_w_out, c_norm_g, c_w_in, s5_lambda_re, s5_lambda_im, s5_log_step, s5_b_re, s5_b_im, s5_c_re, s5_c_im, s5_d, glu_w, glu_b, c_w_out, final_norm_g):
    batch, seq, d = x.shape
    t = batch * seq
    x2 = x.reshape(t, d)

    hk, hv, fw = GLA_HEADS * GLA_DK, GLA_HEADS * GLA_DV, FOX_HEADS * FOX_DH
    sizes = (hk, hk, hv, GLA_LOWRANK, hv, fw, fw, fw, FOX_HEADS, fw)
    offs = [0]
    for s in sizes:
        offs.append(offs[-1] + s)
    w_main, w_small = ab_weight_prep(jnp.swapaxes(ab_w_in, 1, 2), offs)
    c_gq, c_gk, c_gv, c_gg = 0, hk, 2 * hk, 2 * hk + hv
    c_fq = c_gg + hv
    c_fk, c_fv, c_fg = c_fq + fw, c_fq + 2 * fw, c_fq + 3 * fw

    proj, small = norm_proj(x2, ab_norm_g[0], w_main, w_small, tn=w_main.shape[0] // 2, name="ab_in")

    o_gla = gla(proj, small, gla_alpha_up[0], gla_alpha_b[0], gla_head_g[0],
                batch=batch, seq=seq, q_col=c_gq, k_col=c_gk, v_col=c_gv, gate_col=c_gg)

    fb_row = jnp.zeros((1, 128), F32).at[0, GLA_LOWRANK:GLA_LOWRANK + FOX_HEADS].set(fox_f_b[0])
    qa, ka = fox_gate(small, fb_row, batch=batch, seq=seq, col0=GLA_LOWRANK)
    o_fox = fox(proj, qa, ka, batch=batch, seq=seq, q_col=c_fq, k_col=c_fk, v_col=c_fv, gate_col=c_fg)

    x1, h1g = outproj_mid(o_gla, o_fox, ab_w_out[0], x2, c_norm_g[0], steps=S5_T, name="ab_out")

    e = d
    groups = e // S5_GROUP_CH
    n_chunks = t // S5_T
    cols_per_seq = seq // S5_T
    n_levels = max(1, (cols_per_seq - 1).bit_length())
    proj_t = matmul_t_grouped(c_w_in[0], h1g, S5_T, BF16, name="c_in")
    ops = s5_operators(s5_lambda_re[0], s5_lambda_im[0], s5_log_step[0], s5_b_re[0], s5_b_im[0],
                       s5_c_re[0], s5_c_im[0], s5_d[0], n_levels=n_levels, n_seq=batch)
    y_t = s5_apply(proj_t, *ops, groups=groups, cols_per_seq=cols_per_seq, n_levels=n_levels)
    zz = glu_gate(y_t, glu_w[0], glu_b[0], proj_t, gate_row=e)
    out = outproj_final(zz, c_w_out[0], x1, final_norm_g, steps=S5_T, name="c_out")
    return out.reshape(batch, seq, d)
```

```python
import functools
import math

import jax
import jax.numpy as jnp
from jax import lax
from jax.experimental import pallas as pl
from jax.experimental.pallas import tpu as pltpu

EPS = 1e-6
F32 = jnp.float32
BF16 = jnp.bfloat16

GLA_HEADS = 4
GLA_DK = 128
GLA_DV = 256
GLA_LOWRANK = 16
GLA_TAU = 16.0
GLA_CHUNK = 64
GLA_SUB = 8
GLA_UNROLL = 2
FOX_HEADS = 8
FOX_DH = 128
FOX_AUG = 128
FOX_AUG_GROUP = 8
S5_GROUP_CH = 16
S5_STATE = 64
S5_T = 16

VMEM_LIMIT = 56 * 1024 * 1024

NT_DIMS = (((1,), (1,)), ((), ()))
TN_DIMS = (((0,), (0,)), ((), ()))
LOG2E = math.log2(math.e)


def _cparams(sem):
    return pltpu.CompilerParams(dimension_semantics=sem, vmem_limit_bytes=VMEM_LIMIT)


def _dot(a, b):
    return jnp.dot(a, b, preferred_element_type=F32)


def _dot_nt(a, b):
    return lax.dot_general(a, b, NT_DIMS, preferred_element_type=F32)


def _dot_tn(a, b):
    return lax.dot_general(a, b, TN_DIMS, preferred_element_type=F32)


def _log_sigmoid(z):
    return -(jnp.maximum(-z, 0.0) + jnp.log1p(jnp.exp(-jnp.abs(z))))


def _sigmoid(z):
    return 1.0 / (1.0 + jnp.exp(-z))


def _silu(z):
    return z * _sigmoid(z)


def _split3(x):
    hi = x.astype(BF16)
    r1 = x - hi.astype(F32)
    mid = r1.astype(BF16)
    lo = (r1 - mid.astype(F32)).astype(BF16)
    return hi, mid, lo


def _norm_proj_body(x_ref, g_ref, wt_ref, ws_ref, o_ref, s_ref):
    x = x_ref[...]
    ms = jnp.mean(x * x, axis=-1, keepdims=True)
    h = (x * lax.rsqrt(ms + EPS) * g_ref[...]).astype(BF16)
    for c0 in range(0, o_ref.shape[1], MM_SUB):
        o_ref[:, c0:c0 + MM_SUB] = _dot_nt(h, wt_ref[c0:c0 + MM_SUB, :]).astype(o_ref.dtype)
    s_ref[...] = _dot_nt(h, ws_ref[...])


def norm_proj(x, g, wt, wt_small, tm=512, tn=512, name="norm_proj"):
    m, d = x.shape
    n = wt.shape[0]
    ns = wt_small.shape[0]
    proj, small = pl.pallas_call(
        _norm_proj_body,
        grid=(n // tn, m // tm),
        in_specs=[pl.BlockSpec((tm, d), lambda j, i: (i, 0)),
                  pl.BlockSpec((1, d), lambda j, i: (0, 0)),
                  pl.BlockSpec((tn, d), lambda j, i: (j, 0)),
                  pl.BlockSpec((ns, d), lambda j, i: (0, 0))],
        out_specs=[pl.BlockSpec((tm, tn), lambda j, i: (i, j)),
                   pl.BlockSpec((None, tm, ns), lambda j, i: (j, i, 0))],
        out_shape=[jax.ShapeDtypeStruct((m, n), BF16), jax.ShapeDtypeStruct((n // tn, m, ns), F32)],
        compiler_params=_cparams(("arbitrary", "arbitrary")),
        name=name,
    )(x, g.reshape(1, d), wt, wt_small)
    return proj, small[0]


def _ab_wprep_body(w_ref, o_ref, s_ref, *, offs):
    w = w_ref[0]
    seg = lambda n: w[offs[n]:offs[n + 1], :]
    o_ref[...] = jnp.concatenate(
        [w[:offs[3], :], seg(4), seg(5) * (FOX_DH ** -0.5 * LOG2E), w[offs[6]:offs[8], :], seg(9)],
        axis=0).astype(o_ref.dtype)
    pad = s_ref.shape[0] - (offs[4] - offs[3]) - (offs[9] - offs[8])
    s_ref[...] = jnp.concatenate([seg(3), seg(8), jnp.zeros((pad, w.shape[1]), F32)], axis=0).astype(s_ref.dtype)


def ab_weight_prep(w_t3, offs, tk=256):
    _, n_in, d = w_t3.shape
    n_main = n_in - (offs[4] - offs[3]) - (offs[9] - offs[8])
    return pl.pallas_call(
        functools.partial(_ab_wprep_body, offs=tuple(offs)),
        grid=(d // tk,),
        in_specs=[pl.BlockSpec((1, n_in, tk), lambda i: (0, 0, i))],
        out_specs=[pl.BlockSpec((n_main, tk), lambda i: (0, i)), pl.BlockSpec((128, tk), lambda i: (0, i))],
        out_shape=[jax.ShapeDtypeStruct((n_main, d), BF16), jax.ShapeDtypeStruct((128, d), BF16)],
        compiler_params=_cparams(("parallel",)),
        name="ab_wprep",
    )(w_t3)


MM_SUB = 512


def _transpose_into(wt_ref, w_ref):
    for c0 in range(0, w_ref.shape[1], MM_SUB):
        wt_ref[c0:c0 + MM_SUB, :] = w_ref[:, c0:c0 + MM_SUB].astype(wt_ref.dtype).T


def _mm_t_body(w_ref, a_ref, o_ref, wt_ref):
    @pl.when((pl.program_id(1) == 0) & (pl.program_id(2) == 0))
    def _():
        _transpose_into(wt_ref, w_ref)

    a = a_ref[...]
    for r0 in range(0, o_ref.shape[0], MM_SUB):
        o_ref[r0:r0 + MM_SUB, :] = _dot_nt(wt_ref[r0:r0 + MM_SUB, :], a).astype(o_ref.dtype)


def matmul_t_grouped(w, a2, steps, out_dtype, tm=512, tn=2048, name="matmul_tg"):
    k, nf = w.shape
    n = a2.shape[0]
    return pl.pallas_call(
        _mm_t_body,
        grid=(nf // tn, steps, n // tm),
        in_specs=[pl.BlockSpec((k, tn), lambda j, t, i: (0, j), pipeline_mode=pl.Buffered(1)),
                  pl.BlockSpec((tm, k), lambda j, t, i: (i, t))],
        out_specs=pl.BlockSpec((None, tn, tm), lambda j, t, i: (t, j, i)),
        out_shape=jax.ShapeDtypeStruct((steps, nf, n), out_dtype),
        scratch_shapes=[pltpu.VMEM((tn, k), BF16)],
        compiler_params=_cparams(("arbitrary", "arbitrary", "arbitrary")),
        name=name,
    )(w, a2)


def _gla_body(q_ref, k_ref, v_ref, glr_ref, aup_ref, ab_ref, gate_ref, hg_ref, o_ref,
              s_ref, b_ref, kf_ref, kts_ref, dec_ref, *, tb, hpb):
    ib = pl.program_id(2)
    C, SB, dk, dv = GLA_CHUNK, GLA_SUB, GLA_DK, GLA_DV

    @pl.when(ib == 0)
    def _():
        s_ref[...] = jnp.zeros_like(s_ref)

    glr = glr_ref[:, :GLA_LOWRANK].astype(BF16)
    tw = 2 * C
    r = lax.broadcasted_iota(jnp.int32, (tw, tw), 0)
    c = lax.broadcasted_iota(jnp.int32, (tw, tw), 1)
    tri = jnp.where((r // C == c // C) & (c <= r), 1.0, 0.0).astype(BF16)
    nc = tb // C
    for hh in range(hpb):
        kc = slice(hh * dk, (hh + 1) * dk)
        z = _dot(glr, aup_ref[:, kc].astype(BF16)) + ab_ref[:, kc]
        la = _log_sigmoid(z) * (1.0 / GLA_TAU)
        hi = la.astype(BF16)
        lo = (la - hi.astype(F32)).astype(BF16)
        b_blk = jnp.concatenate(
            [_dot(tri, hi[r0:r0 + tw]) + _dot(tri, lo[r0:r0 + tw]) for r0 in range(0, tb, tw)], axis=0)
        k_blk = k_ref[:, kc].astype(F32)
        b_ref[hh] = b_blk
        kf_ref[hh] = k_blk
        b3 = b_blk.reshape(nc, C, dk)
        b_end = b3[:, C - 1:C, :]
        k_end = (k_blk.reshape(nc, C, dk) * jnp.exp(b_end - b3)).reshape(tb, dk)
        kt_end = k_end.T.astype(BF16)
        dec_t = jnp.exp(b_end.reshape(nc, dk)).T
        for ci in range(nc):
            kts_ref[hh, ci] = kt_end[:, ci * C:(ci + 1) * C]
            dec_ref[hh, ci] = jnp.broadcast_to(dec_t[:, ci:ci + 1], (dk, dv))

    row = lax.broadcasted_iota(jnp.int32, (C, 1), 0)
    rowi = lax.broadcasted_iota(jnp.int32, (C, C), 0)
    coli = lax.broadcasted_iota(jnp.int32, (C, C), 1)
    lane_c = lax.broadcasted_iota(jnp.int32, (SB, C), 1)
    sub_r = lax.broadcasted_iota(jnp.int32, (SB, 1), 0)
    ones = jnp.ones((GLA_DK, C), BF16)
    neg = -jnp.inf
    scale = GLA_DK ** -0.5

    def chunk_head(ci, r0, hh):
        kc = slice(hh * dk, (hh + 1) * dk)
        vc = slice(hh * dv, (hh + 1) * dv)
        q = q_ref[pl.ds(r0, C), kc].astype(F32) * scale
        k = kf_ref[hh, pl.ds(r0, C), :]
        b = b_ref[hh, pl.ds(r0, C), :]
        v = v_ref[pl.ds(r0, C), vc]

        s = s_ref[hh]
        o = _dot((q * jnp.exp(b)).astype(BF16), s.astype(BF16))
        s_ref[hh] = s * dec_ref[hh, ci] + _dot(kts_ref[hh, ci], v)

        attn = jnp.zeros((C, C), F32)
        h = C // 2
        while h >= SB:
            ref = jnp.broadcast_to(b.reshape(C // (2 * h), 2 * h, GLA_DK)[:, h - 1:h, :],
                                   (C // (2 * h), 2 * h, GLA_DK)).reshape(C, GLA_DK)
            upper = (row // h) % 2 == 1
            q_h = q * jnp.exp(jnp.where(upper, b - ref, neg))
            k_h = k * jnp.exp(jnp.where(upper, neg, ref - b))
            a_h = _dot_nt(q_h.astype(BF16), k_h.astype(BF16))
            attn = attn + (a_h if 2 * h == C else jnp.where(rowi // (2 * h) == coli // (2 * h), a_h, 0.0))
            h //= 2

        zs = []
        for bi in range(C // SB):
            s0 = bi * SB
            q_i = q[s0:s0 + SB, :]
            b_i = b[s0:s0 + SB, :]
            for j in range(SB):
                k_j = kf_ref[hh, pl.ds(r0 + (s0 + j), 1), :]
                b_j = b_ref[hh, pl.ds(r0 + (s0 + j), 1), :]
                zs.append(q_i * k_j * jnp.exp(jnp.where(sub_r >= j, b_i - b_j, neg)))
        zsum = _dot(jnp.concatenate(zs, axis=0).astype(BF16), ones)
        diag = []
        for bi in range(C // SB):
            acc = jnp.zeros((SB, C), F32)
            for j in range(SB):
                n0 = (bi * SB + j) * SB
                acc = acc + jnp.where(lane_c == bi * SB + j, zsum[n0:n0 + SB, :], 0.0)
            diag.append(acc)
        attn = attn + jnp.concatenate(diag, axis=0)
        o = o + _dot(attn.astype(BF16), v)

        o = o * lax.rsqrt(jnp.mean(o * o, axis=-1, keepdims=True) + EPS) * hg_ref[:, vc]
        g = gate_ref[pl.ds(r0, C), vc].astype(F32)
        o_ref[pl.ds(r0, C), vc] = (o * _silu(g)).astype(o_ref.dtype)

    def chunk(ci, carry):
        r0 = pl.multiple_of(ci * C, C)
        for hh in range(hpb):
            chunk_head(ci, r0, hh)
        return carry

    lax.fori_loop(0, tb // C, chunk, 0, unroll=GLA_UNROLL)


def gla(proj, small, alpha_up, alpha_b, head_g, *, batch, seq, q_col, k_col, v_col, gate_col, tb=512, hpb=4):
    t = batch * seq
    nb = seq // tb
    dk, dv, h = GLA_DK * hpb, GLA_DV * hpb, GLA_HEADS // hpb
    qb, kb, vb, gb = q_col // dk, k_col // dk, v_col // dv, gate_col // dv
    tok = lambda b, hh, i: b * nb + i
    nc = tb // GLA_CHUNK
    return pl.pallas_call(
        functools.partial(_gla_body, tb=tb, hpb=hpb),
        grid=(batch, h, nb),
        in_specs=[
            pl.BlockSpec((tb, dk), lambda b, hh, i: (tok(b, hh, i), qb + hh)),
            pl.BlockSpec((tb, dk), lambda b, hh, i: (tok(b, hh, i), kb + hh)),
            pl.BlockSpec((tb, dv), lambda b, hh, i: (tok(b, hh, i), vb + hh)),
            pl.BlockSpec((tb, 128), lambda b, hh, i: (tok(b, hh, i), 0)),
            pl.BlockSpec((GLA_LOWRANK, dk), lambda b, hh, i: (0, hh)),
            pl.BlockSpec((1, dk), lambda b, hh, i: (0, hh)),
            pl.BlockSpec((tb, dv), lambda b, hh, i: (tok(b, hh, i), gb + hh)),
            pl.BlockSpec((1, dv), lambda b, hh, i: (0, hh)),
        ],
        out_specs=pl.BlockSpec((tb, dv), lambda b, hh, i: (tok(b, hh, i), hh)),
        out_shape=jax.ShapeDtypeStruct((t, GLA_HEADS * GLA_DV), BF16),
        scratch_shapes=[pltpu.VMEM((hpb, GLA_DK, GLA_DV), F32), pltpu.VMEM((hpb, tb, GLA_DK), F32),
                        pltpu.VMEM((hpb, tb, GLA_DK), F32),
                        pltpu.VMEM((hpb, nc, GLA_DK, GLA_CHUNK), BF16),
                        pltpu.VMEM((hpb, nc, GLA_DK, GLA_DV), F32)],
        compiler_params=_cparams(("parallel", "parallel", "arbitrary")),
        name="gla",
    )(proj, proj, proj, small, alpha_up, alpha_b.reshape(1, -1), proj, head_g.reshape(1, -1))


def _fox_gate_body(s_ref, fb_ref, qa_ref, ka_ref, *, blk, col0):
    n = s_ref.shape[0] // blk
    r = lax.broadcasted_iota(jnp.int32, (blk, blk), 0)
    c = lax.broadcasted_iota(jnp.int32, (blk, blk), 1)
    tri = jnp.where(c <= r, 1.0, 0.0).astype(BF16)
    lanes = s_ref.shape[1]
    pr = lax.broadcasted_iota(jnp.int32, (3 * lanes, FOX_AUG), 0)
    pc = lax.broadcasted_iota(jnp.int32, (3 * lanes, FOX_AUG), 1)
    head, piece = pr % lanes - col0, pr // lanes
    is_head = (head >= 0) & (head < FOX_HEADS)
    sel_q = jnp.where(is_head & (pc == head * FOX_AUG_GROUP + piece), 1.0, 0.0).astype(BF16)
    sel_k = jnp.where(is_head & (pc == head * FOX_AUG_GROUP + 3 + piece), -1.0, 0.0).astype(BF16)
    lane = lax.broadcasted_iota(jnp.int32, (1, FOX_AUG), 1)
    used = lane < FOX_HEADS * FOX_AUG_GROUP
    one_q = jnp.where(used & (lane % FOX_AUG_GROUP >= 3) & (lane % FOX_AUG_GROUP < 6), 1.0, 0.0)
    one_k = jnp.where(used & (lane % FOX_AUG_GROUP < 3), 1.0, 0.0)

    def step(i, carry):
        r0 = pl.multiple_of(i * blk, blk)
        lf = _log_sigmoid(s_ref[pl.ds(r0, blk), :] + fb_ref[...])
        hi, mid, lo = _split3(lf)
        cs = _dot(tri, hi) + _dot(tri, mid) + _dot(tri, lo) + carry
        pieces = jnp.concatenate(_split3(cs * LOG2E), axis=1)
        qa_ref[pl.ds(r0, blk), :] = (_dot(pieces, sel_q) + one_q).astype(BF16)
        ka_ref[pl.ds(r0, blk), :] = (_dot(pieces, sel_k) + one_k).astype(BF16)
        return cs[blk - 1:blk, :]

    lax.fori_loop(0, n, step, jnp.zeros((1, s_ref.shape[1]), F32))


def fox_gate(small, fb_row, *, batch, seq, col0, blk=256):
    shp = jax.ShapeDtypeStruct((batch * seq, FOX_AUG), BF16)
    spec = pl.BlockSpec((seq, FOX_AUG), lambda b: (b, 0))
    return pl.pallas_call(
        functools.partial(_fox_gate_body, blk=blk, col0=col0),
        grid=(batch,),
        in_specs=[pl.BlockSpec((seq, 128), lambda b: (b, 0)),
                  pl.BlockSpec((1, 128), lambda b: (0, 0))],
        out_specs=[spec, spec],
        out_shape=[shp, shp],
        compiler_params=_cparams(("parallel",)),
        name="fox_gate",
    )(small, fb_row)


def _fox_body(it_ref, jt_ref, q_ref, qa_ref, k_ref, ka_ref, v_ref, gate_ref, o_ref,
              m_ref, l_ref, acc_ref, *, tq, tk, hpb):
    p = pl.program_id(2)
    i = it_ref[p]
    j = jt_ref[p]
    d = FOX_DH

    @pl.when(j == 0)
    def _():
        m_ref[...] = jnp.full_like(m_ref, -jnp.inf)
        l_ref[...] = jnp.zeros_like(l_ref)
        acc_ref[...] = jnp.zeros_like(acc_ref)

    aug_lane = lax.broadcasted_iota(jnp.int32, (1, FOX_AUG), 1)

    def scores(hh, masked):
        mine = aug_lane // FOX_AUG_GROUP == pl.program_id(1) * hpb + hh
        zero = jnp.zeros((), BF16)
        q_aug = jnp.concatenate([q_ref[:, hh * d:(hh + 1) * d], jnp.where(mine, qa_ref[...], zero)], axis=1)
        k_aug = jnp.concatenate([k_ref[:, hh * d:(hh + 1) * d], jnp.where(mine, ka_ref[...], zero)], axis=1)
        st = _dot_nt(k_aug, q_aug)
        if masked:
            kr = lax.broadcasted_iota(jnp.int32, (tk, tq), 0)
            qc = lax.broadcasted_iota(jnp.int32, (tk, tq), 1)
            st = jnp.where(qc >= kr, st, -jnp.inf)
        return st

    def absorb(hh, st):
        m_old = m_ref[hh]
        m_new = jnp.maximum(m_old, jnp.max(st, axis=0, keepdims=True))
        alpha = jnp.exp2(m_old - m_new)
        pt = jnp.exp2(st - m_new)
        l_ref[hh] = alpha * l_ref[hh] + jnp.sum(pt, axis=0, keepdims=True)
        acc_ref[hh] = alpha * acc_ref[hh] + _dot_tn(v_ref[:, hh * d:(hh + 1) * d], pt.astype(BF16))
        m_ref[hh] = m_new

    def block(masked, finish):
        st = scores(0, masked)
        for hh in range(hpb):
            st_next = scores(hh + 1, masked) if hh + 1 < hpb else None
            absorb(hh, st)
            if finish:
                o = (acc_ref[hh] / l_ref[hh]).T
                g = gate_ref[:, hh * d:(hh + 1) * d].astype(F32)
                o_ref[:, hh * d:(hh + 1) * d] = (o * _silu(g)).astype(o_ref.dtype)
            st = st_next

    @pl.when(j < i)
    def _():
        block(False, False)

    @pl.when(j == i)
    def _():
        block(True, True)


def fox(proj, qa, ka, *, batch, seq, q_col, k_col, v_col, gate_col, t=1024, hpb=4):
    tt = batch * seq
    nb = seq // t
    d, h = FOX_DH, FOX_HEADS
    w = hpb * d
    qb, kb, gb, vb = q_col // w, k_col // w, gate_col // w, v_col // w
    pairs = [(i, j) for i in range(nb) for j in range(i + 1)]
    it = jnp.array([p[0] for p in pairs], jnp.int32)
    jt = jnp.array([p[1] for p in pairs], jnp.int32)
    grid_spec = pltpu.PrefetchScalarGridSpec(
        num_scalar_prefetch=2,
        grid=(batch, h // hpb, len(pairs)),
        in_specs=[
            pl.BlockSpec((t, w), lambda b, hh, p, it, jt: (b * nb + it[p], qb + hh)),
            pl.BlockSpec((t, FOX_AUG), lambda b, hh, p, it, jt: (b * nb + it[p], 0)),
            pl.BlockSpec((t, w), lambda b, hh, p, it, jt: (b * nb + jt[p], kb + hh)),
            pl.BlockSpec((t, FOX_AUG), lambda b, hh, p, it, jt: (b * nb + jt[p], 0)),
            pl.BlockSpec((t, w), lambda b, hh, p, it, jt: (b * nb + jt[p], vb + hh)),
            pl.BlockSpec((t, w), lambda b, hh, p, it, jt: (b * nb + it[p], gb + hh)),
        ],
        out_specs=pl.BlockSpec((t, w), lambda b, hh, p, it, jt: (b * nb + it[p], hh)),
        scratch_shapes=[pltpu.VMEM((hpb, 1, t), F32), pltpu.VMEM((hpb, 1, t), F32), pltpu.VMEM((hpb, d, t), F32)],
    )
    return pl.pallas_call(
        functools.partial(_fox_body, tq=t, tk=t, hpb=hpb),
        grid_spec=grid_spec,
        out_shape=jax.ShapeDtypeStruct((tt, h * d), BF16),
        compiler_params=_cparams(("parallel", "parallel", "arbitrary")),
        name="fox",
    )(it, jt, proj, qa, proj, ka, proj, proj)


def _regroup_perm(tm, steps, to_grouped):
    r = lax.broadcasted_iota(jnp.int32, (tm, tm), 0)
    c = lax.broadcasted_iota(jnp.int32, (tm, tm), 1)
    nc = tm // steps
    src = (r % nc) * steps + r // nc if to_grouped else (r % steps) * nc + r // steps
    return jnp.where(c == src, 1.0, 0.0).astype(BF16)


def _cast_weight_once(wb_ref, w_ref):
    @pl.when(pl.program_id(0) == 0)
    def _():
        for r0 in range(0, w_ref.shape[0], MM_SUB):
            wb_ref[r0:r0 + MM_SUB, :] = w_ref[r0:r0 + MM_SUB, :].astype(wb_ref.dtype)


def _outproj_mid_body(a1_ref, a2_ref, wf_ref, x_ref, g_ref, r_ref, h_ref, w_ref, *, steps):
    k1 = a1_ref.shape[1]
    tm, d = x_ref.shape
    nc = tm // steps
    _cast_weight_once(w_ref, wf_ref)
    acc = x_ref[...] + _dot(a1_ref[...], w_ref[:k1, :]) + _dot(a2_ref[...], w_ref[k1:, :])
    r_ref[...] = acc
    normed = (acc * lax.rsqrt(jnp.mean(acc * acc, axis=-1, keepdims=True) + EPS) * g_ref[...]).astype(BF16)
    by_step = _dot(_regroup_perm(tm, steps, True), normed).astype(h_ref.dtype)
    for s in range(steps):
        h_ref[:, s * d:(s + 1) * d] = by_step[s * nc:(s + 1) * nc, :]


def outproj_mid(a1, a2, w, x, g, *, steps, tm=256, name="outproj_mid"):
    m, d = x.shape
    row = pl.BlockSpec((tm, d), lambda i: (i, 0))
    return pl.pallas_call(
        functools.partial(_outproj_mid_body, steps=steps),
        grid=(m // tm,),
        in_specs=[pl.BlockSpec((tm, a1.shape[1]), lambda i: (i, 0)),
                  pl.BlockSpec((tm, a2.shape[1]), lambda i: (i, 0)),
                  pl.BlockSpec(w.shape, lambda i: (0, 0), pipeline_mode=pl.Buffered(1)),
                  row,
                  pl.BlockSpec((1, d), lambda i: (0, 0))],
        out_specs=[row, pl.BlockSpec((tm // steps, steps * d), lambda i: (i, 0))],
        out_shape=[jax.ShapeDtypeStruct((m, d), F32), jax.ShapeDtypeStruct((m // steps, steps * d), BF16)],
        scratch_shapes=[pltpu.VMEM(w.shape, BF16)],
        compiler_params=_cparams(("arbitrary",)),
        name=name,
    )(a1, a2, w, x, g.reshape(1, d))


def _outproj_final_body(a_ref, wf_ref, x_ref, g_ref, o_ref, w_ref, *, steps):
    tm = x_ref.shape[0]
    e = w_ref.shape[0]
    by_step = jnp.concatenate([a_ref[:, s * e:(s + 1) * e] for s in range(steps)], axis=0)
    a_tok = _dot(_regroup_perm(tm, steps, False), by_step).astype(BF16)
    _cast_weight_once(w_ref, wf_ref)
    acc = x_ref[...] + _dot(a_tok, w_ref[...])
    o_ref[...] = acc * lax.rsqrt(jnp.mean(acc * acc, axis=-1, keepdims=True) + EPS) * g_ref[...]


def outproj_final(a_grouped, w, x, g, *, steps, tm=256, name="outproj_final"):
    m, d = x.shape
    e = w.shape[0]
    row = pl.BlockSpec((tm, d), lambda i: (i, 0))
    return pl.pallas_call(
        functools.partial(_outproj_final_body, steps=steps),
        grid=(m // tm,),
        in_specs=[pl.BlockSpec((tm // steps, steps * e), lambda i: (i, 0)),
                  pl.BlockSpec(w.shape, lambda i: (0, 0), pipeline_mode=pl.Buffered(1)),
                  row,
                  pl.BlockSpec((1, d), lambda i: (0, 0))],
        out_specs=row,
        out_shape=jax.ShapeDtypeStruct((m, d), F32),
        scratch_shapes=[pltpu.VMEM(w.shape, BF16)],
        compiler_params=_cparams(("arbitrary",)),
        name=name,
    )(a_grouped, w, x, g.reshape(1, d))


S5_GPB = 2


def _s5_group(u, wt, nt, mt, arow, *, cols_per_seq, n_levels):
    n = u.shape[1]
    p = S5_STATE
    nseq = n // cols_per_seq
    x = _dot(nt, u)

    def to_rows(part):
        return jnp.concatenate([part[:, s * cols_per_seq:(s + 1) * cols_per_seq] for s in range(nseq)], axis=0).T

    def to_cols(rows):
        rt = rows.T
        return jnp.concatenate([rt[s * p:(s + 1) * p, :] for s in range(nseq)], axis=1)

    xr, xi = to_rows(x[:p, :]), to_rows(x[p:, :])
    pos = lax.broadcasted_iota(jnp.int32, (cols_per_seq, 1), 0)
    for lvl in range(n_levels):
        sh = 1 << lvl
        keep = pos >= sh
        sr = jnp.where(keep, pltpu.roll(xr, sh, axis=0), 0.0)
        si = jnp.where(keep, pltpu.roll(xi, sh, axis=0), 0.0)
        ar, ai = arow[lvl:lvl + 1, :], arow[n_levels + lvl:n_levels + lvl + 1, :]
        xr, xi = xr + ar * sr - ai * si, xi + ar * si + ai * sr
    keep = pos >= 1
    pr = jnp.where(keep, pltpu.roll(xr, 1, axis=0), 0.0)
    pi = jnp.where(keep, pltpu.roll(xi, 1, axis=0), 0.0)
    x_prev = jnp.concatenate([to_cols(pr), to_cols(pi)], axis=0).astype(BF16)
    return _dot(wt, u) + _dot(mt, x_prev)


def _s5_body(u_ref, wt_ref, nt_ref, mt_ref, arow_ref, o_ref, *, cols_per_seq, n_levels):
    t, _, n = u_ref.shape
    cg = S5_GROUP_CH
    for gi in range(S5_GPB):
        ch = slice(gi * cg, (gi + 1) * cg)
        u = u_ref[:, ch, :].reshape(t * cg, n)
        y = _s5_group(u, wt_ref[gi], nt_ref[gi], mt_ref[gi], arow_ref[gi],
                      cols_per_seq=cols_per_seq, n_levels=n_levels)
        o_ref[:, ch, :] = y.astype(o_ref.dtype).reshape(t, cg, n)


def s5_apply(proj_t, wt_g, nt_g, mt_g, arow, *, groups, cols_per_seq, n_levels):
    t, _, n = proj_t.shape
    cg = S5_GROUP_CH * S5_GPB
    spec3 = lambda shp: pl.BlockSpec((S5_GPB,) + shp, lambda i: (i, 0, 0))
    return pl.pallas_call(
        functools.partial(_s5_body, cols_per_seq=cols_per_seq, n_levels=n_levels),
        grid=(groups // S5_GPB,),
        in_specs=[pl.BlockSpec((t, cg, n), lambda i: (0, i, 0)),
                  spec3(wt_g.shape[1:]), spec3(nt_g.shape[1:]), spec3(mt_g.shape[1:]), spec3(arow.shape[1:])],
        out_specs=pl.BlockSpec((t, cg, n), lambda i: (0, i, 0)),
        out_shape=jax.ShapeDtypeStruct((t, groups * S5_GROUP_CH, n), BF16),
        compiler_params=_cparams(("parallel",)),
        name="s5",
    )(proj_t, wt_g, nt_g, mt_g, arow)


def s5_operators(lam_re, lam_im, log_step, b_re, b_im, c_re, c_im, d, *, n_levels, n_seq):
    hp = lax.Precision.HIGHEST
    g, p = lam_re.shape
    cg, t = S5_GROUP_CH, S5_T
    lr = jnp.minimum(lam_re.astype(F32), -1e-4)
    li = lam_im.astype(F32)
    step = jnp.exp(log_step.astype(F32))[:, None]
    mag = jnp.exp(lr * step)
    lb_re, lb_im = mag * jnp.cos(li * step), mag * jnp.sin(li * step)
    den = lr * lr + li * li
    nr, ni = lb_re - 1.0, lb_im
    coef_re = (nr * lr + ni * li) / den
    coef_im = (ni * lr - nr * li) / den
    bb_re = coef_re[..., None] * b_re - coef_im[..., None] * b_im
    bb_im = coef_re[..., None] * b_im + coef_im[..., None] * b_re

    def power(m):
        mm = m.astype(F32)[None, :, None]
        mg = jnp.exp(lr[:, None, :] * step[:, None, :] * mm)
        ang = li[:, None, :] * step[:, None, :] * mm
        return mg * jnp.cos(ang), mg * jnp.sin(ang)

    pw_re, pw_im = power(jnp.arange(t + 1))
    pwt_re, pwt_im = pw_re.transpose(0, 2, 1), pw_im.transpose(0, 2, 1)

    def pow_times_b(pr, pi):
        re = pr[:, :, :, None] * bb_re[:, :, None, :] - pi[:, :, :, None] * bb_im[:, :, None, :]
        im = pr[:, :, :, None] * bb_im[:, :, None, :] + pi[:, :, :, None] * bb_re[:, :, None, :]
        return re.reshape(g, p, t * cg), im.reshape(g, p, t * cg)

    pb_re, pb_im = pow_times_b(pwt_re[:, :, :t], pwt_im[:, :, :t])
    taps = (jnp.einsum('gcp,gpy->gcy', c_re.astype(F32), pb_re, precision=hp)
            - jnp.einsum('gcp,gpy->gcy', c_im.astype(F32), pb_im, precision=hp))
    y_idx = jnp.arange(t * cg)
    taps = taps + d.astype(F32).reshape(g, cg, 1) * (y_idx[None, :] == jnp.arange(cg)[:, None]).astype(F32)[None]
    place = ((y_idx[None, :, None] // cg == jnp.arange(t)[:, None, None] - y_idx[None, None, :] // cg)
             & (y_idx[None, :, None] % cg == y_idx[None, None, :] % cg)).astype(BF16)
    wt_g = jnp.einsum('gcy,tyz->gtcz', taps.astype(BF16), place,
                      preferred_element_type=F32).astype(BF16).reshape(g, t * cg, t * cg)
    n_re, n_im = pow_times_b(pwt_re[:, :, t - 1::-1], pwt_im[:, :, t - 1::-1])
    nt_g = jnp.concatenate([n_re, n_im], axis=1).astype(BF16)
    cp_re = c_re[:, None] * pw_re[:, 1:, None, :] - c_im[:, None] * pw_im[:, 1:, None, :]
    cp_im = c_re[:, None] * pw_im[:, 1:, None, :] + c_im[:, None] * pw_re[:, 1:, None, :]
    mt_g = jnp.concatenate([cp_re, -cp_im], axis=-1).reshape(g, t * cg, 2 * p).astype(BF16)
    ar, ai = power(t * (2 ** jnp.arange(n_levels)))
    arow = jnp.concatenate([jnp.tile(ar, (1, 1, n_seq)), jnp.tile(ai, (1, 1, n_seq))], axis=1)
    return wt_g, nt_g, mt_g, arow


def _gelu_tanh(y):
    return 0.5 * y * (1.0 + jnp.tanh(math.sqrt(2.0 / math.pi) * (y + 0.044715 * (y * y * y))))


def _glu_body(y_ref, w_ref, b_ref, gate_ref, o_ref, z_ref, zb_ref, wt_ref):
    @pl.when((pl.program_id(0) == 0) & (pl.program_id(1) == 0))
    def _():
        _transpose_into(wt_ref, w_ref)

    z = _gelu_tanh(y_ref[...].astype(F32))
    z_ref[...] = z
    zb_ref[...] = z.astype(BF16)
    for f0 in range(0, o_ref.shape[1], MM_SUB):
        rows = slice(f0, f0 + MM_SUB)
        lin = _dot(wt_ref[rows, :], zb_ref[...]) + b_ref[rows, :]
        out = z_ref[rows, :] * _sigmoid(lin) * _silu(gate_ref[rows, :].astype(F32))
        o_ref[:, rows] = out.T.astype(o_ref.dtype)


def glu_gate(y_t, w, b, proj_t, *, gate_row, tm=512):
    s, e, n = y_t.shape
    gb = gate_row // e
    return pl.pallas_call(
        _glu_body,
        grid=(s, n // tm),
        in_specs=[pl.BlockSpec((None, e, tm), lambda t, i: (t, 0, i)),
                  pl.BlockSpec((e, e), lambda t, i: (0, 0), pipeline_mode=pl.Buffered(1)),
                  pl.BlockSpec((e, 1), lambda t, i: (0, 0), pipeline_mode=pl.Buffered(1)),
                  pl.BlockSpec((None, e, tm), lambda t, i: (t, gb, i))],
        out_specs=pl.BlockSpec((tm, e), lambda t, i: (i, t)),
        out_shape=jax.ShapeDtypeStruct((n, s * e), BF16),
        scratch_shapes=[pltpu.VMEM((e, tm), F32), pltpu.VMEM((e, tm), BF16), pltpu.VMEM((e, e), BF16)],
        compiler_params=_cparams(("arbitrary", "arbitrary")),
        name="glu",
    )(y_t, w, b.reshape(e, 1), proj_t)


def kernel(x, ab_norm_g, ab_w_in, gla_alpha_up, gla_alpha_b, gla_head_g, fox_f_b, ab_w_out, c_norm_g, c_w_in, s5_lambda_re, s5_lambda_im, s5_log_step, s5_b_re, s5_b_im, s5_c_re, s5_c_im, s5_d, glu_w, glu_b, c_w_out, final_norm_g):
    batch, seq, d = x.shape
    t = batch * seq
    x2 = x.reshape(t, d)

    hk, hv, fw = GLA_HEADS * GLA_DK, GLA_HEADS * GLA_DV, FOX_HEADS * FOX_DH
    sizes = (hk, hk, hv, GLA_LOWRANK, hv, fw, fw, fw, FOX_HEADS, fw)
    offs = [0]
    for s in sizes:
        offs.append(offs[-1] + s)
    w_main, w_small = ab_weight_prep(jnp.swapaxes(ab_w_in, 1, 2), offs)
    c_gq, c_gk, c_gv, c_gg = 0, hk, 2 * hk, 2 * hk + hv
    c_fq = c_gg + hv
    c_fk, c_fv, c_fg = c_fq + fw, c_fq + 2 * fw, c_fq + 3 * fw

    proj, small = norm_proj(x2, ab_norm_g[0], w_main, w_small, tn=w_main.shape[0] // 2, name="ab_in")

    o_gla = gla(proj, small, gla_alpha_up[0], gla_alpha_b[0], gla_head_g[0],
                batch=batch, seq=seq, q_col=c_gq, k_col=c_gk, v_col=c_gv, gate_col=c_gg)

    fb_row = jnp.zeros((1, 128), F32).at[0, GLA_LOWRANK:GLA_LOWRANK + FOX_HEADS].set(fox_f_b[0])
    qa, ka = fox_gate(small, fb_row, batch=batch, seq=seq, col0=GLA_LOWRANK)
    o_fox = fox(proj, qa, ka, batch=batch, seq=seq, q_col=c_fq, k_col=c_fk, v_col=c_fv, gate_col=c_fg)

    x1, h1g = outproj_mid(o_gla, o_fox, ab_w_out[0], x2, c_norm_g[0], steps=S5_T, name="ab_out")

    e = d
    groups = e // S5_GROUP_CH
    n_chunks = t // S5_T
    cols_per_seq = seq // S5_T
    n_levels = max(1, (cols_per_seq - 1).bit_length())
    proj_t = matmul_t_grouped(c_w_in[0], h1g, S5_T, BF16, name="c_in")
    ops = s5_operators(s5_lambda_re[0], s5_lambda_im[0], s5_log_step[0], s5_b_re[0], s5_b_im[0],
                       s5_c_re[0], s5_c_im[0], s5_d[0], n_levels=n_levels, n_seq=batch)
    y_t = s5_apply(proj_t, *ops, groups=groups, cols_per_seq=cols_per_seq, n_levels=n_levels)
    zz = glu_gate(y_t, glu_w[0], glu_b[0], proj_t, gate_row=e)
    out = outproj_final(zz, c_w_out[0], x1, final_norm_g, steps=S5_T, name="c_out")
    return out.reshape(batch, seq, d)
```

```python
import functools
import math

import jax
import jax.numpy as jnp
from jax import lax
from jax.experimental import pallas as pl
from jax.experimental.pallas import tpu as pltpu

EPS = 1e-6
F32 = jnp.float32
BF16 = jnp.bfloat16

GLA_HEADS = 4
GLA_DK = 128
GLA_DV = 256
GLA_LOWRANK = 16
GLA_TAU = 16.0
GLA_CHUNK = 64
GLA_SUB = 8
GLA_UNROLL = 2
FOX_HEADS = 8
FOX_DH = 128
FOX_AUG = 128
FOX_AUG_GROUP = 8
S5_GROUP_CH = 16
S5_STATE = 64
S5_T = 16

VMEM_LIMIT = 56 * 1024 * 1024

NT_DIMS = (((1,), (1,)), ((), ()))
TN_DIMS = (((0,), (0,)), ((), ()))
LOG2E = math.log2(math.e)


def _cparams(sem):
    return pltpu.CompilerParams(dimension_semantics=sem, vmem_limit_bytes=VMEM_LIMIT)


def _dot(a, b):
    return jnp.dot(a, b, preferred_element_type=F32)


def _dot_nt(a, b):
    return lax.dot_general(a, b, NT_DIMS, preferred_element_type=F32)


def _dot_tn(a, b):
    return lax.dot_general(a, b, TN_DIMS, preferred_element_type=F32)


def _log_sigmoid(z):
    return -(jnp.maximum(-z, 0.0) + jnp.log1p(jnp.exp(-jnp.abs(z))))


def _sigmoid(z):
    return 1.0 / (1.0 + jnp.exp(-z))


def _silu(z):
    return z * _sigmoid(z)


def _split3(x):
    hi = x.astype(BF16)
    r1 = x - hi.astype(F32)
    mid = r1.astype(BF16)
    lo = (r1 - mid.astype(F32)).astype(BF16)
    return hi, mid, lo


def _norm_proj_body(x_ref, g_ref, wt_ref, ws_ref, o_ref, s_ref):
    x = x_ref[...]
    ms = jnp.mean(x * x, axis=-1, keepdims=True)
    h = (x * lax.rsqrt(ms + EPS) * g_ref[...]).astype(BF16)
    for c0 in range(0, o_ref.shape[1], MM_SUB):
        o_ref[:, c0:c0 + MM_SUB] = _dot_nt(h, wt_ref[c0:c0 + MM_SUB, :]).astype(o_ref.dtype)
    s_ref[...] = _dot_nt(h, ws_ref[...])


def norm_proj(x, g, wt, wt_small, tm=512, tn=512, name="norm_proj"):
    m, d = x.shape
    n = wt.shape[0]
    ns = wt_small.shape[0]
    proj, small = pl.pallas_call(
        _norm_proj_body,
        grid=(n // tn, m // tm),
        in_specs=[pl.BlockSpec((tm, d), lambda j, i: (i, 0)),
                  pl.BlockSpec((1, d), lambda j, i: (0, 0)),
                  pl.BlockSpec((tn, d), lambda j, i: (j, 0)),
                  pl.BlockSpec((ns, d), lambda j, i: (0, 0))],
        out_specs=[pl.BlockSpec((tm, tn), lambda j, i: (i, j)),
                   pl.BlockSpec((None, tm, ns), lambda j, i: (j, i, 0))],
        out_shape=[jax.ShapeDtypeStruct((m, n), BF16), jax.ShapeDtypeStruct((n // tn, m, ns), F32)],
        compiler_params=_cparams(("arbitrary", "arbitrary")),
        name=name,
    )(x, g.reshape(1, d), wt, wt_small)
    return proj, small[0]


def _ab_wprep_body(w_ref, o_ref, s_ref, *, offs):
    w = w_ref[0]
    seg = lambda n: w[offs[n]:offs[n + 1], :]
    o_ref[...] = jnp.concatenate(
        [w[:offs[3], :], seg(4), seg(5) * (FOX_DH ** -0.5 * LOG2E), w[offs[6]:offs[8], :], seg(9)],
        axis=0).astype(o_ref.dtype)
    pad = s_ref.shape[0] - (offs[4] - offs[3]) - (offs[9] - offs[8])
    s_ref[...] = jnp.concatenate([seg(3), seg(8), jnp.zeros((pad, w.shape[1]), F32)], axis=0).astype(s_ref.dtype)


def ab_weight_prep(w_t3, offs, tk=256):
    _, n_in, d = w_t3.shape
    n_main = n_in - (offs[4] - offs[3]) - (offs[9] - offs[8])
    return pl.pallas_call(
        functools.partial(_ab_wprep_body, offs=tuple(offs)),
        grid=(d // tk,),
        in_specs=[pl.BlockSpec((1, n_in, tk), lambda i: (0, 0, i))],
        out_specs=[pl.BlockSpec((n_main, tk), lambda i: (0, i)), pl.BlockSpec((128, tk), lambda i: (0, i))],
        out_shape=[jax.ShapeDtypeStruct((n_main, d), BF16), jax.ShapeDtypeStruct((128, d), BF16)],
        compiler_params=_cparams(("parallel",)),
        name="ab_wprep",
    )(w_t3)


MM_SUB = 512


def _transpose_into(wt_ref, w_ref):
    for c0 in range(0, w_ref.shape[1], MM_SUB):
        wt_ref[c0:c0 + MM_SUB, :] = w_ref[:, c0:c0 + MM_SUB].astype(wt_ref.dtype).T


def _mm_t_body(w_ref, a_ref, o_ref, wt_ref):
    @pl.when((pl.program_id(1) == 0) & (pl.program_id(2) == 0))
    def _():
        _transpose_into(wt_ref, w_ref)

    a = a_ref[...]
    for r0 in range(0, o_ref.shape[0], MM_SUB):
        o_ref[r0:r0 + MM_SUB, :] = _dot_nt(wt_ref[r0:r0 + MM_SUB, :], a).astype(o_ref.dtype)


def matmul_t_grouped(w, a2, steps, out_dtype, tm=512, tn=2048, name="matmul_tg"):
    k, nf = w.shape
    n = a2.shape[0]
    return pl.pallas_call(
        _mm_t_body,
        grid=(nf // tn, steps, n // tm),
        in_specs=[pl.BlockSpec((k, tn), lambda j, t, i: (0, j), pipeline_mode=pl.Buffered(1)),
                  pl.BlockSpec((tm, k), lambda j, t, i: (i, t))],
        out_specs=pl.BlockSpec((None, tn, tm), lambda j, t, i: (t, j, i)),
        out_shape=jax.ShapeDtypeStruct((steps, nf, n), out_dtype),
        scratch_shapes=[pltpu.VMEM((tn, k), BF16)],
        compiler_params=_cparams(("arbitrary", "arbitrary", "arbitrary")),
        name=name,
    )(w, a2)


def _gla_body(q_ref, k_ref, v_ref, glr_ref, aup_ref, ab_ref, gate_ref, hg_ref, o_ref,
              s_ref, b_ref, kf_ref, kts_ref, dec_ref, *, tb, hpb):
    ib = pl.program_id(2)
    C, SB, dk, dv = GLA_CHUNK, GLA_SUB, GLA_DK, GLA_DV

    @pl.when(ib == 0)
    def _():
        s_ref[...] = jnp.zeros_like(s_ref)

    glr = glr_ref[:, :GLA_LOWRANK].astype(BF16)
    tw = 2 * C
    r = lax.broadcasted_iota(jnp.int32, (tw, tw), 0)
    c = lax.broadcasted_iota(jnp.int32, (tw, tw), 1)
    tri = jnp.where((r // C == c // C) & (c <= r), 1.0, 0.0).astype(BF16)
    nc = tb // C
    heads = range(hpb)
    kcs = [slice(hh * dk, (hh + 1) * dk) for hh in heads]
    zs = [_dot(glr, aup_ref[:, kcs[hh]].astype(BF16)) + ab_ref[:, kcs[hh]] for hh in heads]
    b_blks = []
    for hh in heads:
        la = _log_sigmoid(zs[hh]) * (1.0 / GLA_TAU)
        hi = la.astype(BF16)
        lo = (la - hi.astype(F32)).astype(BF16)
        b_blks.append(jnp.concatenate(
            [_dot(tri, hi[r0:r0 + tw]) + _dot(tri, lo[r0:r0 + tw]) for r0 in range(0, tb, tw)], axis=0))
    for hh in heads:
        b_blk = b_blks[hh]
        k_blk = k_ref[:, kcs[hh]].astype(F32)
        b_ref[hh] = b_blk
        kf_ref[hh] = k_blk
        b3 = b_blk.reshape(nc, C, dk)
        b_end = b3[:, C - 1:C, :]
        k_end = (k_blk.reshape(nc, C, dk) * jnp.exp(b_end - b3)).reshape(tb, dk)
        kt_end = k_end.T.astype(BF16)
        dec_t = jnp.exp(b_end.reshape(nc, dk)).T
        for ci in range(nc):
            kts_ref[hh, ci] = kt_end[:, ci * C:(ci + 1) * C]
            dec_ref[hh, ci] = jnp.broadcast_to(dec_t[:, ci:ci + 1], (dk, dv))

    row = lax.broadcasted_iota(jnp.int32, (C, 1), 0)
    rowi = lax.broadcasted_iota(jnp.int32, (C, C), 0)
    coli = lax.broadcasted_iota(jnp.int32, (C, C), 1)
    lane_c = lax.broadcasted_iota(jnp.int32, (SB, C), 1)
    sub_r = lax.broadcasted_iota(jnp.int32, (SB, 1), 0)
    ones = jnp.ones((GLA_DK, C), BF16)
    neg = -jnp.inf
    scale = GLA_DK ** -0.5

    def stage_state(ci, r0, hh):
        kc = slice(hh * dk, (hh + 1) * dk)
        vc = slice(hh * dv, (hh + 1) * dv)
        q = q_ref[pl.ds(r0, C), kc].astype(F32) * scale
        k = kf_ref[hh, pl.ds(r0, C), :]
        b = b_ref[hh, pl.ds(r0, C), :]
        v = v_ref[pl.ds(r0, C), vc]

        s = s_ref[hh]
        o = _dot((q * jnp.exp(b)).astype(BF16), s.astype(BF16))
        s_ref[hh] = s * dec_ref[hh, ci] + _dot(kts_ref[hh, ci], v)

        attn = jnp.zeros((C, C), F32)
        h = C // 2
        while h >= SB:
            ref = jnp.broadcast_to(b.reshape(C // (2 * h), 2 * h, GLA_DK)[:, h - 1:h, :],
                                   (C // (2 * h), 2 * h, GLA_DK)).reshape(C, GLA_DK)
            upper = (row // h) % 2 == 1
            q_h = q * jnp.exp(jnp.where(upper, b - ref, neg))
            k_h = k * jnp.exp(jnp.where(upper, neg, ref - b))
            a_h = _dot_nt(q_h.astype(BF16), k_h.astype(BF16))
            attn = attn + (a_h if 2 * h == C else jnp.where(rowi // (2 * h) == coli // (2 * h), a_h, 0.0))
            h //= 2
        return q, b, v, o, attn

    def stage_diag(r0, hh, q, b):
        zs = []
        for bi in range(C // SB):
            s0 = bi * SB
            q_i = q[s0:s0 + SB, :]
            b_i = b[s0:s0 + SB, :]
            for j in range(SB):
                k_j = kf_ref[hh, pl.ds(r0 + (s0 + j), 1), :]
                b_j = b_ref[hh, pl.ds(r0 + (s0 + j), 1), :]
                zs.append(q_i * k_j * jnp.exp(jnp.where(sub_r >= j, b_i - b_j, neg)))
        return _dot(jnp.concatenate(zs, axis=0).astype(BF16), ones)

    def stage_out(r0, hh, v, o, attn, zsum):
        vc = slice(hh * dv, (hh + 1) * dv)
        diag = []
        for bi in range(C // SB):
            acc = jnp.zeros((SB, C), F32)
            for j in range(SB):
                n0 = (bi * SB + j) * SB
                acc = acc + jnp.where(lane_c == bi * SB + j, zsum[n0:n0 + SB, :], 0.0)
            diag.append(acc)
        attn = attn + jnp.concatenate(diag, axis=0)
        o = o + _dot(attn.astype(BF16), v)

        o = o * lax.rsqrt(jnp.mean(o * o, axis=-1, keepdims=True) + EPS) * hg_ref[:, vc]
        g = gate_ref[pl.ds(r0, C), vc].astype(F32)
        o_ref[pl.ds(r0, C), vc] = (o * _silu(g)).astype(o_ref.dtype)

    def chunk(ci, carry):
        r0 = pl.multiple_of(ci * C, C)
        heads = range(hpb)
        st = [stage_state(ci, r0, hh) for hh in heads]
        zsums = [stage_diag(r0, hh, st[hh][0], st[hh][1]) for hh in heads]
        for hh in heads:
            _, _, v, o, attn = st[hh]
            stage_out(r0, hh, v, o, attn, zsums[hh])
        return carry

    lax.fori_loop(0, tb // C, chunk, 0, unroll=GLA_UNROLL)


def gla(proj, small, alpha_up, alpha_b, head_g, *, batch, seq, q_col, k_col, v_col, gate_col, tb=512, hpb=4):
    t = batch * seq
    nb = seq // tb
    dk, dv, h = GLA_DK * hpb, GLA_DV * hpb, GLA_HEADS // hpb
    qb, kb, vb, gb = q_col // dk, k_col // dk, v_col // dv, gate_col // dv
    tok = lambda b, hh, i: b * nb + i
    nc = tb // GLA_CHUNK
    return pl.pallas_call(
        functools.partial(_gla_body, tb=tb, hpb=hpb),
        grid=(batch, h, nb),
        in_specs=[
            pl.BlockSpec((tb, dk), lambda b, hh, i: (tok(b, hh, i), qb + hh)),
            pl.BlockSpec((tb, dk), lambda b, hh, i: (tok(b, hh, i), kb + hh)),
            pl.BlockSpec((tb, dv), lambda b, hh, i: (tok(b, hh, i), vb + hh)),
            pl.BlockSpec((tb, 128), lambda b, hh, i: (tok(b, hh, i), 0)),
            pl.BlockSpec((GLA_LOWRANK, dk), lambda b, hh, i: (0, hh)),
            pl.BlockSpec((1, dk), lambda b, hh, i: (0, hh)),
            pl.BlockSpec((tb, dv), lambda b, hh, i: (tok(b, hh, i), gb + hh)),
            pl.BlockSpec((1, dv), lambda b, hh, i: (0, hh)),
        ],
        out_specs=pl.BlockSpec((tb, dv), lambda b, hh, i: (tok(b, hh, i), hh)),
        out_shape=jax.ShapeDtypeStruct((t, GLA_HEADS * GLA_DV), BF16),
        scratch_shapes=[pltpu.VMEM((hpb, GLA_DK, GLA_DV), F32), pltpu.VMEM((hpb, tb, GLA_DK), F32),
                        pltpu.VMEM((hpb, tb, GLA_DK), F32),
                        pltpu.VMEM((hpb, nc, GLA_DK, GLA_CHUNK), BF16),
                        pltpu.VMEM((hpb, nc, GLA_DK, GLA_DV), F32)],
        compiler_params=_cparams(("parallel", "parallel", "arbitrary")),
        name="gla",
    )(proj, proj, proj, small, alpha_up, alpha_b.reshape(1, -1), proj, head_g.reshape(1, -1))


def _fox_gate_body(s_ref, fb_ref, qa_ref, ka_ref, *, blk, col0):
    n = s_ref.shape[0] // blk
    r = lax.broadcasted_iota(jnp.int32, (blk, blk), 0)
    c = lax.broadcasted_iota(jnp.int32, (blk, blk), 1)
    tri = jnp.where(c <= r, 1.0, 0.0).astype(BF16)
    lanes = s_ref.shape[1]
    pr = lax.broadcasted_iota(jnp.int32, (3 * lanes, FOX_AUG), 0)
    pc = lax.broadcasted_iota(jnp.int32, (3 * lanes, FOX_AUG), 1)
    head, piece = pr % lanes - col0, pr // lanes
    is_head = (head >= 0) & (head < FOX_HEADS)
    sel_q = jnp.where(is_head & (pc == head * FOX_AUG_GROUP + piece), 1.0, 0.0).astype(BF16)
    sel_k = jnp.where(is_head & (pc == head * FOX_AUG_GROUP + 3 + piece), -1.0, 0.0).astype(BF16)
    lane = lax.broadcasted_iota(jnp.int32, (1, FOX_AUG), 1)
    used = lane < FOX_HEADS * FOX_AUG_GROUP
    one_q = jnp.where(used & (lane % FOX_AUG_GROUP >= 3) & (lane % FOX_AUG_GROUP < 6), 1.0, 0.0)
    one_k = jnp.where(used & (lane % FOX_AUG_GROUP < 3), 1.0, 0.0)

    def step(i, carry):
        r0 = pl.multiple_of(i * blk, blk)
        lf = _log_sigmoid(s_ref[pl.ds(r0, blk), :] + fb_ref[...])
        hi, mid, lo = _split3(lf)
        cs = _dot(tri, hi) + _dot(tri, mid) + _dot(tri, lo) + carry
        pieces = jnp.concatenate(_split3(cs * LOG2E), axis=1)
        qa_ref[pl.ds(r0, blk), :] = (_dot(pieces, sel_q) + one_q).astype(BF16)
        ka_ref[pl.ds(r0, blk), :] = (_dot(pieces, sel_k) + one_k).astype(BF16)
        return cs[blk - 1:blk, :]

    lax.fori_loop(0, n, step, jnp.zeros((1, s_ref.shape[1]), F32))


def fox_gate(small, fb_row, *, batch, seq, col0, blk=256):
    shp = jax.ShapeDtypeStruct((batch * seq, FOX_AUG), BF16)
    spec = pl.BlockSpec((seq, FOX_AUG), lambda b: (b, 0))
    return pl.pallas_call(
        functools.partial(_fox_gate_body, blk=blk, col0=col0),
        grid=(batch,),
        in_specs=[pl.BlockSpec((seq, 128), lambda b: (b, 0)),
                  pl.BlockSpec((1, 128), lambda b: (0, 0))],
        out_specs=[spec, spec],
        out_shape=[shp, shp],
        compiler_params=_cparams(("parallel",)),
        name="fox_gate",
    )(small, fb_row)


def _fox_body(it_ref, jt_ref, q_ref, qa_ref, k_ref, ka_ref, v_ref, gate_ref, o_ref,
              m_ref, l_ref, acc_ref, *, tq, tk, hpb):
    p = pl.program_id(2)
    i = it_ref[p]
    j = jt_ref[p]
    d = FOX_DH

    @pl.when(j == 0)
    def _():
        m_ref[...] = jnp.full_like(m_ref, -jnp.inf)
        l_ref[...] = jnp.zeros_like(l_ref)
        acc_ref[...] = jnp.zeros_like(acc_ref)

    aug_lane = lax.broadcasted_iota(jnp.int32, (1, FOX_AUG), 1)

    def scores(hh, masked):
        mine = aug_lane // FOX_AUG_GROUP == pl.program_id(1) * hpb + hh
        zero = jnp.zeros((), BF16)
        q_aug = jnp.concatenate([q_ref[:, hh * d:(hh + 1) * d], jnp.where(mine, qa_ref[...], zero)], axis=1)
        k_aug = jnp.concatenate([k_ref[:, hh * d:(hh + 1) * d], jnp.where(mine, ka_ref[...], zero)], axis=1)
        st = _dot_nt(k_aug, q_aug)
        if masked:
            kr = lax.broadcasted_iota(jnp.int32, (tk, tq), 0)
            qc = lax.broadcasted_iota(jnp.int32, (tk, tq), 1)
            st = jnp.where(qc >= kr, st, -jnp.inf)
        return st

    def absorb(hh, st):
        m_old = m_ref[hh]
        m_new = jnp.maximum(m_old, jnp.max(st, axis=0, keepdims=True))
        alpha = jnp.exp2(m_old - m_new)
        pt = jnp.exp2(st - m_new)
        l_ref[hh] = alpha * l_ref[hh] + jnp.sum(pt, axis=0, keepdims=True)
        acc_ref[hh] = alpha * acc_ref[hh] + _dot_tn(v_ref[:, hh * d:(hh + 1) * d], pt.astype(BF16))
        m_ref[hh] = m_new

    def block(masked, finish):
        st = scores(0, masked)
        for hh in range(hpb):
            st_next = scores(hh + 1, masked) if hh + 1 < hpb else None
            absorb(hh, st)
            if finish:
                o = (acc_ref[hh] / l_ref[hh]).T
                g = gate_ref[:, hh * d:(hh + 1) * d].astype(F32)
                o_ref[:, hh * d:(hh + 1) * d] = (o * _silu(g)).astype(o_ref.dtype)
            st = st_next

    @pl.when(j < i)
    def _():
        block(False, False)

    @pl.when(j == i)
    def _():
        block(True, True)


def fox(proj, qa, ka, *, batch, seq, q_col, k_col, v_col, gate_col, t=1024, hpb=4):
    tt = batch * seq
    nb = seq // t
    d, h = FOX_DH, FOX_HEADS
    w = hpb * d
    qb, kb, gb, vb = q_col // w, k_col // w, gate_col // w, v_col // w
    pairs = [(i, j) for i in range(nb) for j in range(i + 1)]
    it = jnp.array([p[0] for p in pairs], jnp.int32)
    jt = jnp.array([p[1] for p in pairs], jnp.int32)
    grid_spec = pltpu.PrefetchScalarGridSpec(
        num_scalar_prefetch=2,
        grid=(batch, h // hpb, len(pairs)),
        in_specs=[
            pl.BlockSpec((t, w), lambda b, hh, p, it, jt: (b * nb + it[p], qb + hh)),
            pl.BlockSpec((t, FOX_AUG), lambda b, hh, p, it, jt: (b * nb + it[p], 0)),
            pl.BlockSpec((t, w), lambda b, hh, p, it, jt: (b * nb + jt[p], kb + hh)),
            pl.BlockSpec((t, FOX_AUG), lambda b, hh, p, it, jt: (b * nb + jt[p], 0)),
            pl.BlockSpec((t, w), lambda b, hh, p, it, jt: (b * nb + jt[p], vb + hh)),
            pl.BlockSpec((t, w), lambda b, hh, p, it, jt: (b * nb + it[p], gb + hh)),
        ],
        out_specs=pl.BlockSpec((t, w), lambda b, hh, p, it, jt: (b * nb + it[p], hh)),
        scratch_shapes=[pltpu.VMEM((hpb, 1, t), F32), pltpu.VMEM((hpb, 1, t), F32), pltpu.VMEM((hpb, d, t), F32)],
    )
    return pl.pallas_call(
        functools.partial(_fox_body, tq=t, tk=t, hpb=hpb),
        grid_spec=grid_spec,
        out_shape=jax.ShapeDtypeStruct((tt, h * d), BF16),
        compiler_params=_cparams(("parallel", "parallel", "arbitrary")),
        name="fox",
    )(it, jt, proj, qa, proj, ka, proj, proj)


def _regroup_perm(tm, steps, to_grouped):
    r = lax.broadcasted_iota(jnp.int32, (tm, tm), 0)
    c = lax.broadcasted_iota(jnp.int32, (tm, tm), 1)
    nc = tm // steps
    src = (r % nc) * steps + r // nc if to_grouped else (r % steps) * nc + r // steps
    return jnp.where(c == src, 1.0, 0.0).astype(BF16)


def _cast_weight_once(wb_ref, w_ref):
    @pl.when(pl.program_id(0) == 0)
    def _():
        for r0 in range(0, w_ref.shape[0], MM_SUB):
            wb_ref[r0:r0 + MM_SUB, :] = w_ref[r0:r0 + MM_SUB, :].astype(wb_ref.dtype)


def _outproj_mid_body(a1_ref, a2_ref, wf_ref, x_ref, g_ref, r_ref, h_ref, w_ref, *, steps):
    k1 = a1_ref.shape[1]
    tm, d = x_ref.shape
    nc = tm // steps
    _cast_weight_once(w_ref, wf_ref)
    acc = x_ref[...] + _dot(a1_ref[...], w_ref[:k1, :]) + _dot(a2_ref[...], w_ref[k1:, :])
    r_ref[...] = acc
    normed = (acc * lax.rsqrt(jnp.mean(acc * acc, axis=-1, keepdims=True) + EPS) * g_ref[...]).astype(BF16)
    by_step = _dot(_regroup_perm(tm, steps, True), normed).astype(h_ref.dtype)
    for s in range(steps):
        h_ref[:, s * d:(s + 1) * d] = by_step[s * nc:(s + 1) * nc, :]


def outproj_mid(a1, a2, w, x, g, *, steps, tm=256, name="outproj_mid"):
    m, d = x.shape
    row = pl.BlockSpec((tm, d), lambda i: (i, 0))
    return pl.pallas_call(
        functools.partial(_outproj_mid_body, steps=steps),
        grid=(m // tm,),
        in_specs=[pl.BlockSpec((tm, a1.shape[1]), lambda i: (i, 0)),
                  pl.BlockSpec((tm, a2.shape[1]), lambda i: (i, 0)),
                  pl.BlockSpec(w.shape, lambda i: (0, 0), pipeline_mode=pl.Buffered(1)),
                  row,
                  pl.BlockSpec((1, d), lambda i: (0, 0))],
        out_specs=[row, pl.BlockSpec((tm // steps, steps * d), lambda i: (i, 0))],
        out_shape=[jax.ShapeDtypeStruct((m, d), F32), jax.ShapeDtypeStruct((m // steps, steps * d), BF16)],
        scratch_shapes=[pltpu.VMEM(w.shape, BF16)],
        compiler_params=_cparams(("arbitrary",)),
        name=name,
    )(a1, a2, w, x, g.reshape(1, d))


def _outproj_final_body(a_ref, wf_ref, x_ref, g_ref, o_ref, w_ref, *, steps):
    tm = x_ref.shape[0]
    e = w_ref.shape[0]
    by_step = jnp.concatenate([a_ref[:, s * e:(s + 1) * e] for s in range(steps)], axis=0)
    a_tok = _dot(_regroup_perm(tm, steps, False), by_step).astype(BF16)
    _cast_weight_once(w_ref, wf_ref)
    acc = x_ref[...] + _dot(a_tok, w_ref[...])
    o_ref[...] = acc * lax.rsqrt(jnp.mean(acc * acc, axis=-1, keepdims=True) + EPS) * g_ref[...]


def outproj_final(a_grouped, w, x, g, *, steps, tm=256, name="outproj_final"):
    m, d = x.shape
    e = w.shape[0]
    row = pl.BlockSpec((tm, d), lambda i: (i, 0))
    return pl.pallas_call(
        functools.partial(_outproj_final_body, steps=steps),
        grid=(m // tm,),
        in_specs=[pl.BlockSpec((tm // steps, steps * e), lambda i: (i, 0)),
                  pl.BlockSpec(w.shape, lambda i: (0, 0), pipeline_mode=pl.Buffered(1)),
                  row,
                  pl.BlockSpec((1, d), lambda i: (0, 0))],
        out_specs=row,
        out_shape=jax.ShapeDtypeStruct((m, d), F32),
        scratch_shapes=[pltpu.VMEM(w.shape, BF16)],
        compiler_params=_cparams(("arbitrary",)),
        name=name,
    )(a_grouped, w, x, g.reshape(1, d))


S5_GPB = 4


def _s5_entering_state(x, arow, *, cols_per_seq, n_levels):
    n = x.shape[1]
    p = S5_STATE
    nseq = n // cols_per_seq

    def to_rows(part):
        return jnp.concatenate([part[:, s * cols_per_seq:(s + 1) * cols_per_seq] for s in range(nseq)], axis=0).T

    def to_cols(rows):
        rt = rows.T
        return jnp.concatenate([rt[s * p:(s + 1) * p, :] for s in range(nseq)], axis=1)

    xr, xi = to_rows(x[:p, :]), to_rows(x[p:, :])
    pos = lax.broadcasted_iota(jnp.int32, (cols_per_seq, 1), 0)
    for lvl in range(n_levels):
        sh = 1 << lvl
        keep = pos >= sh
        sr = jnp.where(keep, pltpu.roll(xr, sh, axis=0), 0.0)
        si = jnp.where(keep, pltpu.roll(xi, sh, axis=0), 0.0)
        ar, ai = arow[lvl:lvl + 1, :], arow[n_levels + lvl:n_levels + lvl + 1, :]
        xr, xi = xr + ar * sr - ai * si, xi + ar * si + ai * sr
    keep = pos >= 1
    pr = jnp.where(keep, pltpu.roll(xr, 1, axis=0), 0.0)
    pi = jnp.where(keep, pltpu.roll(xi, 1, axis=0), 0.0)
    return jnp.concatenate([to_cols(pr), to_cols(pi)], axis=0).astype(BF16)


def _s5_body(u_ref, wt_ref, nt_ref, mt_ref, arow_ref, o_ref, *, cols_per_seq, n_levels):
    t, _, n = u_ref.shape
    cg = S5_GROUP_CH
    groups = range(S5_GPB)
    ch = [slice(gi * cg, (gi + 1) * cg) for gi in groups]
    us = [u_ref[:, ch[gi], :].reshape(t * cg, n) for gi in groups]
    xs = [_dot(nt_ref[gi], us[gi]) for gi in groups]
    ys = [_dot(wt_ref[gi], us[gi]) for gi in groups]
    prevs = [_s5_entering_state(xs[gi], arow_ref[gi], cols_per_seq=cols_per_seq, n_levels=n_levels)
             for gi in groups]
    for gi in groups:
        y = ys[gi] + _dot(mt_ref[gi], prevs[gi])
        o_ref[:, ch[gi], :] = y.astype(o_ref.dtype).reshape(t, cg, n)


def s5_apply(proj_t, wt_g, nt_g, mt_g, arow, *, groups, cols_per_seq, n_levels):
    t, _, n = proj_t.shape
    cg = S5_GROUP_CH * S5_GPB
    spec3 = lambda shp: pl.BlockSpec((S5_GPB,) + shp, lambda i: (i, 0, 0))
    return pl.pallas_call(
        functools.partial(_s5_body, cols_per_seq=cols_per_seq, n_levels=n_levels),
        grid=(groups // S5_GPB,),
        in_specs=[pl.BlockSpec((t, cg, n), lambda i: (0, i, 0)),
                  spec3(wt_g.shape[1:]), spec3(nt_g.shape[1:]), spec3(mt_g.shape[1:]), spec3(arow.shape[1:])],
        out_specs=pl.BlockSpec((t, cg, n), lambda i: (0, i, 0)),
        out_shape=jax.ShapeDtypeStruct((t, groups * S5_GROUP_CH, n), BF16),
        compiler_params=_cparams(("parallel",)),
        name="s5",
    )(proj_t, wt_g, nt_g, mt_g, arow)


def s5_operators(lam_re, lam_im, log_step, b_re, b_im, c_re, c_im, d, *, n_levels, n_seq):
    hp = lax.Precision.HIGHEST
    g, p = lam_re.shape
    cg, t = S5_GROUP_CH, S5_T
    lr = jnp.minimum(lam_re.astype(F32), -1e-4)
    li = lam_im.astype(F32)
    step = jnp.exp(log_step.astype(F32))[:, None]
    mag = jnp.exp(lr * step)
    lb_re, lb_im = mag * jnp.cos(li * step), mag * jnp.sin(li * step)
    den = lr * lr + li * li
    nr, ni = lb_re - 1.0, lb_im
    coef_re = (nr * lr + ni * li) / den
    coef_im = (ni * lr - nr * li) / den
    bb_re = coef_re[..., None] * b_re - coef_im[..., None] * b_im
    bb_im = coef_re[..., None] * b_im + coef_im[..., None] * b_re

    def power(m):
        mm = m.astype(F32)[None, :, None]
        mg = jnp.exp(lr[:, None, :] * step[:, None, :] * mm)
        ang = li[:, None, :] * step[:, None, :] * mm
        return mg * jnp.cos(ang), mg * jnp.sin(ang)

    pw_re, pw_im = power(jnp.arange(t + 1))
    pwt_re, pwt_im = pw_re.transpose(0, 2, 1), pw_im.transpose(0, 2, 1)

    def pow_times_b(pr, pi):
        re = pr[:, :, :, None] * bb_re[:, :, None, :] - pi[:, :, :, None] * bb_im[:, :, None, :]
        im = pr[:, :, :, None] * bb_im[:, :, None, :] + pi[:, :, :, None] * bb_re[:, :, None, :]
        return re.reshape(g, p, t * cg), im.reshape(g, p, t * cg)

    pb_re, pb_im = pow_times_b(pwt_re[:, :, :t], pwt_im[:, :, :t])
    taps = (jnp.einsum('gcp,gpy->gcy', c_re.astype(F32), pb_re, precision=hp)
            - jnp.einsum('gcp,gpy->gcy', c_im.astype(F32), pb_im, precision=hp))
    y_idx = jnp.arange(t * cg)
    taps = taps + d.astype(F32).reshape(g, cg, 1) * (y_idx[None, :] == jnp.arange(cg)[:, None]).astype(F32)[None]
    place = ((y_idx[None, :, None] // cg == jnp.arange(t)[:, None, None] - y_idx[None, None, :] // cg)
             & (y_idx[None, :, None] % cg == y_idx[None, None, :] % cg)).astype(BF16)
    wt_g = jnp.einsum('gcy,tyz->gtcz', taps.astype(BF16), place,
                      preferred_element_type=F32).astype(BF16).reshape(g, t * cg, t * cg)
    n_re, n_im = pow_times_b(pwt_re[:, :, t - 1::-1], pwt_im[:, :, t - 1::-1])
    nt_g = jnp.concatenate([n_re, n_im], axis=1).astype(BF16)
    cp_re = c_re[:, None] * pw_re[:, 1:, None, :] - c_im[:, None] * pw_im[:, 1:, None, :]
    cp_im = c_re[:, None] * pw_im[:, 1:, None, :] + c_im[:, None] * pw_re[:, 1:, None, :]
    mt_g = jnp.concatenate([cp_re, -cp_im], axis=-1).reshape(g, t * cg, 2 * p).astype(BF16)
    ar, ai = power(t * (2 ** jnp.arange(n_levels)))
    arow = jnp.concatenate([jnp.tile(ar, (1, 1, n_seq)), jnp.tile(ai, (1, 1, n_seq))], axis=1)
    return wt_g, nt_g, mt_g, arow


def _gelu_tanh(y):
    return 0.5 * y * (1.0 + jnp.tanh(math.sqrt(2.0 / math.pi) * (y + 0.044715 * (y * y * y))))


def _glu_body(y_ref, w_ref, b_ref, gate_ref, o_ref, z_ref, zb_ref, wt_ref):
    @pl.when((pl.program_id(0) == 0) & (pl.program_id(1) == 0))
    def _():
        _transpose_into(wt_ref, w_ref)

    z = _gelu_tanh(y_ref[...].astype(F32))
    z_ref[...] = z
    zb_ref[...] = z.astype(BF16)
    for f0 in range(0, o_ref.shape[1], MM_SUB):
        rows = slice(f0, f0 + MM_SUB)
        lin = _dot(wt_ref[rows, :], zb_ref[...]) + b_ref[rows, :]
        out = z_ref[rows, :] * _sigmoid(lin) * _silu(gate_ref[rows, :].astype(F32))
        o_ref[:, rows] = out.T.astype(o_ref.dtype)


def glu_gate(y_t, w, b, proj_t, *, gate_row, tm=512):
    s, e, n = y_t.shape
    gb = gate_row // e
    return pl.pallas_call(
        _glu_body,
        grid=(s, n // tm),
        in_specs=[pl.BlockSpec((None, e, tm), lambda t, i: (t, 0, i)),
                  pl.BlockSpec((e, e), lambda t, i: (0, 0), pipeline_mode=pl.Buffered(1)),
                  pl.BlockSpec((e, 1), lambda t, i: (0, 0), pipeline_mode=pl.Buffered(1)),
                  pl.BlockSpec((None, e, tm), lambda t, i: (t, gb, i))],
        out_specs=pl.BlockSpec((tm, e), lambda t, i: (i, t)),
        out_shape=jax.ShapeDtypeStruct((n, s * e), BF16),
        scratch_shapes=[pltpu.VMEM((e, tm), F32), pltpu.VMEM((e, tm), BF16), pltpu.VMEM((e, e), BF16)],
        compiler_params=_cparams(("arbitrary", "arbitrary")),
        name="glu",
    )(y_t, w, b.reshape(e, 1), proj_t)


def kernel(x, ab_norm_g, ab_w_in, gla_alpha_up, gla_alpha_b, gla_head_g, fox_f_b, ab_w_out, c_norm_g, c_w_in, s5_lambda_re, s5_lambda_im, s5_log_step, s5_b_re, s5_b_im, s5_c_re, s5_c_im, s5_d, glu_w, glu_b, c_w_out, final_norm_g):
    batch, seq, d = x.shape
    t = batch * seq
    x2 = x.reshape(t, d)

    hk, hv, fw = GLA_HEADS * GLA_DK, GLA_HEADS * GLA_DV, FOX_HEADS * FOX_DH
    sizes = (hk, hk, hv, GLA_LOWRANK, hv, fw, fw, fw, FOX_HEADS, fw)
    offs = [0]
    for s in sizes:
        offs.append(offs[-1] + s)
    w_main, w_small = ab_weight_prep(jnp.swapaxes(ab_w_in, 1, 2), offs)
    c_gq, c_gk, c_gv, c_gg = 0, hk, 2 * hk, 2 * hk + hv
    c_fq = c_gg + hv
    c_fk, c_fv, c_fg = c_fq + fw, c_fq + 2 * fw, c_fq + 3 * fw

    proj, small = norm_proj(x2, ab_norm_g[0], w_main, w_small, tn=w_main.shape[0] // 2, name="ab_in")

    o_gla = gla(proj, small, gla_alpha_up[0], gla_alpha_b[0], gla_head_g[0],
                batch=batch, seq=seq, q_col=c_gq, k_col=c_gk, v_col=c_gv, gate_col=c_gg)

    fb_row = jnp.zeros((1, 128), F32).at[0, GLA_LOWRANK:GLA_LOWRANK + FOX_HEADS].set(fox_f_b[0])
    qa, ka = fox_gate(small, fb_row, batch=batch, seq=seq, col0=GLA_LOWRANK)
    o_fox = fox(proj, qa, ka, batch=batch, seq=seq, q_col=c_fq, k_col=c_fk, v_col=c_fv, gate_col=c_fg)

    x1, h1g = outproj_mid(o_gla, o_fox, ab_w_out[0], x2, c_norm_g[0], steps=S5_T, name="ab_out")

    e = d
    groups = e // S5_GROUP_CH
    n_chunks = t // S5_T
    cols_per_seq = seq // S5_T
    n_levels = max(1, (cols_per_seq - 1).bit_length())
    proj_t = matmul_t_grouped(c_w_in[0], h1g, S5_T, BF16, name="c_in")
    ops = s5_operators(s5_lambda_re[0], s5_lambda_im[0], s5_log_step[0], s5_b_re[0], s5_b_im[0],
                       s5_c_re[0], s5_c_im[0], s5_d[0], n_levels=n_levels, n_seq=batch)
    y_t = s5_apply(proj_t, *ops, groups=groups, cols_per_seq=cols_per_seq, n_levels=n_levels)
    zz = glu_gate(y_t, glu_w[0], glu_b[0], proj_t, gate_row=e)
    out = outproj_final(zz, c_w_out[0], x1, final_norm_g, steps=S5_T, name="c_out")
    return out.reshape(batch, seq, d)
```

```python
import functools
import math

import jax
import jax.numpy as jnp
from jax import lax
from jax.experimental import pallas as pl
from jax.experimental.pallas import tpu as pltpu

EPS = 1e-6
F32 = jnp.float32
BF16 = jnp.bfloat16

GLA_HEADS = 4
GLA_DK = 128
GLA_DV = 256
GLA_LOWRANK = 16
GLA_TAU = 16.0
GLA_CHUNK = 64
GLA_SUB = 8
GLA_UNROLL = 2
FOX_HEADS = 8
FOX_DH = 128
FOX_AUG = 128
FOX_AUG_GROUP = 8
S5_GROUP_CH = 16
S5_STATE = 64
S5_T = 16

VMEM_LIMIT = 56 * 1024 * 1024

NT_DIMS = (((1,), (1,)), ((), ()))
TN_DIMS = (((0,), (0,)), ((), ()))
LOG2E = math.log2(math.e)


def _cparams(sem):
    return pltpu.CompilerParams(dimension_semantics=sem, vmem_limit_bytes=VMEM_LIMIT)


def _dot(a, b):
    return jnp.dot(a, b, preferred_element_type=F32)


def _dot_nt(a, b):
    return lax.dot_general(a, b, NT_DIMS, preferred_element_type=F32)


def _dot_tn(a, b):
    return lax.dot_general(a, b, TN_DIMS, preferred_element_type=F32)


def _log_sigmoid(z):
    return -(jnp.maximum(-z, 0.0) + jnp.log1p(jnp.exp(-jnp.abs(z))))


def _sigmoid(z):
    return 0.5 * jnp.tanh(0.5 * z) + 0.5


def _silu(z):
    return z * _sigmoid(z)


def _split3(x):
    hi = x.astype(BF16)
    r1 = x - hi.astype(F32)
    mid = r1.astype(BF16)
    lo = (r1 - mid.astype(F32)).astype(BF16)
    return hi, mid, lo


def _norm_proj_body(x_ref, g_ref, wt_ref, ws_ref, o_ref, s_ref):
    x = x_ref[...]
    ms = jnp.mean(x * x, axis=-1, keepdims=True)
    h = (x * lax.rsqrt(ms + EPS) * g_ref[...]).astype(BF16)
    for c0 in range(0, o_ref.shape[1], MM_SUB):
        o_ref[:, c0:c0 + MM_SUB] = _dot_nt(h, wt_ref[c0:c0 + MM_SUB, :]).astype(o_ref.dtype)
    s_ref[...] = _dot_nt(h, ws_ref[...])


def norm_proj(x, g, wt, wt_small, tm=512, tn=512, name="norm_proj"):
    m, d = x.shape
    n = wt.shape[0]
    ns = wt_small.shape[0]
    proj, small = pl.pallas_call(
        _norm_proj_body,
        grid=(n // tn, m // tm),
        in_specs=[pl.BlockSpec((tm, d), lambda j, i: (i, 0)),
                  pl.BlockSpec((1, d), lambda j, i: (0, 0)),
                  pl.BlockSpec((tn, d), lambda j, i: (j, 0)),
                  pl.BlockSpec((ns, d), lambda j, i: (0, 0))],
        out_specs=[pl.BlockSpec((tm, tn), lambda j, i: (i, j)),
                   pl.BlockSpec((None, tm, ns), lambda j, i: (j, i, 0))],
        out_shape=[jax.ShapeDtypeStruct((m, n), BF16), jax.ShapeDtypeStruct((n // tn, m, ns), F32)],
        compiler_params=_cparams(("arbitrary", "arbitrary")),
        name=name,
    )(x, g.reshape(1, d), wt, wt_small)
    return proj, small[0]


def _ab_wprep_body(w_ref, o_ref, s_ref, *, offs):
    w = w_ref[0]
    seg = lambda n: w[offs[n]:offs[n + 1], :]
    o_ref[...] = jnp.concatenate(
        [w[:offs[3], :], seg(4), seg(5) * (FOX_DH ** -0.5 * LOG2E), w[offs[6]:offs[8], :], seg(9)],
        axis=0).astype(o_ref.dtype)
    pad = s_ref.shape[0] - (offs[4] - offs[3]) - (offs[9] - offs[8])
    s_ref[...] = jnp.concatenate([seg(3), seg(8), jnp.zeros((pad, w.shape[1]), F32)], axis=0).astype(s_ref.dtype)


def ab_weight_prep(w_t3, offs, tk=256):
    _, n_in, d = w_t3.shape
    n_main = n_in - (offs[4] - offs[3]) - (offs[9] - offs[8])
    return pl.pallas_call(
        functools.partial(_ab_wprep_body, offs=tuple(offs)),
        grid=(d // tk,),
        in_specs=[pl.BlockSpec((1, n_in, tk), lambda i: (0, 0, i))],
        out_specs=[pl.BlockSpec((n_main, tk), lambda i: (0, i)), pl.BlockSpec((128, tk), lambda i: (0, i))],
        out_shape=[jax.ShapeDtypeStruct((n_main, d), BF16), jax.ShapeDtypeStruct((128, d), BF16)],
        compiler_params=_cparams(("parallel",)),
        name="ab_wprep",
    )(w_t3)


MM_SUB = 512


def _transpose_into(wt_ref, w_ref):
    for c0 in range(0, w_ref.shape[1], MM_SUB):
        wt_ref[c0:c0 + MM_SUB, :] = w_ref[:, c0:c0 + MM_SUB].astype(wt_ref.dtype).T


def _mm_t_body(w_ref, a_ref, o_ref, wt_ref):
    @pl.when((pl.program_id(1) == 0) & (pl.program_id(2) == 0))
    def _():
        _transpose_into(wt_ref, w_ref)

    a = a_ref[...]
    for r0 in range(0, o_ref.shape[0], MM_SUB):
        o_ref[r0:r0 + MM_SUB, :] = _dot_nt(wt_ref[r0:r0 + MM_SUB, :], a).astype(o_ref.dtype)


def matmul_t_grouped(w, a2, steps, out_dtype, tm=512, tn=2048, name="matmul_tg"):
    k, nf = w.shape
    n = a2.shape[0]
    return pl.pallas_call(
        _mm_t_body,
        grid=(nf // tn, steps, n // tm),
        in_specs=[pl.BlockSpec((k, tn), lambda j, t, i: (0, j), pipeline_mode=pl.Buffered(1)),
                  pl.BlockSpec((tm, k), lambda j, t, i: (i, t))],
        out_specs=pl.BlockSpec((None, tn, tm), lambda j, t, i: (t, j, i)),
        out_shape=jax.ShapeDtypeStruct((steps, nf, n), out_dtype),
        scratch_shapes=[pltpu.VMEM((tn, k), BF16)],
        compiler_params=_cparams(("arbitrary", "arbitrary", "arbitrary")),
        name=name,
    )(w, a2)


def _gla_body(q_ref, k_ref, v_ref, glr_ref, aup_ref, ab_ref, gate_ref, hg_ref, o_ref,
              s_ref, b_ref, kf_ref, kts_ref, dec_ref, *, tb, hpb):
    ib = pl.program_id(2)
    C, SB, dk, dv = GLA_CHUNK, GLA_SUB, GLA_DK, GLA_DV

    @pl.when(ib == 0)
    def _():
        s_ref[...] = jnp.zeros_like(s_ref)

    glr = glr_ref[:, :GLA_LOWRANK].astype(BF16)
    tw = 2 * C
    r = lax.broadcasted_iota(jnp.int32, (tw, tw), 0)
    c = lax.broadcasted_iota(jnp.int32, (tw, tw), 1)
    tri = jnp.where((r // C == c // C) & (c <= r), 1.0, 0.0).astype(BF16)
    nc = tb // C
    heads = range(hpb)
    kcs = [slice(hh * dk, (hh + 1) * dk) for hh in heads]
    zs = [_dot(glr, aup_ref[:, kcs[hh]].astype(BF16)) + ab_ref[:, kcs[hh]] for hh in heads]
    b_blks = []
    for hh in heads:
        la = _log_sigmoid(zs[hh]) * (1.0 / GLA_TAU)
        hi = la.astype(BF16)
        lo = (la - hi.astype(F32)).astype(BF16)
        b_blks.append(jnp.concatenate(
            [_dot(tri, hi[r0:r0 + tw]) + _dot(tri, lo[r0:r0 + tw]) for r0 in range(0, tb, tw)], axis=0))
    for hh in heads:
        b_blk = b_blks[hh]
        k_blk = k_ref[:, kcs[hh]].astype(F32)
        b_ref[hh] = b_blk
        kf_ref[hh] = k_blk
        b3 = b_blk.reshape(nc, C, dk)
        b_end = b3[:, C - 1:C, :]
        k_end = (k_blk.reshape(nc, C, dk) * jnp.exp(b_end - b3)).reshape(tb, dk)
        kt_end = k_end.T.astype(BF16)
        dec_t = jnp.exp(b_end.reshape(nc, dk)).T
        for ci in range(nc):
            kts_ref[hh, ci] = kt_end[:, ci * C:(ci + 1) * C]
            dec_ref[hh, ci] = jnp.broadcast_to(dec_t[:, ci:ci + 1], (dk, dv))

    row = lax.broadcasted_iota(jnp.int32, (C, 1), 0)
    rowi = lax.broadcasted_iota(jnp.int32, (C, C), 0)
    coli = lax.broadcasted_iota(jnp.int32, (C, C), 1)
    lane_c = lax.broadcasted_iota(jnp.int32, (SB, C), 1)
    sub_r = lax.broadcasted_iota(jnp.int32, (SB, 1), 0)
    ones = jnp.ones((GLA_DK, C), BF16)
    neg = -jnp.inf
    scale = GLA_DK ** -0.5

    def stage_state(ci, r0, hh):
        kc = slice(hh * dk, (hh + 1) * dk)
        vc = slice(hh * dv, (hh + 1) * dv)
        q = q_ref[pl.ds(r0, C), kc].astype(F32) * scale
        k = kf_ref[hh, pl.ds(r0, C), :]
        b = b_ref[hh, pl.ds(r0, C), :]
        v = v_ref[pl.ds(r0, C), vc]

        s = s_ref[hh]
        o = _dot((q * jnp.exp(b)).astype(BF16), s.astype(BF16))
        s_ref[hh] = s * dec_ref[hh, ci] + _dot(kts_ref[hh, ci], v)

        attn = jnp.zeros((C, C), F32)
        h = C // 2
        while h >= SB:
            ref = jnp.broadcast_to(b.reshape(C // (2 * h), 2 * h, GLA_DK)[:, h - 1:h, :],
                                   (C // (2 * h), 2 * h, GLA_DK)).reshape(C, GLA_DK)
            upper = (row // h) % 2 == 1
            q_h = q * jnp.exp(jnp.where(upper, b - ref, neg))
            k_h = k * jnp.exp(jnp.where(upper, neg, ref - b))
            a_h = _dot_nt(q_h.astype(BF16), k_h.astype(BF16))
            attn = attn + (a_h if 2 * h == C else jnp.where(rowi // (2 * h) == coli // (2 * h), a_h, 0.0))
            h //= 2
        return q, b, v, o, attn

    def stage_diag(r0, hh, q, b):
        zs = []
        for bi in range(C // SB):
            s0 = bi * SB
            q_i = q[s0:s0 + SB, :]
            b_i = b[s0:s0 + SB, :]
            for j in range(SB):
                k_j = kf_ref[hh, pl.ds(r0 + (s0 + j), 1), :]
                b_j = b_ref[hh, pl.ds(r0 + (s0 + j), 1), :]
                zs.append(q_i * k_j * jnp.exp(jnp.where(sub_r >= j, b_i - b_j, neg)))
        return _dot(jnp.concatenate(zs, axis=0).astype(BF16), ones)

    def stage_out(r0, hh, v, o, attn, zsum):
        vc = slice(hh * dv, (hh + 1) * dv)
        diag = []
        for bi in range(C // SB):
            acc = jnp.zeros((SB, C), F32)
            for j in range(SB):
                n0 = (bi * SB + j) * SB
                acc = acc + jnp.where(lane_c == bi * SB + j, zsum[n0:n0 + SB, :], 0.0)
            diag.append(acc)
        attn = attn + jnp.concatenate(diag, axis=0)
        o = o + _dot(attn.astype(BF16), v)

        o = o * lax.rsqrt(jnp.mean(o * o, axis=-1, keepdims=True) + EPS) * hg_ref[:, vc]
        g = gate_ref[pl.ds(r0, C), vc].astype(F32)
        o_ref[pl.ds(r0, C), vc] = (o * _silu(g)).astype(o_ref.dtype)

    def chunk(ci, carry):
        r0 = pl.multiple_of(ci * C, C)
        heads = range(hpb)
        st = [stage_state(ci, r0, hh) for hh in heads]
        zsums = [stage_diag(r0, hh, st[hh][0], st[hh][1]) for hh in heads]
        for hh in heads:
            _, _, v, o, attn = st[hh]
            stage_out(r0, hh, v, o, attn, zsums[hh])
        return carry

    lax.fori_loop(0, tb // C, chunk, 0, unroll=GLA_UNROLL)


def gla(proj, small, alpha_up, alpha_b, head_g, *, batch, seq, q_col, k_col, v_col, gate_col, tb=512, hpb=4):
    t = batch * seq
    nb = seq // tb
    dk, dv, h = GLA_DK * hpb, GLA_DV * hpb, GLA_HEADS // hpb
    qb, kb, vb, gb = q_col // dk, k_col // dk, v_col // dv, gate_col // dv
    tok = lambda b, hh, i: b * nb + i
    nc = tb // GLA_CHUNK
    return pl.pallas_call(
        functools.partial(_gla_body, tb=tb, hpb=hpb),
        grid=(batch, h, nb),
        in_specs=[
            pl.BlockSpec((tb, dk), lambda b, hh, i: (tok(b, hh, i), qb + hh)),
            pl.BlockSpec((tb, dk), lambda b, hh, i: (tok(b, hh, i), kb + hh)),
            pl.BlockSpec((tb, dv), lambda b, hh, i: (tok(b, hh, i), vb + hh)),
            pl.BlockSpec((tb, 128), lambda b, hh, i: (tok(b, hh, i), 0)),
            pl.BlockSpec((GLA_LOWRANK, dk), lambda b, hh, i: (0, hh)),
            pl.BlockSpec((1, dk), lambda b, hh, i: (0, hh)),
            pl.BlockSpec((tb, dv), lambda b, hh, i: (tok(b, hh, i), gb + hh)),
            pl.BlockSpec((1, dv), lambda b, hh, i: (0, hh)),
        ],
        out_specs=pl.BlockSpec((tb, dv), lambda b, hh, i: (tok(b, hh, i), hh)),
        out_shape=jax.ShapeDtypeStruct((t, GLA_HEADS * GLA_DV), BF16),
        scratch_shapes=[pltpu.VMEM((hpb, GLA_DK, GLA_DV), F32), pltpu.VMEM((hpb, tb, GLA_DK), F32),
                        pltpu.VMEM((hpb, tb, GLA_DK), F32),
                        pltpu.VMEM((hpb, nc, GLA_DK, GLA_CHUNK), BF16),
                        pltpu.VMEM((hpb, nc, GLA_DK, GLA_DV), F32)],
        compiler_params=_cparams(("parallel", "parallel", "arbitrary")),
        name="gla",
    )(proj, proj, proj, small, alpha_up, alpha_b.reshape(1, -1), proj, head_g.reshape(1, -1))


def _fox_gate_body(s_ref, fb_ref, qa_ref, ka_ref, *, blk, col0):
    n = s_ref.shape[0] // blk
    r = lax.broadcasted_iota(jnp.int32, (blk, blk), 0)
    c = lax.broadcasted_iota(jnp.int32, (blk, blk), 1)
    tri = jnp.where(c <= r, 1.0, 0.0).astype(BF16)
    lanes = s_ref.shape[1]
    pr = lax.broadcasted_iota(jnp.int32, (3 * lanes, FOX_AUG), 0)
    pc = lax.broadcasted_iota(jnp.int32, (3 * lanes, FOX_AUG), 1)
    head, piece = pr % lanes - col0, pr // lanes
    is_head = (head >= 0) & (head < FOX_HEADS)
    sel_q = jnp.where(is_head & (pc == head * FOX_AUG_GROUP + piece), 1.0, 0.0).astype(BF16)
    sel_k = jnp.where(is_head & (pc == head * FOX_AUG_GROUP + 3 + piece), -1.0, 0.0).astype(BF16)
    lane = lax.broadcasted_iota(jnp.int32, (1, FOX_AUG), 1)
    used = lane < FOX_HEADS * FOX_AUG_GROUP
    one_q = jnp.where(used & (lane % FOX_AUG_GROUP >= 3) & (lane % FOX_AUG_GROUP < 6), 1.0, 0.0)
    one_k = jnp.where(used & (lane % FOX_AUG_GROUP < 3), 1.0, 0.0)

    def step(i, carry):
        r0 = pl.multiple_of(i * blk, blk)
        lf = _log_sigmoid(s_ref[pl.ds(r0, blk), :] + fb_ref[...])
        hi, mid, lo = _split3(lf)
        cs = _dot(tri, hi) + _dot(tri, mid) + _dot(tri, lo) + carry
        pieces = jnp.concatenate(_split3(cs * LOG2E), axis=1)
        qa_ref[pl.ds(r0, blk), :] = (_dot(pieces, sel_q) + one_q).astype(BF16)
        ka_ref[pl.ds(r0, blk), :] = (_dot(pieces, sel_k) + one_k).astype(BF16)
        return cs[blk - 1:blk, :]

    lax.fori_loop(0, n, step, jnp.zeros((1, s_ref.shape[1]), F32))


def fox_gate(small, fb_row, *, batch, seq, col0, blk=256):
    shp = jax.ShapeDtypeStruct((batch * seq, FOX_AUG), BF16)
    spec = pl.BlockSpec((seq, FOX_AUG), lambda b: (b, 0))
    return pl.pallas_call(
        functools.partial(_fox_gate_body, blk=blk, col0=col0),
        grid=(batch,),
        in_specs=[pl.BlockSpec((seq, 128), lambda b: (b, 0)),
                  pl.BlockSpec((1, 128), lambda b: (0, 0))],
        out_specs=[spec, spec],
        out_shape=[shp, shp],
        compiler_params=_cparams(("parallel",)),
        name="fox_gate",
    )(small, fb_row)


def _fox_body(it_ref, jt_ref, q_ref, qa_ref, k_ref, ka_ref, v_ref, gate_ref, o_ref,
              m_ref, l_ref, acc_ref, *, tq, tk, hpb):
    p = pl.program_id(2)
    i = it_ref[p]
    j = jt_ref[p]
    d = FOX_DH

    @pl.when(j == 0)
    def _():
        m_ref[...] = jnp.full_like(m_ref, -jnp.inf)
        l_ref[...] = jnp.zeros_like(l_ref)
        acc_ref[...] = jnp.zeros_like(acc_ref)

    aug_lane = lax.broadcasted_iota(jnp.int32, (1, FOX_AUG), 1)

    def scores(hh, masked):
        mine = aug_lane // FOX_AUG_GROUP == pl.program_id(1) * hpb + hh
        zero = jnp.zeros((), BF16)
        q_aug = jnp.concatenate([q_ref[:, hh * d:(hh + 1) * d], jnp.where(mine, qa_ref[...], zero)], axis=1)
        k_aug = jnp.concatenate([k_ref[:, hh * d:(hh + 1) * d], jnp.where(mine, ka_ref[...], zero)], axis=1)
        st = _dot_nt(k_aug, q_aug)
        if masked:
            kr = lax.broadcasted_iota(jnp.int32, (tk, tq), 0)
            qc = lax.broadcasted_iota(jnp.int32, (tk, tq), 1)
            st = jnp.where(qc >= kr, st, -jnp.inf)
        return st

    def absorb(hh, st):
        m_old = m_ref[hh]
        m_new = jnp.maximum(m_old, jnp.max(st, axis=0, keepdims=True))
        alpha = jnp.exp2(m_old - m_new)
        pt = jnp.exp2(st - m_new)
        l_ref[hh] = alpha * l_ref[hh] + jnp.sum(pt, axis=0, keepdims=True)
        acc_ref[hh] = alpha * acc_ref[hh] + _dot_tn(v_ref[:, hh * d:(hh + 1) * d], pt.astype(BF16))
        m_ref[hh] = m_new

    def block(masked, finish):
        st = scores(0, masked)
        for hh in range(hpb):
            st_next = scores(hh + 1, masked) if hh + 1 < hpb else None
            absorb(hh, st)
            if finish:
                o = (acc_ref[hh] / l_ref[hh]).T
                g = gate_ref[:, hh * d:(hh + 1) * d].astype(F32)
                o_ref[:, hh * d:(hh + 1) * d] = (o * _silu(g)).astype(o_ref.dtype)
            st = st_next

    @pl.when(j < i)
    def _():
        block(False, False)

    @pl.when(j == i)
    def _():
        block(True, True)


def fox(proj, qa, ka, *, batch, seq, q_col, k_col, v_col, gate_col, t=1024, hpb=4):
    tt = batch * seq
    nb = seq // t
    d, h = FOX_DH, FOX_HEADS
    w = hpb * d
    qb, kb, gb, vb = q_col // w, k_col // w, gate_col // w, v_col // w
    pairs = [(i, j) for i in range(nb) for j in range(i + 1)]
    it = jnp.array([p[0] for p in pairs], jnp.int32)
    jt = jnp.array([p[1] for p in pairs], jnp.int32)
    grid_spec = pltpu.PrefetchScalarGridSpec(
        num_scalar_prefetch=2,
        grid=(batch, h // hpb, len(pairs)),
        in_specs=[
            pl.BlockSpec((t, w), lambda b, hh, p, it, jt: (b * nb + it[p], qb + hh)),
            pl.BlockSpec((t, FOX_AUG), lambda b, hh, p, it, jt: (b * nb + it[p], 0)),
            pl.BlockSpec((t, w), lambda b, hh, p, it, jt: (b * nb + jt[p], kb + hh)),
            pl.BlockSpec((t, FOX_AUG), lambda b, hh, p, it, jt: (b * nb + jt[p], 0)),
            pl.BlockSpec((t, w), lambda b, hh, p, it, jt: (b * nb + jt[p], vb + hh)),
            pl.BlockSpec((t, w), lambda b, hh, p, it, jt: (b * nb + it[p], gb + hh)),
        ],
        out_specs=pl.BlockSpec((t, w), lambda b, hh, p, it, jt: (b * nb + it[p], hh)),
        scratch_shapes=[pltpu.VMEM((hpb, 1, t), F32), pltpu.VMEM((hpb, 1, t), F32), pltpu.VMEM((hpb, d, t), F32)],
    )
    return pl.pallas_call(
        functools.partial(_fox_body, tq=t, tk=t, hpb=hpb),
        grid_spec=grid_spec,
        out_shape=jax.ShapeDtypeStruct((tt, h * d), BF16),
        compiler_params=_cparams(("parallel", "parallel", "arbitrary")),
        name="fox",
    )(it, jt, proj, qa, proj, ka, proj, proj)


def _regroup_perm(tm, steps, to_grouped):
    r = lax.broadcasted_iota(jnp.int32, (tm, tm), 0)
    c = lax.broadcasted_iota(jnp.int32, (tm, tm), 1)
    nc = tm // steps
    src = (r % nc) * steps + r // nc if to_grouped else (r % steps) * nc + r // steps
    return jnp.where(c == src, 1.0, 0.0).astype(BF16)


def _cast_weight_once(wb_ref, w_ref):
    @pl.when(pl.program_id(0) == 0)
    def _():
        for r0 in range(0, w_ref.shape[0], MM_SUB):
            wb_ref[r0:r0 + MM_SUB, :] = w_ref[r0:r0 + MM_SUB, :].astype(wb_ref.dtype)


def _outproj_mid_body(a1_ref, a2_ref, wf_ref, x_ref, g_ref, r_ref, h_ref, w_ref, *, steps):
    k1 = a1_ref.shape[1]
    tm, d = x_ref.shape
    nc = tm // steps
    _cast_weight_once(w_ref, wf_ref)
    acc = x_ref[...] + _dot(a1_ref[...], w_ref[:k1, :]) + _dot(a2_ref[...], w_ref[k1:, :])
    r_ref[...] = acc
    normed = (acc * lax.rsqrt(jnp.mean(acc * acc, axis=-1, keepdims=True) + EPS) * g_ref[...]).astype(BF16)
    by_step = _dot(_regroup_perm(tm, steps, True), normed).astype(h_ref.dtype)
    for s in range(steps):
        h_ref[:, s * d:(s + 1) * d] = by_step[s * nc:(s + 1) * nc, :]


def outproj_mid(a1, a2, w, x, g, *, steps, tm=256, name="outproj_mid"):
    m, d = x.shape
    row = pl.BlockSpec((tm, d), lambda i: (i, 0))
    return pl.pallas_call(
        functools.partial(_outproj_mid_body, steps=steps),
        grid=(m // tm,),
        in_specs=[pl.BlockSpec((tm, a1.shape[1]), lambda i: (i, 0)),
                  pl.BlockSpec((tm, a2.shape[1]), lambda i: (i, 0)),
                  pl.BlockSpec(w.shape, lambda i: (0, 0), pipeline_mode=pl.Buffered(1)),
                  row,
                  pl.BlockSpec((1, d), lambda i: (0, 0))],
        out_specs=[row, pl.BlockSpec((tm // steps, steps * d), lambda i: (i, 0))],
        out_shape=[jax.ShapeDtypeStruct((m, d), F32), jax.ShapeDtypeStruct((m // steps, steps * d), BF16)],
        scratch_shapes=[pltpu.VMEM(w.shape, BF16)],
        compiler_params=_cparams(("arbitrary",)),
        name=name,
    )(a1, a2, w, x, g.reshape(1, d))


def _outproj_final_body(a_ref, wf_ref, x_ref, g_ref, o_ref, w_ref, *, steps):
    tm = x_ref.shape[0]
    e = w_ref.shape[0]
    by_step = jnp.concatenate([a_ref[:, s * e:(s + 1) * e] for s in range(steps)], axis=0)
    a_tok = _dot(_regroup_perm(tm, steps, False), by_step).astype(BF16)
    _cast_weight_once(w_ref, wf_ref)
    acc = x_ref[...] + _dot(a_tok, w_ref[...])
    o_ref[...] = acc * lax.rsqrt(jnp.mean(acc * acc, axis=-1, keepdims=True) + EPS) * g_ref[...]


def outproj_final(a_grouped, w, x, g, *, steps, tm=256, name="outproj_final"):
    m, d = x.shape
    e = w.shape[0]
    row = pl.BlockSpec((tm, d), lambda i: (i, 0))
    return pl.pallas_call(
        functools.partial(_outproj_final_body, steps=steps),
        grid=(m // tm,),
        in_specs=[pl.BlockSpec((tm // steps, steps * e), lambda i: (i, 0)),
                  pl.BlockSpec(w.shape, lambda i: (0, 0), pipeline_mode=pl.Buffered(1)),
                  row,
                  pl.BlockSpec((1, d), lambda i: (0, 0))],
        out_specs=row,
        out_shape=jax.ShapeDtypeStruct((m, d), F32),
        scratch_shapes=[pltpu.VMEM(w.shape, BF16)],
        compiler_params=_cparams(("arbitrary",)),
        name=name,
    )(a_grouped, w, x, g.reshape(1, d))


S5_GPB = 4


def _s5_entering_state(x, arow, *, cols_per_seq, n_levels):
    n = x.shape[1]
    p = S5_STATE
    nseq = n // cols_per_seq

    def to_rows(part):
        return jnp.concatenate([part[:, s * cols_per_seq:(s + 1) * cols_per_seq] for s in range(nseq)], axis=0).T

    def to_cols(rows):
        rt = rows.T
        return jnp.concatenate([rt[s * p:(s + 1) * p, :] for s in range(nseq)], axis=1)

    xr, xi = to_rows(x[:p, :]), to_rows(x[p:, :])
    pos = lax.broadcasted_iota(jnp.int32, (cols_per_seq, 1), 0)
    for lvl in range(n_levels):
        sh = 1 << lvl
        keep = pos >= sh
        sr = jnp.where(keep, pltpu.roll(xr, sh, axis=0), 0.0)
        si = jnp.where(keep, pltpu.roll(xi, sh, axis=0), 0.0)
        ar, ai = arow[lvl:lvl + 1, :], arow[n_levels + lvl:n_levels + lvl + 1, :]
        xr, xi = xr + ar * sr - ai * si, xi + ar * si + ai * sr
    keep = pos >= 1
    pr = jnp.where(keep, pltpu.roll(xr, 1, axis=0), 0.0)
    pi = jnp.where(keep, pltpu.roll(xi, 1, axis=0), 0.0)
    return jnp.concatenate([to_cols(pr), to_cols(pi)], axis=0).astype(BF16)


def _s5_body(u_ref, wt_ref, nt_ref, mt_ref, arow_ref, o_ref, *, cols_per_seq, n_levels):
    t, _, n = u_ref.shape
    cg = S5_GROUP_CH
    groups = range(S5_GPB)
    ch = [slice(gi * cg, (gi + 1) * cg) for gi in groups]
    us = [u_ref[:, ch[gi], :].reshape(t * cg, n) for gi in groups]
    xs = [_dot(nt_ref[gi], us[gi]) for gi in groups]
    ys = [_dot(wt_ref[gi], us[gi]) for gi in groups]
    prevs = [_s5_entering_state(xs[gi], arow_ref[gi], cols_per_seq=cols_per_seq, n_levels=n_levels)
             for gi in groups]
    for gi in groups:
        y = ys[gi] + _dot(mt_ref[gi], prevs[gi])
        o_ref[:, ch[gi], :] = y.astype(o_ref.dtype).reshape(t, cg, n)


def s5_apply(proj_t, wt_g, nt_g, mt_g, arow, *, groups, cols_per_seq, n_levels):
    t, _, n = proj_t.shape
    cg = S5_GROUP_CH * S5_GPB
    spec3 = lambda shp: pl.BlockSpec((S5_GPB,) + shp, lambda i: (i, 0, 0))
    return pl.pallas_call(
        functools.partial(_s5_body, cols_per_seq=cols_per_seq, n_levels=n_levels),
        grid=(groups // S5_GPB,),
        in_specs=[pl.BlockSpec((t, cg, n), lambda i: (0, i, 0)),
                  spec3(wt_g.shape[1:]), spec3(nt_g.shape[1:]), spec3(mt_g.shape[1:]), spec3(arow.shape[1:])],
        out_specs=pl.BlockSpec((t, cg, n), lambda i: (0, i, 0)),
        out_shape=jax.ShapeDtypeStruct((t, groups * S5_GROUP_CH, n), BF16),
        compiler_params=_cparams(("parallel",)),
        name="s5",
    )(proj_t, wt_g, nt_g, mt_g, arow)


def s5_operators(lam_re, lam_im, log_step, b_re, b_im, c_re, c_im, d, *, n_levels, n_seq):
    hp = lax.Precision.HIGHEST
    g, p = lam_re.shape
    cg, t = S5_GROUP_CH, S5_T
    lr = jnp.minimum(lam_re.astype(F32), -1e-4)
    li = lam_im.astype(F32)
    step = jnp.exp(log_step.astype(F32))[:, None]
    mag = jnp.exp(lr * step)
    lb_re, lb_im = mag * jnp.cos(li * step), mag * jnp.sin(li * step)
    den = lr * lr + li * li
    nr, ni = lb_re - 1.0, lb_im
    coef_re = (nr * lr + ni * li) / den
    coef_im = (ni * lr - nr * li) / den
    bb_re = coef_re[..., None] * b_re - coef_im[..., None] * b_im
    bb_im = coef_re[..., None] * b_im + coef_im[..., None] * b_re

    def power(m):
        mm = m.astype(F32)[None, :, None]
        mg = jnp.exp(lr[:, None, :] * step[:, None, :] * mm)
        ang = li[:, None, :] * step[:, None, :] * mm
        return mg * jnp.cos(ang), mg * jnp.sin(ang)

    pw_re, pw_im = power(jnp.arange(t + 1))
    pwt_re, pwt_im = pw_re.transpose(0, 2, 1), pw_im.transpose(0, 2, 1)

    def pow_times_b(pr, pi):
        re = pr[:, :, :, None] * bb_re[:, :, None, :] - pi[:, :, :, None] * bb_im[:, :, None, :]
        im = pr[:, :, :, None] * bb_im[:, :, None, :] + pi[:, :, :, None] * bb_re[:, :, None, :]
        return re.reshape(g, p, t * cg), im.reshape(g, p, t * cg)

    pb_re, pb_im = pow_times_b(pwt_re[:, :, :t], pwt_im[:, :, :t])
    taps = (jnp.einsum('gcp,gpy->gcy', c_re.astype(F32), pb_re, precision=hp)
            - jnp.einsum('gcp,gpy->gcy', c_im.astype(F32), pb_im, precision=hp))
    y_idx = jnp.arange(t * cg)
    taps = taps + d.astype(F32).reshape(g, cg, 1) * (y_idx[None, :] == jnp.arange(cg)[:, None]).astype(F32)[None]
    place = ((y_idx[None, :, None] // cg == jnp.arange(t)[:, None, None] - y_idx[None, None, :] // cg)
             & (y_idx[None, :, None] % cg == y_idx[None, None, :] % cg)).astype(BF16)
    wt_g = jnp.einsum('gcy,tyz->gtcz', taps.astype(BF16), place,
                      preferred_element_type=F32).astype(BF16).reshape(g, t * cg, t * cg)
    n_re, n_im = pow_times_b(pwt_re[:, :, t - 1::-1], pwt_im[:, :, t - 1::-1])
    nt_g = jnp.concatenate([n_re, n_im], axis=1).astype(BF16)
    cp_re = c_re[:, None] * pw_re[:, 1:, None, :] - c_im[:, None] * pw_im[:, 1:, None, :]
    cp_im = c_re[:, None] * pw_im[:, 1:, None, :] + c_im[:, None] * pw_re[:, 1:, None, :]
    mt_g = jnp.concatenate([cp_re, -cp_im], axis=-1).reshape(g, t * cg, 2 * p).astype(BF16)
    ar, ai = power(t * (2 ** jnp.arange(n_levels)))
    arow = jnp.concatenate([jnp.tile(ar, (1, 1, n_seq)), jnp.tile(ai, (1, 1, n_seq))], axis=1)
    return wt_g, nt_g, mt_g, arow


def _gelu_tanh(y):
    return 0.5 * y * (1.0 + jnp.tanh(math.sqrt(2.0 / math.pi) * (y + 0.044715 * (y * y * y))))


GLU_PARTS = 2


def _glu_body(y_ref, w_ref, b_ref, gate_ref, o_ref, z_ref, zb_ref, wt_ref):
    @pl.when((pl.program_id(0) == 0) & (pl.program_id(1) == 0))
    def _():
        _transpose_into(wt_ref, w_ref)

    tm = y_ref.shape[1]
    parts = [slice(c0, c0 + tm // GLU_PARTS) for c0 in range(0, tm, tm // GLU_PARTS)]
    for cols in parts:
        z = _gelu_tanh(y_ref[:, cols].astype(F32))
        z_ref[:, cols] = z
        zb_ref[:, cols] = z.astype(BF16)
    for cols in parts:
        for f0 in range(0, o_ref.shape[1], MM_SUB):
            rows = slice(f0, f0 + MM_SUB)
            lin = _dot(wt_ref[rows, :], zb_ref[:, cols]) + b_ref[rows, :]
            out = z_ref[rows, cols] * _sigmoid(lin) * _silu(gate_ref[rows, cols].astype(F32))
            o_ref[cols, rows] = out.T.astype(o_ref.dtype)


def glu_gate(y_t, w, b, proj_t, *, gate_row, tm=512):
    s, e, n = y_t.shape
    gb = gate_row // e
    return pl.pallas_call(
        _glu_body,
        grid=(s, n // tm),
        in_specs=[pl.BlockSpec((None, e, tm), lambda t, i: (t, 0, i)),
                  pl.BlockSpec((e, e), lambda t, i: (0, 0), pipeline_mode=pl.Buffered(1)),
                  pl.BlockSpec((e, 1), lambda t, i: (0, 0), pipeline_mode=pl.Buffered(1)),
                  pl.BlockSpec((None, e, tm), lambda t, i: (t, gb, i))],
        out_specs=pl.BlockSpec((tm, e), lambda t, i: (i, t)),
        out_shape=jax.ShapeDtypeStruct((n, s * e), BF16),
        scratch_shapes=[pltpu.VMEM((e, tm), F32), pltpu.VMEM((e, tm), BF16), pltpu.VMEM((e, e), BF16)],
        compiler_params=_cparams(("arbitrary", "arbitrary")),
        name="glu",
    )(y_t, w, b.reshape(e, 1), proj_t)


def kernel(x, ab_norm_g, ab_w_in, gla_alpha_up, gla_alpha_b, gla_head_g, fox_f_b, ab_w_out, c_norm_g, c_w_in, s5_lambda_re, s5_lambda_im, s5_log_step, s5_b_re, s5_b_im, s5_c_re, s5_c_im, s5_d, glu_w, glu_b, c_w_out, final_norm_g):
    batch, seq, d = x.shape
    t = batch * seq
    x2 = x.reshape(t, d)

    hk, hv, fw = GLA_HEADS * GLA_DK, GLA_HEADS * GLA_DV, FOX_HEADS * FOX_DH
    sizes = (hk, hk, hv, GLA_LOWRANK, hv, fw, fw, fw, FOX_HEADS, fw)
    offs = [0]
    for s in sizes:
        offs.append(offs[-1] + s)
    w_main, w_small = ab_weight_prep(jnp.swapaxes(ab_w_in, 1, 2), offs)
    c_gq, c_gk, c_gv, c_gg = 0, hk, 2 * hk, 2 * hk + hv
    c_fq = c_gg + hv
    c_fk, c_fv, c_fg = c_fq + fw, c_fq + 2 * fw, c_fq + 3 * fw

    proj, small = norm_proj(x2, ab_norm_g[0], w_main, w_small, tn=w_main.shape[0] // 2, name="ab_in")

    o_gla = gla(proj, small, gla_alpha_up[0], gla_alpha_b[0], gla_head_g[0],
                batch=batch, seq=seq, q_col=c_gq, k_col=c_gk, v_col=c_gv, gate_col=c_gg)

    fb_row = jnp.zeros((1, 128), F32).at[0, GLA_LOWRANK:GLA_LOWRANK + FOX_HEADS].set(fox_f_b[0])
    qa, ka = fox_gate(small, fb_row, batch=batch, seq=seq, col0=GLA_LOWRANK)
    o_fox = fox(proj, qa, ka, batch=batch, seq=seq, q_col=c_fq, k_col=c_fk, v_col=c_fv, gate_col=c_fg)

    x1, h1g = outproj_mid(o_gla, o_fox, ab_w_out[0], x2, c_norm_g[0], steps=S5_T, name="ab_out")

    e = d
    groups = e // S5_GROUP_CH
    n_chunks = t // S5_T
    cols_per_seq = seq // S5_T
    n_levels = max(1, (cols_per_seq - 1).bit_length())
    proj_t = matmul_t_grouped(c_w_in[0], h1g, S5_T, BF16, name="c_in")
    ops = s5_operators(s5_lambda_re[0], s5_lambda_im[0], s5_log_step[0], s5_b_re[0], s5_b_im[0],
                       s5_c_re[0], s5_c_im[0], s5_d[0], n_levels=n_levels, n_seq=batch)
    y_t = s5_apply(proj_t, *ops, groups=groups, cols_per_seq=cols_per_seq, n_levels=n_levels)
    zz = glu_gate(y_t, glu_w[0], glu_b[0], proj_t, gate_row=e)
    out = outproj_final(zz, c_w_out[0], x1, final_norm_g, steps=S5_T, name="c_out")
    return out.reshape(batch, seq, d)
```

```python
import functools
import math

import jax
import jax.numpy as jnp
from jax import lax
from jax.experimental import pallas as pl
from jax.experimental.pallas import tpu as pltpu

EPS = 1e-6
F32 = jnp.float32
BF16 = jnp.bfloat16

GLA_HEADS = 4
GLA_DK = 128
GLA_DV = 256
GLA_LOWRANK = 16
GLA_TAU = 16.0
GLA_CHUNK = 64
GLA_SUB = 8
GLA_UNROLL = 2
FOX_HEADS = 8
FOX_DH = 128
FOX_AUG = 128
FOX_AUG_GROUP = 8
S5_GROUP_CH = 16
S5_STATE = 64
S5_T = 16

VMEM_LIMIT = 56 * 1024 * 1024

NT_DIMS = (((1,), (1,)), ((), ()))
TN_DIMS = (((0,), (0,)), ((), ()))
LOG2E = math.log2(math.e)


def _cparams(sem):
    return pltpu.CompilerParams(dimension_semantics=sem, vmem_limit_bytes=VMEM_LIMIT)


def _dot(a, b):
    return jnp.dot(a, b, preferred_element_type=F32)


def _dot_nt(a, b):
    return lax.dot_general(a, b, NT_DIMS, preferred_element_type=F32)


def _dot_tn(a, b):
    return lax.dot_general(a, b, TN_DIMS, preferred_element_type=F32)


def _log_sigmoid(z):
    return -(jnp.maximum(-z, 0.0) + jnp.log1p(jnp.exp(-jnp.abs(z))))


def _sigmoid(z):
    return 0.5 * jnp.tanh(0.5 * z) + 0.5


def _silu(z):
    return z * _sigmoid(z)


def _split3(x):
    hi = x.astype(BF16)
    r1 = x - hi.astype(F32)
    mid = r1.astype(BF16)
    lo = (r1 - mid.astype(F32)).astype(BF16)
    return hi, mid, lo


def _norm_proj_body(x_ref, g_ref, wt_ref, ws_ref, o_ref, s_ref):
    x = x_ref[...]
    ms = jnp.mean(x * x, axis=-1, keepdims=True)
    h = (x * lax.rsqrt(ms + EPS) * g_ref[...]).astype(BF16)
    for c0 in range(0, o_ref.shape[1], MM_SUB):
        o_ref[:, c0:c0 + MM_SUB] = _dot_nt(h, wt_ref[c0:c0 + MM_SUB, :]).astype(o_ref.dtype)
    s_ref[...] = _dot_nt(h, ws_ref[...])


def norm_proj(x, g, wt, wt_small, tm=512, tn=512, name="norm_proj"):
    m, d = x.shape
    n = wt.shape[0]
    ns = wt_small.shape[0]
    proj, small = pl.pallas_call(
        _norm_proj_body,
        grid=(n // tn, m // tm),
        in_specs=[pl.BlockSpec((tm, d), lambda j, i: (i, 0)),
                  pl.BlockSpec((1, d), lambda j, i: (0, 0)),
                  pl.BlockSpec((tn, d), lambda j, i: (j, 0)),
                  pl.BlockSpec((ns, d), lambda j, i: (0, 0))],
        out_specs=[pl.BlockSpec((tm, tn), lambda j, i: (i, j)),
                   pl.BlockSpec((None, tm, ns), lambda j, i: (j, i, 0))],
        out_shape=[jax.ShapeDtypeStruct((m, n), BF16), jax.ShapeDtypeStruct((n // tn, m, ns), F32)],
        compiler_params=_cparams(("arbitrary", "arbitrary")),
        name=name,
    )(x, g.reshape(1, d), wt, wt_small)
    return proj, small[0]


def _ab_wprep_body(w_ref, o_ref, s_ref, *, offs):
    w = w_ref[0]
    seg = lambda n: w[offs[n]:offs[n + 1], :]
    o_ref[...] = jnp.concatenate(
        [w[:offs[3], :], seg(4), seg(5) * (FOX_DH ** -0.5 * LOG2E), w[offs[6]:offs[8], :], seg(9)],
        axis=0).astype(o_ref.dtype)
    pad = s_ref.shape[0] - (offs[4] - offs[3]) - (offs[9] - offs[8])
    s_ref[...] = jnp.concatenate([seg(3), seg(8), jnp.zeros((pad, w.shape[1]), F32)], axis=0).astype(s_ref.dtype)


def ab_weight_prep(w_t3, offs, tk=256):
    _, n_in, d = w_t3.shape
    n_main = n_in - (offs[4] - offs[3]) - (offs[9] - offs[8])
    return pl.pallas_call(
        functools.partial(_ab_wprep_body, offs=tuple(offs)),
        grid=(d // tk,),
        in_specs=[pl.BlockSpec((1, n_in, tk), lambda i: (0, 0, i))],
        out_specs=[pl.BlockSpec((n_main, tk), lambda i: (0, i)), pl.BlockSpec((128, tk), lambda i: (0, i))],
        out_shape=[jax.ShapeDtypeStruct((n_main, d), BF16), jax.ShapeDtypeStruct((128, d), BF16)],
        compiler_params=_cparams(("parallel",)),
        name="ab_wprep",
    )(w_t3)


MM_SUB = 512


def _transpose_into(wt_ref, w_ref):
    for c0 in range(0, w_ref.shape[1], MM_SUB):
        wt_ref[c0:c0 + MM_SUB, :] = w_ref[:, c0:c0 + MM_SUB].astype(wt_ref.dtype).T


def _mm_t_body(w_ref, a_ref, o_ref, wt_ref):
    @pl.when((pl.program_id(1) == 0) & (pl.program_id(2) == 0))
    def _():
        _transpose_into(wt_ref, w_ref)

    a = a_ref[...]
    for r0 in range(0, o_ref.shape[0], MM_SUB):
        o_ref[r0:r0 + MM_SUB, :] = _dot_nt(wt_ref[r0:r0 + MM_SUB, :], a).astype(o_ref.dtype)


def matmul_t_grouped(w, a2, steps, out_dtype, tm=512, tn=2048, name="matmul_tg"):
    k, nf = w.shape
    n = a2.shape[0]
    return pl.pallas_call(
        _mm_t_body,
        grid=(nf // tn, steps, n // tm),
        in_specs=[pl.BlockSpec((k, tn), lambda j, t, i: (0, j), pipeline_mode=pl.Buffered(1)),
                  pl.BlockSpec((tm, k), lambda j, t, i: (i, t))],
        out_specs=pl.BlockSpec((None, tn, tm), lambda j, t, i: (t, j, i)),
        out_shape=jax.ShapeDtypeStruct((steps, nf, n), out_dtype),
        scratch_shapes=[pltpu.VMEM((tn, k), BF16)],
        compiler_params=_cparams(("arbitrary", "arbitrary", "arbitrary")),
        name=name,
    )(w, a2)


def _gla_body(q_ref, k_ref, v_ref, glr_ref, aup_ref, ab_ref, gate_ref, hg_ref, o_ref,
              s_ref, b_ref, kf_ref, kts_ref, dec_ref, *, tb, hpb):
    ib = pl.program_id(2)
    C, SB, dk, dv = GLA_CHUNK, GLA_SUB, GLA_DK, GLA_DV

    @pl.when(ib == 0)
    def _():
        s_ref[...] = jnp.zeros_like(s_ref)

    glr = glr_ref[:, :GLA_LOWRANK].astype(BF16)
    tw = 2 * C
    r = lax.broadcasted_iota(jnp.int32, (tw, tw), 0)
    c = lax.broadcasted_iota(jnp.int32, (tw, tw), 1)
    tri = jnp.where((r // C == c // C) & (c <= r), 1.0, 0.0).astype(BF16)
    nc = tb // C
    heads = range(hpb)
    kcs = [slice(hh * dk, (hh + 1) * dk) for hh in heads]
    zs = [_dot(glr, aup_ref[:, kcs[hh]].astype(BF16)) + ab_ref[:, kcs[hh]] for hh in heads]
    b_blks = []
    for hh in heads:
        la = _log_sigmoid(zs[hh]) * (1.0 / GLA_TAU)
        hi = la.astype(BF16)
        lo = (la - hi.astype(F32)).astype(BF16)
        b_blks.append(jnp.concatenate(
            [_dot(tri, hi[r0:r0 + tw]) + _dot(tri, lo[r0:r0 + tw]) for r0 in range(0, tb, tw)], axis=0))
    for hh in heads:
        b_blk = b_blks[hh]
        k_blk = k_ref[:, kcs[hh]].astype(F32)
        b_ref[hh] = b_blk
        kf_ref[hh] = k_blk
        b3 = b_blk.reshape(nc, C, dk)
        b_end = b3[:, C - 1:C, :]
        k_end = (k_blk.reshape(nc, C, dk) * jnp.exp(b_end - b3)).reshape(tb, dk)
        kt_end = k_end.T.astype(BF16)
        dec_t = jnp.exp(b_end.reshape(nc, dk)).T
        for ci in range(nc):
            kts_ref[hh, ci] = kt_end[:, ci * C:(ci + 1) * C]
            dec_ref[hh, ci] = jnp.broadcast_to(dec_t[:, ci:ci + 1], (dk, dv))

    row = lax.broadcasted_iota(jnp.int32, (C, 1), 0)
    rowi = lax.broadcasted_iota(jnp.int32, (C, C), 0)
    coli = lax.broadcasted_iota(jnp.int32, (C, C), 1)
    lane_c = lax.broadcasted_iota(jnp.int32, (SB, C), 1)
    sub_r = lax.broadcasted_iota(jnp.int32, (SB, 1), 0)
    ones = jnp.ones((GLA_DK, C), BF16)
    neg = -jnp.inf
    scale = GLA_DK ** -0.5

    def stage_state(ci, r0, hh):
        kc = slice(hh * dk, (hh + 1) * dk)
        vc = slice(hh * dv, (hh + 1) * dv)
        q = q_ref[pl.ds(r0, C), kc].astype(F32) * scale
        k = kf_ref[hh, pl.ds(r0, C), :]
        b = b_ref[hh, pl.ds(r0, C), :]
        v = v_ref[pl.ds(r0, C), vc]

        s = s_ref[hh]
        o = _dot((q * jnp.exp(b)).astype(BF16), s.astype(BF16))
        s_ref[hh] = s * dec_ref[hh, ci] + _dot(kts_ref[hh, ci], v)

        attn = jnp.zeros((C, C), F32)
        h = C // 2
        while h >= SB:
            ref = jnp.broadcast_to(b.reshape(C // (2 * h), 2 * h, GLA_DK)[:, h - 1:h, :],
                                   (C // (2 * h), 2 * h, GLA_DK)).reshape(C, GLA_DK)
            upper = (row // h) % 2 == 1
            q_h = q * jnp.exp(jnp.where(upper, b - ref, neg))
            k_h = k * jnp.exp(jnp.where(upper, neg, ref - b))
            a_h = _dot_nt(q_h.astype(BF16), k_h.astype(BF16))
            attn = attn + (a_h if 2 * h == C else jnp.where(rowi // (2 * h) == coli // (2 * h), a_h, 0.0))
            h //= 2
        return q, b, v, o, attn

    def stage_diag(r0, hh, q, b):
        zs = []
        for bi in range(C // SB):
            s0 = bi * SB
            q_i = q[s0:s0 + SB, :]
            b_i = b[s0:s0 + SB, :]
            for j in range(SB):
                k_j = kf_ref[hh, pl.ds(r0 + (s0 + j), 1), :]
                b_j = b_ref[hh, pl.ds(r0 + (s0 + j), 1), :]
                zs.append(q_i * k_j * jnp.exp(jnp.where(sub_r >= j, b_i - b_j, neg)))
        return _dot(jnp.concatenate(zs, axis=0).astype(BF16), ones)

    def stage_out(r0, hh, v, o, attn, zsum):
        vc = slice(hh * dv, (hh + 1) * dv)
        diag = []
        for bi in range(C // SB):
            acc = jnp.zeros((SB, C), F32)
            for j in range(SB):
                n0 = (bi * SB + j) * SB
                acc = acc + jnp.where(lane_c == bi * SB + j, zsum[n0:n0 + SB, :], 0.0)
            diag.append(acc)
        attn = attn + jnp.concatenate(diag, axis=0)
        o = o + _dot(attn.astype(BF16), v)

        o = o * lax.rsqrt(jnp.mean(o * o, axis=-1, keepdims=True) + EPS) * hg_ref[:, vc]
        g = gate_ref[pl.ds(r0, C), vc].astype(F32)
        o_ref[pl.ds(r0, C), vc] = (o * _silu(g)).astype(o_ref.dtype)

    def chunk(ci, carry):
        r0 = pl.multiple_of(ci * C, C)
        heads = range(hpb)
        st = [stage_state(ci, r0, hh) for hh in heads]
        zsums = [stage_diag(r0, hh, st[hh][0], st[hh][1]) for hh in heads]
        for hh in heads:
            _, _, v, o, attn = st[hh]
            stage_out(r0, hh, v, o, attn, zsums[hh])
        return carry

    lax.fori_loop(0, tb // C, chunk, 0, unroll=GLA_UNROLL)


def gla(proj, small, alpha_up, alpha_b, head_g, *, batch, seq, q_col, k_col, v_col, gate_col, tb=512, hpb=4):
    t = batch * seq
    nb = seq // tb
    dk, dv, h = GLA_DK * hpb, GLA_DV * hpb, GLA_HEADS // hpb
    qb, kb, vb, gb = q_col // dk, k_col // dk, v_col // dv, gate_col // dv
    tok = lambda b, hh, i: b * nb + i
    nc = tb // GLA_CHUNK
    return pl.pallas_call(
        functools.partial(_gla_body, tb=tb, hpb=hpb),
        grid=(batch, h, nb),
        in_specs=[
            pl.BlockSpec((tb, dk), lambda b, hh, i: (tok(b, hh, i), qb + hh)),
            pl.BlockSpec((tb, dk), lambda b, hh, i: (tok(b, hh, i), kb + hh)),
            pl.BlockSpec((tb, dv), lambda b, hh, i: (tok(b, hh, i), vb + hh)),
            pl.BlockSpec((tb, 128), lambda b, hh, i: (tok(b, hh, i), 0)),
            pl.BlockSpec((GLA_LOWRANK, dk), lambda b, hh, i: (0, hh)),
            pl.BlockSpec((1, dk), lambda b, hh, i: (0, hh)),
            pl.BlockSpec((tb, dv), lambda b, hh, i: (tok(b, hh, i), gb + hh)),
            pl.BlockSpec((1, dv), lambda b, hh, i: (0, hh)),
        ],
        out_specs=pl.BlockSpec((tb, dv), lambda b, hh, i: (tok(b, hh, i), hh)),
        out_shape=jax.ShapeDtypeStruct((t, GLA_HEADS * GLA_DV), BF16),
        scratch_shapes=[pltpu.VMEM((hpb, GLA_DK, GLA_DV), F32), pltpu.VMEM((hpb, tb, GLA_DK), F32),
                        pltpu.VMEM((hpb, tb, GLA_DK), F32),
                        pltpu.VMEM((hpb, nc, GLA_DK, GLA_CHUNK), BF16),
                        pltpu.VMEM((hpb, nc, GLA_DK, GLA_DV), F32)],
        compiler_params=_cparams(("parallel", "parallel", "arbitrary")),
        name="gla",
    )(proj, proj, proj, small, alpha_up, alpha_b.reshape(1, -1), proj, head_g.reshape(1, -1))


def _fox_gate_body(s_ref, fb_ref, qa_ref, ka_ref, *, blk, col0):
    n = s_ref.shape[0] // blk
    r = lax.broadcasted_iota(jnp.int32, (blk, blk), 0)
    c = lax.broadcasted_iota(jnp.int32, (blk, blk), 1)
    tri = jnp.where(c <= r, 1.0, 0.0).astype(BF16)
    lanes = s_ref.shape[1]
    pr = lax.broadcasted_iota(jnp.int32, (3 * lanes, FOX_AUG), 0)
    pc = lax.broadcasted_iota(jnp.int32, (3 * lanes, FOX_AUG), 1)
    head, piece = pr % lanes - col0, pr // lanes
    is_head = (head >= 0) & (head < FOX_HEADS)
    sel_q = jnp.where(is_head & (pc == head * FOX_AUG_GROUP + piece), 1.0, 0.0).astype(BF16)
    sel_k = jnp.where(is_head & (pc == head * FOX_AUG_GROUP + 3 + piece), -1.0, 0.0).astype(BF16)
    lane = lax.broadcasted_iota(jnp.int32, (1, FOX_AUG), 1)
    used = lane < FOX_HEADS * FOX_AUG_GROUP
    one_q = jnp.where(used & (lane % FOX_AUG_GROUP >= 3) & (lane % FOX_AUG_GROUP < 6), 1.0, 0.0)
    one_k = jnp.where(used & (lane % FOX_AUG_GROUP < 3), 1.0, 0.0)

    def step(i, carry):
        r0 = pl.multiple_of(i * blk, blk)
        lf = _log_sigmoid(s_ref[pl.ds(r0, blk), :] + fb_ref[...])
        hi, mid, lo = _split3(lf)
        cs = _dot(tri, hi) + _dot(tri, mid) + _dot(tri, lo) + carry
        pieces = jnp.concatenate(_split3(cs * LOG2E), axis=1)
        qa_ref[pl.ds(r0, blk), :] = (_dot(pieces, sel_q) + one_q).astype(BF16)
        ka_ref[pl.ds(r0, blk), :] = (_dot(pieces, sel_k) + one_k).astype(BF16)
        return cs[blk - 1:blk, :]

    lax.fori_loop(0, n, step, jnp.zeros((1, s_ref.shape[1]), F32))


def fox_gate(small, fb_row, *, batch, seq, col0, blk=256):
    shp = jax.ShapeDtypeStruct((batch * seq, FOX_AUG), BF16)
    spec = pl.BlockSpec((seq, FOX_AUG), lambda b: (b, 0))
    return pl.pallas_call(
        functools.partial(_fox_gate_body, blk=blk, col0=col0),
        grid=(batch,),
        in_specs=[pl.BlockSpec((seq, 128), lambda b: (b, 0)),
                  pl.BlockSpec((1, 128), lambda b: (0, 0))],
        out_specs=[spec, spec],
        out_shape=[shp, shp],
        compiler_params=_cparams(("parallel",)),
        name="fox_gate",
    )(small, fb_row)


def _fox_body(it_ref, jt_ref, q_ref, qa_ref, k_ref, ka_ref, v_ref, gate_ref, o_ref,
              m_ref, l_ref, acc_ref, *, tq, tk, hpb):
    p = pl.program_id(2)
    i = it_ref[p]
    j = jt_ref[p]
    d = FOX_DH

    @pl.when(j == 0)
    def _():
        m_ref[...] = jnp.full_like(m_ref, -jnp.inf)
        l_ref[...] = jnp.zeros_like(l_ref)
        acc_ref[...] = jnp.zeros_like(acc_ref)

    aug_lane = lax.broadcasted_iota(jnp.int32, (1, FOX_AUG), 1)

    def scores(item, masked):
        hh, ks, qs = item
        mine = aug_lane // FOX_AUG_GROUP == pl.program_id(1) * hpb + hh
        zero = jnp.zeros((), BF16)
        q_aug = jnp.concatenate([q_ref[qs, hh * d:(hh + 1) * d], jnp.where(mine, qa_ref[qs, :], zero)], axis=1)
        k_aug = jnp.concatenate([k_ref[ks, hh * d:(hh + 1) * d], jnp.where(mine, ka_ref[ks, :], zero)], axis=1)
        st = _dot_nt(k_aug, q_aug)
        if masked:
            kr = lax.broadcasted_iota(jnp.int32, st.shape, 0) + ks.start
            qc = lax.broadcasted_iota(jnp.int32, st.shape, 1) + qs.start
            st = jnp.where(qc >= kr, st, -jnp.inf)
        return st

    def absorb(item, st):
        hh, ks, qs = item
        m_old = m_ref[hh, :, qs]
        m_new = jnp.maximum(m_old, jnp.max(st, axis=0, keepdims=True))
        alpha = jnp.exp2(m_old - m_new)
        pt = jnp.exp2(st - m_new)
        l_ref[hh, :, qs] = alpha * l_ref[hh, :, qs] + jnp.sum(pt, axis=0, keepdims=True)
        acc_ref[hh, :, qs] = (alpha * acc_ref[hh, :, qs]
                              + _dot_tn(v_ref[ks, hh * d:(hh + 1) * d], pt.astype(BF16)))
        m_ref[hh, :, qs] = m_new

    def block(masked, finish):
        full = slice(0, tq)
        halves = [(slice(0, tk // 2), full), (slice(tk // 2, tk), slice(tq // 2, tq))] if masked else [(slice(0, tk), full)]
        items = [(hh, ks, qs) for hh in range(hpb) for ks, qs in halves]
        st = scores(items[0], masked)
        for n, item in enumerate(items):
            st_next = scores(items[n + 1], masked) if n + 1 < len(items) else None
            absorb(item, st)
            hh = item[0]
            if finish and (n + 1 == len(items) or items[n + 1][0] != hh):
                o = (acc_ref[hh] / l_ref[hh]).T
                g = gate_ref[:, hh * d:(hh + 1) * d].astype(F32)
                o_ref[:, hh * d:(hh + 1) * d] = (o * _silu(g)).astype(o_ref.dtype)
            st = st_next

    @pl.when(j < i)
    def _():
        block(False, False)

    @pl.when(j == i)
    def _():
        block(True, True)


def fox(proj, qa, ka, *, batch, seq, q_col, k_col, v_col, gate_col, t=1024, hpb=4):
    tt = batch * seq
    nb = seq // t
    d, h = FOX_DH, FOX_HEADS
    w = hpb * d
    qb, kb, gb, vb = q_col // w, k_col // w, gate_col // w, v_col // w
    pairs = [(i, j) for i in range(nb) for j in range(i + 1)]
    it = jnp.array([p[0] for p in pairs], jnp.int32)
    jt = jnp.array([p[1] for p in pairs], jnp.int32)
    grid_spec = pltpu.PrefetchScalarGridSpec(
        num_scalar_prefetch=2,
        grid=(batch, h // hpb, len(pairs)),
        in_specs=[
            pl.BlockSpec((t, w), lambda b, hh, p, it, jt: (b * nb + it[p], qb + hh)),
            pl.BlockSpec((t, FOX_AUG), lambda b, hh, p, it, jt: (b * nb + it[p], 0)),
            pl.BlockSpec((t, w), lambda b, hh, p, it, jt: (b * nb + jt[p], kb + hh)),
            pl.BlockSpec((t, FOX_AUG), lambda b, hh, p, it, jt: (b * nb + jt[p], 0)),
            pl.BlockSpec((t, w), lambda b, hh, p, it, jt: (b * nb + jt[p], vb + hh)),
            pl.BlockSpec((t, w), lambda b, hh, p, it, jt: (b * nb + it[p], gb + hh)),
        ],
        out_specs=pl.BlockSpec((t, w), lambda b, hh, p, it, jt: (b * nb + it[p], hh)),
        scratch_shapes=[pltpu.VMEM((hpb, 1, t), F32), pltpu.VMEM((hpb, 1, t), F32), pltpu.VMEM((hpb, d, t), F32)],
    )
    return pl.pallas_call(
        functools.partial(_fox_body, tq=t, tk=t, hpb=hpb),
        grid_spec=grid_spec,
        out_shape=jax.ShapeDtypeStruct((tt, h * d), BF16),
        compiler_params=_cparams(("parallel", "parallel", "arbitrary")),
        name="fox",
    )(it, jt, proj, qa, proj, ka, proj, proj)


def _regroup_perm(tm, steps, to_grouped):
    r = lax.broadcasted_iota(jnp.int32, (tm, tm), 0)
    c = lax.broadcasted_iota(jnp.int32, (tm, tm), 1)
    nc = tm // steps
    src = (r % nc) * steps + r // nc if to_grouped else (r % steps) * nc + r // steps
    return jnp.where(c == src, 1.0, 0.0).astype(BF16)


def _cast_weight_once(wb_ref, w_ref):
    @pl.when(pl.program_id(0) == 0)
    def _():
        for r0 in range(0, w_ref.shape[0], MM_SUB):
            wb_ref[r0:r0 + MM_SUB, :] = w_ref[r0:r0 + MM_SUB, :].astype(wb_ref.dtype)


def _outproj_mid_body(a1_ref, a2_ref, wf_ref, x_ref, g_ref, r_ref, h_ref, w_ref, *, steps):
    k1 = a1_ref.shape[1]
    tm, d = x_ref.shape
    nc = tm // steps
    _cast_weight_once(w_ref, wf_ref)
    acc = x_ref[...] + _dot(a1_ref[...], w_ref[:k1, :]) + _dot(a2_ref[...], w_ref[k1:, :])
    r_ref[...] = acc
    normed = (acc * lax.rsqrt(jnp.mean(acc * acc, axis=-1, keepdims=True) + EPS) * g_ref[...]).astype(BF16)
    by_step = _dot(_regroup_perm(tm, steps, True), normed).astype(h_ref.dtype)
    for s in range(steps):
        h_ref[:, s * d:(s + 1) * d] = by_step[s * nc:(s + 1) * nc, :]


def outproj_mid(a1, a2, w, x, g, *, steps, tm=256, name="outproj_mid"):
    m, d = x.shape
    row = pl.BlockSpec((tm, d), lambda i: (i, 0))
    return pl.pallas_call(
        functools.partial(_outproj_mid_body, steps=steps),
        grid=(m // tm,),
        in_specs=[pl.BlockSpec((tm, a1.shape[1]), lambda i: (i, 0)),
                  pl.BlockSpec((tm, a2.shape[1]), lambda i: (i, 0)),
                  pl.BlockSpec(w.shape, lambda i: (0, 0), pipeline_mode=pl.Buffered(1)),
                  row,
                  pl.BlockSpec((1, d), lambda i: (0, 0))],
        out_specs=[row, pl.BlockSpec((tm // steps, steps * d), lambda i: (i, 0))],
        out_shape=[jax.ShapeDtypeStruct((m, d), F32), jax.ShapeDtypeStruct((m // steps, steps * d), BF16)],
        scratch_shapes=[pltpu.VMEM(w.shape, BF16)],
        compiler_params=_cparams(("arbitrary",)),
        name=name,
    )(a1, a2, w, x, g.reshape(1, d))


def _outproj_final_body(a_ref, wf_ref, x_ref, g_ref, o_ref, w_ref, *, steps):
    tm = x_ref.shape[0]
    e = w_ref.shape[0]
    by_step = jnp.concatenate([a_ref[:, s * e:(s + 1) * e] for s in range(steps)], axis=0)
    a_tok = _dot(_regroup_perm(tm, steps, False), by_step).astype(BF16)
    _cast_weight_once(w_ref, wf_ref)
    acc = x_ref[...] + _dot(a_tok, w_ref[...])
    o_ref[...] = acc * lax.rsqrt(jnp.mean(acc * acc, axis=-1, keepdims=True) + EPS) * g_ref[...]


def outproj_final(a_grouped, w, x, g, *, steps, tm=256, name="outproj_final"):
    m, d = x.shape
    e = w.shape[0]
    row = pl.BlockSpec((tm, d), lambda i: (i, 0))
    return pl.pallas_call(
        functools.partial(_outproj_final_body, steps=steps),
        grid=(m // tm,),
        in_specs=[pl.BlockSpec((tm // steps, steps * e), lambda i: (i, 0)),
                  pl.BlockSpec(w.shape, lambda i: (0, 0), pipeline_mode=pl.Buffered(1)),
                  row,
                  pl.BlockSpec((1, d), lambda i: (0, 0))],
        out_specs=row,
        out_shape=jax.ShapeDtypeStruct((m, d), F32),
        scratch_shapes=[pltpu.VMEM(w.shape, BF16)],
        compiler_params=_cparams(("arbitrary",)),
        name=name,
    )(a_grouped, w, x, g.reshape(1, d))


S5_SCAN_BLK = 8
S5_GPB = 4


def _s5_entering_state(x, arow, *, cols_per_seq, n_levels):
    n = x.shape[1]
    p = S5_STATE
    nseq = n // cols_per_seq

    def to_rows(part):
        return jnp.concatenate([part[:, s * cols_per_seq:(s + 1) * cols_per_seq] for s in range(nseq)], axis=0).T

    def to_cols(rows):
        rt = rows.T
        return jnp.concatenate([rt[s * p:(s + 1) * p, :] for s in range(nseq)], axis=1)

    xr, xi = to_rows(x[:p, :]), to_rows(x[p:, :])
    pos = lax.broadcasted_iota(jnp.int32, (cols_per_seq, 1), 0)
    blk = S5_SCAN_BLK
    half = arow.shape[0] // 2
    for lvl in range(n_levels):
        sh = 1 << lvl
        keep = pos % blk >= sh
        sr = jnp.where(keep, pltpu.roll(xr, sh, axis=0), 0.0)
        si = jnp.where(keep, pltpu.roll(xi, sh, axis=0), 0.0)
        ar, ai = arow[lvl:lvl + 1, :], arow[half + lvl:half + lvl + 1, :]
        xr, xi = xr + ar * sr - ai * si, xi + ar * si + ai * sr
    pw_r, pw_i = arow[n_levels:n_levels + blk, :], arow[half + n_levels:half + n_levels + blk, :]
    out_r, out_i = [xr[:blk, :]], [xi[:blk, :]]
    for r0 in range(blk, cols_per_seq, blk):
        cr, ci = out_r[-1][blk - 1:blk, :], out_i[-1][blk - 1:blk, :]
        out_r.append(xr[r0:r0 + blk, :] + pw_r * cr - pw_i * ci)
        out_i.append(xi[r0:r0 + blk, :] + pw_r * ci + pw_i * cr)
    xr, xi = jnp.concatenate(out_r, axis=0), jnp.concatenate(out_i, axis=0)
    keep = pos >= 1
    pr = jnp.where(keep, pltpu.roll(xr, 1, axis=0), 0.0)
    pi = jnp.where(keep, pltpu.roll(xi, 1, axis=0), 0.0)
    return jnp.concatenate([to_cols(pr), to_cols(pi)], axis=0).astype(BF16)


def _s5_body(u_ref, wt_ref, nt_ref, mt_ref, arow_ref, o_ref, *, cols_per_seq, n_levels):
    t, _, n = u_ref.shape
    cg = S5_GROUP_CH
    groups = range(S5_GPB)
    ch = [slice(gi * cg, (gi + 1) * cg) for gi in groups]
    us = [u_ref[:, ch[gi], :].reshape(t * cg, n) for gi in groups]
    xs = [_dot(nt_ref[gi], us[gi]) for gi in groups]
    ys = [_dot(wt_ref[gi], us[gi]) for gi in groups]
    prevs = [_s5_entering_state(xs[gi], arow_ref[gi], cols_per_seq=cols_per_seq, n_levels=n_levels)
             for gi in groups]
    for gi in groups:
        y = ys[gi] + _dot(mt_ref[gi], prevs[gi])
        o_ref[:, ch[gi], :] = y.astype(o_ref.dtype).reshape(t, cg, n)


def s5_apply(proj_t, wt_g, nt_g, mt_g, arow, *, groups, cols_per_seq, n_levels):
    t, _, n = proj_t.shape
    cg = S5_GROUP_CH * S5_GPB
    spec3 = lambda shp: pl.BlockSpec((S5_GPB,) + shp, lambda i: (i, 0, 0))
    return pl.pallas_call(
        functools.partial(_s5_body, cols_per_seq=cols_per_seq, n_levels=n_levels),
        grid=(groups // S5_GPB,),
        in_specs=[pl.BlockSpec((t, cg, n), lambda i: (0, i, 0)),
                  spec3(wt_g.shape[1:]), spec3(nt_g.shape[1:]), spec3(mt_g.shape[1:]), spec3(arow.shape[1:])],
        out_specs=pl.BlockSpec((t, cg, n), lambda i: (0, i, 0)),
        out_shape=jax.ShapeDtypeStruct((t, groups * S5_GROUP_CH, n), BF16),
        compiler_params=_cparams(("parallel",)),
        name="s5",
    )(proj_t, wt_g, nt_g, mt_g, arow)


def s5_operators(lam_re, lam_im, log_step, b_re, b_im, c_re, c_im, d, *, n_levels, n_seq):
    hp = lax.Precision.HIGHEST
    g, p = lam_re.shape
    cg, t = S5_GROUP_CH, S5_T
    lr = jnp.minimum(lam_re.astype(F32), -1e-4)
    li = lam_im.astype(F32)
    step = jnp.exp(log_step.astype(F32))[:, None]
    mag = jnp.exp(lr * step)
    lb_re, lb_im = mag * jnp.cos(li * step), mag * jnp.sin(li * step)
    den = lr * lr + li * li
    nr, ni = lb_re - 1.0, lb_im
    coef_re = (nr * lr + ni * li) / den
    coef_im = (ni * lr - nr * li) / den
    bb_re = coef_re[..., None] * b_re - coef_im[..., None] * b_im
    bb_im = coef_re[..., None] * b_im + coef_im[..., None] * b_re

    def power(m):
        mm = m.astype(F32)[None, :, None]
        mg = jnp.exp(lr[:, None, :] * step[:, None, :] * mm)
        ang = li[:, None, :] * step[:, None, :] * mm
        return mg * jnp.cos(ang), mg * jnp.sin(ang)

    pw_re, pw_im = power(jnp.arange(t + 1))
    pwt_re, pwt_im = pw_re.transpose(0, 2, 1), pw_im.transpose(0, 2, 1)

    def pow_times_b(pr, pi):
        re = pr[:, :, :, None] * bb_re[:, :, None, :] - pi[:, :, :, None] * bb_im[:, :, None, :]
        im = pr[:, :, :, None] * bb_im[:, :, None, :] + pi[:, :, :, None] * bb_re[:, :, None, :]
        return re.reshape(g, p, t * cg), im.reshape(g, p, t * cg)

    pb_re, pb_im = pow_times_b(pwt_re[:, :, :t], pwt_im[:, :, :t])
    taps = (jnp.einsum('gcp,gpy->gcy', c_re.astype(F32), pb_re, precision=hp)
            - jnp.einsum('gcp,gpy->gcy', c_im.astype(F32), pb_im, precision=hp))
    y_idx = jnp.arange(t * cg)
    taps = taps + d.astype(F32).reshape(g, cg, 1) * (y_idx[None, :] == jnp.arange(cg)[:, None]).astype(F32)[None]
    place = ((y_idx[None, :, None] // cg == jnp.arange(t)[:, None, None] - y_idx[None, None, :] // cg)
             & (y_idx[None, :, None] % cg == y_idx[None, None, :] % cg)).astype(BF16)
    wt_g = jnp.einsum('gcy,tyz->gtcz', taps.astype(BF16), place,
                      preferred_element_type=F32).astype(BF16).reshape(g, t * cg, t * cg)
    n_re, n_im = pow_times_b(pwt_re[:, :, t - 1::-1], pwt_im[:, :, t - 1::-1])
    nt_g = jnp.concatenate([n_re, n_im], axis=1).astype(BF16)
    cp_re = c_re[:, None] * pw_re[:, 1:, None, :] - c_im[:, None] * pw_im[:, 1:, None, :]
    cp_im = c_re[:, None] * pw_im[:, 1:, None, :] + c_im[:, None] * pw_re[:, 1:, None, :]
    mt_g = jnp.concatenate([cp_re, -cp_im], axis=-1).reshape(g, t * cg, 2 * p).astype(BF16)
    ar, ai = power(t * jnp.concatenate([2 ** jnp.arange(n_levels), jnp.arange(1, S5_SCAN_BLK + 1)]))
    arow = jnp.concatenate([jnp.tile(ar, (1, 1, n_seq)), jnp.tile(ai, (1, 1, n_seq))], axis=1)
    return wt_g, nt_g, mt_g, arow


def _gelu_tanh(y):
    return 0.5 * y * (1.0 + jnp.tanh(math.sqrt(2.0 / math.pi) * (y + 0.044715 * (y * y * y))))


GLU_PARTS = 2


def _glu_body(y_ref, w_ref, b_ref, gate_ref, o_ref, z_ref, zb_ref, wt_ref):
    @pl.when((pl.program_id(0) == 0) & (pl.program_id(1) == 0))
    def _():
        _transpose_into(wt_ref, w_ref)

    tm = y_ref.shape[1]
    parts = [slice(c0, c0 + tm // GLU_PARTS) for c0 in range(0, tm, tm // GLU_PARTS)]
    for cols in parts:
        z = _gelu_tanh(y_ref[:, cols].astype(F32))
        z_ref[:, cols] = z
        zb_ref[:, cols] = z.astype(BF16)
    for cols in parts:
        for f0 in range(0, o_ref.shape[1], MM_SUB):
            rows = slice(f0, f0 + MM_SUB)
            lin = _dot(wt_ref[rows, :], zb_ref[:, cols]) + b_ref[rows, :]
            out = z_ref[rows, cols] * _sigmoid(lin) * _silu(gate_ref[rows, cols].astype(F32))
            o_ref[cols, rows] = out.T.astype(o_ref.dtype)


def glu_gate(y_t, w, b, proj_t, *, gate_row, tm=512):
    s, e, n = y_t.shape
    gb = gate_row // e
    return pl.pallas_call(
        _glu_body,
        grid=(s, n // tm),
        in_specs=[pl.BlockSpec((None, e, tm), lambda t, i: (t, 0, i)),
                  pl.BlockSpec((e, e), lambda t, i: (0, 0), pipeline_mode=pl.Buffered(1)),
                  pl.BlockSpec((e, 1), lambda t, i: (0, 0), pipeline_mode=pl.Buffered(1)),
                  pl.BlockSpec((None, e, tm), lambda t, i: (t, gb, i))],
        out_specs=pl.BlockSpec((tm, e), lambda t, i: (i, t)),
        out_shape=jax.ShapeDtypeStruct((n, s * e), BF16),
        scratch_shapes=[pltpu.VMEM((e, tm), F32), pltpu.VMEM((e, tm), BF16), pltpu.VMEM((e, e), BF16)],
        compiler_params=_cparams(("arbitrary", "arbitrary")),
        name="glu",
    )(y_t, w, b.reshape(e, 1), proj_t)


def kernel(x, ab_norm_g, ab_w_in, gla_alpha_up, gla_alpha_b, gla_head_g, fox_f_b, ab_w_out, c_norm_g, c_w_in, s5_lambda_re, s5_lambda_im, s5_log_step, s5_b_re, s5_b_im, s5_c_re, s5_c_im, s5_d, glu_w, glu_b, c_w_out, final_norm_g):
    batch, seq, d = x.shape
    t = batch * seq
    x2 = x.reshape(t, d)

    hk, hv, fw = GLA_HEADS * GLA_DK, GLA_HEADS * GLA_DV, FOX_HEADS * FOX_DH
    sizes = (hk, hk, hv, GLA_LOWRANK, hv, fw, fw, fw, FOX_HEADS, fw)
    offs = [0]
    for s in sizes:
        offs.append(offs[-1] + s)
    w_main, w_small = ab_weight_prep(jnp.swapaxes(ab_w_in, 1, 2), offs)
    c_gq, c_gk, c_gv, c_gg = 0, hk, 2 * hk, 2 * hk + hv
    c_fq = c_gg + hv
    c_fk, c_fv, c_fg = c_fq + fw, c_fq + 2 * fw, c_fq + 3 * fw

    proj, small = norm_proj(x2, ab_norm_g[0], w_main, w_small, tn=w_main.shape[0] // 2, name="ab_in")

    o_gla = gla(proj, small, gla_alpha_up[0], gla_alpha_b[0], gla_head_g[0],
                batch=batch, seq=seq, q_col=c_gq, k_col=c_gk, v_col=c_gv, gate_col=c_gg)

    fb_row = jnp.zeros((1, 128), F32).at[0, GLA_LOWRANK:GLA_LOWRANK + FOX_HEADS].set(fox_f_b[0])
    qa, ka = fox_gate(small, fb_row, batch=batch, seq=seq, col0=GLA_LOWRANK)
    o_fox = fox(proj, qa, ka, batch=batch, seq=seq, q_col=c_fq, k_col=c_fk, v_col=c_fv, gate_col=c_fg)

    x1, h1g = outproj_mid(o_gla, o_fox, ab_w_out[0], x2, c_norm_g[0], steps=S5_T, name="ab_out")

    e = d
    groups = e // S5_GROUP_CH
    n_chunks = t // S5_T
    cols_per_seq = seq // S5_T
    n_levels = S5_SCAN_BLK.bit_length() - 1
    proj_t = matmul_t_grouped(c_w_in[0], h1g, S5_T, BF16, name="c_in")
    ops = s5_operators(s5_lambda_re[0], s5_lambda_im[0], s5_log_step[0], s5_b_re[0], s5_b_im[0],
                       s5_c_re[0], s5_c_im[0], s5_d[0], n_levels=n_levels, n_seq=batch)
    y_t = s5_apply(proj_t, *ops, groups=groups, cols_per_seq=cols_per_seq, n_levels=n_levels)
    zz = glu_gate(y_t, glu_w[0], glu_b[0], proj_t, gate_row=e)
    out = outproj_final(zz, c_w_out[0], x1, final_norm_g, steps=S5_T, name="c_out")
    return out.reshape(batch, seq, d)
```

```python
import functools
import math

import jax
import jax.numpy as jnp
from jax import lax
from jax.experimental import pallas as pl
from jax.experimental.pallas import tpu as pltpu

EPS = 1e-6
F32 = jnp.float32
BF16 = jnp.bfloat16

GLA_HEADS = 4
GLA_DK = 128
GLA_DV = 256
GLA_LOWRANK = 16
GLA_TAU = 16.0
GLA_CHUNK = 64
GLA_SUB = 8
GLA_UNROLL = 4
FOX_HEADS = 8
FOX_DH = 128
FOX_AUG = 128
FOX_AUG_GROUP = 8
S5_GROUP_CH = 16
S5_STATE = 64
S5_T = 16

VMEM_LIMIT = 56 * 1024 * 1024

NT_DIMS = (((1,), (1,)), ((), ()))
TN_DIMS = (((0,), (0,)), ((), ()))
LOG2E = math.log2(math.e)


def _cparams(sem):
    return pltpu.CompilerParams(dimension_semantics=sem, vmem_limit_bytes=VMEM_LIMIT)


def _dot(a, b):
    return jnp.dot(a, b, preferred_element_type=F32)


def _dot_nt(a, b):
    return lax.dot_general(a, b, NT_DIMS, preferred_element_type=F32)


def _dot_tn(a, b):
    return lax.dot_general(a, b, TN_DIMS, preferred_element_type=F32)


def _log_sigmoid(z):
    return -(jnp.maximum(-z, 0.0) + jnp.log1p(jnp.exp(-jnp.abs(z))))


def _sigmoid(z):
    return 0.5 * jnp.tanh(0.5 * z) + 0.5


def _silu(z):
    return z * _sigmoid(z)


def _split3(x):
    hi = x.astype(BF16)
    r1 = x - hi.astype(F32)
    mid = r1.astype(BF16)
    lo = (r1 - mid.astype(F32)).astype(BF16)
    return hi, mid, lo


def _norm_proj_body(x_ref, g_ref, wt_ref, ws_ref, o_ref, s_ref):
    x = x_ref[...]
    ms = jnp.mean(x * x, axis=-1, keepdims=True)
    h = (x * lax.rsqrt(ms + EPS) * g_ref[...]).astype(BF16)
    for c0 in range(0, o_ref.shape[1], MM_SUB):
        o_ref[:, c0:c0 + MM_SUB] = _dot_nt(h, wt_ref[c0:c0 + MM_SUB, :]).astype(o_ref.dtype)
    s_ref[...] = _dot_nt(h, ws_ref[...])


def norm_proj(x, g, wt, wt_small, tm=512, tn=512, name="norm_proj"):
    m, d = x.shape
    n = wt.shape[0]
    ns = wt_small.shape[0]
    proj, small = pl.pallas_call(
        _norm_proj_body,
        grid=(n // tn, m // tm),
        in_specs=[pl.BlockSpec((tm, d), lambda j, i: (i, 0)),
                  pl.BlockSpec((1, d), lambda j, i: (0, 0)),
                  pl.BlockSpec((tn, d), lambda j, i: (j, 0)),
                  pl.BlockSpec((ns, d), lambda j, i: (0, 0))],
        out_specs=[pl.BlockSpec((tm, tn), lambda j, i: (i, j)),
                   pl.BlockSpec((None, tm, ns), lambda j, i: (j, i, 0))],
        out_shape=[jax.ShapeDtypeStruct((m, n), BF16), jax.ShapeDtypeStruct((n // tn, m, ns), F32)],
        compiler_params=_cparams(("arbitrary", "arbitrary")),
        name=name,
    )(x, g.reshape(1, d), wt, wt_small)
    return proj, small[0]


def _ab_wprep_body(w_ref, o_ref, s_ref, *, offs):
    w = w_ref[0]
    seg = lambda n: w[offs[n]:offs[n + 1], :]
    o_ref[...] = jnp.concatenate(
        [w[:offs[3], :], seg(4), seg(5) * (FOX_DH ** -0.5 * LOG2E), w[offs[6]:offs[8], :], seg(9)],
        axis=0).astype(o_ref.dtype)
    pad = s_ref.shape[0] - (offs[4] - offs[3]) - (offs[9] - offs[8])
    s_ref[...] = jnp.concatenate([seg(3), seg(8), jnp.zeros((pad, w.shape[1]), F32)], axis=0).astype(s_ref.dtype)


def ab_weight_prep(w_t3, offs, tk=256):
    _, n_in, d = w_t3.shape
    n_main = n_in - (offs[4] - offs[3]) - (offs[9] - offs[8])
    return pl.pallas_call(
        functools.partial(_ab_wprep_body, offs=tuple(offs)),
        grid=(d // tk,),
        in_specs=[pl.BlockSpec((1, n_in, tk), lambda i: (0, 0, i))],
        out_specs=[pl.BlockSpec((n_main, tk), lambda i: (0, i)), pl.BlockSpec((128, tk), lambda i: (0, i))],
        out_shape=[jax.ShapeDtypeStruct((n_main, d), BF16), jax.ShapeDtypeStruct((128, d), BF16)],
        compiler_params=_cparams(("parallel",)),
        name="ab_wprep",
    )(w_t3)


MM_SUB = 512


def _transpose_into(wt_ref, w_ref):
    for c0 in range(0, w_ref.shape[1], MM_SUB):
        wt_ref[c0:c0 + MM_SUB, :] = w_ref[:, c0:c0 + MM_SUB].astype(wt_ref.dtype).T


def _mm_t_body(w_ref, a_ref, o_ref, wt_ref):
    @pl.when((pl.program_id(1) == 0) & (pl.program_id(2) == 0))
    def _():
        _transpose_into(wt_ref, w_ref)

    a = a_ref[...]
    for r0 in range(0, o_ref.shape[0], MM_SUB):
        o_ref[r0:r0 + MM_SUB, :] = _dot_nt(wt_ref[r0:r0 + MM_SUB, :], a).astype(o_ref.dtype)


def matmul_t_grouped(w, a2, steps, out_dtype, tm=512, tn=2048, name="matmul_tg"):
    k, nf = w.shape
    n = a2.shape[0]
    return pl.pallas_call(
        _mm_t_body,
        grid=(nf // tn, steps, n // tm),
        in_specs=[pl.BlockSpec((k, tn), lambda j, t, i: (0, j), pipeline_mode=pl.Buffered(1)),
                  pl.BlockSpec((tm, k), lambda j, t, i: (i, t))],
        out_specs=pl.BlockSpec((None, tn, tm), lambda j, t, i: (t, j, i)),
        out_shape=jax.ShapeDtypeStruct((steps, nf, n), out_dtype),
        scratch_shapes=[pltpu.VMEM((tn, k), BF16)],
        compiler_params=_cparams(("arbitrary", "arbitrary", "arbitrary")),
        name=name,
    )(w, a2)


def _gla_body(q_ref, k_ref, v_ref, glr_ref, aup_ref, ab_ref, gate_ref, hg_ref, o_ref,
              s_ref, b_ref, kf_ref, kts_ref, dec_ref, *, tb, hpb):
    ib = pl.program_id(2)
    C, SB, dk, dv = GLA_CHUNK, GLA_SUB, GLA_DK, GLA_DV

    @pl.when(ib == 0)
    def _():
        s_ref[...] = jnp.zeros_like(s_ref)

    glr = glr_ref[:, :GLA_LOWRANK].astype(BF16)
    tw = 2 * C
    r = lax.broadcasted_iota(jnp.int32, (tw, tw), 0)
    c = lax.broadcasted_iota(jnp.int32, (tw, tw), 1)
    tri = jnp.where((r // C == c // C) & (c <= r), 1.0, 0.0).astype(BF16)
    nc = tb // C
    heads = range(hpb)
    kcs = [slice(hh * dk, (hh + 1) * dk) for hh in heads]
    zs = [_dot(glr, aup_ref[:, kcs[hh]].astype(BF16)) + ab_ref[:, kcs[hh]] for hh in heads]
    b_blks = []
    for hh in heads:
        la = _log_sigmoid(zs[hh]) * (1.0 / GLA_TAU)
        hi = la.astype(BF16)
        lo = (la - hi.astype(F32)).astype(BF16)
        b_blks.append(jnp.concatenate(
            [_dot(tri, hi[r0:r0 + tw]) + _dot(tri, lo[r0:r0 + tw]) for r0 in range(0, tb, tw)], axis=0))
    for hh in heads:
        b_blk = b_blks[hh]
        k_blk = k_ref[:, kcs[hh]].astype(F32)
        b_ref[hh] = b_blk
        kf_ref[hh] = k_blk
        b3 = b_blk.reshape(nc, C, dk)
        b_end = b3[:, C - 1:C, :]
        k_end = (k_blk.reshape(nc, C, dk) * jnp.exp(b_end - b3)).reshape(tb, dk)
        kt_end = k_end.T.astype(BF16)
        dec_t = jnp.exp(b_end.reshape(nc, dk)).T
        for ci in range(nc):
            kts_ref[hh, ci] = kt_end[:, ci * C:(ci + 1) * C]
            dec_ref[hh, ci] = jnp.broadcast_to(dec_t[:, ci:ci + 1], (dk, 128))

    row = lax.broadcasted_iota(jnp.int32, (C, 1), 0)
    rowi = lax.broadcasted_iota(jnp.int32, (C, C), 0)
    coli = lax.broadcasted_iota(jnp.int32, (C, C), 1)
    lane_c = lax.broadcasted_iota(jnp.int32, (SB, C), 1)
    sub_r = lax.broadcasted_iota(jnp.int32, (SB, 1), 0)
    ones = jnp.ones((GLA_DK, C), BF16)
    neg = -jnp.inf
    scale = GLA_DK ** -0.5

    def stage_state(ci, r0, hh):
        kc = slice(hh * dk, (hh + 1) * dk)
        vc = slice(hh * dv, (hh + 1) * dv)
        q = q_ref[pl.ds(r0, C), kc].astype(F32) * scale
        k = kf_ref[hh, pl.ds(r0, C), :]
        b = b_ref[hh, pl.ds(r0, C), :]
        v = v_ref[pl.ds(r0, C), vc]

        s = s_ref[hh]
        o = _dot((q * jnp.exp(b)).astype(BF16), s.astype(BF16))
        dec = dec_ref[hh, ci]
        s_ref[hh] = s * jnp.concatenate([dec] * (dv // 128), axis=1) + _dot(kts_ref[hh, ci], v)

        attn = jnp.zeros((C, C), F32)
        h = C // 2
        while h >= SB:
            ref = jnp.broadcast_to(b.reshape(C // (2 * h), 2 * h, GLA_DK)[:, h - 1:h, :],
                                   (C // (2 * h), 2 * h, GLA_DK)).reshape(C, GLA_DK)
            upper = (row // h) % 2 == 1
            q_h = q * jnp.exp(jnp.where(upper, b - ref, neg))
            k_h = k * jnp.exp(jnp.where(upper, neg, ref - b))
            a_h = _dot_nt(q_h.astype(BF16), k_h.astype(BF16))
            attn = attn + (a_h if 2 * h == C else jnp.where(rowi // (2 * h) == coli // (2 * h), a_h, 0.0))
            h //= 2
        return q, b, v, o, attn

    def stage_diag(r0, hh, q, b):
        zs = []
        for bi in range(C // SB):
            s0 = bi * SB
            q_i = q[s0:s0 + SB, :]
            b_i = b[s0:s0 + SB, :]
            for j in range(SB):
                k_j = kf_ref[hh, pl.ds(r0 + (s0 + j), 1), :]
                b_j = b_ref[hh, pl.ds(r0 + (s0 + j), 1), :]
                zs.append(q_i * k_j * jnp.exp(jnp.where(sub_r >= j, b_i - b_j, neg)))
        return _dot(jnp.concatenate(zs, axis=0).astype(BF16), ones)

    def stage_out(r0, hh, v, o, attn, zsum):
        vc = slice(hh * dv, (hh + 1) * dv)
        diag = []
        for bi in range(C // SB):
            acc = jnp.zeros((SB, C), F32)
            for j in range(SB):
                n0 = (bi * SB + j) * SB
                acc = acc + jnp.where(lane_c == bi * SB + j, zsum[n0:n0 + SB, :], 0.0)
            diag.append(acc)
        attn = attn + jnp.concatenate(diag, axis=0)
        o = o + _dot(attn.astype(BF16), v)

        o = o * lax.rsqrt(jnp.mean(o * o, axis=-1, keepdims=True) + EPS) * hg_ref[:, vc]
        g = gate_ref[pl.ds(r0, C), vc].astype(F32)
        o_ref[pl.ds(r0, C), vc] = (o * _silu(g)).astype(o_ref.dtype)

    def chunk(ci, carry):
        r0 = pl.multiple_of(ci * C, C)
        heads = range(hpb)
        st = [stage_state(ci, r0, hh) for hh in heads]
        zsums = [stage_diag(r0, hh, st[hh][0], st[hh][1]) for hh in heads]
        for hh in heads:
            _, _, v, o, attn = st[hh]
            stage_out(r0, hh, v, o, attn, zsums[hh])
        return carry

    lax.fori_loop(0, tb // C, chunk, 0, unroll=GLA_UNROLL)


def gla(proj, small, alpha_up, alpha_b, head_g, *, batch, seq, q_col, k_col, v_col, gate_col, tb=512, hpb=4):
    t = batch * seq
    nb = seq // tb
    dk, dv, h = GLA_DK * hpb, GLA_DV * hpb, GLA_HEADS // hpb
    qb, kb, vb, gb = q_col // dk, k_col // dk, v_col // dv, gate_col // dv
    tok = lambda b, hh, i: b * nb + i
    nc = tb // GLA_CHUNK
    return pl.pallas_call(
        functools.partial(_gla_body, tb=tb, hpb=hpb),
        grid=(batch, h, nb),
        in_specs=[
            pl.BlockSpec((tb, dk), lambda b, hh, i: (tok(b, hh, i), qb + hh)),
            pl.BlockSpec((tb, dk), lambda b, hh, i: (tok(b, hh, i), kb + hh)),
            pl.BlockSpec((tb, dv), lambda b, hh, i: (tok(b, hh, i), vb + hh)),
            pl.BlockSpec((tb, 128), lambda b, hh, i: (tok(b, hh, i), 0)),
            pl.BlockSpec((GLA_LOWRANK, dk), lambda b, hh, i: (0, hh)),
            pl.BlockSpec((1, dk), lambda b, hh, i: (0, hh)),
            pl.BlockSpec((tb, dv), lambda b, hh, i: (tok(b, hh, i), gb + hh)),
            pl.BlockSpec((1, dv), lambda b, hh, i: (0, hh)),
        ],
        out_specs=pl.BlockSpec((tb, dv), lambda b, hh, i: (tok(b, hh, i), hh)),
        out_shape=jax.ShapeDtypeStruct((t, GLA_HEADS * GLA_DV), BF16),
        scratch_shapes=[pltpu.VMEM((hpb, GLA_DK, GLA_DV), F32), pltpu.VMEM((hpb, tb, GLA_DK), F32),
                        pltpu.VMEM((hpb, tb, GLA_DK), F32),
                        pltpu.VMEM((hpb, nc, GLA_DK, GLA_CHUNK), BF16),
                        pltpu.VMEM((hpb, nc, GLA_DK, 128), F32)],
        compiler_params=_cparams(("parallel", "parallel", "arbitrary")),
        name="gla",
    )(proj, proj, proj, small, alpha_up, alpha_b.reshape(1, -1), proj, head_g.reshape(1, -1))


FOX_GATE_UNROLL = 4


def _fox_gate_body(s_ref, fb_ref, qa_ref, ka_ref, cs_ref, off_ref, *, blk, col0):
    n = s_ref.shape[0] // blk
    r = lax.broadcasted_iota(jnp.int32, (blk, blk), 0)
    c = lax.broadcasted_iota(jnp.int32, (blk, blk), 1)
    tri = jnp.where(c <= r, 1.0, 0.0).astype(BF16)
    lanes = s_ref.shape[1]
    pr = lax.broadcasted_iota(jnp.int32, (3 * lanes, FOX_AUG), 0)
    pc = lax.broadcasted_iota(jnp.int32, (3 * lanes, FOX_AUG), 1)
    head, piece = pr % lanes - col0, pr // lanes
    is_head = (head >= 0) & (head < FOX_HEADS)
    sel_q = jnp.where(is_head & (pc == head * FOX_AUG_GROUP + piece), 1.0, 0.0).astype(BF16)
    sel_k = jnp.where(is_head & (pc == head * FOX_AUG_GROUP + 3 + piece), -1.0, 0.0).astype(BF16)
    lane = lax.broadcasted_iota(jnp.int32, (1, FOX_AUG), 1)
    used = lane < FOX_HEADS * FOX_AUG_GROUP
    one_q = jnp.where(used & (lane % FOX_AUG_GROUP >= 3) & (lane % FOX_AUG_GROUP < 6), 1.0, 0.0)
    one_k = jnp.where(used & (lane % FOX_AUG_GROUP < 3), 1.0, 0.0)

    def local(i, carry):
        r0 = pl.multiple_of(i * blk, blk)
        lf = _log_sigmoid(s_ref[pl.ds(r0, blk), :] + fb_ref[...])
        hi, mid, lo = _split3(lf)
        cs_ref[pl.ds(r0, blk), :] = _dot(tri, hi) + _dot(tri, mid) + _dot(tri, lo)
        return carry

    lax.fori_loop(0, n, local, 0, unroll=FOX_GATE_UNROLL)

    off = jnp.zeros((1, lanes), F32)
    for i in range(n):
        off_ref[i:i + 1, :] = off
        off = off + cs_ref[(i + 1) * blk - 1:(i + 1) * blk, :]

    def place(i, carry):
        r0 = pl.multiple_of(i * blk, blk)
        cs = cs_ref[pl.ds(r0, blk), :] + off_ref[pl.ds(i, 1), :]
        pieces = jnp.concatenate(_split3(cs * LOG2E), axis=1)
        qa_ref[pl.ds(r0, blk), :] = (_dot(pieces, sel_q) + one_q).astype(BF16)
        ka_ref[pl.ds(r0, blk), :] = (_dot(pieces, sel_k) + one_k).astype(BF16)
        return carry

    lax.fori_loop(0, n, place, 0, unroll=FOX_GATE_UNROLL)


def fox_gate(small, fb_row, *, batch, seq, col0, blk=256):
    shp = jax.ShapeDtypeStruct((batch * seq, FOX_AUG), BF16)
    spec = pl.BlockSpec((seq, FOX_AUG), lambda b: (b, 0))
    return pl.pallas_call(
        functools.partial(_fox_gate_body, blk=blk, col0=col0),
        grid=(batch,),
        in_specs=[pl.BlockSpec((seq, 128), lambda b: (b, 0)),
                  pl.BlockSpec((1, 128), lambda b: (0, 0))],
        out_specs=[spec, spec],
        out_shape=[shp, shp],
        scratch_shapes=[pltpu.VMEM((seq, 128), F32), pltpu.VMEM((seq // blk, 128), F32)],
        compiler_params=_cparams(("parallel",)),
        name="fox_gate",
    )(small, fb_row)


def _fox_body(it_ref, jt_ref, q_ref, qa_ref, k_ref, ka_ref, v_ref, gate_ref, o_ref,
              m_ref, l_ref, acc_ref, *, tq, tk, hpb):
    p = pl.program_id(2)
    i = it_ref[p]
    j = jt_ref[p]
    d = FOX_DH

    @pl.when(j == 0)
    def _():
        m_ref[...] = jnp.full_like(m_ref, -jnp.inf)
        l_ref[...] = jnp.zeros_like(l_ref)
        acc_ref[...] = jnp.zeros_like(acc_ref)

    aug_lane = lax.broadcasted_iota(jnp.int32, (1, FOX_AUG), 1)

    def scores(item, masked):
        hh, ks, qs = item
        mine = aug_lane // FOX_AUG_GROUP == pl.program_id(1) * hpb + hh
        zero = jnp.zeros((), BF16)
        q_aug = jnp.concatenate([q_ref[qs, hh * d:(hh + 1) * d], jnp.where(mine, qa_ref[qs, :], zero)], axis=1)
        k_aug = jnp.concatenate([k_ref[ks, hh * d:(hh + 1) * d], jnp.where(mine, ka_ref[ks, :], zero)], axis=1)
        st = _dot_nt(k_aug, q_aug)
        if masked:
            kr = lax.broadcasted_iota(jnp.int32, st.shape, 0) + ks.start
            qc = lax.broadcasted_iota(jnp.int32, st.shape, 1) + qs.start
            st = jnp.where(qc >= kr, st, -jnp.inf)
        return st

    def absorb(item, st):
        hh, ks, qs = item
        m_old = m_ref[hh, :, qs]
        m_new = jnp.maximum(m_old, jnp.max(st, axis=0, keepdims=True))
        alpha = jnp.exp2(m_old - m_new)
        pt = jnp.exp2(st - m_new)
        l_ref[hh, :, qs] = alpha * l_ref[hh, :, qs] + jnp.sum(pt, axis=0, keepdims=True)
        acc_ref[hh, :, qs] = (alpha * acc_ref[hh, :, qs]
                              + _dot_tn(v_ref[ks, hh * d:(hh + 1) * d], pt.astype(BF16)))
        m_ref[hh, :, qs] = m_new

    def block(masked, finish):
        full = slice(0, tq)
        halves = [(slice(0, tk // 2), full), (slice(tk // 2, tk), slice(tq // 2, tq))] if masked else [(slice(0, tk), full)]
        items = [(hh, ks, qs) for hh in range(hpb) for ks, qs in halves]
        st = scores(items[0], masked)
        for n, item in enumerate(items):
            st_next = scores(items[n + 1], masked) if n + 1 < len(items) else None
            absorb(item, st)
            hh = item[0]
            if finish and (n + 1 == len(items) or items[n + 1][0] != hh):
                o = (acc_ref[hh] / l_ref[hh]).T
                g = gate_ref[:, hh * d:(hh + 1) * d].astype(F32)
                o_ref[:, hh * d:(hh + 1) * d] = (o * _silu(g)).astype(o_ref.dtype)
            st = st_next

    @pl.when(j < i)
    def _():
        block(False, False)

    @pl.when(j == i)
    def _():
        block(True, True)


def fox(proj, qa, ka, *, batch, seq, q_col, k_col, v_col, gate_col, t=1024, hpb=8):
    tt = batch * seq
    nb = seq // t
    d, h = FOX_DH, FOX_HEADS
    w = hpb * d
    qb, kb, gb, vb = q_col // w, k_col // w, gate_col // w, v_col // w
    pairs = [(i, j) for i in range(nb) for j in range(i + 1)]
    it = jnp.array([p[0] for p in pairs], jnp.int32)
    jt = jnp.array([p[1] for p in pairs], jnp.int32)
    grid_spec = pltpu.PrefetchScalarGridSpec(
        num_scalar_prefetch=2,
        grid=(batch, h // hpb, len(pairs)),
        in_specs=[
            pl.BlockSpec((t, w), lambda b, hh, p, it, jt: (b * nb + it[p], qb + hh)),
            pl.BlockSpec((t, FOX_AUG), lambda b, hh, p, it, jt: (b * nb + it[p], 0)),
            pl.BlockSpec((t, w), lambda b, hh, p, it, jt: (b * nb + jt[p], kb + hh)),
            pl.BlockSpec((t, FOX_AUG), lambda b, hh, p, it, jt: (b * nb + jt[p], 0)),
            pl.BlockSpec((t, w), lambda b, hh, p, it, jt: (b * nb + jt[p], vb + hh)),
            pl.BlockSpec((t, w), lambda b, hh, p, it, jt: (b * nb + it[p], gb + hh)),
        ],
        out_specs=pl.BlockSpec((t, w), lambda b, hh, p, it, jt: (b * nb + it[p], hh)),
        scratch_shapes=[pltpu.VMEM((hpb, 1, t), F32), pltpu.VMEM((hpb, 1, t), F32), pltpu.VMEM((hpb, d, t), F32)],
    )
    return pl.pallas_call(
        functools.partial(_fox_body, tq=t, tk=t, hpb=hpb),
        grid_spec=grid_spec,
        out_shape=jax.ShapeDtypeStruct((tt, h * d), BF16),
        compiler_params=_cparams(("parallel", "parallel", "arbitrary")),
        name="fox",
    )(it, jt, proj, qa, proj, ka, proj, proj)


def _regroup_perm(tm, steps, to_grouped):
    r = lax.broadcasted_iota(jnp.int32, (tm, tm), 0)
    c = lax.broadcasted_iota(jnp.int32, (tm, tm), 1)
    nc = tm // steps
    src = (r % nc) * steps + r // nc if to_grouped else (r % steps) * nc + r // steps
    return jnp.where(c == src, 1.0, 0.0).astype(BF16)


def _cast_weight_once(wb_ref, w_ref):
    @pl.when(pl.program_id(0) == 0)
    def _():
        for r0 in range(0, w_ref.shape[0], MM_SUB):
            wb_ref[r0:r0 + MM_SUB, :] = w_ref[r0:r0 + MM_SUB, :].astype(wb_ref.dtype)


def _outproj_mid_body(a1_ref, a2_ref, wf_ref, x_ref, g_ref, r_ref, h_ref, w_ref, *, steps):
    k1 = a1_ref.shape[1]
    tm, d = x_ref.shape
    nc = tm // steps
    _cast_weight_once(w_ref, wf_ref)
    acc = x_ref[...] + _dot(a1_ref[...], w_ref[:k1, :]) + _dot(a2_ref[...], w_ref[k1:, :])
    r_ref[...] = acc
    normed = (acc * lax.rsqrt(jnp.mean(acc * acc, axis=-1, keepdims=True) + EPS) * g_ref[...]).astype(BF16)
    by_step = _dot(_regroup_perm(tm, steps, True), normed).astype(h_ref.dtype)
    for s in range(steps):
        h_ref[:, s * d:(s + 1) * d] = by_step[s * nc:(s + 1) * nc, :]


def outproj_mid(a1, a2, w, x, g, *, steps, tm=256, name="outproj_mid"):
    m, d = x.shape
    row = pl.BlockSpec((tm, d), lambda i: (i, 0))
    return pl.pallas_call(
        functools.partial(_outproj_mid_body, steps=steps),
        grid=(m // tm,),
        in_specs=[pl.BlockSpec((tm, a1.shape[1]), lambda i: (i, 0)),
                  pl.BlockSpec((tm, a2.shape[1]), lambda i: (i, 0)),
                  pl.BlockSpec(w.shape, lambda i: (0, 0), pipeline_mode=pl.Buffered(1)),
                  row,
                  pl.BlockSpec((1, d), lambda i: (0, 0))],
        out_specs=[row, pl.BlockSpec((tm // steps, steps * d), lambda i: (i, 0))],
        out_shape=[jax.ShapeDtypeStruct((m, d), F32), jax.ShapeDtypeStruct((m // steps, steps * d), BF16)],
        scratch_shapes=[pltpu.VMEM(w.shape, BF16)],
        compiler_params=_cparams(("arbitrary",)),
        name=name,
    )(a1, a2, w, x, g.reshape(1, d))


def _outproj_final_body(a_ref, wf_ref, x_ref, g_ref, o_ref, w_ref, *, steps):
    tm = x_ref.shape[0]
    e = w_ref.shape[0]
    by_step = jnp.concatenate([a_ref[:, s * e:(s + 1) * e] for s in range(steps)], axis=0)
    a_tok = _dot(_regroup_perm(tm, steps, False), by_step).astype(BF16)
    _cast_weight_once(w_ref, wf_ref)
    acc = x_ref[...] + _dot(a_tok, w_ref[...])
    o_ref[...] = acc * lax.rsqrt(jnp.mean(acc * acc, axis=-1, keepdims=True) + EPS) * g_ref[...]


def outproj_final(a_grouped, w, x, g, *, steps, tm=256, name="outproj_final"):
    m, d = x.shape
    e = w.shape[0]
    row = pl.BlockSpec((tm, d), lambda i: (i, 0))
    return pl.pallas_call(
        functools.partial(_outproj_final_body, steps=steps),
        grid=(m // tm,),
        in_specs=[pl.BlockSpec((tm // steps, steps * e), lambda i: (i, 0)),
                  pl.BlockSpec(w.shape, lambda i: (0, 0), pipeline_mode=pl.Buffered(1)),
                  row,
                  pl.BlockSpec((1, d), lambda i: (0, 0))],
        out_specs=row,
        out_shape=jax.ShapeDtypeStruct((m, d), F32),
        scratch_shapes=[pltpu.VMEM(w.shape, BF16)],
        compiler_params=_cparams(("arbitrary",)),
        name=name,
    )(a_grouped, w, x, g.reshape(1, d))


S5_SCAN_BLK = 8
S5_GPB = 4


def _s5_entering_state(x, arow, *, cols_per_seq, n_levels):
    n = x.shape[1]
    p = S5_STATE
    nseq = n // cols_per_seq

    def to_rows(part):
        return jnp.concatenate([part[:, s * cols_per_seq:(s + 1) * cols_per_seq] for s in range(nseq)], axis=0).T

    def to_cols(rows):
        rt = rows.T
        return jnp.concatenate([rt[s * p:(s + 1) * p, :] for s in range(nseq)], axis=1)

    xr, xi = to_rows(x[:p, :]), to_rows(x[p:, :])
    pos = lax.broadcasted_iota(jnp.int32, (cols_per_seq, 1), 0)
    blk = S5_SCAN_BLK
    half = arow.shape[0] // 2
    for lvl in range(n_levels):
        sh = 1 << lvl
        keep = pos % blk >= sh
        sr = jnp.where(keep, pltpu.roll(xr, sh, axis=0), 0.0)
        si = jnp.where(keep, pltpu.roll(xi, sh, axis=0), 0.0)
        ar, ai = arow[lvl:lvl + 1, :], arow[half + lvl:half + lvl + 1, :]
        xr, xi = xr + ar * sr - ai * si, xi + ar * si + ai * sr
    pw_r, pw_i = arow[n_levels:n_levels + blk, :], arow[half + n_levels:half + n_levels + blk, :]
    out_r, out_i = [xr[:blk, :]], [xi[:blk, :]]
    for r0 in range(blk, cols_per_seq, blk):
        cr, ci = out_r[-1][blk - 1:blk, :], out_i[-1][blk - 1:blk, :]
        out_r.append(xr[r0:r0 + blk, :] + pw_r * cr - pw_i * ci)
        out_i.append(xi[r0:r0 + blk, :] + pw_r * ci + pw_i * cr)
    xr, xi = jnp.concatenate(out_r, axis=0), jnp.concatenate(out_i, axis=0)
    keep = pos >= 1
    pr = jnp.where(keep, pltpu.roll(xr, 1, axis=0), 0.0)
    pi = jnp.where(keep, pltpu.roll(xi, 1, axis=0), 0.0)
    return jnp.concatenate([to_cols(pr), to_cols(pi)], axis=0).astype(BF16)


def _s5_body(u_ref, wt_ref, nt_ref, mt_ref, arow_ref, o_ref, *, cols_per_seq, n_levels):
    t, _, n = u_ref.shape
    cg = S5_GROUP_CH
    groups = range(S5_GPB)
    ch = [slice(gi * cg, (gi + 1) * cg) for gi in groups]
    us = [u_ref[:, ch[gi], :].reshape(t * cg, n) for gi in groups]
    xs = [_dot(nt_ref[gi], us[gi]) for gi in groups]
    ys = [_dot(wt_ref[gi], us[gi]) for gi in groups]
    prevs = [_s5_entering_state(xs[gi], arow_ref[gi], cols_per_seq=cols_per_seq, n_levels=n_levels)
             for gi in groups]
    for gi in groups:
        y = ys[gi] + _dot(mt_ref[gi], prevs[gi])
        o_ref[:, ch[gi], :] = y.astype(o_ref.dtype).reshape(t, cg, n)


def s5_apply(proj_t, wt_g, nt_g, mt_g, arow, *, groups, cols_per_seq, n_levels):
    t, _, n = proj_t.shape
    cg = S5_GROUP_CH * S5_GPB
    spec3 = lambda shp: pl.BlockSpec((S5_GPB,) + shp, lambda i: (i, 0, 0))
    return pl.pallas_call(
        functools.partial(_s5_body, cols_per_seq=cols_per_seq, n_levels=n_levels),
        grid=(groups // S5_GPB,),
        in_specs=[pl.BlockSpec((t, cg, n), lambda i: (0, i, 0)),
                  spec3(wt_g.shape[1:]), spec3(nt_g.shape[1:]), spec3(mt_g.shape[1:]), spec3(arow.shape[1:])],
        out_specs=pl.BlockSpec((t, cg, n), lambda i: (0, i, 0)),
        out_shape=jax.ShapeDtypeStruct((t, groups * S5_GROUP_CH, n), BF16),
        compiler_params=_cparams(("parallel",)),
        name="s5",
    )(proj_t, wt_g, nt_g, mt_g, arow)


def s5_operators(lam_re, lam_im, log_step, b_re, b_im, c_re, c_im, d, *, n_levels, n_seq):
    hp = lax.Precision.HIGHEST
    g, p = lam_re.shape
    cg, t = S5_GROUP_CH, S5_T
    lr = jnp.minimum(lam_re.astype(F32), -1e-4)
    li = lam_im.astype(F32)
    step = jnp.exp(log_step.astype(F32))[:, None]
    mag = jnp.exp(lr * step)
    lb_re, lb_im = mag * jnp.cos(li * step), mag * jnp.sin(li * step)
    den = lr * lr + li * li
    nr, ni = lb_re - 1.0, lb_im
    coef_re = (nr * lr + ni * li) / den
    coef_im = (ni * lr - nr * li) / den
    bb_re = coef_re[..., None] * b_re - coef_im[..., None] * b_im
    bb_im = coef_re[..., None] * b_im + coef_im[..., None] * b_re

    def power(m):
        mm = m.astype(F32)[None, :, None]
        mg = jnp.exp(lr[:, None, :] * step[:, None, :] * mm)
        ang = li[:, None, :] * step[:, None, :] * mm
        return mg * jnp.cos(ang), mg * jnp.sin(ang)

    pw_re, pw_im = power(jnp.arange(t + 1))
    pwt_re, pwt_im = pw_re.transpose(0, 2, 1), pw_im.transpose(0, 2, 1)

    def pow_times_b(pr, pi):
        re = pr[:, :, :, None] * bb_re[:, :, None, :] - pi[:, :, :, None] * bb_im[:, :, None, :]
        im = pr[:, :, :, None] * bb_im[:, :, None, :] + pi[:, :, :, None] * bb_re[:, :, None, :]
        return re.reshape(g, p, t * cg), im.reshape(g, p, t * cg)

    pb_re, pb_im = pow_times_b(pwt_re[:, :, :t], pwt_im[:, :, :t])
    taps = (jnp.einsum('gcp,gpy->gcy', c_re.astype(F32), pb_re, precision=hp)
            - jnp.einsum('gcp,gpy->gcy', c_im.astype(F32), pb_im, precision=hp))
    y_idx = jnp.arange(t * cg)
    taps = taps + d.astype(F32).reshape(g, cg, 1) * (y_idx[None, :] == jnp.arange(cg)[:, None]).astype(F32)[None]
    place = ((y_idx[None, :, None] // cg == jnp.arange(t)[:, None, None] - y_idx[None, None, :] // cg)
             & (y_idx[None, :, None] % cg == y_idx[None, None, :] % cg)).astype(BF16)
    wt_g = jnp.einsum('gcy,tyz->gtcz', taps.astype(BF16), place,
                      preferred_element_type=F32).astype(BF16).reshape(g, t * cg, t * cg)
    n_re, n_im = pow_times_b(pwt_re[:, :, t - 1::-1], pwt_im[:, :, t - 1::-1])
    nt_g = jnp.concatenate([n_re, n_im], axis=1).astype(BF16)
    cp_re = c_re[:, None] * pw_re[:, 1:, None, :] - c_im[:, None] * pw_im[:, 1:, None, :]
    cp_im = c_re[:, None] * pw_im[:, 1:, None, :] + c_im[:, None] * pw_re[:, 1:, None, :]
    mt_g = jnp.concatenate([cp_re, -cp_im], axis=-1).reshape(g, t * cg, 2 * p).astype(BF16)
    ar, ai = power(t * jnp.concatenate([2 ** jnp.arange(n_levels), jnp.arange(1, S5_SCAN_BLK + 1)]))
    arow = jnp.concatenate([jnp.tile(ar, (1, 1, n_seq)), jnp.tile(ai, (1, 1, n_seq))], axis=1)
    return wt_g, nt_g, mt_g, arow


def _gelu_tanh(y):
    return 0.5 * y * (1.0 + jnp.tanh(math.sqrt(2.0 / math.pi) * (y + 0.044715 * (y * y * y))))


GLU_PARTS = 2


def _glu_body(y_ref, w_ref, b_ref, gate_ref, o_ref, z_ref, zb_ref, wt_ref):
    @pl.when((pl.program_id(0) == 0) & (pl.program_id(1) == 0))
    def _():
        _transpose_into(wt_ref, w_ref)

    tm = y_ref.shape[1]
    parts = [slice(c0, c0 + tm // GLU_PARTS) for c0 in range(0, tm, tm // GLU_PARTS)]
    for cols in parts:
        z = _gelu_tanh(y_ref[:, cols].astype(F32))
        z_ref[:, cols] = z
        zb_ref[:, cols] = z.astype(BF16)
    for cols in parts:
        for f0 in range(0, o_ref.shape[1], MM_SUB):
            rows = slice(f0, f0 + MM_SUB)
            lin = _dot(wt_ref[rows, :], zb_ref[:, cols]) + b_ref[rows, :]
            out = z_ref[rows, cols] * _sigmoid(lin) * _silu(gate_ref[rows, cols].astype(F32))
            o_ref[cols, rows] = out.T.astype(o_ref.dtype)


def glu_gate(y_t, w, b, proj_t, *, gate_row, tm=512):
    s, e, n = y_t.shape
    gb = gate_row // e
    return pl.pallas_call(
        _glu_body,
        grid=(s, n // tm),
        in_specs=[pl.BlockSpec((None, e, tm), lambda t, i: (t, 0, i)),
                  pl.BlockSpec((e, e), lambda t, i: (0, 0), pipeline_mode=pl.Buffered(1)),
                  pl.BlockSpec((e, 1), lambda t, i: (0, 0), pipeline_mode=pl.Buffered(1)),
                  pl.BlockSpec((None, e, tm), lambda t, i: (t, gb, i))],
        out_specs=pl.BlockSpec((tm, e), lambda t, i: (i, t)),
        out_shape=jax.ShapeDtypeStruct((n, s * e), BF16),
        scratch_shapes=[pltpu.VMEM((e, tm), F32), pltpu.VMEM((e, tm), BF16), pltpu.VMEM((e, e), BF16)],
        compiler_params=_cparams(("arbitrary", "arbitrary")),
        name="glu",
    )(y_t, w, b.reshape(e, 1), proj_t)


def kernel(x, ab_norm_g, ab_w_in, gla_alpha_up, gla_alpha_b, gla_head_g, fox_f_b, ab_w_out, c_norm_g, c_w_in, s5_lambda_re, s5_lambda_im, s5_log_step, s5_b_re, s5_b_im, s5_c_re, s5_c_im, s5_d, glu_w, glu_b, c_w_out, final_norm_g):
    batch, seq, d = x.shape
    t = batch * seq
    x2 = x.reshape(t, d)

    hk, hv, fw = GLA_HEADS * GLA_DK, GLA_HEADS * GLA_DV, FOX_HEADS * FOX_DH
    sizes = (hk, hk, hv, GLA_LOWRANK, hv, fw, fw, fw, FOX_HEADS, fw)
    offs = [0]
    for s in sizes:
        offs.append(offs[-1] + s)
    w_main, w_small = ab_weight_prep(jnp.swapaxes(ab_w_in, 1, 2), offs)
    c_gq, c_gk, c_gv, c_gg = 0, hk, 2 * hk, 2 * hk + hv
    c_fq = c_gg + hv
    c_fk, c_fv, c_fg = c_fq + fw, c_fq + 2 * fw, c_fq + 3 * fw

    proj, small = norm_proj(x2, ab_norm_g[0], w_main, w_small, tn=w_main.shape[0] // 2, name="ab_in")

    o_gla = gla(proj, small, gla_alpha_up[0], gla_alpha_b[0], gla_head_g[0],
                batch=batch, seq=seq, q_col=c_gq, k_col=c_gk, v_col=c_gv, gate_col=c_gg)

    fb_row = jnp.zeros((1, 128), F32).at[0, GLA_LOWRANK:GLA_LOWRANK + FOX_HEADS].set(fox_f_b[0])
    qa, ka = fox_gate(small, fb_row, batch=batch, seq=seq, col0=GLA_LOWRANK)
    o_fox = fox(proj, qa, ka, batch=batch, seq=seq, q_col=c_fq, k_col=c_fk, v_col=c_fv, gate_col=c_fg)

    x1, h1g = outproj_mid(o_gla, o_fox, ab_w_out[0], x2, c_norm_g[0], steps=S5_T, name="ab_out")

    e = d
    groups = e // S5_GROUP_CH
    n_chunks = t // S5_T
    cols_per_seq = seq // S5_T
    n_levels = S5_SCAN_BLK.bit_length() - 1
    proj_t = matmul_t_grouped(c_w_in[0], h1g, S5_T, BF16, name="c_in")
    ops = s5_operators(s5_lambda_re[0], s5_lambda_im[0], s5_log_step[0], s5_b_re[0], s5_b_im[0],
                       s5_c_re[0], s5_c_im[0], s5_d[0], n_levels=n_levels, n_seq=batch)
    y_t = s5_apply(proj_t, *ops, groups=groups, cols_per_seq=cols_per_seq, n_levels=n_levels)
    zz = glu_gate(y_t, glu_w[0], glu_b[0], proj_t, gate_row=e)
    out = outproj_final(zz, c_w_out[0], x1, final_norm_g, steps=S5_T, name="c_out")
    return out.reshape(batch, seq, d)
```

```python
import functools
import math

import jax
import jax.numpy as jnp
from jax import lax
from jax.experimental import pallas as pl
from jax.experimental.pallas import tpu as pltpu

EPS = 1e-6
F32 = jnp.float32
BF16 = jnp.bfloat16

GLA_HEADS = 4
GLA_DK = 128
GLA_DV = 256
GLA_LOWRANK = 16
GLA_TAU = 16.0
GLA_CHUNK = 64
GLA_SUB = 8
GLA_UNROLL = 4
FOX_HEADS = 8
FOX_DH = 128
FOX_AUG = 128
FOX_AUG_GROUP = 8
S5_GROUP_CH = 16
S5_STATE = 64
S5_T = 16

VMEM_LIMIT = 56 * 1024 * 1024

NT_DIMS = (((1,), (1,)), ((), ()))
TN_DIMS = (((0,), (0,)), ((), ()))
LOG2E = math.log2(math.e)


def _cparams(sem):
    return pltpu.CompilerParams(dimension_semantics=sem, vmem_limit_bytes=VMEM_LIMIT)


def _dot(a, b):
    return jnp.dot(a, b, preferred_element_type=F32)


def _dot_nt(a, b):
    return lax.dot_general(a, b, NT_DIMS, preferred_element_type=F32)


def _dot_tn(a, b):
    return lax.dot_general(a, b, TN_DIMS, preferred_element_type=F32)


def _log_sigmoid(z):
    return jnp.minimum(z, 0.0) - jnp.log(1.0 + jnp.exp(-jnp.abs(z)))


def _sigmoid(z):
    return 0.5 * jnp.tanh(0.5 * z) + 0.5


def _silu(z):
    return z * _sigmoid(z)


def _split3(x):
    hi = x.astype(BF16)
    r1 = x - hi.astype(F32)
    mid = r1.astype(BF16)
    lo = (r1 - mid.astype(F32)).astype(BF16)
    return hi, mid, lo


def _norm_proj_body(x_ref, g_ref, wt_ref, ws_ref, o_ref, s_ref):
    x = x_ref[...]
    ms = jnp.mean(x * x, axis=-1, keepdims=True)
    h = (x * lax.rsqrt(ms + EPS) * g_ref[...]).astype(BF16)
    for c0 in range(0, o_ref.shape[1], MM_SUB):
        o_ref[:, c0:c0 + MM_SUB] = _dot_nt(h, wt_ref[c0:c0 + MM_SUB, :]).astype(o_ref.dtype)
    s_ref[...] = _dot_nt(h, ws_ref[...])


def norm_proj(x, g, wt, wt_small, tm=512, tn=512, name="norm_proj"):
    m, d = x.shape
    n = wt.shape[0]
    ns = wt_small.shape[0]
    proj, small = pl.pallas_call(
        _norm_proj_body,
        grid=(n // tn, m // tm),
        in_specs=[pl.BlockSpec((tm, d), lambda j, i: (i, 0)),
                  pl.BlockSpec((1, d), lambda j, i: (0, 0)),
                  pl.BlockSpec((tn, d), lambda j, i: (j, 0)),
                  pl.BlockSpec((ns, d), lambda j, i: (0, 0))],
        out_specs=[pl.BlockSpec((tm, tn), lambda j, i: (i, j)),
                   pl.BlockSpec((None, tm, ns), lambda j, i: (j, i, 0))],
        out_shape=[jax.ShapeDtypeStruct((m, n), BF16), jax.ShapeDtypeStruct((n // tn, m, ns), F32)],
        compiler_params=_cparams(("arbitrary", "arbitrary")),
        name=name,
    )(x, g.reshape(1, d), wt, wt_small)
    return proj, small[0]


def _ab_wprep_body(w_ref, o_ref, s_ref, *, offs):
    w = w_ref[0]
    seg = lambda n: w[offs[n]:offs[n + 1], :]
    o_ref[...] = jnp.concatenate(
        [w[:offs[3], :], seg(4), seg(5) * (FOX_DH ** -0.5 * LOG2E), w[offs[6]:offs[8], :], seg(9)],
        axis=0).astype(o_ref.dtype)
    pad = s_ref.shape[0] - (offs[4] - offs[3]) - (offs[9] - offs[8])
    s_ref[...] = jnp.concatenate([seg(3), seg(8), jnp.zeros((pad, w.shape[1]), F32)], axis=0).astype(s_ref.dtype)


def ab_weight_prep(w_t3, offs, tk=256):
    _, n_in, d = w_t3.shape
    n_main = n_in - (offs[4] - offs[3]) - (offs[9] - offs[8])
    return pl.pallas_call(
        functools.partial(_ab_wprep_body, offs=tuple(offs)),
        grid=(d // tk,),
        in_specs=[pl.BlockSpec((1, n_in, tk), lambda i: (0, 0, i))],
        out_specs=[pl.BlockSpec((n_main, tk), lambda i: (0, i)), pl.BlockSpec((128, tk), lambda i: (0, i))],
        out_shape=[jax.ShapeDtypeStruct((n_main, d), BF16), jax.ShapeDtypeStruct((128, d), BF16)],
        compiler_params=_cparams(("parallel",)),
        name="ab_wprep",
    )(w_t3)


MM_SUB = 512


def _transpose_into(wt_ref, w_ref):
    for c0 in range(0, w_ref.shape[1], MM_SUB):
        wt_ref[c0:c0 + MM_SUB, :] = w_ref[:, c0:c0 + MM_SUB].astype(wt_ref.dtype).T


def _mm_t_body(w_ref, a_ref, o_ref, wt_ref):
    @pl.when((pl.program_id(1) == 0) & (pl.program_id(2) == 0))
    def _():
        _transpose_into(wt_ref, w_ref)

    a = a_ref[...]
    for r0 in range(0, o_ref.shape[0], MM_SUB):
        o_ref[r0:r0 + MM_SUB, :] = _dot_nt(wt_ref[r0:r0 + MM_SUB, :], a).astype(o_ref.dtype)


def matmul_t_grouped(w, a2, steps, out_dtype, tm=512, tn=2048, name="matmul_tg"):
    k, nf = w.shape
    n = a2.shape[0]
    return pl.pallas_call(
        _mm_t_body,
        grid=(nf // tn, steps, n // tm),
        in_specs=[pl.BlockSpec((k, tn), lambda j, t, i: (0, j), pipeline_mode=pl.Buffered(1)),
                  pl.BlockSpec((tm, k), lambda j, t, i: (i, t))],
        out_specs=pl.BlockSpec((None, tn, tm), lambda j, t, i: (t, j, i)),
        out_shape=jax.ShapeDtypeStruct((steps, nf, n), out_dtype),
        scratch_shapes=[pltpu.VMEM((tn, k), BF16)],
        compiler_params=_cparams(("arbitrary", "arbitrary", "arbitrary")),
        name=name,
    )(w, a2)


def _gla_body(q_ref, k_ref, v_ref, glr_ref, aup_ref, ab_ref, gate_ref, hg_ref, o_ref,
              s_ref, b_ref, kf_ref, kts_ref, dec_ref, *, tb, hpb):
    ib = pl.program_id(2)
    C, SB, dk, dv = GLA_CHUNK, GLA_SUB, GLA_DK, GLA_DV

    @pl.when(ib == 0)
    def _():
        s_ref[...] = jnp.zeros_like(s_ref)

    glr = glr_ref[:, :GLA_LOWRANK].astype(BF16)
    tw = 2 * C
    r = lax.broadcasted_iota(jnp.int32, (tw, tw), 0)
    c = lax.broadcasted_iota(jnp.int32, (tw, tw), 1)
    tri = jnp.where((r // C == c // C) & (c <= r), 1.0, 0.0).astype(BF16)
    nc = tb // C
    heads = range(hpb)
    kcs = [slice(hh * dk, (hh + 1) * dk) for hh in heads]
    zs = [_dot(glr, aup_ref[:, kcs[hh]].astype(BF16)) + ab_ref[:, kcs[hh]] for hh in heads]
    b_blks = []
    for hh in heads:
        la = _log_sigmoid(zs[hh]) * (1.0 / GLA_TAU)
        hi = la.astype(BF16)
        lo = (la - hi.astype(F32)).astype(BF16)
        b_blks.append(jnp.concatenate(
            [_dot(tri, hi[r0:r0 + tw]) + _dot(tri, lo[r0:r0 + tw]) for r0 in range(0, tb, tw)], axis=0))
    for hh in heads:
        b_blk = b_blks[hh]
        k_blk = k_ref[:, kcs[hh]].astype(F32)
        b_ref[hh] = b_blk
        kf_ref[hh] = k_blk
        b3 = b_blk.reshape(nc, C, dk)
        b_end = b3[:, C - 1:C, :]
        k_end = (k_blk.reshape(nc, C, dk) * jnp.exp(b_end - b3)).reshape(tb, dk)
        kt_end = k_end.T.astype(BF16)
        dec_t = jnp.exp(b_end.reshape(nc, dk)).T
        for ci in range(nc):
            kts_ref[hh, ci] = kt_end[:, ci * C:(ci + 1) * C]
            dec_ref[hh, ci] = jnp.broadcast_to(dec_t[:, ci:ci + 1], (dk, 128))

    row = lax.broadcasted_iota(jnp.int32, (C, 1), 0)
    rowi = lax.broadcasted_iota(jnp.int32, (C, C), 0)
    coli = lax.broadcasted_iota(jnp.int32, (C, C), 1)
    lane_c = lax.broadcasted_iota(jnp.int32, (SB, C), 1)
    sub_r = lax.broadcasted_iota(jnp.int32, (SB, 1), 0)
    ones = jnp.ones((GLA_DK, C), BF16)
    neg = -jnp.inf
    scale = GLA_DK ** -0.5

    def stage_state(ci, r0, hh):
        kc = slice(hh * dk, (hh + 1) * dk)
        vc = slice(hh * dv, (hh + 1) * dv)
        q = q_ref[pl.ds(r0, C), kc].astype(F32) * scale
        k = kf_ref[hh, pl.ds(r0, C), :]
        b = b_ref[hh, pl.ds(r0, C), :]
        v = v_ref[pl.ds(r0, C), vc]

        s = s_ref[hh]
        o = _dot((q * jnp.exp(b)).astype(BF16), s.astype(BF16))
        dec = dec_ref[hh, ci]
        s_ref[hh] = s * jnp.concatenate([dec] * (dv // 128), axis=1) + _dot(kts_ref[hh, ci], v)

        attn = jnp.zeros((C, C), F32)
        h = C // 2
        while h >= SB:
            ref = jnp.broadcast_to(b.reshape(C // (2 * h), 2 * h, GLA_DK)[:, h - 1:h, :],
                                   (C // (2 * h), 2 * h, GLA_DK)).reshape(C, GLA_DK)
            upper = (row // h) % 2 == 1
            q_h = q * jnp.exp(jnp.where(upper, b - ref, neg))
            k_h = k * jnp.exp(jnp.where(upper, neg, ref - b))
            a_h = _dot_nt(q_h.astype(BF16), k_h.astype(BF16))
            attn = attn + (a_h if 2 * h == C else jnp.where(rowi // (2 * h) == coli // (2 * h), a_h, 0.0))
            h //= 2
        return q, b, v, o, attn

    def stage_diag(r0, hh, q, b):
        zs = []
        for bi in range(C // SB):
            s0 = bi * SB
            q_i = q[s0:s0 + SB, :]
            b_i = b[s0:s0 + SB, :]
            for j in range(SB):
                k_j = kf_ref[hh, pl.ds(r0 + (s0 + j), 1), :]
                b_j = b_ref[hh, pl.ds(r0 + (s0 + j), 1), :]
                zs.append(q_i * k_j * jnp.exp(jnp.where(sub_r >= j, b_i - b_j, neg)))
        return _dot(jnp.concatenate(zs, axis=0).astype(BF16), ones)

    def stage_out(r0, hh, v, o, attn, zsum):
        vc = slice(hh * dv, (hh + 1) * dv)
        diag = []
        for bi in range(C // SB):
            acc = jnp.zeros((SB, C), F32)
            for j in range(SB):
                n0 = (bi * SB + j) * SB
                acc = acc + jnp.where(lane_c == bi * SB + j, zsum[n0:n0 + SB, :], 0.0)
            diag.append(acc)
        attn = attn + jnp.concatenate(diag, axis=0)
        o = o + _dot(attn.astype(BF16), v)

        o = o * lax.rsqrt(jnp.mean(o * o, axis=-1, keepdims=True) + EPS) * hg_ref[:, vc]
        g = gate_ref[pl.ds(r0, C), vc].astype(F32)
        o_ref[pl.ds(r0, C), vc] = (o * _silu(g)).astype(o_ref.dtype)

    def chunk(ci, carry):
        r0 = pl.multiple_of(ci * C, C)
        heads = range(hpb)
        st = [stage_state(ci, r0, hh) for hh in heads]
        zsums = [stage_diag(r0, hh, st[hh][0], st[hh][1]) for hh in heads]
        for hh in heads:
            _, _, v, o, attn = st[hh]
            stage_out(r0, hh, v, o, attn, zsums[hh])
        return carry

    lax.fori_loop(0, tb // C, chunk, 0, unroll=GLA_UNROLL)


def gla(proj, small, alpha_up, alpha_b, head_g, *, batch, seq, q_col, k_col, v_col, gate_col, tb=512, hpb=4):
    t = batch * seq
    nb = seq // tb
    dk, dv, h = GLA_DK * hpb, GLA_DV * hpb, GLA_HEADS // hpb
    qb, kb, vb, gb = q_col // dk, k_col // dk, v_col // dv, gate_col // dv
    tok = lambda b, hh, i: b * nb + i
    nc = tb // GLA_CHUNK
    return pl.pallas_call(
        functools.partial(_gla_body, tb=tb, hpb=hpb),
        grid=(batch, h, nb),
        in_specs=[
            pl.BlockSpec((tb, dk), lambda b, hh, i: (tok(b, hh, i), qb + hh)),
            pl.BlockSpec((tb, dk), lambda b, hh, i: (tok(b, hh, i), kb + hh)),
            pl.BlockSpec((tb, dv), lambda b, hh, i: (tok(b, hh, i), vb + hh)),
            pl.BlockSpec((tb, 128), lambda b, hh, i: (tok(b, hh, i), 0)),
            pl.BlockSpec((GLA_LOWRANK, dk), lambda b, hh, i: (0, hh)),
            pl.BlockSpec((1, dk), lambda b, hh, i: (0, hh)),
            pl.BlockSpec((tb, dv), lambda b, hh, i: (tok(b, hh, i), gb + hh)),
            pl.BlockSpec((1, dv), lambda b, hh, i: (0, hh)),
        ],
        out_specs=pl.BlockSpec((tb, dv), lambda b, hh, i: (tok(b, hh, i), hh)),
        out_shape=jax.ShapeDtypeStruct((t, GLA_HEADS * GLA_DV), BF16),
        scratch_shapes=[pltpu.VMEM((hpb, GLA_DK, GLA_DV), F32), pltpu.VMEM((hpb, tb, GLA_DK), F32),
                        pltpu.VMEM((hpb, tb, GLA_DK), F32),
                        pltpu.VMEM((hpb, nc, GLA_DK, GLA_CHUNK), BF16),
                        pltpu.VMEM((hpb, nc, GLA_DK, 128), F32)],
        compiler_params=_cparams(("parallel", "parallel", "arbitrary")),
        name="gla",
    )(proj, proj, proj, small, alpha_up, alpha_b.reshape(1, -1), proj, head_g.reshape(1, -1))


FOX_GATE_UNROLL = 4


def _fox_gate_body(s_ref, fb_ref, qa_ref, ka_ref, cs_ref, off_ref, *, blk, col0):
    n = s_ref.shape[0] // blk
    r = lax.broadcasted_iota(jnp.int32, (blk, blk), 0)
    c = lax.broadcasted_iota(jnp.int32, (blk, blk), 1)
    tri = jnp.where(c <= r, 1.0, 0.0).astype(BF16)
    lanes = s_ref.shape[1]
    pr = lax.broadcasted_iota(jnp.int32, (3 * lanes, FOX_AUG), 0)
    pc = lax.broadcasted_iota(jnp.int32, (3 * lanes, FOX_AUG), 1)
    head, piece = pr % lanes - col0, pr // lanes
    is_head = (head >= 0) & (head < FOX_HEADS)
    sel_q = jnp.where(is_head & (pc == head * FOX_AUG_GROUP + piece), 1.0, 0.0).astype(BF16)
    sel_k = jnp.where(is_head & (pc == head * FOX_AUG_GROUP + 3 + piece), -1.0, 0.0).astype(BF16)
    lane = lax.broadcasted_iota(jnp.int32, (1, FOX_AUG), 1)
    used = lane < FOX_HEADS * FOX_AUG_GROUP
    one_q = jnp.where(used & (lane % FOX_AUG_GROUP >= 3) & (lane % FOX_AUG_GROUP < 6), 1.0, 0.0)
    one_k = jnp.where(used & (lane % FOX_AUG_GROUP < 3), 1.0, 0.0)

    def local(i, carry):
        r0 = pl.multiple_of(i * blk, blk)
        lf = _log_sigmoid(s_ref[pl.ds(r0, blk), :] + fb_ref[...])
        hi, mid, lo = _split3(lf)
        cs_ref[pl.ds(r0, blk), :] = _dot(tri, hi) + _dot(tri, mid) + _dot(tri, lo)
        return carry

    lax.fori_loop(0, n, local, 0, unroll=FOX_GATE_UNROLL)

    off = jnp.zeros((1, lanes), F32)
    for i in range(n):
        off_ref[i:i + 1, :] = off
        off = off + cs_ref[(i + 1) * blk - 1:(i + 1) * blk, :]

    def place(i, carry):
        r0 = pl.multiple_of(i * blk, blk)
        cs = cs_ref[pl.ds(r0, blk), :] + off_ref[pl.ds(i, 1), :]
        pieces = jnp.concatenate(_split3(cs * LOG2E), axis=1)
        qa_ref[pl.ds(r0, blk), :] = (_dot(pieces, sel_q) + one_q).astype(BF16)
        ka_ref[pl.ds(r0, blk), :] = (_dot(pieces, sel_k) + one_k).astype(BF16)
        return carry

    lax.fori_loop(0, n, place, 0, unroll=FOX_GATE_UNROLL)


def fox_gate(small, fb_row, *, batch, seq, col0, blk=256):
    shp = jax.ShapeDtypeStruct((batch * seq, FOX_AUG), BF16)
    spec = pl.BlockSpec((seq, FOX_AUG), lambda b: (b, 0))
    return pl.pallas_call(
        functools.partial(_fox_gate_body, blk=blk, col0=col0),
        grid=(batch,),
        in_specs=[pl.BlockSpec((seq, 128), lambda b: (b, 0)),
                  pl.BlockSpec((1, 128), lambda b: (0, 0))],
        out_specs=[spec, spec],
        out_shape=[shp, shp],
        scratch_shapes=[pltpu.VMEM((seq, 128), F32), pltpu.VMEM((seq // blk, 128), F32)],
        compiler_params=_cparams(("parallel",)),
        name="fox_gate",
    )(small, fb_row)


def _fox_body(it_ref, jt_ref, q_ref, qa_ref, k_ref, ka_ref, v_ref, gate_ref, o_ref,
              m_ref, l_ref, acc_ref, *, tq, tk, hpb):
    p = pl.program_id(2)
    i = it_ref[p]
    j = jt_ref[p]
    d = FOX_DH

    @pl.when(j == 0)
    def _():
        m_ref[...] = jnp.full_like(m_ref, -jnp.inf)
        l_ref[...] = jnp.zeros_like(l_ref)
        acc_ref[...] = jnp.zeros_like(acc_ref)

    aug_lane = lax.broadcasted_iota(jnp.int32, (1, FOX_AUG), 1)

    def scores(item, masked):
        hh, ks, qs = item
        mine = aug_lane // FOX_AUG_GROUP == pl.program_id(1) * hpb + hh
        zero = jnp.zeros((), BF16)
        q_aug = jnp.concatenate([q_ref[qs, hh * d:(hh + 1) * d], jnp.where(mine, qa_ref[qs, :], zero)], axis=1)
        k_aug = jnp.concatenate([k_ref[ks, hh * d:(hh + 1) * d], jnp.where(mine, ka_ref[ks, :], zero)], axis=1)
        st = _dot_nt(k_aug, q_aug)
        if masked:
            kr = lax.broadcasted_iota(jnp.int32, st.shape, 0) + ks.start
            qc = lax.broadcasted_iota(jnp.int32, st.shape, 1) + qs.start
            st = jnp.where(qc >= kr, st, -jnp.inf)
        return st

    def absorb(item, st):
        hh, ks, qs = item
        m_old = m_ref[hh, :, qs]
        m_new = jnp.maximum(m_old, jnp.max(st, axis=0, keepdims=True))
        alpha = jnp.exp2(m_old - m_new)
        pt = jnp.exp2(st - m_new)
        l_ref[hh, :, qs] = alpha * l_ref[hh, :, qs] + jnp.sum(pt, axis=0, keepdims=True)
        acc_ref[hh, :, qs] = (alpha * acc_ref[hh, :, qs]
                              + _dot_tn(v_ref[ks, hh * d:(hh + 1) * d], pt.astype(BF16)))
        m_ref[hh, :, qs] = m_new

    def block(masked, finish):
        full = slice(0, tq)
        halves = [(slice(0, tk // 2), full), (slice(tk // 2, tk), slice(tq // 2, tq))] if masked else [(slice(0, tk), full)]
        items = [(hh, ks, qs) for hh in range(hpb) for ks, qs in halves]
        st = scores(items[0], masked)
        for n, item in enumerate(items):
            st_next = scores(items[n + 1], masked) if n + 1 < len(items) else None
            absorb(item, st)
            hh = item[0]
            if finish and (n + 1 == len(items) or items[n + 1][0] != hh):
                o = (acc_ref[hh] / l_ref[hh]).T
                g = gate_ref[:, hh * d:(hh + 1) * d].astype(F32)
                o_ref[:, hh * d:(hh + 1) * d] = (o * _silu(g)).astype(o_ref.dtype)
            st = st_next

    @pl.when(j < i)
    def _():
        block(False, False)

    @pl.when(j == i)
    def _():
        block(True, True)


def fox(proj, qa, ka, *, batch, seq, q_col, k_col, v_col, gate_col, t=1024, hpb=8):
    tt = batch * seq
    nb = seq // t
    d, h = FOX_DH, FOX_HEADS
    w = hpb * d
    qb, kb, gb, vb = q_col // w, k_col // w, gate_col // w, v_col // w
    pairs = [(i, j) for i in range(nb) for j in range(i + 1)]
    it = jnp.array([p[0] for p in pairs], jnp.int32)
    jt = jnp.array([p[1] for p in pairs], jnp.int32)
    grid_spec = pltpu.PrefetchScalarGridSpec(
        num_scalar_prefetch=2,
        grid=(batch, h // hpb, len(pairs)),
        in_specs=[
            pl.BlockSpec((t, w), lambda b, hh, p, it, jt: (b * nb + it[p], qb + hh)),
            pl.BlockSpec((t, FOX_AUG), lambda b, hh, p, it, jt: (b * nb + it[p], 0)),
            pl.BlockSpec((t, w), lambda b, hh, p, it, jt: (b * nb + jt[p], kb + hh)),
            pl.BlockSpec((t, FOX_AUG), lambda b, hh, p, it, jt: (b * nb + jt[p], 0)),
            pl.BlockSpec((t, w), lambda b, hh, p, it, jt: (b * nb + jt[p], vb + hh)),
            pl.BlockSpec((t, w), lambda b, hh, p, it, jt: (b * nb + it[p], gb + hh)),
        ],
        out_specs=pl.BlockSpec((t, w), lambda b, hh, p, it, jt: (b * nb + it[p], hh)),
        scratch_shapes=[pltpu.VMEM((hpb, 1, t), F32), pltpu.VMEM((hpb, 1, t), F32), pltpu.VMEM((hpb, d, t), F32)],
    )
    return pl.pallas_call(
        functools.partial(_fox_body, tq=t, tk=t, hpb=hpb),
        grid_spec=grid_spec,
        out_shape=jax.ShapeDtypeStruct((tt, h * d), BF16),
        compiler_params=_cparams(("parallel", "parallel", "arbitrary")),
        name="fox",
    )(it, jt, proj, qa, proj, ka, proj, proj)


def _regroup_perm(tm, steps, to_grouped):
    r = lax.broadcasted_iota(jnp.int32, (tm, tm), 0)
    c = lax.broadcasted_iota(jnp.int32, (tm, tm), 1)
    nc = tm // steps
    src = (r % nc) * steps + r // nc if to_grouped else (r % steps) * nc + r // steps
    return jnp.where(c == src, 1.0, 0.0).astype(BF16)


def _cast_weight_once(wb_ref, w_ref):
    @pl.when(pl.program_id(0) == 0)
    def _():
        for r0 in range(0, w_ref.shape[0], MM_SUB):
            wb_ref[r0:r0 + MM_SUB, :] = w_ref[r0:r0 + MM_SUB, :].astype(wb_ref.dtype)


def _outproj_mid_body(a1_ref, a2_ref, wf_ref, x_ref, g_ref, r_ref, h_ref, w_ref, *, steps):
    k1 = a1_ref.shape[1]
    tm, d = x_ref.shape
    nc = tm // steps
    _cast_weight_once(w_ref, wf_ref)
    acc = x_ref[...] + _dot(a1_ref[...], w_ref[:k1, :]) + _dot(a2_ref[...], w_ref[k1:, :])
    r_ref[...] = acc
    normed = (acc * lax.rsqrt(jnp.mean(acc * acc, axis=-1, keepdims=True) + EPS) * g_ref[...]).astype(BF16)
    by_step = _dot(_regroup_perm(tm, steps, True), normed).astype(h_ref.dtype)
    for s in range(steps):
        h_ref[:, s * d:(s + 1) * d] = by_step[s * nc:(s + 1) * nc, :]


def outproj_mid(a1, a2, w, x, g, *, steps, tm=256, name="outproj_mid"):
    m, d = x.shape
    row = pl.BlockSpec((tm, d), lambda i: (i, 0))
    return pl.pallas_call(
        functools.partial(_outproj_mid_body, steps=steps),
        grid=(m // tm,),
        in_specs=[pl.BlockSpec((tm, a1.shape[1]), lambda i: (i, 0)),
                  pl.BlockSpec((tm, a2.shape[1]), lambda i: (i, 0)),
                  pl.BlockSpec(w.shape, lambda i: (0, 0), pipeline_mode=pl.Buffered(1)),
                  row,
                  pl.BlockSpec((1, d), lambda i: (0, 0))],
        out_specs=[row, pl.BlockSpec((tm // steps, steps * d), lambda i: (i, 0))],
        out_shape=[jax.ShapeDtypeStruct((m, d), F32), jax.ShapeDtypeStruct((m // steps, steps * d), BF16)],
        scratch_shapes=[pltpu.VMEM(w.shape, BF16)],
        compiler_params=_cparams(("arbitrary",)),
        name=name,
    )(a1, a2, w, x, g.reshape(1, d))


def _outproj_final_body(a_ref, wf_ref, x_ref, g_ref, o_ref, w_ref, *, steps):
    tm = x_ref.shape[0]
    e = w_ref.shape[0]
    by_step = jnp.concatenate([a_ref[:, s * e:(s + 1) * e] for s in range(steps)], axis=0)
    a_tok = _dot(_regroup_perm(tm, steps, False), by_step).astype(BF16)
    _cast_weight_once(w_ref, wf_ref)
    acc = x_ref[...] + _dot(a_tok, w_ref[...])
    o_ref[...] = acc * lax.rsqrt(jnp.mean(acc * acc, axis=-1, keepdims=True) + EPS) * g_ref[...]


def outproj_final(a_grouped, w, x, g, *, steps, tm=256, name="outproj_final"):
    m, d = x.shape
    e = w.shape[0]
    row = pl.BlockSpec((tm, d), lambda i: (i, 0))
    return pl.pallas_call(
        functools.partial(_outproj_final_body, steps=steps),
        grid=(m // tm,),
        in_specs=[pl.BlockSpec((tm // steps, steps * e), lambda i: (i, 0)),
                  pl.BlockSpec(w.shape, lambda i: (0, 0), pipeline_mode=pl.Buffered(1)),
                  row,
                  pl.BlockSpec((1, d), lambda i: (0, 0))],
        out_specs=row,
        out_shape=jax.ShapeDtypeStruct((m, d), F32),
        scratch_shapes=[pltpu.VMEM(w.shape, BF16)],
        compiler_params=_cparams(("arbitrary",)),
        name=name,
    )(a_grouped, w, x, g.reshape(1, d))


S5_SCAN_BLK = 8
S5_GPB = 8


def _s5_entering_state(x, arow, *, cols_per_seq, n_levels):
    n = x.shape[1]
    p = S5_STATE
    nseq = n // cols_per_seq

    def to_rows(part):
        return jnp.concatenate([part[:, s * cols_per_seq:(s + 1) * cols_per_seq] for s in range(nseq)], axis=0).T

    def to_cols(rows):
        rt = rows.T
        return jnp.concatenate([rt[s * p:(s + 1) * p, :] for s in range(nseq)], axis=1)

    xr, xi = to_rows(x[:p, :]), to_rows(x[p:, :])
    pos = lax.broadcasted_iota(jnp.int32, (cols_per_seq, 1), 0)
    blk = S5_SCAN_BLK
    half = arow.shape[0] // 2
    for lvl in range(n_levels):
        sh = 1 << lvl
        keep = pos % blk >= sh
        sr = jnp.where(keep, pltpu.roll(xr, sh, axis=0), 0.0)
        si = jnp.where(keep, pltpu.roll(xi, sh, axis=0), 0.0)
        ar, ai = arow[lvl:lvl + 1, :], arow[half + lvl:half + lvl + 1, :]
        xr, xi = xr + ar * sr - ai * si, xi + ar * si + ai * sr
    pw_r, pw_i = arow[n_levels:n_levels + blk, :], arow[half + n_levels:half + n_levels + blk, :]
    out_r, out_i = [xr[:blk, :]], [xi[:blk, :]]
    for r0 in range(blk, cols_per_seq, blk):
        cr, ci = out_r[-1][blk - 1:blk, :], out_i[-1][blk - 1:blk, :]
        out_r.append(xr[r0:r0 + blk, :] + pw_r * cr - pw_i * ci)
        out_i.append(xi[r0:r0 + blk, :] + pw_r * ci + pw_i * cr)
    xr, xi = jnp.concatenate(out_r, axis=0), jnp.concatenate(out_i, axis=0)
    keep = pos >= 1
    pr = jnp.where(keep, pltpu.roll(xr, 1, axis=0), 0.0)
    pi = jnp.where(keep, pltpu.roll(xi, 1, axis=0), 0.0)
    return jnp.concatenate([to_cols(pr), to_cols(pi)], axis=0).astype(BF16)


def _s5_body(u_ref, wt_ref, nt_ref, mt_ref, arow_ref, o_ref, *, cols_per_seq, n_levels):
    t, _, n = u_ref.shape
    cg = S5_GROUP_CH
    groups = range(S5_GPB)
    ch = [slice(gi * cg, (gi + 1) * cg) for gi in groups]
    us = [u_ref[:, ch[gi], :].reshape(t * cg, n) for gi in groups]
    xs = [_dot(nt_ref[gi], us[gi]) for gi in groups]
    ys = [_dot(wt_ref[gi], us[gi]) for gi in groups]
    prevs = [_s5_entering_state(xs[gi], arow_ref[gi], cols_per_seq=cols_per_seq, n_levels=n_levels)
             for gi in groups]
    for gi in groups:
        y = ys[gi] + _dot(mt_ref[gi], prevs[gi])
        o_ref[:, ch[gi], :] = y.astype(o_ref.dtype).reshape(t, cg, n)


def s5_apply(proj_t, wt_g, nt_g, mt_g, arow, *, groups, cols_per_seq, n_levels):
    t, _, n = proj_t.shape
    cg = S5_GROUP_CH * S5_GPB
    spec3 = lambda shp: pl.BlockSpec((S5_GPB,) + shp, lambda i: (i, 0, 0))
    return pl.pallas_call(
        functools.partial(_s5_body, cols_per_seq=cols_per_seq, n_levels=n_levels),
        grid=(groups // S5_GPB,),
        in_specs=[pl.BlockSpec((t, cg, n), lambda i: (0, i, 0)),
                  spec3(wt_g.shape[1:]), spec3(nt_g.shape[1:]), spec3(mt_g.shape[1:]), spec3(arow.shape[1:])],
        out_specs=pl.BlockSpec((t, cg, n), lambda i: (0, i, 0)),
        out_shape=jax.ShapeDtypeStruct((t, groups * S5_GROUP_CH, n), BF16),
        compiler_params=_cparams(("parallel",)),
        name="s5",
    )(proj_t, wt_g, nt_g, mt_g, arow)


def s5_operators(lam_re, lam_im, log_step, b_re, b_im, c_re, c_im, d, *, n_levels, n_seq):
    hp = lax.Precision.HIGHEST
    g, p = lam_re.shape
    cg, t = S5_GROUP_CH, S5_T
    lr = jnp.minimum(lam_re.astype(F32), -1e-4)
    li = lam_im.astype(F32)
    step = jnp.exp(log_step.astype(F32))[:, None]
    mag = jnp.exp(lr * step)
    lb_re, lb_im = mag * jnp.cos(li * step), mag * jnp.sin(li * step)
    den = lr * lr + li * li
    nr, ni = lb_re - 1.0, lb_im
    coef_re = (nr * lr + ni * li) / den
    coef_im = (ni * lr - nr * li) / den
    bb_re = coef_re[..., None] * b_re - coef_im[..., None] * b_im
    bb_im = coef_re[..., None] * b_im + coef_im[..., None] * b_re

    def power(m):
        mm = m.astype(F32)[None, :, None]
        mg = jnp.exp(lr[:, None, :] * step[:, None, :] * mm)
        ang = li[:, None, :] * step[:, None, :] * mm
        return mg * jnp.cos(ang), mg * jnp.sin(ang)

    pw_re, pw_im = power(jnp.arange(t + 1))
    pwt_re, pwt_im = pw_re.transpose(0, 2, 1), pw_im.transpose(0, 2, 1)

    def pow_times_b(pr, pi):
        re = pr[:, :, :, None] * bb_re[:, :, None, :] - pi[:, :, :, None] * bb_im[:, :, None, :]
        im = pr[:, :, :, None] * bb_im[:, :, None, :] + pi[:, :, :, None] * bb_re[:, :, None, :]
        return re.reshape(g, p, t * cg), im.reshape(g, p, t * cg)

    pb_re, pb_im = pow_times_b(pwt_re[:, :, :t], pwt_im[:, :, :t])
    taps = (jnp.einsum('gcp,gpy->gcy', c_re.astype(F32), pb_re, precision=hp)
            - jnp.einsum('gcp,gpy->gcy', c_im.astype(F32), pb_im, precision=hp))
    y_idx = jnp.arange(t * cg)
    taps = taps + d.astype(F32).reshape(g, cg, 1) * (y_idx[None, :] == jnp.arange(cg)[:, None]).astype(F32)[None]
    place = ((y_idx[None, :, None] // cg == jnp.arange(t)[:, None, None] - y_idx[None, None, :] // cg)
             & (y_idx[None, :, None] % cg == y_idx[None, None, :] % cg)).astype(BF16)
    wt_g = jnp.einsum('gcy,tyz->gtcz', taps.astype(BF16), place,
                      preferred_element_type=F32).astype(BF16).reshape(g, t * cg, t * cg)
    n_re, n_im = pow_times_b(pwt_re[:, :, t - 1::-1], pwt_im[:, :, t - 1::-1])
    nt_g = jnp.concatenate([n_re, n_im], axis=1).astype(BF16)
    cp_re = c_re[:, None] * pw_re[:, 1:, None, :] - c_im[:, None] * pw_im[:, 1:, None, :]
    cp_im = c_re[:, None] * pw_im[:, 1:, None, :] + c_im[:, None] * pw_re[:, 1:, None, :]
    mt_g = jnp.concatenate([cp_re, -cp_im], axis=-1).reshape(g, t * cg, 2 * p).astype(BF16)
    ar, ai = power(t * jnp.concatenate([2 ** jnp.arange(n_levels), jnp.arange(1, S5_SCAN_BLK + 1)]))
    arow = jnp.concatenate([jnp.tile(ar, (1, 1, n_seq)), jnp.tile(ai, (1, 1, n_seq))], axis=1)
    return wt_g, nt_g, mt_g, arow


def _gelu_tanh(y):
    return 0.5 * y * (1.0 + jnp.tanh(math.sqrt(2.0 / math.pi) * (y + 0.044715 * (y * y * y))))


GLU_PARTS = 2


def _glu_body(y_ref, w_ref, b_ref, gate_ref, o_ref, z_ref, zb_ref, wt_ref):
    @pl.when((pl.program_id(0) == 0) & (pl.program_id(1) == 0))
    def _():
        _transpose_into(wt_ref, w_ref)

    tm = y_ref.shape[1]
    parts = [slice(c0, c0 + tm // GLU_PARTS) for c0 in range(0, tm, tm // GLU_PARTS)]
    for cols in parts:
        z = _gelu_tanh(y_ref[:, cols].astype(F32))
        z_ref[:, cols] = z
        zb_ref[:, cols] = z.astype(BF16)
    for cols in parts:
        for f0 in range(0, o_ref.shape[1], MM_SUB):
            rows = slice(f0, f0 + MM_SUB)
            lin = _dot(wt_ref[rows, :], zb_ref[:, cols]) + b_ref[rows, :]
            out = z_ref[rows, cols] * _sigmoid(lin) * _silu(gate_ref[rows, cols].astype(F32))
            o_ref[cols, rows] = out.T.astype(o_ref.dtype)


def glu_gate(y_t, w, b, proj_t, *, gate_row, tm=512):
    s, e, n = y_t.shape
    gb = gate_row // e
    return pl.pallas_call(
        _glu_body,
        grid=(s, n // tm),
        in_specs=[pl.BlockSpec((None, e, tm), lambda t, i: (t, 0, i)),
                  pl.BlockSpec((e, e), lambda t, i: (0, 0), pipeline_mode=pl.Buffered(1)),
                  pl.BlockSpec((e, 1), lambda t, i: (0, 0), pipeline_mode=pl.Buffered(1)),
                  pl.BlockSpec((None, e, tm), lambda t, i: (t, gb, i))],
        out_specs=pl.BlockSpec((tm, e), lambda t, i: (i, t)),
        out_shape=jax.ShapeDtypeStruct((n, s * e), BF16),
        scratch_shapes=[pltpu.VMEM((e, tm), F32), pltpu.VMEM((e, tm), BF16), pltpu.VMEM((e, e), BF16)],
        compiler_params=_cparams(("arbitrary", "arbitrary")),
        name="glu",
    )(y_t, w, b.reshape(e, 1), proj_t)


def kernel(x, ab_norm_g, ab_w_in, gla_alpha_up, gla_alpha_b, gla_head_g, fox_f_b, ab_w_out, c_norm_g, c_w_in, s5_lambda_re, s5_lambda_im, s5_log_step, s5_b_re, s5_b_im, s5_c_re, s5_c_im, s5_d, glu_w, glu_b, c_w_out, final_norm_g):
    batch, seq, d = x.shape
    t = batch * seq
    x2 = x.reshape(t, d)

    hk, hv, fw = GLA_HEADS * GLA_DK, GLA_HEADS * GLA_DV, FOX_HEADS * FOX_DH
    sizes = (hk, hk, hv, GLA_LOWRANK, hv, fw, fw, fw, FOX_HEADS, fw)
    offs = [0]
    for s in sizes:
        offs.append(offs[-1] + s)
    w_main, w_small = ab_weight_prep(jnp.swapaxes(ab_w_in, 1, 2), offs)
    c_gq, c_gk, c_gv, c_gg = 0, hk, 2 * hk, 2 * hk + hv
    c_fq = c_gg + hv
    c_fk, c_fv, c_fg = c_fq + fw, c_fq + 2 * fw, c_fq + 3 * fw

    proj, small = norm_proj(x2, ab_norm_g[0], w_main, w_small, tn=w_main.shape[0] // 2, name="ab_in")

    o_gla = gla(proj, small, gla_alpha_up[0], gla_alpha_b[0], gla_head_g[0],
                batch=batch, seq=seq, q_col=c_gq, k_col=c_gk, v_col=c_gv, gate_col=c_gg)

    fb_row = jnp.zeros((1, 128), F32).at[0, GLA_LOWRANK:GLA_LOWRANK + FOX_HEADS].set(fox_f_b[0])
    qa, ka = fox_gate(small, fb_row, batch=batch, seq=seq, col0=GLA_LOWRANK)
    o_fox = fox(proj, qa, ka, batch=batch, seq=seq, q_col=c_fq, k_col=c_fk, v_col=c_fv, gate_col=c_fg)

    x1, h1g = outproj_mid(o_gla, o_fox, ab_w_out[0], x2, c_norm_g[0], steps=S5_T, name="ab_out")

    e = d
    groups = e // S5_GROUP_CH
    n_chunks = t // S5_T
    cols_per_seq = seq // S5_T
    n_levels = S5_SCAN_BLK.bit_length() - 1
    proj_t = matmul_t_grouped(c_w_in[0], h1g, S5_T, BF16, name="c_in")
    ops = s5_operators(s5_lambda_re[0], s5_lambda_im[0], s5_log_step[0], s5_b_re[0], s5_b_im[0],
                       s5_c_re[0], s5_c_im[0], s5_d[0], n_levels=n_levels, n_seq=batch)
    y_t = s5_apply(proj_t, *ops, groups=groups, cols_per_seq=cols_per_seq, n_levels=n_levels)
    zz = glu_gate(y_t, glu_w[0], glu_b[0], proj_t, gate_row=e)
    out = outproj_final(zz, c_w_out[0], x1, final_norm_g, steps=S5_T, name="c_out")
    return out.reshape(batch, seq, d)
```

```python
import functools
import math

import jax
import jax.numpy as jnp
from jax import lax
from jax.experimental import pallas as pl
from jax.experimental.pallas import tpu as pltpu

EPS = 1e-6
F32 = jnp.float32
BF16 = jnp.bfloat16

GLA_HEADS = 4
GLA_DK = 128
GLA_DV = 256
GLA_LOWRANK = 16
GLA_TAU = 16.0
GLA_CHUNK = 64
GLA_SUB = 8
GLA_UNROLL = 8
FOX_HEADS = 8
FOX_DH = 128
FOX_AUG = 128
FOX_AUG_GROUP = 8
S5_GROUP_CH = 16
S5_STATE = 64
S5_T = 16

VMEM_LIMIT = 56 * 1024 * 1024

NT_DIMS = (((1,), (1,)), ((), ()))
TN_DIMS = (((0,), (0,)), ((), ()))
LOG2E = math.log2(math.e)


def _cparams(sem):
    return pltpu.CompilerParams(dimension_semantics=sem, vmem_limit_bytes=VMEM_LIMIT)


def _dot(a, b):
    return jnp.dot(a, b, preferred_element_type=F32)


def _dot_nt(a, b):
    return lax.dot_general(a, b, NT_DIMS, preferred_element_type=F32)


def _dot_tn(a, b):
    return lax.dot_general(a, b, TN_DIMS, preferred_element_type=F32)


def _log_sigmoid(z):
    return jnp.minimum(z, 0.0) - jnp.log(1.0 + jnp.exp(-jnp.abs(z)))


def _sigmoid(z):
    return 0.5 * jnp.tanh(0.5 * z) + 0.5


def _silu(z):
    return z * _sigmoid(z)


def _split3(x):
    hi = x.astype(BF16)
    r1 = x - hi.astype(F32)
    mid = r1.astype(BF16)
    lo = (r1 - mid.astype(F32)).astype(BF16)
    return hi, mid, lo


def _norm_proj_body(x_ref, g_ref, wt_ref, ws_ref, o_ref, s_ref):
    x = x_ref[...]
    ms = jnp.mean(x * x, axis=-1, keepdims=True)
    h = (x * lax.rsqrt(ms + EPS) * g_ref[...]).astype(BF16)
    for c0 in range(0, o_ref.shape[1], MM_SUB):
        o_ref[:, c0:c0 + MM_SUB] = _dot_nt(h, wt_ref[c0:c0 + MM_SUB, :]).astype(o_ref.dtype)
    s_ref[...] = _dot_nt(h, ws_ref[...])


def norm_proj(x, g, wt, wt_small, tm=512, tn=512, name="norm_proj"):
    m, d = x.shape
    n = wt.shape[0]
    ns = wt_small.shape[0]
    proj, small = pl.pallas_call(
        _norm_proj_body,
        grid=(n // tn, m // tm),
        in_specs=[pl.BlockSpec((tm, d), lambda j, i: (i, 0)),
                  pl.BlockSpec((1, d), lambda j, i: (0, 0)),
                  pl.BlockSpec((tn, d), lambda j, i: (j, 0)),
                  pl.BlockSpec((ns, d), lambda j, i: (0, 0))],
        out_specs=[pl.BlockSpec((tm, tn), lambda j, i: (i, j)),
                   pl.BlockSpec((None, tm, ns), lambda j, i: (j, i, 0))],
        out_shape=[jax.ShapeDtypeStruct((m, n), BF16), jax.ShapeDtypeStruct((n // tn, m, ns), F32)],
        compiler_params=_cparams(("arbitrary", "arbitrary")),
        name=name,
    )(x, g.reshape(1, d), wt, wt_small)
    return proj, small[0]


def _ab_wprep_body(w_ref, o_ref, s_ref, *, offs):
    w = w_ref[0]
    seg = lambda n: w[offs[n]:offs[n + 1], :]
    o_ref[...] = jnp.concatenate(
        [w[:offs[3], :], seg(4), seg(5) * (FOX_DH ** -0.5 * LOG2E), w[offs[6]:offs[8], :], seg(9)],
        axis=0).astype(o_ref.dtype)
    pad = s_ref.shape[0] - (offs[4] - offs[3]) - (offs[9] - offs[8])
    s_ref[...] = jnp.concatenate([seg(3), seg(8), jnp.zeros((pad, w.shape[1]), F32)], axis=0).astype(s_ref.dtype)


def ab_weight_prep(w_t3, offs, tk=256):
    _, n_in, d = w_t3.shape
    n_main = n_in - (offs[4] - offs[3]) - (offs[9] - offs[8])
    return pl.pallas_call(
        functools.partial(_ab_wprep_body, offs=tuple(offs)),
        grid=(d // tk,),
        in_specs=[pl.BlockSpec((1, n_in, tk), lambda i: (0, 0, i))],
        out_specs=[pl.BlockSpec((n_main, tk), lambda i: (0, i)), pl.BlockSpec((128, tk), lambda i: (0, i))],
        out_shape=[jax.ShapeDtypeStruct((n_main, d), BF16), jax.ShapeDtypeStruct((128, d), BF16)],
        compiler_params=_cparams(("parallel",)),
        name="ab_wprep",
    )(w_t3)


MM_SUB = 512


def _transpose_into(wt_ref, w_ref):
    for c0 in range(0, w_ref.shape[1], MM_SUB):
        wt_ref[c0:c0 + MM_SUB, :] = w_ref[:, c0:c0 + MM_SUB].astype(wt_ref.dtype).T


def _mm_t_body(w_ref, a_ref, o_ref, wt_ref):
    @pl.when((pl.program_id(1) == 0) & (pl.program_id(2) == 0))
    def _():
        _transpose_into(wt_ref, w_ref)

    a = a_ref[...]
    for r0 in range(0, o_ref.shape[0], MM_SUB):
        o_ref[r0:r0 + MM_SUB, :] = _dot_nt(wt_ref[r0:r0 + MM_SUB, :], a).astype(o_ref.dtype)


def matmul_t_grouped(w, a2, steps, out_dtype, tm=512, tn=2048, name="matmul_tg"):
    k, nf = w.shape
    n = a2.shape[0]
    return pl.pallas_call(
        _mm_t_body,
        grid=(nf // tn, steps, n // tm),
        in_specs=[pl.BlockSpec((k, tn), lambda j, t, i: (0, j), pipeline_mode=pl.Buffered(1)),
                  pl.BlockSpec((tm, k), lambda j, t, i: (i, t))],
        out_specs=pl.BlockSpec((None, tn, tm), lambda j, t, i: (t, j, i)),
        out_shape=jax.ShapeDtypeStruct((steps, nf, n), out_dtype),
        scratch_shapes=[pltpu.VMEM((tn, k), BF16)],
        compiler_params=_cparams(("arbitrary", "arbitrary", "arbitrary")),
        name=name,
    )(w, a2)


def _gla_body(q_ref, k_ref, v_ref, glr_ref, aup_ref, ab_ref, gate_ref, hg_ref, o_ref,
              s_ref, b_ref, kf_ref, kts_ref, dec_ref, *, tb, hpb):
    ib = pl.program_id(2)
    C, SB, dk, dv = GLA_CHUNK, GLA_SUB, GLA_DK, GLA_DV

    @pl.when(ib == 0)
    def _():
        s_ref[...] = jnp.zeros_like(s_ref)

    glr = glr_ref[:, :GLA_LOWRANK].astype(BF16)
    tw = 2 * C
    r = lax.broadcasted_iota(jnp.int32, (tw, tw), 0)
    c = lax.broadcasted_iota(jnp.int32, (tw, tw), 1)
    tri = jnp.where((r // C == c // C) & (c <= r), 1.0, 0.0).astype(BF16)
    nc = tb // C
    heads = range(hpb)
    kcs = [slice(hh * dk, (hh + 1) * dk) for hh in heads]
    zs = [_dot(glr, aup_ref[:, kcs[hh]].astype(BF16)) + ab_ref[:, kcs[hh]] for hh in heads]
    b_blks = []
    for hh in heads:
        la = _log_sigmoid(zs[hh]) * (1.0 / GLA_TAU)
        hi = la.astype(BF16)
        lo = (la - hi.astype(F32)).astype(BF16)
        b_blks.append(jnp.concatenate(
            [_dot(tri, hi[r0:r0 + tw]) + _dot(tri, lo[r0:r0 + tw]) for r0 in range(0, tb, tw)], axis=0))
    for hh in heads:
        b_blk = b_blks[hh]
        k_blk = k_ref[:, kcs[hh]].astype(F32)
        b_ref[hh] = b_blk
        kf_ref[hh] = k_blk
        b3 = b_blk.reshape(nc, C, dk)
        b_end = b3[:, C - 1:C, :]
        k_end = (k_blk.reshape(nc, C, dk) * jnp.exp(b_end - b3)).reshape(tb, dk)
        kt_end = k_end.T.astype(BF16)
        dec_t = jnp.exp(b_end.reshape(nc, dk)).T
        for ci in range(nc):
            kts_ref[hh, ci] = kt_end[:, ci * C:(ci + 1) * C]
            dec_ref[hh, ci] = jnp.broadcast_to(dec_t[:, ci:ci + 1], (dk, 128))

    row = lax.broadcasted_iota(jnp.int32, (C, 1), 0)
    rowi = lax.broadcasted_iota(jnp.int32, (C, C), 0)
    coli = lax.broadcasted_iota(jnp.int32, (C, C), 1)
    lane_c = lax.broadcasted_iota(jnp.int32, (SB, C), 1)
    sub_r = lax.broadcasted_iota(jnp.int32, (SB, 1), 0)
    ones = jnp.ones((GLA_DK, C), BF16)
    neg = -jnp.inf
    scale = GLA_DK ** -0.5

    def stage_state(ci, r0, hh):
        kc = slice(hh * dk, (hh + 1) * dk)
        vc = slice(hh * dv, (hh + 1) * dv)
        q = q_ref[pl.ds(r0, C), kc].astype(F32) * scale
        k = kf_ref[hh, pl.ds(r0, C), :]
        b = b_ref[hh, pl.ds(r0, C), :]
        v = v_ref[pl.ds(r0, C), vc]

        s = s_ref[hh]
        o = _dot((q * jnp.exp(b)).astype(BF16), s.astype(BF16))
        dec = dec_ref[hh, ci]
        s_ref[hh] = s * jnp.concatenate([dec] * (dv // 128), axis=1) + _dot(kts_ref[hh, ci], v)

        attn = jnp.zeros((C, C), F32)
        h = C // 2
        while h >= SB:
            ref = jnp.broadcast_to(b.reshape(C // (2 * h), 2 * h, GLA_DK)[:, h - 1:h, :],
                                   (C // (2 * h), 2 * h, GLA_DK)).reshape(C, GLA_DK)
            upper = (row // h) % 2 == 1
            q_h = q * jnp.exp(jnp.where(upper, b - ref, neg))
            k_h = k * jnp.exp(jnp.where(upper, neg, ref - b))
            a_h = _dot_nt(q_h.astype(BF16), k_h.astype(BF16))
            attn = attn + (a_h if 2 * h == C else jnp.where(rowi // (2 * h) == coli // (2 * h), a_h, 0.0))
            h //= 2
        return q, b, v, o, attn

    def stage_diag(r0, hh, q, b):
        zs = []
        for bi in range(C // SB):
            s0 = bi * SB
            q_i = q[s0:s0 + SB, :]
            b_i = b[s0:s0 + SB, :]
            for j in range(SB):
                k_j = kf_ref[hh, pl.ds(r0 + (s0 + j), 1), :]
                b_j = b_ref[hh, pl.ds(r0 + (s0 + j), 1), :]
                zs.append(q_i * k_j * jnp.exp(jnp.where(sub_r >= j, b_i - b_j, neg)))
        return _dot(jnp.concatenate(zs, axis=0).astype(BF16), ones)

    def stage_out(r0, hh, v, o, attn, zsum):
        vc = slice(hh * dv, (hh + 1) * dv)
        diag = []
        for bi in range(C // SB):
            acc = jnp.zeros((SB, C), F32)
            for j in range(SB):
                n0 = (bi * SB + j) * SB
                acc = acc + jnp.where(lane_c == bi * SB + j, zsum[n0:n0 + SB, :], 0.0)
            diag.append(acc)
        attn = attn + jnp.concatenate(diag, axis=0)
        o = o + _dot(attn.astype(BF16), v)

        o = o * lax.rsqrt(jnp.mean(o * o, axis=-1, keepdims=True) + EPS) * hg_ref[:, vc]
        g = gate_ref[pl.ds(r0, C), vc].astype(F32)
        o_ref[pl.ds(r0, C), vc] = (o * _silu(g)).astype(o_ref.dtype)

    def chunk(ci, carry):
        r0 = pl.multiple_of(ci * C, C)
        heads = range(hpb)
        st = [stage_state(ci, r0, hh) for hh in heads]
        zsums = [stage_diag(r0, hh, st[hh][0], st[hh][1]) for hh in heads]
        for hh in heads:
            _, _, v, o, attn = st[hh]
            stage_out(r0, hh, v, o, attn, zsums[hh])
        return carry

    lax.fori_loop(0, tb // C, chunk, 0, unroll=GLA_UNROLL)


def gla(proj, small, alpha_up, alpha_b, head_g, *, batch, seq, q_col, k_col, v_col, gate_col, tb=512, hpb=4):
    t = batch * seq
    nb = seq // tb
    dk, dv, h = GLA_DK * hpb, GLA_DV * hpb, GLA_HEADS // hpb
    qb, kb, vb, gb = q_col // dk, k_col // dk, v_col // dv, gate_col // dv
    tok = lambda b, hh, i: b * nb + i
    nc = tb // GLA_CHUNK
    return pl.pallas_call(
        functools.partial(_gla_body, tb=tb, hpb=hpb),
        grid=(batch, h, nb),
        in_specs=[
            pl.BlockSpec((tb, dk), lambda b, hh, i: (tok(b, hh, i), qb + hh)),
            pl.BlockSpec((tb, dk), lambda b, hh, i: (tok(b, hh, i), kb + hh)),
            pl.BlockSpec((tb, dv), lambda b, hh, i: (tok(b, hh, i), vb + hh)),
            pl.BlockSpec((tb, 128), lambda b, hh, i: (tok(b, hh, i), 0)),
            pl.BlockSpec((GLA_LOWRANK, dk), lambda b, hh, i: (0, hh)),
            pl.BlockSpec((1, dk), lambda b, hh, i: (0, hh)),
            pl.BlockSpec((tb, dv), lambda b, hh, i: (tok(b, hh, i), gb + hh)),
            pl.BlockSpec((1, dv), lambda b, hh, i: (0, hh)),
        ],
        out_specs=pl.BlockSpec((tb, dv), lambda b, hh, i: (tok(b, hh, i), hh)),
        out_shape=jax.ShapeDtypeStruct((t, GLA_HEADS * GLA_DV), BF16),
        scratch_shapes=[pltpu.VMEM((hpb, GLA_DK, GLA_DV), F32), pltpu.VMEM((hpb, tb, GLA_DK), F32),
                        pltpu.VMEM((hpb, tb, GLA_DK), F32),
                        pltpu.VMEM((hpb, nc, GLA_DK, GLA_CHUNK), BF16),
                        pltpu.VMEM((hpb, nc, GLA_DK, 128), F32)],
        compiler_params=_cparams(("parallel", "parallel", "arbitrary")),
        name="gla",
    )(proj, proj, proj, small, alpha_up, alpha_b.reshape(1, -1), proj, head_g.reshape(1, -1))


FOX_GATE_UNROLL = 4


def _fox_gate_body(s_ref, fb_ref, qa_ref, ka_ref, cs_ref, off_ref, *, blk, col0):
    n = s_ref.shape[0] // blk
    r = lax.broadcasted_iota(jnp.int32, (blk, blk), 0)
    c = lax.broadcasted_iota(jnp.int32, (blk, blk), 1)
    tri = jnp.where(c <= r, 1.0, 0.0).astype(BF16)
    lanes = s_ref.shape[1]
    pr = lax.broadcasted_iota(jnp.int32, (3 * lanes, FOX_AUG), 0)
    pc = lax.broadcasted_iota(jnp.int32, (3 * lanes, FOX_AUG), 1)
    head, piece = pr % lanes - col0, pr // lanes
    is_head = (head >= 0) & (head < FOX_HEADS)
    sel_q = jnp.where(is_head & (pc == head * FOX_AUG_GROUP + piece), 1.0, 0.0).astype(BF16)
    sel_k = jnp.where(is_head & (pc == head * FOX_AUG_GROUP + 3 + piece), -1.0, 0.0).astype(BF16)
    lane = lax.broadcasted_iota(jnp.int32, (1, FOX_AUG), 1)
    used = lane < FOX_HEADS * FOX_AUG_GROUP
    one_q = jnp.where(used & (lane % FOX_AUG_GROUP >= 3) & (lane % FOX_AUG_GROUP < 6), 1.0, 0.0)
    one_k = jnp.where(used & (lane % FOX_AUG_GROUP < 3), 1.0, 0.0)

    def local(i, carry):
        r0 = pl.multiple_of(i * blk, blk)
        lf = _log_sigmoid(s_ref[pl.ds(r0, blk), :] + fb_ref[...])
        hi, mid, lo = _split3(lf)
        cs_ref[pl.ds(r0, blk), :] = _dot(tri, hi) + _dot(tri, mid) + _dot(tri, lo)
        return carry

    lax.fori_loop(0, n, local, 0, unroll=FOX_GATE_UNROLL)

    off = jnp.zeros((1, lanes), F32)
    for i in range(n):
        off_ref[i:i + 1, :] = off
        off = off + cs_ref[(i + 1) * blk - 1:(i + 1) * blk, :]

    def place(i, carry):
        r0 = pl.multiple_of(i * blk, blk)
        cs = cs_ref[pl.ds(r0, blk), :] + off_ref[pl.ds(i, 1), :]
        pieces = jnp.concatenate(_split3(cs * LOG2E), axis=1)
        qa_ref[pl.ds(r0, blk), :] = (_dot(pieces, sel_q) + one_q).astype(BF16)
        ka_ref[pl.ds(r0, blk), :] = (_dot(pieces, sel_k) + one_k).astype(BF16)
        return carry

    lax.fori_loop(0, n, place, 0, unroll=FOX_GATE_UNROLL)


def fox_gate(small, fb_row, *, batch, seq, col0, blk=256):
    shp = jax.ShapeDtypeStruct((batch * seq, FOX_AUG), BF16)
    spec = pl.BlockSpec((seq, FOX_AUG), lambda b: (b, 0))
    return pl.pallas_call(
        functools.partial(_fox_gate_body, blk=blk, col0=col0),
        grid=(batch,),
        in_specs=[pl.BlockSpec((seq, 128), lambda b: (b, 0)),
                  pl.BlockSpec((1, 128), lambda b: (0, 0))],
        out_specs=[spec, spec],
        out_shape=[shp, shp],
        scratch_shapes=[pltpu.VMEM((seq, 128), F32), pltpu.VMEM((seq // blk, 128), F32)],
        compiler_params=_cparams(("parallel",)),
        name="fox_gate",
    )(small, fb_row)


def _fox_body(it_ref, jt_ref, q_ref, qa_ref, k_ref, ka_ref, v_ref, gate_ref, o_ref,
              m_ref, l_ref, acc_ref, *, tq, tk, hpb):
    p = pl.program_id(2)
    i = it_ref[p]
    j = jt_ref[p]
    d = FOX_DH

    @pl.when(j == 0)
    def _():
        m_ref[...] = jnp.full_like(m_ref, -jnp.inf)
        l_ref[...] = jnp.zeros_like(l_ref)
        acc_ref[...] = jnp.zeros_like(acc_ref)

    aug_lane = lax.broadcasted_iota(jnp.int32, (1, FOX_AUG), 1)

    def scores(item, masked):
        hh, ks, qs = item
        mine = aug_lane // FOX_AUG_GROUP == pl.program_id(1) * hpb + hh
        zero = jnp.zeros((), BF16)
        q_aug = jnp.concatenate([q_ref[qs, hh * d:(hh + 1) * d], jnp.where(mine, qa_ref[qs, :], zero)], axis=1)
        k_aug = jnp.concatenate([k_ref[ks, hh * d:(hh + 1) * d], jnp.where(mine, ka_ref[ks, :], zero)], axis=1)
        st = _dot_nt(k_aug, q_aug)
        if masked:
            kr = lax.broadcasted_iota(jnp.int32, st.shape, 0) + ks.start
            qc = lax.broadcasted_iota(jnp.int32, st.shape, 1) + qs.start
            st = jnp.where(qc >= kr, st, -jnp.inf)
        return st

    def absorb(item, st):
        hh, ks, qs = item
        m_old = m_ref[hh, :, qs]
        m_new = jnp.maximum(m_old, jnp.max(st, axis=0, keepdims=True))
        alpha = jnp.exp2(m_old - m_new)
        pt = jnp.exp2(st - m_new)
        l_ref[hh, :, qs] = alpha * l_ref[hh, :, qs] + jnp.sum(pt, axis=0, keepdims=True)
        acc_ref[hh, :, qs] = (alpha * acc_ref[hh, :, qs]
                              + _dot_tn(v_ref[ks, hh * d:(hh + 1) * d], pt.astype(BF16)))
        m_ref[hh, :, qs] = m_new

    def block(masked, finish):
        full = slice(0, tq)
        halves = [(slice(0, tk // 2), full), (slice(tk // 2, tk), slice(tq // 2, tq))] if masked else [(slice(0, tk), full)]
        items = [(hh, ks, qs) for hh in range(hpb) for ks, qs in halves]
        st = scores(items[0], masked)
        for n, item in enumerate(items):
            st_next = scores(items[n + 1], masked) if n + 1 < len(items) else None
            absorb(item, st)
            hh = item[0]
            if finish and (n + 1 == len(items) or items[n + 1][0] != hh):
                o = (acc_ref[hh] / l_ref[hh]).T
                g = gate_ref[:, hh * d:(hh + 1) * d].astype(F32)
                o_ref[:, hh * d:(hh + 1) * d] = (o * _silu(g)).astype(o_ref.dtype)
            st = st_next

    @pl.when(j < i)
    def _():
        block(False, False)

    @pl.when(j == i)
    def _():
        block(True, True)


def fox(proj, qa, ka, *, batch, seq, q_col, k_col, v_col, gate_col, t=1024, hpb=8):
    tt = batch * seq
    nb = seq // t
    d, h = FOX_DH, FOX_HEADS
    w = hpb * d
    qb, kb, gb, vb = q_col // w, k_col // w, gate_col // w, v_col // w
    pairs = [(i, j) for i in range(nb) for j in range(i + 1)]
    it = jnp.array([p[0] for p in pairs], jnp.int32)
    jt = jnp.array([p[1] for p in pairs], jnp.int32)
    grid_spec = pltpu.PrefetchScalarGridSpec(
        num_scalar_prefetch=2,
        grid=(batch, h // hpb, len(pairs)),
        in_specs=[
            pl.BlockSpec((t, w), lambda b, hh, p, it, jt: (b * nb + it[p], qb + hh)),
            pl.BlockSpec((t, FOX_AUG), lambda b, hh, p, it, jt: (b * nb + it[p], 0)),
            pl.BlockSpec((t, w), lambda b, hh, p, it, jt: (b * nb + jt[p], kb + hh)),
            pl.BlockSpec((t, FOX_AUG), lambda b, hh, p, it, jt: (b * nb + jt[p], 0)),
            pl.BlockSpec((t, w), lambda b, hh, p, it, jt: (b * nb + jt[p], vb + hh)),
            pl.BlockSpec((t, w), lambda b, hh, p, it, jt: (b * nb + it[p], gb + hh)),
        ],
        out_specs=pl.BlockSpec((t, w), lambda b, hh, p, it, jt: (b * nb + it[p], hh)),
        scratch_shapes=[pltpu.VMEM((hpb, 1, t), F32), pltpu.VMEM((hpb, 1, t), F32), pltpu.VMEM((hpb, d, t), F32)],
    )
    return pl.pallas_call(
        functools.partial(_fox_body, tq=t, tk=t, hpb=hpb),
        grid_spec=grid_spec,
        out_shape=jax.ShapeDtypeStruct((tt, h * d), BF16),
        compiler_params=_cparams(("parallel", "parallel", "arbitrary")),
        name="fox",
    )(it, jt, proj, qa, proj, ka, proj, proj)


def _regroup_perm(tm, steps, to_grouped):
    r = lax.broadcasted_iota(jnp.int32, (tm, tm), 0)
    c = lax.broadcasted_iota(jnp.int32, (tm, tm), 1)
    nc = tm // steps
    src = (r % nc) * steps + r // nc if to_grouped else (r % steps) * nc + r // steps
    return jnp.where(c == src, 1.0, 0.0).astype(BF16)


def _cast_weight_once(wb_ref, w_ref):
    @pl.when(pl.program_id(0) == 0)
    def _():
        for r0 in range(0, w_ref.shape[0], MM_SUB):
            wb_ref[r0:r0 + MM_SUB, :] = w_ref[r0:r0 + MM_SUB, :].astype(wb_ref.dtype)


def _outproj_mid_body(a1_ref, a2_ref, wf_ref, x_ref, g_ref, r_ref, h_ref, w_ref, *, steps):
    k1 = a1_ref.shape[1]
    tm, d = x_ref.shape
    nc = tm // steps
    _cast_weight_once(w_ref, wf_ref)
    acc = x_ref[...] + _dot(a1_ref[...], w_ref[:k1, :]) + _dot(a2_ref[...], w_ref[k1:, :])
    r_ref[...] = acc
    normed = (acc * lax.rsqrt(jnp.mean(acc * acc, axis=-1, keepdims=True) + EPS) * g_ref[...]).astype(BF16)
    by_step = _dot(_regroup_perm(tm, steps, True), normed).astype(h_ref.dtype)
    for s in range(steps):
        h_ref[:, s * d:(s + 1) * d] = by_step[s * nc:(s + 1) * nc, :]


def outproj_mid(a1, a2, w, x, g, *, steps, tm=256, name="outproj_mid"):
    m, d = x.shape
    row = pl.BlockSpec((tm, d), lambda i: (i, 0))
    return pl.pallas_call(
        functools.partial(_outproj_mid_body, steps=steps),
        grid=(m // tm,),
        in_specs=[pl.BlockSpec((tm, a1.shape[1]), lambda i: (i, 0)),
                  pl.BlockSpec((tm, a2.shape[1]), lambda i: (i, 0)),
                  pl.BlockSpec(w.shape, lambda i: (0, 0), pipeline_mode=pl.Buffered(1)),
                  row,
                  pl.BlockSpec((1, d), lambda i: (0, 0))],
        out_specs=[row, pl.BlockSpec((tm // steps, steps * d), lambda i: (i, 0))],
        out_shape=[jax.ShapeDtypeStruct((m, d), F32), jax.ShapeDtypeStruct((m // steps, steps * d), BF16)],
        scratch_shapes=[pltpu.VMEM(w.shape, BF16)],
        compiler_params=_cparams(("arbitrary",)),
        name=name,
    )(a1, a2, w, x, g.reshape(1, d))


def _outproj_final_body(a_ref, wf_ref, x_ref, g_ref, o_ref, w_ref, *, steps):
    tm = x_ref.shape[0]
    e = w_ref.shape[0]
    by_step = jnp.concatenate([a_ref[:, s * e:(s + 1) * e] for s in range(steps)], axis=0)
    a_tok = _dot(_regroup_perm(tm, steps, False), by_step).astype(BF16)
    _cast_weight_once(w_ref, wf_ref)
    acc = x_ref[...] + _dot(a_tok, w_ref[...])
    o_ref[...] = acc * lax.rsqrt(jnp.mean(acc * acc, axis=-1, keepdims=True) + EPS) * g_ref[...]


def outproj_final(a_grouped, w, x, g, *, steps, tm=256, name="outproj_final"):
    m, d = x.shape
    e = w.shape[0]
    row = pl.BlockSpec((tm, d), lambda i: (i, 0))
    return pl.pallas_call(
        functools.partial(_outproj_final_body, steps=steps),
        grid=(m // tm,),
        in_specs=[pl.BlockSpec((tm // steps, steps * e), lambda i: (i, 0)),
                  pl.BlockSpec(w.shape, lambda i: (0, 0), pipeline_mode=pl.Buffered(1)),
                  row,
                  pl.BlockSpec((1, d), lambda i: (0, 0))],
        out_specs=row,
        out_shape=jax.ShapeDtypeStruct((m, d), F32),
        scratch_shapes=[pltpu.VMEM(w.shape, BF16)],
        compiler_params=_cparams(("arbitrary",)),
        name=name,
    )(a_grouped, w, x, g.reshape(1, d))


S5_SCAN_BLK = 8
S5_GPB = 8


def _s5_entering_state(x, arow, *, cols_per_seq, n_levels):
    n = x.shape[1]
    p = S5_STATE
    nseq = n // cols_per_seq

    def to_rows(part):
        return jnp.concatenate([part[:, s * cols_per_seq:(s + 1) * cols_per_seq] for s in range(nseq)], axis=0).T

    def to_cols(rows):
        rt = rows.T
        return jnp.concatenate([rt[s * p:(s + 1) * p, :] for s in range(nseq)], axis=1)

    xr, xi = to_rows(x[:p, :]), to_rows(x[p:, :])
    pos = lax.broadcasted_iota(jnp.int32, (cols_per_seq, 1), 0)
    blk = S5_SCAN_BLK
    half = arow.shape[0] // 2
    for lvl in range(n_levels):
        sh = 1 << lvl
        keep = pos % blk >= sh
        sr = jnp.where(keep, pltpu.roll(xr, sh, axis=0), 0.0)
        si = jnp.where(keep, pltpu.roll(xi, sh, axis=0), 0.0)
        ar, ai = arow[lvl:lvl + 1, :], arow[half + lvl:half + lvl + 1, :]
        xr, xi = xr + ar * sr - ai * si, xi + ar * si + ai * sr
    pw_r, pw_i = arow[n_levels:n_levels + blk, :], arow[half + n_levels:half + n_levels + blk, :]
    out_r, out_i = [xr[:blk, :]], [xi[:blk, :]]
    for r0 in range(blk, cols_per_seq, blk):
        cr, ci = out_r[-1][blk - 1:blk, :], out_i[-1][blk - 1:blk, :]
        out_r.append(xr[r0:r0 + blk, :] + pw_r * cr - pw_i * ci)
        out_i.append(xi[r0:r0 + blk, :] + pw_r * ci + pw_i * cr)
    xr, xi = jnp.concatenate(out_r, axis=0), jnp.concatenate(out_i, axis=0)
    keep = pos >= 1
    pr = jnp.where(keep, pltpu.roll(xr, 1, axis=0), 0.0)
    pi = jnp.where(keep, pltpu.roll(xi, 1, axis=0), 0.0)
    return jnp.concatenate([to_cols(pr), to_cols(pi)], axis=0).astype(BF16)


def _s5_body(u_ref, wt_ref, nt_ref, mt_ref, arow_ref, o_ref, *, cols_per_seq, n_levels):
    t, _, n = u_ref.shape
    cg = S5_GROUP_CH
    groups = range(S5_GPB)
    ch = [slice(gi * cg, (gi + 1) * cg) for gi in groups]
    us = [u_ref[:, ch[gi], :].reshape(t * cg, n) for gi in groups]
    xs = [_dot(nt_ref[gi], us[gi]) for gi in groups]
    ys = [_dot(wt_ref[gi], us[gi]) for gi in groups]
    prevs = [_s5_entering_state(xs[gi], arow_ref[gi], cols_per_seq=cols_per_seq, n_levels=n_levels)
             for gi in groups]
    for gi in groups:
        y = ys[gi] + _dot(mt_ref[gi], prevs[gi])
        o_ref[:, ch[gi], :] = y.astype(o_ref.dtype).reshape(t, cg, n)


def s5_apply(proj_t, wt_g, nt_g, mt_g, arow, *, groups, cols_per_seq, n_levels):
    t, _, n = proj_t.shape
    cg = S5_GROUP_CH * S5_GPB
    spec3 = lambda shp: pl.BlockSpec((S5_GPB,) + shp, lambda i: (i, 0, 0))
    return pl.pallas_call(
        functools.partial(_s5_body, cols_per_seq=cols_per_seq, n_levels=n_levels),
        grid=(groups // S5_GPB,),
        in_specs=[pl.BlockSpec((t, cg, n), lambda i: (0, i, 0)),
                  spec3(wt_g.shape[1:]), spec3(nt_g.shape[1:]), spec3(mt_g.shape[1:]), spec3(arow.shape[1:])],
        out_specs=pl.BlockSpec((t, cg, n), lambda i: (0, i, 0)),
        out_shape=jax.ShapeDtypeStruct((t, groups * S5_GROUP_CH, n), BF16),
        compiler_params=_cparams(("parallel",)),
        name="s5",
    )(proj_t, wt_g, nt_g, mt_g, arow)


def s5_operators(lam_re, lam_im, log_step, b_re, b_im, c_re, c_im, d, *, n_levels, n_seq):
    hp = lax.Precision.HIGH
    g, p = lam_re.shape
    cg, t = S5_GROUP_CH, S5_T
    lr = jnp.minimum(lam_re.astype(F32), -1e-4)
    li = lam_im.astype(F32)
    step = jnp.exp(log_step.astype(F32))[:, None]
    mag = jnp.exp(lr * step)
    lb_re, lb_im = mag * jnp.cos(li * step), mag * jnp.sin(li * step)
    den = lr * lr + li * li
    nr, ni = lb_re - 1.0, lb_im
    coef_re = (nr * lr + ni * li) / den
    coef_im = (ni * lr - nr * li) / den
    bb_re = coef_re[..., None] * b_re - coef_im[..., None] * b_im
    bb_im = coef_re[..., None] * b_im + coef_im[..., None] * b_re

    def power(m):
        mm = m.astype(F32)[None, :, None]
        mg = jnp.exp(lr[:, None, :] * step[:, None, :] * mm)
        ang = li[:, None, :] * step[:, None, :] * mm
        return mg * jnp.cos(ang), mg * jnp.sin(ang)

    pw_re, pw_im = power(jnp.arange(t + 1))
    pwt_re, pwt_im = pw_re.transpose(0, 2, 1), pw_im.transpose(0, 2, 1)

    def pow_times_b(pr, pi):
        re = pr[:, :, :, None] * bb_re[:, :, None, :] - pi[:, :, :, None] * bb_im[:, :, None, :]
        im = pr[:, :, :, None] * bb_im[:, :, None, :] + pi[:, :, :, None] * bb_re[:, :, None, :]
        return re.reshape(g, p, t * cg), im.reshape(g, p, t * cg)

    pb_re, pb_im = pow_times_b(pwt_re[:, :, :t], pwt_im[:, :, :t])
    taps = (jnp.einsum('gcp,gpy->gcy', c_re.astype(F32), pb_re, precision=hp)
            - jnp.einsum('gcp,gpy->gcy', c_im.astype(F32), pb_im, precision=hp))
    y_idx = jnp.arange(t * cg)
    taps = taps + d.astype(F32).reshape(g, cg, 1) * (y_idx[None, :] == jnp.arange(cg)[:, None]).astype(F32)[None]
    place = ((y_idx[None, :, None] // cg == jnp.arange(t)[:, None, None] - y_idx[None, None, :] // cg)
             & (y_idx[None, :, None] % cg == y_idx[None, None, :] % cg)).astype(BF16)
    wt_g = jnp.einsum('gcy,tyz->gtcz', taps.astype(BF16), place,
                      preferred_element_type=F32).astype(BF16).reshape(g, t * cg, t * cg)
    n_re, n_im = pow_times_b(pwt_re[:, :, t - 1::-1], pwt_im[:, :, t - 1::-1])
    nt_g = jnp.concatenate([n_re, n_im], axis=1).astype(BF16)
    cp_re = c_re[:, None] * pw_re[:, 1:, None, :] - c_im[:, None] * pw_im[:, 1:, None, :]
    cp_im = c_re[:, None] * pw_im[:, 1:, None, :] + c_im[:, None] * pw_re[:, 1:, None, :]
    mt_g = jnp.concatenate([cp_re, -cp_im], axis=-1).reshape(g, t * cg, 2 * p).astype(BF16)
    ar, ai = power(t * jnp.concatenate([2 ** jnp.arange(n_levels), jnp.arange(1, S5_SCAN_BLK + 1)]))
    arow = jnp.concatenate([jnp.tile(ar, (1, 1, n_seq)), jnp.tile(ai, (1, 1, n_seq))], axis=1)
    return wt_g, nt_g, mt_g, arow


def _gelu_tanh(y):
    return 0.5 * y * (1.0 + jnp.tanh(math.sqrt(2.0 / math.pi) * (y + 0.044715 * (y * y * y))))


GLU_PARTS = 2


def _glu_body(y_ref, w_ref, b_ref, gate_ref, o_ref, z_ref, zb_ref, wt_ref):
    @pl.when((pl.program_id(0) == 0) & (pl.program_id(1) == 0))
    def _():
        _transpose_into(wt_ref, w_ref)

    tm = y_ref.shape[1]
    parts = [slice(c0, c0 + tm // GLU_PARTS) for c0 in range(0, tm, tm // GLU_PARTS)]
    for cols in parts:
        z = _gelu_tanh(y_ref[:, cols].astype(F32))
        z_ref[:, cols] = z
        zb_ref[:, cols] = z.astype(BF16)
    for cols in parts:
        for f0 in range(0, o_ref.shape[1], MM_SUB):
            rows = slice(f0, f0 + MM_SUB)
            lin = _dot(wt_ref[rows, :], zb_ref[:, cols]) + b_ref[rows, :]
            out = z_ref[rows, cols] * _sigmoid(lin) * _silu(gate_ref[rows, cols].astype(F32))
            o_ref[cols, rows] = out.T.astype(o_ref.dtype)


def glu_gate(y_t, w, b, proj_t, *, gate_row, tm=512):
    s, e, n = y_t.shape
    gb = gate_row // e
    return pl.pallas_call(
        _glu_body,
        grid=(s, n // tm),
        in_specs=[pl.BlockSpec((None, e, tm), lambda t, i: (t, 0, i)),
                  pl.BlockSpec((e, e), lambda t, i: (0, 0), pipeline_mode=pl.Buffered(1)),
                  pl.BlockSpec((e, 1), lambda t, i: (0, 0), pipeline_mode=pl.Buffered(1)),
                  pl.BlockSpec((None, e, tm), lambda t, i: (t, gb, i))],
        out_specs=pl.BlockSpec((tm, e), lambda t, i: (i, t)),
        out_shape=jax.ShapeDtypeStruct((n, s * e), BF16),
        scratch_shapes=[pltpu.VMEM((e, tm), F32), pltpu.VMEM((e, tm), BF16), pltpu.VMEM((e, e), BF16)],
        compiler_params=_cparams(("arbitrary", "arbitrary")),
        name="glu",
    )(y_t, w, b.reshape(e, 1), proj_t)


def kernel(x, ab_norm_g, ab_w_in, gla_alpha_up, gla_alpha_b, gla_head_g, fox_f_b, ab_w_out, c_norm_g, c_w_in, s5_lambda_re, s5_lambda_im, s5_log_step, s5_b_re, s5_b_im, s5_c_re, s5_c_im, s5_d, glu_w, glu_b, c_w_out, final_norm_g):
    batch, seq, d = x.shape
    t = batch * seq
    x2 = x.reshape(t, d)

    hk, hv, fw = GLA_HEADS * GLA_DK, GLA_HEADS * GLA_DV, FOX_HEADS * FOX_DH
    sizes = (hk, hk, hv, GLA_LOWRANK, hv, fw, fw, fw, FOX_HEADS, fw)
    offs = [0]
    for s in sizes:
        offs.append(offs[-1] + s)
    w_main, w_small = ab_weight_prep(jnp.swapaxes(ab_w_in, 1, 2), offs)
    c_gq, c_gk, c_gv, c_gg = 0, hk, 2 * hk, 2 * hk + hv
    c_fq = c_gg + hv
    c_fk, c_fv, c_fg = c_fq + fw, c_fq + 2 * fw, c_fq + 3 * fw

    proj, small = norm_proj(x2, ab_norm_g[0], w_main, w_small, tn=w_main.shape[0] // 2, name="ab_in")

    o_gla = gla(proj, small, gla_alpha_up[0], gla_alpha_b[0], gla_head_g[0],
                batch=batch, seq=seq, q_col=c_gq, k_col=c_gk, v_col=c_gv, gate_col=c_gg)

    fb_row = jnp.zeros((1, 128), F32).at[0, GLA_LOWRANK:GLA_LOWRANK + FOX_HEADS].set(fox_f_b[0])
    qa, ka = fox_gate(small, fb_row, batch=batch, seq=seq, col0=GLA_LOWRANK)
    o_fox = fox(proj, qa, ka, batch=batch, seq=seq, q_col=c_fq, k_col=c_fk, v_col=c_fv, gate_col=c_fg)

    x1, h1g = outproj_mid(o_gla, o_fox, ab_w_out[0], x2, c_norm_g[0], steps=S5_T, name="ab_out")

    e = d
    groups = e // S5_GROUP_CH
    n_chunks = t // S5_T
    cols_per_seq = seq // S5_T
    n_levels = S5_SCAN_BLK.bit_length() - 1
    proj_t = matmul_t_grouped(c_w_in[0], h1g, S5_T, BF16, name="c_in")
    ops = s5_operators(s5_lambda_re[0], s5_lambda_im[0], s5_log_step[0], s5_b_re[0], s5_b_im[0],
                       s5_c_re[0], s5_c_im[0], s5_d[0], n_levels=n_levels, n_seq=batch)
    y_t = s5_apply(proj_t, *ops, groups=groups, cols_per_seq=cols_per_seq, n_levels=n_levels)
    zz = glu_gate(y_t, glu_w[0], glu_b[0], proj_t, gate_row=e)
    out = outproj_final(zz, c_w_out[0], x1, final_norm_g, steps=S5_T, name="c_out")
    return out.reshape(batch, seq, d)
```

```python
import functools
import math

import jax
import jax.numpy as jnp
from jax import lax
from jax.experimental import pallas as pl
from jax.experimental.pallas import tpu as pltpu

EPS = 1e-6
F32 = jnp.float32
BF16 = jnp.bfloat16

GLA_HEADS = 4
GLA_DK = 128
GLA_DV = 256
GLA_LOWRANK = 16
GLA_TAU = 16.0
GLA_CHUNK = 64
GLA_SUB = 8
GLA_UNROLL = 8
FOX_HEADS = 8
FOX_DH = 128
FOX_AUG = 128
FOX_AUG_GROUP = 8
S5_GROUP_CH = 16
S5_STATE = 64
S5_T = 16

VMEM_LIMIT = 56 * 1024 * 1024

NT_DIMS = (((1,), (1,)), ((), ()))
TN_DIMS = (((0,), (0,)), ((), ()))
LOG2E = math.log2(math.e)


def _cparams(sem):
    return pltpu.CompilerParams(dimension_semantics=sem, vmem_limit_bytes=VMEM_LIMIT)


def _dot(a, b):
    return jnp.dot(a, b, preferred_element_type=F32)


def _dot_nt(a, b):
    return lax.dot_general(a, b, NT_DIMS, preferred_element_type=F32)


def _dot_tn(a, b):
    return lax.dot_general(a, b, TN_DIMS, preferred_element_type=F32)


def _log_sigmoid(z):
    return jnp.minimum(z, 0.0) - jnp.log(1.0 + jnp.exp(-jnp.abs(z)))


def _sigmoid(z):
    return 0.5 * jnp.tanh(0.5 * z) + 0.5


def _silu(z):
    return z * _sigmoid(z)


def _split3(x):
    hi = x.astype(BF16)
    r1 = x - hi.astype(F32)
    mid = r1.astype(BF16)
    lo = (r1 - mid.astype(F32)).astype(BF16)
    return hi, mid, lo


def _norm_proj_body(x_ref, g_ref, wt_ref, ws_ref, o_ref, s_ref):
    x = x_ref[...]
    ms = jnp.mean(x * x, axis=-1, keepdims=True)
    h = (x * lax.rsqrt(ms + EPS) * g_ref[...]).astype(BF16)
    for c0 in range(0, o_ref.shape[1], MM_SUB):
        o_ref[:, c0:c0 + MM_SUB] = _dot_nt(h, wt_ref[c0:c0 + MM_SUB, :]).astype(o_ref.dtype)
    s_ref[...] = _dot_nt(h, ws_ref[...])


def norm_proj(x, g, wt, wt_small, tm=512, tn=512, name="norm_proj"):
    m, d = x.shape
    n = wt.shape[0]
    ns = wt_small.shape[0]
    proj, small = pl.pallas_call(
        _norm_proj_body,
        grid=(n // tn, m // tm),
        in_specs=[pl.BlockSpec((tm, d), lambda j, i: (i, 0)),
                  pl.BlockSpec((1, d), lambda j, i: (0, 0)),
                  pl.BlockSpec((tn, d), lambda j, i: (j, 0)),
                  pl.BlockSpec((ns, d), lambda j, i: (0, 0))],
        out_specs=[pl.BlockSpec((tm, tn), lambda j, i: (i, j)),
                   pl.BlockSpec((None, tm, ns), lambda j, i: (j, i, 0))],
        out_shape=[jax.ShapeDtypeStruct((m, n), BF16), jax.ShapeDtypeStruct((n // tn, m, ns), F32)],
        compiler_params=_cparams(("arbitrary", "arbitrary")),
        name=name,
    )(x, g.reshape(1, d), wt, wt_small)
    return proj, small[0]


def _ab_wprep_body(w_ref, o_ref, s_ref, *, offs):
    w = w_ref[0]
    seg = lambda n: w[offs[n]:offs[n + 1], :]
    o_ref[...] = jnp.concatenate(
        [w[:offs[3], :], seg(4), seg(5) * (FOX_DH ** -0.5 * LOG2E), w[offs[6]:offs[8], :], seg(9)],
        axis=0).astype(o_ref.dtype)
    pad = s_ref.shape[0] - (offs[4] - offs[3]) - (offs[9] - offs[8])
    s_ref[...] = jnp.concatenate([seg(3), seg(8), jnp.zeros((pad, w.shape[1]), F32)], axis=0).astype(s_ref.dtype)


def ab_weight_prep(w_t3, offs, tk=256):
    _, n_in, d = w_t3.shape
    n_main = n_in - (offs[4] - offs[3]) - (offs[9] - offs[8])
    return pl.pallas_call(
        functools.partial(_ab_wprep_body, offs=tuple(offs)),
        grid=(d // tk,),
        in_specs=[pl.BlockSpec((1, n_in, tk), lambda i: (0, 0, i))],
        out_specs=[pl.BlockSpec((n_main, tk), lambda i: (0, i)), pl.BlockSpec((128, tk), lambda i: (0, i))],
        out_shape=[jax.ShapeDtypeStruct((n_main, d), BF16), jax.ShapeDtypeStruct((128, d), BF16)],
        compiler_params=_cparams(("parallel",)),
        name="ab_wprep",
    )(w_t3)


MM_SUB = 512


def _transpose_into(wt_ref, w_ref):
    for c0 in range(0, w_ref.shape[1], MM_SUB):
        wt_ref[c0:c0 + MM_SUB, :] = w_ref[:, c0:c0 + MM_SUB].astype(wt_ref.dtype).T


def _mm_t_body(w_ref, a_ref, o_ref, wt_ref):
    @pl.when((pl.program_id(1) == 0) & (pl.program_id(2) == 0))
    def _():
        _transpose_into(wt_ref, w_ref)

    a = a_ref[...]
    for r0 in range(0, o_ref.shape[0], MM_SUB):
        o_ref[r0:r0 + MM_SUB, :] = _dot_nt(wt_ref[r0:r0 + MM_SUB, :], a).astype(o_ref.dtype)


def matmul_t_grouped(w, a2, steps, out_dtype, tm=512, tn=2048, name="matmul_tg"):
    k, nf = w.shape
    n = a2.shape[0]
    return pl.pallas_call(
        _mm_t_body,
        grid=(nf // tn, steps, n // tm),
        in_specs=[pl.BlockSpec((k, tn), lambda j, t, i: (0, j), pipeline_mode=pl.Buffered(1)),
                  pl.BlockSpec((tm, k), lambda j, t, i: (i, t))],
        out_specs=pl.BlockSpec((None, tn, tm), lambda j, t, i: (t, j, i)),
        out_shape=jax.ShapeDtypeStruct((steps, nf, n), out_dtype),
        scratch_shapes=[pltpu.VMEM((tn, k), BF16)],
        compiler_params=_cparams(("arbitrary", "arbitrary", "arbitrary")),
        name=name,
    )(w, a2)


def _gla_body(q_ref, k_ref, v_ref, glr_ref, aup_ref, ab_ref, gate_ref, hg_ref, o_ref,
              s_ref, b_ref, kf_ref, kts_ref, dec_ref, *, tb, hpb):
    ib = pl.program_id(2)
    C, SB, dk, dv = GLA_CHUNK, GLA_SUB, GLA_DK, GLA_DV

    @pl.when(ib == 0)
    def _():
        s_ref[...] = jnp.zeros_like(s_ref)

    glr = glr_ref[:, :GLA_LOWRANK].astype(BF16)
    tw = 2 * C
    r = lax.broadcasted_iota(jnp.int32, (tw, tw), 0)
    c = lax.broadcasted_iota(jnp.int32, (tw, tw), 1)
    tri = jnp.where((r // C == c // C) & (c <= r), 1.0, 0.0).astype(BF16)
    nc = tb // C
    heads = range(hpb)
    kcs = [slice(hh * dk, (hh + 1) * dk) for hh in heads]
    zs = [_dot(glr, aup_ref[:, kcs[hh]].astype(BF16)) + ab_ref[:, kcs[hh]] for hh in heads]
    b_blks = []
    for hh in heads:
        la = _log_sigmoid(zs[hh]) * (1.0 / GLA_TAU)
        hi = la.astype(BF16)
        lo = (la - hi.astype(F32)).astype(BF16)
        b_blks.append(jnp.concatenate(
            [_dot(tri, hi[r0:r0 + tw]) + _dot(tri, lo[r0:r0 + tw]) for r0 in range(0, tb, tw)], axis=0))
    for hh in heads:
        b_blk = b_blks[hh]
        k_blk = k_ref[:, kcs[hh]].astype(F32)
        b_ref[hh] = b_blk
        kf_ref[hh] = k_blk
        b3 = b_blk.reshape(nc, C, dk)
        b_end = b3[:, C - 1:C, :]
        k_end = (k_blk.reshape(nc, C, dk) * jnp.exp(b_end - b3)).reshape(tb, dk)
        kt_end = k_end.T.astype(BF16)
        dec_t = jnp.exp(b_end.reshape(nc, dk)).T
        for ci in range(nc):
            kts_ref[hh, ci] = kt_end[:, ci * C:(ci + 1) * C]
            dec_ref[hh, ci] = jnp.broadcast_to(dec_t[:, ci:ci + 1], (dk, 128))

    row = lax.broadcasted_iota(jnp.int32, (C, 1), 0)
    rowi = lax.broadcasted_iota(jnp.int32, (C, C), 0)
    coli = lax.broadcasted_iota(jnp.int32, (C, C), 1)
    lane_c = lax.broadcasted_iota(jnp.int32, (SB, C), 1)
    sub_r = lax.broadcasted_iota(jnp.int32, (SB, 1), 0)
    ones = jnp.ones((GLA_DK, C), BF16)
    neg = -jnp.inf
    scale = GLA_DK ** -0.5

    def stage_state(ci, r0, hh):
        kc = slice(hh * dk, (hh + 1) * dk)
        vc = slice(hh * dv, (hh + 1) * dv)
        q = q_ref[pl.ds(r0, C), kc].astype(F32) * scale
        k = kf_ref[hh, pl.ds(r0, C), :]
        b = b_ref[hh, pl.ds(r0, C), :]
        v = v_ref[pl.ds(r0, C), vc]

        s = s_ref[hh]
        o = _dot((q * jnp.exp(b)).astype(BF16), s.astype(BF16))
        dec = dec_ref[hh, ci]
        s_ref[hh] = s * jnp.concatenate([dec] * (dv // 128), axis=1) + _dot(kts_ref[hh, ci], v)

        attn = jnp.zeros((C, C), F32)
        h = C // 2
        while h >= SB:
            ref = jnp.broadcast_to(b.reshape(C // (2 * h), 2 * h, GLA_DK)[:, h - 1:h, :],
                                   (C // (2 * h), 2 * h, GLA_DK)).reshape(C, GLA_DK)
            upper = (row // h) % 2 == 1
            q_h = q * jnp.exp(jnp.where(upper, b - ref, neg))
            k_h = k * jnp.exp(jnp.where(upper, neg, ref - b))
            a_h = _dot_nt(q_h.astype(BF16), k_h.astype(BF16))
            attn = attn + (a_h if 2 * h == C else jnp.where(rowi // (2 * h) == coli // (2 * h), a_h, 0.0))
            h //= 2
        return q, b, v, o, attn

    def stage_diag(r0, hh, q, b):
        zs = []
        for bi in range(C // SB):
            s0 = bi * SB
            q_i = q[s0:s0 + SB, :]
            b_i = b[s0:s0 + SB, :]
            for j in range(SB):
                k_j = kf_ref[hh, pl.ds(r0 + (s0 + j), 1), :]
                b_j = b_ref[hh, pl.ds(r0 + (s0 + j), 1), :]
                zs.append(q_i * k_j * jnp.exp(jnp.where(sub_r >= j, b_i - b_j, neg)))
        return _dot(jnp.concatenate(zs, axis=0).astype(BF16), ones)

    def stage_out(r0, hh, v, o, attn, zsum):
        vc = slice(hh * dv, (hh + 1) * dv)
        diag = []
        for bi in range(C // SB):
            acc = jnp.zeros((SB, C), F32)
            for j in range(SB):
                n0 = (bi * SB + j) * SB
                acc = acc + jnp.where(lane_c == bi * SB + j, zsum[n0:n0 + SB, :], 0.0)
            diag.append(acc)
        attn = attn + jnp.concatenate(diag, axis=0)
        o = o + _dot(attn.astype(BF16), v)

        o = o * lax.rsqrt(jnp.mean(o * o, axis=-1, keepdims=True) + EPS) * hg_ref[:, vc]
        g = gate_ref[pl.ds(r0, C), vc].astype(F32)
        o_ref[pl.ds(r0, C), vc] = (o * _silu(g)).astype(o_ref.dtype)

    def chunk(ci, carry):
        r0 = pl.multiple_of(ci * C, C)
        heads = range(hpb)
        st = [stage_state(ci, r0, hh) for hh in heads]
        zsums = [stage_diag(r0, hh, st[hh][0], st[hh][1]) for hh in heads]
        for hh in heads:
            _, _, v, o, attn = st[hh]
            stage_out(r0, hh, v, o, attn, zsums[hh])
        return carry

    lax.fori_loop(0, tb // C, chunk, 0, unroll=GLA_UNROLL)


def gla(proj, small, alpha_up, alpha_b, head_g, *, batch, seq, q_col, k_col, v_col, gate_col, tb=512, hpb=4):
    t = batch * seq
    nb = seq // tb
    dk, dv, h = GLA_DK * hpb, GLA_DV * hpb, GLA_HEADS // hpb
    qb, kb, vb, gb = q_col // dk, k_col // dk, v_col // dv, gate_col // dv
    tok = lambda b, hh, i: b * nb + i
    nc = tb // GLA_CHUNK
    return pl.pallas_call(
        functools.partial(_gla_body, tb=tb, hpb=hpb),
        grid=(batch, h, nb),
        in_specs=[
            pl.BlockSpec((tb, dk), lambda b, hh, i: (tok(b, hh, i), qb + hh)),
            pl.BlockSpec((tb, dk), lambda b, hh, i: (tok(b, hh, i), kb + hh)),
            pl.BlockSpec((tb, dv), lambda b, hh, i: (tok(b, hh, i), vb + hh)),
            pl.BlockSpec((tb, 128), lambda b, hh, i: (tok(b, hh, i), 0)),
            pl.BlockSpec((GLA_LOWRANK, dk), lambda b, hh, i: (0, hh)),
            pl.BlockSpec((1, dk), lambda b, hh, i: (0, hh)),
            pl.BlockSpec((tb, dv), lambda b, hh, i: (tok(b, hh, i), gb + hh)),
            pl.BlockSpec((1, dv), lambda b, hh, i: (0, hh)),
        ],
        out_specs=pl.BlockSpec((tb, dv), lambda b, hh, i: (tok(b, hh, i), hh)),
        out_shape=jax.ShapeDtypeStruct((t, GLA_HEADS * GLA_DV), BF16),
        scratch_shapes=[pltpu.VMEM((hpb, GLA_DK, GLA_DV), F32), pltpu.VMEM((hpb, tb, GLA_DK), F32),
                        pltpu.VMEM((hpb, tb, GLA_DK), F32),
                        pltpu.VMEM((hpb, nc, GLA_DK, GLA_CHUNK), BF16),
                        pltpu.VMEM((hpb, nc, GLA_DK, 128), F32)],
        compiler_params=_cparams(("parallel", "parallel", "arbitrary")),
        name="gla",
    )(proj, proj, proj, small, alpha_up, alpha_b.reshape(1, -1), proj, head_g.reshape(1, -1))


FOX_GATE_UNROLL = 4


def _fox_gate_body(s_ref, fb_ref, qa_ref, ka_ref, cs_ref, off_ref, *, blk, col0):
    n = s_ref.shape[0] // blk
    r = lax.broadcasted_iota(jnp.int32, (blk, blk), 0)
    c = lax.broadcasted_iota(jnp.int32, (blk, blk), 1)
    tri = jnp.where(c <= r, 1.0, 0.0).astype(BF16)
    lanes = s_ref.shape[1]
    pr = lax.broadcasted_iota(jnp.int32, (3 * lanes, FOX_AUG), 0)
    pc = lax.broadcasted_iota(jnp.int32, (3 * lanes, FOX_AUG), 1)
    head, piece = pr % lanes - col0, pr // lanes
    is_head = (head >= 0) & (head < FOX_HEADS)
    sel_q = jnp.where(is_head & (pc == head * FOX_AUG_GROUP + piece), 1.0, 0.0).astype(BF16)
    sel_k = jnp.where(is_head & (pc == head * FOX_AUG_GROUP + 3 + piece), -1.0, 0.0).astype(BF16)
    lane = lax.broadcasted_iota(jnp.int32, (1, FOX_AUG), 1)
    used = lane < FOX_HEADS * FOX_AUG_GROUP
    one_q = jnp.where(used & (lane % FOX_AUG_GROUP >= 3) & (lane % FOX_AUG_GROUP < 6), 1.0, 0.0)
    one_k = jnp.where(used & (lane % FOX_AUG_GROUP < 3), 1.0, 0.0)

    def local(i, carry):
        r0 = pl.multiple_of(i * blk, blk)
        lf = _log_sigmoid(s_ref[pl.ds(r0, blk), :] + fb_ref[...])
        hi, mid, lo = _split3(lf)
        cs_ref[pl.ds(r0, blk), :] = _dot(tri, hi) + _dot(tri, mid) + _dot(tri, lo)
        return carry

    lax.fori_loop(0, n, local, 0, unroll=FOX_GATE_UNROLL)

    off = jnp.zeros((1, lanes), F32)
    for i in range(n):
        off_ref[i:i + 1, :] = off
        off = off + cs_ref[(i + 1) * blk - 1:(i + 1) * blk, :]

    def place(i, carry):
        r0 = pl.multiple_of(i * blk, blk)
        cs = cs_ref[pl.ds(r0, blk), :] + off_ref[pl.ds(i, 1), :]
        pieces = jnp.concatenate(_split3(cs * LOG2E), axis=1)
        qa_ref[pl.ds(r0, blk), :] = (_dot(pieces, sel_q) + one_q).astype(BF16)
        ka_ref[pl.ds(r0, blk), :] = (_dot(pieces, sel_k) + one_k).astype(BF16)
        return carry

    lax.fori_loop(0, n, place, 0, unroll=FOX_GATE_UNROLL)


def fox_gate(small, fb_row, *, batch, seq, col0, blk=256):
    shp = jax.ShapeDtypeStruct((batch * seq, FOX_AUG), BF16)
    spec = pl.BlockSpec((seq, FOX_AUG), lambda b: (b, 0))
    return pl.pallas_call(
        functools.partial(_fox_gate_body, blk=blk, col0=col0),
        grid=(batch,),
        in_specs=[pl.BlockSpec((seq, 128), lambda b: (b, 0)),
                  pl.BlockSpec((1, 128), lambda b: (0, 0))],
        out_specs=[spec, spec],
        out_shape=[shp, shp],
        scratch_shapes=[pltpu.VMEM((seq, 128), F32), pltpu.VMEM((seq // blk, 128), F32)],
        compiler_params=_cparams(("parallel",)),
        name="fox_gate",
    )(small, fb_row)


def _fox_body(it_ref, jt_ref, q_ref, qa_ref, k_ref, ka_ref, v_ref, gate_ref, o_ref,
              m_ref, l_ref, acc_ref, *, tq, tk, hpb):
    p = pl.program_id(2)
    i = it_ref[p]
    j = jt_ref[p]
    d = FOX_DH

    @pl.when(j == 0)
    def _():
        m_ref[...] = jnp.full_like(m_ref, -jnp.inf)
        l_ref[...] = jnp.zeros_like(l_ref)
        acc_ref[...] = jnp.zeros_like(acc_ref)

    aug_lane = lax.broadcasted_iota(jnp.int32, (1, FOX_AUG), 1)

    def scores(item, masked):
        hh, ks, qs = item
        mine = aug_lane // FOX_AUG_GROUP == pl.program_id(1) * hpb + hh
        zero = jnp.zeros((), BF16)
        q_aug = jnp.concatenate([q_ref[qs, hh * d:(hh + 1) * d], jnp.where(mine, qa_ref[qs, :], zero)], axis=1)
        k_aug = jnp.concatenate([k_ref[ks, hh * d:(hh + 1) * d], jnp.where(mine, ka_ref[ks, :], zero)], axis=1)
        st = _dot_nt(k_aug, q_aug)
        if masked:
            kr = lax.broadcasted_iota(jnp.int32, st.shape, 0) + ks.start
            qc = lax.broadcasted_iota(jnp.int32, st.shape, 1) + qs.start
            st = jnp.where(qc >= kr, st, -jnp.inf)
        return st

    def absorb(item, st):
        hh, ks, qs = item
        m_old = m_ref[hh, :, qs]
        m_new = jnp.maximum(m_old, jnp.max(st, axis=0, keepdims=True))
        alpha = jnp.exp2(m_old - m_new)
        pt = jnp.exp2(st - m_new)
        l_ref[hh, :, qs] = alpha * l_ref[hh, :, qs] + jnp.sum(pt, axis=0, keepdims=True)
        acc_ref[hh, :, qs] = (alpha * acc_ref[hh, :, qs]
                              + _dot_tn(v_ref[ks, hh * d:(hh + 1) * d], pt.astype(BF16)))
        m_ref[hh, :, qs] = m_new

    def block(masked, finish):
        full = slice(0, tq)
        halves = [(slice(0, tk // 2), full), (slice(tk // 2, tk), slice(tq // 2, tq))] if masked else [(slice(0, tk), full)]
        items = [(hh, ks, qs) for hh in range(hpb) for ks, qs in halves]
        st = scores(items[0], masked)
        for n, item in enumerate(items):
            st_next = scores(items[n + 1], masked) if n + 1 < len(items) else None
            absorb(item, st)
            hh = item[0]
            if finish and (n + 1 == len(items) or items[n + 1][0] != hh):
                o = (acc_ref[hh] / l_ref[hh]).T
                g = gate_ref[:, hh * d:(hh + 1) * d].astype(F32)
                o_ref[:, hh * d:(hh + 1) * d] = (o * _silu(g)).astype(o_ref.dtype)
            st = st_next

    @pl.when(j < i)
    def _():
        block(False, False)

    @pl.when(j == i)
    def _():
        block(True, True)


def fox(proj, qa, ka, *, batch, seq, q_col, k_col, v_col, gate_col, t=1024, hpb=8):
    tt = batch * seq
    nb = seq // t
    d, h = FOX_DH, FOX_HEADS
    w = hpb * d
    qb, kb, gb, vb = q_col // w, k_col // w, gate_col // w, v_col // w
    pairs = [(i, j) for i in range(nb) for j in range(i + 1)]
    it = jnp.array([p[0] for p in pairs], jnp.int32)
    jt = jnp.array([p[1] for p in pairs], jnp.int32)
    grid_spec = pltpu.PrefetchScalarGridSpec(
        num_scalar_prefetch=2,
        grid=(batch, h // hpb, len(pairs)),
        in_specs=[
            pl.BlockSpec((t, w), lambda b, hh, p, it, jt: (b * nb + it[p], qb + hh)),
            pl.BlockSpec((t, FOX_AUG), lambda b, hh, p, it, jt: (b * nb + it[p], 0)),
            pl.BlockSpec((t, w), lambda b, hh, p, it, jt: (b * nb + jt[p], kb + hh)),
            pl.BlockSpec((t, FOX_AUG), lambda b, hh, p, it, jt: (b * nb + jt[p], 0)),
            pl.BlockSpec((t, w), lambda b, hh, p, it, jt: (b * nb + jt[p], vb + hh)),
            pl.BlockSpec((t, w), lambda b, hh, p, it, jt: (b * nb + it[p], gb + hh)),
        ],
        out_specs=pl.BlockSpec((t, w), lambda b, hh, p, it, jt: (b * nb + it[p], hh)),
        scratch_shapes=[pltpu.VMEM((hpb, 1, t), F32), pltpu.VMEM((hpb, 1, t), F32), pltpu.VMEM((hpb, d, t), F32)],
    )
    return pl.pallas_call(
        functools.partial(_fox_body, tq=t, tk=t, hpb=hpb),
        grid_spec=grid_spec,
        out_shape=jax.ShapeDtypeStruct((tt, h * d), BF16),
        compiler_params=_cparams(("parallel", "parallel", "arbitrary")),
        name="fox",
    )(it, jt, proj, qa, proj, ka, proj, proj)


def _regroup_perm(tm, steps, to_grouped):
    r = lax.broadcasted_iota(jnp.int32, (tm, tm), 0)
    c = lax.broadcasted_iota(jnp.int32, (tm, tm), 1)
    nc = tm // steps
    src = (r % nc) * steps + r // nc if to_grouped else (r % steps) * nc + r // steps
    return jnp.where(c == src, 1.0, 0.0).astype(BF16)


OUTPROJ_PART_ROWS = 256


def _cast_weight_once(wb_ref, w_ref):
    @pl.when(pl.program_id(0) == 0)
    def _():
        for r0 in range(0, w_ref.shape[0], MM_SUB):
            wb_ref[r0:r0 + MM_SUB, :] = w_ref[r0:r0 + MM_SUB, :].astype(wb_ref.dtype)


def _outproj_mid_body(a1_ref, a2_ref, wf_ref, x_ref, g_ref, r_ref, h_ref, w_ref, *, steps):
    k1 = a1_ref.shape[1]
    tm, d = x_ref.shape
    tp = OUTPROJ_PART_ROWS
    nc = tp // steps
    _cast_weight_once(w_ref, wf_ref)
    parts = [slice(p * tp, (p + 1) * tp) for p in range(tm // tp)]
    accs = [x_ref[rows, :] + _dot(a1_ref[rows, :], w_ref[:k1, :]) + _dot(a2_ref[rows, :], w_ref[k1:, :])
            for rows in parts]
    perm = _regroup_perm(tp, steps, True)
    for p, rows in enumerate(parts):
        acc = accs[p]
        r_ref[rows, :] = acc
        normed = (acc * lax.rsqrt(jnp.mean(acc * acc, axis=-1, keepdims=True) + EPS) * g_ref[...]).astype(BF16)
        by_step = _dot(perm, normed).astype(h_ref.dtype)
        for s in range(steps):
            h_ref[p * nc:(p + 1) * nc, s * d:(s + 1) * d] = by_step[s * nc:(s + 1) * nc, :]


def outproj_mid(a1, a2, w, x, g, *, steps, tm=256, name="outproj_mid"):
    m, d = x.shape
    row = pl.BlockSpec((tm, d), lambda i: (i, 0))
    return pl.pallas_call(
        functools.partial(_outproj_mid_body, steps=steps),
        grid=(m // tm,),
        in_specs=[pl.BlockSpec((tm, a1.shape[1]), lambda i: (i, 0)),
                  pl.BlockSpec((tm, a2.shape[1]), lambda i: (i, 0)),
                  pl.BlockSpec(w.shape, lambda i: (0, 0), pipeline_mode=pl.Buffered(1)),
                  row,
                  pl.BlockSpec((1, d), lambda i: (0, 0))],
        out_specs=[row, pl.BlockSpec((tm // steps, steps * d), lambda i: (i, 0))],
        out_shape=[jax.ShapeDtypeStruct((m, d), F32), jax.ShapeDtypeStruct((m // steps, steps * d), BF16)],
        scratch_shapes=[pltpu.VMEM(w.shape, BF16)],
        compiler_params=_cparams(("arbitrary",)),
        name=name,
    )(a1, a2, w, x, g.reshape(1, d))


def _outproj_final_body(a_ref, wf_ref, x_ref, g_ref, o_ref, w_ref, *, steps):
    tm = x_ref.shape[0]
    e = w_ref.shape[0]
    tp = OUTPROJ_PART_ROWS
    nc = tp // steps
    _cast_weight_once(w_ref, wf_ref)
    perm = _regroup_perm(tp, steps, False)
    a_toks = []
    for p in range(tm // tp):
        by_step = jnp.concatenate([a_ref[p * nc:(p + 1) * nc, s * e:(s + 1) * e] for s in range(steps)],
                                  axis=0)
        a_toks.append(_dot(perm, by_step).astype(BF16))
    parts = [slice(p * tp, (p + 1) * tp) for p in range(tm // tp)]
    accs = [x_ref[rows, :] + _dot(a_toks[p], w_ref[...]) for p, rows in enumerate(parts)]
    for p, rows in enumerate(parts):
        acc = accs[p]
        o_ref[rows, :] = acc * lax.rsqrt(jnp.mean(acc * acc, axis=-1, keepdims=True) + EPS) * g_ref[...]


def outproj_final(a_grouped, w, x, g, *, steps, tm=512, name="outproj_final"):
    m, d = x.shape
    e = w.shape[0]
    row = pl.BlockSpec((tm, d), lambda i: (i, 0))
    return pl.pallas_call(
        functools.partial(_outproj_final_body, steps=steps),
        grid=(m // tm,),
        in_specs=[pl.BlockSpec((tm // steps, steps * e), lambda i: (i, 0)),
                  pl.BlockSpec(w.shape, lambda i: (0, 0), pipeline_mode=pl.Buffered(1)),
                  row,
                  pl.BlockSpec((1, d), lambda i: (0, 0))],
        out_specs=row,
        out_shape=jax.ShapeDtypeStruct((m, d), F32),
        scratch_shapes=[pltpu.VMEM(w.shape, BF16)],
        compiler_params=_cparams(("arbitrary",)),
        name=name,
    )(a_grouped, w, x, g.reshape(1, d))


S5_SCAN_BLK = 8
S5_GPB = 8


def _s5_entering_state(x, arow, *, cols_per_seq, n_levels):
    n = x.shape[1]
    p = S5_STATE
    nseq = n // cols_per_seq

    def to_rows(part):
        return jnp.concatenate([part[:, s * cols_per_seq:(s + 1) * cols_per_seq] for s in range(nseq)], axis=0).T

    def to_cols(rows):
        rt = rows.T
        return jnp.concatenate([rt[s * p:(s + 1) * p, :] for s in range(nseq)], axis=1)

    xr, xi = to_rows(x[:p, :]), to_rows(x[p:, :])
    pos = lax.broadcasted_iota(jnp.int32, (cols_per_seq, 1), 0)
    blk = S5_SCAN_BLK
    half = arow.shape[0] // 2
    for lvl in range(n_levels):
        sh = 1 << lvl
        keep = pos % blk >= sh
        sr = jnp.where(keep, pltpu.roll(xr, sh, axis=0), 0.0)
        si = jnp.where(keep, pltpu.roll(xi, sh, axis=0), 0.0)
        ar, ai = arow[lvl:lvl + 1, :], arow[half + lvl:half + lvl + 1, :]
        xr, xi = xr + ar * sr - ai * si, xi + ar * si + ai * sr
    pw_r, pw_i = arow[n_levels:n_levels + blk, :], arow[half + n_levels:half + n_levels + blk, :]
    out_r, out_i = [xr[:blk, :]], [xi[:blk, :]]
    for r0 in range(blk, cols_per_seq, blk):
        cr, ci = out_r[-1][blk - 1:blk, :], out_i[-1][blk - 1:blk, :]
        out_r.append(xr[r0:r0 + blk, :] + pw_r * cr - pw_i * ci)
        out_i.append(xi[r0:r0 + blk, :] + pw_r * ci + pw_i * cr)
    xr, xi = jnp.concatenate(out_r, axis=0), jnp.concatenate(out_i, axis=0)
    keep = pos >= 1
    pr = jnp.where(keep, pltpu.roll(xr, 1, axis=0), 0.0)
    pi = jnp.where(keep, pltpu.roll(xi, 1, axis=0), 0.0)
    return jnp.concatenate([to_cols(pr), to_cols(pi)], axis=0).astype(BF16)


def _s5_body(u_ref, wt_ref, nt_ref, mt_ref, arow_ref, o_ref, *, cols_per_seq, n_levels):
    t, _, n = u_ref.shape
    cg = S5_GROUP_CH
    groups = range(S5_GPB)
    ch = [slice(gi * cg, (gi + 1) * cg) for gi in groups]
    us = [u_ref[:, ch[gi], :].reshape(t * cg, n) for gi in groups]
    xs = [_dot(nt_ref[gi], us[gi]) for gi in groups]
    ys = [_dot(wt_ref[gi], us[gi]) for gi in groups]
    prevs = [_s5_entering_state(xs[gi], arow_ref[gi], cols_per_seq=cols_per_seq, n_levels=n_levels)
             for gi in groups]
    for gi in groups:
        y = ys[gi] + _dot(mt_ref[gi], prevs[gi])
        o_ref[:, ch[gi], :] = y.astype(o_ref.dtype).reshape(t, cg, n)


def s5_apply(proj_t, wt_g, nt_g, mt_g, arow, *, groups, cols_per_seq, n_levels):
    t, _, n = proj_t.shape
    cg = S5_GROUP_CH * S5_GPB
    spec3 = lambda shp: pl.BlockSpec((S5_GPB,) + shp, lambda i: (i, 0, 0))
    return pl.pallas_call(
        functools.partial(_s5_body, cols_per_seq=cols_per_seq, n_levels=n_levels),
        grid=(groups // S5_GPB,),
        in_specs=[pl.BlockSpec((t, cg, n), lambda i: (0, i, 0)),
                  spec3(wt_g.shape[1:]), spec3(nt_g.shape[1:]), spec3(mt_g.shape[1:]), spec3(arow.shape[1:])],
        out_specs=pl.BlockSpec((t, cg, n), lambda i: (0, i, 0)),
        out_shape=jax.ShapeDtypeStruct((t, groups * S5_GROUP_CH, n), BF16),
        compiler_params=_cparams(("parallel",)),
        name="s5",
    )(proj_t, wt_g, nt_g, mt_g, arow)


def s5_operators(lam_re, lam_im, log_step, b_re, b_im, c_re, c_im, d, *, n_levels, n_seq):
    hp = lax.Precision.HIGH
    g, p = lam_re.shape
    cg, t = S5_GROUP_CH, S5_T
    lr = jnp.minimum(lam_re.astype(F32), -1e-4)
    li = lam_im.astype(F32)
    step = jnp.exp(log_step.astype(F32))[:, None]
    mag = jnp.exp(lr * step)
    lb_re, lb_im = mag * jnp.cos(li * step), mag * jnp.sin(li * step)
    den = lr * lr + li * li
    nr, ni = lb_re - 1.0, lb_im
    coef_re = (nr * lr + ni * li) / den
    coef_im = (ni * lr - nr * li) / den
    bb_re = coef_re[..., None] * b_re - coef_im[..., None] * b_im
    bb_im = coef_re[..., None] * b_im + coef_im[..., None] * b_re

    def power(m):
        mm = m.astype(F32)[None, :, None]
        mg = jnp.exp(lr[:, None, :] * step[:, None, :] * mm)
        ang = li[:, None, :] * step[:, None, :] * mm
        return mg * jnp.cos(ang), mg * jnp.sin(ang)

    pw_re, pw_im = power(jnp.arange(t + 1))
    pwt_re, pwt_im = pw_re.transpose(0, 2, 1), pw_im.transpose(0, 2, 1)

    def pow_times_b(pr, pi):
        re = pr[:, :, :, None] * bb_re[:, :, None, :] - pi[:, :, :, None] * bb_im[:, :, None, :]
        im = pr[:, :, :, None] * bb_im[:, :, None, :] + pi[:, :, :, None] * bb_re[:, :, None, :]
        return re.reshape(g, p, t * cg), im.reshape(g, p, t * cg)

    pb_re, pb_im = pow_times_b(pwt_re[:, :, :t], pwt_im[:, :, :t])
    taps = (jnp.einsum('gcp,gpy->gcy', c_re.astype(F32), pb_re, precision=hp)
            - jnp.einsum('gcp,gpy->gcy', c_im.astype(F32), pb_im, precision=hp))
    y_idx = jnp.arange(t * cg)
    taps = taps + d.astype(F32).reshape(g, cg, 1) * (y_idx[None, :] == jnp.arange(cg)[:, None]).astype(F32)[None]
    place = ((y_idx[None, :, None] // cg == jnp.arange(t)[:, None, None] - y_idx[None, None, :] // cg)
             & (y_idx[None, :, None] % cg == y_idx[None, None, :] % cg)).astype(BF16)
    wt_g = jnp.einsum('gcy,tyz->gtcz', taps.astype(BF16), place,
                      preferred_element_type=F32).astype(BF16).reshape(g, t * cg, t * cg)
    n_re, n_im = pow_times_b(pwt_re[:, :, t - 1::-1], pwt_im[:, :, t - 1::-1])
    nt_g = jnp.concatenate([n_re, n_im], axis=1).astype(BF16)
    cp_re = c_re[:, None] * pw_re[:, 1:, None, :] - c_im[:, None] * pw_im[:, 1:, None, :]
    cp_im = c_re[:, None] * pw_im[:, 1:, None, :] + c_im[:, None] * pw_re[:, 1:, None, :]
    mt_g = jnp.concatenate([cp_re, -cp_im], axis=-1).reshape(g, t * cg, 2 * p).astype(BF16)
    ar, ai = power(t * jnp.concatenate([2 ** jnp.arange(n_levels), jnp.arange(1, S5_SCAN_BLK + 1)]))
    arow = jnp.concatenate([jnp.tile(ar, (1, 1, n_seq)), jnp.tile(ai, (1, 1, n_seq))], axis=1)
    return wt_g, nt_g, mt_g, arow


def _gelu_tanh(y):
    return 0.5 * y * (1.0 + jnp.tanh(math.sqrt(2.0 / math.pi) * (y + 0.044715 * (y * y * y))))


GLU_PARTS = 2


def _glu_body(y_ref, w_ref, b_ref, gate_ref, o_ref, z_ref, zb_ref, wt_ref):
    @pl.when((pl.program_id(0) == 0) & (pl.program_id(1) == 0))
    def _():
        _transpose_into(wt_ref, w_ref)

    tm = y_ref.shape[1]
    parts = [slice(c0, c0 + tm // GLU_PARTS) for c0 in range(0, tm, tm // GLU_PARTS)]
    for cols in parts:
        z = _gelu_tanh(y_ref[:, cols].astype(F32))
        z_ref[:, cols] = z
        zb_ref[:, cols] = z.astype(BF16)
    for cols in parts:
        for f0 in range(0, o_ref.shape[1], MM_SUB):
            rows = slice(f0, f0 + MM_SUB)
            lin = _dot(wt_ref[rows, :], zb_ref[:, cols]) + b_ref[rows, :]
            out = z_ref[rows, cols] * _sigmoid(lin) * _silu(gate_ref[rows, cols].astype(F32))
            o_ref[cols, rows] = out.T.astype(o_ref.dtype)


def glu_gate(y_t, w, b, proj_t, *, gate_row, tm=512):
    s, e, n = y_t.shape
    gb = gate_row // e
    return pl.pallas_call(
        _glu_body,
        grid=(s, n // tm),
        in_specs=[pl.BlockSpec((None, e, tm), lambda t, i: (t, 0, i)),
                  pl.BlockSpec((e, e), lambda t, i: (0, 0), pipeline_mode=pl.Buffered(1)),
                  pl.BlockSpec((e, 1), lambda t, i: (0, 0), pipeline_mode=pl.Buffered(1)),
                  pl.BlockSpec((None, e, tm), lambda t, i: (t, gb, i))],
        out_specs=pl.BlockSpec((tm, e), lambda t, i: (i, t)),
        out_shape=jax.ShapeDtypeStruct((n, s * e), BF16),
        scratch_shapes=[pltpu.VMEM((e, tm), F32), pltpu.VMEM((e, tm), BF16), pltpu.VMEM((e, e), BF16)],
        compiler_params=_cparams(("arbitrary", "arbitrary")),
        name="glu",
    )(y_t, w, b.reshape(e, 1), proj_t)


def kernel(x, ab_norm_g, ab_w_in, gla_alpha_up, gla_alpha_b, gla_head_g, fox_f_b, ab_w_out, c_norm_g, c_w_in, s5_lambda_re, s5_lambda_im, s5_log_step, s5_b_re, s5_b_im, s5_c_re, s5_c_im, s5_d, glu_w, glu_b, c_w_out, final_norm_g):
    batch, seq, d = x.shape
    t = batch * seq
    x2 = x.reshape(t, d)

    hk, hv, fw = GLA_HEADS * GLA_DK, GLA_HEADS * GLA_DV, FOX_HEADS * FOX_DH
    sizes = (hk, hk, hv, GLA_LOWRANK, hv, fw, fw, fw, FOX_HEADS, fw)
    offs = [0]
    for s in sizes:
        offs.append(offs[-1] + s)
    w_main, w_small = ab_weight_prep(jnp.swapaxes(ab_w_in, 1, 2), offs)
    c_gq, c_gk, c_gv, c_gg = 0, hk, 2 * hk, 2 * hk + hv
    c_fq = c_gg + hv
    c_fk, c_fv, c_fg = c_fq + fw, c_fq + 2 * fw, c_fq + 3 * fw

    proj, small = norm_proj(x2, ab_norm_g[0], w_main, w_small, tn=w_main.shape[0] // 2, name="ab_in")

    o_gla = gla(proj, small, gla_alpha_up[0], gla_alpha_b[0], gla_head_g[0],
                batch=batch, seq=seq, q_col=c_gq, k_col=c_gk, v_col=c_gv, gate_col=c_gg)

    fb_row = jnp.zeros((1, 128), F32).at[0, GLA_LOWRANK:GLA_LOWRANK + FOX_HEADS].set(fox_f_b[0])
    qa, ka = fox_gate(small, fb_row, batch=batch, seq=seq, col0=GLA_LOWRANK)
    o_fox = fox(proj, qa, ka, batch=batch, seq=seq, q_col=c_fq, k_col=c_fk, v_col=c_fv, gate_col=c_fg)

    x1, h1g = outproj_mid(o_gla, o_fox, ab_w_out[0], x2, c_norm_g[0], steps=S5_T, name="ab_out")

    e = d
    groups = e // S5_GROUP_CH
    n_chunks = t // S5_T
    cols_per_seq = seq // S5_T
    n_levels = S5_SCAN_BLK.bit_length() - 1
    proj_t = matmul_t_grouped(c_w_in[0], h1g, S5_T, BF16, name="c_in")
    ops = s5_operators(s5_lambda_re[0], s5_lambda_im[0], s5_log_step[0], s5_b_re[0], s5_b_im[0],
                       s5_c_re[0], s5_c_im[0], s5_d[0], n_levels=n_levels, n_seq=batch)
    y_t = s5_apply(proj_t, *ops, groups=groups, cols_per_seq=cols_per_seq, n_levels=n_levels)
    zz = glu_gate(y_t, glu_w[0], glu_b[0], proj_t, gate_row=e)
    out = outproj_final(zz, c_w_out[0], x1, final_norm_g, steps=S5_T, name="c_out")
    return out.reshape(batch, seq, d)
```

```python
import functools
import math

import jax
import jax.numpy as jnp
from jax import lax
from jax.experimental import pallas as pl
from jax.experimental.pallas import tpu as pltpu

EPS = 1e-6
F32 = jnp.float32
BF16 = jnp.bfloat16

GLA_HEADS = 4
GLA_DK = 128
GLA_DV = 256
GLA_LOWRANK = 16
GLA_TAU = 16.0
GLA_CHUNK = 64
GLA_SUB = 8
GLA_UNROLL = 8
FOX_HEADS = 8
FOX_DH = 128
FOX_AUG = 128
FOX_AUG_GROUP = 8
S5_GROUP_CH = 16
S5_STATE = 64
S5_T = 16

VMEM_LIMIT = 58 * 1024 * 1024

NT_DIMS = (((1,), (1,)), ((), ()))
TN_DIMS = (((0,), (0,)), ((), ()))
LOG2E = math.log2(math.e)


def _cparams(sem):
    return pltpu.CompilerParams(dimension_semantics=sem, vmem_limit_bytes=VMEM_LIMIT)


def _dot(a, b):
    return jnp.dot(a, b, preferred_element_type=F32)


def _dot_nt(a, b):
    return lax.dot_general(a, b, NT_DIMS, preferred_element_type=F32)


def _dot_tn(a, b):
    return lax.dot_general(a, b, TN_DIMS, preferred_element_type=F32)


def _log_sigmoid(z):
    return jnp.minimum(z, 0.0) - jnp.log(1.0 + jnp.exp(-jnp.abs(z)))


def _sigmoid(z):
    return 0.5 * jnp.tanh(0.5 * z) + 0.5


def _silu(z):
    return z * _sigmoid(z)


def _split3(x):
    hi = x.astype(BF16)
    r1 = x - hi.astype(F32)
    mid = r1.astype(BF16)
    lo = (r1 - mid.astype(F32)).astype(BF16)
    return hi, mid, lo


def _norm_proj_body(x_ref, g_ref, wt_ref, ws_ref, o_ref, s_ref):
    x = x_ref[...]
    ms = jnp.mean(x * x, axis=-1, keepdims=True)
    h = (x * lax.rsqrt(ms + EPS) * g_ref[...]).astype(BF16)
    for c0 in range(0, o_ref.shape[1], MM_SUB):
        o_ref[:, c0:c0 + MM_SUB] = _dot_nt(h, wt_ref[c0:c0 + MM_SUB, :]).astype(o_ref.dtype)
    s_ref[...] = _dot_nt(h, ws_ref[...])


def norm_proj(x, g, wt, wt_small, tm=512, tn=512, name="norm_proj"):
    m, d = x.shape
    n = wt.shape[0]
    ns = wt_small.shape[0]
    proj, small = pl.pallas_call(
        _norm_proj_body,
        grid=(n // tn, m // tm),
        in_specs=[pl.BlockSpec((tm, d), lambda j, i: (i, 0)),
                  pl.BlockSpec((1, d), lambda j, i: (0, 0)),
                  pl.BlockSpec((tn, d), lambda j, i: (j, 0)),
                  pl.BlockSpec((ns, d), lambda j, i: (0, 0))],
        out_specs=[pl.BlockSpec((tm, tn), lambda j, i: (i, j)),
                   pl.BlockSpec((None, tm, ns), lambda j, i: (j, i, 0))],
        out_shape=[jax.ShapeDtypeStruct((m, n), BF16), jax.ShapeDtypeStruct((n // tn, m, ns), F32)],
        compiler_params=_cparams(("arbitrary", "arbitrary")),
        name=name,
    )(x, g.reshape(1, d), wt, wt_small)
    return proj, small[0]


def _ab_wprep_body(w_ref, o_ref, s_ref, *, offs):
    w = w_ref[0]
    seg = lambda n: w[offs[n]:offs[n + 1], :]
    o_ref[...] = jnp.concatenate(
        [w[:offs[3], :], seg(4), seg(5) * (FOX_DH ** -0.5 * LOG2E), w[offs[6]:offs[8], :], seg(9)],
        axis=0).astype(o_ref.dtype)
    pad = s_ref.shape[0] - (offs[4] - offs[3]) - (offs[9] - offs[8])
    s_ref[...] = jnp.concatenate([seg(3), seg(8), jnp.zeros((pad, w.shape[1]), F32)], axis=0).astype(s_ref.dtype)


def ab_weight_prep(w_t3, offs, tk=256):
    _, n_in, d = w_t3.shape
    n_main = n_in - (offs[4] - offs[3]) - (offs[9] - offs[8])
    return pl.pallas_call(
        functools.partial(_ab_wprep_body, offs=tuple(offs)),
        grid=(d // tk,),
        in_specs=[pl.BlockSpec((1, n_in, tk), lambda i: (0, 0, i))],
        out_specs=[pl.BlockSpec((n_main, tk), lambda i: (0, i)), pl.BlockSpec((128, tk), lambda i: (0, i))],
        out_shape=[jax.ShapeDtypeStruct((n_main, d), BF16), jax.ShapeDtypeStruct((128, d), BF16)],
        compiler_params=_cparams(("parallel",)),
        name="ab_wprep",
    )(w_t3)


MM_SUB = 512


def _transpose_into(wt_ref, w_ref):
    for c0 in range(0, w_ref.shape[1], MM_SUB):
        wt_ref[c0:c0 + MM_SUB, :] = w_ref[:, c0:c0 + MM_SUB].astype(wt_ref.dtype).T


def _mm_t_body(w_ref, a_ref, o_ref, wt_ref):
    @pl.when((pl.program_id(1) == 0) & (pl.program_id(2) == 0))
    def _():
        _transpose_into(wt_ref, w_ref)

    a = a_ref[...]
    for r0 in range(0, o_ref.shape[0], MM_SUB):
        o_ref[r0:r0 + MM_SUB, :] = _dot_nt(wt_ref[r0:r0 + MM_SUB, :], a).astype(o_ref.dtype)


def matmul_t_grouped(w, a2, steps, out_dtype, tm=512, tn=2048, name="matmul_tg"):
    k, nf = w.shape
    n = a2.shape[0]
    return pl.pallas_call(
        _mm_t_body,
        grid=(nf // tn, steps, n // tm),
        in_specs=[pl.BlockSpec((k, tn), lambda j, t, i: (0, j), pipeline_mode=pl.Buffered(1)),
                  pl.BlockSpec((tm, k), lambda j, t, i: (i, t))],
        out_specs=pl.BlockSpec((None, tn, tm), lambda j, t, i: (t, j, i)),
        out_shape=jax.ShapeDtypeStruct((steps, nf, n), out_dtype),
        scratch_shapes=[pltpu.VMEM((tn, k), BF16)],
        compiler_params=_cparams(("arbitrary", "arbitrary", "arbitrary")),
        name=name,
    )(w, a2)


def _gla_body(q_ref, k_ref, v_ref, glr_ref, aup_ref, ab_ref, gate_ref, hg_ref, o_ref,
              s_ref, b_ref, kf_ref, kts_ref, dec_ref, *, tb, hpb):
    ib = pl.program_id(2)
    C, SB, dk, dv = GLA_CHUNK, GLA_SUB, GLA_DK, GLA_DV

    @pl.when(ib == 0)
    def _():
        s_ref[...] = jnp.zeros_like(s_ref)

    glr = glr_ref[:, :GLA_LOWRANK].astype(BF16)
    tw = 2 * C
    r = lax.broadcasted_iota(jnp.int32, (tw, tw), 0)
    c = lax.broadcasted_iota(jnp.int32, (tw, tw), 1)
    tri = jnp.where((r // C == c // C) & (c <= r), 1.0, 0.0).astype(BF16)
    nc = tb // C
    heads = range(hpb)
    kcs = [slice(hh * dk, (hh + 1) * dk) for hh in heads]
    zs = [_dot(glr, aup_ref[:, kcs[hh]].astype(BF16)) + ab_ref[:, kcs[hh]] for hh in heads]
    b_blks = []
    for hh in heads:
        la = _log_sigmoid(zs[hh]) * (1.0 / GLA_TAU)
        hi = la.astype(BF16)
        lo = (la - hi.astype(F32)).astype(BF16)
        b_blks.append(jnp.concatenate(
            [_dot(tri, hi[r0:r0 + tw]) + _dot(tri, lo[r0:r0 + tw]) for r0 in range(0, tb, tw)], axis=0))
    for hh in heads:
        b_blk = b_blks[hh]
        k_blk = k_ref[:, kcs[hh]].astype(F32)
        b_ref[hh] = b_blk
        kf_ref[hh] = k_blk
        b3 = b_blk.reshape(nc, C, dk)
        b_end = b3[:, C - 1:C, :]
        k_end = (k_blk.reshape(nc, C, dk) * jnp.exp(b_end - b3)).reshape(tb, dk)
        kt_end = k_end.T.astype(BF16)
        dec_t = jnp.exp(b_end.reshape(nc, dk)).T
        for ci in range(nc):
            kts_ref[hh, ci] = kt_end[:, ci * C:(ci + 1) * C]
            dec_ref[hh, ci] = jnp.broadcast_to(dec_t[:, ci:ci + 1], (dk, 128))

    row = lax.broadcasted_iota(jnp.int32, (C, 1), 0)
    rowi = lax.broadcasted_iota(jnp.int32, (C, C), 0)
    coli = lax.broadcasted_iota(jnp.int32, (C, C), 1)
    lane_c = lax.broadcasted_iota(jnp.int32, (SB, C), 1)
    sub_r = lax.broadcasted_iota(jnp.int32, (SB, 1), 0)
    ones = jnp.ones((GLA_DK, C), BF16)
    neg = -jnp.inf
    scale = GLA_DK ** -0.5

    def stage_state(ci, r0, hh):
        kc = slice(hh * dk, (hh + 1) * dk)
        vc = slice(hh * dv, (hh + 1) * dv)
        q = q_ref[pl.ds(r0, C), kc].astype(F32) * scale
        k = kf_ref[hh, pl.ds(r0, C), :]
        b = b_ref[hh, pl.ds(r0, C), :]
        v = v_ref[pl.ds(r0, C), vc]

        s = s_ref[hh]
        o = _dot((q * jnp.exp(b)).astype(BF16), s.astype(BF16))
        dec = dec_ref[hh, ci]
        s_ref[hh] = s * jnp.concatenate([dec] * (dv // 128), axis=1) + _dot(kts_ref[hh, ci], v)

        attn = jnp.zeros((C, C), F32)
        h = C // 2
        while h >= SB:
            ref = jnp.broadcast_to(b.reshape(C // (2 * h), 2 * h, GLA_DK)[:, h - 1:h, :],
                                   (C // (2 * h), 2 * h, GLA_DK)).reshape(C, GLA_DK)
            upper = (row // h) % 2 == 1
            q_h = q * jnp.exp(jnp.where(upper, b - ref, neg))
            k_h = k * jnp.exp(jnp.where(upper, neg, ref - b))
            a_h = _dot_nt(q_h.astype(BF16), k_h.astype(BF16))
            attn = attn + (a_h if 2 * h == C else jnp.where(rowi // (2 * h) == coli // (2 * h), a_h, 0.0))
            h //= 2
        return q, b, v, o, attn

    def stage_diag(r0, hh, q, b):
        zs = []
        for bi in range(C // SB):
            s0 = bi * SB
            q_i = q[s0:s0 + SB, :]
            b_i = b[s0:s0 + SB, :]
            for j in range(SB):
                k_j = kf_ref[hh, pl.ds(r0 + (s0 + j), 1), :]
                b_j = b_ref[hh, pl.ds(r0 + (s0 + j), 1), :]
                zs.append(q_i * k_j * jnp.exp(jnp.where(sub_r >= j, b_i - b_j, neg)))
        return _dot(jnp.concatenate(zs, axis=0).astype(BF16), ones)

    def stage_out(r0, hh, v, o, attn, zsum):
        vc = slice(hh * dv, (hh + 1) * dv)
        diag = []
        for bi in range(C // SB):
            acc = jnp.zeros((SB, C), F32)
            for j in range(SB):
                n0 = (bi * SB + j) * SB
                acc = acc + jnp.where(lane_c == bi * SB + j, zsum[n0:n0 + SB, :], 0.0)
            diag.append(acc)
        attn = attn + jnp.concatenate(diag, axis=0)
        o = o + _dot(attn.astype(BF16), v)

        o = o * lax.rsqrt(jnp.mean(o * o, axis=-1, keepdims=True) + EPS) * hg_ref[:, vc]
        g = gate_ref[pl.ds(r0, C), vc].astype(F32)
        o_ref[pl.ds(r0, C), vc] = (o * _silu(g)).astype(o_ref.dtype)

    def chunk(ci, carry):
        r0 = pl.multiple_of(ci * C, C)
        heads = range(hpb)
        st = [stage_state(ci, r0, hh) for hh in heads]
        zsums = [stage_diag(r0, hh, st[hh][0], st[hh][1]) for hh in heads]
        for hh in heads:
            _, _, v, o, attn = st[hh]
            stage_out(r0, hh, v, o, attn, zsums[hh])
        return carry

    lax.fori_loop(0, tb // C, chunk, 0, unroll=GLA_UNROLL)


def gla(proj, small, alpha_up, alpha_b, head_g, *, batch, seq, q_col, k_col, v_col, gate_col, tb=512, hpb=4):
    t = batch * seq
    nb = seq // tb
    dk, dv, h = GLA_DK * hpb, GLA_DV * hpb, GLA_HEADS // hpb
    qb, kb, vb, gb = q_col // dk, k_col // dk, v_col // dv, gate_col // dv
    tok = lambda b, hh, i: b * nb + i
    nc = tb // GLA_CHUNK
    return pl.pallas_call(
        functools.partial(_gla_body, tb=tb, hpb=hpb),
        grid=(batch, h, nb),
        in_specs=[
            pl.BlockSpec((tb, dk), lambda b, hh, i: (tok(b, hh, i), qb + hh)),
            pl.BlockSpec((tb, dk), lambda b, hh, i: (tok(b, hh, i), kb + hh)),
            pl.BlockSpec((tb, dv), lambda b, hh, i: (tok(b, hh, i), vb + hh)),
            pl.BlockSpec((tb, 128), lambda b, hh, i: (tok(b, hh, i), 0)),
            pl.BlockSpec((GLA_LOWRANK, dk), lambda b, hh, i: (0, hh)),
            pl.BlockSpec((1, dk), lambda b, hh, i: (0, hh)),
            pl.BlockSpec((tb, dv), lambda b, hh, i: (tok(b, hh, i), gb + hh)),
            pl.BlockSpec((1, dv), lambda b, hh, i: (0, hh)),
        ],
        out_specs=pl.BlockSpec((tb, dv), lambda b, hh, i: (tok(b, hh, i), hh)),
        out_shape=jax.ShapeDtypeStruct((t, GLA_HEADS * GLA_DV), BF16),
        scratch_shapes=[pltpu.VMEM((hpb, GLA_DK, GLA_DV), F32), pltpu.VMEM((hpb, tb, GLA_DK), F32),
                        pltpu.VMEM((hpb, tb, GLA_DK), F32),
                        pltpu.VMEM((hpb, nc, GLA_DK, GLA_CHUNK), BF16),
                        pltpu.VMEM((hpb, nc, GLA_DK, 128), F32)],
        compiler_params=_cparams(("parallel", "parallel", "arbitrary")),
        name="gla",
    )(proj, proj, proj, small, alpha_up, alpha_b.reshape(1, -1), proj, head_g.reshape(1, -1))


FOX_GATE_UNROLL = 4


def _fox_gate_body(s_ref, fb_ref, qa_ref, ka_ref, cs_ref, off_ref, *, blk, col0):
    n = s_ref.shape[0] // blk
    r = lax.broadcasted_iota(jnp.int32, (blk, blk), 0)
    c = lax.broadcasted_iota(jnp.int32, (blk, blk), 1)
    tri = jnp.where(c <= r, 1.0, 0.0).astype(BF16)
    lanes = s_ref.shape[1]
    pr = lax.broadcasted_iota(jnp.int32, (3 * lanes, FOX_AUG), 0)
    pc = lax.broadcasted_iota(jnp.int32, (3 * lanes, FOX_AUG), 1)
    head, piece = pr % lanes - col0, pr // lanes
    is_head = (head >= 0) & (head < FOX_HEADS)
    sel_q = jnp.where(is_head & (pc == head * FOX_AUG_GROUP + piece), 1.0, 0.0).astype(BF16)
    sel_k = jnp.where(is_head & (pc == head * FOX_AUG_GROUP + 3 + piece), -1.0, 0.0).astype(BF16)
    lane = lax.broadcasted_iota(jnp.int32, (1, FOX_AUG), 1)
    used = lane < FOX_HEADS * FOX_AUG_GROUP
    one_q = jnp.where(used & (lane % FOX_AUG_GROUP >= 3) & (lane % FOX_AUG_GROUP < 6), 1.0, 0.0)
    one_k = jnp.where(used & (lane % FOX_AUG_GROUP < 3), 1.0, 0.0)

    def local(i, carry):
        r0 = pl.multiple_of(i * blk, blk)
        lf = _log_sigmoid(s_ref[pl.ds(r0, blk), :] + fb_ref[...])
        hi, mid, lo = _split3(lf)
        cs_ref[pl.ds(r0, blk), :] = _dot(tri, hi) + _dot(tri, mid) + _dot(tri, lo)
        return carry

    lax.fori_loop(0, n, local, 0, unroll=FOX_GATE_UNROLL)

    off = jnp.zeros((1, lanes), F32)
    for i in range(n):
        off_ref[i:i + 1, :] = off
        off = off + cs_ref[(i + 1) * blk - 1:(i + 1) * blk, :]

    def place(i, carry):
        r0 = pl.multiple_of(i * blk, blk)
        cs = cs_ref[pl.ds(r0, blk), :] + off_ref[pl.ds(i, 1), :]
        pieces = jnp.concatenate(_split3(cs * LOG2E), axis=1)
        qa_ref[pl.ds(r0, blk), :] = (_dot(pieces, sel_q) + one_q).astype(BF16)
        ka_ref[pl.ds(r0, blk), :] = (_dot(pieces, sel_k) + one_k).astype(BF16)
        return carry

    lax.fori_loop(0, n, place, 0, unroll=FOX_GATE_UNROLL)


def fox_gate(small, fb_row, *, batch, seq, col0, blk=256):
    shp = jax.ShapeDtypeStruct((batch * seq, FOX_AUG), BF16)
    spec = pl.BlockSpec((seq, FOX_AUG), lambda b: (b, 0))
    return pl.pallas_call(
        functools.partial(_fox_gate_body, blk=blk, col0=col0),
        grid=(batch,),
        in_specs=[pl.BlockSpec((seq, 128), lambda b: (b, 0)),
                  pl.BlockSpec((1, 128), lambda b: (0, 0))],
        out_specs=[spec, spec],
        out_shape=[shp, shp],
        scratch_shapes=[pltpu.VMEM((seq, 128), F32), pltpu.VMEM((seq // blk, 128), F32)],
        compiler_params=_cparams(("parallel",)),
        name="fox_gate",
    )(small, fb_row)


def _fox_body(it_ref, jt_ref, q_ref, qa_ref, k_ref, ka_ref, v_ref, gate_ref, o_ref,
              m_ref, l_ref, acc_ref, *, tq, tk, hpb):
    p = pl.program_id(2)
    i = it_ref[p]
    j = jt_ref[p]
    d = FOX_DH

    @pl.when(j == 0)
    def _():
        m_ref[...] = jnp.full_like(m_ref, -jnp.inf)
        l_ref[...] = jnp.zeros_like(l_ref)
        acc_ref[...] = jnp.zeros_like(acc_ref)

    aug_lane = lax.broadcasted_iota(jnp.int32, (1, FOX_AUG), 1)

    def scores(item, masked):
        hh, ks, qs = item
        mine = aug_lane // FOX_AUG_GROUP == pl.program_id(1) * hpb + hh
        zero = jnp.zeros((), BF16)
        q_aug = jnp.concatenate([q_ref[qs, hh * d:(hh + 1) * d], jnp.where(mine, qa_ref[qs, :], zero)], axis=1)
        k_aug = jnp.concatenate([k_ref[ks, hh * d:(hh + 1) * d], jnp.where(mine, ka_ref[ks, :], zero)], axis=1)
        st = _dot_nt(k_aug, q_aug)
        if masked:
            kr = lax.broadcasted_iota(jnp.int32, st.shape, 0) + ks.start
            qc = lax.broadcasted_iota(jnp.int32, st.shape, 1) + qs.start
            st = jnp.where(qc >= kr, st, -jnp.inf)
        return st

    def absorb(item, st):
        hh, ks, qs = item
        m_old = m_ref[hh, :, qs]
        m_new = jnp.maximum(m_old, jnp.max(st, axis=0, keepdims=True))
        alpha = jnp.exp2(m_old - m_new)
        pt = jnp.exp2(st - m_new)
        l_ref[hh, :, qs] = alpha * l_ref[hh, :, qs] + jnp.sum(pt, axis=0, keepdims=True)
        acc_ref[hh, :, qs] = (alpha * acc_ref[hh, :, qs]
                              + _dot_tn(v_ref[ks, hh * d:(hh + 1) * d], pt.astype(BF16)))
        m_ref[hh, :, qs] = m_new

    def block(masked, finish):
        full = slice(0, tq)
        halves = [(slice(0, tk // 2), full), (slice(tk // 2, tk), slice(tq // 2, tq))] if masked else [(slice(0, tk), full)]
        items = [(hh, ks, qs) for hh in range(hpb) for ks, qs in halves]
        st = scores(items[0], masked)
        for n, item in enumerate(items):
            st_next = scores(items[n + 1], masked) if n + 1 < len(items) else None
            absorb(item, st)
            hh = item[0]
            if finish and (n + 1 == len(items) or items[n + 1][0] != hh):
                o = (acc_ref[hh] / l_ref[hh]).T
                g = gate_ref[:, hh * d:(hh + 1) * d].astype(F32)
                o_ref[:, hh * d:(hh + 1) * d] = (o * _silu(g)).astype(o_ref.dtype)
            st = st_next

    @pl.when(j < i)
    def _():
        block(False, False)

    @pl.when(j == i)
    def _():
        block(True, True)


def fox(proj, qa, ka, *, batch, seq, q_col, k_col, v_col, gate_col, t=1024, hpb=8):
    tt = batch * seq
    nb = seq // t
    d, h = FOX_DH, FOX_HEADS
    w = hpb * d
    qb, kb, gb, vb = q_col // w, k_col // w, gate_col // w, v_col // w
    pairs = [(i, j) for i in range(nb) for j in range(i + 1)]
    it = jnp.array([p[0] for p in pairs], jnp.int32)
    jt = jnp.array([p[1] for p in pairs], jnp.int32)
    grid_spec = pltpu.PrefetchScalarGridSpec(
        num_scalar_prefetch=2,
        grid=(batch, h // hpb, len(pairs)),
        in_specs=[
            pl.BlockSpec((t, w), lambda b, hh, p, it, jt: (b * nb + it[p], qb + hh)),
            pl.BlockSpec((t, FOX_AUG), lambda b, hh, p, it, jt: (b * nb + it[p], 0)),
            pl.BlockSpec((t, w), lambda b, hh, p, it, jt: (b * nb + jt[p], kb + hh)),
            pl.BlockSpec((t, FOX_AUG), lambda b, hh, p, it, jt: (b * nb + jt[p], 0)),
            pl.BlockSpec((t, w), lambda b, hh, p, it, jt: (b * nb + jt[p], vb + hh)),
            pl.BlockSpec((t, w), lambda b, hh, p, it, jt: (b * nb + it[p], gb + hh)),
        ],
        out_specs=pl.BlockSpec((t, w), lambda b, hh, p, it, jt: (b * nb + it[p], hh)),
        scratch_shapes=[pltpu.VMEM((hpb, 1, t), F32), pltpu.VMEM((hpb, 1, t), F32), pltpu.VMEM((hpb, d, t), F32)],
    )
    return pl.pallas_call(
        functools.partial(_fox_body, tq=t, tk=t, hpb=hpb),
        grid_spec=grid_spec,
        out_shape=jax.ShapeDtypeStruct((tt, h * d), BF16),
        compiler_params=_cparams(("parallel", "parallel", "arbitrary")),
        name="fox",
    )(it, jt, proj, qa, proj, ka, proj, proj)


def _regroup_perm(tm, steps, to_grouped):
    r = lax.broadcasted_iota(jnp.int32, (tm, tm), 0)
    c = lax.broadcasted_iota(jnp.int32, (tm, tm), 1)
    nc = tm // steps
    src = (r % nc) * steps + r // nc if to_grouped else (r % steps) * nc + r // steps
    return jnp.where(c == src, 1.0, 0.0).astype(BF16)


OUTPROJ_PART_ROWS = 256


def _cast_weight_once(wb_ref, w_ref):
    @pl.when(pl.program_id(0) == 0)
    def _():
        for r0 in range(0, w_ref.shape[0], MM_SUB):
            wb_ref[r0:r0 + MM_SUB, :] = w_ref[r0:r0 + MM_SUB, :].astype(wb_ref.dtype)


def _outproj_mid_body(a1_ref, a2_ref, wf_ref, x_ref, g_ref, r_ref, h_ref, w_ref, *, steps):
    k1 = a1_ref.shape[1]
    tm, d = x_ref.shape
    tp = OUTPROJ_PART_ROWS
    nc = tp // steps
    _cast_weight_once(w_ref, wf_ref)
    parts = [slice(p * tp, (p + 1) * tp) for p in range(tm // tp)]
    accs = [x_ref[rows, :] + _dot(a1_ref[rows, :], w_ref[:k1, :]) + _dot(a2_ref[rows, :], w_ref[k1:, :])
            for rows in parts]
    perm = _regroup_perm(tp, steps, True)
    for p, rows in enumerate(parts):
        acc = accs[p]
        r_ref[rows, :] = acc
        normed = (acc * lax.rsqrt(jnp.mean(acc * acc, axis=-1, keepdims=True) + EPS) * g_ref[...]).astype(BF16)
        by_step = _dot(perm, normed).astype(h_ref.dtype)
        for s in range(steps):
            h_ref[p * nc:(p + 1) * nc, s * d:(s + 1) * d] = by_step[s * nc:(s + 1) * nc, :]


def outproj_mid(a1, a2, w, x, g, *, steps, tm=512, name="outproj_mid"):
    m, d = x.shape
    row = pl.BlockSpec((tm, d), lambda i: (i, 0))
    return pl.pallas_call(
        functools.partial(_outproj_mid_body, steps=steps),
        grid=(m // tm,),
        in_specs=[pl.BlockSpec((tm, a1.shape[1]), lambda i: (i, 0)),
                  pl.BlockSpec((tm, a2.shape[1]), lambda i: (i, 0)),
                  pl.BlockSpec(w.shape, lambda i: (0, 0), pipeline_mode=pl.Buffered(1)),
                  row,
                  pl.BlockSpec((1, d), lambda i: (0, 0))],
        out_specs=[row, pl.BlockSpec((tm // steps, steps * d), lambda i: (i, 0))],
        out_shape=[jax.ShapeDtypeStruct((m, d), F32), jax.ShapeDtypeStruct((m // steps, steps * d), BF16)],
        scratch_shapes=[pltpu.VMEM(w.shape, BF16)],
        compiler_params=_cparams(("arbitrary",)),
        name=name,
    )(a1, a2, w, x, g.reshape(1, d))


def _outproj_final_body(a_ref, wf_ref, x_ref, g_ref, o_ref, w_ref, *, steps):
    tm = x_ref.shape[0]
    e = w_ref.shape[0]
    tp = OUTPROJ_PART_ROWS
    nc = tp // steps
    _cast_weight_once(w_ref, wf_ref)
    perm = _regroup_perm(tp, steps, False)
    a_toks = []
    for p in range(tm // tp):
        by_step = jnp.concatenate([a_ref[p * nc:(p + 1) * nc, s * e:(s + 1) * e] for s in range(steps)],
                                  axis=0)
        a_toks.append(_dot(perm, by_step).astype(BF16))
    parts = [slice(p * tp, (p + 1) * tp) for p in range(tm // tp)]
    accs = [x_ref[rows, :] + _dot(a_toks[p], w_ref[...]) for p, rows in enumerate(parts)]
    for p, rows in enumerate(parts):
        acc = accs[p]
        o_ref[rows, :] = acc * lax.rsqrt(jnp.mean(acc * acc, axis=-1, keepdims=True) + EPS) * g_ref[...]


def outproj_final(a_grouped, w, x, g, *, steps, tm=512, name="outproj_final"):
    m, d = x.shape
    e = w.shape[0]
    row = pl.BlockSpec((tm, d), lambda i: (i, 0))
    return pl.pallas_call(
        functools.partial(_outproj_final_body, steps=steps),
        grid=(m // tm,),
        in_specs=[pl.BlockSpec((tm // steps, steps * e), lambda i: (i, 0)),
                  pl.BlockSpec(w.shape, lambda i: (0, 0), pipeline_mode=pl.Buffered(1)),
                  row,
                  pl.BlockSpec((1, d), lambda i: (0, 0))],
        out_specs=row,
        out_shape=jax.ShapeDtypeStruct((m, d), F32),
        scratch_shapes=[pltpu.VMEM(w.shape, BF16)],
        compiler_params=_cparams(("arbitrary",)),
        name=name,
    )(a_grouped, w, x, g.reshape(1, d))


S5_SCAN_BLK = 8
S5_GPB = 8


def _s5_entering_state(x, arow, *, cols_per_seq, n_levels):
    n = x.shape[1]
    p = S5_STATE
    nseq = n // cols_per_seq

    def to_rows(part):
        return jnp.concatenate([part[:, s * cols_per_seq:(s + 1) * cols_per_seq] for s in range(nseq)], axis=0).T

    def to_cols(rows):
        rt = rows.T
        return jnp.concatenate([rt[s * p:(s + 1) * p, :] for s in range(nseq)], axis=1)

    xr, xi = to_rows(x[:p, :]), to_rows(x[p:, :])
    pos = lax.broadcasted_iota(jnp.int32, (cols_per_seq, 1), 0)
    blk = S5_SCAN_BLK
    half = arow.shape[0] // 2
    for lvl in range(n_levels):
        sh = 1 << lvl
        keep = pos % blk >= sh
        sr = jnp.where(keep, pltpu.roll(xr, sh, axis=0), 0.0)
        si = jnp.where(keep, pltpu.roll(xi, sh, axis=0), 0.0)
        ar, ai = arow[lvl:lvl + 1, :], arow[half + lvl:half + lvl + 1, :]
        xr, xi = xr + ar * sr - ai * si, xi + ar * si + ai * sr
    pw_r, pw_i = arow[n_levels:n_levels + blk, :], arow[half + n_levels:half + n_levels + blk, :]
    out_r, out_i = [xr[:blk, :]], [xi[:blk, :]]
    for r0 in range(blk, cols_per_seq, blk):
        cr, ci = out_r[-1][blk - 1:blk, :], out_i[-1][blk - 1:blk, :]
        out_r.append(xr[r0:r0 + blk, :] + pw_r * cr - pw_i * ci)
        out_i.append(xi[r0:r0 + blk, :] + pw_r * ci + pw_i * cr)
    xr, xi = jnp.concatenate(out_r, axis=0), jnp.concatenate(out_i, axis=0)
    keep = pos >= 1
    pr = jnp.where(keep, pltpu.roll(xr, 1, axis=0), 0.0)
    pi = jnp.where(keep, pltpu.roll(xi, 1, axis=0), 0.0)
    return jnp.concatenate([to_cols(pr), to_cols(pi)], axis=0).astype(BF16)


def _s5_body(u_ref, wt_ref, nt_ref, mt_ref, arow_ref, o_ref, *, cols_per_seq, n_levels):
    t, _, n = u_ref.shape
    cg = S5_GROUP_CH
    groups = range(S5_GPB)
    ch = [slice(gi * cg, (gi + 1) * cg) for gi in groups]
    us = [u_ref[:, ch[gi], :].reshape(t * cg, n) for gi in groups]
    xs = [_dot(nt_ref[gi], us[gi]) for gi in groups]
    ys = [_dot(wt_ref[gi], us[gi]) for gi in groups]
    prevs = [_s5_entering_state(xs[gi], arow_ref[gi], cols_per_seq=cols_per_seq, n_levels=n_levels)
             for gi in groups]
    for gi in groups:
        y = ys[gi] + _dot(mt_ref[gi], prevs[gi])
        o_ref[:, ch[gi], :] = y.astype(o_ref.dtype).reshape(t, cg, n)


def s5_apply(proj_t, wt_g, nt_g, mt_g, arow, *, groups, cols_per_seq, n_levels):
    t, _, n = proj_t.shape
    cg = S5_GROUP_CH * S5_GPB
    spec3 = lambda shp: pl.BlockSpec((S5_GPB,) + shp, lambda i: (i, 0, 0))
    return pl.pallas_call(
        functools.partial(_s5_body, cols_per_seq=cols_per_seq, n_levels=n_levels),
        grid=(groups // S5_GPB,),
        in_specs=[pl.BlockSpec((t, cg, n), lambda i: (0, i, 0)),
                  spec3(wt_g.shape[1:]), spec3(nt_g.shape[1:]), spec3(mt_g.shape[1:]), spec3(arow.shape[1:])],
        out_specs=pl.BlockSpec((t, cg, n), lambda i: (0, i, 0)),
        out_shape=jax.ShapeDtypeStruct((t, groups * S5_GROUP_CH, n), BF16),
        compiler_params=_cparams(("parallel",)),
        name="s5",
    )(proj_t, wt_g, nt_g, mt_g, arow)


def s5_operators(lam_re, lam_im, log_step, b_re, b_im, c_re, c_im, d, *, n_levels, n_seq):
    hp = lax.Precision.HIGH
    g, p = lam_re.shape
    cg, t = S5_GROUP_CH, S5_T
    lr = jnp.minimum(lam_re.astype(F32), -1e-4)
    li = lam_im.astype(F32)
    step = jnp.exp(log_step.astype(F32))[:, None]
    mag = jnp.exp(lr * step)
    lb_re, lb_im = mag * jnp.cos(li * step), mag * jnp.sin(li * step)
    den = lr * lr + li * li
    nr, ni = lb_re - 1.0, lb_im
    coef_re = (nr * lr + ni * li) / den
    coef_im = (ni * lr - nr * li) / den
    bb_re = coef_re[..., None] * b_re - coef_im[..., None] * b_im
    bb_im = coef_re[..., None] * b_im + coef_im[..., None] * b_re

    def power(m):
        mm = m.astype(F32)[None, :, None]
        mg = jnp.exp(lr[:, None, :] * step[:, None, :] * mm)
        ang = li[:, None, :] * step[:, None, :] * mm
        return mg * jnp.cos(ang), mg * jnp.sin(ang)

    pw_re, pw_im = power(jnp.arange(t + 1))
    pwt_re, pwt_im = pw_re.transpose(0, 2, 1), pw_im.transpose(0, 2, 1)

    def pow_times_b(pr, pi):
        re = pr[:, :, :, None] * bb_re[:, :, None, :] - pi[:, :, :, None] * bb_im[:, :, None, :]
        im = pr[:, :, :, None] * bb_im[:, :, None, :] + pi[:, :, :, None] * bb_re[:, :, None, :]
        return re.reshape(g, p, t * cg), im.reshape(g, p, t * cg)

    pb_re, pb_im = pow_times_b(pwt_re[:, :, :t], pwt_im[:, :, :t])
    taps = (jnp.einsum('gcp,gpy->gcy', c_re.astype(F32), pb_re, precision=hp)
            - jnp.einsum('gcp,gpy->gcy', c_im.astype(F32), pb_im, precision=hp))
    y_idx = jnp.arange(t * cg)
    taps = taps + d.astype(F32).reshape(g, cg, 1) * (y_idx[None, :] == jnp.arange(cg)[:, None]).astype(F32)[None]
    place = ((y_idx[None, :, None] // cg == jnp.arange(t)[:, None, None] - y_idx[None, None, :] // cg)
             & (y_idx[None, :, None] % cg == y_idx[None, None, :] % cg)).astype(BF16)
    wt_g = jnp.einsum('gcy,tyz->gtcz', taps.astype(BF16), place,
                      preferred_element_type=F32).astype(BF16).reshape(g, t * cg, t * cg)
    n_re, n_im = pow_times_b(pwt_re[:, :, t - 1::-1], pwt_im[:, :, t - 1::-1])
    nt_g = jnp.concatenate([n_re, n_im], axis=1).astype(BF16)
    cp_re = c_re[:, None] * pw_re[:, 1:, None, :] - c_im[:, None] * pw_im[:, 1:, None, :]
    cp_im = c_re[:, None] * pw_im[:, 1:, None, :] + c_im[:, None] * pw_re[:, 1:, None, :]
    mt_g = jnp.concatenate([cp_re, -cp_im], axis=-1).reshape(g, t * cg, 2 * p).astype(BF16)
    ar, ai = power(t * jnp.concatenate([2 ** jnp.arange(n_levels), jnp.arange(1, S5_SCAN_BLK + 1)]))
    arow = jnp.concatenate([jnp.tile(ar, (1, 1, n_seq)), jnp.tile(ai, (1, 1, n_seq))], axis=1)
    return wt_g, nt_g, mt_g, arow


def _gelu_tanh(y):
    return 0.5 * y * (1.0 + jnp.tanh(math.sqrt(2.0 / math.pi) * (y + 0.044715 * (y * y * y))))


GLU_PARTS = 2


def _glu_body(y_ref, w_ref, b_ref, gate_ref, o_ref, z_ref, zb_ref, wt_ref):
    @pl.when((pl.program_id(0) == 0) & (pl.program_id(1) == 0))
    def _():
        _transpose_into(wt_ref, w_ref)

    tm = y_ref.shape[1]
    parts = [slice(c0, c0 + tm // GLU_PARTS) for c0 in range(0, tm, tm // GLU_PARTS)]
    for cols in parts:
        z = _gelu_tanh(y_ref[:, cols].astype(F32))
        z_ref[:, cols] = z
        zb_ref[:, cols] = z.astype(BF16)
    for cols in parts:
        for f0 in range(0, o_ref.shape[1], MM_SUB):
            rows = slice(f0, f0 + MM_SUB)
            lin = _dot(wt_ref[rows, :], zb_ref[:, cols]) + b_ref[rows, :]
            out = z_ref[rows, cols] * _sigmoid(lin) * _silu(gate_ref[rows, cols].astype(F32))
            o_ref[cols, rows] = out.T.astype(o_ref.dtype)


def glu_gate(y_t, w, b, proj_t, *, gate_row, tm=512):
    s, e, n = y_t.shape
    gb = gate_row // e
    return pl.pallas_call(
        _glu_body,
        grid=(s, n // tm),
        in_specs=[pl.BlockSpec((None, e, tm), lambda t, i: (t, 0, i)),
                  pl.BlockSpec((e, e), lambda t, i: (0, 0), pipeline_mode=pl.Buffered(1)),
                  pl.BlockSpec((e, 1), lambda t, i: (0, 0), pipeline_mode=pl.Buffered(1)),
                  pl.BlockSpec((None, e, tm), lambda t, i: (t, gb, i))],
        out_specs=pl.BlockSpec((tm, e), lambda t, i: (i, t)),
        out_shape=jax.ShapeDtypeStruct((n, s * e), BF16),
        scratch_shapes=[pltpu.VMEM((e, tm), F32), pltpu.VMEM((e, tm), BF16), pltpu.VMEM((e, e), BF16)],
        compiler_params=_cparams(("arbitrary", "arbitrary")),
        name="glu",
    )(y_t, w, b.reshape(e, 1), proj_t)


def kernel(x, ab_norm_g, ab_w_in, gla_alpha_up, gla_alpha_b, gla_head_g, fox_f_b, ab_w_out, c_norm_g, c_w_in, s5_lambda_re, s5_lambda_im, s5_log_step, s5_b_re, s5_b_im, s5_c_re, s5_c_im, s5_d, glu_w, glu_b, c_w_out, final_norm_g):
    batch, seq, d = x.shape
    t = batch * seq
    x2 = x.reshape(t, d)

    hk, hv, fw = GLA_HEADS * GLA_DK, GLA_HEADS * GLA_DV, FOX_HEADS * FOX_DH
    sizes = (hk, hk, hv, GLA_LOWRANK, hv, fw, fw, fw, FOX_HEADS, fw)
    offs = [0]
    for s in sizes:
        offs.append(offs[-1] + s)
    w_main, w_small = ab_weight_prep(jnp.swapaxes(ab_w_in, 1, 2), offs)
    c_gq, c_gk, c_gv, c_gg = 0, hk, 2 * hk, 2 * hk + hv
    c_fq = c_gg + hv
    c_fk, c_fv, c_fg = c_fq + fw, c_fq + 2 * fw, c_fq + 3 * fw

    proj, small = norm_proj(x2, ab_norm_g[0], w_main, w_small, tn=w_main.shape[0] // 2, name="ab_in")

    o_gla = gla(proj, small, gla_alpha_up[0], gla_alpha_b[0], gla_head_g[0],
                batch=batch, seq=seq, q_col=c_gq, k_col=c_gk, v_col=c_gv, gate_col=c_gg)

    fb_row = jnp.zeros((1, 128), F32).at[0, GLA_LOWRANK:GLA_LOWRANK + FOX_HEADS].set(fox_f_b[0])
    qa, ka = fox_gate(small, fb_row, batch=batch, seq=seq, col0=GLA_LOWRANK)
    o_fox = fox(proj, qa, ka, batch=batch, seq=seq, q_col=c_fq, k_col=c_fk, v_col=c_fv, gate_col=c_fg)

    x1, h1g = outproj_mid(o_gla, o_fox, ab_w_out[0], x2, c_norm_g[0], steps=S5_T, name="ab_out")

    e = d
    groups = e // S5_GROUP_CH
    n_chunks = t // S5_T
    cols_per_seq = seq // S5_T
    n_levels = S5_SCAN_BLK.bit_length() - 1
    proj_t = matmul_t_grouped(c_w_in[0], h1g, S5_T, BF16, name="c_in")
    ops = s5_operators(s5_lambda_re[0], s5_lambda_im[0], s5_log_step[0], s5_b_re[0], s5_b_im[0],
                       s5_c_re[0], s5_c_im[0], s5_d[0], n_levels=n_levels, n_seq=batch)
    y_t = s5_apply(proj_t, *ops, groups=groups, cols_per_seq=cols_per_seq, n_levels=n_levels)
    zz = glu_gate(y_t, glu_w[0], glu_b[0], proj_t, gate_row=e)
    out = outproj_final(zz, c_w_out[0], x1, final_norm_g, steps=S5_T, name="c_out")
    return out.reshape(batch, seq, d)
```

```python
import functools
import math

import jax
import jax.numpy as jnp
from jax import lax
from jax.experimental import pallas as pl
from jax.experimental.pallas import tpu as pltpu

EPS = 1e-6
F32 = jnp.float32
BF16 = jnp.bfloat16

GLA_HEADS = 4
GLA_DK = 128
GLA_DV = 256
GLA_LOWRANK = 16
GLA_TAU = 16.0
GLA_CHUNK = 64
GLA_SUB = 8
GLA_UNROLL = 8
FOX_HEADS = 8
FOX_DH = 128
FOX_AUG = 128
FOX_AUG_GROUP = 8
S5_GROUP_CH = 16
S5_STATE = 64
S5_T = 16

VMEM_LIMIT = 58 * 1024 * 1024

NT_DIMS = (((1,), (1,)), ((), ()))
TN_DIMS = (((0,), (0,)), ((), ()))
LOG2E = math.log2(math.e)


def _cparams(sem):
    return pltpu.CompilerParams(dimension_semantics=sem, vmem_limit_bytes=VMEM_LIMIT)


def _dot(a, b):
    return jnp.dot(a, b, preferred_element_type=F32)


def _dot_nt(a, b):
    return lax.dot_general(a, b, NT_DIMS, preferred_element_type=F32)


def _dot_tn(a, b):
    return lax.dot_general(a, b, TN_DIMS, preferred_element_type=F32)


def _log_sigmoid(z):
    return jnp.minimum(z, 0.0) - jnp.log(1.0 + jnp.exp(-jnp.abs(z)))


def _sigmoid(z):
    return 0.5 * jnp.tanh(0.5 * z) + 0.5


def _silu(z):
    return z * _sigmoid(z)


def _split3(x):
    hi = x.astype(BF16)
    r1 = x - hi.astype(F32)
    mid = r1.astype(BF16)
    lo = (r1 - mid.astype(F32)).astype(BF16)
    return hi, mid, lo


def _norm_proj_body(x_ref, g_ref, wt_ref, ws_ref, h_ref, o_ref, s_ref):
    x = x_ref[...]
    ms = jnp.mean(x * x, axis=-1, keepdims=True)
    h = (x * lax.rsqrt(ms + EPS) * g_ref[...]).astype(BF16)
    h_ref[...] = h
    for c0 in range(0, o_ref.shape[1], MM_SUB):
        o_ref[:, c0:c0 + MM_SUB] = _dot_nt(h, wt_ref[c0:c0 + MM_SUB, :]).astype(o_ref.dtype)
    s_ref[...] = _dot_nt(h, ws_ref[...])


def norm_proj(x, g, wt, wt_small, tm=512, name="norm_proj"):
    m, d = x.shape
    n = wt.shape[0]
    ns = wt_small.shape[0]
    return pl.pallas_call(
        _norm_proj_body,
        grid=(m // tm,),
        in_specs=[pl.BlockSpec((tm, d), lambda i: (i, 0)),
                  pl.BlockSpec((1, d), lambda i: (0, 0)),
                  pl.BlockSpec((n, d), lambda i: (0, 0), pipeline_mode=pl.Buffered(1)),
                  pl.BlockSpec((ns, d), lambda i: (0, 0))],
        out_specs=[pl.BlockSpec((tm, d), lambda i: (i, 0)),
                   pl.BlockSpec((tm, n), lambda i: (i, 0)),
                   pl.BlockSpec((tm, ns), lambda i: (i, 0))],
        out_shape=[jax.ShapeDtypeStruct((m, d), BF16), jax.ShapeDtypeStruct((m, n), BF16),
                   jax.ShapeDtypeStruct((m, ns), F32)],
        compiler_params=_cparams(("parallel",)),
        name=name,
    )(x, g.reshape(1, d), wt, wt_small)


def _mm_body(a_ref, wt_ref, o_ref):
    a = a_ref[...]
    for c0 in range(0, o_ref.shape[1], MM_SUB):
        o_ref[:, c0:c0 + MM_SUB] = _dot_nt(a, wt_ref[c0:c0 + MM_SUB, :]).astype(o_ref.dtype)


def matmul(a, wt, tm=512, name="matmul"):
    m, k = a.shape
    n = wt.shape[0]
    return pl.pallas_call(
        _mm_body,
        grid=(m // tm,),
        in_specs=[pl.BlockSpec((tm, k), lambda i: (i, 0)),
                  pl.BlockSpec((n, k), lambda i: (0, 0), pipeline_mode=pl.Buffered(1))],
        out_specs=pl.BlockSpec((tm, n), lambda i: (i, 0)),
        out_shape=jax.ShapeDtypeStruct((m, n), BF16),
        compiler_params=_cparams(("parallel",)),
        name=name,
    )(a, wt)


def _ab_wprep_body(w_ref, og_ref, of_ref, s_ref, *, offs):
    w = w_ref[0]
    seg = lambda n: w[offs[n]:offs[n + 1], :]
    og_ref[...] = jnp.concatenate([w[:offs[3], :], seg(4)], axis=0).astype(og_ref.dtype)
    of_ref[...] = jnp.concatenate([seg(5) * (FOX_DH ** -0.5 * LOG2E), w[offs[6]:offs[8], :], seg(9)],
                                  axis=0).astype(of_ref.dtype)
    pad = s_ref.shape[0] - (offs[4] - offs[3]) - (offs[9] - offs[8])
    s_ref[...] = jnp.concatenate([seg(3), seg(8), jnp.zeros((pad, w.shape[1]), F32)], axis=0).astype(s_ref.dtype)


def ab_weight_prep(w_t3, offs, tk=256):
    _, n_in, d = w_t3.shape
    n_gla = offs[3] + offs[5] - offs[4]
    n_fox = offs[8] - offs[5] + offs[10] - offs[9]
    return pl.pallas_call(
        functools.partial(_ab_wprep_body, offs=tuple(offs)),
        grid=(d // tk,),
        in_specs=[pl.BlockSpec((1, n_in, tk), lambda i: (0, 0, i))],
        out_specs=[pl.BlockSpec((n_gla, tk), lambda i: (0, i)), pl.BlockSpec((n_fox, tk), lambda i: (0, i)),
                   pl.BlockSpec((128, tk), lambda i: (0, i))],
        out_shape=[jax.ShapeDtypeStruct((n_gla, d), BF16), jax.ShapeDtypeStruct((n_fox, d), BF16),
                   jax.ShapeDtypeStruct((128, d), BF16)],
        compiler_params=_cparams(("parallel",)),
        name="ab_wprep",
    )(w_t3)


MM_SUB = 512


def _transpose_into(wt_ref, w_ref):
    for c0 in range(0, w_ref.shape[1], MM_SUB):
        wt_ref[c0:c0 + MM_SUB, :] = w_ref[:, c0:c0 + MM_SUB].astype(wt_ref.dtype).T


def _mm_t_body(w_ref, a_ref, o_ref, wt_ref):
    @pl.when((pl.program_id(1) == 0) & (pl.program_id(2) == 0))
    def _():
        _transpose_into(wt_ref, w_ref)

    a = a_ref[...]
    for r0 in range(0, o_ref.shape[0], MM_SUB):
        o_ref[r0:r0 + MM_SUB, :] = _dot_nt(wt_ref[r0:r0 + MM_SUB, :], a).astype(o_ref.dtype)


def matmul_t_grouped(w, a2, steps, out_dtype, tm=512, tn=2048, name="matmul_tg"):
    k, nf = w.shape
    n = a2.shape[0]
    return pl.pallas_call(
        _mm_t_body,
        grid=(nf // tn, steps, n // tm),
        in_specs=[pl.BlockSpec((k, tn), lambda j, t, i: (0, j), pipeline_mode=pl.Buffered(1)),
                  pl.BlockSpec((tm, k), lambda j, t, i: (i, t))],
        out_specs=pl.BlockSpec((None, tn, tm), lambda j, t, i: (t, j, i)),
        out_shape=jax.ShapeDtypeStruct((steps, nf, n), out_dtype),
        scratch_shapes=[pltpu.VMEM((tn, k), BF16)],
        compiler_params=_cparams(("arbitrary", "arbitrary", "arbitrary")),
        name=name,
    )(w, a2)


def _gla_body(q_ref, k_ref, v_ref, glr_ref, aup_ref, ab_ref, gate_ref, hg_ref, o_ref,
              s_ref, b_ref, kf_ref, kts_ref, dec_ref, *, tb, hpb):
    ib = pl.program_id(2)
    C, SB, dk, dv = GLA_CHUNK, GLA_SUB, GLA_DK, GLA_DV

    @pl.when(ib == 0)
    def _():
        s_ref[...] = jnp.zeros_like(s_ref)

    glr = glr_ref[:, :GLA_LOWRANK].astype(BF16)
    tw = 2 * C
    r = lax.broadcasted_iota(jnp.int32, (tw, tw), 0)
    c = lax.broadcasted_iota(jnp.int32, (tw, tw), 1)
    tri = jnp.where((r // C == c // C) & (c <= r), 1.0, 0.0).astype(BF16)
    nc = tb // C
    heads = range(hpb)
    kcs = [slice(hh * dk, (hh + 1) * dk) for hh in heads]
    zs = [_dot(glr, aup_ref[:, kcs[hh]].astype(BF16)) + ab_ref[:, kcs[hh]] for hh in heads]
    b_blks = []
    for hh in heads:
        la = _log_sigmoid(zs[hh]) * (1.0 / GLA_TAU)
        hi = la.astype(BF16)
        lo = (la - hi.astype(F32)).astype(BF16)
        b_blks.append(jnp.concatenate(
            [_dot(tri, hi[r0:r0 + tw]) + _dot(tri, lo[r0:r0 + tw]) for r0 in range(0, tb, tw)], axis=0))
    for hh in heads:
        b_blk = b_blks[hh]
        k_blk = k_ref[:, kcs[hh]].astype(F32)
        b_ref[hh] = b_blk
        kf_ref[hh] = k_blk
        b3 = b_blk.reshape(nc, C, dk)
        b_end = b3[:, C - 1:C, :]
        k_end = (k_blk.reshape(nc, C, dk) * jnp.exp(b_end - b3)).reshape(tb, dk)
        kt_end = k_end.T.astype(BF16)
        dec_t = jnp.exp(b_end.reshape(nc, dk)).T
        for ci in range(nc):
            kts_ref[hh, ci] = kt_end[:, ci * C:(ci + 1) * C]
            dec_ref[hh, ci] = jnp.broadcast_to(dec_t[:, ci:ci + 1], (dk, 128))

    row = lax.broadcasted_iota(jnp.int32, (C, 1), 0)
    rowi = lax.broadcasted_iota(jnp.int32, (C, C), 0)
    coli = lax.broadcasted_iota(jnp.int32, (C, C), 1)
    lane_c = lax.broadcasted_iota(jnp.int32, (SB, C), 1)
    sub_r = lax.broadcasted_iota(jnp.int32, (SB, 1), 0)
    ones = jnp.ones((GLA_DK, C), BF16)
    neg = -jnp.inf
    scale = GLA_DK ** -0.5

    def stage_state(ci, r0, hh):
        kc = slice(hh * dk, (hh + 1) * dk)
        vc = slice(hh * dv, (hh + 1) * dv)
        q = q_ref[pl.ds(r0, C), kc].astype(F32) * scale
        k = kf_ref[hh, pl.ds(r0, C), :]
        b = b_ref[hh, pl.ds(r0, C), :]
        v = v_ref[pl.ds(r0, C), vc]

        s = s_ref[hh]
        o = _dot((q * jnp.exp(b)).astype(BF16), s.astype(BF16))
        dec = dec_ref[hh, ci]
        s_ref[hh] = s * jnp.concatenate([dec] * (dv // 128), axis=1) + _dot(kts_ref[hh, ci], v)

        attn = jnp.zeros((C, C), F32)
        h = C // 2
        while h >= SB:
            ref = jnp.broadcast_to(b.reshape(C // (2 * h), 2 * h, GLA_DK)[:, h - 1:h, :],
                                   (C // (2 * h), 2 * h, GLA_DK)).reshape(C, GLA_DK)
            upper = (row // h) % 2 == 1
            q_h = q * jnp.exp(jnp.where(upper, b - ref, neg))
            k_h = k * jnp.exp(jnp.where(upper, neg, ref - b))
            a_h = _dot_nt(q_h.astype(BF16), k_h.astype(BF16))
            attn = attn + (a_h if 2 * h == C else jnp.where(rowi // (2 * h) == coli // (2 * h), a_h, 0.0))
            h //= 2
        return q, b, v, o, attn

    def stage_diag(r0, hh, q, b):
        zs = []
        for bi in range(C // SB):
            s0 = bi * SB
            q_i = q[s0:s0 + SB, :]
            b_i = b[s0:s0 + SB, :]
            for j in range(SB):
                k_j = kf_ref[hh, pl.ds(r0 + (s0 + j), 1), :]
                b_j = b_ref[hh, pl.ds(r0 + (s0 + j), 1), :]
                zs.append(q_i * k_j * jnp.exp(jnp.where(sub_r >= j, b_i - b_j, neg)))
        return _dot(jnp.concatenate(zs, axis=0).astype(BF16), ones)

    def stage_out(r0, hh, v, o, attn, zsum):
        vc = slice(hh * dv, (hh + 1) * dv)
        diag = []
        for bi in range(C // SB):
            acc = jnp.zeros((SB, C), F32)
            for j in range(SB):
                n0 = (bi * SB + j) * SB
                acc = acc + jnp.where(lane_c == bi * SB + j, zsum[n0:n0 + SB, :], 0.0)
            diag.append(acc)
        attn = attn + jnp.concatenate(diag, axis=0)
        o = o + _dot(attn.astype(BF16), v)

        o = o * lax.rsqrt(jnp.mean(o * o, axis=-1, keepdims=True) + EPS) * hg_ref[:, vc]
        g = gate_ref[pl.ds(r0, C), vc].astype(F32)
        o_ref[pl.ds(r0, C), vc] = (o * _silu(g)).astype(o_ref.dtype)

    def chunk(ci, carry):
        r0 = pl.multiple_of(ci * C, C)
        heads = range(hpb)
        st = [stage_state(ci, r0, hh) for hh in heads]
        zsums = [stage_diag(r0, hh, st[hh][0], st[hh][1]) for hh in heads]
        for hh in heads:
            _, _, v, o, attn = st[hh]
            stage_out(r0, hh, v, o, attn, zsums[hh])
        return carry

    lax.fori_loop(0, tb // C, chunk, 0, unroll=GLA_UNROLL)


def gla(proj, small, alpha_up, alpha_b, head_g, *, batch, seq, q_col, k_col, v_col, gate_col, tb=512, hpb=4):
    t = batch * seq
    nb = seq // tb
    dk, dv, h = GLA_DK * hpb, GLA_DV * hpb, GLA_HEADS // hpb
    qb, kb, vb, gb = q_col // dk, k_col // dk, v_col // dv, gate_col // dv
    tok = lambda b, hh, i: b * nb + i
    nc = tb // GLA_CHUNK
    return pl.pallas_call(
        functools.partial(_gla_body, tb=tb, hpb=hpb),
        grid=(batch, h, nb),
        in_specs=[
            pl.BlockSpec((tb, dk), lambda b, hh, i: (tok(b, hh, i), qb + hh)),
            pl.BlockSpec((tb, dk), lambda b, hh, i: (tok(b, hh, i), kb + hh)),
            pl.BlockSpec((tb, dv), lambda b, hh, i: (tok(b, hh, i), vb + hh)),
            pl.BlockSpec((tb, 128), lambda b, hh, i: (tok(b, hh, i), 0)),
            pl.BlockSpec((GLA_LOWRANK, dk), lambda b, hh, i: (0, hh)),
            pl.BlockSpec((1, dk), lambda b, hh, i: (0, hh)),
            pl.BlockSpec((tb, dv), lambda b, hh, i: (tok(b, hh, i), gb + hh)),
            pl.BlockSpec((1, dv), lambda b, hh, i: (0, hh)),
        ],
        out_specs=pl.BlockSpec((tb, dv), lambda b, hh, i: (tok(b, hh, i), hh)),
        out_shape=jax.ShapeDtypeStruct((t, GLA_HEADS * GLA_DV), BF16),
        scratch_shapes=[pltpu.VMEM((hpb, GLA_DK, GLA_DV), F32), pltpu.VMEM((hpb, tb, GLA_DK), F32),
                        pltpu.VMEM((hpb, tb, GLA_DK), F32),
                        pltpu.VMEM((hpb, nc, GLA_DK, GLA_CHUNK), BF16),
                        pltpu.VMEM((hpb, nc, GLA_DK, 128), F32)],
        compiler_params=_cparams(("parallel", "parallel", "arbitrary")),
        name="gla",
    )(proj, proj, proj, small, alpha_up, alpha_b.reshape(1, -1), proj, head_g.reshape(1, -1))


FOX_GATE_UNROLL = 4


def _fox_gate_body(s_ref, fb_ref, qa_ref, ka_ref, cs_ref, off_ref, *, blk, col0):
    n = s_ref.shape[0] // blk
    r = lax.broadcasted_iota(jnp.int32, (blk, blk), 0)
    c = lax.broadcasted_iota(jnp.int32, (blk, blk), 1)
    tri = jnp.where(c <= r, 1.0, 0.0).astype(BF16)
    lanes = s_ref.shape[1]
    pr = lax.broadcasted_iota(jnp.int32, (3 * lanes, FOX_AUG), 0)
    pc = lax.broadcasted_iota(jnp.int32, (3 * lanes, FOX_AUG), 1)
    head, piece = pr % lanes - col0, pr // lanes
    is_head = (head >= 0) & (head < FOX_HEADS)
    sel_q = jnp.where(is_head & (pc == head * FOX_AUG_GROUP + piece), 1.0, 0.0).astype(BF16)
    sel_k = jnp.where(is_head & (pc == head * FOX_AUG_GROUP + 3 + piece), -1.0, 0.0).astype(BF16)
    lane = lax.broadcasted_iota(jnp.int32, (1, FOX_AUG), 1)
    used = lane < FOX_HEADS * FOX_AUG_GROUP
    one_q = jnp.where(used & (lane % FOX_AUG_GROUP >= 3) & (lane % FOX_AUG_GROUP < 6), 1.0, 0.0)
    one_k = jnp.where(used & (lane % FOX_AUG_GROUP < 3), 1.0, 0.0)

    def local(i, carry):
        r0 = pl.multiple_of(i * blk, blk)
        lf = _log_sigmoid(s_ref[pl.ds(r0, blk), :] + fb_ref[...])
        hi, mid, lo = _split3(lf)
        cs_ref[pl.ds(r0, blk), :] = _dot(tri, hi) + _dot(tri, mid) + _dot(tri, lo)
        return carry

    lax.fori_loop(0, n, local, 0, unroll=FOX_GATE_UNROLL)

    off = jnp.zeros((1, lanes), F32)
    for i in range(n):
        off_ref[i:i + 1, :] = off
        off = off + cs_ref[(i + 1) * blk - 1:(i + 1) * blk, :]

    def place(i, carry):
        r0 = pl.multiple_of(i * blk, blk)
        cs = cs_ref[pl.ds(r0, blk), :] + off_ref[pl.ds(i, 1), :]
        pieces = jnp.concatenate(_split3(cs * LOG2E), axis=1)
        qa_ref[pl.ds(r0, blk), :] = (_dot(pieces, sel_q) + one_q).astype(BF16)
        ka_ref[pl.ds(r0, blk), :] = (_dot(pieces, sel_k) + one_k).astype(BF16)
        return carry

    lax.fori_loop(0, n, place, 0, unroll=FOX_GATE_UNROLL)


def fox_gate(small, fb_row, *, batch, seq, col0, blk=256):
    shp = jax.ShapeDtypeStruct((batch * seq, FOX_AUG), BF16)
    spec = pl.BlockSpec((seq, FOX_AUG), lambda b: (b, 0))
    return pl.pallas_call(
        functools.partial(_fox_gate_body, blk=blk, col0=col0),
        grid=(batch,),
        in_specs=[pl.BlockSpec((seq, 128), lambda b: (b, 0)),
                  pl.BlockSpec((1, 128), lambda b: (0, 0))],
        out_specs=[spec, spec],
        out_shape=[shp, shp],
        scratch_shapes=[pltpu.VMEM((seq, 128), F32), pltpu.VMEM((seq // blk, 128), F32)],
        compiler_params=_cparams(("parallel",)),
        name="fox_gate",
    )(small, fb_row)


def _fox_body(it_ref, jt_ref, q_ref, qa_ref, k_ref, ka_ref, v_ref, gate_ref, o_ref,
              m_ref, l_ref, acc_ref, *, tq, tk, hpb):
    p = pl.program_id(2)
    i = it_ref[p]
    j = jt_ref[p]
    d = FOX_DH

    @pl.when(j == 0)
    def _():
        m_ref[...] = jnp.full_like(m_ref, -jnp.inf)
        l_ref[...] = jnp.zeros_like(l_ref)
        acc_ref[...] = jnp.zeros_like(acc_ref)

    aug_lane = lax.broadcasted_iota(jnp.int32, (1, FOX_AUG), 1)

    def scores(item, masked):
        hh, ks, qs = item
        mine = aug_lane // FOX_AUG_GROUP == pl.program_id(1) * hpb + hh
        zero = jnp.zeros((), BF16)
        q_aug = jnp.concatenate([q_ref[qs, hh * d:(hh + 1) * d], jnp.where(mine, qa_ref[qs, :], zero)], axis=1)
        k_aug = jnp.concatenate([k_ref[ks, hh * d:(hh + 1) * d], jnp.where(mine, ka_ref[ks, :], zero)], axis=1)
        st = _dot_nt(k_aug, q_aug)
        if masked:
            kr = lax.broadcasted_iota(jnp.int32, st.shape, 0) + ks.start
            qc = lax.broadcasted_iota(jnp.int32, st.shape, 1) + qs.start
            st = jnp.where(qc >= kr, st, -jnp.inf)
        return st

    def absorb(item, st):
        hh, ks, qs = item
        m_old = m_ref[hh, :, qs]
        m_new = jnp.maximum(m_old, jnp.max(st, axis=0, keepdims=True))
        alpha = jnp.exp2(m_old - m_new)
        pt = jnp.exp2(st - m_new)
        l_ref[hh, :, qs] = alpha * l_ref[hh, :, qs] + jnp.sum(pt, axis=0, keepdims=True)
        acc_ref[hh, :, qs] = (alpha * acc_ref[hh, :, qs]
                              + _dot_tn(v_ref[ks, hh * d:(hh + 1) * d], pt.astype(BF16)))
        m_ref[hh, :, qs] = m_new

    def block(masked, finish):
        full = slice(0, tq)
        halves = [(slice(0, tk // 2), full), (slice(tk // 2, tk), slice(tq // 2, tq))] if masked else [(slice(0, tk), full)]
        items = [(hh, ks, qs) for hh in range(hpb) for ks, qs in halves]
        st = scores(items[0], masked)
        for n, item in enumerate(items):
            st_next = scores(items[n + 1], masked) if n + 1 < len(items) else None
            absorb(item, st)
            hh = item[0]
            if finish and (n + 1 == len(items) or items[n + 1][0] != hh):
                o = (acc_ref[hh] / l_ref[hh]).T
                g = gate_ref[:, hh * d:(hh + 1) * d].astype(F32)
                o_ref[:, hh * d:(hh + 1) * d] = (o * _silu(g)).astype(o_ref.dtype)
            st = st_next

    @pl.when(j < i)
    def _():
        block(False, False)

    @pl.when(j == i)
    def _():
        block(True, True)


def fox(proj, qa, ka, *, batch, seq, q_col, k_col, v_col, gate_col, t=1024, hpb=8):
    tt = batch * seq
    nb = seq // t
    d, h = FOX_DH, FOX_HEADS
    w = hpb * d
    qb, kb, gb, vb = q_col // w, k_col // w, gate_col // w, v_col // w
    pairs = [(i, j) for i in range(nb) for j in range(i + 1)]
    it = jnp.array([p[0] for p in pairs], jnp.int32)
    jt = jnp.array([p[1] for p in pairs], jnp.int32)
    grid_spec = pltpu.PrefetchScalarGridSpec(
        num_scalar_prefetch=2,
        grid=(batch, h // hpb, len(pairs)),
        in_specs=[
            pl.BlockSpec((t, w), lambda b, hh, p, it, jt: (b * nb + it[p], qb + hh)),
            pl.BlockSpec((t, FOX_AUG), lambda b, hh, p, it, jt: (b * nb + it[p], 0)),
            pl.BlockSpec((t, w), lambda b, hh, p, it, jt: (b * nb + jt[p], kb + hh)),
            pl.BlockSpec((t, FOX_AUG), lambda b, hh, p, it, jt: (b * nb + jt[p], 0)),
            pl.BlockSpec((t, w), lambda b, hh, p, it, jt: (b * nb + jt[p], vb + hh)),
            pl.BlockSpec((t, w), lambda b, hh, p, it, jt: (b * nb + it[p], gb + hh)),
        ],
        out_specs=pl.BlockSpec((t, w), lambda b, hh, p, it, jt: (b * nb + it[p], hh)),
        scratch_shapes=[pltpu.VMEM((hpb, 1, t), F32), pltpu.VMEM((hpb, 1, t), F32), pltpu.VMEM((hpb, d, t), F32)],
    )
    return pl.pallas_call(
        functools.partial(_fox_body, tq=t, tk=t, hpb=hpb),
        grid_spec=grid_spec,
        out_shape=jax.ShapeDtypeStruct((tt, h * d), BF16),
        compiler_params=_cparams(("parallel", "parallel", "arbitrary")),
        name="fox",
    )(it, jt, proj, qa, proj, ka, proj, proj)


def _regroup_perm(tm, steps, to_grouped):
    r = lax.broadcasted_iota(jnp.int32, (tm, tm), 0)
    c = lax.broadcasted_iota(jnp.int32, (tm, tm), 1)
    nc = tm // steps
    src = (r % nc) * steps + r // nc if to_grouped else (r % steps) * nc + r // steps
    return jnp.where(c == src, 1.0, 0.0).astype(BF16)


OUTPROJ_PART_ROWS = 256


def _cast_weight_once(wb_ref, w_ref):
    @pl.when(pl.program_id(0) == 0)
    def _():
        for r0 in range(0, w_ref.shape[0], MM_SUB):
            wb_ref[r0:r0 + MM_SUB, :] = w_ref[r0:r0 + MM_SUB, :].astype(wb_ref.dtype)


def _outproj_mid_body(a1_ref, a2_ref, wf_ref, x_ref, g_ref, r_ref, h_ref, w_ref, *, steps):
    k1 = a1_ref.shape[1]
    tm, d = x_ref.shape
    tp = OUTPROJ_PART_ROWS
    nc = tp // steps
    _cast_weight_once(w_ref, wf_ref)
    parts = [slice(p * tp, (p + 1) * tp) for p in range(tm // tp)]
    accs = [x_ref[rows, :] + _dot(a1_ref[rows, :], w_ref[:k1, :]) + _dot(a2_ref[rows, :], w_ref[k1:, :])
            for rows in parts]
    perm = _regroup_perm(tp, steps, True)
    for p, rows in enumerate(parts):
        acc = accs[p]
        r_ref[rows, :] = acc
        normed = (acc * lax.rsqrt(jnp.mean(acc * acc, axis=-1, keepdims=True) + EPS) * g_ref[...]).astype(BF16)
        by_step = _dot(perm, normed).astype(h_ref.dtype)
        for s in range(steps):
            h_ref[p * nc:(p + 1) * nc, s * d:(s + 1) * d] = by_step[s * nc:(s + 1) * nc, :]


def outproj_mid(a1, a2, w, x, g, *, steps, tm=512, name="outproj_mid"):
    m, d = x.shape
    row = pl.BlockSpec((tm, d), lambda i: (i, 0))
    return pl.pallas_call(
        functools.partial(_outproj_mid_body, steps=steps),
        grid=(m // tm,),
        in_specs=[pl.BlockSpec((tm, a1.shape[1]), lambda i: (i, 0)),
                  pl.BlockSpec((tm, a2.shape[1]), lambda i: (i, 0)),
                  pl.BlockSpec(w.shape, lambda i: (0, 0), pipeline_mode=pl.Buffered(1)),
                  row,
                  pl.BlockSpec((1, d), lambda i: (0, 0))],
        out_specs=[row, pl.BlockSpec((tm // steps, steps * d), lambda i: (i, 0))],
        out_shape=[jax.ShapeDtypeStruct((m, d), F32), jax.ShapeDtypeStruct((m // steps, steps * d), BF16)],
        scratch_shapes=[pltpu.VMEM(w.shape, BF16)],
        compiler_params=_cparams(("arbitrary",)),
        name=name,
    )(a1, a2, w, x, g.reshape(1, d))


def _outproj_final_body(a_ref, wf_ref, x_ref, g_ref, o_ref, w_ref, *, steps):
    tm = x_ref.shape[0]
    e = w_ref.shape[0]
    tp = OUTPROJ_PART_ROWS
    nc = tp // steps
    _cast_weight_once(w_ref, wf_ref)
    perm = _regroup_perm(tp, steps, False)
    a_toks = []
    for p in range(tm // tp):
        by_step = jnp.concatenate([a_ref[p * nc:(p + 1) * nc, s * e:(s + 1) * e] for s in range(steps)],
                                  axis=0)
        a_toks.append(_dot(perm, by_step).astype(BF16))
    parts = [slice(p * tp, (p + 1) * tp) for p in range(tm // tp)]
    accs = [x_ref[rows, :] + _dot(a_toks[p], w_ref[...]) for p, rows in enumerate(parts)]
    for p, rows in enumerate(parts):
        acc = accs[p]
        o_ref[rows, :] = acc * lax.rsqrt(jnp.mean(acc * acc, axis=-1, keepdims=True) + EPS) * g_ref[...]


def outproj_final(a_grouped, w, x, g, *, steps, tm=512, name="outproj_final"):
    m, d = x.shape
    e = w.shape[0]
    row = pl.BlockSpec((tm, d), lambda i: (i, 0))
    return pl.pallas_call(
        functools.partial(_outproj_final_body, steps=steps),
        grid=(m // tm,),
        in_specs=[pl.BlockSpec((tm // steps, steps * e), lambda i: (i, 0)),
                  pl.BlockSpec(w.shape, lambda i: (0, 0), pipeline_mode=pl.Buffered(1)),
                  row,
                  pl.BlockSpec((1, d), lambda i: (0, 0))],
        out_specs=row,
        out_shape=jax.ShapeDtypeStruct((m, d), F32),
        scratch_shapes=[pltpu.VMEM(w.shape, BF16)],
        compiler_params=_cparams(("arbitrary",)),
        name=name,
    )(a_grouped, w, x, g.reshape(1, d))


S5_SCAN_BLK = 8
S5_GPB = 8


def _s5_entering_state(x, arow, *, cols_per_seq, n_levels):
    n = x.shape[1]
    p = S5_STATE
    nseq = n // cols_per_seq

    def to_rows(part):
        return jnp.concatenate([part[:, s * cols_per_seq:(s + 1) * cols_per_seq] for s in range(nseq)], axis=0).T

    def to_cols(rows):
        rt = rows.T
        return jnp.concatenate([rt[s * p:(s + 1) * p, :] for s in range(nseq)], axis=1)

    xr, xi = to_rows(x[:p, :]), to_rows(x[p:, :])
    pos = lax.broadcasted_iota(jnp.int32, (cols_per_seq, 1), 0)
    blk = S5_SCAN_BLK
    half = arow.shape[0] // 2
    for lvl in range(n_levels):
        sh = 1 << lvl
        keep = pos % blk >= sh
        sr = jnp.where(keep, pltpu.roll(xr, sh, axis=0), 0.0)
        si = jnp.where(keep, pltpu.roll(xi, sh, axis=0), 0.0)
        ar, ai = arow[lvl:lvl + 1, :], arow[half + lvl:half + lvl + 1, :]
        xr, xi = xr + ar * sr - ai * si, xi + ar * si + ai * sr
    pw_r, pw_i = arow[n_levels:n_levels + blk, :], arow[half + n_levels:half + n_levels + blk, :]
    out_r, out_i = [xr[:blk, :]], [xi[:blk, :]]
    for r0 in range(blk, cols_per_seq, blk):
        cr, ci = out_r[-1][blk - 1:blk, :], out_i[-1][blk - 1:blk, :]
        out_r.append(xr[r0:r0 + blk, :] + pw_r * cr - pw_i * ci)
        out_i.append(xi[r0:r0 + blk, :] + pw_r * ci + pw_i * cr)
    xr, xi = jnp.concatenate(out_r, axis=0), jnp.concatenate(out_i, axis=0)
    keep = pos >= 1
    pr = jnp.where(keep, pltpu.roll(xr, 1, axis=0), 0.0)
    pi = jnp.where(keep, pltpu.roll(xi, 1, axis=0), 0.0)
    return jnp.concatenate([to_cols(pr), to_cols(pi)], axis=0).astype(BF16)


def _s5_body(u_ref, wt_ref, nt_ref, mt_ref, arow_ref, o_ref, *, cols_per_seq, n_levels):
    t, _, n = u_ref.shape
    cg = S5_GROUP_CH
    groups = range(S5_GPB)
    ch = [slice(gi * cg, (gi + 1) * cg) for gi in groups]
    us = [u_ref[:, ch[gi], :].reshape(t * cg, n) for gi in groups]
    xs = [_dot(nt_ref[gi], us[gi]) for gi in groups]
    ys = [_dot(wt_ref[gi], us[gi]) for gi in groups]
    prevs = [_s5_entering_state(xs[gi], arow_ref[gi], cols_per_seq=cols_per_seq, n_levels=n_levels)
             for gi in groups]
    for gi in groups:
        y = ys[gi] + _dot(mt_ref[gi], prevs[gi])
        o_ref[:, ch[gi], :] = y.astype(o_ref.dtype).reshape(t, cg, n)


def s5_apply(proj_t, wt_g, nt_g, mt_g, arow, *, groups, cols_per_seq, n_levels):
    t, _, n = proj_t.shape
    cg = S5_GROUP_CH * S5_GPB
    spec3 = lambda shp: pl.BlockSpec((S5_GPB,) + shp, lambda i: (i, 0, 0))
    return pl.pallas_call(
        functools.partial(_s5_body, cols_per_seq=cols_per_seq, n_levels=n_levels),
        grid=(groups // S5_GPB,),
        in_specs=[pl.BlockSpec((t, cg, n), lambda i: (0, i, 0)),
                  spec3(wt_g.shape[1:]), spec3(nt_g.shape[1:]), spec3(mt_g.shape[1:]), spec3(arow.shape[1:])],
        out_specs=pl.BlockSpec((t, cg, n), lambda i: (0, i, 0)),
        out_shape=jax.ShapeDtypeStruct((t, groups * S5_GROUP_CH, n), BF16),
        compiler_params=_cparams(("parallel",)),
        name="s5",
    )(proj_t, wt_g, nt_g, mt_g, arow)


def s5_operators(lam_re, lam_im, log_step, b_re, b_im, c_re, c_im, d, *, n_levels, n_seq):
    hp = lax.Precision.HIGH
    g, p = lam_re.shape
    cg, t = S5_GROUP_CH, S5_T
    lr = jnp.minimum(lam_re.astype(F32), -1e-4)
    li = lam_im.astype(F32)
    step = jnp.exp(log_step.astype(F32))[:, None]
    mag = jnp.exp(lr * step)
    lb_re, lb_im = mag * jnp.cos(li * step), mag * jnp.sin(li * step)
    den = lr * lr + li * li
    nr, ni = lb_re - 1.0, lb_im
    coef_re = (nr * lr + ni * li) / den
    coef_im = (ni * lr - nr * li) / den
    bb_re = coef_re[..., None] * b_re - coef_im[..., None] * b_im
    bb_im = coef_re[..., None] * b_im + coef_im[..., None] * b_re

    def power(m):
        mm = m.astype(F32)[None, :, None]
        mg = jnp.exp(lr[:, None, :] * step[:, None, :] * mm)
        ang = li[:, None, :] * step[:, None, :] * mm
        return mg * jnp.cos(ang), mg * jnp.sin(ang)

    pw_re, pw_im = power(jnp.arange(t + 1))
    pwt_re, pwt_im = pw_re.transpose(0, 2, 1), pw_im.transpose(0, 2, 1)

    def pow_times_b(pr, pi):
        re = pr[:, :, :, None] * bb_re[:, :, None, :] - pi[:, :, :, None] * bb_im[:, :, None, :]
        im = pr[:, :, :, None] * bb_im[:, :, None, :] + pi[:, :, :, None] * bb_re[:, :, None, :]
        return re.reshape(g, p, t * cg), im.reshape(g, p, t * cg)

    pb_re, pb_im = pow_times_b(pwt_re[:, :, :t], pwt_im[:, :, :t])
    taps = (jnp.einsum('gcp,gpy->gcy', c_re.astype(F32), pb_re, precision=hp)
            - jnp.einsum('gcp,gpy->gcy', c_im.astype(F32), pb_im, precision=hp))
    y_idx = jnp.arange(t * cg)
    taps = taps + d.astype(F32).reshape(g, cg, 1) * (y_idx[None, :] == jnp.arange(cg)[:, None]).astype(F32)[None]
    place = ((y_idx[None, :, None] // cg == jnp.arange(t)[:, None, None] - y_idx[None, None, :] // cg)
             & (y_idx[None, :, None] % cg == y_idx[None, None, :] % cg)).astype(BF16)
    wt_g = jnp.einsum('gcy,tyz->gtcz', taps.astype(BF16), place,
                      preferred_element_type=F32).astype(BF16).reshape(g, t * cg, t * cg)
    n_re, n_im = pow_times_b(pwt_re[:, :, t - 1::-1], pwt_im[:, :, t - 1::-1])
    nt_g = jnp.concatenate([n_re, n_im], axis=1).astype(BF16)
    cp_re = c_re[:, None] * pw_re[:, 1:, None, :] - c_im[:, None] * pw_im[:, 1:, None, :]
    cp_im = c_re[:, None] * pw_im[:, 1:, None, :] + c_im[:, None] * pw_re[:, 1:, None, :]
    mt_g = jnp.concatenate([cp_re, -cp_im], axis=-1).reshape(g, t * cg, 2 * p).astype(BF16)
    ar, ai = power(t * jnp.concatenate([2 ** jnp.arange(n_levels), jnp.arange(1, S5_SCAN_BLK + 1)]))
    arow = jnp.concatenate([jnp.tile(ar, (1, 1, n_seq)), jnp.tile(ai, (1, 1, n_seq))], axis=1)
    return wt_g, nt_g, mt_g, arow


def _gelu_tanh(y):
    return 0.5 * y * (1.0 + jnp.tanh(math.sqrt(2.0 / math.pi) * (y + 0.044715 * (y * y * y))))


GLU_PARTS = 2


def _glu_body(y_ref, w_ref, b_ref, gate_ref, o_ref, z_ref, zb_ref, wt_ref):
    @pl.when((pl.program_id(0) == 0) & (pl.program_id(1) == 0))
    def _():
        _transpose_into(wt_ref, w_ref)

    tm = y_ref.shape[1]
    parts = [slice(c0, c0 + tm // GLU_PARTS) for c0 in range(0, tm, tm // GLU_PARTS)]
    for cols in parts:
        z = _gelu_tanh(y_ref[:, cols].astype(F32))
        z_ref[:, cols] = z
        zb_ref[:, cols] = z.astype(BF16)
    for cols in parts:
        for f0 in range(0, o_ref.shape[1], MM_SUB):
            rows = slice(f0, f0 + MM_SUB)
            lin = _dot(wt_ref[rows, :], zb_ref[:, cols]) + b_ref[rows, :]
            out = z_ref[rows, cols] * _sigmoid(lin) * _silu(gate_ref[rows, cols].astype(F32))
            o_ref[cols, rows] = out.T.astype(o_ref.dtype)


def glu_gate(y_t, w, b, proj_t, *, gate_row, tm=512):
    s, e, n = y_t.shape
    gb = gate_row // e
    return pl.pallas_call(
        _glu_body,
        grid=(s, n // tm),
        in_specs=[pl.BlockSpec((None, e, tm), lambda t, i: (t, 0, i)),
                  pl.BlockSpec((e, e), lambda t, i: (0, 0), pipeline_mode=pl.Buffered(1)),
                  pl.BlockSpec((e, 1), lambda t, i: (0, 0), pipeline_mode=pl.Buffered(1)),
                  pl.BlockSpec((None, e, tm), lambda t, i: (t, gb, i))],
        out_specs=pl.BlockSpec((tm, e), lambda t, i: (i, t)),
        out_shape=jax.ShapeDtypeStruct((n, s * e), BF16),
        scratch_shapes=[pltpu.VMEM((e, tm), F32), pltpu.VMEM((e, tm), BF16), pltpu.VMEM((e, e), BF16)],
        compiler_params=_cparams(("arbitrary", "arbitrary")),
        name="glu",
    )(y_t, w, b.reshape(e, 1), proj_t)


def kernel(x, ab_norm_g, ab_w_in, gla_alpha_up, gla_alpha_b, gla_head_g, fox_f_b, ab_w_out, c_norm_g, c_w_in, s5_lambda_re, s5_lambda_im, s5_log_step, s5_b_re, s5_b_im, s5_c_re, s5_c_im, s5_d, glu_w, glu_b, c_w_out, final_norm_g):
    batch, seq, d = x.shape
    t = batch * seq
    x2 = x.reshape(t, d)

    hk, hv, fw = GLA_HEADS * GLA_DK, GLA_HEADS * GLA_DV, FOX_HEADS * FOX_DH
    sizes = (hk, hk, hv, GLA_LOWRANK, hv, fw, fw, fw, FOX_HEADS, fw)
    offs = [0]
    for s in sizes:
        offs.append(offs[-1] + s)
    w_gla, w_fox, w_small = ab_weight_prep(jnp.swapaxes(ab_w_in, 1, 2), offs)
    c_gq, c_gk, c_gv, c_gg = 0, hk, 2 * hk, 2 * hk + hv
    c_fq, c_fk, c_fv, c_fg = 0, fw, 2 * fw, 3 * fw

    h0, proj_g, small = norm_proj(x2, ab_norm_g[0], w_gla, w_small, name="ab_in")
    proj_f = matmul(h0, w_fox, name="ab_in_fox")

    o_gla = gla(proj_g, small, gla_alpha_up[0], gla_alpha_b[0], gla_head_g[0],
                batch=batch, seq=seq, q_col=c_gq, k_col=c_gk, v_col=c_gv, gate_col=c_gg)

    fb_row = jnp.zeros((1, 128), F32).at[0, GLA_LOWRANK:GLA_LOWRANK + FOX_HEADS].set(fox_f_b[0])
    qa, ka = fox_gate(small, fb_row, batch=batch, seq=seq, col0=GLA_LOWRANK)
    o_fox = fox(proj_f, qa, ka, batch=batch, seq=seq, q_col=c_fq, k_col=c_fk, v_col=c_fv, gate_col=c_fg)

    x1, h1g = outproj_mid(o_gla, o_fox, ab_w_out[0], x2, c_norm_g[0], steps=S5_T, name="ab_out")

    e = d
    groups = e // S5_GROUP_CH
    n_chunks = t // S5_T
    cols_per_seq = seq // S5_T
    n_levels = S5_SCAN_BLK.bit_length() - 1
    proj_t = matmul_t_grouped(c_w_in[0], h1g, S5_T, BF16, name="c_in")
    ops = s5_operators(s5_lambda_re[0], s5_lambda_im[0], s5_log_step[0], s5_b_re[0], s5_b_im[0],
                       s5_c_re[0], s5_c_im[0], s5_d[0], n_levels=n_levels, n_seq=batch)
    y_t = s5_apply(proj_t, *ops, groups=groups, cols_per_seq=cols_per_seq, n_levels=n_levels)
    zz = glu_gate(y_t, glu_w[0], glu_b[0], proj_t, gate_row=e)
    out = outproj_final(zz, c_w_out[0], x1, final_norm_g, steps=S5_T, name="c_out")
    return out.reshape(batch, seq, d)
```

```python
import functools
import math

import jax
import jax.numpy as jnp
from jax import lax
from jax.experimental import pallas as pl
from jax.experimental.pallas import tpu as pltpu

EPS = 1e-6
F32 = jnp.float32
BF16 = jnp.bfloat16

GLA_HEADS = 4
GLA_DK = 128
GLA_DV = 256
GLA_LOWRANK = 16
GLA_TAU = 16.0
GLA_CHUNK = 64
GLA_SUB = 8
GLA_UNROLL = 8
FOX_HEADS = 8
FOX_DH = 128
FOX_AUG = 128
FOX_AUG_GROUP = 8
S5_GROUP_CH = 16
S5_STATE = 64
S5_T = 16

VMEM_LIMIT = 58 * 1024 * 1024

NT_DIMS = (((1,), (1,)), ((), ()))
TN_DIMS = (((0,), (0,)), ((), ()))
LOG2E = math.log2(math.e)


def _cparams(sem):
    return pltpu.CompilerParams(dimension_semantics=sem, vmem_limit_bytes=VMEM_LIMIT)


def _dot(a, b):
    return jnp.dot(a, b, preferred_element_type=F32)


def _dot_nt(a, b):
    return lax.dot_general(a, b, NT_DIMS, preferred_element_type=F32)


def _dot_tn(a, b):
    return lax.dot_general(a, b, TN_DIMS, preferred_element_type=F32)


def _log_sigmoid(z):
    return jnp.minimum(z, 0.0) - jnp.log(1.0 + jnp.exp(-jnp.abs(z)))


def _sigmoid(z):
    return 0.5 * jnp.tanh(0.5 * z) + 0.5


def _silu(z):
    return z * _sigmoid(z)


def _split3(x):
    hi = x.astype(BF16)
    r1 = x - hi.astype(F32)
    mid = r1.astype(BF16)
    lo = (r1 - mid.astype(F32)).astype(BF16)
    return hi, mid, lo


NORM_PART_ROWS = 256


def _norm_proj_body(x_ref, g_ref, wt_ref, ws_ref, h_ref, o_ref, s_ref):
    tm = x_ref.shape[0]
    parts = [slice(r0, r0 + NORM_PART_ROWS) for r0 in range(0, tm, NORM_PART_ROWS)]
    hs = []
    for rows in parts:
        x = x_ref[rows, :]
        ms = jnp.mean(x * x, axis=-1, keepdims=True)
        h = (x * lax.rsqrt(ms + EPS) * g_ref[...]).astype(BF16)
        h_ref[rows, :] = h
        hs.append(h)
    for rows, h in zip(parts, hs):
        for c0 in range(0, o_ref.shape[1], MM_SUB):
            o_ref[rows, c0:c0 + MM_SUB] = _dot_nt(h, wt_ref[c0:c0 + MM_SUB, :]).astype(o_ref.dtype)
        s_ref[rows, :] = _dot_nt(h, ws_ref[...])


def norm_proj(x, g, wt, wt_small, tm=512, name="norm_proj"):
    m, d = x.shape
    n = wt.shape[0]
    ns = wt_small.shape[0]
    return pl.pallas_call(
        _norm_proj_body,
        grid=(m // tm,),
        in_specs=[pl.BlockSpec((tm, d), lambda i: (i, 0)),
                  pl.BlockSpec((1, d), lambda i: (0, 0)),
                  pl.BlockSpec((n, d), lambda i: (0, 0), pipeline_mode=pl.Buffered(1)),
                  pl.BlockSpec((ns, d), lambda i: (0, 0))],
        out_specs=[pl.BlockSpec((tm, d), lambda i: (i, 0)),
                   pl.BlockSpec((tm, n), lambda i: (i, 0)),
                   pl.BlockSpec((tm, ns), lambda i: (i, 0))],
        out_shape=[jax.ShapeDtypeStruct((m, d), BF16), jax.ShapeDtypeStruct((m, n), BF16),
                   jax.ShapeDtypeStruct((m, ns), F32)],
        compiler_params=_cparams(("parallel",)),
        name=name,
    )(x, g.reshape(1, d), wt, wt_small)


def _mm_body(a_ref, wt_ref, o_ref):
    a = a_ref[...]
    for c0 in range(0, o_ref.shape[1], MM_SUB):
        o_ref[:, c0:c0 + MM_SUB] = _dot_nt(a, wt_ref[c0:c0 + MM_SUB, :]).astype(o_ref.dtype)


def matmul(a, wt, tm=512, name="matmul"):
    m, k = a.shape
    n = wt.shape[0]
    return pl.pallas_call(
        _mm_body,
        grid=(m // tm,),
        in_specs=[pl.BlockSpec((tm, k), lambda i: (i, 0)),
                  pl.BlockSpec((n, k), lambda i: (0, 0), pipeline_mode=pl.Buffered(1))],
        out_specs=pl.BlockSpec((tm, n), lambda i: (i, 0)),
        out_shape=jax.ShapeDtypeStruct((m, n), BF16),
        compiler_params=_cparams(("parallel",)),
        name=name,
    )(a, wt)


def _ab_wprep_body(w_ref, og_ref, of_ref, s_ref, *, offs):
    w = w_ref[0]
    seg = lambda n: w[offs[n]:offs[n + 1], :]
    og_ref[...] = jnp.concatenate([w[:offs[3], :], seg(4)], axis=0).astype(og_ref.dtype)
    of_ref[...] = jnp.concatenate([seg(5) * (FOX_DH ** -0.5 * LOG2E), w[offs[6]:offs[8], :], seg(9)],
                                  axis=0).astype(of_ref.dtype)
    pad = s_ref.shape[0] - (offs[4] - offs[3]) - (offs[9] - offs[8])
    s_ref[...] = jnp.concatenate([seg(3), seg(8), jnp.zeros((pad, w.shape[1]), F32)], axis=0).astype(s_ref.dtype)


def ab_weight_prep(w_t3, offs, tk=512):
    _, n_in, d = w_t3.shape
    n_gla = offs[3] + offs[5] - offs[4]
    n_fox = offs[8] - offs[5] + offs[10] - offs[9]
    return pl.pallas_call(
        functools.partial(_ab_wprep_body, offs=tuple(offs)),
        grid=(d // tk,),
        in_specs=[pl.BlockSpec((1, n_in, tk), lambda i: (0, 0, i))],
        out_specs=[pl.BlockSpec((n_gla, tk), lambda i: (0, i)), pl.BlockSpec((n_fox, tk), lambda i: (0, i)),
                   pl.BlockSpec((128, tk), lambda i: (0, i))],
        out_shape=[jax.ShapeDtypeStruct((n_gla, d), BF16), jax.ShapeDtypeStruct((n_fox, d), BF16),
                   jax.ShapeDtypeStruct((128, d), BF16)],
        compiler_params=_cparams(("parallel",)),
        name="ab_wprep",
    )(w_t3)


MM_SUB = 512


def _transpose_into(wt_ref, w_ref):
    for c0 in range(0, w_ref.shape[1], MM_SUB):
        wt_ref[c0:c0 + MM_SUB, :] = w_ref[:, c0:c0 + MM_SUB].astype(wt_ref.dtype).T


def _mm_t_body(w_ref, a_ref, o_ref, wt_ref):
    @pl.when((pl.program_id(1) == 0) & (pl.program_id(2) == 0))
    def _():
        _transpose_into(wt_ref, w_ref)

    a = a_ref[...]
    for r0 in range(0, o_ref.shape[0], MM_SUB):
        o_ref[r0:r0 + MM_SUB, :] = _dot_nt(wt_ref[r0:r0 + MM_SUB, :], a).astype(o_ref.dtype)


def matmul_t_grouped(w, a2, steps, out_dtype, tm=512, tn=2048, name="matmul_tg"):
    k, nf = w.shape
    n = a2.shape[0]
    return pl.pallas_call(
        _mm_t_body,
        grid=(nf // tn, steps, n // tm),
        in_specs=[pl.BlockSpec((k, tn), lambda j, t, i: (0, j), pipeline_mode=pl.Buffered(1)),
                  pl.BlockSpec((tm, k), lambda j, t, i: (i, t))],
        out_specs=pl.BlockSpec((None, tn, tm), lambda j, t, i: (t, j, i)),
        out_shape=jax.ShapeDtypeStruct((steps, nf, n), out_dtype),
        scratch_shapes=[pltpu.VMEM((tn, k), BF16)],
        compiler_params=_cparams(("arbitrary", "arbitrary", "arbitrary")),
        name=name,
    )(w, a2)


def _gla_body(q_ref, k_ref, v_ref, glr_ref, aup_ref, ab_ref, gate_ref, hg_ref, o_ref,
              s_ref, b_ref, kf_ref, kts_ref, dec_ref, *, tb, hpb):
    ib = pl.program_id(2)
    C, SB, dk, dv = GLA_CHUNK, GLA_SUB, GLA_DK, GLA_DV

    @pl.when(ib == 0)
    def _():
        s_ref[...] = jnp.zeros_like(s_ref)

    glr = glr_ref[:, :GLA_LOWRANK].astype(BF16)
    tw = 2 * C
    r = lax.broadcasted_iota(jnp.int32, (tw, tw), 0)
    c = lax.broadcasted_iota(jnp.int32, (tw, tw), 1)
    tri = jnp.where((r // C == c // C) & (c <= r), 1.0, 0.0).astype(BF16)
    nc = tb // C
    heads = range(hpb)
    kcs = [slice(hh * dk, (hh + 1) * dk) for hh in heads]
    zs = [_dot(glr, aup_ref[:, kcs[hh]].astype(BF16)) + ab_ref[:, kcs[hh]] for hh in heads]
    b_blks = []
    for hh in heads:
        la = _log_sigmoid(zs[hh]) * (1.0 / GLA_TAU)
        hi = la.astype(BF16)
        lo = (la - hi.astype(F32)).astype(BF16)
        b_blks.append(jnp.concatenate(
            [_dot(tri, hi[r0:r0 + tw]) + _dot(tri, lo[r0:r0 + tw]) for r0 in range(0, tb, tw)], axis=0))
    for hh in heads:
        b_blk = b_blks[hh]
        k_blk = k_ref[:, kcs[hh]].astype(F32)
        b_ref[hh] = b_blk
        kf_ref[hh] = k_blk
        b3 = b_blk.reshape(nc, C, dk)
        b_end = b3[:, C - 1:C, :]
        k_end = (k_blk.reshape(nc, C, dk) * jnp.exp(b_end - b3)).reshape(tb, dk)
        kt_end = k_end.T.astype(BF16)
        dec_t = jnp.exp(b_end.reshape(nc, dk)).T
        for ci in range(nc):
            kts_ref[hh, ci] = kt_end[:, ci * C:(ci + 1) * C]
            dec_ref[hh, ci] = jnp.broadcast_to(dec_t[:, ci:ci + 1], (dk, 128))

    row = lax.broadcasted_iota(jnp.int32, (C, 1), 0)
    rowi = lax.broadcasted_iota(jnp.int32, (C, C), 0)
    coli = lax.broadcasted_iota(jnp.int32, (C, C), 1)
    lane_c = lax.broadcasted_iota(jnp.int32, (SB, C), 1)
    sub_r = lax.broadcasted_iota(jnp.int32, (SB, 1), 0)
    ones = jnp.ones((GLA_DK, C), BF16)
    neg = -jnp.inf
    scale = GLA_DK ** -0.5

    def stage_state(ci, r0, hh):
        kc = slice(hh * dk, (hh + 1) * dk)
        vc = slice(hh * dv, (hh + 1) * dv)
        q = q_ref[pl.ds(r0, C), kc].astype(F32) * scale
        k = kf_ref[hh, pl.ds(r0, C), :]
        b = b_ref[hh, pl.ds(r0, C), :]
        v = v_ref[pl.ds(r0, C), vc]

        s = s_ref[hh]
        o = _dot((q * jnp.exp(b)).astype(BF16), s.astype(BF16))
        dec = dec_ref[hh, ci]
        s_ref[hh] = s * jnp.concatenate([dec] * (dv // 128), axis=1) + _dot(kts_ref[hh, ci], v)

        attn = jnp.zeros((C, C), F32)
        h = C // 2
        while h >= SB:
            ref = jnp.broadcast_to(b.reshape(C // (2 * h), 2 * h, GLA_DK)[:, h - 1:h, :],
                                   (C // (2 * h), 2 * h, GLA_DK)).reshape(C, GLA_DK)
            upper = (row // h) % 2 == 1
            q_h = q * jnp.exp(jnp.where(upper, b - ref, neg))
            k_h = k * jnp.exp(jnp.where(upper, neg, ref - b))
            a_h = _dot_nt(q_h.astype(BF16), k_h.astype(BF16))
            attn = attn + (a_h if 2 * h == C else jnp.where(rowi // (2 * h) == coli // (2 * h), a_h, 0.0))
            h //= 2
        return q, b, v, o, attn

    def stage_diag(r0, hh, q, b):
        zs = []
        for bi in range(C // SB):
            s0 = bi * SB
            q_i = q[s0:s0 + SB, :]
            b_i = b[s0:s0 + SB, :]
            for j in range(SB):
                k_j = kf_ref[hh, pl.ds(r0 + (s0 + j), 1), :]
                b_j = b_ref[hh, pl.ds(r0 + (s0 + j), 1), :]
                zs.append(q_i * k_j * jnp.exp(jnp.where(sub_r >= j, b_i - b_j, neg)))
        return _dot(jnp.concatenate(zs, axis=0).astype(BF16), ones)

    def stage_out(r0, hh, v, o, attn, zsum):
        vc = slice(hh * dv, (hh + 1) * dv)
        diag = []
        for bi in range(C // SB):
            acc = jnp.zeros((SB, C), F32)
            for j in range(SB):
                n0 = (bi * SB + j) * SB
                acc = acc + jnp.where(lane_c == bi * SB + j, zsum[n0:n0 + SB, :], 0.0)
            diag.append(acc)
        attn = attn + jnp.concatenate(diag, axis=0)
        o = o + _dot(attn.astype(BF16), v)

        o = o * lax.rsqrt(jnp.mean(o * o, axis=-1, keepdims=True) + EPS) * hg_ref[:, vc]
        g = gate_ref[pl.ds(r0, C), vc].astype(F32)
        o_ref[pl.ds(r0, C), vc] = (o * _silu(g)).astype(o_ref.dtype)

    def chunk(ci, carry):
        r0 = pl.multiple_of(ci * C, C)
        heads = range(hpb)
        st = [stage_state(ci, r0, hh) for hh in heads]
        zsums = [stage_diag(r0, hh, st[hh][0], st[hh][1]) for hh in heads]
        for hh in heads:
            _, _, v, o, attn = st[hh]
            stage_out(r0, hh, v, o, attn, zsums[hh])
        return carry

    lax.fori_loop(0, tb // C, chunk, 0, unroll=GLA_UNROLL)


def gla(proj, small, alpha_up, alpha_b, head_g, *, batch, seq, q_col, k_col, v_col, gate_col, tb=512, hpb=4):
    t = batch * seq
    nb = seq // tb
    dk, dv, h = GLA_DK * hpb, GLA_DV * hpb, GLA_HEADS // hpb
    qb, kb, vb, gb = q_col // dk, k_col // dk, v_col // dv, gate_col // dv
    tok = lambda b, hh, i: b * nb + i
    nc = tb // GLA_CHUNK
    return pl.pallas_call(
        functools.partial(_gla_body, tb=tb, hpb=hpb),
        grid=(batch, h, nb),
        in_specs=[
            pl.BlockSpec((tb, dk), lambda b, hh, i: (tok(b, hh, i), qb + hh)),
            pl.BlockSpec((tb, dk), lambda b, hh, i: (tok(b, hh, i), kb + hh)),
            pl.BlockSpec((tb, dv), lambda b, hh, i: (tok(b, hh, i), vb + hh)),
            pl.BlockSpec((tb, 128), lambda b, hh, i: (tok(b, hh, i), 0)),
            pl.BlockSpec((GLA_LOWRANK, dk), lambda b, hh, i: (0, hh)),
            pl.BlockSpec((1, dk), lambda b, hh, i: (0, hh)),
            pl.BlockSpec((tb, dv), lambda b, hh, i: (tok(b, hh, i), gb + hh)),
            pl.BlockSpec((1, dv), lambda b, hh, i: (0, hh)),
        ],
        out_specs=pl.BlockSpec((tb, dv), lambda b, hh, i: (tok(b, hh, i), hh)),
        out_shape=jax.ShapeDtypeStruct((t, GLA_HEADS * GLA_DV), BF16),
        scratch_shapes=[pltpu.VMEM((hpb, GLA_DK, GLA_DV), F32), pltpu.VMEM((hpb, tb, GLA_DK), F32),
                        pltpu.VMEM((hpb, tb, GLA_DK), F32),
                        pltpu.VMEM((hpb, nc, GLA_DK, GLA_CHUNK), BF16),
                        pltpu.VMEM((hpb, nc, GLA_DK, 128), F32)],
        compiler_params=_cparams(("parallel", "parallel", "arbitrary")),
        name="gla",
    )(proj, proj, proj, small, alpha_up, alpha_b.reshape(1, -1), proj, head_g.reshape(1, -1))


FOX_GATE_UNROLL = 4


def _fox_gate_body(s_ref, fb_ref, qa_ref, ka_ref, cs_ref, off_ref, *, blk, col0):
    n = s_ref.shape[0] // blk
    r = lax.broadcasted_iota(jnp.int32, (blk, blk), 0)
    c = lax.broadcasted_iota(jnp.int32, (blk, blk), 1)
    tri = jnp.where(c <= r, 1.0, 0.0).astype(BF16)
    lanes = s_ref.shape[1]
    pr = lax.broadcasted_iota(jnp.int32, (3 * lanes, FOX_AUG), 0)
    pc = lax.broadcasted_iota(jnp.int32, (3 * lanes, FOX_AUG), 1)
    head, piece = pr % lanes - col0, pr // lanes
    is_head = (head >= 0) & (head < FOX_HEADS)
    sel_q = jnp.where(is_head & (pc == head * FOX_AUG_GROUP + piece), 1.0, 0.0).astype(BF16)
    sel_k = jnp.where(is_head & (pc == head * FOX_AUG_GROUP + 3 + piece), -1.0, 0.0).astype(BF16)
    lane = lax.broadcasted_iota(jnp.int32, (1, FOX_AUG), 1)
    used = lane < FOX_HEADS * FOX_AUG_GROUP
    one_q = jnp.where(used & (lane % FOX_AUG_GROUP >= 3) & (lane % FOX_AUG_GROUP < 6), 1.0, 0.0)
    one_k = jnp.where(used & (lane % FOX_AUG_GROUP < 3), 1.0, 0.0)

    def local(i, carry):
        r0 = pl.multiple_of(i * blk, blk)
        lf = _log_sigmoid(s_ref[pl.ds(r0, blk), :] + fb_ref[...])
        hi, mid, lo = _split3(lf)
        cs_ref[pl.ds(r0, blk), :] = _dot(tri, hi) + _dot(tri, mid) + _dot(tri, lo)
        return carry

    lax.fori_loop(0, n, local, 0, unroll=FOX_GATE_UNROLL)

    off = jnp.zeros((1, lanes), F32)
    for i in range(n):
        off_ref[i:i + 1, :] = off
        off = off + cs_ref[(i + 1) * blk - 1:(i + 1) * blk, :]

    def place(i, carry):
        r0 = pl.multiple_of(i * blk, blk)
        cs = cs_ref[pl.ds(r0, blk), :] + off_ref[pl.ds(i, 1), :]
        pieces = jnp.concatenate(_split3(cs * LOG2E), axis=1)
        qa_ref[pl.ds(r0, blk), :] = (_dot(pieces, sel_q) + one_q).astype(BF16)
        ka_ref[pl.ds(r0, blk), :] = (_dot(pieces, sel_k) + one_k).astype(BF16)
        return carry

    lax.fori_loop(0, n, place, 0, unroll=FOX_GATE_UNROLL)


def fox_gate(small, fb_row, *, batch, seq, col0, blk=256):
    shp = jax.ShapeDtypeStruct((batch * seq, FOX_AUG), BF16)
    spec = pl.BlockSpec((seq, FOX_AUG), lambda b: (b, 0))
    return pl.pallas_call(
        functools.partial(_fox_gate_body, blk=blk, col0=col0),
        grid=(batch,),
        in_specs=[pl.BlockSpec((seq, 128), lambda b: (b, 0)),
                  pl.BlockSpec((1, 128), lambda b: (0, 0))],
        out_specs=[spec, spec],
        out_shape=[shp, shp],
        scratch_shapes=[pltpu.VMEM((seq, 128), F32), pltpu.VMEM((seq // blk, 128), F32)],
        compiler_params=_cparams(("parallel",)),
        name="fox_gate",
    )(small, fb_row)


def _fox_body(it_ref, jt_ref, q_ref, qa_ref, k_ref, ka_ref, v_ref, gate_ref, o_ref,
              m_ref, l_ref, acc_ref, *, tq, tk, hpb):
    p = pl.program_id(2)
    i = it_ref[p]
    j = jt_ref[p]
    d = FOX_DH

    @pl.when(j == 0)
    def _():
        m_ref[...] = jnp.full_like(m_ref, -jnp.inf)
        l_ref[...] = jnp.zeros_like(l_ref)
        acc_ref[...] = jnp.zeros_like(acc_ref)

    aug_lane = lax.broadcasted_iota(jnp.int32, (1, FOX_AUG), 1)

    def scores(item, masked):
        hh, ks, qs = item
        mine = aug_lane // FOX_AUG_GROUP == pl.program_id(1) * hpb + hh
        zero = jnp.zeros((), BF16)
        q_aug = jnp.concatenate([q_ref[qs, hh * d:(hh + 1) * d], jnp.where(mine, qa_ref[qs, :], zero)], axis=1)
        k_aug = jnp.concatenate([k_ref[ks, hh * d:(hh + 1) * d], jnp.where(mine, ka_ref[ks, :], zero)], axis=1)
        st = _dot_nt(k_aug, q_aug)
        if masked:
            kr = lax.broadcasted_iota(jnp.int32, st.shape, 0) + ks.start
            qc = lax.broadcasted_iota(jnp.int32, st.shape, 1) + qs.start
            st = jnp.where(qc >= kr, st, -jnp.inf)
        return st

    def absorb(item, st):
        hh, ks, qs = item
        m_old = m_ref[hh, :, qs]
        m_new = jnp.maximum(m_old, jnp.max(st, axis=0, keepdims=True))
        alpha = jnp.exp2(m_old - m_new)
        pt = jnp.exp2(st - m_new)
        l_ref[hh, :, qs] = alpha * l_ref[hh, :, qs] + jnp.sum(pt, axis=0, keepdims=True)
        acc_ref[hh, :, qs] = (alpha * acc_ref[hh, :, qs]
                              + _dot_tn(v_ref[ks, hh * d:(hh + 1) * d], pt.astype(BF16)))
        m_ref[hh, :, qs] = m_new

    def block(masked, finish):
        full = slice(0, tq)
        halves = [(slice(0, tk // 2), full), (slice(tk // 2, tk), slice(tq // 2, tq))] if masked else [(slice(0, tk), full)]
        items = [(hh, ks, qs) for hh in range(hpb) for ks, qs in halves]
        st = scores(items[0], masked)
        for n, item in enumerate(items):
            st_next = scores(items[n + 1], masked) if n + 1 < len(items) else None
            absorb(item, st)
            hh = item[0]
            if finish and (n + 1 == len(items) or items[n + 1][0] != hh):
                o = (acc_ref[hh] / l_ref[hh]).T
                g = gate_ref[:, hh * d:(hh + 1) * d].astype(F32)
                o_ref[:, hh * d:(hh + 1) * d] = (o * _silu(g)).astype(o_ref.dtype)
            st = st_next

    @pl.when(j < i)
    def _():
        block(False, False)

    @pl.when(j == i)
    def _():
        block(True, True)


def fox(proj, qa, ka, *, batch, seq, q_col, k_col, v_col, gate_col, t=1024, hpb=8):
    tt = batch * seq
    nb = seq // t
    d, h = FOX_DH, FOX_HEADS
    w = hpb * d
    qb, kb, gb, vb = q_col // w, k_col // w, gate_col // w, v_col // w
    pairs = [(i, j) for i in range(nb) for j in range(i + 1)]
    it = jnp.array([p[0] for p in pairs], jnp.int32)
    jt = jnp.array([p[1] for p in pairs], jnp.int32)
    grid_spec = pltpu.PrefetchScalarGridSpec(
        num_scalar_prefetch=2,
        grid=(batch, h // hpb, len(pairs)),
        in_specs=[
            pl.BlockSpec((t, w), lambda b, hh, p, it, jt: (b * nb + it[p], qb + hh)),
            pl.BlockSpec((t, FOX_AUG), lambda b, hh, p, it, jt: (b * nb + it[p], 0)),
            pl.BlockSpec((t, w), lambda b, hh, p, it, jt: (b * nb + jt[p], kb + hh)),
            pl.BlockSpec((t, FOX_AUG), lambda b, hh, p, it, jt: (b * nb + jt[p], 0)),
            pl.BlockSpec((t, w), lambda b, hh, p, it, jt: (b * nb + jt[p], vb + hh)),
            pl.BlockSpec((t, w), lambda b, hh, p, it, jt: (b * nb + it[p], gb + hh)),
        ],
        out_specs=pl.BlockSpec((t, w), lambda b, hh, p, it, jt: (b * nb + it[p], hh)),
        scratch_shapes=[pltpu.VMEM((hpb, 1, t), F32), pltpu.VMEM((hpb, 1, t), F32), pltpu.VMEM((hpb, d, t), F32)],
    )
    return pl.pallas_call(
        functools.partial(_fox_body, tq=t, tk=t, hpb=hpb),
        grid_spec=grid_spec,
        out_shape=jax.ShapeDtypeStruct((tt, h * d), BF16),
        compiler_params=_cparams(("parallel", "parallel", "arbitrary")),
        name="fox",
    )(it, jt, proj, qa, proj, ka, proj, proj)


def _regroup_perm(tm, steps, to_grouped):
    r = lax.broadcasted_iota(jnp.int32, (tm, tm), 0)
    c = lax.broadcasted_iota(jnp.int32, (tm, tm), 1)
    nc = tm // steps
    src = (r % nc) * steps + r // nc if to_grouped else (r % steps) * nc + r // steps
    return jnp.where(c == src, 1.0, 0.0).astype(BF16)


OUTPROJ_PART_ROWS = 256


def _cast_weight_once(wb_ref, w_ref):
    @pl.when(pl.program_id(0) == 0)
    def _():
        for r0 in range(0, w_ref.shape[0], MM_SUB):
            wb_ref[r0:r0 + MM_SUB, :] = w_ref[r0:r0 + MM_SUB, :].astype(wb_ref.dtype)


def _outproj_mid_body(a1_ref, a2_ref, wf_ref, x_ref, g_ref, r_ref, h_ref, w_ref, *, steps):
    k1 = a1_ref.shape[1]
    tm, d = x_ref.shape
    tp = OUTPROJ_PART_ROWS
    nc = tp // steps
    _cast_weight_once(w_ref, wf_ref)
    parts = [slice(p * tp, (p + 1) * tp) for p in range(tm // tp)]
    accs = [x_ref[rows, :] + _dot(a1_ref[rows, :], w_ref[:k1, :]) + _dot(a2_ref[rows, :], w_ref[k1:, :])
            for rows in parts]
    perm = _regroup_perm(tp, steps, True)
    for p, rows in enumerate(parts):
        acc = accs[p]
        r_ref[rows, :] = acc
        normed = (acc * lax.rsqrt(jnp.mean(acc * acc, axis=-1, keepdims=True) + EPS) * g_ref[...]).astype(BF16)
        by_step = _dot(perm, normed).astype(h_ref.dtype)
        for s in range(steps):
            h_ref[p * nc:(p + 1) * nc, s * d:(s + 1) * d] = by_step[s * nc:(s + 1) * nc, :]


def outproj_mid(a1, a2, w, x, g, *, steps, tm=512, name="outproj_mid"):
    m, d = x.shape
    row = pl.BlockSpec((tm, d), lambda i: (i, 0))
    return pl.pallas_call(
        functools.partial(_outproj_mid_body, steps=steps),
        grid=(m // tm,),
        in_specs=[pl.BlockSpec((tm, a1.shape[1]), lambda i: (i, 0)),
                  pl.BlockSpec((tm, a2.shape[1]), lambda i: (i, 0)),
                  pl.BlockSpec(w.shape, lambda i: (0, 0), pipeline_mode=pl.Buffered(1)),
                  row,
                  pl.BlockSpec((1, d), lambda i: (0, 0))],
        out_specs=[row, pl.BlockSpec((tm // steps, steps * d), lambda i: (i, 0))],
        out_shape=[jax.ShapeDtypeStruct((m, d), F32), jax.ShapeDtypeStruct((m // steps, steps * d), BF16)],
        scratch_shapes=[pltpu.VMEM(w.shape, BF16)],
        compiler_params=_cparams(("arbitrary",)),
        name=name,
    )(a1, a2, w, x, g.reshape(1, d))


def _outproj_final_body(a_ref, wf_ref, x_ref, g_ref, o_ref, w_ref, *, steps):
    tm = x_ref.shape[0]
    e = w_ref.shape[0]
    tp = OUTPROJ_PART_ROWS
    nc = tp // steps
    _cast_weight_once(w_ref, wf_ref)
    perm = _regroup_perm(tp, steps, False)
    a_toks = []
    for p in range(tm // tp):
        by_step = jnp.concatenate([a_ref[p * nc:(p + 1) * nc, s * e:(s + 1) * e] for s in range(steps)],
                                  axis=0)
        a_toks.append(_dot(perm, by_step).astype(BF16))
    parts = [slice(p * tp, (p + 1) * tp) for p in range(tm // tp)]
    accs = [x_ref[rows, :] + _dot(a_toks[p], w_ref[...]) for p, rows in enumerate(parts)]
    for p, rows in enumerate(parts):
        acc = accs[p]
        o_ref[rows, :] = acc * lax.rsqrt(jnp.mean(acc * acc, axis=-1, keepdims=True) + EPS) * g_ref[...]


def outproj_final(a_grouped, w, x, g, *, steps, tm=512, name="outproj_final"):
    m, d = x.shape
    e = w.shape[0]
    row = pl.BlockSpec((tm, d), lambda i: (i, 0))
    return pl.pallas_call(
        functools.partial(_outproj_final_body, steps=steps),
        grid=(m // tm,),
        in_specs=[pl.BlockSpec((tm // steps, steps * e), lambda i: (i, 0)),
                  pl.BlockSpec(w.shape, lambda i: (0, 0), pipeline_mode=pl.Buffered(1)),
                  row,
                  pl.BlockSpec((1, d), lambda i: (0, 0))],
        out_specs=row,
        out_shape=jax.ShapeDtypeStruct((m, d), F32),
        scratch_shapes=[pltpu.VMEM(w.shape, BF16)],
        compiler_params=_cparams(("arbitrary",)),
        name=name,
    )(a_grouped, w, x, g.reshape(1, d))


S5_SCAN_BLK = 8
S5_GPB = 8


def _s5_entering_state(x, arow, *, cols_per_seq, n_levels):
    n = x.shape[1]
    p = S5_STATE
    nseq = n // cols_per_seq

    def to_rows(part):
        return jnp.concatenate([part[:, s * cols_per_seq:(s + 1) * cols_per_seq] for s in range(nseq)], axis=0).T

    def to_cols(rows):
        rt = rows.T
        return jnp.concatenate([rt[s * p:(s + 1) * p, :] for s in range(nseq)], axis=1)

    xr, xi = to_rows(x[:p, :]), to_rows(x[p:, :])
    pos = lax.broadcasted_iota(jnp.int32, (cols_per_seq, 1), 0)
    blk = S5_SCAN_BLK
    half = arow.shape[0] // 2
    for lvl in range(n_levels):
        sh = 1 << lvl
        keep = pos % blk >= sh
        sr = jnp.where(keep, pltpu.roll(xr, sh, axis=0), 0.0)
        si = jnp.where(keep, pltpu.roll(xi, sh, axis=0), 0.0)
        ar, ai = arow[lvl:lvl + 1, :], arow[half + lvl:half + lvl + 1, :]
        xr, xi = xr + ar * sr - ai * si, xi + ar * si + ai * sr
    pw_r, pw_i = arow[n_levels:n_levels + blk, :], arow[half + n_levels:half + n_levels + blk, :]
    out_r, out_i = [xr[:blk, :]], [xi[:blk, :]]
    for r0 in range(blk, cols_per_seq, blk):
        cr, ci = out_r[-1][blk - 1:blk, :], out_i[-1][blk - 1:blk, :]
        out_r.append(xr[r0:r0 + blk, :] + pw_r * cr - pw_i * ci)
        out_i.append(xi[r0:r0 + blk, :] + pw_r * ci + pw_i * cr)
    xr, xi = jnp.concatenate(out_r, axis=0), jnp.concatenate(out_i, axis=0)
    keep = pos >= 1
    pr = jnp.where(keep, pltpu.roll(xr, 1, axis=0), 0.0)
    pi = jnp.where(keep, pltpu.roll(xi, 1, axis=0), 0.0)
    return jnp.concatenate([to_cols(pr), to_cols(pi)], axis=0).astype(BF16)


def _s5_body(u_ref, wt_ref, nt_ref, mt_ref, arow_ref, o_ref, *, cols_per_seq, n_levels):
    t, _, n = u_ref.shape
    cg = S5_GROUP_CH
    groups = range(S5_GPB)
    ch = [slice(gi * cg, (gi + 1) * cg) for gi in groups]
    us = [u_ref[:, ch[gi], :].reshape(t * cg, n) for gi in groups]
    xs = [_dot(nt_ref[gi], us[gi]) for gi in groups]
    ys = [_dot(wt_ref[gi], us[gi]) for gi in groups]
    prevs = [_s5_entering_state(xs[gi], arow_ref[gi], cols_per_seq=cols_per_seq, n_levels=n_levels)
             for gi in groups]
    for gi in groups:
        y = ys[gi] + _dot(mt_ref[gi], prevs[gi])
        o_ref[:, ch[gi], :] = y.astype(o_ref.dtype).reshape(t, cg, n)


def s5_apply(proj_t, wt_g, nt_g, mt_g, arow, *, groups, cols_per_seq, n_levels):
    t, _, n = proj_t.shape
    cg = S5_GROUP_CH * S5_GPB
    spec3 = lambda shp: pl.BlockSpec((S5_GPB,) + shp, lambda i: (i, 0, 0))
    return pl.pallas_call(
        functools.partial(_s5_body, cols_per_seq=cols_per_seq, n_levels=n_levels),
        grid=(groups // S5_GPB,),
        in_specs=[pl.BlockSpec((t, cg, n), lambda i: (0, i, 0)),
                  spec3(wt_g.shape[1:]), spec3(nt_g.shape[1:]), spec3(mt_g.shape[1:]), spec3(arow.shape[1:])],
        out_specs=pl.BlockSpec((t, cg, n), lambda i: (0, i, 0)),
        out_shape=jax.ShapeDtypeStruct((t, groups * S5_GROUP_CH, n), BF16),
        compiler_params=_cparams(("parallel",)),
        name="s5",
    )(proj_t, wt_g, nt_g, mt_g, arow)


def s5_operators(lam_re, lam_im, log_step, b_re, b_im, c_re, c_im, d, *, n_levels, n_seq):
    hp = lax.Precision.HIGH
    g, p = lam_re.shape
    cg, t = S5_GROUP_CH, S5_T
    lr = jnp.minimum(lam_re.astype(F32), -1e-4)
    li = lam_im.astype(F32)
    step = jnp.exp(log_step.astype(F32))[:, None]
    mag = jnp.exp(lr * step)
    lb_re, lb_im = mag * jnp.cos(li * step), mag * jnp.sin(li * step)
    den = lr * lr + li * li
    nr, ni = lb_re - 1.0, lb_im
    coef_re = (nr * lr + ni * li) / den
    coef_im = (ni * lr - nr * li) / den
    bb_re = coef_re[..., None] * b_re - coef_im[..., None] * b_im
    bb_im = coef_re[..., None] * b_im + coef_im[..., None] * b_re

    def power(m):
        mm = m.astype(F32)[None, :, None]
        mg = jnp.exp(lr[:, None, :] * step[:, None, :] * mm)
        ang = li[:, None, :] * step[:, None, :] * mm
        return mg * jnp.cos(ang), mg * jnp.sin(ang)

    pw_re, pw_im = power(jnp.arange(t + 1))
    pwt_re, pwt_im = pw_re.transpose(0, 2, 1), pw_im.transpose(0, 2, 1)

    def pow_times_b(pr, pi):
        re = pr[:, :, :, None] * bb_re[:, :, None, :] - pi[:, :, :, None] * bb_im[:, :, None, :]
        im = pr[:, :, :, None] * bb_im[:, :, None, :] + pi[:, :, :, None] * bb_re[:, :, None, :]
        return re.reshape(g, p, t * cg), im.reshape(g, p, t * cg)

    pb_re, pb_im = pow_times_b(pwt_re[:, :, :t], pwt_im[:, :, :t])
    taps = (jnp.einsum('gcp,gpy->gcy', c_re.astype(F32), pb_re, precision=hp)
            - jnp.einsum('gcp,gpy->gcy', c_im.astype(F32), pb_im, precision=hp))
    y_idx = jnp.arange(t * cg)
    taps = taps + d.astype(F32).reshape(g, cg, 1) * (y_idx[None, :] == jnp.arange(cg)[:, None]).astype(F32)[None]
    place = ((y_idx[None, :, None] // cg == jnp.arange(t)[:, None, None] - y_idx[None, None, :] // cg)
             & (y_idx[None, :, None] % cg == y_idx[None, None, :] % cg)).astype(BF16)
    wt_g = jnp.einsum('gcy,tyz->gtcz', taps.astype(BF16), place,
                      preferred_element_type=F32).astype(BF16).reshape(g, t * cg, t * cg)
    n_re, n_im = pow_times_b(pwt_re[:, :, t - 1::-1], pwt_im[:, :, t - 1::-1])
    nt_g = jnp.concatenate([n_re, n_im], axis=1).astype(BF16)
    cp_re = c_re[:, None] * pw_re[:, 1:, None, :] - c_im[:, None] * pw_im[:, 1:, None, :]
    cp_im = c_re[:, None] * pw_im[:, 1:, None, :] + c_im[:, None] * pw_re[:, 1:, None, :]
    mt_g = jnp.concatenate([cp_re, -cp_im], axis=-1).reshape(g, t * cg, 2 * p).astype(BF16)
    ar, ai = power(t * jnp.concatenate([2 ** jnp.arange(n_levels), jnp.arange(1, S5_SCAN_BLK + 1)]))
    arow = jnp.concatenate([jnp.tile(ar, (1, 1, n_seq)), jnp.tile(ai, (1, 1, n_seq))], axis=1)
    return wt_g, nt_g, mt_g, arow


def _gelu_tanh(y):
    return 0.5 * y * (1.0 + jnp.tanh(math.sqrt(2.0 / math.pi) * (y + 0.044715 * (y * y * y))))


GLU_PARTS = 2


def _glu_body(y_ref, w_ref, b_ref, gate_ref, o_ref, z_ref, zb_ref, wt_ref):
    @pl.when((pl.program_id(0) == 0) & (pl.program_id(1) == 0))
    def _():
        _transpose_into(wt_ref, w_ref)

    tm = y_ref.shape[1]
    parts = [slice(c0, c0 + tm // GLU_PARTS) for c0 in range(0, tm, tm // GLU_PARTS)]
    for cols in parts:
        z = _gelu_tanh(y_ref[:, cols].astype(F32))
        z_ref[:, cols] = z
        zb_ref[:, cols] = z.astype(BF16)
    for cols in parts:
        for f0 in range(0, o_ref.shape[1], MM_SUB):
            rows = slice(f0, f0 + MM_SUB)
            lin = _dot(wt_ref[rows, :], zb_ref[:, cols]) + b_ref[rows, :]
            out = z_ref[rows, cols] * _sigmoid(lin) * _silu(gate_ref[rows, cols].astype(F32))
            o_ref[cols, rows] = out.T.astype(o_ref.dtype)


def glu_gate(y_t, w, b, proj_t, *, gate_row, tm=512):
    s, e, n = y_t.shape
    gb = gate_row // e
    return pl.pallas_call(
        _glu_body,
        grid=(s, n // tm),
        in_specs=[pl.BlockSpec((None, e, tm), lambda t, i: (t, 0, i)),
                  pl.BlockSpec((e, e), lambda t, i: (0, 0), pipeline_mode=pl.Buffered(1)),
                  pl.BlockSpec((e, 1), lambda t, i: (0, 0), pipeline_mode=pl.Buffered(1)),
                  pl.BlockSpec((None, e, tm), lambda t, i: (t, gb, i))],
        out_specs=pl.BlockSpec((tm, e), lambda t, i: (i, t)),
        out_shape=jax.ShapeDtypeStruct((n, s * e), BF16),
        scratch_shapes=[pltpu.VMEM((e, tm), F32), pltpu.VMEM((e, tm), BF16), pltpu.VMEM((e, e), BF16)],
        compiler_params=_cparams(("arbitrary", "arbitrary")),
        name="glu",
    )(y_t, w, b.reshape(e, 1), proj_t)


def kernel(x, ab_norm_g, ab_w_in, gla_alpha_up, gla_alpha_b, gla_head_g, fox_f_b, ab_w_out, c_norm_g, c_w_in, s5_lambda_re, s5_lambda_im, s5_log_step, s5_b_re, s5_b_im, s5_c_re, s5_c_im, s5_d, glu_w, glu_b, c_w_out, final_norm_g):
    batch, seq, d = x.shape
    t = batch * seq
    x2 = x.reshape(t, d)

    hk, hv, fw = GLA_HEADS * GLA_DK, GLA_HEADS * GLA_DV, FOX_HEADS * FOX_DH
    sizes = (hk, hk, hv, GLA_LOWRANK, hv, fw, fw, fw, FOX_HEADS, fw)
    offs = [0]
    for s in sizes:
        offs.append(offs[-1] + s)
    w_gla, w_fox, w_small = ab_weight_prep(jnp.swapaxes(ab_w_in, 1, 2), offs)
    c_gq, c_gk, c_gv, c_gg = 0, hk, 2 * hk, 2 * hk + hv
    c_fq, c_fk, c_fv, c_fg = 0, fw, 2 * fw, 3 * fw

    h0, proj_g, small = norm_proj(x2, ab_norm_g[0], w_gla, w_small, name="ab_in")
    proj_f = matmul(h0, w_fox, name="ab_in_fox")

    o_gla = gla(proj_g, small, gla_alpha_up[0], gla_alpha_b[0], gla_head_g[0],
                batch=batch, seq=seq, q_col=c_gq, k_col=c_gk, v_col=c_gv, gate_col=c_gg)

    fb_row = jnp.zeros((1, 128), F32).at[0, GLA_LOWRANK:GLA_LOWRANK + FOX_HEADS].set(fox_f_b[0])
    qa, ka = fox_gate(small, fb_row, batch=batch, seq=seq, col0=GLA_LOWRANK)
    o_fox = fox(proj_f, qa, ka, batch=batch, seq=seq, q_col=c_fq, k_col=c_fk, v_col=c_fv, gate_col=c_fg)

    x1, h1g = outproj_mid(o_gla, o_fox, ab_w_out[0], x2, c_norm_g[0], steps=S5_T, name="ab_out")

    e = d
    groups = e // S5_GROUP_CH
    n_chunks = t // S5_T
    cols_per_seq = seq // S5_T
    n_levels = S5_SCAN_BLK.bit_length() - 1
    proj_t = matmul_t_grouped(c_w_in[0], h1g, S5_T, BF16, name="c_in")
    ops = s5_operators(s5_lambda_re[0], s5_lambda_im[0], s5_log_step[0], s5_b_re[0], s5_b_im[0],
                       s5_c_re[0], s5_c_im[0], s5_d[0], n_levels=n_levels, n_seq=batch)
    y_t = s5_apply(proj_t, *ops, groups=groups, cols_per_seq=cols_per_seq, n_levels=n_levels)
    zz = glu_gate(y_t, glu_w[0], glu_b[0], proj_t, gate_row=e)
    out = outproj_final(zz, c_w_out[0], x1, final_norm_g, steps=S5_T, name="c_out")
    return out.reshape(batch, seq, d)
```

```python
import functools
import math

import jax
import jax.numpy as jnp
from jax import lax
from jax.experimental import pallas as pl
from jax.experimental.pallas import tpu as pltpu

EPS = 1e-6
F32 = jnp.float32
BF16 = jnp.bfloat16

GLA_HEADS = 4
GLA_DK = 128
GLA_DV = 256
GLA_LOWRANK = 16
GLA_TAU = 16.0
GLA_CHUNK = 64
GLA_SUB = 8
GLA_UNROLL = 8
FOX_HEADS = 8
FOX_DH = 128
FOX_AUG = 128
FOX_AUG_GROUP = 8
S5_GROUP_CH = 16
S5_STATE = 64
S5_T = 16

VMEM_LIMIT = 58 * 1024 * 1024

NT_DIMS = (((1,), (1,)), ((), ()))
TN_DIMS = (((0,), (0,)), ((), ()))
LOG2E = math.log2(math.e)


def _cparams(sem):
    return pltpu.CompilerParams(dimension_semantics=sem, vmem_limit_bytes=VMEM_LIMIT)


def _dot(a, b):
    return jnp.dot(a, b, preferred_element_type=F32)


def _dot_nt(a, b):
    return lax.dot_general(a, b, NT_DIMS, preferred_element_type=F32)


def _dot_tn(a, b):
    return lax.dot_general(a, b, TN_DIMS, preferred_element_type=F32)


def _log_sigmoid(z):
    return jnp.minimum(z, 0.0) - jnp.log(1.0 + jnp.exp(-jnp.abs(z)))


def _sigmoid(z):
    return 0.5 * jnp.tanh(0.5 * z) + 0.5


def _silu(z):
    return z * _sigmoid(z)


def _split3(x):
    hi = x.astype(BF16)
    r1 = x - hi.astype(F32)
    mid = r1.astype(BF16)
    lo = (r1 - mid.astype(F32)).astype(BF16)
    return hi, mid, lo


def _norm_proj_body(x_ref, g_ref, wt_ref, ws_ref, h_ref, o_ref, s_ref):
    x = x_ref[...]
    ms = jnp.mean(x * x, axis=-1, keepdims=True)
    h = (x * lax.rsqrt(ms + EPS) * g_ref[...]).astype(BF16)
    h_ref[...] = h
    for c0 in range(0, o_ref.shape[1], MM_SUB):
        o_ref[:, c0:c0 + MM_SUB] = _dot_nt(h, wt_ref[c0:c0 + MM_SUB, :]).astype(o_ref.dtype)
    s_ref[...] = _dot_nt(h, ws_ref[...])


def norm_proj(x, g, wt, wt_small, tm=512, name="norm_proj"):
    m, d = x.shape
    n = wt.shape[0]
    ns = wt_small.shape[0]
    return pl.pallas_call(
        _norm_proj_body,
        grid=(m // tm,),
        in_specs=[pl.BlockSpec((tm, d), lambda i: (i, 0)),
                  pl.BlockSpec((1, d), lambda i: (0, 0)),
                  pl.BlockSpec((n, d), lambda i: (0, 0), pipeline_mode=pl.Buffered(1)),
                  pl.BlockSpec((ns, d), lambda i: (0, 0))],
        out_specs=[pl.BlockSpec((tm, d), lambda i: (i, 0)),
                   pl.BlockSpec((tm, n), lambda i: (i, 0)),
                   pl.BlockSpec((tm, ns), lambda i: (i, 0))],
        out_shape=[jax.ShapeDtypeStruct((m, d), BF16), jax.ShapeDtypeStruct((m, n), BF16),
                   jax.ShapeDtypeStruct((m, ns), F32)],
        compiler_params=_cparams(("parallel",)),
        name=name,
    )(x, g.reshape(1, d), wt, wt_small)


def _mm_body(a_ref, wt_ref, o_ref):
    a = a_ref[...]
    for c0 in range(0, o_ref.shape[1], MM_SUB):
        o_ref[:, c0:c0 + MM_SUB] = _dot_nt(a, wt_ref[c0:c0 + MM_SUB, :]).astype(o_ref.dtype)


def matmul(a, wt, tm=512, name="matmul"):
    m, k = a.shape
    n = wt.shape[0]
    return pl.pallas_call(
        _mm_body,
        grid=(m // tm,),
        in_specs=[pl.BlockSpec((tm, k), lambda i: (i, 0)),
                  pl.BlockSpec((n, k), lambda i: (0, 0), pipeline_mode=pl.Buffered(1))],
        out_specs=pl.BlockSpec((tm, n), lambda i: (i, 0)),
        out_shape=jax.ShapeDtypeStruct((m, n), BF16),
        compiler_params=_cparams(("parallel",)),
        name=name,
    )(a, wt)


def _ab_wprep_body(w_ref, og_ref, of_ref, s_ref, *, offs):
    w = w_ref[0]
    seg = lambda n: w[offs[n]:offs[n + 1], :]
    og_ref[...] = jnp.concatenate([w[:offs[3], :], seg(4)], axis=0).astype(og_ref.dtype)
    of_ref[...] = jnp.concatenate([seg(5) * (FOX_DH ** -0.5 * LOG2E), w[offs[6]:offs[8], :], seg(9)],
                                  axis=0).astype(of_ref.dtype)
    pad = s_ref.shape[0] - (offs[4] - offs[3]) - (offs[9] - offs[8])
    s_ref[...] = jnp.concatenate([seg(3), seg(8), jnp.zeros((pad, w.shape[1]), F32)], axis=0).astype(s_ref.dtype)


def ab_weight_prep(w_t3, offs, tk=256):
    _, n_in, d = w_t3.shape
    n_gla = offs[3] + offs[5] - offs[4]
    n_fox = offs[8] - offs[5] + offs[10] - offs[9]
    return pl.pallas_call(
        functools.partial(_ab_wprep_body, offs=tuple(offs)),
        grid=(d // tk,),
        in_specs=[pl.BlockSpec((1, n_in, tk), lambda i: (0, 0, i))],
        out_specs=[pl.BlockSpec((n_gla, tk), lambda i: (0, i)), pl.BlockSpec((n_fox, tk), lambda i: (0, i)),
                   pl.BlockSpec((128, tk), lambda i: (0, i))],
        out_shape=[jax.ShapeDtypeStruct((n_gla, d), BF16), jax.ShapeDtypeStruct((n_fox, d), BF16),
                   jax.ShapeDtypeStruct((128, d), BF16)],
        compiler_params=_cparams(("parallel",)),
        name="ab_wprep",
    )(w_t3)


MM_SUB = 512


def _transpose_into(wt_ref, w_ref):
    for c0 in range(0, w_ref.shape[1], MM_SUB):
        wt_ref[c0:c0 + MM_SUB, :] = w_ref[:, c0:c0 + MM_SUB].astype(wt_ref.dtype).T


def _mm_t_body(w_ref, a_ref, o_ref, wt_ref):
    @pl.when((pl.program_id(1) == 0) & (pl.program_id(2) == 0))
    def _():
        _transpose_into(wt_ref, w_ref)

    a = a_ref[...]
    for r0 in range(0, o_ref.shape[0], MM_SUB):
        o_ref[r0:r0 + MM_SUB, :] = _dot_nt(wt_ref[r0:r0 + MM_SUB, :], a).astype(o_ref.dtype)


def matmul_t_grouped(w, a2, steps, out_dtype, tm=512, tn=2048, name="matmul_tg"):
    k, nf = w.shape
    n = a2.shape[0]
    return pl.pallas_call(
        _mm_t_body,
        grid=(nf // tn, steps, n // tm),
        in_specs=[pl.BlockSpec((k, tn), lambda j, t, i: (0, j), pipeline_mode=pl.Buffered(1)),
                  pl.BlockSpec((tm, k), lambda j, t, i: (i, t))],
        out_specs=pl.BlockSpec((None, tn, tm), lambda j, t, i: (t, j, i)),
        out_shape=jax.ShapeDtypeStruct((steps, nf, n), out_dtype),
        scratch_shapes=[pltpu.VMEM((tn, k), BF16)],
        compiler_params=_cparams(("arbitrary", "arbitrary", "arbitrary")),
        name=name,
    )(w, a2)


def _gla_body(q_ref, k_ref, v_ref, glr_ref, aup_ref, ab_ref, gate_ref, hg_ref, o_ref,
              s_ref, b_ref, kf_ref, kts_ref, dec_ref, *, tb, hpb):
    ib = pl.program_id(2)
    C, SB, dk, dv = GLA_CHUNK, GLA_SUB, GLA_DK, GLA_DV

    @pl.when(ib == 0)
    def _():
        s_ref[...] = jnp.zeros_like(s_ref)

    glr = glr_ref[:, :GLA_LOWRANK].astype(BF16)
    tw = 2 * C
    r = lax.broadcasted_iota(jnp.int32, (tw, tw), 0)
    c = lax.broadcasted_iota(jnp.int32, (tw, tw), 1)
    tri = jnp.where((r // C == c // C) & (c <= r), 1.0, 0.0).astype(BF16)
    nc = tb // C
    heads = range(hpb)
    kcs = [slice(hh * dk, (hh + 1) * dk) for hh in heads]
    zs = [_dot(glr, aup_ref[:, kcs[hh]].astype(BF16)) + ab_ref[:, kcs[hh]] for hh in heads]
    b_blks = []
    for hh in heads:
        la = _log_sigmoid(zs[hh]) * (1.0 / GLA_TAU)
        hi = la.astype(BF16)
        lo = (la - hi.astype(F32)).astype(BF16)
        b_blks.append(jnp.concatenate(
            [_dot(tri, hi[r0:r0 + tw]) + _dot(tri, lo[r0:r0 + tw]) for r0 in range(0, tb, tw)], axis=0))
    for hh in heads:
        b_blk = b_blks[hh]
        k_blk = k_ref[:, kcs[hh]].astype(F32)
        b_ref[hh] = b_blk
        kf_ref[hh] = k_blk
        b3 = b_blk.reshape(nc, C, dk)
        b_end = b3[:, C - 1:C, :]
        k_end = (k_blk.reshape(nc, C, dk) * jnp.exp(b_end - b3)).reshape(tb, dk)
        kt_end = k_end.T.astype(BF16)
        dec_t = jnp.exp(b_end.reshape(nc, dk)).T
        for ci in range(nc):
            kts_ref[hh, ci] = kt_end[:, ci * C:(ci + 1) * C]
            dec_ref[hh, ci] = jnp.broadcast_to(dec_t[:, ci:ci + 1], (dk, 128))

    row = lax.broadcasted_iota(jnp.int32, (C, 1), 0)
    rowi = lax.broadcasted_iota(jnp.int32, (C, C), 0)
    coli = lax.broadcasted_iota(jnp.int32, (C, C), 1)
    lane_c = lax.broadcasted_iota(jnp.int32, (SB, C), 1)
    sub_r = lax.broadcasted_iota(jnp.int32, (SB, 1), 0)
    ones = jnp.ones((GLA_DK, C), BF16)
    neg = -jnp.inf
    scale = GLA_DK ** -0.5

    def stage_state(ci, r0, hh):
        kc = slice(hh * dk, (hh + 1) * dk)
        vc = slice(hh * dv, (hh + 1) * dv)
        q = q_ref[pl.ds(r0, C), kc].astype(F32) * scale
        k = kf_ref[hh, pl.ds(r0, C), :]
        b = b_ref[hh, pl.ds(r0, C), :]
        v = v_ref[pl.ds(r0, C), vc]

        s = s_ref[hh]
        o = _dot((q * jnp.exp(b)).astype(BF16), s.astype(BF16))
        dec = dec_ref[hh, ci]
        s_ref[hh] = s * jnp.concatenate([dec] * (dv // 128), axis=1) + _dot(kts_ref[hh, ci], v)

        attn = jnp.zeros((C, C), F32)
        h = C // 2
        while h >= SB:
            ref = jnp.broadcast_to(b.reshape(C // (2 * h), 2 * h, GLA_DK)[:, h - 1:h, :],
                                   (C // (2 * h), 2 * h, GLA_DK)).reshape(C, GLA_DK)
            upper = (row // h) % 2 == 1
            q_h = q * jnp.exp(jnp.where(upper, b - ref, neg))
            k_h = k * jnp.exp(jnp.where(upper, neg, ref - b))
            a_h = _dot_nt(q_h.astype(BF16), k_h.astype(BF16))
            attn = attn + (a_h if 2 * h == C else jnp.where(rowi // (2 * h) == coli // (2 * h), a_h, 0.0))
            h //= 2
        return q, b, v, o, attn

    def stage_diag(r0, hh, q, b):
        zs = []
        for bi in range(C // SB):
            s0 = bi * SB
            q_i = q[s0:s0 + SB, :]
            b_i = b[s0:s0 + SB, :]
            for j in range(SB):
                k_j = kf_ref[hh, pl.ds(r0 + (s0 + j), 1), :]
                b_j = b_ref[hh, pl.ds(r0 + (s0 + j), 1), :]
                zs.append(q_i * k_j * jnp.exp(jnp.where(sub_r >= j, b_i - b_j, neg)))
        return _dot(jnp.concatenate(zs, axis=0).astype(BF16), ones)

    def stage_out(r0, hh, v, o, attn, zsum):
        vc = slice(hh * dv, (hh + 1) * dv)
        diag = []
        for bi in range(C // SB):
            acc = jnp.zeros((SB, C), F32)
            for j in range(SB):
                n0 = (bi * SB + j) * SB
                acc = acc + jnp.where(lane_c == bi * SB + j, zsum[n0:n0 + SB, :], 0.0)
            diag.append(acc)
        attn = attn + jnp.concatenate(diag, axis=0)
        o = o + _dot(attn.astype(BF16), v)

        o = o * lax.rsqrt(jnp.mean(o * o, axis=-1, keepdims=True) + EPS) * hg_ref[:, vc]
        g = gate_ref[pl.ds(r0, C), vc].astype(F32)
        o_ref[pl.ds(r0, C), vc] = (o * _silu(g)).astype(o_ref.dtype)

    def chunk(ci, carry):
        r0 = pl.multiple_of(ci * C, C)
        heads = range(hpb)
        st = [stage_state(ci, r0, hh) for hh in heads]
        zsums = [stage_diag(r0, hh, st[hh][0], st[hh][1]) for hh in heads]
        for hh in heads:
            _, _, v, o, attn = st[hh]
            stage_out(r0, hh, v, o, attn, zsums[hh])
        return carry

    lax.fori_loop(0, tb // C, chunk, 0, unroll=GLA_UNROLL)


def gla(proj, small, alpha_up, alpha_b, head_g, *, batch, seq, q_col, k_col, v_col, gate_col, tb=512, hpb=4):
    t = batch * seq
    nb = seq // tb
    dk, dv, h = GLA_DK * hpb, GLA_DV * hpb, GLA_HEADS // hpb
    qb, kb, vb, gb = q_col // dk, k_col // dk, v_col // dv, gate_col // dv
    tok = lambda b, hh, i: b * nb + i
    nc = tb // GLA_CHUNK
    return pl.pallas_call(
        functools.partial(_gla_body, tb=tb, hpb=hpb),
        grid=(batch, h, nb),
        in_specs=[
            pl.BlockSpec((tb, dk), lambda b, hh, i: (tok(b, hh, i), qb + hh)),
            pl.BlockSpec((tb, dk), lambda b, hh, i: (tok(b, hh, i), kb + hh)),
            pl.BlockSpec((tb, dv), lambda b, hh, i: (tok(b, hh, i), vb + hh)),
            pl.BlockSpec((tb, 128), lambda b, hh, i: (tok(b, hh, i), 0)),
            pl.BlockSpec((GLA_LOWRANK, dk), lambda b, hh, i: (0, hh)),
            pl.BlockSpec((1, dk), lambda b, hh, i: (0, hh)),
            pl.BlockSpec((tb, dv), lambda b, hh, i: (tok(b, hh, i), gb + hh)),
            pl.BlockSpec((1, dv), lambda b, hh, i: (0, hh)),
        ],
        out_specs=pl.BlockSpec((tb, dv), lambda b, hh, i: (tok(b, hh, i), hh)),
        out_shape=jax.ShapeDtypeStruct((t, GLA_HEADS * GLA_DV), BF16),
        scratch_shapes=[pltpu.VMEM((hpb, GLA_DK, GLA_DV), F32), pltpu.VMEM((hpb, tb, GLA_DK), F32),
                        pltpu.VMEM((hpb, tb, GLA_DK), F32),
                        pltpu.VMEM((hpb, nc, GLA_DK, GLA_CHUNK), BF16),
                        pltpu.VMEM((hpb, nc, GLA_DK, 128), F32)],
        compiler_params=_cparams(("parallel", "parallel", "arbitrary")),
        name="gla",
    )(proj, proj, proj, small, alpha_up, alpha_b.reshape(1, -1), proj, head_g.reshape(1, -1))


FOX_GATE_UNROLL = 4


def _fox_gate_body(s_ref, fb_ref, qa_ref, ka_ref, cs_ref, off_ref, *, blk, col0):
    n = s_ref.shape[0] // blk
    r = lax.broadcasted_iota(jnp.int32, (blk, blk), 0)
    c = lax.broadcasted_iota(jnp.int32, (blk, blk), 1)
    tri = jnp.where(c <= r, 1.0, 0.0).astype(BF16)
    lanes = s_ref.shape[1]
    pr = lax.broadcasted_iota(jnp.int32, (3 * lanes, FOX_AUG), 0)
    pc = lax.broadcasted_iota(jnp.int32, (3 * lanes, FOX_AUG), 1)
    head, piece = pr % lanes - col0, pr // lanes
    is_head = (head >= 0) & (head < FOX_HEADS)
    sel_q = jnp.where(is_head & (pc == head * FOX_AUG_GROUP + piece), 1.0, 0.0).astype(BF16)
    sel_k = jnp.where(is_head & (pc == head * FOX_AUG_GROUP + 3 + piece), -1.0, 0.0).astype(BF16)
    lane = lax.broadcasted_iota(jnp.int32, (1, FOX_AUG), 1)
    used = lane < FOX_HEADS * FOX_AUG_GROUP
    one_q = jnp.where(used & (lane % FOX_AUG_GROUP >= 3) & (lane % FOX_AUG_GROUP < 6), 1.0, 0.0)
    one_k = jnp.where(used & (lane % FOX_AUG_GROUP < 3), 1.0, 0.0)

    def local(i, carry):
        r0 = pl.multiple_of(i * blk, blk)
        lf = _log_sigmoid(s_ref[pl.ds(r0, blk), :] + fb_ref[...])
        hi, mid, lo = _split3(lf)
        cs_ref[pl.ds(r0, blk), :] = _dot(tri, hi) + _dot(tri, mid) + _dot(tri, lo)
        return carry

    lax.fori_loop(0, n, local, 0, unroll=FOX_GATE_UNROLL)

    off = jnp.zeros((1, lanes), F32)
    for i in range(n):
        off_ref[i:i + 1, :] = off
        off = off + cs_ref[(i + 1) * blk - 1:(i + 1) * blk, :]

    def place(i, carry):
        r0 = pl.multiple_of(i * blk, blk)
        cs = cs_ref[pl.ds(r0, blk), :] + off_ref[pl.ds(i, 1), :]
        pieces = jnp.concatenate(_split3(cs * LOG2E), axis=1)
        qa_ref[pl.ds(r0, blk), :] = (_dot(pieces, sel_q) + one_q).astype(BF16)
        ka_ref[pl.ds(r0, blk), :] = (_dot(pieces, sel_k) + one_k).astype(BF16)
        return carry

    lax.fori_loop(0, n, place, 0, unroll=FOX_GATE_UNROLL)


def fox_gate(small, fb_row, *, batch, seq, col0, blk=256):
    shp = jax.ShapeDtypeStruct((batch * seq, FOX_AUG), BF16)
    spec = pl.BlockSpec((seq, FOX_AUG), lambda b: (b, 0))
    return pl.pallas_call(
        functools.partial(_fox_gate_body, blk=blk, col0=col0),
        grid=(batch,),
        in_specs=[pl.BlockSpec((seq, 128), lambda b: (b, 0)),
                  pl.BlockSpec((1, 128), lambda b: (0, 0))],
        out_specs=[spec, spec],
        out_shape=[shp, shp],
        scratch_shapes=[pltpu.VMEM((seq, 128), F32), pltpu.VMEM((seq // blk, 128), F32)],
        compiler_params=_cparams(("parallel",)),
        name="fox_gate",
    )(small, fb_row)


def _fox_body(it_ref, jt_ref, q_ref, qa_ref, k_ref, ka_ref, v_ref, gate_ref, o_ref,
              m_ref, l_ref, acc_ref, *, tq, tk, hpb):
    p = pl.program_id(2)
    i = it_ref[p]
    j = jt_ref[p]
    d = FOX_DH

    @pl.when(j == 0)
    def _():
        m_ref[...] = jnp.full_like(m_ref, -jnp.inf)
        l_ref[...] = jnp.zeros_like(l_ref)
        acc_ref[...] = jnp.zeros_like(acc_ref)

    aug_lane = lax.broadcasted_iota(jnp.int32, (1, FOX_AUG), 1)

    def scores(item, masked):
        hh, ks, qs = item
        mine = aug_lane // FOX_AUG_GROUP == pl.program_id(1) * hpb + hh
        zero = jnp.zeros((), BF16)
        q_aug = jnp.concatenate([q_ref[qs, hh * d:(hh + 1) * d], jnp.where(mine, qa_ref[qs, :], zero)], axis=1)
        k_aug = jnp.concatenate([k_ref[ks, hh * d:(hh + 1) * d], jnp.where(mine, ka_ref[ks, :], zero)], axis=1)
        st = _dot_nt(k_aug, q_aug)
        if masked:
            kr = lax.broadcasted_iota(jnp.int32, st.shape, 0) + ks.start
            qc = lax.broadcasted_iota(jnp.int32, st.shape, 1) + qs.start
            st = jnp.where(qc >= kr, st, -jnp.inf)
        return st

    def absorb(item, st):
        hh, ks, qs = item
        m_old = m_ref[hh, :, qs]
        m_new = jnp.maximum(m_old, jnp.max(st, axis=0, keepdims=True))
        alpha = jnp.exp2(m_old - m_new)
        pt = jnp.exp2(st - m_new).astype(BF16)
        psum = _dot(jnp.ones((8, pt.shape[0]), BF16), pt)[0:1, :]
        l_ref[hh, :, qs] = alpha * l_ref[hh, :, qs] + psum
        acc_ref[hh, :, qs] = alpha * acc_ref[hh, :, qs] + _dot_tn(v_ref[ks, hh * d:(hh + 1) * d], pt)
        m_ref[hh, :, qs] = m_new

    def block(masked, finish):
        full = slice(0, tq)
        halves = [(slice(0, tk // 2), full), (slice(tk // 2, tk), slice(tq // 2, tq))] if masked else [(slice(0, tk), full)]
        items = [(hh, ks, qs) for hh in range(hpb) for ks, qs in halves]
        st = scores(items[0], masked)
        for n, item in enumerate(items):
            st_next = scores(items[n + 1], masked) if n + 1 < len(items) else None
            absorb(item, st)
            hh = item[0]
            if finish and (n + 1 == len(items) or items[n + 1][0] != hh):
                o = (acc_ref[hh] / l_ref[hh]).T
                g = gate_ref[:, hh * d:(hh + 1) * d].astype(F32)
                o_ref[:, hh * d:(hh + 1) * d] = (o * _silu(g)).astype(o_ref.dtype)
            st = st_next

    @pl.when(j < i)
    def _():
        block(False, False)

    @pl.when(j == i)
    def _():
        block(True, True)


def fox(proj, qa, ka, *, batch, seq, q_col, k_col, v_col, gate_col, t=1024, hpb=8):
    tt = batch * seq
    nb = seq // t
    d, h = FOX_DH, FOX_HEADS
    w = hpb * d
    qb, kb, gb, vb = q_col // w, k_col // w, gate_col // w, v_col // w
    pairs = [(i, j) for i in range(nb) for j in range(i + 1)]
    it = jnp.array([p[0] for p in pairs], jnp.int32)
    jt = jnp.array([p[1] for p in pairs], jnp.int32)
    grid_spec = pltpu.PrefetchScalarGridSpec(
        num_scalar_prefetch=2,
        grid=(batch, h // hpb, len(pairs)),
        in_specs=[
            pl.BlockSpec((t, w), lambda b, hh, p, it, jt: (b * nb + it[p], qb + hh)),
            pl.BlockSpec((t, FOX_AUG), lambda b, hh, p, it, jt: (b * nb + it[p], 0)),
            pl.BlockSpec((t, w), lambda b, hh, p, it, jt: (b * nb + jt[p], kb + hh)),
            pl.BlockSpec((t, FOX_AUG), lambda b, hh, p, it, jt: (b * nb + jt[p], 0)),
            pl.BlockSpec((t, w), lambda b, hh, p, it, jt: (b * nb + jt[p], vb + hh)),
            pl.BlockSpec((t, w), lambda b, hh, p, it, jt: (b * nb + it[p], gb + hh)),
        ],
        out_specs=pl.BlockSpec((t, w), lambda b, hh, p, it, jt: (b * nb + it[p], hh)),
        scratch_shapes=[pltpu.VMEM((hpb, 1, t), F32), pltpu.VMEM((hpb, 1, t), F32), pltpu.VMEM((hpb, d, t), F32)],
    )
    return pl.pallas_call(
        functools.partial(_fox_body, tq=t, tk=t, hpb=hpb),
        grid_spec=grid_spec,
        out_shape=jax.ShapeDtypeStruct((tt, h * d), BF16),
        compiler_params=_cparams(("parallel", "parallel", "arbitrary")),
        name="fox",
    )(it, jt, proj, qa, proj, ka, proj, proj)


def _regroup_perm(tm, steps, to_grouped):
    r = lax.broadcasted_iota(jnp.int32, (tm, tm), 0)
    c = lax.broadcasted_iota(jnp.int32, (tm, tm), 1)
    nc = tm // steps
    src = (r % nc) * steps + r // nc if to_grouped else (r % steps) * nc + r // steps
    return jnp.where(c == src, 1.0, 0.0).astype(BF16)


OUTPROJ_PART_ROWS = 256


def _cast_weight_once(wb_ref, w_ref):
    @pl.when(pl.program_id(0) == 0)
    def _():
        for r0 in range(0, w_ref.shape[0], MM_SUB):
            wb_ref[r0:r0 + MM_SUB, :] = w_ref[r0:r0 + MM_SUB, :].astype(wb_ref.dtype)


def _outproj_mid_body(a1_ref, a2_ref, wf_ref, x_ref, g_ref, r_ref, h_ref, w_ref, *, steps):
    k1 = a1_ref.shape[1]
    tm, d = x_ref.shape
    tp = OUTPROJ_PART_ROWS
    nc = tp // steps
    _cast_weight_once(w_ref, wf_ref)
    parts = [slice(p * tp, (p + 1) * tp) for p in range(tm // tp)]
    accs = [x_ref[rows, :] + _dot(a1_ref[rows, :], w_ref[:k1, :]) + _dot(a2_ref[rows, :], w_ref[k1:, :])
            for rows in parts]
    perm = _regroup_perm(tp, steps, True)
    for p, rows in enumerate(parts):
        acc = accs[p]
        r_ref[rows, :] = acc
        normed = (acc * lax.rsqrt(jnp.mean(acc * acc, axis=-1, keepdims=True) + EPS) * g_ref[...]).astype(BF16)
        by_step = _dot(perm, normed).astype(h_ref.dtype)
        for s in range(steps):
            h_ref[p * nc:(p + 1) * nc, s * d:(s + 1) * d] = by_step[s * nc:(s + 1) * nc, :]


def outproj_mid(a1, a2, w, x, g, *, steps, tm=512, name="outproj_mid"):
    m, d = x.shape
    row = pl.BlockSpec((tm, d), lambda i: (i, 0))
    return pl.pallas_call(
        functools.partial(_outproj_mid_body, steps=steps),
        grid=(m // tm,),
        in_specs=[pl.BlockSpec((tm, a1.shape[1]), lambda i: (i, 0)),
                  pl.BlockSpec((tm, a2.shape[1]), lambda i: (i, 0)),
                  pl.BlockSpec(w.shape, lambda i: (0, 0), pipeline_mode=pl.Buffered(1)),
                  row,
                  pl.BlockSpec((1, d), lambda i: (0, 0))],
        out_specs=[row, pl.BlockSpec((tm // steps, steps * d), lambda i: (i, 0))],
        out_shape=[jax.ShapeDtypeStruct((m, d), F32), jax.ShapeDtypeStruct((m // steps, steps * d), BF16)],
        scratch_shapes=[pltpu.VMEM(w.shape, BF16)],
        compiler_params=_cparams(("arbitrary",)),
        name=name,
    )(a1, a2, w, x, g.reshape(1, d))


def _outproj_final_body(a_ref, wf_ref, x_ref, g_ref, o_ref, w_ref, *, steps):
    tm = x_ref.shape[0]
    e = w_ref.shape[0]
    tp = OUTPROJ_PART_ROWS
    nc = tp // steps
    _cast_weight_once(w_ref, wf_ref)
    perm = _regroup_perm(tp, steps, False)
    a_toks = []
    for p in range(tm // tp):
        by_step = jnp.concatenate([a_ref[p * nc:(p + 1) * nc, s * e:(s + 1) * e] for s in range(steps)],
                                  axis=0)
        a_toks.append(_dot(perm, by_step).astype(BF16))
    parts = [slice(p * tp, (p + 1) * tp) for p in range(tm // tp)]
    accs = [x_ref[rows, :] + _dot(a_toks[p], w_ref[...]) for p, rows in enumerate(parts)]
    for p, rows in enumerate(parts):
        acc = accs[p]
        o_ref[rows, :] = acc * lax.rsqrt(jnp.mean(acc * acc, axis=-1, keepdims=True) + EPS) * g_ref[...]


def outproj_final(a_grouped, w, x, g, *, steps, tm=512, name="outproj_final"):
    m, d = x.shape
    e = w.shape[0]
    row = pl.BlockSpec((tm, d), lambda i: (i, 0))
    return pl.pallas_call(
        functools.partial(_outproj_final_body, steps=steps),
        grid=(m // tm,),
        in_specs=[pl.BlockSpec((tm // steps, steps * e), lambda i: (i, 0)),
                  pl.BlockSpec(w.shape, lambda i: (0, 0), pipeline_mode=pl.Buffered(1)),
                  row,
                  pl.BlockSpec((1, d), lambda i: (0, 0))],
        out_specs=row,
        out_shape=jax.ShapeDtypeStruct((m, d), F32),
        scratch_shapes=[pltpu.VMEM(w.shape, BF16)],
        compiler_params=_cparams(("arbitrary",)),
        name=name,
    )(a_grouped, w, x, g.reshape(1, d))


S5_SCAN_BLK = 8
S5_GPB = 8


def _s5_entering_state(x, arow, *, cols_per_seq, n_levels):
    n = x.shape[1]
    p = S5_STATE
    nseq = n // cols_per_seq

    def to_rows(part):
        return jnp.concatenate([part[:, s * cols_per_seq:(s + 1) * cols_per_seq] for s in range(nseq)], axis=0).T

    def to_cols(rows):
        rt = rows.T
        return jnp.concatenate([rt[s * p:(s + 1) * p, :] for s in range(nseq)], axis=1)

    xr, xi = to_rows(x[:p, :]), to_rows(x[p:, :])
    pos = lax.broadcasted_iota(jnp.int32, (cols_per_seq, 1), 0)
    blk = S5_SCAN_BLK
    half = arow.shape[0] // 2
    for lvl in range(n_levels):
        sh = 1 << lvl
        keep = pos % blk >= sh
        sr = jnp.where(keep, pltpu.roll(xr, sh, axis=0), 0.0)
        si = jnp.where(keep, pltpu.roll(xi, sh, axis=0), 0.0)
        ar, ai = arow[lvl:lvl + 1, :], arow[half + lvl:half + lvl + 1, :]
        xr, xi = xr + ar * sr - ai * si, xi + ar * si + ai * sr
    pw_r, pw_i = arow[n_levels:n_levels + blk, :], arow[half + n_levels:half + n_levels + blk, :]
    out_r, out_i = [xr[:blk, :]], [xi[:blk, :]]
    for r0 in range(blk, cols_per_seq, blk):
        cr, ci = out_r[-1][blk - 1:blk, :], out_i[-1][blk - 1:blk, :]
        out_r.append(xr[r0:r0 + blk, :] + pw_r * cr - pw_i * ci)
        out_i.append(xi[r0:r0 + blk, :] + pw_r * ci + pw_i * cr)
    xr, xi = jnp.concatenate(out_r, axis=0), jnp.concatenate(out_i, axis=0)
    keep = pos >= 1
    pr = jnp.where(keep, pltpu.roll(xr, 1, axis=0), 0.0)
    pi = jnp.where(keep, pltpu.roll(xi, 1, axis=0), 0.0)
    return jnp.concatenate([to_cols(pr), to_cols(pi)], axis=0).astype(BF16)


def _s5_body(u_ref, wt_ref, nt_ref, mt_ref, arow_ref, o_ref, *, cols_per_seq, n_levels):
    t, _, n = u_ref.shape
    cg = S5_GROUP_CH
    groups = range(S5_GPB)
    ch = [slice(gi * cg, (gi + 1) * cg) for gi in groups]
    us = [u_ref[:, ch[gi], :].reshape(t * cg, n) for gi in groups]
    xs = [_dot(nt_ref[gi], us[gi]) for gi in groups]
    ys = [_dot(wt_ref[gi], us[gi]) for gi in groups]
    prevs = [_s5_entering_state(xs[gi], arow_ref[gi], cols_per_seq=cols_per_seq, n_levels=n_levels)
             for gi in groups]
    for gi in groups:
        y = ys[gi] + _dot(mt_ref[gi], prevs[gi])
        o_ref[:, ch[gi], :] = y.astype(o_ref.dtype).reshape(t, cg, n)


def s5_apply(proj_t, wt_g, nt_g, mt_g, arow, *, groups, cols_per_seq, n_levels):
    t, _, n = proj_t.shape
    cg = S5_GROUP_CH * S5_GPB
    spec3 = lambda shp: pl.BlockSpec((S5_GPB,) + shp, lambda i: (i, 0, 0))
    return pl.pallas_call(
        functools.partial(_s5_body, cols_per_seq=cols_per_seq, n_levels=n_levels),
        grid=(groups // S5_GPB,),
        in_specs=[pl.BlockSpec((t, cg, n), lambda i: (0, i, 0)),
                  spec3(wt_g.shape[1:]), spec3(nt_g.shape[1:]), spec3(mt_g.shape[1:]), spec3(arow.shape[1:])],
        out_specs=pl.BlockSpec((t, cg, n), lambda i: (0, i, 0)),
        out_shape=jax.ShapeDtypeStruct((t, groups * S5_GROUP_CH, n), BF16),
        compiler_params=_cparams(("parallel",)),
        name="s5",
    )(proj_t, wt_g, nt_g, mt_g, arow)


def s5_operators(lam_re, lam_im, log_step, b_re, b_im, c_re, c_im, d, *, n_levels, n_seq):
    hp = lax.Precision.HIGH
    g, p = lam_re.shape
    cg, t = S5_GROUP_CH, S5_T
    lr = jnp.minimum(lam_re.astype(F32), -1e-4)
    li = lam_im.astype(F32)
    step = jnp.exp(log_step.astype(F32))[:, None]
    mag = jnp.exp(lr * step)
    lb_re, lb_im = mag * jnp.cos(li * step), mag * jnp.sin(li * step)
    den = lr * lr + li * li
    nr, ni = lb_re - 1.0, lb_im
    coef_re = (nr * lr + ni * li) / den
    coef_im = (ni * lr - nr * li) / den
    bb_re = coef_re[..., None] * b_re - coef_im[..., None] * b_im
    bb_im = coef_re[..., None] * b_im + coef_im[..., None] * b_re

    def power(m):
        mm = m.astype(F32)[None, :, None]
        mg = jnp.exp(lr[:, None, :] * step[:, None, :] * mm)
        ang = li[:, None, :] * step[:, None, :] * mm
        return mg * jnp.cos(ang), mg * jnp.sin(ang)

    pw_re, pw_im = power(jnp.arange(t + 1))
    pwt_re, pwt_im = pw_re.transpose(0, 2, 1), pw_im.transpose(0, 2, 1)

    def pow_times_b(pr, pi):
        re = pr[:, :, :, None] * bb_re[:, :, None, :] - pi[:, :, :, None] * bb_im[:, :, None, :]
        im = pr[:, :, :, None] * bb_im[:, :, None, :] + pi[:, :, :, None] * bb_re[:, :, None, :]
        return re.reshape(g, p, t * cg), im.reshape(g, p, t * cg)

    pb_re, pb_im = pow_times_b(pwt_re[:, :, :t], pwt_im[:, :, :t])
    taps = (jnp.einsum('gcp,gpy->gcy', c_re.astype(F32), pb_re, precision=hp)
            - jnp.einsum('gcp,gpy->gcy', c_im.astype(F32), pb_im, precision=hp))
    y_idx = jnp.arange(t * cg)
    taps = taps + d.astype(F32).reshape(g, cg, 1) * (y_idx[None, :] == jnp.arange(cg)[:, None]).astype(F32)[None]
    place = ((y_idx[None, :, None] // cg == jnp.arange(t)[:, None, None] - y_idx[None, None, :] // cg)
             & (y_idx[None, :, None] % cg == y_idx[None, None, :] % cg)).astype(BF16)
    wt_g = jnp.einsum('gcy,tyz->gtcz', taps.astype(BF16), place,
                      preferred_element_type=F32).astype(BF16).reshape(g, t * cg, t * cg)
    n_re, n_im = pow_times_b(pwt_re[:, :, t - 1::-1], pwt_im[:, :, t - 1::-1])
    nt_g = jnp.concatenate([n_re, n_im], axis=1).astype(BF16)
    cp_re = c_re[:, None] * pw_re[:, 1:, None, :] - c_im[:, None] * pw_im[:, 1:, None, :]
    cp_im = c_re[:, None] * pw_im[:, 1:, None, :] + c_im[:, None] * pw_re[:, 1:, None, :]
    mt_g = jnp.concatenate([cp_re, -cp_im], axis=-1).reshape(g, t * cg, 2 * p).astype(BF16)
    ar, ai = power(t * jnp.concatenate([2 ** jnp.arange(n_levels), jnp.arange(1, S5_SCAN_BLK + 1)]))
    arow = jnp.concatenate([jnp.tile(ar, (1, 1, n_seq)), jnp.tile(ai, (1, 1, n_seq))], axis=1)
    return wt_g, nt_g, mt_g, arow


def _gelu_tanh(y):
    return 0.5 * y * (1.0 + jnp.tanh(math.sqrt(2.0 / math.pi) * (y + 0.044715 * (y * y * y))))


GLU_PARTS = 2


def _glu_body(y_ref, w_ref, b_ref, gate_ref, o_ref, z_ref, zb_ref, wt_ref):
    @pl.when((pl.program_id(0) == 0) & (pl.program_id(1) == 0))
    def _():
        _transpose_into(wt_ref, w_ref)

    tm = y_ref.shape[1]
    parts = [slice(c0, c0 + tm // GLU_PARTS) for c0 in range(0, tm, tm // GLU_PARTS)]
    for cols in parts:
        z = _gelu_tanh(y_ref[:, cols].astype(F32))
        z_ref[:, cols] = z
        zb_ref[:, cols] = z.astype(BF16)
    for cols in parts:
        for f0 in range(0, o_ref.shape[1], MM_SUB):
            rows = slice(f0, f0 + MM_SUB)
            lin = _dot(wt_ref[rows, :], zb_ref[:, cols]) + b_ref[rows, :]
            out = z_ref[rows, cols] * _sigmoid(lin) * _silu(gate_ref[rows, cols].astype(F32))
            o_ref[cols, rows] = out.T.astype(o_ref.dtype)


def glu_gate(y_t, w, b, proj_t, *, gate_row, tm=512):
    s, e, n = y_t.shape
    gb = gate_row // e
    return pl.pallas_call(
        _glu_body,
        grid=(s, n // tm),
        in_specs=[pl.BlockSpec((None, e, tm), lambda t, i: (t, 0, i)),
                  pl.BlockSpec((e, e), lambda t, i: (0, 0), pipeline_mode=pl.Buffered(1)),
                  pl.BlockSpec((e, 1), lambda t, i: (0, 0), pipeline_mode=pl.Buffered(1)),
                  pl.BlockSpec((None, e, tm), lambda t, i: (t, gb, i))],
        out_specs=pl.BlockSpec((tm, e), lambda t, i: (i, t)),
        out_shape=jax.ShapeDtypeStruct((n, s * e), BF16),
        scratch_shapes=[pltpu.VMEM((e, tm), F32), pltpu.VMEM((e, tm), BF16), pltpu.VMEM((e, e), BF16)],
        compiler_params=_cparams(("arbitrary", "arbitrary")),
        name="glu",
    )(y_t, w, b.reshape(e, 1), proj_t)


def kernel(x, ab_norm_g, ab_w_in, gla_alpha_up, gla_alpha_b, gla_head_g, fox_f_b, ab_w_out, c_norm_g, c_w_in, s5_lambda_re, s5_lambda_im, s5_log_step, s5_b_re, s5_b_im, s5_c_re, s5_c_im, s5_d, glu_w, glu_b, c_w_out, final_norm_g):
    batch, seq, d = x.shape
    t = batch * seq
    x2 = x.reshape(t, d)

    hk, hv, fw = GLA_HEADS * GLA_DK, GLA_HEADS * GLA_DV, FOX_HEADS * FOX_DH
    sizes = (hk, hk, hv, GLA_LOWRANK, hv, fw, fw, fw, FOX_HEADS, fw)
    offs = [0]
    for s in sizes:
        offs.append(offs[-1] + s)
    w_gla, w_fox, w_small = ab_weight_prep(jnp.swapaxes(ab_w_in, 1, 2), offs)
    c_gq, c_gk, c_gv, c_gg = 0, hk, 2 * hk, 2 * hk + hv
    c_fq, c_fk, c_fv, c_fg = 0, fw, 2 * fw, 3 * fw

    h0, proj_g, small = norm_proj(x2, ab_norm_g[0], w_gla, w_small, name="ab_in")
    proj_f = matmul(h0, w_fox, name="ab_in_fox")

    o_gla = gla(proj_g, small, gla_alpha_up[0], gla_alpha_b[0], gla_head_g[0],
                batch=batch, seq=seq, q_col=c_gq, k_col=c_gk, v_col=c_gv, gate_col=c_gg)

    fb_row = jnp.zeros((1, 128), F32).at[0, GLA_LOWRANK:GLA_LOWRANK + FOX_HEADS].set(fox_f_b[0])
    qa, ka = fox_gate(small, fb_row, batch=batch, seq=seq, col0=GLA_LOWRANK)
    o_fox = fox(proj_f, qa, ka, batch=batch, seq=seq, q_col=c_fq, k_col=c_fk, v_col=c_fv, gate_col=c_fg)

    x1, h1g = outproj_mid(o_gla, o_fox, ab_w_out[0], x2, c_norm_g[0], steps=S5_T, name="ab_out")

    e = d
    groups = e // S5_GROUP_CH
    n_chunks = t // S5_T
    cols_per_seq = seq // S5_T
    n_levels = S5_SCAN_BLK.bit_length() - 1
    proj_t = matmul_t_grouped(c_w_in[0], h1g, S5_T, BF16, name="c_in")
    ops = s5_operators(s5_lambda_re[0], s5_lambda_im[0], s5_log_step[0], s5_b_re[0], s5_b_im[0],
                       s5_c_re[0], s5_c_im[0], s5_d[0], n_levels=n_levels, n_seq=batch)
    y_t = s5_apply(proj_t, *ops, groups=groups, cols_per_seq=cols_per_seq, n_levels=n_levels)
    zz = glu_gate(y_t, glu_w[0], glu_b[0], proj_t, gate_row=e)
    out = outproj_final(zz, c_w_out[0], x1, final_norm_g, steps=S5_T, name="c_out")
    return out.reshape(batch, seq, d)
```

```python
import functools
import math

import jax
import jax.numpy as jnp
from jax import lax
from jax.experimental import pallas as pl
from jax.experimental.pallas import tpu as pltpu

EPS = 1e-6
F32 = jnp.float32
BF16 = jnp.bfloat16

GLA_HEADS = 4
GLA_DK = 128
GLA_DV = 256
GLA_LOWRANK = 16
GLA_TAU = 16.0
GLA_CHUNK = 64
GLA_SUB = 8
GLA_UNROLL = 8
FOX_HEADS = 8
FOX_DH = 128
FOX_AUG = 128
FOX_AUG_GROUP = 8
S5_GROUP_CH = 16
S5_STATE = 64
S5_T = 16

VMEM_LIMIT = 58 * 1024 * 1024

NT_DIMS = (((1,), (1,)), ((), ()))
TN_DIMS = (((0,), (0,)), ((), ()))
LOG2E = math.log2(math.e)


def _cparams(sem):
    return pltpu.CompilerParams(dimension_semantics=sem, vmem_limit_bytes=VMEM_LIMIT)


def _dot(a, b):
    return jnp.dot(a, b, preferred_element_type=F32)


def _dot_nt(a, b):
    return lax.dot_general(a, b, NT_DIMS, preferred_element_type=F32)


def _dot_tn(a, b):
    return lax.dot_general(a, b, TN_DIMS, preferred_element_type=F32)


def _log_sigmoid(z):
    return jnp.minimum(z, 0.0) - jnp.log(1.0 + jnp.exp(-jnp.abs(z)))


def _sigmoid(z):
    return 0.5 * jnp.tanh(0.5 * z) + 0.5


def _silu(z):
    return z * _sigmoid(z)


def _split3(x):
    hi = x.astype(BF16)
    r1 = x - hi.astype(F32)
    mid = r1.astype(BF16)
    lo = (r1 - mid.astype(F32)).astype(BF16)
    return hi, mid, lo


def _norm_proj_body(x_ref, g_ref, wt_ref, ws_ref, h_ref, o_ref, s_ref):
    x = x_ref[...]
    ms = jnp.mean(x * x, axis=-1, keepdims=True)
    h = (x * lax.rsqrt(ms + EPS) * g_ref[...]).astype(BF16)
    h_ref[...] = h
    for c0 in range(0, o_ref.shape[1], MM_SUB):
        o_ref[:, c0:c0 + MM_SUB] = _dot_nt(h, wt_ref[c0:c0 + MM_SUB, :]).astype(o_ref.dtype)
    s_ref[...] = _dot_nt(h, ws_ref[...])


def norm_proj(x, g, wt, wt_small, tm=512, name="norm_proj"):
    m, d = x.shape
    n = wt.shape[0]
    ns = wt_small.shape[0]
    return pl.pallas_call(
        _norm_proj_body,
        grid=(m // tm,),
        in_specs=[pl.BlockSpec((tm, d), lambda i: (i, 0)),
                  pl.BlockSpec((1, d), lambda i: (0, 0)),
                  pl.BlockSpec((n, d), lambda i: (0, 0), pipeline_mode=pl.Buffered(1)),
                  pl.BlockSpec((ns, d), lambda i: (0, 0))],
        out_specs=[pl.BlockSpec((tm, d), lambda i: (i, 0)),
                   pl.BlockSpec((tm, n), lambda i: (i, 0)),
                   pl.BlockSpec((tm, ns), lambda i: (i, 0))],
        out_shape=[jax.ShapeDtypeStruct((m, d), BF16), jax.ShapeDtypeStruct((m, n), BF16),
                   jax.ShapeDtypeStruct((m, ns), F32)],
        compiler_params=_cparams(("parallel",)),
        name=name,
    )(x, g.reshape(1, d), wt, wt_small)


def _mm_body(a_ref, wt_ref, o_ref):
    a = a_ref[...]
    for c0 in range(0, o_ref.shape[1], MM_SUB):
        o_ref[:, c0:c0 + MM_SUB] = _dot_nt(a, wt_ref[c0:c0 + MM_SUB, :]).astype(o_ref.dtype)


def matmul(a, wt, tm=512, name="matmul"):
    m, k = a.shape
    n = wt.shape[0]
    return pl.pallas_call(
        _mm_body,
        grid=(m // tm,),
        in_specs=[pl.BlockSpec((tm, k), lambda i: (i, 0)),
                  pl.BlockSpec((n, k), lambda i: (0, 0), pipeline_mode=pl.Buffered(1))],
        out_specs=pl.BlockSpec((tm, n), lambda i: (i, 0)),
        out_shape=jax.ShapeDtypeStruct((m, n), BF16),
        compiler_params=_cparams(("parallel",)),
        name=name,
    )(a, wt)


def _ab_wprep_body(w_ref, og_ref, of_ref, s_ref, *, offs):
    w = w_ref[0]
    seg = lambda n: w[offs[n]:offs[n + 1], :]
    og_ref[...] = jnp.concatenate([w[:offs[3], :], seg(4)], axis=0).astype(og_ref.dtype)
    of_ref[...] = jnp.concatenate([seg(5) * (FOX_DH ** -0.5 * LOG2E), w[offs[6]:offs[8], :], seg(9)],
                                  axis=0).astype(of_ref.dtype)
    pad = s_ref.shape[0] - (offs[4] - offs[3]) - (offs[9] - offs[8])
    s_ref[...] = jnp.concatenate([seg(3), seg(8), jnp.zeros((pad, w.shape[1]), F32)], axis=0).astype(s_ref.dtype)


def ab_weight_prep(w_t3, offs, tk=256):
    _, n_in, d = w_t3.shape
    n_gla = offs[3] + offs[5] - offs[4]
    n_fox = offs[8] - offs[5] + offs[10] - offs[9]
    return pl.pallas_call(
        functools.partial(_ab_wprep_body, offs=tuple(offs)),
        grid=(d // tk,),
        in_specs=[pl.BlockSpec((1, n_in, tk), lambda i: (0, 0, i))],
        out_specs=[pl.BlockSpec((n_gla, tk), lambda i: (0, i)), pl.BlockSpec((n_fox, tk), lambda i: (0, i)),
                   pl.BlockSpec((128, tk), lambda i: (0, i))],
        out_shape=[jax.ShapeDtypeStruct((n_gla, d), BF16), jax.ShapeDtypeStruct((n_fox, d), BF16),
                   jax.ShapeDtypeStruct((128, d), BF16)],
        compiler_params=_cparams(("parallel",)),
        name="ab_wprep",
    )(w_t3)


MM_SUB = 512


def _transpose_into(wt_ref, w_ref):
    for c0 in range(0, w_ref.shape[1], MM_SUB):
        wt_ref[c0:c0 + MM_SUB, :] = w_ref[:, c0:c0 + MM_SUB].astype(wt_ref.dtype).T


def _mm_t_body(w_ref, a_ref, o_ref, wt_ref):
    @pl.when((pl.program_id(1) == 0) & (pl.program_id(2) == 0))
    def _():
        _transpose_into(wt_ref, w_ref)

    a = a_ref[...]
    for r0 in range(0, o_ref.shape[0], MM_SUB):
        o_ref[r0:r0 + MM_SUB, :] = _dot_nt(wt_ref[r0:r0 + MM_SUB, :], a).astype(o_ref.dtype)


def matmul_t_grouped(w, a2, steps, out_dtype, tm=512, tn=2048, name="matmul_tg"):
    k, nf = w.shape
    n = a2.shape[0]
    return pl.pallas_call(
        _mm_t_body,
        grid=(nf // tn, steps, n // tm),
        in_specs=[pl.BlockSpec((k, tn), lambda j, t, i: (0, j), pipeline_mode=pl.Buffered(1)),
                  pl.BlockSpec((tm, k), lambda j, t, i: (i, t))],
        out_specs=pl.BlockSpec((None, tn, tm), lambda j, t, i: (t, j, i)),
        out_shape=jax.ShapeDtypeStruct((steps, nf, n), out_dtype),
        scratch_shapes=[pltpu.VMEM((tn, k), BF16)],
        compiler_params=_cparams(("arbitrary", "arbitrary", "arbitrary")),
        name=name,
    )(w, a2)


def _gla_body(q_ref, k_ref, v_ref, glr_ref, aup_ref, ab_ref, gate_ref, hg_ref, o_ref,
              s_ref, b_ref, kf_ref, kts_ref, dec_ref, *, tb, hpb):
    ib = pl.program_id(2)
    C, SB, dk, dv = GLA_CHUNK, GLA_SUB, GLA_DK, GLA_DV

    @pl.when(ib == 0)
    def _():
        s_ref[...] = jnp.zeros_like(s_ref)

    glr = glr_ref[:, :GLA_LOWRANK].astype(BF16)
    tw = 2 * C
    r = lax.broadcasted_iota(jnp.int32, (tw, tw), 0)
    c = lax.broadcasted_iota(jnp.int32, (tw, tw), 1)
    tri = jnp.where((r // C == c // C) & (c <= r), 1.0, 0.0).astype(BF16)
    nc = tb // C
    heads = range(hpb)
    kcs = [slice(hh * dk, (hh + 1) * dk) for hh in heads]
    zs = [_dot(glr, aup_ref[:, kcs[hh]].astype(BF16)) + ab_ref[:, kcs[hh]] for hh in heads]
    b_blks = []
    for hh in heads:
        la = _log_sigmoid(zs[hh]) * (1.0 / GLA_TAU)
        hi = la.astype(BF16)
        lo = (la - hi.astype(F32)).astype(BF16)
        b_blks.append(jnp.concatenate(
            [_dot(tri, hi[r0:r0 + tw]) + _dot(tri, lo[r0:r0 + tw]) for r0 in range(0, tb, tw)], axis=0))
    for hh in heads:
        b_blk = b_blks[hh]
        k_blk = k_ref[:, kcs[hh]].astype(F32)
        b_ref[hh] = b_blk
        kf_ref[hh] = k_blk
        b3 = b_blk.reshape(nc, C, dk)
        b_end = b3[:, C - 1:C, :]
        k_end = (k_blk.reshape(nc, C, dk) * jnp.exp(b_end - b3)).reshape(tb, dk)
        kt_end = k_end.T.astype(BF16)
        dec_t = jnp.exp(b_end.reshape(nc, dk)).T
        for ci in range(nc):
            kts_ref[hh, ci] = kt_end[:, ci * C:(ci + 1) * C]
            dec_ref[hh, ci] = jnp.broadcast_to(dec_t[:, ci:ci + 1], (dk, 128))

    row = lax.broadcasted_iota(jnp.int32, (C, 1), 0)
    rowi = lax.broadcasted_iota(jnp.int32, (C, C), 0)
    coli = lax.broadcasted_iota(jnp.int32, (C, C), 1)
    lane_c = lax.broadcasted_iota(jnp.int32, (SB, C), 1)
    sub_r = lax.broadcasted_iota(jnp.int32, (SB, 1), 0)
    ones = jnp.ones((GLA_DK, C), BF16)
    neg = -jnp.inf
    scale = GLA_DK ** -0.5

    def stage_state(ci, r0, hh):
        kc = slice(hh * dk, (hh + 1) * dk)
        vc = slice(hh * dv, (hh + 1) * dv)
        q = q_ref[pl.ds(r0, C), kc].astype(F32) * scale
        k = kf_ref[hh, pl.ds(r0, C), :]
        b = b_ref[hh, pl.ds(r0, C), :]
        v = v_ref[pl.ds(r0, C), vc]

        s = s_ref[hh]
        o = _dot((q * jnp.exp(b)).astype(BF16), s.astype(BF16))
        dec = dec_ref[hh, ci]
        s_ref[hh] = s * jnp.concatenate([dec] * (dv // 128), axis=1) + _dot(kts_ref[hh, ci], v)

        attn = jnp.zeros((C, C), F32)
        h = C // 2
        while h >= SB:
            ref = jnp.broadcast_to(b.reshape(C // (2 * h), 2 * h, GLA_DK)[:, h - 1:h, :],
                                   (C // (2 * h), 2 * h, GLA_DK)).reshape(C, GLA_DK)
            upper = (row // h) % 2 == 1
            q_h = q * jnp.exp(jnp.where(upper, b - ref, neg))
            k_h = k * jnp.exp(jnp.where(upper, neg, ref - b))
            a_h = _dot_nt(q_h.astype(BF16), k_h.astype(BF16))
            attn = attn + (a_h if 2 * h == C else jnp.where(rowi // (2 * h) == coli // (2 * h), a_h, 0.0))
            h //= 2
        return q, b, v, o, attn

    def stage_diag(r0, hh, q, b):
        zs = []
        for bi in range(C // SB):
            s0 = bi * SB
            q_i = q[s0:s0 + SB, :]
            b_i = b[s0:s0 + SB, :]
            for j in range(SB):
                k_j = kf_ref[hh, pl.ds(r0 + (s0 + j), 1), :]
                b_j = b_ref[hh, pl.ds(r0 + (s0 + j), 1), :]
                zs.append(q_i * k_j * jnp.exp(jnp.where(sub_r >= j, b_i - b_j, neg)))
        return _dot(jnp.concatenate(zs, axis=0).astype(BF16), ones)

    def stage_out(r0, hh, v, o, attn, zsum):
        vc = slice(hh * dv, (hh + 1) * dv)
        diag = []
        for bi in range(C // SB):
            acc = jnp.zeros((SB, C), F32)
            for j in range(SB):
                n0 = (bi * SB + j) * SB
                acc = acc + jnp.where(lane_c == bi * SB + j, zsum[n0:n0 + SB, :], 0.0)
            diag.append(acc)
        attn = attn + jnp.concatenate(diag, axis=0)
        o = o + _dot(attn.astype(BF16), v)

        o = o * lax.rsqrt(jnp.mean(o * o, axis=-1, keepdims=True) + EPS) * hg_ref[:, vc]
        g = gate_ref[pl.ds(r0, C), vc].astype(F32)
        o_ref[pl.ds(r0, C), vc] = (o * _silu(g)).astype(o_ref.dtype)

    def chunk(ci, carry):
        r0 = pl.multiple_of(ci * C, C)
        heads = range(hpb)
        st = [stage_state(ci, r0, hh) for hh in heads]
        zsums = [stage_diag(r0, hh, st[hh][0], st[hh][1]) for hh in heads]
        for hh in heads:
            _, _, v, o, attn = st[hh]
            stage_out(r0, hh, v, o, attn, zsums[hh])
        return carry

    lax.fori_loop(0, tb // C, chunk, 0, unroll=GLA_UNROLL)


def gla(proj, small, alpha_up, alpha_b, head_g, *, batch, seq, q_col, k_col, v_col, gate_col, tb=512, hpb=4):
    t = batch * seq
    nb = seq // tb
    dk, dv, h = GLA_DK * hpb, GLA_DV * hpb, GLA_HEADS // hpb
    qb, kb, vb, gb = q_col // dk, k_col // dk, v_col // dv, gate_col // dv
    tok = lambda b, hh, i: b * nb + i
    nc = tb // GLA_CHUNK
    return pl.pallas_call(
        functools.partial(_gla_body, tb=tb, hpb=hpb),
        grid=(batch, h, nb),
        in_specs=[
            pl.BlockSpec((tb, dk), lambda b, hh, i: (tok(b, hh, i), qb + hh)),
            pl.BlockSpec((tb, dk), lambda b, hh, i: (tok(b, hh, i), kb + hh)),
            pl.BlockSpec((tb, dv), lambda b, hh, i: (tok(b, hh, i), vb + hh)),
            pl.BlockSpec((tb, 128), lambda b, hh, i: (tok(b, hh, i), 0)),
            pl.BlockSpec((GLA_LOWRANK, dk), lambda b, hh, i: (0, hh)),
            pl.BlockSpec((1, dk), lambda b, hh, i: (0, hh)),
            pl.BlockSpec((tb, dv), lambda b, hh, i: (tok(b, hh, i), gb + hh)),
            pl.BlockSpec((1, dv), lambda b, hh, i: (0, hh)),
        ],
        out_specs=pl.BlockSpec((tb, dv), lambda b, hh, i: (tok(b, hh, i), hh)),
        out_shape=jax.ShapeDtypeStruct((t, GLA_HEADS * GLA_DV), BF16),
        scratch_shapes=[pltpu.VMEM((hpb, GLA_DK, GLA_DV), F32), pltpu.VMEM((hpb, tb, GLA_DK), F32),
                        pltpu.VMEM((hpb, tb, GLA_DK), F32),
                        pltpu.VMEM((hpb, nc, GLA_DK, GLA_CHUNK), BF16),
                        pltpu.VMEM((hpb, nc, GLA_DK, 128), F32)],
        compiler_params=_cparams(("parallel", "parallel", "arbitrary")),
        name="gla",
    )(proj, proj, proj, small, alpha_up, alpha_b.reshape(1, -1), proj, head_g.reshape(1, -1))


FOX_GATE_UNROLL = 4


def _fox_gate_body(s_ref, fb_ref, qa_ref, ka_ref, cs_ref, off_ref, *, blk, col0):
    n = s_ref.shape[0] // blk
    r = lax.broadcasted_iota(jnp.int32, (blk, blk), 0)
    c = lax.broadcasted_iota(jnp.int32, (blk, blk), 1)
    tri = jnp.where(c <= r, 1.0, 0.0).astype(BF16)
    lanes = s_ref.shape[1]
    pr = lax.broadcasted_iota(jnp.int32, (3 * lanes, FOX_AUG), 0)
    pc = lax.broadcasted_iota(jnp.int32, (3 * lanes, FOX_AUG), 1)
    head, piece = pr % lanes - col0, pr // lanes
    is_head = (head >= 0) & (head < FOX_HEADS)
    sel_q = jnp.where(is_head & (pc == head * FOX_AUG_GROUP + piece), 1.0, 0.0).astype(BF16)
    sel_k = jnp.where(is_head & (pc == head * FOX_AUG_GROUP + 3 + piece), -1.0, 0.0).astype(BF16)
    lane = lax.broadcasted_iota(jnp.int32, (1, FOX_AUG), 1)
    used = lane < FOX_HEADS * FOX_AUG_GROUP
    one_q = jnp.where(used & (lane % FOX_AUG_GROUP >= 3) & (lane % FOX_AUG_GROUP < 6), 1.0, 0.0)
    one_k = jnp.where(used & (lane % FOX_AUG_GROUP < 3), 1.0, 0.0)

    def local(i, carry):
        r0 = pl.multiple_of(i * blk, blk)
        lf = _log_sigmoid(s_ref[pl.ds(r0, blk), :] + fb_ref[...])
        hi, mid, lo = _split3(lf)
        cs_ref[pl.ds(r0, blk), :] = _dot(tri, hi) + _dot(tri, mid) + _dot(tri, lo)
        return carry

    lax.fori_loop(0, n, local, 0, unroll=FOX_GATE_UNROLL)

    off = jnp.zeros((1, lanes), F32)
    for i in range(n):
        off_ref[i:i + 1, :] = off
        off = off + cs_ref[(i + 1) * blk - 1:(i + 1) * blk, :]

    def place(i, carry):
        r0 = pl.multiple_of(i * blk, blk)
        cs = cs_ref[pl.ds(r0, blk), :] + off_ref[pl.ds(i, 1), :]
        pieces = jnp.concatenate(_split3(cs * LOG2E), axis=1)
        qa_ref[pl.ds(r0, blk), :] = (_dot(pieces, sel_q) + one_q).astype(BF16)
        ka_ref[pl.ds(r0, blk), :] = (_dot(pieces, sel_k) + one_k).astype(BF16)
        return carry

    lax.fori_loop(0, n, place, 0, unroll=FOX_GATE_UNROLL)


def fox_gate(small, fb_row, *, batch, seq, col0, blk=256):
    shp = jax.ShapeDtypeStruct((batch * seq, FOX_AUG), BF16)
    spec = pl.BlockSpec((seq, FOX_AUG), lambda b: (b, 0))
    return pl.pallas_call(
        functools.partial(_fox_gate_body, blk=blk, col0=col0),
        grid=(batch,),
        in_specs=[pl.BlockSpec((seq, 128), lambda b: (b, 0)),
                  pl.BlockSpec((1, 128), lambda b: (0, 0))],
        out_specs=[spec, spec],
        out_shape=[shp, shp],
        scratch_shapes=[pltpu.VMEM((seq, 128), F32), pltpu.VMEM((seq // blk, 128), F32)],
        compiler_params=_cparams(("parallel",)),
        name="fox_gate",
    )(small, fb_row)


def _fox_body(it_ref, jt_ref, q_ref, qa_ref, k_ref, ka_ref, v_ref, gate_ref, o_ref,
              m_ref, l_ref, acc_ref, *, tq, tk, hpb):
    p = pl.program_id(2)
    i = it_ref[p]
    j = jt_ref[p]
    d = FOX_DH

    @pl.when(j == 0)
    def _():
        m_ref[...] = jnp.full_like(m_ref, -jnp.inf)
        l_ref[...] = jnp.zeros_like(l_ref)
        acc_ref[...] = jnp.zeros_like(acc_ref)

    aug_lane = lax.broadcasted_iota(jnp.int32, (1, FOX_AUG), 1)

    def scores(item, masked):
        hh, ks, qs = item
        mine = aug_lane // FOX_AUG_GROUP == pl.program_id(1) * hpb + hh
        zero = jnp.zeros((), BF16)
        q_aug = jnp.concatenate([q_ref[qs, hh * d:(hh + 1) * d], jnp.where(mine, qa_ref[qs, :], zero)], axis=1)
        k_aug = jnp.concatenate([k_ref[ks, hh * d:(hh + 1) * d], jnp.where(mine, ka_ref[ks, :], zero)], axis=1)
        st = _dot_nt(k_aug, q_aug)
        if masked:
            kr = lax.broadcasted_iota(jnp.int32, st.shape, 0) + ks.start
            qc = lax.broadcasted_iota(jnp.int32, st.shape, 1) + qs.start
            st = jnp.where(qc >= kr, st, -jnp.inf)
        return st

    def absorb(item, st):
        hh, ks, qs = item
        m_old = m_ref[hh, :, qs]
        m_new = jnp.maximum(m_old, jnp.max(st, axis=0, keepdims=True))
        alpha = jnp.exp2(m_old - m_new)
        pt = jnp.exp2(st - m_new)
        l_ref[hh, :, qs] = alpha * l_ref[hh, :, qs] + jnp.sum(pt, axis=0, keepdims=True)
        acc_ref[hh, :, qs] = (alpha * acc_ref[hh, :, qs]
                              + _dot_tn(v_ref[ks, hh * d:(hh + 1) * d], pt.astype(BF16)))
        m_ref[hh, :, qs] = m_new

    def block(masked, finish):
        full = slice(0, tq)
        halves = [(slice(0, tk // 2), full), (slice(tk // 2, tk), slice(tq // 2, tq))] if masked else [(slice(0, tk), full)]
        items = [(hh, ks, qs) for hh in range(hpb) for ks, qs in halves]
        st = scores(items[0], masked)
        for n, item in enumerate(items):
            st_next = scores(items[n + 1], masked) if n + 1 < len(items) else None
            absorb(item, st)
            hh = item[0]
            if finish and (n + 1 == len(items) or items[n + 1][0] != hh):
                o = (acc_ref[hh] / l_ref[hh]).T
                g = gate_ref[:, hh * d:(hh + 1) * d].astype(F32)
                o_ref[:, hh * d:(hh + 1) * d] = (o * _silu(g)).astype(o_ref.dtype)
            st = st_next

    @pl.when(j < i)
    def _():
        block(False, False)

    @pl.when(j == i)
    def _():
        block(True, True)


def fox(proj, qa, ka, *, batch, seq, q_col, k_col, v_col, gate_col, t=1024, hpb=8):
    tt = batch * seq
    nb = seq // t
    d, h = FOX_DH, FOX_HEADS
    w = hpb * d
    qb, kb, gb, vb = q_col // w, k_col // w, gate_col // w, v_col // w
    pairs = [(i, j) for i in range(nb) for j in range(i + 1)]
    it = jnp.array([p[0] for p in pairs], jnp.int32)
    jt = jnp.array([p[1] for p in pairs], jnp.int32)
    grid_spec = pltpu.PrefetchScalarGridSpec(
        num_scalar_prefetch=2,
        grid=(batch, h // hpb, len(pairs)),
        in_specs=[
            pl.BlockSpec((t, w), lambda b, hh, p, it, jt: (b * nb + it[p], qb + hh)),
            pl.BlockSpec((t, FOX_AUG), lambda b, hh, p, it, jt: (b * nb + it[p], 0)),
            pl.BlockSpec((t, w), lambda b, hh, p, it, jt: (b * nb + jt[p], kb + hh)),
            pl.BlockSpec((t, FOX_AUG), lambda b, hh, p, it, jt: (b * nb + jt[p], 0)),
            pl.BlockSpec((t, w), lambda b, hh, p, it, jt: (b * nb + jt[p], vb + hh)),
            pl.BlockSpec((t, w), lambda b, hh, p, it, jt: (b * nb + it[p], gb + hh)),
        ],
        out_specs=pl.BlockSpec((t, w), lambda b, hh, p, it, jt: (b * nb + it[p], hh)),
        scratch_shapes=[pltpu.VMEM((hpb, 1, t), F32), pltpu.VMEM((hpb, 1, t), F32), pltpu.VMEM((hpb, d, t), F32)],
    )
    return pl.pallas_call(
        functools.partial(_fox_body, tq=t, tk=t, hpb=hpb),
        grid_spec=grid_spec,
        out_shape=jax.ShapeDtypeStruct((tt, h * d), BF16),
        compiler_params=_cparams(("parallel", "parallel", "arbitrary")),
        name="fox",
    )(it, jt, proj, qa, proj, ka, proj, proj)


def _regroup_perm(tm, steps, to_grouped):
    r = lax.broadcasted_iota(jnp.int32, (tm, tm), 0)
    c = lax.broadcasted_iota(jnp.int32, (tm, tm), 1)
    nc = tm // steps
    src = (r % nc) * steps + r // nc if to_grouped else (r % steps) * nc + r // steps
    return jnp.where(c == src, 1.0, 0.0).astype(BF16)


OUTPROJ_PART_ROWS = 256


def _cast_weight_once(wb_ref, w_ref):
    @pl.when(pl.program_id(0) == 0)
    def _():
        for r0 in range(0, w_ref.shape[0], MM_SUB):
            wb_ref[r0:r0 + MM_SUB, :] = w_ref[r0:r0 + MM_SUB, :].astype(wb_ref.dtype)


def _outproj_mid_body(a1_ref, a2_ref, wf_ref, x_ref, g_ref, r_ref, h_ref, w_ref, *, steps):
    k1 = a1_ref.shape[1]
    tm, d = x_ref.shape
    tp = OUTPROJ_PART_ROWS
    nc = tp // steps
    _cast_weight_once(w_ref, wf_ref)
    parts = [slice(p * tp, (p + 1) * tp) for p in range(tm // tp)]
    accs = [x_ref[rows, :] + _dot(a1_ref[rows, :], w_ref[:k1, :]) + _dot(a2_ref[rows, :], w_ref[k1:, :])
            for rows in parts]
    perm = _regroup_perm(tp, steps, True)
    for p, rows in enumerate(parts):
        acc = accs[p]
        r_ref[rows, :] = acc
        normed = (acc * lax.rsqrt(jnp.mean(acc * acc, axis=-1, keepdims=True) + EPS) * g_ref[...]).astype(BF16)
        by_step = _dot(perm, normed).astype(h_ref.dtype)
        for s in range(steps):
            h_ref[p * nc:(p + 1) * nc, s * d:(s + 1) * d] = by_step[s * nc:(s + 1) * nc, :]


def outproj_mid(a1, a2, w, x, g, *, steps, tm=512, name="outproj_mid"):
    m, d = x.shape
    row = pl.BlockSpec((tm, d), lambda i: (i, 0))
    return pl.pallas_call(
        functools.partial(_outproj_mid_body, steps=steps),
        grid=(m // tm,),
        in_specs=[pl.BlockSpec((tm, a1.shape[1]), lambda i: (i, 0)),
                  pl.BlockSpec((tm, a2.shape[1]), lambda i: (i, 0)),
                  pl.BlockSpec(w.shape, lambda i: (0, 0), pipeline_mode=pl.Buffered(1)),
                  row,
                  pl.BlockSpec((1, d), lambda i: (0, 0))],
        out_specs=[row, pl.BlockSpec((tm // steps, steps * d), lambda i: (i, 0))],
        out_shape=[jax.ShapeDtypeStruct((m, d), F32), jax.ShapeDtypeStruct((m // steps, steps * d), BF16)],
        scratch_shapes=[pltpu.VMEM(w.shape, BF16)],
        compiler_params=_cparams(("arbitrary",)),
        name=name,
    )(a1, a2, w, x, g.reshape(1, d))


def _outproj_final_body(a_ref, wf_ref, x_ref, g_ref, o_ref, w_ref, *, steps):
    tm = x_ref.shape[0]
    e = w_ref.shape[0]
    tp = OUTPROJ_PART_ROWS
    nc = tp // steps
    _cast_weight_once(w_ref, wf_ref)
    perm = _regroup_perm(tp, steps, False)
    a_toks = []
    for p in range(tm // tp):
        by_step = jnp.concatenate([a_ref[p * nc:(p + 1) * nc, s * e:(s + 1) * e] for s in range(steps)],
                                  axis=0)
        a_toks.append(_dot(perm, by_step).astype(BF16))
    parts = [slice(p * tp, (p + 1) * tp) for p in range(tm // tp)]
    accs = [x_ref[rows, :] + _dot(a_toks[p], w_ref[...]) for p, rows in enumerate(parts)]
    for p, rows in enumerate(parts):
        acc = accs[p]
        o_ref[rows, :] = acc * lax.rsqrt(jnp.mean(acc * acc, axis=-1, keepdims=True) + EPS) * g_ref[...]


def outproj_final(a_grouped, w, x, g, *, steps, tm=512, name="outproj_final"):
    m, d = x.shape
    e = w.shape[0]
    row = pl.BlockSpec((tm, d), lambda i: (i, 0))
    return pl.pallas_call(
        functools.partial(_outproj_final_body, steps=steps),
        grid=(m // tm,),
        in_specs=[pl.BlockSpec((tm // steps, steps * e), lambda i: (i, 0)),
                  pl.BlockSpec(w.shape, lambda i: (0, 0), pipeline_mode=pl.Buffered(1)),
                  row,
                  pl.BlockSpec((1, d), lambda i: (0, 0))],
        out_specs=row,
        out_shape=jax.ShapeDtypeStruct((m, d), F32),
        scratch_shapes=[pltpu.VMEM(w.shape, BF16)],
        compiler_params=_cparams(("arbitrary",)),
        name=name,
    )(a_grouped, w, x, g.reshape(1, d))


S5_SCAN_BLK = 8
S5_GPB = 16


def _s5_entering_state(x, arow, *, cols_per_seq, n_levels):
    n = x.shape[1]
    p = S5_STATE
    nseq = n // cols_per_seq

    def to_rows(part):
        return jnp.concatenate([part[:, s * cols_per_seq:(s + 1) * cols_per_seq] for s in range(nseq)], axis=0).T

    def to_cols(rows):
        rt = rows.T
        return jnp.concatenate([rt[s * p:(s + 1) * p, :] for s in range(nseq)], axis=1)

    xr, xi = to_rows(x[:p, :]), to_rows(x[p:, :])
    pos = lax.broadcasted_iota(jnp.int32, (cols_per_seq, 1), 0)
    blk = S5_SCAN_BLK
    half = arow.shape[0] // 2
    for lvl in range(n_levels):
        sh = 1 << lvl
        keep = pos % blk >= sh
        sr = jnp.where(keep, pltpu.roll(xr, sh, axis=0), 0.0)
        si = jnp.where(keep, pltpu.roll(xi, sh, axis=0), 0.0)
        ar, ai = arow[lvl:lvl + 1, :], arow[half + lvl:half + lvl + 1, :]
        xr, xi = xr + ar * sr - ai * si, xi + ar * si + ai * sr
    pw_r, pw_i = arow[n_levels:n_levels + blk, :], arow[half + n_levels:half + n_levels + blk, :]
    out_r, out_i = [xr[:blk, :]], [xi[:blk, :]]
    for r0 in range(blk, cols_per_seq, blk):
        cr, ci = out_r[-1][blk - 1:blk, :], out_i[-1][blk - 1:blk, :]
        out_r.append(xr[r0:r0 + blk, :] + pw_r * cr - pw_i * ci)
        out_i.append(xi[r0:r0 + blk, :] + pw_r * ci + pw_i * cr)
    xr, xi = jnp.concatenate(out_r, axis=0), jnp.concatenate(out_i, axis=0)
    keep = pos >= 1
    pr = jnp.where(keep, pltpu.roll(xr, 1, axis=0), 0.0)
    pi = jnp.where(keep, pltpu.roll(xi, 1, axis=0), 0.0)
    return jnp.concatenate([to_cols(pr), to_cols(pi)], axis=0).astype(BF16)


def _s5_body(u_ref, wt_ref, nt_ref, mt_ref, arow_ref, o_ref, *, cols_per_seq, n_levels):
    t, _, n = u_ref.shape
    cg = S5_GROUP_CH
    groups = range(S5_GPB)
    ch = [slice(gi * cg, (gi + 1) * cg) for gi in groups]
    us = [u_ref[:, ch[gi], :].reshape(t * cg, n) for gi in groups]
    xs = [_dot(nt_ref[gi], us[gi]) for gi in groups]
    ys = [_dot(wt_ref[gi], us[gi]) for gi in groups]
    prevs = [_s5_entering_state(xs[gi], arow_ref[gi], cols_per_seq=cols_per_seq, n_levels=n_levels)
             for gi in groups]
    for gi in groups:
        y = ys[gi] + _dot(mt_ref[gi], prevs[gi])
        o_ref[:, ch[gi], :] = y.astype(o_ref.dtype).reshape(t, cg, n)


def s5_apply(proj_t, wt_g, nt_g, mt_g, arow, *, groups, cols_per_seq, n_levels):
    t, _, n = proj_t.shape
    cg = S5_GROUP_CH * S5_GPB
    spec3 = lambda shp: pl.BlockSpec((S5_GPB,) + shp, lambda i: (i, 0, 0))
    return pl.pallas_call(
        functools.partial(_s5_body, cols_per_seq=cols_per_seq, n_levels=n_levels),
        grid=(groups // S5_GPB,),
        in_specs=[pl.BlockSpec((t, cg, n), lambda i: (0, i, 0)),
                  spec3(wt_g.shape[1:]), spec3(nt_g.shape[1:]), spec3(mt_g.shape[1:]), spec3(arow.shape[1:])],
        out_specs=pl.BlockSpec((t, cg, n), lambda i: (0, i, 0)),
        out_shape=jax.ShapeDtypeStruct((t, groups * S5_GROUP_CH, n), BF16),
        compiler_params=_cparams(("parallel",)),
        name="s5",
    )(proj_t, wt_g, nt_g, mt_g, arow)


def s5_operators(lam_re, lam_im, log_step, b_re, b_im, c_re, c_im, d, *, n_levels, n_seq):
    hp = lax.Precision.HIGH
    g, p = lam_re.shape
    cg, t = S5_GROUP_CH, S5_T
    lr = jnp.minimum(lam_re.astype(F32), -1e-4)
    li = lam_im.astype(F32)
    step = jnp.exp(log_step.astype(F32))[:, None]
    mag = jnp.exp(lr * step)
    lb_re, lb_im = mag * jnp.cos(li * step), mag * jnp.sin(li * step)
    den = lr * lr + li * li
    nr, ni = lb_re - 1.0, lb_im
    coef_re = (nr * lr + ni * li) / den
    coef_im = (ni * lr - nr * li) / den
    bb_re = coef_re[..., None] * b_re - coef_im[..., None] * b_im
    bb_im = coef_re[..., None] * b_im + coef_im[..., None] * b_re

    def power(m):
        mm = m.astype(F32)[None, :, None]
        mg = jnp.exp(lr[:, None, :] * step[:, None, :] * mm)
        ang = li[:, None, :] * step[:, None, :] * mm
        return mg * jnp.cos(ang), mg * jnp.sin(ang)

    pw_re, pw_im = power(jnp.arange(t + 1))
    pwt_re, pwt_im = pw_re.transpose(0, 2, 1), pw_im.transpose(0, 2, 1)

    def pow_times_b(pr, pi):
        re = pr[:, :, :, None] * bb_re[:, :, None, :] - pi[:, :, :, None] * bb_im[:, :, None, :]
        im = pr[:, :, :, None] * bb_im[:, :, None, :] + pi[:, :, :, None] * bb_re[:, :, None, :]
        return re.reshape(g, p, t * cg), im.reshape(g, p, t * cg)

    pb_re, pb_im = pow_times_b(pwt_re[:, :, :t], pwt_im[:, :, :t])
    taps = (jnp.einsum('gcp,gpy->gcy', c_re.astype(F32), pb_re, precision=hp)
            - jnp.einsum('gcp,gpy->gcy', c_im.astype(F32), pb_im, precision=hp))
    y_idx = jnp.arange(t * cg)
    taps = taps + d.astype(F32).reshape(g, cg, 1) * (y_idx[None, :] == jnp.arange(cg)[:, None]).astype(F32)[None]
    place = ((y_idx[None, :, None] // cg == jnp.arange(t)[:, None, None] - y_idx[None, None, :] // cg)
             & (y_idx[None, :, None] % cg == y_idx[None, None, :] % cg)).astype(BF16)
    wt_g = jnp.einsum('gcy,tyz->gtcz', taps.astype(BF16), place,
                      preferred_element_type=F32).astype(BF16).reshape(g, t * cg, t * cg)
    n_re, n_im = pow_times_b(pwt_re[:, :, t - 1::-1], pwt_im[:, :, t - 1::-1])
    nt_g = jnp.concatenate([n_re, n_im], axis=1).astype(BF16)
    cp_re = c_re[:, None] * pw_re[:, 1:, None, :] - c_im[:, None] * pw_im[:, 1:, None, :]
    cp_im = c_re[:, None] * pw_im[:, 1:, None, :] + c_im[:, None] * pw_re[:, 1:, None, :]
    mt_g = jnp.concatenate([cp_re, -cp_im], axis=-1).reshape(g, t * cg, 2 * p).astype(BF16)
    ar, ai = power(t * jnp.concatenate([2 ** jnp.arange(n_levels), jnp.arange(1, S5_SCAN_BLK + 1)]))
    arow = jnp.concatenate([jnp.tile(ar, (1, 1, n_seq)), jnp.tile(ai, (1, 1, n_seq))], axis=1)
    return wt_g, nt_g, mt_g, arow


def _gelu_tanh(y):
    return 0.5 * y * (1.0 + jnp.tanh(math.sqrt(2.0 / math.pi) * (y + 0.044715 * (y * y * y))))


GLU_PARTS = 2


def _glu_body(y_ref, w_ref, b_ref, gate_ref, o_ref, z_ref, zb_ref, wt_ref):
    @pl.when((pl.program_id(0) == 0) & (pl.program_id(1) == 0))
    def _():
        _transpose_into(wt_ref, w_ref)

    tm = y_ref.shape[1]
    parts = [slice(c0, c0 + tm // GLU_PARTS) for c0 in range(0, tm, tm // GLU_PARTS)]
    for cols in parts:
        z = _gelu_tanh(y_ref[:, cols].astype(F32))
        z_ref[:, cols] = z
        zb_ref[:, cols] = z.astype(BF16)
    for cols in parts:
        for f0 in range(0, o_ref.shape[1], MM_SUB):
            rows = slice(f0, f0 + MM_SUB)
            lin = _dot(wt_ref[rows, :], zb_ref[:, cols]) + b_ref[rows, :]
            out = z_ref[rows, cols] * _sigmoid(lin) * _silu(gate_ref[rows, cols].astype(F32))
            o_ref[cols, rows] = out.T.astype(o_ref.dtype)


def glu_gate(y_t, w, b, proj_t, *, gate_row, tm=512):
    s, e, n = y_t.shape
    gb = gate_row // e
    return pl.pallas_call(
        _glu_body,
        grid=(s, n // tm),
        in_specs=[pl.BlockSpec((None, e, tm), lambda t, i: (t, 0, i)),
                  pl.BlockSpec((e, e), lambda t, i: (0, 0), pipeline_mode=pl.Buffered(1)),
                  pl.BlockSpec((e, 1), lambda t, i: (0, 0), pipeline_mode=pl.Buffered(1)),
                  pl.BlockSpec((None, e, tm), lambda t, i: (t, gb, i))],
        out_specs=pl.BlockSpec((tm, e), lambda t, i: (i, t)),
        out_shape=jax.ShapeDtypeStruct((n, s * e), BF16),
        scratch_shapes=[pltpu.VMEM((e, tm), F32), pltpu.VMEM((e, tm), BF16), pltpu.VMEM((e, e), BF16)],
        compiler_params=_cparams(("arbitrary", "arbitrary")),
        name="glu",
    )(y_t, w, b.reshape(e, 1), proj_t)


def kernel(x, ab_norm_g, ab_w_in, gla_alpha_up, gla_alpha_b, gla_head_g, fox_f_b, ab_w_out, c_norm_g, c_w_in, s5_lambda_re, s5_lambda_im, s5_log_step, s5_b_re, s5_b_im, s5_c_re, s5_c_im, s5_d, glu_w, glu_b, c_w_out, final_norm_g):
    batch, seq, d = x.shape
    t = batch * seq
    x2 = x.reshape(t, d)

    hk, hv, fw = GLA_HEADS * GLA_DK, GLA_HEADS * GLA_DV, FOX_HEADS * FOX_DH
    sizes = (hk, hk, hv, GLA_LOWRANK, hv, fw, fw, fw, FOX_HEADS, fw)
    offs = [0]
    for s in sizes:
        offs.append(offs[-1] + s)
    w_gla, w_fox, w_small = ab_weight_prep(jnp.swapaxes(ab_w_in, 1, 2), offs)
    c_gq, c_gk, c_gv, c_gg = 0, hk, 2 * hk, 2 * hk + hv
    c_fq, c_fk, c_fv, c_fg = 0, fw, 2 * fw, 3 * fw

    h0, proj_g, small = norm_proj(x2, ab_norm_g[0], w_gla, w_small, name="ab_in")
    proj_f = matmul(h0, w_fox, name="ab_in_fox")

    o_gla = gla(proj_g, small, gla_alpha_up[0], gla_alpha_b[0], gla_head_g[0],
                batch=batch, seq=seq, q_col=c_gq, k_col=c_gk, v_col=c_gv, gate_col=c_gg)

    fb_row = jnp.zeros((1, 128), F32).at[0, GLA_LOWRANK:GLA_LOWRANK + FOX_HEADS].set(fox_f_b[0])
    qa, ka = fox_gate(small, fb_row, batch=batch, seq=seq, col0=GLA_LOWRANK)
    o_fox = fox(proj_f, qa, ka, batch=batch, seq=seq, q_col=c_fq, k_col=c_fk, v_col=c_fv, gate_col=c_fg)

    x1, h1g = outproj_mid(o_gla, o_fox, ab_w_out[0], x2, c_norm_g[0], steps=S5_T, name="ab_out")

    e = d
    groups = e // S5_GROUP_CH
    n_chunks = t // S5_T
    cols_per_seq = seq // S5_T
    n_levels = S5_SCAN_BLK.bit_length() - 1
    proj_t = matmul_t_grouped(c_w_in[0], h1g, S5_T, BF16, name="c_in")
    ops = s5_operators(s5_lambda_re[0], s5_lambda_im[0], s5_log_step[0], s5_b_re[0], s5_b_im[0],
                       s5_c_re[0], s5_c_im[0], s5_d[0], n_levels=n_levels, n_seq=batch)
    y_t = s5_apply(proj_t, *ops, groups=groups, cols_per_seq=cols_per_seq, n_levels=n_levels)
    zz = glu_gate(y_t, glu_w[0], glu_b[0], proj_t, gate_row=e)
    out = outproj_final(zz, c_w_out[0], x1, final_norm_g, steps=S5_T, name="c_out")
    return out.reshape(batch, seq, d)
```
